```python
import math, functools
import jax, jax.numpy as jnp
from jax import lax
import numpy as np

D_MODEL = 1024
BATCH = 8
SEQ = 2048
DEPTH = 2
DEC_BATCH = 128
DEC_SEQ = 4
PAST_LEN = 16384
PAGE_SIZE = 128

POOL_WIDTH = D_MODEL // 2
POOL_WINDOWS = (2, 4, 8, 16)
N_POOL_GROUPS = len(POOL_WINDOWS)
POOL_GROUP = POOL_WIDTH // N_POOL_GROUPS
POOL_PAD = max(POOL_WINDOWS) - 1
N_HEADS = 8
N_KV_HEADS = 2
HEAD_DIM = 64
GQA_GROUP = N_HEADS // N_KV_HEADS
ATTN_WIDTH = N_HEADS * HEAD_DIM
WINDOW = 128
ATTN_BLOCK = 128
D_FF = 4 * D_MODEL
RMS_EPS = 1e-6
NEG_INF = -1e30
IN_SIZES = (POOL_WIDTH, ATTN_WIDTH, N_KV_HEADS * HEAD_DIM, N_KV_HEADS * HEAD_DIM, D_MODEL, D_MODEL)
IN_SPLITS = tuple(int(s) for s in np.cumsum(IN_SIZES)[:-1])
IN_WIDTH = int(sum(IN_SIZES))

kernel_name = "hybrid_pool_swa_gated_decoder_step"


def rms_norm(x, g):
    xf = x.astype(jnp.float32)
    y = xf * lax.rsqrt(jnp.mean(xf * xf, axis=-1, keepdims=True) + RMS_EPS)
    return (y * g.astype(jnp.float32)).astype(x.dtype)


def alibi_slopes():
    s = 2.0 ** (-8.0 * (np.arange(N_HEADS) + 1) / N_HEADS)
    return jnp.asarray(s, dtype=jnp.float32).reshape(N_KV_HEADS, GQA_GROUP)


def attend(q, k, v, dist, valid, sinks):
    scale = HEAD_DIM ** -0.5
    s = jnp.einsum('...qhgd,...khd->...hgqk', q, k, preferred_element_type=jnp.float32) * scale
    d = dist[..., None, None, :, :].astype(jnp.float32)
    s = s - alibi_slopes()[:, :, None, None] * d
    s = jnp.where(valid[..., None, None, :, :], s, NEG_INF)
    sink = sinks.astype(jnp.float32).reshape(N_KV_HEADS, GQA_GROUP)[:, :, None, None]
    m = jnp.maximum(jnp.max(s, axis=-1, keepdims=True), sink)
    p = jnp.exp(s - m)
    den = jnp.sum(p, axis=-1, keepdims=True) + jnp.exp(sink - m)
    return jnp.einsum('...hgqk,...khd->...qhgd', (p / den).astype(v.dtype), v)


def swa_prompt(q, k, v, sinks):
    B, T = q.shape[0], q.shape[1]
    nb = T // ATTN_BLOCK
    qb = q.reshape(B, nb, ATTN_BLOCK, N_KV_HEADS, GQA_GROUP, HEAD_DIM)
    kb = k.reshape(B, nb, ATTN_BLOCK, N_KV_HEADS, HEAD_DIM)
    vb = v.reshape(B, nb, ATTN_BLOCK, N_KV_HEADS, HEAD_DIM)
    kc = jnp.concatenate([jnp.concatenate([jnp.zeros_like(kb[:, :1]), kb[:, :-1]], axis=1), kb], axis=2)
    vc = jnp.concatenate([jnp.concatenate([jnp.zeros_like(vb[:, :1]), vb[:, :-1]], axis=1), vb], axis=2)
    blk = jnp.arange(nb)[:, None] * ATTN_BLOCK
    pos_q = blk + jnp.arange(ATTN_BLOCK)[None, :]
    pos_k = blk - ATTN_BLOCK + jnp.arange(2 * ATTN_BLOCK)[None, :]
    dist = pos_q[:, :, None] - pos_k[:, None, :]
    valid = (dist >= 0) & (dist < WINDOW) & (pos_k[:, None, :] >= 0)
    o = attend(qb, kc, vc, dist, valid, sinks)
    o = o.reshape(B, T, N_KV_HEADS, GQA_GROUP, HEAD_DIM)
    return o, k[:, -WINDOW:], v[:, -WINDOW:]


def swa_sample(q, k, v, sinks, cache_k, cache_v):
    Tn = q.shape[1]
    kc = jnp.concatenate([cache_k.astype(k.dtype), k], axis=1)
    vc = jnp.concatenate([cache_v.astype(v.dtype), v], axis=1)
    pos_q = jnp.arange(Tn)
    pos_k = jnp.arange(WINDOW + Tn) - WINDOW
    dist = pos_q[:, None] - pos_k[None, :]
    valid = (dist >= 0) & (dist < WINDOW)
    o = attend(q, kc, vc, dist, valid, sinks)
    return o, kc[:, -WINDOW:], vc[:, -WINDOW:]


def pool_mix(u, prefix, pos, w_pool, pool_scale):
    B, T, P = u.shape
    up = jnp.concatenate([prefix.astype(u.dtype), u], axis=1)
    upf = up.astype(jnp.float32)
    csum = jnp.concatenate([jnp.zeros((B, 1, P), jnp.float32), jnp.cumsum(upf, axis=1)], axis=1)
    uf = upf[:, POOL_PAD:]
    outs = []
    for g, w in enumerate(POOL_WINDOWS):
        lo, hi = g * POOL_GROUP, (g + 1) * POOL_GROUP
        wsum = csum[:, POOL_PAD + 1:POOL_PAD + 1 + T, lo:hi] - csum[:, POOL_PAD + 1 - w:POOL_PAD + 1 - w + T, lo:hi]
        cnt = jnp.minimum(w, pos + 1).astype(jnp.float32)[None, :, None]
        outs.append(wsum / cnt - uf[:, :, lo:hi])
    d = jnp.stack(outs, axis=2).astype(u.dtype)
    y = jnp.einsum('btgc,gcd->btgd', d, w_pool).reshape(B, T, P) * pool_scale
    return y, up[:, -POOL_PAD:]


def block(x, c, w_ada, b_ada, g1, w_in, w_pool, pool_scale, sinks, w_a, w_b, w_out,
          g2, w_up, w_down, pool_prefix, pos, attn_fn):
    B, T, _ = x.shape
    mod = (jax.nn.silu(c) @ w_ada + b_ada)[:, None, :]
    sh1, sc1, gt1, sh2, sc2, gt2 = jnp.split(mod, 6, axis=-1)
    h = rms_norm(x, g1) * (1 + sc1) + sh1
    z = h @ w_in
    u, q, k, v, ga, gb = jnp.split(z, IN_SPLITS, axis=-1)
    a, new_pool = pool_mix(u, pool_prefix, pos, w_pool, pool_scale)
    q = q.reshape(B, T, N_KV_HEADS, GQA_GROUP, HEAD_DIM)
    k = k.reshape(B, T, N_KV_HEADS, HEAD_DIM)
    v = v.reshape(B, T, N_KV_HEADS, HEAD_DIM)
    o, new_k, new_v = attn_fn(q, k, v, sinks)
    br_a = a @ w_a
    br_b = o.reshape(B, T, ATTN_WIDTH) @ w_b
    merged = jax.nn.sigmoid(ga) * br_a + jax.nn.sigmoid(gb) * br_b
    x = x + gt1 * (merged @ w_out)
    h2 = rms_norm(x, g2) * (1 + sc2) + sh2
    x = x + gt2 * (jnp.square(jax.nn.relu(h2 @ w_up)) @ w_down)
    return x, new_k, new_v, new_pool


def setup_inputs(seed: int = 0) -> dict:
    key = jax.random.key(seed)
    ks = jax.random.split(key, 24)
    f32 = jnp.float32
    nrm = lambda k, shape, s: jax.random.normal(k, shape, f32) * s
    return {
        "x_prompt": nrm(ks[0], (BATCH, SEQ, D_MODEL), 1.0),
        "x_sample": nrm(ks[1], (DEC_BATCH, DEC_SEQ, D_MODEL), 1.0),
        "cache_k": nrm(ks[2], (DEPTH, DEC_BATCH, WINDOW, N_KV_HEADS, HEAD_DIM), 1.0),
        "cache_v": nrm(ks[3], (DEPTH, DEC_BATCH, WINDOW, N_KV_HEADS, HEAD_DIM), 1.0),
        "state_pool": nrm(ks[4], (DEPTH, DEC_BATCH, POOL_PAD, POOL_WIDTH), 1.0),
        "c_prompt": nrm(ks[5], (BATCH, D_MODEL), 1.0),
        "c_sample": nrm(ks[6], (DEC_BATCH, D_MODEL), 1.0),
        "w_ada": nrm(ks[7], (DEPTH, D_MODEL, 6 * D_MODEL), 0.5 * D_MODEL ** -0.5),
        "b_ada": nrm(ks[8], (DEPTH, 6 * D_MODEL), 0.01),
        "norm1_g": 1.0 + nrm(ks[9], (DEPTH, D_MODEL), 0.05),
        "w_in": nrm(ks[10], (DEPTH, D_MODEL, IN_WIDTH), D_MODEL ** -0.5),
        "w_pool": nrm(ks[11], (DEPTH, N_POOL_GROUPS, POOL_GROUP, POOL_GROUP), POOL_GROUP ** -0.5),
        "pool_scale": 1.0 + nrm(ks[12], (DEPTH, POOL_WIDTH), 0.1),
        "attn_sinks": nrm(ks[13], (DEPTH, N_HEADS), 1.0),
        "w_a": nrm(ks[14], (DEPTH, POOL_WIDTH, D_MODEL), POOL_WIDTH ** -0.5),
        "w_b": nrm(ks[15], (DEPTH, ATTN_WIDTH, D_MODEL), ATTN_WIDTH ** -0.5),
        "w_out": nrm(ks[16], (DEPTH, D_MODEL, D_MODEL), D_MODEL ** -0.5),
        "norm2_g": 1.0 + nrm(ks[17], (DEPTH, D_MODEL), 0.05),
        "w_up": nrm(ks[18], (DEPTH, D_MODEL, D_FF), D_MODEL ** -0.5),
        "w_down": nrm(ks[19], (DEPTH, D_FF, D_MODEL), D_FF ** -0.5),
        "final_g": 1.0 + nrm(ks[20], (D_MODEL,), 0.05),
    }


def reference(x_prompt, x_sample, cache_k, cache_v, state_pool, c_prompt, c_sample,
              w_ada, b_ada, norm1_g, w_in, w_pool, pool_scale, attn_sinks, w_a, w_b,
              w_out, norm2_g, w_up, w_down, final_g):
    T, Tn = x_prompt.shape[1], x_sample.shape[1]
    pos_p = jnp.arange(T)
    pos_s = PAST_LEN + jnp.arange(Tn)
    xp, xs = x_prompt, x_sample
    kp_l, vp_l, pp_l, ks_l, vs_l, ps_l = [], [], [], [], [], []
    for l in range(DEPTH):
        lw = (w_ada[l], b_ada[l], norm1_g[l], w_in[l], w_pool[l], pool_scale[l], attn_sinks[l],
              w_a[l], w_b[l], w_out[l], norm2_g[l], w_up[l], w_down[l])
        prefix_p = jnp.zeros((xp.shape[0], POOL_PAD, POOL_WIDTH), xp.dtype)
        xp, kp, vp, pp = block(xp, c_prompt, *lw, prefix_p, pos_p, swa_prompt)
        attn_s = functools.partial(swa_sample, cache_k=cache_k[l], cache_v=cache_v[l])
        xs, k_s, v_s, p_s = block(xs, c_sample, *lw, state_pool[l], pos_s, attn_s)
        kp_l.append(kp); vp_l.append(vp); pp_l.append(pp)
        ks_l.append(k_s); vs_l.append(v_s); ps_l.append(p_s)
    y_prompt = rms_norm(xp, final_g)
    y_sample = rms_norm(xs, final_g)
    new_k_prompt = jnp.stack(kp_l, axis=0)
    new_v_prompt = jnp.stack(vp_l, axis=0)
    new_pool_prompt = jnp.stack(pp_l, axis=0)
    new_k_sample = jnp.stack(ks_l, axis=0)
    new_v_sample = jnp.stack(vs_l, axis=0)
    new_pool_sample = jnp.stack(ps_l, axis=0)
    return (y_prompt, y_sample, new_k_prompt, new_v_prompt, new_pool_prompt, new_k_sample, new_v_sample, new_pool_sample)
```

```python
import functools

import numpy as np
import jax
import jax.numpy as jnp
from jax import lax
from jax.experimental import pallas as pl
from jax.experimental.pallas import tpu as pltpu

D_MODEL = 1024
DEPTH = 2
PAST_LEN = 16384
POOL_WIDTH = D_MODEL // 2
POOL_WINDOWS = (2, 4, 8, 16)
POOL_GROUP = POOL_WIDTH // len(POOL_WINDOWS)
POOL_PAD = max(POOL_WINDOWS) - 1
N_HEADS = 8
N_KV_HEADS = 2
HEAD_DIM = 64
GQA_GROUP = N_HEADS // N_KV_HEADS
ATTN_WIDTH = N_HEADS * HEAD_DIM
KV_WIDTH = N_KV_HEADS * HEAD_DIM
WINDOW = 128
ATTN_BLOCK = 128
D_FF = 4 * D_MODEL
RMS_EPS = 1e-6
NEG_INF = -1e30

OFF_U = 0
OFF_Q = OFF_U + POOL_WIDTH
OFF_K = OFF_Q + ATTN_WIDTH
OFF_V = OFF_K + KV_WIDTH
OFF_GA = OFF_V + KV_WIDTH
OFF_GB = OFF_GA + D_MODEL
IN_WIDTH = OFF_GB + D_MODEL

LANES = 128
HIST = 16
VMEM_LIMIT = 56 * 1024 * 1024

F32 = jnp.float32
BF16 = jnp.bfloat16


def _bdot(a, b):
    return jnp.dot(a.astype(BF16), b.astype(BF16), preferred_element_type=F32)


def _rms_mod(x, g, sc, sh):
    ms = jnp.mean(x * x, axis=-1, keepdims=True)
    return (x * lax.rsqrt(ms + RMS_EPS) * g) * (1.0 + sc) + sh


def _alibi_slopes():
    return 2.0 ** (-8.0 * (np.arange(N_HEADS) + 1) / N_HEADS)


def _mod_kernel(c_ref, w_ref, b_ref, o_ref):
    c = c_ref[...]
    s = c * jax.nn.sigmoid(c)
    o_ref[...] = _bdot(s, w_ref[...]) + b_ref[...]


def _mod_call(c_all, w_ada, b_ada):
    m = c_all.shape[0]
    tn = 1536
    n = 6 * D_MODEL
    return pl.pallas_call(
        _mod_kernel,
        grid=(DEPTH, n // tn),
        in_specs=[
            pl.BlockSpec((m, D_MODEL), lambda l, j: (0, 0)),
            pl.BlockSpec((None, D_MODEL, tn), lambda l, j: (l, 0, j)),
            pl.BlockSpec((None, 1, tn), lambda l, j: (l, 0, j)),
        ],
        out_specs=pl.BlockSpec((None, m, tn), lambda l, j: (l, 0, j)),
        out_shape=jax.ShapeDtypeStruct((DEPTH, m, n), F32),
        compiler_params=pltpu.CompilerParams(
            dimension_semantics=("arbitrary", "arbitrary"),
            vmem_limit_bytes=VMEM_LIMIT),
        name="adaln_mod",
    )(c_all, w_ada, b_ada.reshape(DEPTH, 1, n))


def _prompt_bias():
    i = np.arange(ATTN_BLOCK)[:, None]
    j = np.arange(2 * ATTN_BLOCK)[None, :]
    dist = i + ATTN_BLOCK - j
    valid = (dist >= 0) & (dist < WINDOW)
    valid_first = valid & (j >= ATTN_BLOCK)
    sl = _alibi_slopes()[:, None, None]
    b = np.where(valid[None], -sl * dist[None], NEG_INF)
    b0 = np.where(valid_first[None], -sl * dist[None], NEG_INF)
    return np.stack([b, b0]).astype(np.float32)


def _pool_windows(u, ubuf, tm, pos):
    outs = []
    for g, w in enumerate(POOL_WINDOWS):
        lo = g * POOL_GROUP
        ug = u[:, lo:lo + POOL_GROUP]
        acc = ug
        for s in range(1, w):
            acc = acc + ubuf[HIST - s:HIST - s + tm, lo:lo + POOL_GROUP]
        cnt = jnp.minimum(w, pos + 1).astype(F32)
        outs.append(acc / cnt - ug)
    return outs


def _pmix_kernel(sinks_ref, x_ref, mod_ref, g1_ref, win_ref, wpool_ref, pscale_ref,
                 wa_ref, wb_ref, wout_ref, bias_ref,
                 xo_ref, nk_ref, nv_ref, npool_ref,
                 ubuf, ka, kb, va, vb, obuf, *, tm):
    t = pl.program_id(1)
    nt = pl.num_programs(1)

    @pl.when(t == 0)
    def _init():
        ubuf[0:HIST, :] = jnp.zeros((HIST, POOL_WIDTH), F32)
        zero = jnp.zeros((N_KV_HEADS, ATTN_BLOCK, LANES), BF16)
        ka[:, 0:ATTN_BLOCK, :] = zero
        kb[:, 0:ATTN_BLOCK, :] = zero
        va[:, 0:ATTN_BLOCK, :] = zero
        vb[:, 0:ATTN_BLOCK, :] = zero

    x = x_ref[...]
    mod = mod_ref[...]
    sh1 = mod[:, 0:D_MODEL]
    sc1 = mod[:, D_MODEL:2 * D_MODEL]
    gt1 = mod[:, 2 * D_MODEL:3 * D_MODEL]
    h = _rms_mod(x, g1_ref[...], sc1, sh1).astype(BF16)

    u = jnp.dot(h, win_ref[:, OFF_U:OFF_Q], preferred_element_type=F32)
    ubuf[HIST:HIST + tm, :] = u
    pos = t * tm + lax.broadcasted_iota(jnp.int32, (tm, 1), 0)
    d = _pool_windows(u, ubuf, tm, pos)
    y = jnp.concatenate(
        [_bdot(d[g], wpool_ref[g]) for g in range(len(POOL_WINDOWS))], axis=1)
    y = y * pscale_ref[...]
    br_a = _bdot(y, wa_ref[...])

    q = jnp.dot(h, win_ref[:, OFF_Q:OFF_K], preferred_element_type=F32)
    qb = (q * (HEAD_DIM ** -0.5)).astype(BF16)
    kv = jnp.dot(h, win_ref[:, OFF_K:OFF_GA], preferred_element_type=F32)
    k = kv[:, 0:KV_WIDTH]
    v = kv[:, KV_WIDTH:2 * KV_WIDTH]
    lane = lax.broadcasted_iota(jnp.int32, (tm, LANES), 1)
    low = lane < HEAD_DIM
    kr = pltpu.roll(k, HEAD_DIM, axis=1)
    vr = pltpu.roll(v, HEAD_DIM, axis=1)
    ka[0, ATTN_BLOCK:ATTN_BLOCK + tm, :] = jnp.where(low, k, 0.0).astype(BF16)
    kb[0, ATTN_BLOCK:ATTN_BLOCK + tm, :] = jnp.where(low, 0.0, kr).astype(BF16)
    ka[1, ATTN_BLOCK:ATTN_BLOCK + tm, :] = jnp.where(low, kr, 0.0).astype(BF16)
    kb[1, ATTN_BLOCK:ATTN_BLOCK + tm, :] = jnp.where(low, 0.0, k).astype(BF16)
    va[0, ATTN_BLOCK:ATTN_BLOCK + tm, :] = jnp.where(low, v, 0.0).astype(BF16)
    vb[0, ATTN_BLOCK:ATTN_BLOCK + tm, :] = jnp.where(low, 0.0, vr).astype(BF16)
    va[1, ATTN_BLOCK:ATTN_BLOCK + tm, :] = jnp.where(low, vr, 0.0).astype(BF16)
    vb[1, ATTN_BLOCK:ATTN_BLOCK + tm, :] = jnp.where(low, 0.0, v).astype(BF16)

    low_q = lax.broadcasted_iota(jnp.int32, (ATTN_BLOCK, LANES), 1) < HEAD_DIM
    nt_dims = (((1,), (1,)), ((), ()))
    for j in range(tm // ATTN_BLOCK):
        r0 = j * ATTN_BLOCK
        for hk in range(N_KV_HEADS):
            for pr in range(GQA_GROUP // 2):
                c0 = (hk * (GQA_GROUP // 2) + pr) * LANES
                q2 = qb[r0:r0 + ATTN_BLOCK, c0:c0 + LANES]
                ks = (ka[hk, r0:r0 + 2 * ATTN_BLOCK, :], kb[hk, r0:r0 + 2 * ATTN_BLOCK, :])
                vs = (va[hk, r0:r0 + 2 * ATTN_BLOCK, :], vb[hk, r0:r0 + 2 * ATTN_BLOCK, :])
                o2 = None
                dens = []
                for e in range(2):
                    head = hk * GQA_GROUP + pr * 2 + e
                    s = lax.dot_general(q2, ks[e], nt_dims, preferred_element_type=F32)
                    if j == 0:
                        bias = jnp.where(t == 0, bias_ref[1, head], bias_ref[0, head])
                    else:
                        bias = bias_ref[0, head]
                    s = s + bias
                    sink = sinks_ref[head]
                    m = jnp.maximum(jnp.max(s, axis=-1, keepdims=True), sink)
                    p = jnp.exp(s - m)
                    dens.append(jnp.sum(p, axis=-1, keepdims=True) + jnp.exp(sink - m))
                    pv = jnp.dot(p.astype(BF16), vs[e], preferred_element_type=F32)
                    o2 = pv if o2 is None else o2 + pv
                den = jnp.where(low_q, dens[0], dens[1])
                obuf[r0:r0 + ATTN_BLOCK, c0:c0 + LANES] = o2 / den

    br_b = _bdot(obuf[...], wb_ref[...])
    ga = jnp.dot(h, win_ref[:, OFF_GA:OFF_GB], preferred_element_type=F32)
    merged = jax.nn.sigmoid(ga) * br_a
    gb = jnp.dot(h, win_ref[:, OFF_GB:IN_WIDTH], preferred_element_type=F32)
    merged = merged + jax.nn.sigmoid(gb) * br_b
    xo_ref[...] = x + gt1 * _bdot(merged, wout_ref[...])

    @pl.when(t == nt - 1)
    def _state():
        nk_ref[...] = k[tm - WINDOW:, :]
        nv_ref[...] = v[tm - WINDOW:, :]
        npool_ref[...] = ubuf[HIST + tm - POOL_PAD:HIST + tm, :]

    ubuf[0:HIST, :] = ubuf[tm:tm + HIST, :]
    for buf in (ka, kb, va, vb):
        buf[:, 0:ATTN_BLOCK, :] = buf[:, tm:tm + ATTN_BLOCK, :]


def _const_spec(shape):
    nd = len(shape)
    return pl.BlockSpec(shape, lambda *_: (0,) * nd)


def _pmix_call(x, mod_p, g1, win, wpool, pscale, wa, wb, wout, sinks, bias, tm):
    b, t, _ = x.shape
    kern = functools.partial(_pmix_kernel, tm=tm)
    grid_spec = pltpu.PrefetchScalarGridSpec(
        num_scalar_prefetch=1,
        grid=(b, t // tm),
        in_specs=[
            pl.BlockSpec((None, tm, D_MODEL), lambda i, j, s: (i, j, 0)),
            pl.BlockSpec((None, 1, 3 * D_MODEL), lambda i, j, s: (i, 0, 0)),
            _const_spec((1, D_MODEL)),
            _const_spec((D_MODEL, IN_WIDTH)),
            _const_spec((len(POOL_WINDOWS), POOL_GROUP, POOL_GROUP)),
            _const_spec((1, POOL_WIDTH)),
            _const_spec((POOL_WIDTH, D_MODEL)),
            _const_spec((ATTN_WIDTH, D_MODEL)),
            _const_spec((D_MODEL, D_MODEL)),
            _const_spec((2, N_HEADS, ATTN_BLOCK, 2 * ATTN_BLOCK)),
        ],
        out_specs=[
            pl.BlockSpec((None, tm, D_MODEL), lambda i, j, s: (i, j, 0)),
            pl.BlockSpec((None, WINDOW, KV_WIDTH), lambda i, j, s: (i, 0, 0)),
            pl.BlockSpec((None, WINDOW, KV_WIDTH), lambda i, j, s: (i, 0, 0)),
            pl.BlockSpec((None, POOL_PAD, POOL_WIDTH), lambda i, j, s: (i, 0, 0)),
        ],
        scratch_shapes=[
            pltpu.VMEM((HIST + tm, POOL_WIDTH), F32),
            pltpu.VMEM((N_KV_HEADS, ATTN_BLOCK + tm, LANES), BF16),
            pltpu.VMEM((N_KV_HEADS, ATTN_BLOCK + tm, LANES), BF16),
            pltpu.VMEM((N_KV_HEADS, ATTN_BLOCK + tm, LANES), BF16),
            pltpu.VMEM((N_KV_HEADS, ATTN_BLOCK + tm, LANES), BF16),
            pltpu.VMEM((tm, ATTN_WIDTH), F32),
        ],
    )
    return pl.pallas_call(
        kern,
        grid_spec=grid_spec,
        out_shape=[
            jax.ShapeDtypeStruct((b, t, D_MODEL), F32),
            jax.ShapeDtypeStruct((b, WINDOW, KV_WIDTH), F32),
            jax.ShapeDtypeStruct((b, WINDOW, KV_WIDTH), F32),
            jax.ShapeDtypeStruct((b, POOL_PAD, POOL_WIDTH), F32),
        ],
        compiler_params=pltpu.CompilerParams(
            dimension_semantics=("arbitrary", "arbitrary"),
            vmem_limit_bytes=VMEM_LIMIT),
        name="prompt_mixer",
    )(sinks, x, mod_p, g1, win, wpool, pscale, wa, wb, wout, bias)


def _ffn_kernel(x_ref, mod_ref, g2_ref, wup_ref, wdn_ref, fg_ref, o_ref, *, final):
    x = x_ref[...]
    mod = mod_ref[...]
    sh2 = mod[:, 0:D_MODEL]
    sc2 = mod[:, D_MODEL:2 * D_MODEL]
    gt2 = mod[:, 2 * D_MODEL:3 * D_MODEL]
    h2 = _rms_mod(x, g2_ref[...], sc2, sh2)
    ff = _bdot(h2, wup_ref[...])
    ff = jnp.square(jnp.maximum(ff, 0.0))
    y = x + gt2 * _bdot(ff, wdn_ref[...])
    if final:
        ms = jnp.mean(y * y, axis=-1, keepdims=True)
        y = y * lax.rsqrt(ms + RMS_EPS) * fg_ref[...]
    o_ref[...] = y


def _ffn_call(x, mod, g2, wup, wdn, fg, tm, final, per_row_mod):
    b, t, _ = x.shape
    if per_row_mod:
        mod_spec = pl.BlockSpec((None, tm, 3 * D_MODEL), lambda i, j: (i, j, 1))
    else:
        mod_spec = pl.BlockSpec((None, 1, 3 * D_MODEL), lambda i, j: (i, 0, 1))
    return pl.pallas_call(
        functools.partial(_ffn_kernel, final=final),
        grid=(b, t // tm),
        in_specs=[
            pl.BlockSpec((None, tm, D_MODEL), lambda i, j: (i, j, 0)),
            mod_spec,
            _const_spec((1, D_MODEL)),
            _const_spec((D_MODEL, D_FF)),
            _const_spec((D_FF, D_MODEL)),
            _const_spec((1, D_MODEL)),
        ],
        out_specs=pl.BlockSpec((None, tm, D_MODEL), lambda i, j: (i, j, 0)),
        out_shape=jax.ShapeDtypeStruct((b, t, D_MODEL), F32),
        compiler_params=pltpu.CompilerParams(
            dimension_semantics=("arbitrary", "arbitrary"),
            vmem_limit_bytes=VMEM_LIMIT),
        name="ffn_final" if final else "ffn",
    )(x, mod, g2, wup, wdn, fg)


def _sproj_kernel(x_ref, mod_ref, g1_ref, win_ref, wpool_ref, pscale_ref, wa_ref, pre_ref,
                  q_ref, k_ref, v_ref, ma_ref, sgb_ref, npool_ref, *, nb, tn):
    x = x_ref[...]
    mod = mod_ref[...]
    sh1 = mod[:, 0:D_MODEL]
    sc1 = mod[:, D_MODEL:2 * D_MODEL]
    h = _rms_mod(x, g1_ref[...], sc1, sh1).astype(BF16)

    u = jnp.dot(h, win_ref[:, OFF_U:OFF_Q], preferred_element_type=F32)
    def up(r):
        if r < POOL_PAD:
            return pre_ref[r]
        r -= POOL_PAD
        return u[r * nb:(r + 1) * nb, :]

    d_rows = []
    for tt in range(tn):
        parts = []
        for g, w in enumerate(POOL_WINDOWS):
            lo = g * POOL_GROUP
            acc = up(POOL_PAD + tt)[:, lo:lo + POOL_GROUP]
            for s in range(1, w):
                acc = acc + up(POOL_PAD + tt - s)[:, lo:lo + POOL_GROUP]
            cnt = float(min(w, PAST_LEN + tt + 1))
            parts.append(acc / cnt - up(POOL_PAD + tt)[:, lo:lo + POOL_GROUP])
        d_rows.append(parts)
    y = jnp.concatenate(
        [_bdot(jnp.concatenate([d_rows[tt][g] for tt in range(tn)], axis=0), wpool_ref[g])
         for g in range(len(POOL_WINDOWS))], axis=1)
    y = y * pscale_ref[...]
    br_a = _bdot(y, wa_ref[...])
    for r in range(POOL_PAD):
        npool_ref[r] = up(r + tn)

    q = jnp.dot(h, win_ref[:, OFF_Q:OFF_K], preferred_element_type=F32)
    q_ref[...] = q * (HEAD_DIM ** -0.5)
    kv = jnp.dot(h, win_ref[:, OFF_K:OFF_GA], preferred_element_type=F32)
    k_ref[...] = kv[:, 0:KV_WIDTH]
    v_ref[...] = kv[:, KV_WIDTH:2 * KV_WIDTH]
    ga = jnp.dot(h, win_ref[:, OFF_GA:OFF_GB], preferred_element_type=F32)
    ma_ref[...] = jax.nn.sigmoid(ga) * br_a
    gb = jnp.dot(h, win_ref[:, OFF_GB:IN_WIDTH], preferred_element_type=F32)
    sgb_ref[...] = jax.nn.sigmoid(gb)


def _sproj_call(x, mod_s, g1, win, wpool, pscale, wa, prefix_t, nb, tn):
    m = x.shape[0]
    shapes = [
        jax.ShapeDtypeStruct((m, ATTN_WIDTH), F32),
        jax.ShapeDtypeStruct((m, KV_WIDTH), F32),
        jax.ShapeDtypeStruct((m, KV_WIDTH), F32),
        jax.ShapeDtypeStruct((m, D_MODEL), F32),
        jax.ShapeDtypeStruct((m, D_MODEL), F32),
        jax.ShapeDtypeStruct((POOL_PAD, nb, POOL_WIDTH), F32),
    ]
    in_shapes = [x.shape, (m, 3 * D_MODEL), g1.shape, win.shape, wpool.shape, pscale.shape,
                 wa.shape, prefix_t.shape]
    return pl.pallas_call(
        functools.partial(_sproj_kernel, nb=nb, tn=tn),
        grid=(1,),
        in_specs=[_const_spec(s) for s in in_shapes],
        out_specs=[_const_spec(s.shape) for s in shapes],
        out_shape=shapes,
        compiler_params=pltpu.CompilerParams(
            dimension_semantics=("arbitrary",),
            vmem_limit_bytes=VMEM_LIMIT),
        name="sample_proj",
    )(x, mod_s, g1, win, wpool, pscale, wa, prefix_t)


def _sattn_kernel(sinks_ref, slopes_ref, q_ref, kn_ref, vn_ref, kc_ref, vc_ref, o_ref,
                  qt, knt, vnt, sbuf, ot, *, nb, tn):
    hk = pl.program_id(0)
    nq = tn * GQA_GROUP
    srows = WINDOW + 8
    hoff = pl.multiple_of(hk * HEAD_DIM, HEAD_DIM)
    for tt in range(tn):
        qt[tt] = q_ref[tt * nb:(tt + 1) * nb, :].T
        knt[tt] = kn_ref[tt * nb:(tt + 1) * nb, :].T
        vnt[tt] = vn_ref[tt * nb:(tt + 1) * nb, :].T

    jb = 16
    for c in range(WINDOW // jb):
        def qk_body(dd, accs):
            k2 = kc_ref[dd, c * jb:(c + 1) * jb, :]
            new = []
            for tt in range(tn):
                for g in range(GQA_GROUP):
                    qrow = qt[tt, pl.ds(g * HEAD_DIM + dd, 1), :]
                    new.append(accs[tt * GQA_GROUP + g] + qrow * k2)
            return tuple(new)
        accs = lax.fori_loop(0, HEAD_DIM, qk_body,
                             tuple(jnp.zeros((jb, nb), F32) for _ in range(nq)))
        for i in range(nq):
            sbuf[i, c * jb:(c + 1) * jb, :] = accs[i]

    for tt in range(tn):
        for g in range(GQA_GROUP):
            qh = qt[tt, g * HEAD_DIM:(g + 1) * HEAD_DIM, :]
            rows = [jnp.sum(qh * knt[t2, pl.ds(hoff, HEAD_DIM), :], axis=0, keepdims=True)
                    for t2 in range(tn)]
            rows.append(jnp.zeros((8 - tn, nb), F32))
            sbuf[tt * GQA_GROUP + g, WINDOW:srows, :] = jnp.concatenate(rows, axis=0)

    jpos = lax.broadcasted_iota(jnp.int32, (srows, nb), 0)
    for tt in range(tn):
        dist = tt + WINDOW - jpos
        valid = (dist >= 0) & (dist < WINDOW)
        distf = dist.astype(F32)
        for g in range(GQA_GROUP):
            i = tt * GQA_GROUP + g
            head = hk * GQA_GROUP + g
            s = sbuf[i] - slopes_ref[head] * distf
            s = jnp.where(valid, s, NEG_INF)
            sink = sinks_ref[head]
            m = jnp.maximum(jnp.max(s, axis=0, keepdims=True), sink)
            p = jnp.exp(s - m)
            den = jnp.sum(p, axis=0, keepdims=True) + jnp.exp(sink - m)
            sbuf[i] = p / den

    def pv_body(dd, carry):
        vd = vc_ref[dd]
        for tt in range(tn):
            for g in range(GQA_GROUP):
                i = tt * GQA_GROUP + g
                row = jnp.sum(sbuf[i, 0:WINDOW, :] * vd, axis=0, keepdims=True)
                ot[tt, pl.ds(g * HEAD_DIM + dd, 1), :] = row
        return carry
    lax.fori_loop(0, HEAD_DIM, pv_body, 0)

    for tt in range(tn):
        for g in range(GQA_GROUP):
            i = tt * GQA_GROUP + g
            acc = ot[tt, g * HEAD_DIM:(g + 1) * HEAD_DIM, :]
            for t2 in range(tt + 1):
                acc = acc + sbuf[i, WINDOW + t2:WINDOW + t2 + 1, :] * vnt[t2, pl.ds(hoff, HEAD_DIM), :]
            ot[tt, g * HEAD_DIM:(g + 1) * HEAD_DIM, :] = acc
        o_ref[tt * nb:(tt + 1) * nb, :] = ot[tt].T


def _sattn_call(q, kn, vn, kc_t, vc_t, sinks, slopes, nb, tn):
    m = q.shape[0]
    gw = GQA_GROUP * HEAD_DIM
    grid_spec = pltpu.PrefetchScalarGridSpec(
        num_scalar_prefetch=2,
        grid=(N_KV_HEADS,),
        in_specs=[
            pl.BlockSpec((m, gw), lambda h, a, b: (0, h)),
            pl.BlockSpec((m, KV_WIDTH), lambda h, a, b: (0, 0)),
            pl.BlockSpec((m, KV_WIDTH), lambda h, a, b: (0, 0)),
            pl.BlockSpec((HEAD_DIM, WINDOW, nb), lambda h, a, b: (h, 0, 0)),
            pl.BlockSpec((HEAD_DIM, WINDOW, nb), lambda h, a, b: (h, 0, 0)),
        ],
        out_specs=pl.BlockSpec((m, gw), lambda h, a, b: (0, h)),
        scratch_shapes=[
            pltpu.VMEM((tn, gw, nb), F32),
            pltpu.VMEM((tn, KV_WIDTH, nb), F32),
            pltpu.VMEM((tn, KV_WIDTH, nb), F32),
            pltpu.VMEM((tn * GQA_GROUP, WINDOW + 8, nb), F32),
            pltpu.VMEM((tn, gw, nb), F32),
        ],
    )
    return pl.pallas_call(
        functools.partial(_sattn_kernel, nb=nb, tn=tn),
        grid_spec=grid_spec,
        out_shape=jax.ShapeDtypeStruct((m, ATTN_WIDTH), F32),
        compiler_params=pltpu.CompilerParams(
            dimension_semantics=("arbitrary",),
            vmem_limit_bytes=VMEM_LIMIT),
        name="sample_attn",
    )(sinks, slopes, q, kn, vn, kc_t, vc_t)


def _spost_kernel(x_ref, o_ref, ma_ref, sgb_ref, mod_ref, wb_ref, wout_ref, xo_ref):
    gt1 = mod_ref[...]
    br_b = _bdot(o_ref[...], wb_ref[...])
    merged = ma_ref[...] + sgb_ref[...] * br_b
    xo_ref[...] = x_ref[...] + gt1 * _bdot(merged, wout_ref[...])


def _spost_call(x, o, ma, sgb, mod_s, wb, wout):
    ins = (x, o, ma, sgb, mod_s, wb, wout)
    specs = [_const_spec(a.shape) for a in ins]
    specs[4] = pl.BlockSpec(x.shape, lambda i: (0, 2))
    return pl.pallas_call(
        _spost_kernel,
        grid=(1,),
        in_specs=specs,
        out_specs=_const_spec(x.shape),
        out_shape=jax.ShapeDtypeStruct(x.shape, F32),
        compiler_params=pltpu.CompilerParams(
            dimension_semantics=("arbitrary",),
            vmem_limit_bytes=VMEM_LIMIT),
        name="sample_post",
    )(*ins)


def _roll_kernel(ck_ref, cv_ref, kn_ref, vn_ref, ok_ref, ov_ref, *, tn):
    keep = WINDOW - tn
    ok_ref[:, 0:keep, :] = ck_ref[:, tn:WINDOW, :]
    ok_ref[:, keep:WINDOW, :] = kn_ref[...]
    ov_ref[:, 0:keep, :] = cv_ref[:, tn:WINDOW, :]
    ov_ref[:, keep:WINDOW, :] = vn_ref[...]


def _roll_call(ck, cv, kn, vn, tn):
    nl, nb = ck.shape[0], ck.shape[1]
    bb = 16
    cspec = pl.BlockSpec((None, bb, WINDOW, KV_WIDTH), lambda l, i: (l, i, 0, 0))
    nspec = pl.BlockSpec((None, bb, tn, KV_WIDTH), lambda l, i: (l, i, 0, 0))
    shp = jax.ShapeDtypeStruct(ck.shape, F32)
    return pl.pallas_call(
        functools.partial(_roll_kernel, tn=tn),
        grid=(nl, nb // bb),
        in_specs=[cspec, cspec, nspec, nspec],
        out_specs=[cspec, cspec],
        out_shape=[shp, shp],
        compiler_params=pltpu.CompilerParams(
            dimension_semantics=("arbitrary", "arbitrary"),
            vmem_limit_bytes=VMEM_LIMIT),
        name="cache_roll",
    )(ck, cv, kn, vn)


def kernel(x_prompt, x_sample, cache_k, cache_v, state_pool, c_prompt, c_sample,
           w_ada, b_ada, norm1_g, w_in, w_pool, pool_scale, attn_sinks, w_a, w_b,
           w_out, norm2_g, w_up, w_down, final_g):
    bp, tp, _ = x_prompt.shape
    nb, tn, _ = x_sample.shape
    assert tp % ATTN_BLOCK == 0 and tn <= 8 and nb == LANES

    c_all = jnp.concatenate([c_prompt, jnp.tile(c_sample, (tn, 1))], axis=0)
    mod = _mod_call(c_all, w_ada, b_ada)
    mod_p = mod[:, :bp].reshape(DEPTH, bp, 1, 6 * D_MODEL)
    mod_s = mod[:, bp:]

    bf = lambda w: w.astype(BF16)
    win, wpool, wa, wb, wout, wup, wdn = map(bf, (w_in, w_pool, w_a, w_b, w_out, w_up, w_down))
    bias = jnp.asarray(_prompt_bias())
    slopes = jnp.asarray(_alibi_slopes(), F32)
    fg = final_g.reshape(1, D_MODEL)

    xs = x_sample.transpose(1, 0, 2).reshape(tn * nb, D_MODEL)
    ck = cache_k.reshape(DEPTH, nb, WINDOW, KV_WIDTH)
    cv = cache_v.reshape(DEPTH, nb, WINDOW, KV_WIDTH)
    ck_t = ck.transpose(0, 3, 2, 1)
    cv_t = cv.transpose(0, 3, 2, 1)
    prefix_t = state_pool.transpose(0, 2, 1, 3)

    xp = x_prompt
    kp, vp, pp, kn_l, vn_l, ps = [], [], [], [], [], []
    tm_mix, tm_ffn = 256, 256
    for l in range(DEPTH):
        last = l == DEPTH - 1
        g1 = norm1_g[l].reshape(1, D_MODEL)
        g2 = norm2_g[l].reshape(1, D_MODEL)
        psc = pool_scale[l].reshape(1, POOL_WIDTH)

        xp, nk, nv, npool = _pmix_call(xp, mod_p[l], g1, win[l], wpool[l], psc, wa[l], wb[l],
                                       wout[l], attn_sinks[l], bias, tm_mix)
        xp = _ffn_call(xp, mod_p[l], g2, wup[l], wdn[l], fg, tm_ffn, last, False)
        kp.append(nk); vp.append(nv); pp.append(npool)

        q, kn, vn, ma, sgb, npool_s = _sproj_call(xs, mod_s[l], g1, win[l], wpool[l], psc, wa[l],
                                                  prefix_t[l], nb, tn)
        o = _sattn_call(q, kn, vn, ck_t[l], cv_t[l], attn_sinks[l], slopes, nb, tn)
        xs = _spost_call(xs, o, ma, sgb, mod_s[l], wb[l], wout[l])
        xs = _ffn_call(xs[None], mod_s[l][None], g2, wup[l], wdn[l], fg, 256, last, True)[0]
        kn_l.append(kn); vn_l.append(vn); ps.append(npool_s)

    to_bt = lambda a: a.reshape(tn, nb, a.shape[-1]).transpose(1, 0, 2)
    nks, nvs = _roll_call(ck, cv, jnp.stack([to_bt(a) for a in kn_l]),
                          jnp.stack([to_bt(a) for a in vn_l]), tn)

    kv_shape_p = (DEPTH, bp, WINDOW, N_KV_HEADS, HEAD_DIM)
    kv_shape_s = (DEPTH, nb, WINDOW, N_KV_HEADS, HEAD_DIM)
    return (xp,
            to_bt(xs),
            jnp.stack(kp).reshape(kv_shape_p),
            jnp.stack(vp).reshape(kv_shape_p),
            jnp.stack(pp),
            nks.reshape(kv_shape_s),
            nvs.reshape(kv_shape_s),
            jnp.stack(ps).transpose(0, 2, 1, 3))
```

```python
import functools

import numpy as np
import jax
import jax.numpy as jnp
from jax import lax
from jax.experimental import pallas as pl
from jax.experimental.pallas import tpu as pltpu

D_MODEL = 1024
DEPTH = 2
PAST_LEN = 16384
POOL_WIDTH = D_MODEL // 2
POOL_WINDOWS = (2, 4, 8, 16)
POOL_GROUP = POOL_WIDTH // len(POOL_WINDOWS)
POOL_PAD = max(POOL_WINDOWS) - 1
N_HEADS = 8
N_KV_HEADS = 2
HEAD_DIM = 64
GQA_GROUP = N_HEADS // N_KV_HEADS
ATTN_WIDTH = N_HEADS * HEAD_DIM
KV_WIDTH = N_KV_HEADS * HEAD_DIM
WINDOW = 128
ATTN_BLOCK = 128
D_FF = 4 * D_MODEL
RMS_EPS = 1e-6
NEG_INF = -1e30

OFF_U = 0
OFF_Q = OFF_U + POOL_WIDTH
OFF_K = OFF_Q + ATTN_WIDTH
OFF_V = OFF_K + KV_WIDTH
OFF_GA = OFF_V + KV_WIDTH
OFF_GB = OFF_GA + D_MODEL
IN_WIDTH = OFF_GB + D_MODEL

LANES = 128
HIST = 16
VMEM_LIMIT = 56 * 1024 * 1024

F32 = jnp.float32
BF16 = jnp.bfloat16


def _bdot(a, b):
    return jnp.dot(a.astype(BF16), b.astype(BF16), preferred_element_type=F32)


def _rms_mod(x, g, sc, sh):
    ms = jnp.mean(x * x, axis=-1, keepdims=True)
    return (x * lax.rsqrt(ms + RMS_EPS) * g) * (1.0 + sc) + sh


def _alibi_slopes():
    return 2.0 ** (-8.0 * (np.arange(N_HEADS) + 1) / N_HEADS)


def _mod_kernel(c_ref, w_ref, b_ref, o_ref):
    c = c_ref[...]
    s = c * jax.nn.sigmoid(c)
    o_ref[...] = _bdot(s, w_ref[...]) + b_ref[...]


def _mod_call(c_all, w_ada, b_ada):
    m = c_all.shape[0]
    tn = 1536
    n = 6 * D_MODEL
    return pl.pallas_call(
        _mod_kernel,
        grid=(DEPTH, n // tn),
        in_specs=[
            pl.BlockSpec((m, D_MODEL), lambda l, j: (0, 0)),
            pl.BlockSpec((None, D_MODEL, tn), lambda l, j: (l, 0, j)),
            pl.BlockSpec((None, 1, tn), lambda l, j: (l, 0, j)),
        ],
        out_specs=pl.BlockSpec((None, m, tn), lambda l, j: (l, 0, j)),
        out_shape=jax.ShapeDtypeStruct((DEPTH, m, n), F32),
        compiler_params=pltpu.CompilerParams(
            dimension_semantics=("arbitrary", "arbitrary"),
            vmem_limit_bytes=VMEM_LIMIT),
        name="adaln_mod",
    )(c_all, w_ada, b_ada.reshape(DEPTH, 1, n))


def _prompt_bias():
    i = np.arange(ATTN_BLOCK)[:, None]
    j = np.arange(2 * ATTN_BLOCK)[None, :]
    dist = i + ATTN_BLOCK - j
    valid = (dist >= 0) & (dist < WINDOW)
    valid_first = valid & (j >= ATTN_BLOCK)
    sl = _alibi_slopes()[:, None, None]
    b = np.where(valid[None], -sl * dist[None], NEG_INF)
    b0 = np.where(valid_first[None], -sl * dist[None], NEG_INF)
    return np.stack([b, b0]).astype(np.float32)


def _pool_windows(u, ubuf, tm, pos):
    outs = []
    for g, w in enumerate(POOL_WINDOWS):
        lo = g * POOL_GROUP
        ug = u[:, lo:lo + POOL_GROUP]
        acc = ug
        for s in range(1, w):
            acc = acc + ubuf[HIST - s:HIST - s + tm, lo:lo + POOL_GROUP]
        cnt = jnp.minimum(w, pos + 1).astype(F32)
        outs.append(acc / cnt - ug)
    return outs


def _pmix_kernel(sinks_ref, x_ref, mod_ref, g1_ref, win_ref, wpool_ref, pscale_ref,
                 wa_ref, wb_ref, wout_ref, bias_ref,
                 xo_ref, nk_ref, nv_ref, npool_ref,
                 ubuf, ka, kb, va, vb, obuf, *, tm):
    t = pl.program_id(1)
    nt = pl.num_programs(1)

    @pl.when(t == 0)
    def _init():
        ubuf[0:HIST, :] = jnp.zeros((HIST, POOL_WIDTH), F32)
        zero = jnp.zeros((N_KV_HEADS, ATTN_BLOCK, LANES), BF16)
        ka[:, 0:ATTN_BLOCK, :] = zero
        kb[:, 0:ATTN_BLOCK, :] = zero
        va[:, 0:ATTN_BLOCK, :] = zero
        vb[:, 0:ATTN_BLOCK, :] = zero

    x = x_ref[...]
    mod = mod_ref[...]
    sh1 = mod[:, 0:D_MODEL]
    sc1 = mod[:, D_MODEL:2 * D_MODEL]
    gt1 = mod[:, 2 * D_MODEL:3 * D_MODEL]
    h = _rms_mod(x, g1_ref[...], sc1, sh1).astype(BF16)

    u = jnp.dot(h, win_ref[:, OFF_U:OFF_Q], preferred_element_type=F32)
    ubuf[HIST:HIST + tm, :] = u
    pos = t * tm + lax.broadcasted_iota(jnp.int32, (tm, 1), 0)
    d = _pool_windows(u, ubuf, tm, pos)
    y = jnp.concatenate(
        [_bdot(d[g], wpool_ref[g]) for g in range(len(POOL_WINDOWS))], axis=1)
    y = y * pscale_ref[...]
    br_a = _bdot(y, wa_ref[...])

    q = jnp.dot(h, win_ref[:, OFF_Q:OFF_K], preferred_element_type=F32)
    qb = (q * (HEAD_DIM ** -0.5)).astype(BF16)
    kv = jnp.dot(h, win_ref[:, OFF_K:OFF_GA], preferred_element_type=F32)
    k = kv[:, 0:KV_WIDTH]
    v = kv[:, KV_WIDTH:2 * KV_WIDTH]
    lane = lax.broadcasted_iota(jnp.int32, (tm, LANES), 1)
    low = lane < HEAD_DIM
    kr = pltpu.roll(k, HEAD_DIM, axis=1)
    vr = pltpu.roll(v, HEAD_DIM, axis=1)
    ka[0, ATTN_BLOCK:ATTN_BLOCK + tm, :] = jnp.where(low, k, 0.0).astype(BF16)
    kb[0, ATTN_BLOCK:ATTN_BLOCK + tm, :] = jnp.where(low, 0.0, kr).astype(BF16)
    ka[1, ATTN_BLOCK:ATTN_BLOCK + tm, :] = jnp.where(low, kr, 0.0).astype(BF16)
    kb[1, ATTN_BLOCK:ATTN_BLOCK + tm, :] = jnp.where(low, 0.0, k).astype(BF16)
    va[0, ATTN_BLOCK:ATTN_BLOCK + tm, :] = jnp.where(low, v, 0.0).astype(BF16)
    vb[0, ATTN_BLOCK:ATTN_BLOCK + tm, :] = jnp.where(low, 0.0, vr).astype(BF16)
    va[1, ATTN_BLOCK:ATTN_BLOCK + tm, :] = jnp.where(low, vr, 0.0).astype(BF16)
    vb[1, ATTN_BLOCK:ATTN_BLOCK + tm, :] = jnp.where(low, 0.0, v).astype(BF16)

    low_q = lax.broadcasted_iota(jnp.int32, (ATTN_BLOCK, LANES), 1) < HEAD_DIM
    nt_dims = (((1,), (1,)), ((), ()))
    units = [(j, hk, pr) for j in range(tm // ATTN_BLOCK)
             for hk in range(N_KV_HEADS) for pr in range(GQA_GROUP // 2)]
    scores = {}
    for (j, hk, pr) in units:
        r0 = j * ATTN_BLOCK
        c0 = (hk * (GQA_GROUP // 2) + pr) * LANES
        q2 = qb[r0:r0 + ATTN_BLOCK, c0:c0 + LANES]
        ks = (ka[hk, r0:r0 + 2 * ATTN_BLOCK, :], kb[hk, r0:r0 + 2 * ATTN_BLOCK, :])
        for e in range(2):
            scores[(j, hk, pr, e)] = lax.dot_general(q2, ks[e], nt_dims,
                                                     preferred_element_type=F32)
    ga = jnp.dot(h, win_ref[:, OFF_GA:OFF_GB], preferred_element_type=F32)
    probs, dens = {}, {}
    for (j, hk, pr) in units:
        for e in range(2):
            head = hk * GQA_GROUP + pr * 2 + e
            if j == 0:
                bias = jnp.where(t == 0, bias_ref[1, head], bias_ref[0, head])
            else:
                bias = bias_ref[0, head]
            s = scores[(j, hk, pr, e)] + bias
            sink = sinks_ref[head]
            m = jnp.maximum(jnp.max(s, axis=-1, keepdims=True), sink)
            p = jnp.exp(s - m)
            dens[(j, hk, pr, e)] = jnp.sum(p, axis=-1, keepdims=True) + jnp.exp(sink - m)
            probs[(j, hk, pr, e)] = p.astype(BF16)
    gb = jnp.dot(h, win_ref[:, OFF_GB:IN_WIDTH], preferred_element_type=F32)
    for (j, hk, pr) in units:
        r0 = j * ATTN_BLOCK
        c0 = (hk * (GQA_GROUP // 2) + pr) * LANES
        vs = (va[hk, r0:r0 + 2 * ATTN_BLOCK, :], vb[hk, r0:r0 + 2 * ATTN_BLOCK, :])
        o2 = (jnp.dot(probs[(j, hk, pr, 0)], vs[0], preferred_element_type=F32)
              + jnp.dot(probs[(j, hk, pr, 1)], vs[1], preferred_element_type=F32))
        den = jnp.where(low_q, dens[(j, hk, pr, 0)], dens[(j, hk, pr, 1)])
        obuf[r0:r0 + ATTN_BLOCK, c0:c0 + LANES] = o2 / den

    br_b = _bdot(obuf[...], wb_ref[...])
    merged = jax.nn.sigmoid(ga) * br_a
    merged = merged + jax.nn.sigmoid(gb) * br_b
    xo_ref[...] = x + gt1 * _bdot(merged, wout_ref[...])

    @pl.when(t == nt - 1)
    def _state():
        nk_ref[...] = k[tm - WINDOW:, :]
        nv_ref[...] = v[tm - WINDOW:, :]
        npool_ref[...] = ubuf[HIST + tm - POOL_PAD:HIST + tm, :]

    ubuf[0:HIST, :] = ubuf[tm:tm + HIST, :]
    for buf in (ka, kb, va, vb):
        buf[:, 0:ATTN_BLOCK, :] = buf[:, tm:tm + ATTN_BLOCK, :]


def _const_spec(shape):
    nd = len(shape)
    return pl.BlockSpec(shape, lambda *_: (0,) * nd, pipeline_mode=pl.Buffered(1))


def _whole_out_spec(shape):
    nd = len(shape)
    return pl.BlockSpec(shape, lambda *_: (0,) * nd)


def _pmix_call(x, mod_p, g1, win, wpool, pscale, wa, wb, wout, sinks, bias, tm):
    b, t, _ = x.shape
    kern = functools.partial(_pmix_kernel, tm=tm)
    grid_spec = pltpu.PrefetchScalarGridSpec(
        num_scalar_prefetch=1,
        grid=(b, t // tm),
        in_specs=[
            pl.BlockSpec((None, tm, D_MODEL), lambda i, j, s: (i, j, 0)),
            pl.BlockSpec((None, 1, 3 * D_MODEL), lambda i, j, s: (i, 0, 0)),
            _const_spec((1, D_MODEL)),
            _const_spec((D_MODEL, IN_WIDTH)),
            _const_spec((len(POOL_WINDOWS), POOL_GROUP, POOL_GROUP)),
            _const_spec((1, POOL_WIDTH)),
            _const_spec((POOL_WIDTH, D_MODEL)),
            _const_spec((ATTN_WIDTH, D_MODEL)),
            _const_spec((D_MODEL, D_MODEL)),
            _const_spec((2, N_HEADS, ATTN_BLOCK, 2 * ATTN_BLOCK)),
        ],
        out_specs=[
            pl.BlockSpec((None, tm, D_MODEL), lambda i, j, s: (i, j, 0)),
            pl.BlockSpec((None, WINDOW, KV_WIDTH), lambda i, j, s: (i, 0, 0)),
            pl.BlockSpec((None, WINDOW, KV_WIDTH), lambda i, j, s: (i, 0, 0)),
            pl.BlockSpec((None, POOL_PAD, POOL_WIDTH), lambda i, j, s: (i, 0, 0)),
        ],
        scratch_shapes=[
            pltpu.VMEM((HIST + tm, POOL_WIDTH), F32),
            pltpu.VMEM((N_KV_HEADS, ATTN_BLOCK + tm, LANES), BF16),
            pltpu.VMEM((N_KV_HEADS, ATTN_BLOCK + tm, LANES), BF16),
            pltpu.VMEM((N_KV_HEADS, ATTN_BLOCK + tm, LANES), BF16),
            pltpu.VMEM((N_KV_HEADS, ATTN_BLOCK + tm, LANES), BF16),
            pltpu.VMEM((tm, ATTN_WIDTH), F32),
        ],
    )
    return pl.pallas_call(
        kern,
        grid_spec=grid_spec,
        out_shape=[
            jax.ShapeDtypeStruct((b, t, D_MODEL), F32),
            jax.ShapeDtypeStruct((b, WINDOW, KV_WIDTH), F32),
            jax.ShapeDtypeStruct((b, WINDOW, KV_WIDTH), F32),
            jax.ShapeDtypeStruct((b, POOL_PAD, POOL_WIDTH), F32),
        ],
        compiler_params=pltpu.CompilerParams(
            dimension_semantics=("arbitrary", "arbitrary"),
            vmem_limit_bytes=VMEM_LIMIT),
        name="prompt_mixer",
    )(sinks, x, mod_p, g1, win, wpool, pscale, wa, wb, wout, bias)


def _ffn_kernel(x_ref, mod_ref, g2_ref, wup_ref, wdn_ref, fg_ref, o_ref, *, final):
    x = x_ref[...]
    mod = mod_ref[...]
    sh2 = mod[:, 0:D_MODEL]
    sc2 = mod[:, D_MODEL:2 * D_MODEL]
    gt2 = mod[:, 2 * D_MODEL:3 * D_MODEL]
    h2 = _rms_mod(x, g2_ref[...], sc2, sh2)
    ff = _bdot(h2, wup_ref[...])
    ff = jnp.square(jnp.maximum(ff, 0.0))
    y = x + gt2 * _bdot(ff, wdn_ref[...])
    if final:
        ms = jnp.mean(y * y, axis=-1, keepdims=True)
        y = y * lax.rsqrt(ms + RMS_EPS) * fg_ref[...]
    o_ref[...] = y


def _ffn_call(x, mod, g2, wup, wdn, fg, tm, final, per_row_mod):
    b, t, _ = x.shape
    if per_row_mod:
        mod_spec = pl.BlockSpec((None, tm, 3 * D_MODEL), lambda i, j: (i, j, 1))
    else:
        mod_spec = pl.BlockSpec((None, 1, 3 * D_MODEL), lambda i, j: (i, 0, 1))
    return pl.pallas_call(
        functools.partial(_ffn_kernel, final=final),
        grid=(b, t // tm),
        in_specs=[
            pl.BlockSpec((None, tm, D_MODEL), lambda i, j: (i, j, 0)),
            mod_spec,
            _const_spec((1, D_MODEL)),
            _const_spec((D_MODEL, D_FF)),
            _const_spec((D_FF, D_MODEL)),
            _const_spec((1, D_MODEL)),
        ],
        out_specs=pl.BlockSpec((None, tm, D_MODEL), lambda i, j: (i, j, 0)),
        out_shape=jax.ShapeDtypeStruct((b, t, D_MODEL), F32),
        compiler_params=pltpu.CompilerParams(
            dimension_semantics=("arbitrary", "arbitrary"),
            vmem_limit_bytes=VMEM_LIMIT),
        name="ffn_final" if final else "ffn",
    )(x, mod, g2, wup, wdn, fg)


def _sproj_kernel(x_ref, mod_ref, g1_ref, win_ref, wpool_ref, pscale_ref, wa_ref, pre_ref,
                  q_ref, k_ref, v_ref, ma_ref, sgb_ref, npool_ref, *, nb, tn):
    x = x_ref[...]
    mod = mod_ref[...]
    sh1 = mod[:, 0:D_MODEL]
    sc1 = mod[:, D_MODEL:2 * D_MODEL]
    h = _rms_mod(x, g1_ref[...], sc1, sh1).astype(BF16)

    u = jnp.dot(h, win_ref[:, OFF_U:OFF_Q], preferred_element_type=F32)
    def up(r):
        if r < POOL_PAD:
            return pre_ref[r]
        r -= POOL_PAD
        return u[r * nb:(r + 1) * nb, :]

    d_rows = []
    for tt in range(tn):
        parts = []
        for g, w in enumerate(POOL_WINDOWS):
            lo = g * POOL_GROUP
            acc = up(POOL_PAD + tt)[:, lo:lo + POOL_GROUP]
            for s in range(1, w):
                acc = acc + up(POOL_PAD + tt - s)[:, lo:lo + POOL_GROUP]
            cnt = float(min(w, PAST_LEN + tt + 1))
            parts.append(acc / cnt - up(POOL_PAD + tt)[:, lo:lo + POOL_GROUP])
        d_rows.append(parts)
    y = jnp.concatenate(
        [_bdot(jnp.concatenate([d_rows[tt][g] for tt in range(tn)], axis=0), wpool_ref[g])
         for g in range(len(POOL_WINDOWS))], axis=1)
    y = y * pscale_ref[...]
    br_a = _bdot(y, wa_ref[...])
    for r in range(POOL_PAD):
        npool_ref[r] = up(r + tn)

    q = jnp.dot(h, win_ref[:, OFF_Q:OFF_K], preferred_element_type=F32)
    q_ref[...] = q * (HEAD_DIM ** -0.5)
    kv = jnp.dot(h, win_ref[:, OFF_K:OFF_GA], preferred_element_type=F32)
    k_ref[...] = kv[:, 0:KV_WIDTH]
    v_ref[...] = kv[:, KV_WIDTH:2 * KV_WIDTH]
    ga = jnp.dot(h, win_ref[:, OFF_GA:OFF_GB], preferred_element_type=F32)
    ma_ref[...] = jax.nn.sigmoid(ga) * br_a
    gb = jnp.dot(h, win_ref[:, OFF_GB:IN_WIDTH], preferred_element_type=F32)
    sgb_ref[...] = jax.nn.sigmoid(gb)


def _sproj_call(x, mod_s, g1, win, wpool, pscale, wa, prefix_t, nb, tn):
    m = x.shape[0]
    shapes = [
        jax.ShapeDtypeStruct((m, ATTN_WIDTH), F32),
        jax.ShapeDtypeStruct((m, KV_WIDTH), F32),
        jax.ShapeDtypeStruct((m, KV_WIDTH), F32),
        jax.ShapeDtypeStruct((m, D_MODEL), F32),
        jax.ShapeDtypeStruct((m, D_MODEL), F32),
        jax.ShapeDtypeStruct((POOL_PAD, nb, POOL_WIDTH), F32),
    ]
    in_shapes = [x.shape, (m, 3 * D_MODEL), g1.shape, win.shape, wpool.shape, pscale.shape,
                 wa.shape, prefix_t.shape]
    return pl.pallas_call(
        functools.partial(_sproj_kernel, nb=nb, tn=tn),
        grid=(1,),
        in_specs=[_const_spec(s) for s in in_shapes],
        out_specs=[_whole_out_spec(s.shape) for s in shapes],
        out_shape=shapes,
        compiler_params=pltpu.CompilerParams(
            dimension_semantics=("arbitrary",),
            vmem_limit_bytes=VMEM_LIMIT),
        name="sample_proj",
    )(x, mod_s, g1, win, wpool, pscale, wa, prefix_t)


def _sattn_kernel(sinks_ref, slopes_ref, q_ref, kn_ref, vn_ref, kc_ref, vc_ref, o_ref,
                  qt, knt, vnt, sbuf, ot, *, nb, tn):
    hk = pl.program_id(0)
    nq = tn * GQA_GROUP
    srows = WINDOW + 8
    hoff = pl.multiple_of(hk * HEAD_DIM, HEAD_DIM)
    for tt in range(tn):
        qt[tt] = q_ref[tt * nb:(tt + 1) * nb, :].T
        knt[tt] = kn_ref[tt * nb:(tt + 1) * nb, :].T
        vnt[tt] = vn_ref[tt * nb:(tt + 1) * nb, :].T

    jb = 16
    for c in range(WINDOW // jb):
        def qk_body(dd, accs):
            k2 = kc_ref[dd, c * jb:(c + 1) * jb, :]
            new = []
            for tt in range(tn):
                for g in range(GQA_GROUP):
                    qrow = qt[tt, pl.ds(g * HEAD_DIM + dd, 1), :]
                    new.append(accs[tt * GQA_GROUP + g] + qrow * k2)
            return tuple(new)
        accs = lax.fori_loop(0, HEAD_DIM, qk_body,
                             tuple(jnp.zeros((jb, nb), F32) for _ in range(nq)))
        for i in range(nq):
            sbuf[i, c * jb:(c + 1) * jb, :] = accs[i]

    for tt in range(tn):
        for g in range(GQA_GROUP):
            qh = qt[tt, g * HEAD_DIM:(g + 1) * HEAD_DIM, :]
            rows = [jnp.sum(qh * knt[t2, pl.ds(hoff, HEAD_DIM), :], axis=0, keepdims=True)
                    for t2 in range(tn)]
            rows.append(jnp.zeros((8 - tn, nb), F32))
            sbuf[tt * GQA_GROUP + g, WINDOW:srows, :] = jnp.concatenate(rows, axis=0)

    jpos = lax.broadcasted_iota(jnp.int32, (srows, nb), 0)
    for tt in range(tn):
        dist = tt + WINDOW - jpos
        valid = (dist >= 0) & (dist < WINDOW)
        distf = dist.astype(F32)
        for g in range(GQA_GROUP):
            i = tt * GQA_GROUP + g
            head = hk * GQA_GROUP + g
            s = sbuf[i] - slopes_ref[head] * distf
            s = jnp.where(valid, s, NEG_INF)
            sink = sinks_ref[head]
            m = jnp.maximum(jnp.max(s, axis=0, keepdims=True), sink)
            p = jnp.exp(s - m)
            den = jnp.sum(p, axis=0, keepdims=True) + jnp.exp(sink - m)
            sbuf[i] = p / den

    def pv_body(dd, carry):
        vd = vc_ref[dd]
        for tt in range(tn):
            for g in range(GQA_GROUP):
                i = tt * GQA_GROUP + g
                row = jnp.sum(sbuf[i, 0:WINDOW, :] * vd, axis=0, keepdims=True)
                ot[tt, pl.ds(g * HEAD_DIM + dd, 1), :] = row
        return carry
    lax.fori_loop(0, HEAD_DIM, pv_body, 0)

    for tt in range(tn):
        for g in range(GQA_GROUP):
            i = tt * GQA_GROUP + g
            acc = ot[tt, g * HEAD_DIM:(g + 1) * HEAD_DIM, :]
            for t2 in range(tt + 1):
                acc = acc + sbuf[i, WINDOW + t2:WINDOW + t2 + 1, :] * vnt[t2, pl.ds(hoff, HEAD_DIM), :]
            ot[tt, g * HEAD_DIM:(g + 1) * HEAD_DIM, :] = acc
        o_ref[tt * nb:(tt + 1) * nb, :] = ot[tt].T


def _sattn_call(q, kn, vn, kc_t, vc_t, sinks, slopes, nb, tn):
    m = q.shape[0]
    gw = GQA_GROUP * HEAD_DIM
    grid_spec = pltpu.PrefetchScalarGridSpec(
        num_scalar_prefetch=2,
        grid=(N_KV_HEADS,),
        in_specs=[
            pl.BlockSpec((m, gw), lambda h, a, b: (0, h)),
            pl.BlockSpec((m, KV_WIDTH), lambda h, a, b: (0, 0)),
            pl.BlockSpec((m, KV_WIDTH), lambda h, a, b: (0, 0)),
            pl.BlockSpec((HEAD_DIM, WINDOW, nb), lambda h, a, b: (h, 0, 0)),
            pl.BlockSpec((HEAD_DIM, WINDOW, nb), lambda h, a, b: (h, 0, 0)),
        ],
        out_specs=pl.BlockSpec((m, gw), lambda h, a, b: (0, h)),
        scratch_shapes=[
            pltpu.VMEM((tn, gw, nb), F32),
            pltpu.VMEM((tn, KV_WIDTH, nb), F32),
            pltpu.VMEM((tn, KV_WIDTH, nb), F32),
            pltpu.VMEM((tn * GQA_GROUP, WINDOW + 8, nb), F32),
            pltpu.VMEM((tn, gw, nb), F32),
        ],
    )
    return pl.pallas_call(
        functools.partial(_sattn_kernel, nb=nb, tn=tn),
        grid_spec=grid_spec,
        out_shape=jax.ShapeDtypeStruct((m, ATTN_WIDTH), F32),
        compiler_params=pltpu.CompilerParams(
            dimension_semantics=("arbitrary",),
            vmem_limit_bytes=VMEM_LIMIT),
        name="sample_attn",
    )(sinks, slopes, q, kn, vn, kc_t, vc_t)


def _spost_kernel(x_ref, o_ref, ma_ref, sgb_ref, mod_ref, wb_ref, wout_ref, xo_ref):
    gt1 = mod_ref[...]
    br_b = _bdot(o_ref[...], wb_ref[...])
    merged = ma_ref[...] + sgb_ref[...] * br_b
    xo_ref[...] = x_ref[...] + gt1 * _bdot(merged, wout_ref[...])


def _spost_call(x, o, ma, sgb, mod_s, wb, wout):
    ins = (x, o, ma, sgb, mod_s, wb, wout)
    specs = [_const_spec(a.shape) for a in ins]
    specs[4] = pl.BlockSpec(x.shape, lambda i: (0, 2))
    return pl.pallas_call(
        _spost_kernel,
        grid=(1,),
        in_specs=specs,
        out_specs=_whole_out_spec(x.shape),
        out_shape=jax.ShapeDtypeStruct(x.shape, F32),
        compiler_params=pltpu.CompilerParams(
            dimension_semantics=("arbitrary",),
            vmem_limit_bytes=VMEM_LIMIT),
        name="sample_post",
    )(*ins)


def _roll_kernel(ck_ref, cv_ref, kn_ref, vn_ref, ok_ref, ov_ref, *, tn):
    keep = WINDOW - tn
    ok_ref[:, 0:keep, :] = ck_ref[:, tn:WINDOW, :]
    ok_ref[:, keep:WINDOW, :] = kn_ref[...]
    ov_ref[:, 0:keep, :] = cv_ref[:, tn:WINDOW, :]
    ov_ref[:, keep:WINDOW, :] = vn_ref[...]


def _roll_call(ck, cv, kn, vn, tn):
    nl, nb = ck.shape[0], ck.shape[1]
    bb = 16
    cspec = pl.BlockSpec((None, bb, WINDOW, KV_WIDTH), lambda l, i: (l, i, 0, 0))
    nspec = pl.BlockSpec((None, bb, tn, KV_WIDTH), lambda l, i: (l, i, 0, 0))
    shp = jax.ShapeDtypeStruct(ck.shape, F32)
    return pl.pallas_call(
        functools.partial(_roll_kernel, tn=tn),
        grid=(nl, nb // bb),
        in_specs=[cspec, cspec, nspec, nspec],
        out_specs=[cspec, cspec],
        out_shape=[shp, shp],
        compiler_params=pltpu.CompilerParams(
            dimension_semantics=("arbitrary", "arbitrary"),
            vmem_limit_bytes=VMEM_LIMIT),
        name="cache_roll",
    )(ck, cv, kn, vn)


def kernel(x_prompt, x_sample, cache_k, cache_v, state_pool, c_prompt, c_sample,
           w_ada, b_ada, norm1_g, w_in, w_pool, pool_scale, attn_sinks, w_a, w_b,
           w_out, norm2_g, w_up, w_down, final_g):
    bp, tp, _ = x_prompt.shape
    nb, tn, _ = x_sample.shape
    assert tp % ATTN_BLOCK == 0 and tn <= 8 and nb == LANES

    c_all = jnp.concatenate([c_prompt, jnp.tile(c_sample, (tn, 1))], axis=0)
    mod = _mod_call(c_all, w_ada, b_ada)
    mod_p = mod[:, :bp].reshape(DEPTH, bp, 1, 6 * D_MODEL)
    mod_s = mod[:, bp:]

    bf = lambda w: w.astype(BF16)
    win, wpool, wa, wb, wout, wup, wdn = map(bf, (w_in, w_pool, w_a, w_b, w_out, w_up, w_down))
    bias = jnp.asarray(_prompt_bias())
    slopes = jnp.asarray(_alibi_slopes(), F32)
    fg = final_g.reshape(1, D_MODEL)

    xs = x_sample.transpose(1, 0, 2).reshape(tn * nb, D_MODEL)
    ck = cache_k.reshape(DEPTH, nb, WINDOW, KV_WIDTH)
    cv = cache_v.reshape(DEPTH, nb, WINDOW, KV_WIDTH)
    ck_t = ck.transpose(0, 3, 2, 1)
    cv_t = cv.transpose(0, 3, 2, 1)
    prefix_t = state_pool.transpose(0, 2, 1, 3)

    xp = x_prompt
    kp, vp, pp, kn_l, vn_l, ps = [], [], [], [], [], []
    tm_mix, tm_ffn = 256, 512
    for l in range(DEPTH):
        last = l == DEPTH - 1
        g1 = norm1_g[l].reshape(1, D_MODEL)
        g2 = norm2_g[l].reshape(1, D_MODEL)
        psc = pool_scale[l].reshape(1, POOL_WIDTH)

        xp, nk, nv, npool = _pmix_call(xp, mod_p[l], g1, win[l], wpool[l], psc, wa[l], wb[l],
                                       wout[l], attn_sinks[l], bias, tm_mix)
        xp = _ffn_call(xp, mod_p[l], g2, wup[l], wdn[l], fg, tm_ffn, last, False)
        kp.append(nk); vp.append(nv); pp.append(npool)

        q, kn, vn, ma, sgb, npool_s = _sproj_call(xs, mod_s[l], g1, win[l], wpool[l], psc, wa[l],
                                                  prefix_t[l], nb, tn)
        o = _sattn_call(q, kn, vn, ck_t[l], cv_t[l], attn_sinks[l], slopes, nb, tn)
        xs = _spost_call(xs, o, ma, sgb, mod_s[l], wb[l], wout[l])
        xs = _ffn_call(xs[None], mod_s[l][None], g2, wup[l], wdn[l], fg, 256, last, True)[0]
        kn_l.append(kn); vn_l.append(vn); ps.append(npool_s)

    to_bt = lambda a: a.reshape(tn, nb, a.shape[-1]).transpose(1, 0, 2)
    nks, nvs = _roll_call(ck, cv, jnp.stack([to_bt(a) for a in kn_l]),
                          jnp.stack([to_bt(a) for a in vn_l]), tn)

    kv_shape_p = (DEPTH, bp, WINDOW, N_KV_HEADS, HEAD_DIM)
    kv_shape_s = (DEPTH, nb, WINDOW, N_KV_HEADS, HEAD_DIM)
    return (xp,
            to_bt(xs),
            jnp.stack(kp).reshape(kv_shape_p),
            jnp.stack(vp).reshape(kv_shape_p),
            jnp.stack(pp),
            nks.reshape(kv_shape_s),
            nvs.reshape(kv_shape_s),
            jnp.stack(ps).transpose(0, 2, 1, 3))
```

```python
import functools

import numpy as np
import jax
import jax.numpy as jnp
from jax import lax
from jax.experimental import pallas as pl
from jax.experimental.pallas import tpu as pltpu

D_MODEL = 1024
DEPTH = 2
PAST_LEN = 16384
POOL_WIDTH = D_MODEL // 2
POOL_WINDOWS = (2, 4, 8, 16)
POOL_GROUP = POOL_WIDTH // len(POOL_WINDOWS)
POOL_PAD = max(POOL_WINDOWS) - 1
N_HEADS = 8
N_KV_HEADS = 2
HEAD_DIM = 64
GQA_GROUP = N_HEADS // N_KV_HEADS
ATTN_WIDTH = N_HEADS * HEAD_DIM
KV_WIDTH = N_KV_HEADS * HEAD_DIM
WINDOW = 128
ATTN_BLOCK = 128
D_FF = 4 * D_MODEL
RMS_EPS = 1e-6
NEG_INF = -1e30

OFF_U = 0
OFF_Q = OFF_U + POOL_WIDTH
OFF_K = OFF_Q + ATTN_WIDTH
OFF_V = OFF_K + KV_WIDTH
OFF_GA = OFF_V + KV_WIDTH
OFF_GB = OFF_GA + D_MODEL
IN_WIDTH = OFF_GB + D_MODEL
MOD_SLAB = 3 * D_MODEL

LANES = 128
HIST = 16
assert all(w & (w - 1) == 0 for w in POOL_WINDOWS) and list(POOL_WINDOWS) == sorted(POOL_WINDOWS)
assert HIST >= POOL_PAD
VMEM_LIMIT = 56 * 1024 * 1024
SUB_ROWS = 256
MIX_ROWS = 2 * SUB_ROWS
FFN_ROWS = 512

F32 = jnp.float32
BF16 = jnp.bfloat16


def _bdot(a, b):
    return jnp.dot(a.astype(BF16), b.astype(BF16), preferred_element_type=F32)


def _rms_mod(x, g, sc, sh):
    ms = jnp.mean(x * x, axis=-1, keepdims=True)
    return (x * lax.rsqrt(ms + RMS_EPS) * g) * (1.0 + sc) + sh


def _alibi_slopes():
    return 2.0 ** (-8.0 * (np.arange(N_HEADS) + 1) / N_HEADS)


def _const_spec(shape):
    nd = len(shape)
    return pl.BlockSpec(shape, lambda *_: (0,) * nd, pipeline_mode=pl.Buffered(1))


def _layer_spec(shape, l):
    nd = len(shape)
    return pl.BlockSpec((None,) + tuple(shape), lambda *_: (l,) + (0,) * nd,
                        pipeline_mode=pl.Buffered(1))


def _whole_out_spec(shape):
    nd = len(shape)
    return pl.BlockSpec(shape, lambda *_: (0,) * nd)


def _mod_kernel(cp_ref, cs_ref, w_ref, b_ref, op_ref, os_ref):
    w = w_ref[...].astype(BF16)
    b = b_ref[...]
    for c_ref, o_ref in ((cp_ref, op_ref), (cs_ref, os_ref)):
        c = c_ref[...]
        s = (c * jax.nn.sigmoid(c)).astype(BF16)
        o_ref[...] = jnp.dot(s, w, preferred_element_type=F32) + b


def _mod_call(c_p, c_s, w_ada, b_ada):
    tn = 1024
    n = 6 * D_MODEL
    mp, ms = c_p.shape[0], c_s.shape[0]
    return pl.pallas_call(
        _mod_kernel,
        grid=(DEPTH, n // tn),
        in_specs=[
            _const_spec((mp, D_MODEL)),
            _const_spec((ms, D_MODEL)),
            pl.BlockSpec((None, D_MODEL, tn), lambda l, j: (l, 0, j)),
            pl.BlockSpec((None, 1, tn), lambda l, j: (l, 0, j)),
        ],
        out_specs=[
            pl.BlockSpec((None, mp, tn), lambda l, j: (l, 0, j)),
            pl.BlockSpec((None, ms, tn), lambda l, j: (l, 0, j)),
        ],
        out_shape=[jax.ShapeDtypeStruct((DEPTH, mp, n), F32),
                   jax.ShapeDtypeStruct((DEPTH, ms, n), F32)],
        compiler_params=pltpu.CompilerParams(
            dimension_semantics=("arbitrary", "arbitrary"),
            vmem_limit_bytes=VMEM_LIMIT),
        name="adaln_mod",
    )(c_p, c_s, w_ada, b_ada.reshape(DEPTH, 1, n))


def _prompt_bias():
    i = np.arange(ATTN_BLOCK)[:, None]
    j = np.arange(2 * ATTN_BLOCK)[None, :]
    dist = i + ATTN_BLOCK - j
    valid = (dist >= 0) & (dist < WINDOW)
    valid_first = valid & (j >= ATTN_BLOCK)
    sl = _alibi_slopes()[:, None, None]
    b = np.where(valid[None], -sl * dist[None], NEG_INF)
    b0 = np.where(valid_first[None], -sl * dist[None], NEG_INF)
    return np.stack([b, b0]).astype(np.float32)


def _pmix_kernel(sinks_ref, x_ref, mod_ref, g1_ref, win_ref, wpool_ref, pscale_ref,
                 wa_ref, wb_ref, wout_ref, bias_ref,
                 xo_ref, nk_ref, nv_ref, npool_ref,
                 ubuf, ka, kb, va, vb, obuf, *, layer):
    tm, sub = MIX_ROWS, SUB_ROWS
    bi = pl.program_id(0)
    t = pl.program_id(1)
    nt = pl.num_programs(1)

    @pl.when(t == 0)
    def _init():
        ubuf[0:HIST, :] = jnp.zeros((HIST, POOL_WIDTH), F32)
        zero = jnp.zeros((N_KV_HEADS, ATTN_BLOCK, LANES), BF16)
        ka[:, 0:ATTN_BLOCK, :] = zero
        kb[:, 0:ATTN_BLOCK, :] = zero
        va[:, 0:ATTN_BLOCK, :] = zero
        vb[:, 0:ATTN_BLOCK, :] = zero

    mod = mod_ref[pl.ds(bi, 1), :]
    sh1 = mod[:, 0:D_MODEL]
    sc1 = mod[:, D_MODEL:2 * D_MODEL]
    gt1 = mod[:, 2 * D_MODEL:3 * D_MODEL]
    lane = lax.broadcasted_iota(jnp.int32, (sub, LANES), 1)
    low = lane < HEAD_DIM
    low_q = lax.broadcasted_iota(jnp.int32, (ATTN_BLOCK, LANES), 1) < HEAD_DIM
    nt_dims = (((1,), (1,)), ((), ()))
    st = [dict(ro=i * sub) for i in range(tm // sub)]

    def norm(c):
        c["x"] = x_ref[c["ro"]:c["ro"] + sub, :]
        c["h"] = _rms_mod(c["x"], g1_ref[...], sc1, sh1).astype(BF16)

    def proj(c):
        h = c["h"]
        c["u"] = jnp.dot(h, win_ref[:, OFF_U:OFF_Q], preferred_element_type=F32)
        q = jnp.dot(h, win_ref[:, OFF_Q:OFF_K], preferred_element_type=F32)
        c["qb"] = (q * (HEAD_DIM ** -0.5)).astype(BF16)
        kv = jnp.dot(h, win_ref[:, OFF_K:OFF_GA], preferred_element_type=F32)
        c["k"] = kv[:, 0:KV_WIDTH]
        c["v"] = kv[:, KV_WIDTH:2 * KV_WIDTH]

    def pool_sums(c):
        ro, u = c["ro"], c["u"]
        ubuf[HIST + ro:HIST + ro + sub, :] = u
        pos = t * tm + ro + lax.broadcasted_iota(jnp.int32, (sub, 1), 0)
        cur = ubuf[ro:ro + HIST + sub, :]
        d, w = [], 1
        for g, wg in enumerate(POOL_WINDOWS):
            while w < wg:
                cur = cur + pltpu.roll(cur, w, axis=0)
                w *= 2
            ug = u[:, g * POOL_GROUP:(g + 1) * POOL_GROUP]
            cnt = jnp.minimum(wg, pos + 1).astype(F32)
            d.append((cur[HIST:, 0:POOL_GROUP] / cnt - ug).astype(BF16))
            if g + 1 < len(POOL_WINDOWS):
                cur = cur[:, POOL_GROUP:]
        c["d"] = d

    def kv_store(c):
        r = ATTN_BLOCK + c["ro"]
        k, v = c["k"], c["v"]
        kr = pltpu.roll(k, HEAD_DIM, axis=1)
        vr = pltpu.roll(v, HEAD_DIM, axis=1)
        ka[0, r:r + sub, :] = jnp.where(low, k, 0.0).astype(BF16)
        kb[0, r:r + sub, :] = jnp.where(low, 0.0, kr).astype(BF16)
        ka[1, r:r + sub, :] = jnp.where(low, kr, 0.0).astype(BF16)
        kb[1, r:r + sub, :] = jnp.where(low, 0.0, k).astype(BF16)
        va[0, r:r + sub, :] = jnp.where(low, v, 0.0).astype(BF16)
        vb[0, r:r + sub, :] = jnp.where(low, 0.0, vr).astype(BF16)
        va[1, r:r + sub, :] = jnp.where(low, vr, 0.0).astype(BF16)
        vb[1, r:r + sub, :] = jnp.where(low, 0.0, v).astype(BF16)

    def pool_proj(c):
        y = jnp.concatenate(
            [jnp.dot(c["d"][g], wpool_ref[g], preferred_element_type=F32)
             for g in range(len(POOL_WINDOWS))], axis=1)
        c["br_a"] = _bdot(y * pscale_ref[...], wa_ref[...])

    def units(c):
        j0 = c["ro"] // ATTN_BLOCK
        return [(j, hk, pr) for j in range(j0, j0 + sub // ATTN_BLOCK)
                for hk in range(N_KV_HEADS) for pr in range(GQA_GROUP // 2)]

    def scores(c):
        sc = {}
        for (j, hk, pr) in units(c):
            r0 = j * ATTN_BLOCK
            c0 = (hk * (GQA_GROUP // 2) + pr) * LANES
            q2 = c["qb"][r0 - c["ro"]:r0 - c["ro"] + ATTN_BLOCK, c0:c0 + LANES]
            ks = (ka[hk, r0:r0 + 2 * ATTN_BLOCK, :], kb[hk, r0:r0 + 2 * ATTN_BLOCK, :])
            for e in range(2):
                sc[(j, hk, pr, e)] = lax.dot_general(q2, ks[e], nt_dims,
                                                     preferred_element_type=F32)
        c["s"] = sc

    def gate_a(c):
        c["ga"] = jnp.dot(c["h"], win_ref[:, OFF_GA:OFF_GB], preferred_element_type=F32)

    def gate_b(c):
        c["gb"] = jnp.dot(c["h"], win_ref[:, OFF_GB:IN_WIDTH], preferred_element_type=F32)

    def softmax(c):
        probs, dens = {}, {}
        for (j, hk, pr) in units(c):
            for e in range(2):
                head = hk * GQA_GROUP + pr * 2 + e
                if j == 0:
                    bias = jnp.where(t == 0, bias_ref[1, head], bias_ref[0, head])
                else:
                    bias = bias_ref[0, head]
                s = c["s"][(j, hk, pr, e)] + bias
                sink = sinks_ref[layer, head]
                m = jnp.maximum(jnp.max(s, axis=-1, keepdims=True), sink)
                p = jnp.exp(s - m)
                dens[(j, hk, pr, e)] = jnp.sum(p, axis=-1, keepdims=True) + jnp.exp(sink - m)
                probs[(j, hk, pr, e)] = p.astype(BF16)
        c["p"], c["den"] = probs, dens

    def values(c):
        for (j, hk, pr) in units(c):
            r0 = j * ATTN_BLOCK
            c0 = (hk * (GQA_GROUP // 2) + pr) * LANES
            vs = (va[hk, r0:r0 + 2 * ATTN_BLOCK, :], vb[hk, r0:r0 + 2 * ATTN_BLOCK, :])
            o2 = (jnp.dot(c["p"][(j, hk, pr, 0)], vs[0], preferred_element_type=F32)
                  + jnp.dot(c["p"][(j, hk, pr, 1)], vs[1], preferred_element_type=F32))
            den = jnp.where(low_q, c["den"][(j, hk, pr, 0)], c["den"][(j, hk, pr, 1)])
            obuf[r0:r0 + ATTN_BLOCK, c0:c0 + LANES] = o2 / den

    def tail(c):
        ro = c["ro"]
        br_b = _bdot(obuf[ro:ro + sub, :], wb_ref[...])
        merged = jax.nn.sigmoid(c["ga"]) * c["br_a"] + jax.nn.sigmoid(c["gb"]) * br_b
        xo_ref[ro:ro + sub, :] = c["x"] + gt1 * _bdot(merged, wout_ref[...])

    a, b = st
    norm(a); proj(a)
    norm(b)
    pool_sums(a); kv_store(a)
    proj(b)
    pool_proj(a); scores(a); gate_a(a)
    pool_sums(b); kv_store(b)
    softmax(a)
    pool_proj(b); scores(b); gate_a(b)
    values(a); gate_b(a)
    softmax(b)
    tail(a)
    values(b); gate_b(b)
    tail(b)

    @pl.when(t == nt - 1)
    def _state():
        nk_ref[...] = b["k"][sub - WINDOW:, :]
        nv_ref[...] = b["v"][sub - WINDOW:, :]
        npool_ref[...] = ubuf[HIST + tm - POOL_PAD:HIST + tm, :]

    ubuf[0:HIST, :] = ubuf[tm:tm + HIST, :]
    for buf in (ka, kb, va, vb):
        buf[:, 0:ATTN_BLOCK, :] = buf[:, tm:tm + ATTN_BLOCK, :]


def _pmix_call(x, mod_p, g1, win, wpool, pscale, wa, wb, wout, sinks, bias, l):
    b, t, _ = x.shape
    tm = MIX_ROWS
    grid_spec = pltpu.PrefetchScalarGridSpec(
        num_scalar_prefetch=1,
        grid=(b, t // tm),
        in_specs=[
            pl.BlockSpec((None, tm, D_MODEL), lambda i, j, s: (i, j, 0)),
            pl.BlockSpec((None, b, MOD_SLAB), lambda i, j, s: (l, 0, 0),
                         pipeline_mode=pl.Buffered(1)),
            _layer_spec((1, D_MODEL), l),
            _layer_spec((D_MODEL, IN_WIDTH), l),
            _layer_spec((len(POOL_WINDOWS), POOL_GROUP, POOL_GROUP), l),
            _layer_spec((1, POOL_WIDTH), l),
            _layer_spec((POOL_WIDTH, D_MODEL), l),
            _layer_spec((ATTN_WIDTH, D_MODEL), l),
            _layer_spec((D_MODEL, D_MODEL), l),
            _const_spec((2, N_HEADS, ATTN_BLOCK, 2 * ATTN_BLOCK)),
        ],
        out_specs=[
            pl.BlockSpec((None, tm, D_MODEL), lambda i, j, s: (i, j, 0)),
            pl.BlockSpec((None, WINDOW, KV_WIDTH), lambda i, j, s: (i, 0, 0)),
            pl.BlockSpec((None, WINDOW, KV_WIDTH), lambda i, j, s: (i, 0, 0)),
            pl.BlockSpec((None, POOL_PAD, POOL_WIDTH), lambda i, j, s: (i, 0, 0)),
        ],
        scratch_shapes=[
            pltpu.VMEM((HIST + tm, POOL_WIDTH), F32),
            pltpu.VMEM((N_KV_HEADS, ATTN_BLOCK + tm, LANES), BF16),
            pltpu.VMEM((N_KV_HEADS, ATTN_BLOCK + tm, LANES), BF16),
            pltpu.VMEM((N_KV_HEADS, ATTN_BLOCK + tm, LANES), BF16),
            pltpu.VMEM((N_KV_HEADS, ATTN_BLOCK + tm, LANES), BF16),
            pltpu.VMEM((tm, ATTN_WIDTH), F32),
        ],
    )
    return pl.pallas_call(
        functools.partial(_pmix_kernel, layer=l),
        grid_spec=grid_spec,
        out_shape=[
            jax.ShapeDtypeStruct((b, t, D_MODEL), F32),
            jax.ShapeDtypeStruct((b, WINDOW, KV_WIDTH), F32),
            jax.ShapeDtypeStruct((b, WINDOW, KV_WIDTH), F32),
            jax.ShapeDtypeStruct((b, POOL_PAD, POOL_WIDTH), F32),
        ],
        compiler_params=pltpu.CompilerParams(
            dimension_semantics=("arbitrary", "arbitrary"),
            vmem_limit_bytes=VMEM_LIMIT),
        name="prompt_mixer",
    )(sinks, x, mod_p, g1, win, wpool, pscale, wa, wb, wout, bias)


def _ffn_kernel(xp_ref, xs_ref, modp_ref, mods_ref, g2_ref, wup_ref, wdn_ref, fg_ref,
                op_ref, os_ref, *, final, n_prompt, tiles_per_row, reps):
    i = pl.program_id(0)

    def ffn(x, sh2, sc2, gt2):
        h2 = _rms_mod(x, g2_ref[...], sc2, sh2)
        ff = _bdot(h2, wup_ref[...])
        ff = jnp.square(jnp.maximum(ff, 0.0))
        y = x + gt2 * _bdot(ff, wdn_ref[...])
        if final:
            ms = jnp.mean(y * y, axis=-1, keepdims=True)
            y = y * lax.rsqrt(ms + RMS_EPS) * fg_ref[...]
        return y

    @pl.when(i < n_prompt)
    def _prompt():
        mod = modp_ref[pl.ds(i // tiles_per_row, 1), :]
        op_ref[...] = ffn(xp_ref[...], mod[:, 0:D_MODEL], mod[:, D_MODEL:2 * D_MODEL],
                          mod[:, 2 * D_MODEL:3 * D_MODEL])

    @pl.when(i == n_prompt)
    def _sample():
        mod = jnp.concatenate([mods_ref[...]] * reps, axis=0)
        os_ref[...] = ffn(xs_ref[...], mod[:, 0:D_MODEL], mod[:, D_MODEL:2 * D_MODEL],
                          mod[:, 2 * D_MODEL:3 * D_MODEL])


def _ffn_call(xp, xs, mod_p, mod_s, g2, wup, wdn, fg, l, final):
    b, t, _ = xp.shape
    ms = xs.shape[0]
    nb = mod_s.shape[1]
    tm = FFN_ROWS
    tpr = t // tm
    n_prompt = b * tpr
    last = n_prompt - 1

    def p_idx(i):
        ii = jnp.minimum(i, last)
        return (ii // tpr, ii % tpr, 0)

    return pl.pallas_call(
        functools.partial(_ffn_kernel, final=final, n_prompt=n_prompt, tiles_per_row=tpr,
                          reps=ms // nb),
        grid=(n_prompt + 1,),
        in_specs=[
            pl.BlockSpec((None, tm, D_MODEL), p_idx),
            _const_spec((ms, D_MODEL)),
            pl.BlockSpec((None, b, MOD_SLAB), lambda i: (l, 0, 1), pipeline_mode=pl.Buffered(1)),
            pl.BlockSpec((None, nb, MOD_SLAB), lambda i: (l, 0, 1), pipeline_mode=pl.Buffered(1)),
            _layer_spec((1, D_MODEL), l),
            _layer_spec((D_MODEL, D_FF), l),
            _layer_spec((D_FF, D_MODEL), l),
            _const_spec((1, D_MODEL)),
        ],
        out_specs=[
            pl.BlockSpec((None, tm, D_MODEL), p_idx),
            _whole_out_spec((ms, D_MODEL)),
        ],
        out_shape=[jax.ShapeDtypeStruct((b, t, D_MODEL), F32),
                   jax.ShapeDtypeStruct((ms, D_MODEL), F32)],
        compiler_params=pltpu.CompilerParams(
            dimension_semantics=("arbitrary",),
            vmem_limit_bytes=VMEM_LIMIT),
        name="ffn_final" if final else "ffn",
    )(xp, xs, mod_p, mod_s, g2, wup, wdn, fg)


def _sproj_kernel(x_ref, mod_ref, g1_ref, win_ref, wpool_ref, pscale_ref, wa_ref, pre_ref,
                  q_ref, k_ref, v_ref, ma_ref, sgb_ref, npool_ref, *, nb, tn):
    x = x_ref[...]
    mod = jnp.concatenate([mod_ref[:, 0:2 * D_MODEL]] * tn, axis=0)
    sh1 = mod[:, 0:D_MODEL]
    sc1 = mod[:, D_MODEL:2 * D_MODEL]
    h = _rms_mod(x, g1_ref[...], sc1, sh1).astype(BF16)

    u = jnp.dot(h, win_ref[:, OFF_U:OFF_Q], preferred_element_type=F32)
    def up(r):
        if r < POOL_PAD:
            return pre_ref[r]
        r -= POOL_PAD
        return u[r * nb:(r + 1) * nb, :]

    d_rows = []
    for tt in range(tn):
        parts = []
        for g, w in enumerate(POOL_WINDOWS):
            lo = g * POOL_GROUP
            acc = up(POOL_PAD + tt)[:, lo:lo + POOL_GROUP]
            for s in range(1, w):
                acc = acc + up(POOL_PAD + tt - s)[:, lo:lo + POOL_GROUP]
            cnt = float(min(w, PAST_LEN + tt + 1))
            parts.append(acc / cnt - up(POOL_PAD + tt)[:, lo:lo + POOL_GROUP])
        d_rows.append(parts)
    y = jnp.concatenate(
        [_bdot(jnp.concatenate([d_rows[tt][g] for tt in range(tn)], axis=0), wpool_ref[g])
         for g in range(len(POOL_WINDOWS))], axis=1)
    y = y * pscale_ref[...]
    br_a = _bdot(y, wa_ref[...])
    for r in range(POOL_PAD):
        npool_ref[r] = up(r + tn)

    q = jnp.dot(h, win_ref[:, OFF_Q:OFF_K], preferred_element_type=F32)
    q_ref[...] = q * (HEAD_DIM ** -0.5)
    kv = jnp.dot(h, win_ref[:, OFF_K:OFF_GA], preferred_element_type=F32)
    k_ref[...] = kv[:, 0:KV_WIDTH]
    v_ref[...] = kv[:, KV_WIDTH:2 * KV_WIDTH]
    ga = jnp.dot(h, win_ref[:, OFF_GA:OFF_GB], preferred_element_type=F32)
    ma_ref[...] = jax.nn.sigmoid(ga) * br_a
    gb = jnp.dot(h, win_ref[:, OFF_GB:IN_WIDTH], preferred_element_type=F32)
    sgb_ref[...] = jax.nn.sigmoid(gb)


def _sproj_call(x, mod_s, g1, win, wpool, pscale, wa, prefix_t, l, nb, tn):
    m = x.shape[0]
    shapes = [
        jax.ShapeDtypeStruct((m, ATTN_WIDTH), F32),
        jax.ShapeDtypeStruct((m, KV_WIDTH), F32),
        jax.ShapeDtypeStruct((m, KV_WIDTH), F32),
        jax.ShapeDtypeStruct((m, D_MODEL), F32),
        jax.ShapeDtypeStruct((m, D_MODEL), F32),
        jax.ShapeDtypeStruct((POOL_PAD, nb, POOL_WIDTH), F32),
    ]
    return pl.pallas_call(
        functools.partial(_sproj_kernel, nb=nb, tn=tn),
        grid=(1,),
        in_specs=[
            _const_spec(x.shape),
            pl.BlockSpec((None, nb, MOD_SLAB), lambda i: (l, 0, 0), pipeline_mode=pl.Buffered(1)),
            _layer_spec((1, D_MODEL), l),
            _layer_spec((D_MODEL, IN_WIDTH), l),
            _layer_spec((len(POOL_WINDOWS), POOL_GROUP, POOL_GROUP), l),
            _layer_spec((1, POOL_WIDTH), l),
            _layer_spec((POOL_WIDTH, D_MODEL), l),
            _layer_spec((POOL_PAD, nb, POOL_WIDTH), l),
        ],
        out_specs=[_whole_out_spec(s.shape) for s in shapes],
        out_shape=shapes,
        compiler_params=pltpu.CompilerParams(
            dimension_semantics=("arbitrary",),
            vmem_limit_bytes=VMEM_LIMIT),
        name="sample_proj",
    )(x, mod_s, g1, win, wpool, pscale, wa, prefix_t)


def _sattn_kernel(sinks_ref, slopes_ref, q_ref, kn_ref, vn_ref, kc_ref, vc_ref, o_ref,
                  qt, knt, vnt, sbuf, ot, *, layer, nb, tn):
    hk = pl.program_id(0)
    nq = tn * GQA_GROUP
    srows = WINDOW + 8
    hoff = pl.multiple_of(hk * HEAD_DIM, HEAD_DIM)
    for tt in range(tn):
        qt[tt] = q_ref[tt * nb:(tt + 1) * nb, :].T
        knt[tt] = kn_ref[tt * nb:(tt + 1) * nb, :].T
        vnt[tt] = vn_ref[tt * nb:(tt + 1) * nb, :].T

    jb = 16
    for c in range(WINDOW // jb):
        def qk_body(dd, accs):
            k2 = kc_ref[dd, c * jb:(c + 1) * jb, :]
            new = []
            for tt in range(tn):
                for g in range(GQA_GROUP):
                    qrow = qt[tt, pl.ds(g * HEAD_DIM + dd, 1), :]
                    new.append(accs[tt * GQA_GROUP + g] + qrow * k2)
            return tuple(new)
        accs = lax.fori_loop(0, HEAD_DIM, qk_body,
                             tuple(jnp.zeros((jb, nb), F32) for _ in range(nq)))
        for i in range(nq):
            sbuf[i, c * jb:(c + 1) * jb, :] = accs[i]

    for tt in range(tn):
        for g in range(GQA_GROUP):
            qh = qt[tt, g * HEAD_DIM:(g + 1) * HEAD_DIM, :]
            rows = [jnp.sum(qh * knt[t2, pl.ds(hoff, HEAD_DIM), :], axis=0, keepdims=True)
                    for t2 in range(tn)]
            rows.append(jnp.zeros((8 - tn, nb), F32))
            sbuf[tt * GQA_GROUP + g, WINDOW:srows, :] = jnp.concatenate(rows, axis=0)

    jpos = lax.broadcasted_iota(jnp.int32, (srows, nb), 0)
    for tt in range(tn):
        dist = tt + WINDOW - jpos
        valid = (dist >= 0) & (dist < WINDOW)
        distf = dist.astype(F32)
        for g in range(GQA_GROUP):
            i = tt * GQA_GROUP + g
            head = hk * GQA_GROUP + g
            s = sbuf[i] - slopes_ref[head] * distf
            s = jnp.where(valid, s, NEG_INF)
            sink = sinks_ref[layer, head]
            m = jnp.maximum(jnp.max(s, axis=0, keepdims=True), sink)
            p = jnp.exp(s - m)
            den = jnp.sum(p, axis=0, keepdims=True) + jnp.exp(sink - m)
            sbuf[i] = p / den

    def pv_body(dd, carry):
        vd = vc_ref[dd]
        for tt in range(tn):
            for g in range(GQA_GROUP):
                i = tt * GQA_GROUP + g
                row = jnp.sum(sbuf[i, 0:WINDOW, :] * vd, axis=0, keepdims=True)
                ot[tt, pl.ds(g * HEAD_DIM + dd, 1), :] = row
        return carry
    lax.fori_loop(0, HEAD_DIM, pv_body, 0)

    for tt in range(tn):
        for g in range(GQA_GROUP):
            i = tt * GQA_GROUP + g
            acc = ot[tt, g * HEAD_DIM:(g + 1) * HEAD_DIM, :]
            for t2 in range(tt + 1):
                acc = acc + sbuf[i, WINDOW + t2:WINDOW + t2 + 1, :] * vnt[t2, pl.ds(hoff, HEAD_DIM), :]
            ot[tt, g * HEAD_DIM:(g + 1) * HEAD_DIM, :] = acc
        o_ref[tt * nb:(tt + 1) * nb, :] = ot[tt].T


def _sattn_call(q, kn, vn, kc_t, vc_t, sinks, slopes, l, nb, tn):
    m = q.shape[0]
    gw = GQA_GROUP * HEAD_DIM
    grid_spec = pltpu.PrefetchScalarGridSpec(
        num_scalar_prefetch=2,
        grid=(N_KV_HEADS,),
        in_specs=[
            pl.BlockSpec((m, gw), lambda h, a, b: (0, h)),
            _const_spec((m, KV_WIDTH)),
            _const_spec((m, KV_WIDTH)),
            pl.BlockSpec((None, HEAD_DIM, WINDOW, nb), lambda h, a, b: (l, h, 0, 0)),
            pl.BlockSpec((None, HEAD_DIM, WINDOW, nb), lambda h, a, b: (l, h, 0, 0)),
        ],
        out_specs=pl.BlockSpec((m, gw), lambda h, a, b: (0, h)),
        scratch_shapes=[
            pltpu.VMEM((tn, gw, nb), F32),
            pltpu.VMEM((tn, KV_WIDTH, nb), F32),
            pltpu.VMEM((tn, KV_WIDTH, nb), F32),
            pltpu.VMEM((tn * GQA_GROUP, WINDOW + 8, nb), F32),
            pltpu.VMEM((tn, gw, nb), F32),
        ],
    )
    return pl.pallas_call(
        functools.partial(_sattn_kernel, layer=l, nb=nb, tn=tn),
        grid_spec=grid_spec,
        out_shape=jax.ShapeDtypeStruct((m, ATTN_WIDTH), F32),
        compiler_params=pltpu.CompilerParams(
            dimension_semantics=("arbitrary",),
            vmem_limit_bytes=VMEM_LIMIT),
        name="sample_attn",
    )(sinks, slopes, q, kn, vn, kc_t, vc_t)


def _spost_kernel(x_ref, o_ref, ma_ref, sgb_ref, mod_ref, wb_ref, wout_ref, xo_ref, *, tn):
    gt1 = jnp.concatenate([mod_ref[...]] * tn, axis=0)
    br_b = _bdot(o_ref[...], wb_ref[...])
    merged = ma_ref[...] + sgb_ref[...] * br_b
    xo_ref[...] = x_ref[...] + gt1 * _bdot(merged, wout_ref[...])


def _spost_call(x, o, ma, sgb, mod_s, wb, wout, l, nb, tn):
    return pl.pallas_call(
        functools.partial(_spost_kernel, tn=tn),
        grid=(1,),
        in_specs=[
            _const_spec(x.shape), _const_spec(o.shape), _const_spec(ma.shape),
            _const_spec(sgb.shape),
            pl.BlockSpec((None, nb, D_MODEL), lambda i: (l, 0, 2), pipeline_mode=pl.Buffered(1)),
            _layer_spec((ATTN_WIDTH, D_MODEL), l),
            _layer_spec((D_MODEL, D_MODEL), l),
        ],
        out_specs=_whole_out_spec(x.shape),
        out_shape=jax.ShapeDtypeStruct(x.shape, F32),
        compiler_params=pltpu.CompilerParams(
            dimension_semantics=("arbitrary",),
            vmem_limit_bytes=VMEM_LIMIT),
        name="sample_post",
    )(x, o, ma, sgb, mod_s, wb, wout)


def _roll_kernel(ck_ref, cv_ref, kn_ref, vn_ref, ok_ref, ov_ref, *, tn):
    keep = WINDOW - tn
    ok_ref[:, 0:keep, :] = ck_ref[:, tn:WINDOW, :]
    ok_ref[:, keep:WINDOW, :] = kn_ref[...]
    ov_ref[:, 0:keep, :] = cv_ref[:, tn:WINDOW, :]
    ov_ref[:, keep:WINDOW, :] = vn_ref[...]


def _roll_call(ck, cv, kn, vn, tn):
    nl, nb = ck.shape[0], ck.shape[1]
    bb = 16
    cspec = pl.BlockSpec((None, bb, WINDOW, KV_WIDTH), lambda l, i: (l, i, 0, 0))
    nspec = pl.BlockSpec((None, bb, tn, KV_WIDTH), lambda l, i: (l, i, 0, 0))
    shp = jax.ShapeDtypeStruct(ck.shape, F32)
    return pl.pallas_call(
        functools.partial(_roll_kernel, tn=tn),
        grid=(nl, nb // bb),
        in_specs=[cspec, cspec, nspec, nspec],
        out_specs=[cspec, cspec],
        out_shape=[shp, shp],
        compiler_params=pltpu.CompilerParams(
            dimension_semantics=("arbitrary", "arbitrary"),
            vmem_limit_bytes=VMEM_LIMIT),
        name="cache_roll",
    )(ck, cv, kn, vn)


def kernel(x_prompt, x_sample, cache_k, cache_v, state_pool, c_prompt, c_sample,
           w_ada, b_ada, norm1_g, w_in, w_pool, pool_scale, attn_sinks, w_a, w_b,
           w_out, norm2_g, w_up, w_down, final_g):
    bp, tp, _ = x_prompt.shape
    nb, tn, _ = x_sample.shape
    assert tp % MIX_ROWS == 0 and tp % FFN_ROWS == 0 and tn <= 8 and nb == LANES
    assert bp % 8 == 0

    mod_p, mod_s = _mod_call(c_prompt, c_sample, w_ada, b_ada)

    bf = lambda w: w.astype(BF16)
    win, wpool, wa, wb, wout, wup, wdn = map(bf, (w_in, w_pool, w_a, w_b, w_out, w_up, w_down))
    bias = jnp.asarray(_prompt_bias())
    slopes = jnp.asarray(_alibi_slopes(), F32)
    fg = final_g.reshape(1, D_MODEL)
    g1 = norm1_g.reshape(DEPTH, 1, D_MODEL)
    g2 = norm2_g.reshape(DEPTH, 1, D_MODEL)
    psc = pool_scale.reshape(DEPTH, 1, POOL_WIDTH)

    xs = x_sample.transpose(1, 0, 2).reshape(tn * nb, D_MODEL)
    ck = cache_k.reshape(DEPTH, nb, WINDOW, KV_WIDTH)
    cv = cache_v.reshape(DEPTH, nb, WINDOW, KV_WIDTH)
    ck_t = ck.transpose(0, 3, 2, 1)
    cv_t = cv.transpose(0, 3, 2, 1)
    prefix_t = state_pool.transpose(0, 2, 1, 3)

    xp = x_prompt
    kp, vp, pp, kn_l, vn_l, ps = [], [], [], [], [], []
    for l in range(DEPTH):
        last = l == DEPTH - 1
        xp, nk, nv, npool = _pmix_call(xp, mod_p, g1, win, wpool, psc, wa, wb, wout,
                                       attn_sinks, bias, l)
        kp.append(nk); vp.append(nv); pp.append(npool)

        q, kn, vn, ma, sgb, npool_s = _sproj_call(xs, mod_s, g1, win, wpool, psc, wa,
                                                  prefix_t, l, nb, tn)
        o = _sattn_call(q, kn, vn, ck_t, cv_t, attn_sinks, slopes, l, nb, tn)
        xs = _spost_call(xs, o, ma, sgb, mod_s, wb, wout, l, nb, tn)
        kn_l.append(kn); vn_l.append(vn); ps.append(npool_s)

        xp, xs = _ffn_call(xp, xs, mod_p, mod_s, g2, wup, wdn, fg, l, last)

    to_bt = lambda a: a.reshape(tn, nb, a.shape[-1]).transpose(1, 0, 2)
    nks, nvs = _roll_call(ck, cv, jnp.stack([to_bt(a) for a in kn_l]),
                          jnp.stack([to_bt(a) for a in vn_l]), tn)

    kv_shape_p = (DEPTH, bp, WINDOW, N_KV_HEADS, HEAD_DIM)
    kv_shape_s = (DEPTH, nb, WINDOW, N_KV_HEADS, HEAD_DIM)
    return (xp,
            to_bt(xs),
            jnp.stack(kp).reshape(kv_shape_p),
            jnp.stack(vp).reshape(kv_shape_p),
            jnp.stack(pp),
            nks.reshape(kv_shape_s),
            nvs.reshape(kv_shape_s),
            jnp.stack(ps).transpose(0, 2, 1, 3))
```

```python
import functools

import numpy as np
import jax
import jax.numpy as jnp
from jax import lax
from jax.experimental import pallas as pl
from jax.experimental.pallas import tpu as pltpu

D_MODEL = 1024
DEPTH = 2
PAST_LEN = 16384
POOL_WIDTH = D_MODEL // 2
POOL_WINDOWS = (2, 4, 8, 16)
POOL_GROUP = POOL_WIDTH // len(POOL_WINDOWS)
POOL_PAD = max(POOL_WINDOWS) - 1
N_HEADS = 8
N_KV_HEADS = 2
HEAD_DIM = 64
GQA_GROUP = N_HEADS // N_KV_HEADS
ATTN_WIDTH = N_HEADS * HEAD_DIM
KV_WIDTH = N_KV_HEADS * HEAD_DIM
WINDOW = 128
ATTN_BLOCK = 128
D_FF = 4 * D_MODEL
RMS_EPS = 1e-6
NEG_INF = -1e30

OFF_U = 0
OFF_Q = OFF_U + POOL_WIDTH
OFF_K = OFF_Q + ATTN_WIDTH
OFF_V = OFF_K + KV_WIDTH
OFF_GA = OFF_V + KV_WIDTH
OFF_GB = OFF_GA + D_MODEL
IN_WIDTH = OFF_GB + D_MODEL
MOD_SLAB = 3 * D_MODEL

LANES = 128
HIST = 16
assert all(w & (w - 1) == 0 for w in POOL_WINDOWS) and list(POOL_WINDOWS) == sorted(POOL_WINDOWS)
assert HIST >= POOL_PAD
VMEM_LIMIT = 56 * 1024 * 1024
SUB_ROWS = 256
MIX_ROWS = 2 * SUB_ROWS
FFN_ROWS = 512
SATTN_KEYS = 16

F32 = jnp.float32
BF16 = jnp.bfloat16


def _bdot(a, b):
    return jnp.dot(a.astype(BF16), b.astype(BF16), preferred_element_type=F32)


def _rms_mod(x, g, sc, sh):
    ms = jnp.mean(x * x, axis=-1, keepdims=True)
    return (x * lax.rsqrt(ms + RMS_EPS) * g) * (1.0 + sc) + sh


def _alibi_slopes():
    return 2.0 ** (-8.0 * (np.arange(N_HEADS) + 1) / N_HEADS)


def _const_spec(shape):
    nd = len(shape)
    return pl.BlockSpec(shape, lambda *_: (0,) * nd, pipeline_mode=pl.Buffered(1))


def _layer_spec(shape, l):
    nd = len(shape)
    return pl.BlockSpec((None,) + tuple(shape), lambda *_: (l,) + (0,) * nd,
                        pipeline_mode=pl.Buffered(1))


def _whole_out_spec(shape):
    nd = len(shape)
    return pl.BlockSpec(shape, lambda *_: (0,) * nd)


def _mod_kernel(cp_ref, cs_ref, w_ref, b_ref, op_ref, os_ref):
    w = w_ref[...].astype(BF16)
    b = b_ref[...]
    for c_ref, o_ref in ((cp_ref, op_ref), (cs_ref, os_ref)):
        c = c_ref[...]
        s = (c * jax.nn.sigmoid(c)).astype(BF16)
        o_ref[...] = jnp.dot(s, w, preferred_element_type=F32) + b


def _mod_call(c_p, c_s, w_ada, b_ada):
    tn = 1024
    n = 6 * D_MODEL
    mp, ms = c_p.shape[0], c_s.shape[0]
    return pl.pallas_call(
        _mod_kernel,
        grid=(DEPTH, n // tn),
        in_specs=[
            _const_spec((mp, D_MODEL)),
            _const_spec((ms, D_MODEL)),
            pl.BlockSpec((None, D_MODEL, tn), lambda l, j: (l, 0, j)),
            pl.BlockSpec((None, 1, tn), lambda l, j: (l, 0, j)),
        ],
        out_specs=[
            pl.BlockSpec((None, mp, tn), lambda l, j: (l, 0, j)),
            pl.BlockSpec((None, ms, tn), lambda l, j: (l, 0, j)),
        ],
        out_shape=[jax.ShapeDtypeStruct((DEPTH, mp, n), F32),
                   jax.ShapeDtypeStruct((DEPTH, ms, n), F32)],
        compiler_params=pltpu.CompilerParams(
            dimension_semantics=("arbitrary", "arbitrary"),
            vmem_limit_bytes=VMEM_LIMIT),
        name="adaln_mod",
    )(c_p, c_s, w_ada, b_ada.reshape(DEPTH, 1, n))


def _prompt_bias():
    i = np.arange(ATTN_BLOCK)[:, None]
    j = np.arange(2 * ATTN_BLOCK)[None, :]
    dist = i + ATTN_BLOCK - j
    valid = (dist >= 0) & (dist < WINDOW)
    valid_first = valid & (j >= ATTN_BLOCK)
    sl = _alibi_slopes()[:, None, None]
    b = np.where(valid[None], -sl * dist[None], NEG_INF)
    b0 = np.where(valid_first[None], -sl * dist[None], NEG_INF)
    return np.stack([b, b0]).astype(np.float32)


def _pmix_kernel(sinks_ref, x_ref, mod_ref, g1_ref, win_ref, wpool_ref, pscale_ref,
                 wa_ref, wb_ref, wout_ref, bias_ref,
                 xo_ref, nk_ref, nv_ref, npool_ref,
                 ubuf, ka, kb, va, vb, obuf, *, layer):
    tm, sub = MIX_ROWS, SUB_ROWS
    bi = pl.program_id(0)
    t = pl.program_id(1)
    nt = pl.num_programs(1)

    @pl.when(t == 0)
    def _init():
        ubuf[0:HIST, :] = jnp.zeros((HIST, POOL_WIDTH), F32)
        zero = jnp.zeros((N_KV_HEADS, ATTN_BLOCK, LANES), BF16)
        ka[:, 0:ATTN_BLOCK, :] = zero
        kb[:, 0:ATTN_BLOCK, :] = zero
        va[:, 0:ATTN_BLOCK, :] = zero
        vb[:, 0:ATTN_BLOCK, :] = zero

    mod = mod_ref[pl.ds(bi, 1), :]
    sh1 = mod[:, 0:D_MODEL]
    sc1 = mod[:, D_MODEL:2 * D_MODEL]
    gt1 = mod[:, 2 * D_MODEL:3 * D_MODEL]
    lane = lax.broadcasted_iota(jnp.int32, (sub, LANES), 1)
    low = lane < HEAD_DIM
    low_q = lax.broadcasted_iota(jnp.int32, (ATTN_BLOCK, LANES), 1) < HEAD_DIM
    nt_dims = (((1,), (1,)), ((), ()))
    st = [dict(ro=i * sub) for i in range(tm // sub)]

    def norm(c):
        c["x"] = x_ref[c["ro"]:c["ro"] + sub, :]
        c["h"] = _rms_mod(c["x"], g1_ref[...], sc1, sh1).astype(BF16)

    def proj(c):
        h = c["h"]
        c["u"] = jnp.dot(h, win_ref[:, OFF_U:OFF_Q], preferred_element_type=F32)
        q = jnp.dot(h, win_ref[:, OFF_Q:OFF_K], preferred_element_type=F32)
        c["qb"] = (q * (HEAD_DIM ** -0.5)).astype(BF16)
        kv = jnp.dot(h, win_ref[:, OFF_K:OFF_GA], preferred_element_type=F32)
        c["k"] = kv[:, 0:KV_WIDTH]
        c["v"] = kv[:, KV_WIDTH:2 * KV_WIDTH]

    def pool_sums(c):
        ro, u = c["ro"], c["u"]
        ubuf[HIST + ro:HIST + ro + sub, :] = u
        pos = t * tm + ro + lax.broadcasted_iota(jnp.int32, (sub, 1), 0)
        cur = ubuf[ro:ro + HIST + sub, :]
        d, w = [], 1
        for g, wg in enumerate(POOL_WINDOWS):
            while w < wg:
                cur = cur + pltpu.roll(cur, w, axis=0)
                w *= 2
            ug = u[:, g * POOL_GROUP:(g + 1) * POOL_GROUP]
            cnt = jnp.minimum(wg, pos + 1).astype(F32)
            d.append((cur[HIST:, 0:POOL_GROUP] / cnt - ug).astype(BF16))
            if g + 1 < len(POOL_WINDOWS):
                cur = cur[:, POOL_GROUP:]
        c["d"] = d

    def kv_store(c):
        r = ATTN_BLOCK + c["ro"]
        k, v = c["k"], c["v"]
        kr = pltpu.roll(k, HEAD_DIM, axis=1)
        vr = pltpu.roll(v, HEAD_DIM, axis=1)
        ka[0, r:r + sub, :] = jnp.where(low, k, 0.0).astype(BF16)
        kb[0, r:r + sub, :] = jnp.where(low, 0.0, kr).astype(BF16)
        ka[1, r:r + sub, :] = jnp.where(low, kr, 0.0).astype(BF16)
        kb[1, r:r + sub, :] = jnp.where(low, 0.0, k).astype(BF16)
        va[0, r:r + sub, :] = jnp.where(low, v, 0.0).astype(BF16)
        vb[0, r:r + sub, :] = jnp.where(low, 0.0, vr).astype(BF16)
        va[1, r:r + sub, :] = jnp.where(low, vr, 0.0).astype(BF16)
        vb[1, r:r + sub, :] = jnp.where(low, 0.0, v).astype(BF16)

    def pool_proj(c):
        y = jnp.concatenate(
            [jnp.dot(c["d"][g], wpool_ref[g], preferred_element_type=F32)
             for g in range(len(POOL_WINDOWS))], axis=1)
        c["br_a"] = _bdot(y * pscale_ref[...], wa_ref[...])

    def units(c):
        j0 = c["ro"] // ATTN_BLOCK
        return [(j, hk, pr) for j in range(j0, j0 + sub // ATTN_BLOCK)
                for hk in range(N_KV_HEADS) for pr in range(GQA_GROUP // 2)]

    def scores(c):
        sc = {}
        for (j, hk, pr) in units(c):
            r0 = j * ATTN_BLOCK
            c0 = (hk * (GQA_GROUP // 2) + pr) * LANES
            q2 = c["qb"][r0 - c["ro"]:r0 - c["ro"] + ATTN_BLOCK, c0:c0 + LANES]
            ks = (ka[hk, r0:r0 + 2 * ATTN_BLOCK, :], kb[hk, r0:r0 + 2 * ATTN_BLOCK, :])
            for e in range(2):
                sc[(j, hk, pr, e)] = lax.dot_general(q2, ks[e], nt_dims,
                                                     preferred_element_type=F32)
        c["s"] = sc

    def gate_a(c):
        c["ga"] = jnp.dot(c["h"], win_ref[:, OFF_GA:OFF_GB], preferred_element_type=F32)

    def gate_b(c):
        c["gb"] = jnp.dot(c["h"], win_ref[:, OFF_GB:IN_WIDTH], preferred_element_type=F32)

    def softmax(c):
        probs, dens = {}, {}
        for (j, hk, pr) in units(c):
            for e in range(2):
                head = hk * GQA_GROUP + pr * 2 + e
                if j == 0:
                    bias = jnp.where(t == 0, bias_ref[1, head], bias_ref[0, head])
                else:
                    bias = bias_ref[0, head]
                s = c["s"][(j, hk, pr, e)] + bias
                sink = sinks_ref[layer, head]
                m = jnp.maximum(jnp.max(s, axis=-1, keepdims=True), sink)
                p = jnp.exp(s - m)
                dens[(j, hk, pr, e)] = jnp.sum(p, axis=-1, keepdims=True) + jnp.exp(sink - m)
                probs[(j, hk, pr, e)] = p.astype(BF16)
        c["p"], c["den"] = probs, dens

    def values(c):
        for (j, hk, pr) in units(c):
            r0 = j * ATTN_BLOCK
            c0 = (hk * (GQA_GROUP // 2) + pr) * LANES
            vs = (va[hk, r0:r0 + 2 * ATTN_BLOCK, :], vb[hk, r0:r0 + 2 * ATTN_BLOCK, :])
            o2 = (jnp.dot(c["p"][(j, hk, pr, 0)], vs[0], preferred_element_type=F32)
                  + jnp.dot(c["p"][(j, hk, pr, 1)], vs[1], preferred_element_type=F32))
            den = jnp.where(low_q, c["den"][(j, hk, pr, 0)], c["den"][(j, hk, pr, 1)])
            obuf[r0:r0 + ATTN_BLOCK, c0:c0 + LANES] = o2 / den

    def tail(c):
        ro = c["ro"]
        br_b = _bdot(obuf[ro:ro + sub, :], wb_ref[...])
        merged = jax.nn.sigmoid(c["ga"]) * c["br_a"] + jax.nn.sigmoid(c["gb"]) * br_b
        xo_ref[ro:ro + sub, :] = c["x"] + gt1 * _bdot(merged, wout_ref[...])

    stages = [
        (norm,),
        (proj,),
        (pool_sums, kv_store),
        (pool_proj, scores, gate_a),
        (softmax,),
        (values, gate_b),
        (tail,),
    ]
    order = sorted((2 * s + 3 * i, i, s) for i in range(len(st)) for s in range(len(stages)))
    for _, i, s in order:
        for fn in stages[s]:
            fn(st[i])

    @pl.when(t == nt - 1)
    def _state():
        nk_ref[...] = st[-1]["k"][sub - WINDOW:, :]
        nv_ref[...] = st[-1]["v"][sub - WINDOW:, :]
        npool_ref[...] = ubuf[HIST + tm - POOL_PAD:HIST + tm, :]

    ubuf[0:HIST, :] = ubuf[tm:tm + HIST, :]
    for buf in (ka, kb, va, vb):
        buf[:, 0:ATTN_BLOCK, :] = buf[:, tm:tm + ATTN_BLOCK, :]


def _pmix_call(x, mod_p, g1, win, wpool, pscale, wa, wb, wout, sinks, bias, l):
    b, t, _ = x.shape
    tm = MIX_ROWS
    grid_spec = pltpu.PrefetchScalarGridSpec(
        num_scalar_prefetch=1,
        grid=(b, t // tm),
        in_specs=[
            pl.BlockSpec((None, tm, D_MODEL), lambda i, j, s: (i, j, 0)),
            pl.BlockSpec((None, b, MOD_SLAB), lambda i, j, s: (l, 0, 0),
                         pipeline_mode=pl.Buffered(1)),
            _layer_spec((1, D_MODEL), l),
            _layer_spec((D_MODEL, IN_WIDTH), l),
            _layer_spec((len(POOL_WINDOWS), POOL_GROUP, POOL_GROUP), l),
            _layer_spec((1, POOL_WIDTH), l),
            _layer_spec((POOL_WIDTH, D_MODEL), l),
            _layer_spec((ATTN_WIDTH, D_MODEL), l),
            _layer_spec((D_MODEL, D_MODEL), l),
            _const_spec((2, N_HEADS, ATTN_BLOCK, 2 * ATTN_BLOCK)),
        ],
        out_specs=[
            pl.BlockSpec((None, tm, D_MODEL), lambda i, j, s: (i, j, 0)),
            pl.BlockSpec((None, WINDOW, KV_WIDTH), lambda i, j, s: (i, 0, 0)),
            pl.BlockSpec((None, WINDOW, KV_WIDTH), lambda i, j, s: (i, 0, 0)),
            pl.BlockSpec((None, POOL_PAD, POOL_WIDTH), lambda i, j, s: (i, 0, 0)),
        ],
        scratch_shapes=[
            pltpu.VMEM((HIST + tm, POOL_WIDTH), F32),
            pltpu.VMEM((N_KV_HEADS, ATTN_BLOCK + tm, LANES), BF16),
            pltpu.VMEM((N_KV_HEADS, ATTN_BLOCK + tm, LANES), BF16),
            pltpu.VMEM((N_KV_HEADS, ATTN_BLOCK + tm, LANES), BF16),
            pltpu.VMEM((N_KV_HEADS, ATTN_BLOCK + tm, LANES), BF16),
            pltpu.VMEM((tm, ATTN_WIDTH), F32),
        ],
    )
    return pl.pallas_call(
        functools.partial(_pmix_kernel, layer=l),
        grid_spec=grid_spec,
        out_shape=[
            jax.ShapeDtypeStruct((b, t, D_MODEL), F32),
            jax.ShapeDtypeStruct((b, WINDOW, KV_WIDTH), F32),
            jax.ShapeDtypeStruct((b, WINDOW, KV_WIDTH), F32),
            jax.ShapeDtypeStruct((b, POOL_PAD, POOL_WIDTH), F32),
        ],
        compiler_params=pltpu.CompilerParams(
            dimension_semantics=("arbitrary", "arbitrary"),
            vmem_limit_bytes=VMEM_LIMIT),
        name="prompt_mixer",
    )(sinks, x, mod_p, g1, win, wpool, pscale, wa, wb, wout, bias)


def _ffn_kernel(xp_ref, xs_ref, modp_ref, mods_ref, g2_ref, wup_ref, wdn_ref, fg_ref,
                op_ref, os_ref, *, final, n_prompt, tiles_per_row, reps):
    i = pl.program_id(0)

    def ffn(x, sh2, sc2, gt2):
        h2 = _rms_mod(x, g2_ref[...], sc2, sh2)
        ff = _bdot(h2, wup_ref[...])
        ff = jnp.square(jnp.maximum(ff, 0.0))
        y = x + gt2 * _bdot(ff, wdn_ref[...])
        if final:
            ms = jnp.mean(y * y, axis=-1, keepdims=True)
            y = y * lax.rsqrt(ms + RMS_EPS) * fg_ref[...]
        return y

    @pl.when(i < n_prompt)
    def _prompt():
        mod = modp_ref[pl.ds(i // tiles_per_row, 1), :]
        op_ref[...] = ffn(xp_ref[...], mod[:, 0:D_MODEL], mod[:, D_MODEL:2 * D_MODEL],
                          mod[:, 2 * D_MODEL:3 * D_MODEL])

    @pl.when(i == n_prompt)
    def _sample():
        mod = jnp.concatenate([mods_ref[...]] * reps, axis=0)
        os_ref[...] = ffn(xs_ref[...], mod[:, 0:D_MODEL], mod[:, D_MODEL:2 * D_MODEL],
                          mod[:, 2 * D_MODEL:3 * D_MODEL])


def _ffn_call(xp, xs, mod_p, mod_s, g2, wup, wdn, fg, l, final):
    b, t, _ = xp.shape
    ms = xs.shape[0]
    nb = mod_s.shape[1]
    tm = FFN_ROWS
    tpr = t // tm
    n_prompt = b * tpr
    last = n_prompt - 1

    def p_idx(i):
        ii = jnp.minimum(i, last)
        return (ii // tpr, ii % tpr, 0)

    return pl.pallas_call(
        functools.partial(_ffn_kernel, final=final, n_prompt=n_prompt, tiles_per_row=tpr,
                          reps=ms // nb),
        grid=(n_prompt + 1,),
        in_specs=[
            pl.BlockSpec((None, tm, D_MODEL), p_idx),
            _const_spec((ms, D_MODEL)),
            pl.BlockSpec((None, b, MOD_SLAB), lambda i: (l, 0, 1), pipeline_mode=pl.Buffered(1)),
            pl.BlockSpec((None, nb, MOD_SLAB), lambda i: (l, 0, 1), pipeline_mode=pl.Buffered(1)),
            _layer_spec((1, D_MODEL), l),
            _layer_spec((D_MODEL, D_FF), l),
            _layer_spec((D_FF, D_MODEL), l),
            _const_spec((1, D_MODEL)),
        ],
        out_specs=[
            pl.BlockSpec((None, tm, D_MODEL), p_idx),
            _whole_out_spec((ms, D_MODEL)),
        ],
        out_shape=[jax.ShapeDtypeStruct((b, t, D_MODEL), F32),
                   jax.ShapeDtypeStruct((ms, D_MODEL), F32)],
        compiler_params=pltpu.CompilerParams(
            dimension_semantics=("arbitrary",),
            vmem_limit_bytes=VMEM_LIMIT),
        name="ffn_final" if final else "ffn",
    )(xp, xs, mod_p, mod_s, g2, wup, wdn, fg)


def _sproj_kernel(x_ref, mod_ref, g1_ref, win_ref, wpool_ref, pscale_ref, wa_ref, pre_ref,
                  q_ref, k_ref, v_ref, kb_ref, vb_ref, ma_ref, sgb_ref, npool_ref, *, nb, tn):
    x = x_ref[...]
    mod = jnp.concatenate([mod_ref[:, 0:2 * D_MODEL]] * tn, axis=0)
    sh1 = mod[:, 0:D_MODEL]
    sc1 = mod[:, D_MODEL:2 * D_MODEL]
    h = _rms_mod(x, g1_ref[...], sc1, sh1).astype(BF16)

    u = jnp.dot(h, win_ref[:, OFF_U:OFF_Q], preferred_element_type=F32)
    def up(r):
        if r < POOL_PAD:
            return pre_ref[r]
        r -= POOL_PAD
        return u[r * nb:(r + 1) * nb, :]

    d_rows = []
    for tt in range(tn):
        parts = []
        for g, w in enumerate(POOL_WINDOWS):
            lo = g * POOL_GROUP
            acc = up(POOL_PAD + tt)[:, lo:lo + POOL_GROUP]
            for s in range(1, w):
                acc = acc + up(POOL_PAD + tt - s)[:, lo:lo + POOL_GROUP]
            cnt = float(min(w, PAST_LEN + tt + 1))
            parts.append(acc / cnt - up(POOL_PAD + tt)[:, lo:lo + POOL_GROUP])
        d_rows.append(parts)
    y = jnp.concatenate(
        [_bdot(jnp.concatenate([d_rows[tt][g] for tt in range(tn)], axis=0), wpool_ref[g])
         for g in range(len(POOL_WINDOWS))], axis=1)
    y = y * pscale_ref[...]
    br_a = _bdot(y, wa_ref[...])
    for r in range(POOL_PAD):
        npool_ref[r] = up(r + tn)

    q = jnp.dot(h, win_ref[:, OFF_Q:OFF_K], preferred_element_type=F32)
    q_ref[...] = q * (HEAD_DIM ** -0.5)
    kv = jnp.dot(h, win_ref[:, OFF_K:OFF_GA], preferred_element_type=F32)
    k_ref[...] = kv[:, 0:KV_WIDTH]
    v_ref[...] = kv[:, KV_WIDTH:2 * KV_WIDTH]
    for tt in range(tn):
        rows = kv[tt * nb:(tt + 1) * nb, :]
        kb_ref[:, tt * KV_WIDTH:(tt + 1) * KV_WIDTH] = rows[:, 0:KV_WIDTH]
        vb_ref[:, tt * KV_WIDTH:(tt + 1) * KV_WIDTH] = rows[:, KV_WIDTH:2 * KV_WIDTH]
    ga = jnp.dot(h, win_ref[:, OFF_GA:OFF_GB], preferred_element_type=F32)
    ma_ref[...] = jax.nn.sigmoid(ga) * br_a
    gb = jnp.dot(h, win_ref[:, OFF_GB:IN_WIDTH], preferred_element_type=F32)
    sgb_ref[...] = jax.nn.sigmoid(gb)


def _sproj_call(x, mod_s, g1, win, wpool, pscale, wa, prefix_t, l, nb, tn):
    m = x.shape[0]
    shapes = [
        jax.ShapeDtypeStruct((m, ATTN_WIDTH), F32),
        jax.ShapeDtypeStruct((m, KV_WIDTH), F32),
        jax.ShapeDtypeStruct((m, KV_WIDTH), F32),
        jax.ShapeDtypeStruct((nb, tn * KV_WIDTH), F32),
        jax.ShapeDtypeStruct((nb, tn * KV_WIDTH), F32),
        jax.ShapeDtypeStruct((m, D_MODEL), F32),
        jax.ShapeDtypeStruct((m, D_MODEL), F32),
        jax.ShapeDtypeStruct((POOL_PAD, nb, POOL_WIDTH), F32),
    ]
    return pl.pallas_call(
        functools.partial(_sproj_kernel, nb=nb, tn=tn),
        grid=(1,),
        in_specs=[
            _const_spec(x.shape),
            pl.BlockSpec((None, nb, MOD_SLAB), lambda i: (l, 0, 0), pipeline_mode=pl.Buffered(1)),
            _layer_spec((1, D_MODEL), l),
            _layer_spec((D_MODEL, IN_WIDTH), l),
            _layer_spec((len(POOL_WINDOWS), POOL_GROUP, POOL_GROUP), l),
            _layer_spec((1, POOL_WIDTH), l),
            _layer_spec((POOL_WIDTH, D_MODEL), l),
            _layer_spec((POOL_PAD, nb, POOL_WIDTH), l),
        ],
        out_specs=[_whole_out_spec(s.shape) for s in shapes],
        out_shape=shapes,
        compiler_params=pltpu.CompilerParams(
            dimension_semantics=("arbitrary",),
            vmem_limit_bytes=VMEM_LIMIT),
        name="sample_proj",
    )(x, mod_s, g1, win, wpool, pscale, wa, prefix_t)


def _sattn_kernel(sinks_ref, slopes_ref, q_ref, kn_ref, vn_ref, kc_ref, vc_ref, *rest,
                  layer, nb, tn, ch, nch, n_roll):
    n_roll_in = 2 + 2 * n_roll if n_roll else 0
    roll_in, o_ref, rest = rest[:n_roll_in], rest[n_roll_in], rest[n_roll_in + 1:]
    if n_roll:
        nk_hbm, nv_hbm, qt, knt, vnt, tbuf, sbuf, ot, sems = rest
    else:
        qt, knt, vnt, tbuf, sbuf, ot = rest
    i = pl.program_id(0)
    srows = WINDOW + 8
    combos = [(tt, head) for tt in range(tn) for head in range(N_HEADS)]

    def roll_copies():
        keep = (WINDOW - tn) * KV_WIDTH
        cps = []
        for l in range(n_roll):
            ck_hbm, cv_hbm = roll_in[0], roll_in[1]
            knb_hbm, vnb_hbm = roll_in[2 + 2 * l], roll_in[3 + 2 * l]
            cps += [
                pltpu.make_async_copy(ck_hbm.at[l, :, pl.ds(tn * KV_WIDTH, keep)],
                                      nk_hbm.at[l, :, pl.ds(0, keep)], sems.at[4 * l]),
                pltpu.make_async_copy(cv_hbm.at[l, :, pl.ds(tn * KV_WIDTH, keep)],
                                      nv_hbm.at[l, :, pl.ds(0, keep)], sems.at[4 * l + 1]),
                pltpu.make_async_copy(knb_hbm, nk_hbm.at[l, :, pl.ds(keep, tn * KV_WIDTH)],
                                      sems.at[4 * l + 2]),
                pltpu.make_async_copy(vnb_hbm, nv_hbm.at[l, :, pl.ds(keep, tn * KV_WIDTH)],
                                      sems.at[4 * l + 3]),
            ]
        return cps

    @pl.when(i == 0)
    def _start():
        for cp in roll_copies():
            cp.start()
        for tt in range(tn):
            qt[tt] = q_ref[tt * nb:(tt + 1) * nb, :].T
            knt[tt] = kn_ref[tt * nb:(tt + 1) * nb, :].T
            vnt[tt] = vn_ref[tt * nb:(tt + 1) * nb, :].T

    def load_tiles(c_ref):
        for jj in range(ch):
            tbuf[jj] = c_ref[:, jj * KV_WIDTH:(jj + 1) * KV_WIDTH].T

    def head_rows(head):
        hk = head // GQA_GROUP
        return slice(hk * HEAD_DIM, (hk + 1) * HEAD_DIM)

    @pl.when(i < nch)
    def _scores():
        load_tiles(kc_ref)
        row0 = pl.multiple_of(i * ch, ch)
        for ci, (tt, head) in enumerate(combos):
            qh = qt[tt, head * HEAD_DIM:(head + 1) * HEAD_DIM, :]
            rows = [jnp.sum(qh * tbuf[jj, head_rows(head), :], axis=0, keepdims=True)
                    for jj in range(ch)]
            sbuf[ci, pl.ds(row0, ch), :] = jnp.concatenate(rows, axis=0)

    @pl.when(i == nch - 1)
    def _softmax():
        for ci, (tt, head) in enumerate(combos):
            qh = qt[tt, head * HEAD_DIM:(head + 1) * HEAD_DIM, :]
            rows = [jnp.sum(qh * knt[t2, head_rows(head), :], axis=0, keepdims=True)
                    for t2 in range(tn)]
            rows.append(jnp.zeros((8 - tn, nb), F32))
            sbuf[ci, WINDOW:srows, :] = jnp.concatenate(rows, axis=0)
        jpos = lax.broadcasted_iota(jnp.int32, (srows, nb), 0)
        for ci, (tt, head) in enumerate(combos):
            dist = tt + WINDOW - jpos
            valid = (dist >= 0) & (dist < WINDOW)
            s = sbuf[ci] - slopes_ref[head] * dist.astype(F32)
            s = jnp.where(valid, s, NEG_INF)
            sink = sinks_ref[layer, head]
            m = jnp.maximum(jnp.max(s, axis=0, keepdims=True), sink)
            p = jnp.exp(s - m)
            den = jnp.sum(p, axis=0, keepdims=True) + jnp.exp(sink - m)
            p = p / den
            sbuf[ci] = p
            acc = jnp.zeros((HEAD_DIM, nb), F32)
            for t2 in range(tt + 1):
                acc = acc + p[WINDOW + t2:WINDOW + t2 + 1, :] * vnt[t2, head_rows(head), :]
            ot[tt, head * HEAD_DIM:(head + 1) * HEAD_DIM, :] = acc

    @pl.when(i >= nch)
    def _values():
        load_tiles(vc_ref)
        row0 = pl.multiple_of((i - nch) * ch, ch)
        for ci, (tt, head) in enumerate(combos):
            acc = ot[tt, head * HEAD_DIM:(head + 1) * HEAD_DIM, :]
            for jj in range(ch):
                acc = acc + sbuf[ci, pl.ds(row0 + jj, 1), :] * tbuf[jj, head_rows(head), :]
            ot[tt, head * HEAD_DIM:(head + 1) * HEAD_DIM, :] = acc

    @pl.when(i == 2 * nch - 1)
    def _finish():
        for tt in range(tn):
            o_ref[tt * nb:(tt + 1) * nb, :] = ot[tt].T
        for cp in roll_copies():
            cp.wait()


def _sattn_call(q, kn, vn, ck, cv, sinks, slopes, new_rows, l, nb, tn):
    m = q.shape[0]
    ch = SATTN_KEYS
    nch = WINDOW // ch
    any_spec = pl.BlockSpec(memory_space=pl.ANY)
    n_roll = len(new_rows) if new_rows else 0
    in_specs = [
        _const_spec((m, ATTN_WIDTH)),
        _const_spec((m, KV_WIDTH)),
        _const_spec((m, KV_WIDTH)),
        pl.BlockSpec((None, nb, ch * KV_WIDTH), lambda i, a, b: (l, 0, jnp.minimum(i, nch - 1))),
        pl.BlockSpec((None, nb, ch * KV_WIDTH), lambda i, a, b: (l, 0, jnp.maximum(i - nch, 0))),
    ]
    args = [sinks, slopes, q, kn, vn, ck, cv]
    out_specs = [_whole_out_spec((m, ATTN_WIDTH))]
    out_shape = [jax.ShapeDtypeStruct((m, ATTN_WIDTH), F32)]
    scratch = [
        pltpu.VMEM((tn, ATTN_WIDTH, nb), F32),
        pltpu.VMEM((tn, KV_WIDTH, nb), F32),
        pltpu.VMEM((tn, KV_WIDTH, nb), F32),
        pltpu.VMEM((ch, KV_WIDTH, nb), F32),
        pltpu.VMEM((tn * N_HEADS, WINDOW + 8, nb), F32),
        pltpu.VMEM((tn, ATTN_WIDTH, nb), F32),
    ]
    if n_roll:
        args += [ck, cv] + [a for pair in new_rows for a in pair]
        in_specs += [any_spec] * (2 + 2 * n_roll)
        out_specs += [any_spec, any_spec]
        out_shape += [jax.ShapeDtypeStruct(ck.shape, F32), jax.ShapeDtypeStruct(cv.shape, F32)]
        scratch.append(pltpu.SemaphoreType.DMA((4 * n_roll,)))
    grid_spec = pltpu.PrefetchScalarGridSpec(
        num_scalar_prefetch=2,
        grid=(2 * nch,),
        in_specs=in_specs,
        out_specs=out_specs,
        scratch_shapes=scratch,
    )
    return pl.pallas_call(
        functools.partial(_sattn_kernel, layer=l, nb=nb, tn=tn, ch=ch, nch=nch, n_roll=n_roll),
        grid_spec=grid_spec,
        out_shape=out_shape,
        compiler_params=pltpu.CompilerParams(
            dimension_semantics=("arbitrary",),
            vmem_limit_bytes=VMEM_LIMIT),
        name="sample_attn_roll" if n_roll else "sample_attn",
    )(*args)


def _spost_kernel(x_ref, o_ref, ma_ref, sgb_ref, mod_ref, wb_ref, wout_ref, xo_ref, *, tn):
    gt1 = jnp.concatenate([mod_ref[...]] * tn, axis=0)
    br_b = _bdot(o_ref[...], wb_ref[...])
    merged = ma_ref[...] + sgb_ref[...] * br_b
    xo_ref[...] = x_ref[...] + gt1 * _bdot(merged, wout_ref[...])


def _spost_call(x, o, ma, sgb, mod_s, wb, wout, l, nb, tn):
    return pl.pallas_call(
        functools.partial(_spost_kernel, tn=tn),
        grid=(1,),
        in_specs=[
            _const_spec(x.shape), _const_spec(o.shape), _const_spec(ma.shape),
            _const_spec(sgb.shape),
            pl.BlockSpec((None, nb, D_MODEL), lambda i: (l, 0, 2), pipeline_mode=pl.Buffered(1)),
            _layer_spec((ATTN_WIDTH, D_MODEL), l),
            _layer_spec((D_MODEL, D_MODEL), l),
        ],
        out_specs=_whole_out_spec(x.shape),
        out_shape=jax.ShapeDtypeStruct(x.shape, F32),
        compiler_params=pltpu.CompilerParams(
            dimension_semantics=("arbitrary",),
            vmem_limit_bytes=VMEM_LIMIT),
        name="sample_post",
    )(x, o, ma, sgb, mod_s, wb, wout)


def kernel(x_prompt, x_sample, cache_k, cache_v, state_pool, c_prompt, c_sample,
           w_ada, b_ada, norm1_g, w_in, w_pool, pool_scale, attn_sinks, w_a, w_b,
           w_out, norm2_g, w_up, w_down, final_g):
    bp, tp, _ = x_prompt.shape
    nb, tn, _ = x_sample.shape
    assert tp % MIX_ROWS == 0 and tp % FFN_ROWS == 0 and tn <= 8 and nb == LANES
    assert bp % 8 == 0

    mod_p, mod_s = _mod_call(c_prompt, c_sample, w_ada, b_ada)

    bf = lambda w: w.astype(BF16)
    win, wpool, wa, wb, wout, wup, wdn = map(bf, (w_in, w_pool, w_a, w_b, w_out, w_up, w_down))
    bias = jnp.asarray(_prompt_bias())
    slopes = jnp.asarray(_alibi_slopes(), F32)
    fg = final_g.reshape(1, D_MODEL)
    g1 = norm1_g.reshape(DEPTH, 1, D_MODEL)
    g2 = norm2_g.reshape(DEPTH, 1, D_MODEL)
    psc = pool_scale.reshape(DEPTH, 1, POOL_WIDTH)

    xs = x_sample.transpose(1, 0, 2).reshape(tn * nb, D_MODEL)
    ck = cache_k.reshape(DEPTH, nb, WINDOW * KV_WIDTH)
    cv = cache_v.reshape(DEPTH, nb, WINDOW * KV_WIDTH)
    prefix_t = state_pool.transpose(0, 2, 1, 3)

    xp = x_prompt
    kp, vp, pp, ps, new_rows = [], [], [], [], []
    for l in range(DEPTH):
        last = l == DEPTH - 1
        xp, nk, nv, npool = _pmix_call(xp, mod_p, g1, win, wpool, psc, wa, wb, wout,
                                       attn_sinks, bias, l)
        kp.append(nk); vp.append(nv); pp.append(npool)

        q, kn, vn, knb, vnb, ma, sgb, npool_s = _sproj_call(xs, mod_s, g1, win, wpool, psc, wa,
                                                            prefix_t, l, nb, tn)
        new_rows.append((knb, vnb))
        if last:
            o, nks, nvs = _sattn_call(q, kn, vn, ck, cv, attn_sinks, slopes, new_rows, l, nb, tn)
        else:
            o, = _sattn_call(q, kn, vn, ck, cv, attn_sinks, slopes, None, l, nb, tn)
        xs = _spost_call(xs, o, ma, sgb, mod_s, wb, wout, l, nb, tn)
        ps.append(npool_s)

        xp, xs = _ffn_call(xp, xs, mod_p, mod_s, g2, wup, wdn, fg, l, last)

    to_bt = lambda a: a.reshape(tn, nb, a.shape[-1]).transpose(1, 0, 2)

    kv_shape_p = (DEPTH, bp, WINDOW, N_KV_HEADS, HEAD_DIM)
    kv_shape_s = (DEPTH, nb, WINDOW, N_KV_HEADS, HEAD_DIM)
    return (xp,
            to_bt(xs),
            jnp.stack(kp).reshape(kv_shape_p),
            jnp.stack(vp).reshape(kv_shape_p),
            jnp.stack(pp),
            nks.reshape(kv_shape_s),
            nvs.reshape(kv_shape_s),
            jnp.stack(ps).transpose(0, 2, 1, 3))
```

```python
import functools

import numpy as np
import jax
import jax.numpy as jnp
from jax import lax
from jax.experimental import pallas as pl
from jax.experimental.pallas import tpu as pltpu

D_MODEL = 1024
DEPTH = 2
PAST_LEN = 16384
POOL_WIDTH = D_MODEL // 2
POOL_WINDOWS = (2, 4, 8, 16)
POOL_GROUP = POOL_WIDTH // len(POOL_WINDOWS)
POOL_PAD = max(POOL_WINDOWS) - 1
N_HEADS = 8
N_KV_HEADS = 2
HEAD_DIM = 64
GQA_GROUP = N_HEADS // N_KV_HEADS
ATTN_WIDTH = N_HEADS * HEAD_DIM
KV_WIDTH = N_KV_HEADS * HEAD_DIM
WINDOW = 128
ATTN_BLOCK = 128
D_FF = 4 * D_MODEL
RMS_EPS = 1e-6
NEG_INF = -1e30

OFF_U = 0
OFF_Q = OFF_U + POOL_WIDTH
OFF_K = OFF_Q + ATTN_WIDTH
OFF_V = OFF_K + KV_WIDTH
OFF_GA = OFF_V + KV_WIDTH
OFF_GB = OFF_GA + D_MODEL
IN_WIDTH = OFF_GB + D_MODEL
MOD_SLAB = 3 * D_MODEL

LANES = 128
HIST = 16
assert all(w & (w - 1) == 0 for w in POOL_WINDOWS) and list(POOL_WINDOWS) == sorted(POOL_WINDOWS)
assert HIST >= POOL_PAD
VMEM_LIMIT = 56 * 1024 * 1024
SUB_ROWS = 256
MIX_ROWS = 2 * SUB_ROWS
FFN_ROWS = 512
SATTN_KEYS = 16

F32 = jnp.float32
BF16 = jnp.bfloat16


def _bdot(a, b):
    return jnp.dot(a.astype(BF16), b.astype(BF16), preferred_element_type=F32)


def _rms_mod(x, g, sc, sh):
    ms = jnp.mean(x * x, axis=-1, keepdims=True)
    return (x * lax.rsqrt(ms + RMS_EPS) * g) * (1.0 + sc) + sh


def _alibi_slopes():
    return 2.0 ** (-8.0 * (np.arange(N_HEADS) + 1) / N_HEADS)


def _const_spec(shape):
    nd = len(shape)
    return pl.BlockSpec(shape, lambda *_: (0,) * nd, pipeline_mode=pl.Buffered(1))


def _layer_spec(shape, l):
    nd = len(shape)
    return pl.BlockSpec((None,) + tuple(shape), lambda *_: (l,) + (0,) * nd,
                        pipeline_mode=pl.Buffered(1))


def _whole_out_spec(shape):
    nd = len(shape)
    return pl.BlockSpec(shape, lambda *_: (0,) * nd)


def _mod_kernel(cp_ref, cs_ref, w_ref, b_ref, op_ref, os_ref):
    w = w_ref[...].astype(BF16)
    b = b_ref[...]
    for c_ref, o_ref in ((cp_ref, op_ref), (cs_ref, os_ref)):
        c = c_ref[...]
        s = (c * jax.nn.sigmoid(c)).astype(BF16)
        o_ref[...] = jnp.dot(s, w, preferred_element_type=F32) + b


def _mod_call(c_p, c_s, w_ada, b_ada):
    tn = 1024
    n = 6 * D_MODEL
    mp, ms = c_p.shape[0], c_s.shape[0]
    return pl.pallas_call(
        _mod_kernel,
        grid=(DEPTH, n // tn),
        in_specs=[
            _const_spec((mp, D_MODEL)),
            _const_spec((ms, D_MODEL)),
            pl.BlockSpec((None, D_MODEL, tn), lambda l, j: (l, 0, j)),
            pl.BlockSpec((None, 1, tn), lambda l, j: (l, 0, j)),
        ],
        out_specs=[
            pl.BlockSpec((None, mp, tn), lambda l, j: (l, 0, j)),
            pl.BlockSpec((None, ms, tn), lambda l, j: (l, 0, j)),
        ],
        out_shape=[jax.ShapeDtypeStruct((DEPTH, mp, n), F32),
                   jax.ShapeDtypeStruct((DEPTH, ms, n), F32)],
        compiler_params=pltpu.CompilerParams(
            dimension_semantics=("arbitrary", "arbitrary"),
            vmem_limit_bytes=VMEM_LIMIT),
        name="adaln_mod",
    )(c_p, c_s, w_ada, b_ada.reshape(DEPTH, 1, n))


def _prompt_bias():
    i = np.arange(ATTN_BLOCK)[:, None]
    j = np.arange(2 * ATTN_BLOCK)[None, :]
    dist = i + ATTN_BLOCK - j
    valid = (dist >= 0) & (dist < WINDOW)
    valid_first = valid & (j >= ATTN_BLOCK)
    sl = _alibi_slopes()[:, None, None]
    b = np.where(valid[None], -sl * dist[None], NEG_INF)
    b0 = np.where(valid_first[None], -sl * dist[None], NEG_INF)
    return np.stack([b, b0]).astype(np.float32)


def _pmix_kernel(sinks_ref, x_ref, mod_ref, g1_ref, win_ref, wpool_ref, pscale_ref,
                 wa_ref, wb_ref, wout_ref, bias_ref,
                 xo_ref, nk_ref, nv_ref, npool_ref,
                 ubuf, ka, kb, va, vb, obuf, *, layer):
    tm, sub = MIX_ROWS, SUB_ROWS
    bi = pl.program_id(0)
    t = pl.program_id(1)
    nt = pl.num_programs(1)

    @pl.when(t == 0)
    def _init():
        ubuf[0:HIST, :] = jnp.zeros((HIST, POOL_WIDTH), F32)
        zero = jnp.zeros((N_KV_HEADS, ATTN_BLOCK, LANES), BF16)
        ka[:, 0:ATTN_BLOCK, :] = zero
        kb[:, 0:ATTN_BLOCK, :] = zero
        va[:, 0:ATTN_BLOCK, :] = zero
        vb[:, 0:ATTN_BLOCK, :] = zero

    mod = mod_ref[pl.ds(bi, 1), :]
    sh1 = mod[:, 0:D_MODEL]
    sc1 = mod[:, D_MODEL:2 * D_MODEL]
    gt1 = mod[:, 2 * D_MODEL:3 * D_MODEL]
    lane = lax.broadcasted_iota(jnp.int32, (sub, LANES), 1)
    low = lane < HEAD_DIM
    low_q = lax.broadcasted_iota(jnp.int32, (ATTN_BLOCK, LANES), 1) < HEAD_DIM
    nt_dims = (((1,), (1,)), ((), ()))
    st = [dict(ro=i * sub) for i in range(tm // sub)]

    def norm(c):
        c["x"] = x_ref[c["ro"]:c["ro"] + sub, :]
        c["h"] = _rms_mod(c["x"], g1_ref[...], sc1, sh1).astype(BF16)

    def proj(c):
        h = c["h"]
        c["u"] = jnp.dot(h, win_ref[:, OFF_U:OFF_Q], preferred_element_type=F32)
        q = jnp.dot(h, win_ref[:, OFF_Q:OFF_K], preferred_element_type=F32)
        c["qb"] = (q * (HEAD_DIM ** -0.5)).astype(BF16)
        kv = jnp.dot(h, win_ref[:, OFF_K:OFF_GA], preferred_element_type=F32)
        c["k"] = kv[:, 0:KV_WIDTH]
        c["v"] = kv[:, KV_WIDTH:2 * KV_WIDTH]

    def pool_sums(c):
        ro, u = c["ro"], c["u"]
        ubuf[HIST + ro:HIST + ro + sub, :] = u
        pos = t * tm + ro + lax.broadcasted_iota(jnp.int32, (sub, 1), 0)
        cur = ubuf[ro:ro + HIST + sub, :]
        d, w = [], 1
        for g, wg in enumerate(POOL_WINDOWS):
            while w < wg:
                cur = cur + pltpu.roll(cur, w, axis=0)
                w *= 2
            ug = u[:, g * POOL_GROUP:(g + 1) * POOL_GROUP]
            cnt = jnp.minimum(wg, pos + 1).astype(F32)
            d.append((cur[HIST:, 0:POOL_GROUP] / cnt - ug).astype(BF16))
            if g + 1 < len(POOL_WINDOWS):
                cur = cur[:, POOL_GROUP:]
        c["d"] = d

    def kv_store(c):
        r = ATTN_BLOCK + c["ro"]
        k, v = c["k"], c["v"]
        kr = pltpu.roll(k, HEAD_DIM, axis=1)
        vr = pltpu.roll(v, HEAD_DIM, axis=1)
        ka[0, r:r + sub, :] = jnp.where(low, k, 0.0).astype(BF16)
        kb[0, r:r + sub, :] = jnp.where(low, 0.0, kr).astype(BF16)
        ka[1, r:r + sub, :] = jnp.where(low, kr, 0.0).astype(BF16)
        kb[1, r:r + sub, :] = jnp.where(low, 0.0, k).astype(BF16)
        va[0, r:r + sub, :] = jnp.where(low, v, 0.0).astype(BF16)
        vb[0, r:r + sub, :] = jnp.where(low, 0.0, vr).astype(BF16)
        va[1, r:r + sub, :] = jnp.where(low, vr, 0.0).astype(BF16)
        vb[1, r:r + sub, :] = jnp.where(low, 0.0, v).astype(BF16)

    def pool_proj(c):
        y = jnp.concatenate(
            [jnp.dot(c["d"][g], wpool_ref[g], preferred_element_type=F32)
             for g in range(len(POOL_WINDOWS))], axis=1)
        c["br_a"] = _bdot(y * pscale_ref[...], wa_ref[...])

    def units(c):
        j0 = c["ro"] // ATTN_BLOCK
        return [(j, hk, pr) for j in range(j0, j0 + sub // ATTN_BLOCK)
                for hk in range(N_KV_HEADS) for pr in range(GQA_GROUP // 2)]

    def scores(c):
        sc = {}
        for (j, hk, pr) in units(c):
            r0 = j * ATTN_BLOCK
            c0 = (hk * (GQA_GROUP // 2) + pr) * LANES
            q2 = c["qb"][r0 - c["ro"]:r0 - c["ro"] + ATTN_BLOCK, c0:c0 + LANES]
            ks = (ka[hk, r0:r0 + 2 * ATTN_BLOCK, :], kb[hk, r0:r0 + 2 * ATTN_BLOCK, :])
            for e in range(2):
                sc[(j, hk, pr, e)] = lax.dot_general(q2, ks[e], nt_dims,
                                                     preferred_element_type=F32)
        c["s"] = sc

    def gate_a(c):
        c["ga"] = jnp.dot(c["h"], win_ref[:, OFF_GA:OFF_GB], preferred_element_type=F32)

    def gate_b(c):
        c["gb"] = jnp.dot(c["h"], win_ref[:, OFF_GB:IN_WIDTH], preferred_element_type=F32)

    def softmax(c):
        probs, dens = {}, {}
        for (j, hk, pr) in units(c):
            for e in range(2):
                head = hk * GQA_GROUP + pr * 2 + e
                if j == 0:
                    bias = jnp.where(t == 0, bias_ref[1, head], bias_ref[0, head])
                else:
                    bias = bias_ref[0, head]
                s = c["s"][(j, hk, pr, e)] + bias
                sink = sinks_ref[layer, head]
                m = jnp.maximum(jnp.max(s, axis=-1, keepdims=True), sink)
                p = jnp.exp(s - m)
                dens[(j, hk, pr, e)] = jnp.sum(p, axis=-1, keepdims=True) + jnp.exp(sink - m)
                probs[(j, hk, pr, e)] = p.astype(BF16)
        c["p"], c["den"] = probs, dens

    def values(c):
        for (j, hk, pr) in units(c):
            r0 = j * ATTN_BLOCK
            c0 = (hk * (GQA_GROUP // 2) + pr) * LANES
            vs = (va[hk, r0:r0 + 2 * ATTN_BLOCK, :], vb[hk, r0:r0 + 2 * ATTN_BLOCK, :])
            o2 = (jnp.dot(c["p"][(j, hk, pr, 0)], vs[0], preferred_element_type=F32)
                  + jnp.dot(c["p"][(j, hk, pr, 1)], vs[1], preferred_element_type=F32))
            den = jnp.where(low_q, c["den"][(j, hk, pr, 0)], c["den"][(j, hk, pr, 1)])
            obuf[r0:r0 + ATTN_BLOCK, c0:c0 + LANES] = o2 / den

    def tail(c):
        ro = c["ro"]
        br_b = _bdot(obuf[ro:ro + sub, :], wb_ref[...])
        merged = jax.nn.sigmoid(c["ga"]) * c["br_a"] + jax.nn.sigmoid(c["gb"]) * br_b
        xo_ref[ro:ro + sub, :] = c["x"] + gt1 * _bdot(merged, wout_ref[...])

    stages = [
        (norm,),
        (proj,),
        (pool_sums, kv_store),
        (pool_proj, scores, gate_a),
        (softmax,),
        (values, gate_b),
        (tail,),
    ]
    order = sorted((2 * s + 3 * i, i, s) for i in range(len(st)) for s in range(len(stages)))
    for _, i, s in order:
        for fn in stages[s]:
            fn(st[i])

    @pl.when(t == nt - 1)
    def _state():
        nk_ref[...] = st[-1]["k"][sub - WINDOW:, :]
        nv_ref[...] = st[-1]["v"][sub - WINDOW:, :]
        npool_ref[...] = ubuf[HIST + tm - POOL_PAD:HIST + tm, :]

    ubuf[0:HIST, :] = ubuf[tm:tm + HIST, :]
    for buf in (ka, kb, va, vb):
        buf[:, 0:ATTN_BLOCK, :] = buf[:, tm:tm + ATTN_BLOCK, :]


def _pmix_call(x, mod_p, g1, win, wpool, pscale, wa, wb, wout, sinks, bias, l):
    b, t, _ = x.shape
    tm = MIX_ROWS
    grid_spec = pltpu.PrefetchScalarGridSpec(
        num_scalar_prefetch=1,
        grid=(b, t // tm),
        in_specs=[
            pl.BlockSpec((None, tm, D_MODEL), lambda i, j, s: (i, j, 0)),
            pl.BlockSpec((None, b, MOD_SLAB), lambda i, j, s: (l, 0, 0),
                         pipeline_mode=pl.Buffered(1)),
            _layer_spec((1, D_MODEL), l),
            _layer_spec((D_MODEL, IN_WIDTH), l),
            _layer_spec((len(POOL_WINDOWS), POOL_GROUP, POOL_GROUP), l),
            _layer_spec((1, POOL_WIDTH), l),
            _layer_spec((POOL_WIDTH, D_MODEL), l),
            _layer_spec((ATTN_WIDTH, D_MODEL), l),
            _layer_spec((D_MODEL, D_MODEL), l),
            _const_spec((2, N_HEADS, ATTN_BLOCK, 2 * ATTN_BLOCK)),
        ],
        out_specs=[
            pl.BlockSpec((None, tm, D_MODEL), lambda i, j, s: (i, j, 0)),
            pl.BlockSpec((None, WINDOW, KV_WIDTH), lambda i, j, s: (i, 0, 0)),
            pl.BlockSpec((None, WINDOW, KV_WIDTH), lambda i, j, s: (i, 0, 0)),
            pl.BlockSpec((None, POOL_PAD, POOL_WIDTH), lambda i, j, s: (i, 0, 0)),
        ],
        scratch_shapes=[
            pltpu.VMEM((HIST + tm, POOL_WIDTH), F32),
            pltpu.VMEM((N_KV_HEADS, ATTN_BLOCK + tm, LANES), BF16),
            pltpu.VMEM((N_KV_HEADS, ATTN_BLOCK + tm, LANES), BF16),
            pltpu.VMEM((N_KV_HEADS, ATTN_BLOCK + tm, LANES), BF16),
            pltpu.VMEM((N_KV_HEADS, ATTN_BLOCK + tm, LANES), BF16),
            pltpu.VMEM((tm, ATTN_WIDTH), F32),
        ],
    )
    return pl.pallas_call(
        functools.partial(_pmix_kernel, layer=l),
        grid_spec=grid_spec,
        out_shape=[
            jax.ShapeDtypeStruct((b, t, D_MODEL), F32),
            jax.ShapeDtypeStruct((b, WINDOW, KV_WIDTH), F32),
            jax.ShapeDtypeStruct((b, WINDOW, KV_WIDTH), F32),
            jax.ShapeDtypeStruct((b, POOL_PAD, POOL_WIDTH), F32),
        ],
        compiler_params=pltpu.CompilerParams(
            dimension_semantics=("arbitrary", "arbitrary"),
            vmem_limit_bytes=VMEM_LIMIT),
        name="prompt_mixer",
    )(sinks, x, mod_p, g1, win, wpool, pscale, wa, wb, wout, bias)


def _ffn_kernel(xp_ref, xs_ref, modp_ref, mods_ref, g2_ref, wup_ref, wdn_ref, fg_ref,
                op_ref, os_ref, *, final, n_prompt, tiles_per_row, reps):
    i = pl.program_id(0)

    def ffn(x, sh2, sc2, gt2):
        h2 = _rms_mod(x, g2_ref[...], sc2, sh2)
        ff = _bdot(h2, wup_ref[...])
        ff = jnp.square(jnp.maximum(ff, 0.0))
        y = x + gt2 * _bdot(ff, wdn_ref[...])
        if final:
            ms = jnp.mean(y * y, axis=-1, keepdims=True)
            y = y * lax.rsqrt(ms + RMS_EPS) * fg_ref[...]
        return y

    @pl.when(i < n_prompt)
    def _prompt():
        mod = modp_ref[pl.ds(i // tiles_per_row, 1), :]
        op_ref[...] = ffn(xp_ref[...], mod[:, 0:D_MODEL], mod[:, D_MODEL:2 * D_MODEL],
                          mod[:, 2 * D_MODEL:3 * D_MODEL])

    @pl.when(i == n_prompt)
    def _sample():
        mod = jnp.concatenate([mods_ref[...]] * reps, axis=0)
        os_ref[...] = ffn(xs_ref[...], mod[:, 0:D_MODEL], mod[:, D_MODEL:2 * D_MODEL],
                          mod[:, 2 * D_MODEL:3 * D_MODEL])


def _ffn_call(xp, xs, mod_p, mod_s, g2, wup, wdn, fg, l, final):
    b, t, _ = xp.shape
    ms = xs.shape[0]
    nb = mod_s.shape[1]
    tm = FFN_ROWS
    tpr = t // tm
    n_prompt = b * tpr
    last = n_prompt - 1

    def p_idx(i):
        ii = jnp.minimum(i, last)
        return (ii // tpr, ii % tpr, 0)

    return pl.pallas_call(
        functools.partial(_ffn_kernel, final=final, n_prompt=n_prompt, tiles_per_row=tpr,
                          reps=ms // nb),
        grid=(n_prompt + 1,),
        in_specs=[
            pl.BlockSpec((None, tm, D_MODEL), p_idx),
            _const_spec((ms, D_MODEL)),
            pl.BlockSpec((None, b, MOD_SLAB), lambda i: (l, 0, 1), pipeline_mode=pl.Buffered(1)),
            pl.BlockSpec((None, nb, MOD_SLAB), lambda i: (l, 0, 1), pipeline_mode=pl.Buffered(1)),
            _layer_spec((1, D_MODEL), l),
            _layer_spec((D_MODEL, D_FF), l),
            _layer_spec((D_FF, D_MODEL), l),
            _const_spec((1, D_MODEL)),
        ],
        out_specs=[
            pl.BlockSpec((None, tm, D_MODEL), p_idx),
            _whole_out_spec((ms, D_MODEL)),
        ],
        out_shape=[jax.ShapeDtypeStruct((b, t, D_MODEL), F32),
                   jax.ShapeDtypeStruct((ms, D_MODEL), F32)],
        compiler_params=pltpu.CompilerParams(
            dimension_semantics=("arbitrary",),
            vmem_limit_bytes=VMEM_LIMIT),
        name="ffn_final" if final else "ffn",
    )(xp, xs, mod_p, mod_s, g2, wup, wdn, fg)


def _sproj_kernel(x_ref, mod_ref, g1_ref, win_ref, wpool_ref, pscale_ref, wa_ref, pre_ref,
                  q_ref, k_ref, v_ref, ma_ref, sgb_ref, npool_ref, *, nb, tn):
    x = x_ref[...]
    mod = jnp.concatenate([mod_ref[:, 0:2 * D_MODEL]] * tn, axis=0)
    sh1 = mod[:, 0:D_MODEL]
    sc1 = mod[:, D_MODEL:2 * D_MODEL]
    h = _rms_mod(x, g1_ref[...], sc1, sh1).astype(BF16)

    u = jnp.dot(h, win_ref[:, OFF_U:OFF_Q], preferred_element_type=F32)
    def up(r):
        if r < POOL_PAD:
            return pre_ref[r]
        r -= POOL_PAD
        return u[r * nb:(r + 1) * nb, :]

    d_rows = []
    for tt in range(tn):
        parts = []
        for g, w in enumerate(POOL_WINDOWS):
            lo = g * POOL_GROUP
            acc = up(POOL_PAD + tt)[:, lo:lo + POOL_GROUP]
            for s in range(1, w):
                acc = acc + up(POOL_PAD + tt - s)[:, lo:lo + POOL_GROUP]
            cnt = float(min(w, PAST_LEN + tt + 1))
            parts.append(acc / cnt - up(POOL_PAD + tt)[:, lo:lo + POOL_GROUP])
        d_rows.append(parts)
    y = jnp.concatenate(
        [_bdot(jnp.concatenate([d_rows[tt][g] for tt in range(tn)], axis=0), wpool_ref[g])
         for g in range(len(POOL_WINDOWS))], axis=1)
    y = y * pscale_ref[...]
    br_a = _bdot(y, wa_ref[...])
    for r in range(POOL_PAD):
        npool_ref[r] = up(r + tn)

    q = jnp.dot(h, win_ref[:, OFF_Q:OFF_K], preferred_element_type=F32)
    q_ref[...] = q * (HEAD_DIM ** -0.5)
    kv = jnp.dot(h, win_ref[:, OFF_K:OFF_GA], preferred_element_type=F32)
    k_ref[...] = kv[:, 0:KV_WIDTH]
    v_ref[...] = kv[:, KV_WIDTH:2 * KV_WIDTH]
    ga = jnp.dot(h, win_ref[:, OFF_GA:OFF_GB], preferred_element_type=F32)
    ma_ref[...] = jax.nn.sigmoid(ga) * br_a
    gb = jnp.dot(h, win_ref[:, OFF_GB:IN_WIDTH], preferred_element_type=F32)
    sgb_ref[...] = jax.nn.sigmoid(gb)


def _sproj_call(x, mod_s, g1, win, wpool, pscale, wa, prefix_t, l, nb, tn):
    m = x.shape[0]
    shapes = [
        jax.ShapeDtypeStruct((m, ATTN_WIDTH), F32),
        jax.ShapeDtypeStruct((m, KV_WIDTH), F32),
        jax.ShapeDtypeStruct((m, KV_WIDTH), F32),
        jax.ShapeDtypeStruct((m, D_MODEL), F32),
        jax.ShapeDtypeStruct((m, D_MODEL), F32),
        jax.ShapeDtypeStruct((POOL_PAD, nb, POOL_WIDTH), F32),
    ]
    return pl.pallas_call(
        functools.partial(_sproj_kernel, nb=nb, tn=tn),
        grid=(1,),
        in_specs=[
            _const_spec(x.shape),
            pl.BlockSpec((None, nb, MOD_SLAB), lambda i: (l, 0, 0), pipeline_mode=pl.Buffered(1)),
            _layer_spec((1, D_MODEL), l),
            _layer_spec((D_MODEL, IN_WIDTH), l),
            _layer_spec((len(POOL_WINDOWS), POOL_GROUP, POOL_GROUP), l),
            _layer_spec((1, POOL_WIDTH), l),
            _layer_spec((POOL_WIDTH, D_MODEL), l),
            _layer_spec((POOL_PAD, nb, POOL_WIDTH), l),
        ],
        out_specs=[_whole_out_spec(s.shape) for s in shapes],
        out_shape=shapes,
        compiler_params=pltpu.CompilerParams(
            dimension_semantics=("arbitrary",),
            vmem_limit_bytes=VMEM_LIMIT),
        name="sample_proj",
    )(x, mod_s, g1, win, wpool, pscale, wa, prefix_t)


def _sattn_kernel(sinks_ref, slopes_ref, q_ref, kn_ref, vn_ref, kc_ref, vc_ref,
                  o_ref, nk_ref, nv_ref, qt, knt, vnt, tbuf, sbuf, ot, carry,
                  *, layer, nb, tn, ch, nch):
    i = pl.program_id(0)
    srows = WINDOW + 8
    combos = [(tt, head) for tt in range(tn) for head in range(N_HEADS)]

    def rolled_chunk(out_ref, nxt):
        out_ref[:, 0:ch - tn, :] = carry[:, tn:ch, :]
        for tt in range(tn):
            out_ref[:, ch - tn + tt, :] = nxt(tt)

    @pl.when(i == 0)
    def _start():
        for tt in range(tn):
            qt[tt] = q_ref[tt * nb:(tt + 1) * nb, :].T
            knt[tt] = kn_ref[tt * nb:(tt + 1) * nb, :].T
            vnt[tt] = vn_ref[tt * nb:(tt + 1) * nb, :].T

    def load_tiles(c_ref):
        for jj in range(ch):
            tbuf[jj] = c_ref[:, jj, :].T

    def head_rows(head):
        hk = head // GQA_GROUP
        return slice(hk * HEAD_DIM, (hk + 1) * HEAD_DIM)

    @pl.when((i >= 1) & (i < nch))
    def _roll_k():
        rolled_chunk(nk_ref, lambda tt: kc_ref[:, tt, :])

    @pl.when(i == nch)
    def _roll_k_last():
        rolled_chunk(nk_ref, lambda tt: kn_ref[tt * nb:(tt + 1) * nb, :])

    @pl.when((i > nch) & (i < 2 * nch))
    def _roll_v():
        rolled_chunk(nv_ref, lambda tt: vc_ref[:, tt, :])

    @pl.when(i == 2 * nch)
    def _roll_v_last():
        rolled_chunk(nv_ref, lambda tt: vn_ref[tt * nb:(tt + 1) * nb, :])

    @pl.when(i < nch)
    def _scores():
        carry[...] = kc_ref[...]
        load_tiles(kc_ref)
        row0 = pl.multiple_of(i * ch, ch)
        for ci, (tt, head) in enumerate(combos):
            qh = qt[tt, head * HEAD_DIM:(head + 1) * HEAD_DIM, :]
            rows = [jnp.sum(qh * tbuf[jj, head_rows(head), :], axis=0, keepdims=True)
                    for jj in range(ch)]
            sbuf[ci, pl.ds(row0, ch), :] = jnp.concatenate(rows, axis=0)

    @pl.when(i == nch - 1)
    def _softmax():
        for ci, (tt, head) in enumerate(combos):
            qh = qt[tt, head * HEAD_DIM:(head + 1) * HEAD_DIM, :]
            rows = [jnp.sum(qh * knt[t2, head_rows(head), :], axis=0, keepdims=True)
                    for t2 in range(tn)]
            rows.append(jnp.zeros((8 - tn, nb), F32))
            sbuf[ci, WINDOW:srows, :] = jnp.concatenate(rows, axis=0)
        jpos = lax.broadcasted_iota(jnp.int32, (srows, nb), 0)
        for ci, (tt, head) in enumerate(combos):
            dist = tt + WINDOW - jpos
            valid = (dist >= 0) & (dist < WINDOW)
            s = sbuf[ci] - slopes_ref[head] * dist.astype(F32)
            s = jnp.where(valid, s, NEG_INF)
            sink = sinks_ref[layer, head]
            m = jnp.maximum(jnp.max(s, axis=0, keepdims=True), sink)
            p = jnp.exp(s - m)
            den = jnp.sum(p, axis=0, keepdims=True) + jnp.exp(sink - m)
            p = p / den
            sbuf[ci] = p
            acc = jnp.zeros((HEAD_DIM, nb), F32)
            for t2 in range(tt + 1):
                acc = acc + p[WINDOW + t2:WINDOW + t2 + 1, :] * vnt[t2, head_rows(head), :]
            ot[tt, head * HEAD_DIM:(head + 1) * HEAD_DIM, :] = acc

    @pl.when((i >= nch) & (i < 2 * nch))
    def _values():
        carry[...] = vc_ref[...]
        load_tiles(vc_ref)
        row0 = pl.multiple_of((i - nch) * ch, ch)
        for ci, (tt, head) in enumerate(combos):
            acc = ot[tt, head * HEAD_DIM:(head + 1) * HEAD_DIM, :]
            for jj in range(ch):
                acc = acc + sbuf[ci, pl.ds(row0 + jj, 1), :] * tbuf[jj, head_rows(head), :]
            ot[tt, head * HEAD_DIM:(head + 1) * HEAD_DIM, :] = acc

    @pl.when(i == 2 * nch)
    def _finish():
        for tt in range(tn):
            o_ref[tt * nb:(tt + 1) * nb, :] = ot[tt].T


def _sattn_call(q, kn, vn, ck, cv, sinks, slopes, l, nb, tn):
    m = q.shape[0]
    ch = SATTN_KEYS
    nch = WINDOW // ch
    last = nch - 1
    chunk = (None, nb, ch, KV_WIDTH)
    cache_shape = jax.ShapeDtypeStruct((nb, WINDOW, KV_WIDTH), F32)
    grid_spec = pltpu.PrefetchScalarGridSpec(
        num_scalar_prefetch=2,
        grid=(2 * nch + 1,),
        in_specs=[
            _const_spec((m, ATTN_WIDTH)),
            _const_spec((m, KV_WIDTH)),
            _const_spec((m, KV_WIDTH)),
            pl.BlockSpec(chunk, lambda i, a, b: (l, 0, jnp.minimum(i, last), 0)),
            pl.BlockSpec(chunk, lambda i, a, b: (l, 0, jnp.clip(i - nch, 0, last), 0)),
        ],
        out_specs=[
            _whole_out_spec((m, ATTN_WIDTH)),
            pl.BlockSpec(chunk[1:], lambda i, a, b: (0, jnp.clip(i - 1, 0, last), 0)),
            pl.BlockSpec(chunk[1:], lambda i, a, b: (0, jnp.clip(i - nch - 1, 0, last), 0)),
        ],
        scratch_shapes=[
            pltpu.VMEM((tn, ATTN_WIDTH, nb), F32),
            pltpu.VMEM((tn, KV_WIDTH, nb), F32),
            pltpu.VMEM((tn, KV_WIDTH, nb), F32),
            pltpu.VMEM((ch, KV_WIDTH, nb), F32),
            pltpu.VMEM((tn * N_HEADS, WINDOW + 8, nb), F32),
            pltpu.VMEM((tn, ATTN_WIDTH, nb), F32),
            pltpu.VMEM((nb, ch, KV_WIDTH), F32),
        ],
    )
    return pl.pallas_call(
        functools.partial(_sattn_kernel, layer=l, nb=nb, tn=tn, ch=ch, nch=nch),
        grid_spec=grid_spec,
        out_shape=[jax.ShapeDtypeStruct((m, ATTN_WIDTH), F32), cache_shape, cache_shape],
        compiler_params=pltpu.CompilerParams(
            dimension_semantics=("arbitrary",),
            vmem_limit_bytes=VMEM_LIMIT),
        name="sample_attn",
    )(sinks, slopes, q, kn, vn, ck, cv)


def _spost_kernel(x_ref, o_ref, ma_ref, sgb_ref, mod_ref, wb_ref, wout_ref, xo_ref, *, tn):
    gt1 = jnp.concatenate([mod_ref[...]] * tn, axis=0)
    br_b = _bdot(o_ref[...], wb_ref[...])
    merged = ma_ref[...] + sgb_ref[...] * br_b
    xo_ref[...] = x_ref[...] + gt1 * _bdot(merged, wout_ref[...])


def _spost_call(x, o, ma, sgb, mod_s, wb, wout, l, nb, tn):
    return pl.pallas_call(
        functools.partial(_spost_kernel, tn=tn),
        grid=(1,),
        in_specs=[
            _const_spec(x.shape), _const_spec(o.shape), _const_spec(ma.shape),
            _const_spec(sgb.shape),
            pl.BlockSpec((None, nb, D_MODEL), lambda i: (l, 0, 2), pipeline_mode=pl.Buffered(1)),
            _layer_spec((ATTN_WIDTH, D_MODEL), l),
            _layer_spec((D_MODEL, D_MODEL), l),
        ],
        out_specs=_whole_out_spec(x.shape),
        out_shape=jax.ShapeDtypeStruct(x.shape, F32),
        compiler_params=pltpu.CompilerParams(
            dimension_semantics=("arbitrary",),
            vmem_limit_bytes=VMEM_LIMIT),
        name="sample_post",
    )(x, o, ma, sgb, mod_s, wb, wout)


def kernel(x_prompt, x_sample, cache_k, cache_v, state_pool, c_prompt, c_sample,
           w_ada, b_ada, norm1_g, w_in, w_pool, pool_scale, attn_sinks, w_a, w_b,
           w_out, norm2_g, w_up, w_down, final_g):
    bp, tp, _ = x_prompt.shape
    nb, tn, _ = x_sample.shape
    assert tp % MIX_ROWS == 0 and tp % FFN_ROWS == 0 and tn <= 8 and nb == LANES
    assert bp % 8 == 0

    mod_p, mod_s = _mod_call(c_prompt, c_sample, w_ada, b_ada)

    bf = lambda w: w.astype(BF16)
    win, wpool, wa, wb, wout, wup, wdn = map(bf, (w_in, w_pool, w_a, w_b, w_out, w_up, w_down))
    bias = jnp.asarray(_prompt_bias())
    slopes = jnp.asarray(_alibi_slopes(), F32)
    fg = final_g.reshape(1, D_MODEL)
    g1 = norm1_g.reshape(DEPTH, 1, D_MODEL)
    g2 = norm2_g.reshape(DEPTH, 1, D_MODEL)
    psc = pool_scale.reshape(DEPTH, 1, POOL_WIDTH)

    xs = x_sample.transpose(1, 0, 2).reshape(tn * nb, D_MODEL)
    ck = cache_k.reshape(DEPTH, nb, WINDOW, KV_WIDTH)
    cv = cache_v.reshape(DEPTH, nb, WINDOW, KV_WIDTH)
    prefix_t = state_pool.transpose(0, 2, 1, 3)

    xp = x_prompt
    kp, vp, pp, ps, ks, vs = [], [], [], [], [], []
    for l in range(DEPTH):
        last = l == DEPTH - 1
        xp, nk, nv, npool = _pmix_call(xp, mod_p, g1, win, wpool, psc, wa, wb, wout,
                                       attn_sinks, bias, l)
        kp.append(nk); vp.append(nv); pp.append(npool)

        q, kn, vn, ma, sgb, npool_s = _sproj_call(xs, mod_s, g1, win, wpool, psc, wa,
                                                  prefix_t, l, nb, tn)
        o, nk_s, nv_s = _sattn_call(q, kn, vn, ck, cv, attn_sinks, slopes, l, nb, tn)
        xs = _spost_call(xs, o, ma, sgb, mod_s, wb, wout, l, nb, tn)
        ps.append(npool_s); ks.append(nk_s); vs.append(nv_s)

        xp, xs = _ffn_call(xp, xs, mod_p, mod_s, g2, wup, wdn, fg, l, last)

    to_bt = lambda a: a.reshape(tn, nb, a.shape[-1]).transpose(1, 0, 2)

    kv_shape_p = (DEPTH, bp, WINDOW, N_KV_HEADS, HEAD_DIM)
    kv_shape_s = (DEPTH, nb, WINDOW, N_KV_HEADS, HEAD_DIM)
    return (xp,
            to_bt(xs),
            jnp.stack(kp).reshape(kv_shape_p),
            jnp.stack(vp).reshape(kv_shape_p),
            jnp.stack(pp),
            jnp.stack(ks).reshape(kv_shape_s),
            jnp.stack(vs).reshape(kv_shape_s),
            jnp.stack(ps).transpose(0, 2, 1, 3))
```

```python
import functools

import numpy as np
import jax
import jax.numpy as jnp
from jax import lax
from jax.experimental import pallas as pl
from jax.experimental.pallas import tpu as pltpu

D_MODEL = 1024
DEPTH = 2
PAST_LEN = 16384
POOL_WIDTH = D_MODEL // 2
POOL_WINDOWS = (2, 4, 8, 16)
POOL_GROUP = POOL_WIDTH // len(POOL_WINDOWS)
POOL_PAD = max(POOL_WINDOWS) - 1
N_HEADS = 8
N_KV_HEADS = 2
HEAD_DIM = 64
GQA_GROUP = N_HEADS // N_KV_HEADS
ATTN_WIDTH = N_HEADS * HEAD_DIM
KV_WIDTH = N_KV_HEADS * HEAD_DIM
WINDOW = 128
ATTN_BLOCK = 128
D_FF = 4 * D_MODEL
RMS_EPS = 1e-6
NEG_INF = -1e30

OFF_U = 0
OFF_Q = OFF_U + POOL_WIDTH
OFF_K = OFF_Q + ATTN_WIDTH
OFF_V = OFF_K + KV_WIDTH
OFF_GA = OFF_V + KV_WIDTH
OFF_GB = OFF_GA + D_MODEL
IN_WIDTH = OFF_GB + D_MODEL
MOD_SLAB = 3 * D_MODEL

LANES = 128
HIST = 16
assert all(w & (w - 1) == 0 for w in POOL_WINDOWS) and list(POOL_WINDOWS) == sorted(POOL_WINDOWS)
assert HIST >= POOL_PAD
VMEM_LIMIT = 56 * 1024 * 1024
SUB_ROWS = 256
MIX_ROWS = 2 * SUB_ROWS
FFN_ROWS = 512
SATTN_KEYS = 16

F32 = jnp.float32
BF16 = jnp.bfloat16


def _bdot(a, b):
    return jnp.dot(a.astype(BF16), b.astype(BF16), preferred_element_type=F32)


def _rms_mod(x, g, sc, sh):
    ms = jnp.mean(x * x, axis=-1, keepdims=True)
    return (x * lax.rsqrt(ms + RMS_EPS) * g) * (1.0 + sc) + sh


def _alibi_slopes():
    return 2.0 ** (-8.0 * (np.arange(N_HEADS) + 1) / N_HEADS)


def _const_spec(shape):
    nd = len(shape)
    return pl.BlockSpec(shape, lambda *_: (0,) * nd, pipeline_mode=pl.Buffered(1))


def _layer_spec(shape, l):
    nd = len(shape)
    return pl.BlockSpec((None,) + tuple(shape), lambda *_: (l,) + (0,) * nd,
                        pipeline_mode=pl.Buffered(1))


def _whole_out_spec(shape):
    nd = len(shape)
    return pl.BlockSpec(shape, lambda *_: (0,) * nd)


def _mod_kernel(cp_ref, cs_ref, w_ref, b_ref, op_ref, os_ref):
    w = w_ref[...].astype(BF16)
    b = b_ref[...]
    for c_ref, o_ref in ((cp_ref, op_ref), (cs_ref, os_ref)):
        c = c_ref[...]
        s = (c * jax.nn.sigmoid(c)).astype(BF16)
        o_ref[...] = jnp.dot(s, w, preferred_element_type=F32) + b


def _mod_call(c_p, c_s, w_ada, b_ada):
    tn = 1024
    n = 6 * D_MODEL
    mp, ms = c_p.shape[0], c_s.shape[0]
    return pl.pallas_call(
        _mod_kernel,
        grid=(DEPTH, n // tn),
        in_specs=[
            _const_spec((mp, D_MODEL)),
            _const_spec((ms, D_MODEL)),
            pl.BlockSpec((None, D_MODEL, tn), lambda l, j: (l, 0, j)),
            pl.BlockSpec((None, 1, tn), lambda l, j: (l, 0, j)),
        ],
        out_specs=[
            pl.BlockSpec((None, mp, tn), lambda l, j: (l, 0, j)),
            pl.BlockSpec((None, ms, tn), lambda l, j: (l, 0, j)),
        ],
        out_shape=[jax.ShapeDtypeStruct((DEPTH, mp, n), F32),
                   jax.ShapeDtypeStruct((DEPTH, ms, n), F32)],
        compiler_params=pltpu.CompilerParams(
            dimension_semantics=("arbitrary", "arbitrary"),
            vmem_limit_bytes=VMEM_LIMIT),
        name="adaln_mod",
    )(c_p, c_s, w_ada, b_ada.reshape(DEPTH, 1, n))


def _prompt_bias():
    i = np.arange(ATTN_BLOCK)[:, None]
    j = np.arange(2 * ATTN_BLOCK)[None, :]
    dist = i + ATTN_BLOCK - j
    valid = (dist >= 0) & (dist < WINDOW)
    valid_first = valid & (j >= ATTN_BLOCK)
    sl = _alibi_slopes()[:, None, None]
    b = np.where(valid[None], -sl * dist[None], NEG_INF)
    b0 = np.where(valid_first[None], -sl * dist[None], NEG_INF)
    return np.stack([b, b0]).astype(np.float32)


def _pmix_kernel(sinks_ref, x_ref, mod_ref, g1_ref, win_ref, wpool_ref, pscale_ref,
                 wa_ref, wb_ref, wout_ref, bias_ref,
                 xo_ref, nk_ref, nv_ref, npool_ref,
                 ubuf, ka, kb, va, vb, obuf, *, layer):
    tm, sub = MIX_ROWS, SUB_ROWS
    bi = pl.program_id(0)
    t = pl.program_id(1)
    nt = pl.num_programs(1)

    @pl.when(t == 0)
    def _init():
        ubuf[0:HIST, :] = jnp.zeros((HIST, POOL_WIDTH), F32)
        zero = jnp.zeros((N_KV_HEADS, ATTN_BLOCK, LANES), BF16)
        ka[:, 0:ATTN_BLOCK, :] = zero
        kb[:, 0:ATTN_BLOCK, :] = zero
        va[:, 0:ATTN_BLOCK, :] = zero
        vb[:, 0:ATTN_BLOCK, :] = zero

    mod = mod_ref[pl.ds(bi, 1), :]
    sh1 = mod[:, 0:D_MODEL]
    sc1 = mod[:, D_MODEL:2 * D_MODEL]
    gt1 = mod[:, 2 * D_MODEL:3 * D_MODEL]
    lane = lax.broadcasted_iota(jnp.int32, (sub, LANES), 1)
    low = lane < HEAD_DIM
    low_q = lax.broadcasted_iota(jnp.int32, (ATTN_BLOCK, LANES), 1) < HEAD_DIM
    nt_dims = (((1,), (1,)), ((), ()))
    st = [dict(ro=i * sub) for i in range(tm // sub)]

    def norm(c):
        c["x"] = x_ref[c["ro"]:c["ro"] + sub, :]
        c["h"] = _rms_mod(c["x"], g1_ref[...], sc1, sh1).astype(BF16)

    def proj(c):
        h = c["h"]
        c["u"] = jnp.dot(h, win_ref[:, OFF_U:OFF_Q], preferred_element_type=F32)
        q = jnp.dot(h, win_ref[:, OFF_Q:OFF_K], preferred_element_type=F32)
        c["qb"] = (q * (HEAD_DIM ** -0.5)).astype(BF16)
        kv = jnp.dot(h, win_ref[:, OFF_K:OFF_GA], preferred_element_type=F32)
        c["k"] = kv[:, 0:KV_WIDTH]
        c["v"] = kv[:, KV_WIDTH:2 * KV_WIDTH]

    def pool_sums(c):
        ro, u = c["ro"], c["u"]
        ubuf[HIST + ro:HIST + ro + sub, :] = u
        pos = t * tm + ro + lax.broadcasted_iota(jnp.int32, (sub, 1), 0)
        cur = ubuf[ro:ro + HIST + sub, :]
        d, w = [], 1
        for g, wg in enumerate(POOL_WINDOWS):
            while w < wg:
                cur = cur + pltpu.roll(cur, w, axis=0)
                w *= 2
            ug = u[:, g * POOL_GROUP:(g + 1) * POOL_GROUP]
            cnt = jnp.minimum(wg, pos + 1).astype(F32)
            d.append((cur[HIST:, 0:POOL_GROUP] / cnt - ug).astype(BF16))
            if g + 1 < len(POOL_WINDOWS):
                cur = cur[:, POOL_GROUP:]
        c["d"] = d

    def kv_store(c):
        r = ATTN_BLOCK + c["ro"]
        k, v = c["k"], c["v"]
        kr = pltpu.roll(k, HEAD_DIM, axis=1)
        vr = pltpu.roll(v, HEAD_DIM, axis=1)
        ka[0, r:r + sub, :] = jnp.where(low, k, 0.0).astype(BF16)
        kb[0, r:r + sub, :] = jnp.where(low, 0.0, kr).astype(BF16)
        ka[1, r:r + sub, :] = jnp.where(low, kr, 0.0).astype(BF16)
        kb[1, r:r + sub, :] = jnp.where(low, 0.0, k).astype(BF16)
        va[0, r:r + sub, :] = jnp.where(low, v, 0.0).astype(BF16)
        vb[0, r:r + sub, :] = jnp.where(low, 0.0, vr).astype(BF16)
        va[1, r:r + sub, :] = jnp.where(low, vr, 0.0).astype(BF16)
        vb[1, r:r + sub, :] = jnp.where(low, 0.0, v).astype(BF16)

    def pool_proj(c):
        y = jnp.concatenate(
            [jnp.dot(c["d"][g], wpool_ref[g], preferred_element_type=F32)
             for g in range(len(POOL_WINDOWS))], axis=1)
        c["br_a"] = _bdot(y * pscale_ref[...], wa_ref[...])

    def units(c):
        j0 = c["ro"] // ATTN_BLOCK
        return [(j, hk, pr) for j in range(j0, j0 + sub // ATTN_BLOCK)
                for hk in range(N_KV_HEADS) for pr in range(GQA_GROUP // 2)]

    def scores(c):
        sc = {}
        for (j, hk, pr) in units(c):
            r0 = j * ATTN_BLOCK
            c0 = (hk * (GQA_GROUP // 2) + pr) * LANES
            q2 = c["qb"][r0 - c["ro"]:r0 - c["ro"] + ATTN_BLOCK, c0:c0 + LANES]
            ks = (ka[hk, r0:r0 + 2 * ATTN_BLOCK, :], kb[hk, r0:r0 + 2 * ATTN_BLOCK, :])
            for e in range(2):
                sc[(j, hk, pr, e)] = lax.dot_general(q2, ks[e], nt_dims,
                                                     preferred_element_type=F32)
        c["s"] = sc

    def gate_a(c):
        c["ga"] = jnp.dot(c["h"], win_ref[:, OFF_GA:OFF_GB], preferred_element_type=F32)

    def gate_b(c):
        c["gb"] = jnp.dot(c["h"], win_ref[:, OFF_GB:IN_WIDTH], preferred_element_type=F32)

    def softmax(c):
        probs, dens = {}, {}
        for (j, hk, pr) in units(c):
            for e in range(2):
                head = hk * GQA_GROUP + pr * 2 + e
                if j == 0:
                    bias = jnp.where(t == 0, bias_ref[1, head], bias_ref[0, head])
                else:
                    bias = bias_ref[0, head]
                s = c["s"][(j, hk, pr, e)] + bias
                sink = sinks_ref[layer, head]
                m = jnp.maximum(jnp.max(s, axis=-1, keepdims=True), sink)
                p = jnp.exp(s - m)
                dens[(j, hk, pr, e)] = jnp.sum(p, axis=-1, keepdims=True) + jnp.exp(sink - m)
                probs[(j, hk, pr, e)] = p.astype(BF16)
        c["p"], c["den"] = probs, dens

    def values(c):
        for (j, hk, pr) in units(c):
            r0 = j * ATTN_BLOCK
            c0 = (hk * (GQA_GROUP // 2) + pr) * LANES
            vs = (va[hk, r0:r0 + 2 * ATTN_BLOCK, :], vb[hk, r0:r0 + 2 * ATTN_BLOCK, :])
            o2 = (jnp.dot(c["p"][(j, hk, pr, 0)], vs[0], preferred_element_type=F32)
                  + jnp.dot(c["p"][(j, hk, pr, 1)], vs[1], preferred_element_type=F32))
            den = jnp.where(low_q, c["den"][(j, hk, pr, 0)], c["den"][(j, hk, pr, 1)])
            obuf[r0:r0 + ATTN_BLOCK, c0:c0 + LANES] = o2 / den

    def tail(c):
        ro = c["ro"]
        br_b = _bdot(obuf[ro:ro + sub, :], wb_ref[...])
        merged = jax.nn.sigmoid(c["ga"]) * c["br_a"] + jax.nn.sigmoid(c["gb"]) * br_b
        xo_ref[ro:ro + sub, :] = c["x"] + gt1 * _bdot(merged, wout_ref[...])

    stages = [
        (norm,),
        (proj,),
        (pool_sums, kv_store),
        (pool_proj, scores, gate_a),
        (softmax,),
        (values, gate_b),
        (tail,),
    ]
    order = sorted((2 * s + 3 * i, i, s) for i in range(len(st)) for s in range(len(stages)))
    for _, i, s in order:
        for fn in stages[s]:
            fn(st[i])

    @pl.when(t == nt - 1)
    def _state():
        nk_ref[...] = st[-1]["k"][sub - WINDOW:, :]
        nv_ref[...] = st[-1]["v"][sub - WINDOW:, :]
        npool_ref[...] = ubuf[HIST + tm - POOL_PAD:HIST + tm, :]

    ubuf[0:HIST, :] = ubuf[tm:tm + HIST, :]
    for buf in (ka, kb, va, vb):
        buf[:, 0:ATTN_BLOCK, :] = buf[:, tm:tm + ATTN_BLOCK, :]


def _pmix_call(x, mod_p, g1, win, wpool, pscale, wa, wb, wout, sinks, bias, l):
    b, t, _ = x.shape
    tm = MIX_ROWS
    grid_spec = pltpu.PrefetchScalarGridSpec(
        num_scalar_prefetch=1,
        grid=(b, t // tm),
        in_specs=[
            pl.BlockSpec((None, tm, D_MODEL), lambda i, j, s: (i, j, 0)),
            pl.BlockSpec((None, b, MOD_SLAB), lambda i, j, s: (l, 0, 0),
                         pipeline_mode=pl.Buffered(1)),
            _layer_spec((1, D_MODEL), l),
            _layer_spec((D_MODEL, IN_WIDTH), l),
            _layer_spec((len(POOL_WINDOWS), POOL_GROUP, POOL_GROUP), l),
            _layer_spec((1, POOL_WIDTH), l),
            _layer_spec((POOL_WIDTH, D_MODEL), l),
            _layer_spec((ATTN_WIDTH, D_MODEL), l),
            _layer_spec((D_MODEL, D_MODEL), l),
            _const_spec((2, N_HEADS, ATTN_BLOCK, 2 * ATTN_BLOCK)),
        ],
        out_specs=[
            pl.BlockSpec((None, tm, D_MODEL), lambda i, j, s: (i, j, 0)),
            pl.BlockSpec((None, WINDOW, KV_WIDTH), lambda i, j, s: (i, 0, 0)),
            pl.BlockSpec((None, WINDOW, KV_WIDTH), lambda i, j, s: (i, 0, 0)),
            pl.BlockSpec((None, POOL_PAD, POOL_WIDTH), lambda i, j, s: (i, 0, 0)),
        ],
        scratch_shapes=[
            pltpu.VMEM((HIST + tm, POOL_WIDTH), F32),
            pltpu.VMEM((N_KV_HEADS, ATTN_BLOCK + tm, LANES), BF16),
            pltpu.VMEM((N_KV_HEADS, ATTN_BLOCK + tm, LANES), BF16),
            pltpu.VMEM((N_KV_HEADS, ATTN_BLOCK + tm, LANES), BF16),
            pltpu.VMEM((N_KV_HEADS, ATTN_BLOCK + tm, LANES), BF16),
            pltpu.VMEM((tm, ATTN_WIDTH), F32),
        ],
    )
    return pl.pallas_call(
        functools.partial(_pmix_kernel, layer=l),
        grid_spec=grid_spec,
        out_shape=[
            jax.ShapeDtypeStruct((b, t, D_MODEL), F32),
            jax.ShapeDtypeStruct((b, WINDOW, KV_WIDTH), F32),
            jax.ShapeDtypeStruct((b, WINDOW, KV_WIDTH), F32),
            jax.ShapeDtypeStruct((b, POOL_PAD, POOL_WIDTH), F32),
        ],
        compiler_params=pltpu.CompilerParams(
            dimension_semantics=("arbitrary", "arbitrary"),
            vmem_limit_bytes=VMEM_LIMIT),
        name="prompt_mixer",
    )(sinks, x, mod_p, g1, win, wpool, pscale, wa, wb, wout, bias)


def _ffn_kernel(xp_ref, xs_ref, modp_ref, mods_ref, g2_ref, wup_ref, wdn_ref, fg_ref,
                op_ref, os_ref, *, final, n_prompt, tiles_per_row, reps):
    i = pl.program_id(0)

    def ffn(x, sh2, sc2, gt2):
        h2 = _rms_mod(x, g2_ref[...], sc2, sh2)
        ff = _bdot(h2, wup_ref[...])
        ff = jnp.square(jnp.maximum(ff, 0.0))
        y = x + gt2 * _bdot(ff, wdn_ref[...])
        if final:
            ms = jnp.mean(y * y, axis=-1, keepdims=True)
            y = y * lax.rsqrt(ms + RMS_EPS) * fg_ref[...]
        return y

    @pl.when(i < n_prompt)
    def _prompt():
        mod = modp_ref[pl.ds(i // tiles_per_row, 1), :]
        op_ref[...] = ffn(xp_ref[...], mod[:, 0:D_MODEL], mod[:, D_MODEL:2 * D_MODEL],
                          mod[:, 2 * D_MODEL:3 * D_MODEL])

    @pl.when(i == n_prompt)
    def _sample():
        mod = jnp.concatenate([mods_ref[...]] * reps, axis=0)
        os_ref[...] = ffn(xs_ref[...], mod[:, 0:D_MODEL], mod[:, D_MODEL:2 * D_MODEL],
                          mod[:, 2 * D_MODEL:3 * D_MODEL])


def _ffn_call(xp, xs, mod_p, mod_s, g2, wup, wdn, fg, l, final):
    b, t, _ = xp.shape
    ms = xs.shape[0]
    nb = mod_s.shape[1]
    tm = FFN_ROWS
    tpr = t // tm
    n_prompt = b * tpr
    last = n_prompt - 1

    def p_idx(i):
        ii = jnp.minimum(i, last)
        return (ii // tpr, ii % tpr, 0)

    return pl.pallas_call(
        functools.partial(_ffn_kernel, final=final, n_prompt=n_prompt, tiles_per_row=tpr,
                          reps=ms // nb),
        grid=(n_prompt + 1,),
        in_specs=[
            pl.BlockSpec((None, tm, D_MODEL), p_idx),
            _const_spec((ms, D_MODEL)),
            pl.BlockSpec((None, b, MOD_SLAB), lambda i: (l, 0, 1), pipeline_mode=pl.Buffered(1)),
            pl.BlockSpec((None, nb, MOD_SLAB), lambda i: (l, 0, 1), pipeline_mode=pl.Buffered(1)),
            _layer_spec((1, D_MODEL), l),
            _layer_spec((D_MODEL, D_FF), l),
            _layer_spec((D_FF, D_MODEL), l),
            _const_spec((1, D_MODEL)),
        ],
        out_specs=[
            pl.BlockSpec((None, tm, D_MODEL), p_idx),
            _whole_out_spec((ms, D_MODEL)),
        ],
        out_shape=[jax.ShapeDtypeStruct((b, t, D_MODEL), F32),
                   jax.ShapeDtypeStruct((ms, D_MODEL), F32)],
        compiler_params=pltpu.CompilerParams(
            dimension_semantics=("arbitrary",),
            vmem_limit_bytes=VMEM_LIMIT),
        name="ffn_final" if final else "ffn",
    )(xp, xs, mod_p, mod_s, g2, wup, wdn, fg)


def _sproj_kernel(x_ref, mod_ref, g1_ref, win_ref, wpool_ref, pscale_ref, wa_ref, pre_ref,
                  q_ref, k_ref, v_ref, ma_ref, sgb_ref, npool_ref, *, nb, tn):
    x = x_ref[...]
    mod = jnp.concatenate([mod_ref[:, 0:2 * D_MODEL]] * tn, axis=0)
    sh1 = mod[:, 0:D_MODEL]
    sc1 = mod[:, D_MODEL:2 * D_MODEL]
    h = _rms_mod(x, g1_ref[...], sc1, sh1).astype(BF16)

    u = jnp.dot(h, win_ref[:, OFF_U:OFF_Q], preferred_element_type=F32)
    def up(r):
        if r < POOL_PAD:
            return pre_ref[r]
        r -= POOL_PAD
        return u[r * nb:(r + 1) * nb, :]

    d_rows = []
    for tt in range(tn):
        parts = []
        for g, w in enumerate(POOL_WINDOWS):
            lo = g * POOL_GROUP
            acc = up(POOL_PAD + tt)[:, lo:lo + POOL_GROUP]
            for s in range(1, w):
                acc = acc + up(POOL_PAD + tt - s)[:, lo:lo + POOL_GROUP]
            cnt = float(min(w, PAST_LEN + tt + 1))
            parts.append(acc / cnt - up(POOL_PAD + tt)[:, lo:lo + POOL_GROUP])
        d_rows.append(parts)
    y = jnp.concatenate(
        [_bdot(jnp.concatenate([d_rows[tt][g] for tt in range(tn)], axis=0), wpool_ref[g])
         for g in range(len(POOL_WINDOWS))], axis=1)
    y = y * pscale_ref[...]
    br_a = _bdot(y, wa_ref[...])
    for r in range(POOL_PAD):
        npool_ref[r] = up(r + tn)

    q = jnp.dot(h, win_ref[:, OFF_Q:OFF_K], preferred_element_type=F32)
    q_ref[...] = q * (HEAD_DIM ** -0.5)
    kv = jnp.dot(h, win_ref[:, OFF_K:OFF_GA], preferred_element_type=F32)
    k_ref[...] = kv[:, 0:KV_WIDTH]
    v_ref[...] = kv[:, KV_WIDTH:2 * KV_WIDTH]
    ga = jnp.dot(h, win_ref[:, OFF_GA:OFF_GB], preferred_element_type=F32)
    ma_ref[...] = jax.nn.sigmoid(ga) * br_a
    gb = jnp.dot(h, win_ref[:, OFF_GB:IN_WIDTH], preferred_element_type=F32)
    sgb_ref[...] = jax.nn.sigmoid(gb)


def _sproj_call(x, mod_s, g1, win, wpool, pscale, wa, prefix_t, l, nb, tn):
    m = x.shape[0]
    shapes = [
        jax.ShapeDtypeStruct((m, ATTN_WIDTH), F32),
        jax.ShapeDtypeStruct((m, KV_WIDTH), F32),
        jax.ShapeDtypeStruct((m, KV_WIDTH), F32),
        jax.ShapeDtypeStruct((m, D_MODEL), F32),
        jax.ShapeDtypeStruct((m, D_MODEL), F32),
        jax.ShapeDtypeStruct((POOL_PAD, nb, POOL_WIDTH), F32),
    ]
    return pl.pallas_call(
        functools.partial(_sproj_kernel, nb=nb, tn=tn),
        grid=(1,),
        in_specs=[
            _const_spec(x.shape),
            pl.BlockSpec((None, nb, MOD_SLAB), lambda i: (l, 0, 0), pipeline_mode=pl.Buffered(1)),
            _layer_spec((1, D_MODEL), l),
            _layer_spec((D_MODEL, IN_WIDTH), l),
            _layer_spec((len(POOL_WINDOWS), POOL_GROUP, POOL_GROUP), l),
            _layer_spec((1, POOL_WIDTH), l),
            _layer_spec((POOL_WIDTH, D_MODEL), l),
            _layer_spec((POOL_PAD, nb, POOL_WIDTH), l),
        ],
        out_specs=[_whole_out_spec(s.shape) for s in shapes],
        out_shape=shapes,
        compiler_params=pltpu.CompilerParams(
            dimension_semantics=("arbitrary",),
            vmem_limit_bytes=VMEM_LIMIT),
        name="sample_proj",
    )(x, mod_s, g1, win, wpool, pscale, wa, prefix_t)


def _sattn_kernel(sinks_ref, slopes_ref, q_ref, kn_ref, vn_ref, kc_ref, vc_ref, *rest,
                  layer, nb, tn, ch, nch):
    if layer:
        pk_ref, pv_ref = rest[:2]
        rest = rest[2:]
    o_ref, nk_ref, nv_ref, qt, knt, vnt, tbuf, sbuf, ot, carry = rest
    i = pl.program_id(0)
    srows = WINDOW + 8
    combos = [(tt, head) for tt in range(tn) for head in range(N_HEADS)]

    def rolled_chunk(out_ref, prev_ref, nxt):
        if layer:
            out_ref[0:layer] = prev_ref[...]
        out_ref[layer, :, 0:ch - tn, :] = carry[:, tn:ch, :]
        for tt in range(tn):
            out_ref[layer, :, ch - tn + tt, :] = nxt(tt)

    def roll_k(nxt):
        rolled_chunk(nk_ref, pk_ref if layer else None, nxt)

    def roll_v(nxt):
        rolled_chunk(nv_ref, pv_ref if layer else None, nxt)

    @pl.when(i == 0)
    def _start():
        for tt in range(tn):
            qt[tt] = q_ref[tt * nb:(tt + 1) * nb, :].T
            knt[tt] = kn_ref[tt * nb:(tt + 1) * nb, :].T
            vnt[tt] = vn_ref[tt * nb:(tt + 1) * nb, :].T

    def load_tiles(c_ref):
        keys = pltpu.einshape("bjc->jbc", c_ref[...])
        for jj in range(ch):
            tbuf[jj] = keys[jj].T

    def head_rows(head):
        hk = head // GQA_GROUP
        return slice(hk * HEAD_DIM, (hk + 1) * HEAD_DIM)

    @pl.when((i >= 1) & (i < nch))
    def _roll_k():
        roll_k(lambda tt: kc_ref[:, tt, :])

    @pl.when(i == nch)
    def _roll_k_last():
        roll_k(lambda tt: kn_ref[tt * nb:(tt + 1) * nb, :])

    @pl.when((i > nch) & (i < 2 * nch))
    def _roll_v():
        roll_v(lambda tt: vc_ref[:, tt, :])

    @pl.when(i == 2 * nch)
    def _roll_v_last():
        roll_v(lambda tt: vn_ref[tt * nb:(tt + 1) * nb, :])

    @pl.when(i < nch)
    def _scores():
        carry[...] = kc_ref[...]
        load_tiles(kc_ref)
        row0 = pl.multiple_of(i * ch, ch)
        for ci, (tt, head) in enumerate(combos):
            qh = qt[tt, head * HEAD_DIM:(head + 1) * HEAD_DIM, :]
            rows = [jnp.sum(qh * tbuf[jj, head_rows(head), :], axis=0, keepdims=True)
                    for jj in range(ch)]
            sbuf[ci, pl.ds(row0, ch), :] = jnp.concatenate(rows, axis=0)

    @pl.when(i == nch - 1)
    def _softmax():
        for ci, (tt, head) in enumerate(combos):
            qh = qt[tt, head * HEAD_DIM:(head + 1) * HEAD_DIM, :]
            rows = [jnp.sum(qh * knt[t2, head_rows(head), :], axis=0, keepdims=True)
                    for t2 in range(tn)]
            rows.append(jnp.zeros((8 - tn, nb), F32))
            sbuf[ci, WINDOW:srows, :] = jnp.concatenate(rows, axis=0)
        jpos = lax.broadcasted_iota(jnp.int32, (srows, nb), 0)
        for ci, (tt, head) in enumerate(combos):
            dist = tt + WINDOW - jpos
            valid = (dist >= 0) & (dist < WINDOW)
            s = sbuf[ci] - slopes_ref[head] * dist.astype(F32)
            s = jnp.where(valid, s, NEG_INF)
            sink = sinks_ref[layer, head]
            m = jnp.maximum(jnp.max(s, axis=0, keepdims=True), sink)
            p = jnp.exp(s - m)
            den = jnp.sum(p, axis=0, keepdims=True) + jnp.exp(sink - m)
            p = p / den
            sbuf[ci] = p
            acc = jnp.zeros((HEAD_DIM, nb), F32)
            for t2 in range(tt + 1):
                acc = acc + p[WINDOW + t2:WINDOW + t2 + 1, :] * vnt[t2, head_rows(head), :]
            ot[tt, head * HEAD_DIM:(head + 1) * HEAD_DIM, :] = acc

    @pl.when((i >= nch) & (i < 2 * nch))
    def _values():
        carry[...] = vc_ref[...]
        load_tiles(vc_ref)
        row0 = pl.multiple_of((i - nch) * ch, ch)
        for ci, (tt, head) in enumerate(combos):
            acc = ot[tt, head * HEAD_DIM:(head + 1) * HEAD_DIM, :]
            for jj in range(ch):
                acc = acc + sbuf[ci, pl.ds(row0 + jj, 1), :] * tbuf[jj, head_rows(head), :]
            ot[tt, head * HEAD_DIM:(head + 1) * HEAD_DIM, :] = acc

    @pl.when(i == 2 * nch)
    def _finish():
        for tt in range(tn):
            o_ref[tt * nb:(tt + 1) * nb, :] = ot[tt].T


def _sattn_call(q, kn, vn, ck, cv, sinks, slopes, rolled, l, nb, tn):
    m = q.shape[0]
    ch = SATTN_KEYS
    nch = WINDOW // ch
    last = nch - 1
    chunk = (None, nb, ch, KV_WIDTH)
    cache_shape = jax.ShapeDtypeStruct((l + 1, nb, WINDOW, KV_WIDTH), F32)
    k_lag = lambda i, a, b: (0, 0, jnp.clip(i - 1, 0, last), 0)
    v_lag = lambda i, a, b: (0, 0, jnp.clip(i - nch - 1, 0, last), 0)
    in_specs = [
        _const_spec((m, ATTN_WIDTH)),
        _const_spec((m, KV_WIDTH)),
        _const_spec((m, KV_WIDTH)),
        pl.BlockSpec(chunk, lambda i, a, b: (l, 0, jnp.minimum(i, last), 0)),
        pl.BlockSpec(chunk, lambda i, a, b: (l, 0, jnp.clip(i - nch, 0, last), 0)),
    ]
    args = [sinks, slopes, q, kn, vn, ck, cv]
    if l:
        in_specs += [pl.BlockSpec((l, nb, ch, KV_WIDTH), k_lag),
                     pl.BlockSpec((l, nb, ch, KV_WIDTH), v_lag)]
        args += list(rolled)
    grid_spec = pltpu.PrefetchScalarGridSpec(
        num_scalar_prefetch=2,
        grid=(2 * nch + 1,),
        in_specs=in_specs,
        out_specs=[
            _whole_out_spec((m, ATTN_WIDTH)),
            pl.BlockSpec((l + 1, nb, ch, KV_WIDTH), k_lag),
            pl.BlockSpec((l + 1, nb, ch, KV_WIDTH), v_lag),
        ],
        scratch_shapes=[
            pltpu.VMEM((tn, ATTN_WIDTH, nb), F32),
            pltpu.VMEM((tn, KV_WIDTH, nb), F32),
            pltpu.VMEM((tn, KV_WIDTH, nb), F32),
            pltpu.VMEM((ch, KV_WIDTH, nb), F32),
            pltpu.VMEM((tn * N_HEADS, WINDOW + 8, nb), F32),
            pltpu.VMEM((tn, ATTN_WIDTH, nb), F32),
            pltpu.VMEM((nb, ch, KV_WIDTH), F32),
        ],
    )
    return pl.pallas_call(
        functools.partial(_sattn_kernel, layer=l, nb=nb, tn=tn, ch=ch, nch=nch),
        grid_spec=grid_spec,
        out_shape=[jax.ShapeDtypeStruct((m, ATTN_WIDTH), F32), cache_shape, cache_shape],
        compiler_params=pltpu.CompilerParams(
            dimension_semantics=("arbitrary",),
            vmem_limit_bytes=VMEM_LIMIT),
        name="sample_attn",
    )(*args)


def _spost_kernel(x_ref, o_ref, ma_ref, sgb_ref, mod_ref, wb_ref, wout_ref, xo_ref, *, tn):
    gt1 = jnp.concatenate([mod_ref[...]] * tn, axis=0)
    br_b = _bdot(o_ref[...], wb_ref[...])
    merged = ma_ref[...] + sgb_ref[...] * br_b
    xo_ref[...] = x_ref[...] + gt1 * _bdot(merged, wout_ref[...])


def _spost_call(x, o, ma, sgb, mod_s, wb, wout, l, nb, tn):
    return pl.pallas_call(
        functools.partial(_spost_kernel, tn=tn),
        grid=(1,),
        in_specs=[
            _const_spec(x.shape), _const_spec(o.shape), _const_spec(ma.shape),
            _const_spec(sgb.shape),
            pl.BlockSpec((None, nb, D_MODEL), lambda i: (l, 0, 2), pipeline_mode=pl.Buffered(1)),
            _layer_spec((ATTN_WIDTH, D_MODEL), l),
            _layer_spec((D_MODEL, D_MODEL), l),
        ],
        out_specs=_whole_out_spec(x.shape),
        out_shape=jax.ShapeDtypeStruct(x.shape, F32),
        compiler_params=pltpu.CompilerParams(
            dimension_semantics=("arbitrary",),
            vmem_limit_bytes=VMEM_LIMIT),
        name="sample_post",
    )(x, o, ma, sgb, mod_s, wb, wout)


def kernel(x_prompt, x_sample, cache_k, cache_v, state_pool, c_prompt, c_sample,
           w_ada, b_ada, norm1_g, w_in, w_pool, pool_scale, attn_sinks, w_a, w_b,
           w_out, norm2_g, w_up, w_down, final_g):
    bp, tp, _ = x_prompt.shape
    nb, tn, _ = x_sample.shape
    assert tp % MIX_ROWS == 0 and tp % FFN_ROWS == 0 and tn <= 8 and nb == LANES
    assert bp % 8 == 0

    mod_p, mod_s = _mod_call(c_prompt, c_sample, w_ada, b_ada)

    bf = lambda w: w.astype(BF16)
    win, wpool, wa, wb, wout, wup, wdn = map(bf, (w_in, w_pool, w_a, w_b, w_out, w_up, w_down))
    bias = jnp.asarray(_prompt_bias())
    slopes = jnp.asarray(_alibi_slopes(), F32)
    fg = final_g.reshape(1, D_MODEL)
    g1 = norm1_g.reshape(DEPTH, 1, D_MODEL)
    g2 = norm2_g.reshape(DEPTH, 1, D_MODEL)
    psc = pool_scale.reshape(DEPTH, 1, POOL_WIDTH)

    xs = x_sample.transpose(1, 0, 2).reshape(tn * nb, D_MODEL)
    ck = cache_k.reshape(DEPTH, nb, WINDOW, KV_WIDTH)
    cv = cache_v.reshape(DEPTH, nb, WINDOW, KV_WIDTH)
    prefix_t = state_pool.transpose(0, 2, 1, 3)

    xp = x_prompt
    kp, vp, pp, ps, rolled = [], [], [], [], None
    for l in range(DEPTH):
        last = l == DEPTH - 1
        xp, nk, nv, npool = _pmix_call(xp, mod_p, g1, win, wpool, psc, wa, wb, wout,
                                       attn_sinks, bias, l)
        kp.append(nk); vp.append(nv); pp.append(npool)

        q, kn, vn, ma, sgb, npool_s = _sproj_call(xs, mod_s, g1, win, wpool, psc, wa,
                                                  prefix_t, l, nb, tn)
        o, *rolled = _sattn_call(q, kn, vn, ck, cv, attn_sinks, slopes, rolled, l, nb, tn)
        xs = _spost_call(xs, o, ma, sgb, mod_s, wb, wout, l, nb, tn)
        ps.append(npool_s)

        xp, xs = _ffn_call(xp, xs, mod_p, mod_s, g2, wup, wdn, fg, l, last)

    to_bt = lambda a: a.reshape(tn, nb, a.shape[-1]).transpose(1, 0, 2)

    kv_shape_p = (DEPTH, bp, WINDOW, N_KV_HEADS, HEAD_DIM)
    kv_shape_s = (DEPTH, nb, WINDOW, N_KV_HEADS, HEAD_DIM)
    return (xp,
            to_bt(xs),
            jnp.stack(kp).reshape(kv_shape_p),
            jnp.stack(vp).reshape(kv_shape_p),
            jnp.stack(pp),
            rolled[0].reshape(kv_shape_s),
            rolled[1].reshape(kv_shape_s),
            jnp.stack(ps).transpose(0, 2, 1, 3))
```

```python
import functools

import numpy as np
import jax
import jax.numpy as jnp
from jax import lax
from jax.experimental import pallas as pl
from jax.experimental.pallas import tpu as pltpu

D_MODEL = 1024
DEPTH = 2
PAST_LEN = 16384
POOL_WIDTH = D_MODEL // 2
POOL_WINDOWS = (2, 4, 8, 16)
POOL_GROUP = POOL_WIDTH // len(POOL_WINDOWS)
POOL_PAD = max(POOL_WINDOWS) - 1
N_HEADS = 8
N_KV_HEADS = 2
HEAD_DIM = 64
GQA_GROUP = N_HEADS // N_KV_HEADS
ATTN_WIDTH = N_HEADS * HEAD_DIM
KV_WIDTH = N_KV_HEADS * HEAD_DIM
WINDOW = 128
ATTN_BLOCK = 128
D_FF = 4 * D_MODEL
RMS_EPS = 1e-6
NEG_INF = -1e30

OFF_U = 0
OFF_Q = OFF_U + POOL_WIDTH
OFF_K = OFF_Q + ATTN_WIDTH
OFF_V = OFF_K + KV_WIDTH
OFF_GA = OFF_V + KV_WIDTH
OFF_GB = OFF_GA + D_MODEL
IN_WIDTH = OFF_GB + D_MODEL
MOD_SLAB = 3 * D_MODEL

LANES = 128
HIST = 16
assert all(w & (w - 1) == 0 for w in POOL_WINDOWS) and list(POOL_WINDOWS) == sorted(POOL_WINDOWS)
assert HIST >= POOL_PAD
VMEM_LIMIT = 56 * 1024 * 1024
SUB_ROWS = 256
MIX_ROWS = 2 * SUB_ROWS
FFN_ROWS = 512
SATTN_KEYS = 16

F32 = jnp.float32
BF16 = jnp.bfloat16


def _bdot(a, b):
    return jnp.dot(a.astype(BF16), b.astype(BF16), preferred_element_type=F32)


def _rms_mod(x, g, sc, sh):
    ms = jnp.mean(x * x, axis=-1, keepdims=True)
    return (x * lax.rsqrt(ms + RMS_EPS) * g) * (1.0 + sc) + sh


def _alibi_slopes():
    return 2.0 ** (-8.0 * (np.arange(N_HEADS) + 1) / N_HEADS)


def _const_spec(shape):
    nd = len(shape)
    return pl.BlockSpec(shape, lambda *_: (0,) * nd, pipeline_mode=pl.Buffered(1))


def _layer_spec(shape, l):
    nd = len(shape)
    return pl.BlockSpec((None,) + tuple(shape), lambda *_: (l,) + (0,) * nd,
                        pipeline_mode=pl.Buffered(1))


def _whole_out_spec(shape):
    nd = len(shape)
    return pl.BlockSpec(shape, lambda *_: (0,) * nd)


def _mod_kernel(cp_ref, cs_ref, w_ref, b_ref, op_ref, os_ref):
    w = w_ref[...].astype(BF16)
    b = b_ref[...]
    for c_ref, o_ref in ((cp_ref, op_ref), (cs_ref, os_ref)):
        c = c_ref[...]
        s = (c * jax.nn.sigmoid(c)).astype(BF16)
        o_ref[...] = jnp.dot(s, w, preferred_element_type=F32) + b


def _mod_call(c_p, c_s, w_ada, b_ada):
    tn = 2048
    n = 6 * D_MODEL
    mp, ms = c_p.shape[0], c_s.shape[0]
    return pl.pallas_call(
        _mod_kernel,
        grid=(DEPTH, n // tn),
        in_specs=[
            _const_spec((mp, D_MODEL)),
            _const_spec((ms, D_MODEL)),
            pl.BlockSpec((None, D_MODEL, tn), lambda l, j: (l, 0, j)),
            pl.BlockSpec((None, 1, tn), lambda l, j: (l, 0, j)),
        ],
        out_specs=[
            pl.BlockSpec((None, mp, tn), lambda l, j: (l, 0, j)),
            pl.BlockSpec((None, ms, tn), lambda l, j: (l, 0, j)),
        ],
        out_shape=[jax.ShapeDtypeStruct((DEPTH, mp, n), F32),
                   jax.ShapeDtypeStruct((DEPTH, ms, n), F32)],
        compiler_params=pltpu.CompilerParams(
            dimension_semantics=("arbitrary", "arbitrary"),
            vmem_limit_bytes=VMEM_LIMIT),
        name="adaln_mod",
    )(c_p, c_s, w_ada, b_ada.reshape(DEPTH, 1, n))


def _prompt_bias():
    i = np.arange(ATTN_BLOCK)[:, None]
    j = np.arange(2 * ATTN_BLOCK)[None, :]
    dist = i + ATTN_BLOCK - j
    valid = (dist >= 0) & (dist < WINDOW)
    valid_first = valid & (j >= ATTN_BLOCK)
    sl = _alibi_slopes()[:, None, None]
    b = np.where(valid[None], -sl * dist[None], NEG_INF)
    b0 = np.where(valid_first[None], -sl * dist[None], NEG_INF)
    return np.stack([b, b0]).astype(np.float32)


def _pmix_kernel(sinks_ref, x_ref, mod_ref, g1_ref, win_ref, wpool_ref, pscale_ref,
                 wa_ref, wb_ref, wout_ref, bias_ref,
                 xo_ref, nk_ref, nv_ref, npool_ref,
                 ubuf, ka, kb, va, vb, obuf, *, layer):
    tm, sub = MIX_ROWS, SUB_ROWS
    bi = pl.program_id(0)
    t = pl.program_id(1)
    nt = pl.num_programs(1)

    @pl.when(t == 0)
    def _init():
        ubuf[0:HIST, :] = jnp.zeros((HIST, POOL_WIDTH), F32)
        zero = jnp.zeros((N_KV_HEADS, ATTN_BLOCK, LANES), BF16)
        ka[:, 0:ATTN_BLOCK, :] = zero
        kb[:, 0:ATTN_BLOCK, :] = zero
        va[:, 0:ATTN_BLOCK, :] = zero
        vb[:, 0:ATTN_BLOCK, :] = zero

    mod = mod_ref[pl.ds(bi, 1), :]
    sh1 = mod[:, 0:D_MODEL]
    sc1 = mod[:, D_MODEL:2 * D_MODEL]
    gt1 = mod[:, 2 * D_MODEL:3 * D_MODEL]
    lane = lax.broadcasted_iota(jnp.int32, (sub, LANES), 1)
    low = lane < HEAD_DIM
    low_q = lax.broadcasted_iota(jnp.int32, (ATTN_BLOCK, LANES), 1) < HEAD_DIM
    nt_dims = (((1,), (1,)), ((), ()))
    st = [dict(ro=i * sub) for i in range(tm // sub)]

    def norm(c):
        c["x"] = x_ref[c["ro"]:c["ro"] + sub, :]
        c["h"] = _rms_mod(c["x"], g1_ref[...], sc1, sh1).astype(BF16)

    def proj(c):
        h = c["h"]
        c["u"] = jnp.dot(h, win_ref[:, OFF_U:OFF_Q], preferred_element_type=F32)
        q = jnp.dot(h, win_ref[:, OFF_Q:OFF_K], preferred_element_type=F32)
        c["qb"] = (q * (HEAD_DIM ** -0.5)).astype(BF16)
        kv = jnp.dot(h, win_ref[:, OFF_K:OFF_GA], preferred_element_type=F32)
        c["k"] = kv[:, 0:KV_WIDTH]
        c["v"] = kv[:, KV_WIDTH:2 * KV_WIDTH]

    def pool_sums(c):
        ro, u = c["ro"], c["u"]
        ubuf[HIST + ro:HIST + ro + sub, :] = u
        pos = t * tm + ro + lax.broadcasted_iota(jnp.int32, (sub, 1), 0)
        cur = ubuf[ro:ro + HIST + sub, :]
        d, w = [], 1
        for g, wg in enumerate(POOL_WINDOWS):
            while w < wg:
                cur = cur + pltpu.roll(cur, w, axis=0)
                w *= 2
            ug = u[:, g * POOL_GROUP:(g + 1) * POOL_GROUP]
            cnt = jnp.minimum(wg, pos + 1).astype(F32)
            d.append((cur[HIST:, 0:POOL_GROUP] / cnt - ug).astype(BF16))
            if g + 1 < len(POOL_WINDOWS):
                cur = cur[:, POOL_GROUP:]
        c["d"] = d

    def kv_store(c):
        r = ATTN_BLOCK + c["ro"]
        k, v = c["k"], c["v"]
        kr = pltpu.roll(k, HEAD_DIM, axis=1)
        vr = pltpu.roll(v, HEAD_DIM, axis=1)
        ka[0, r:r + sub, :] = jnp.where(low, k, 0.0).astype(BF16)
        kb[0, r:r + sub, :] = jnp.where(low, 0.0, kr).astype(BF16)
        ka[1, r:r + sub, :] = jnp.where(low, kr, 0.0).astype(BF16)
        kb[1, r:r + sub, :] = jnp.where(low, 0.0, k).astype(BF16)
        va[0, r:r + sub, :] = jnp.where(low, v, 0.0).astype(BF16)
        vb[0, r:r + sub, :] = jnp.where(low, 0.0, vr).astype(BF16)
        va[1, r:r + sub, :] = jnp.where(low, vr, 0.0).astype(BF16)
        vb[1, r:r + sub, :] = jnp.where(low, 0.0, v).astype(BF16)

    def pool_proj(c):
        y = jnp.concatenate(
            [jnp.dot(c["d"][g], wpool_ref[g], preferred_element_type=F32)
             for g in range(len(POOL_WINDOWS))], axis=1)
        c["br_a"] = _bdot(y * pscale_ref[...], wa_ref[...])

    def units(c):
        j0 = c["ro"] // ATTN_BLOCK
        return [(j, hk, pr) for j in range(j0, j0 + sub // ATTN_BLOCK)
                for hk in range(N_KV_HEADS) for pr in range(GQA_GROUP // 2)]

    def scores(c):
        sc = {}
        for (j, hk, pr) in units(c):
            r0 = j * ATTN_BLOCK
            c0 = (hk * (GQA_GROUP // 2) + pr) * LANES
            q2 = c["qb"][r0 - c["ro"]:r0 - c["ro"] + ATTN_BLOCK, c0:c0 + LANES]
            ks = (ka[hk, r0:r0 + 2 * ATTN_BLOCK, :], kb[hk, r0:r0 + 2 * ATTN_BLOCK, :])
            for e in range(2):
                sc[(j, hk, pr, e)] = lax.dot_general(q2, ks[e], nt_dims,
                                                     preferred_element_type=F32)
        c["s"] = sc

    def gate_a(c):
        c["ga"] = jnp.dot(c["h"], win_ref[:, OFF_GA:OFF_GB], preferred_element_type=F32)

    def gate_b(c):
        c["gb"] = jnp.dot(c["h"], win_ref[:, OFF_GB:IN_WIDTH], preferred_element_type=F32)

    def softmax(c):
        probs, dens = {}, {}
        for (j, hk, pr) in units(c):
            for e in range(2):
                head = hk * GQA_GROUP + pr * 2 + e
                if j == 0:
                    bias = jnp.where(t == 0, bias_ref[1, head], bias_ref[0, head])
                else:
                    bias = bias_ref[0, head]
                s = c["s"][(j, hk, pr, e)] + bias
                sink = sinks_ref[layer, head]
                m = jnp.maximum(jnp.max(s, axis=-1, keepdims=True), sink)
                p = jnp.exp(s - m)
                dens[(j, hk, pr, e)] = jnp.sum(p, axis=-1, keepdims=True) + jnp.exp(sink - m)
                probs[(j, hk, pr, e)] = p.astype(BF16)
        c["p"], c["den"] = probs, dens

    def values(c):
        for (j, hk, pr) in units(c):
            r0 = j * ATTN_BLOCK
            c0 = (hk * (GQA_GROUP // 2) + pr) * LANES
            vs = (va[hk, r0:r0 + 2 * ATTN_BLOCK, :], vb[hk, r0:r0 + 2 * ATTN_BLOCK, :])
            o2 = (jnp.dot(c["p"][(j, hk, pr, 0)], vs[0], preferred_element_type=F32)
                  + jnp.dot(c["p"][(j, hk, pr, 1)], vs[1], preferred_element_type=F32))
            den = jnp.where(low_q, c["den"][(j, hk, pr, 0)], c["den"][(j, hk, pr, 1)])
            obuf[r0:r0 + ATTN_BLOCK, c0:c0 + LANES] = o2 / den

    def tail(c):
        ro = c["ro"]
        br_b = _bdot(obuf[ro:ro + sub, :], wb_ref[...])
        merged = jax.nn.sigmoid(c["ga"]) * c["br_a"] + jax.nn.sigmoid(c["gb"]) * br_b
        xo_ref[ro:ro + sub, :] = c["x"] + gt1 * _bdot(merged, wout_ref[...])

    stages = [
        (norm,),
        (proj,),
        (pool_sums, kv_store),
        (pool_proj, scores, gate_a),
        (softmax,),
        (values, gate_b),
        (tail,),
    ]
    order = sorted((2 * s + 3 * i, i, s) for i in range(len(st)) for s in range(len(stages)))
    for _, i, s in order:
        for fn in stages[s]:
            fn(st[i])

    @pl.when(t == nt - 1)
    def _state():
        nk_ref[...] = st[-1]["k"][sub - WINDOW:, :]
        nv_ref[...] = st[-1]["v"][sub - WINDOW:, :]
        npool_ref[...] = ubuf[HIST + tm - POOL_PAD:HIST + tm, :]

    ubuf[0:HIST, :] = ubuf[tm:tm + HIST, :]
    for buf in (ka, kb, va, vb):
        buf[:, 0:ATTN_BLOCK, :] = buf[:, tm:tm + ATTN_BLOCK, :]


def _pmix_call(x, mod_p, g1, win, wpool, pscale, wa, wb, wout, sinks, bias, l):
    b, t, _ = x.shape
    tm = MIX_ROWS
    grid_spec = pltpu.PrefetchScalarGridSpec(
        num_scalar_prefetch=1,
        grid=(b, t // tm),
        in_specs=[
            pl.BlockSpec((None, tm, D_MODEL), lambda i, j, s: (i, j, 0)),
            pl.BlockSpec((None, b, MOD_SLAB), lambda i, j, s: (l, 0, 0),
                         pipeline_mode=pl.Buffered(1)),
            _layer_spec((1, D_MODEL), l),
            _layer_spec((D_MODEL, IN_WIDTH), l),
            _layer_spec((len(POOL_WINDOWS), POOL_GROUP, POOL_GROUP), l),
            _layer_spec((1, POOL_WIDTH), l),
            _layer_spec((POOL_WIDTH, D_MODEL), l),
            _layer_spec((ATTN_WIDTH, D_MODEL), l),
            _layer_spec((D_MODEL, D_MODEL), l),
            _const_spec((2, N_HEADS, ATTN_BLOCK, 2 * ATTN_BLOCK)),
        ],
        out_specs=[
            pl.BlockSpec((None, tm, D_MODEL), lambda i, j, s: (i, j, 0)),
            pl.BlockSpec((None, WINDOW, KV_WIDTH), lambda i, j, s: (i, 0, 0)),
            pl.BlockSpec((None, WINDOW, KV_WIDTH), lambda i, j, s: (i, 0, 0)),
            pl.BlockSpec((None, POOL_PAD, POOL_WIDTH), lambda i, j, s: (i, 0, 0)),
        ],
        scratch_shapes=[
            pltpu.VMEM((HIST + tm, POOL_WIDTH), F32),
            pltpu.VMEM((N_KV_HEADS, ATTN_BLOCK + tm, LANES), BF16),
            pltpu.VMEM((N_KV_HEADS, ATTN_BLOCK + tm, LANES), BF16),
            pltpu.VMEM((N_KV_HEADS, ATTN_BLOCK + tm, LANES), BF16),
            pltpu.VMEM((N_KV_HEADS, ATTN_BLOCK + tm, LANES), BF16),
            pltpu.VMEM((tm, ATTN_WIDTH), F32),
        ],
    )
    return pl.pallas_call(
        functools.partial(_pmix_kernel, layer=l),
        grid_spec=grid_spec,
        out_shape=[
            jax.ShapeDtypeStruct((b, t, D_MODEL), F32),
            jax.ShapeDtypeStruct((b, WINDOW, KV_WIDTH), F32),
            jax.ShapeDtypeStruct((b, WINDOW, KV_WIDTH), F32),
            jax.ShapeDtypeStruct((b, POOL_PAD, POOL_WIDTH), F32),
        ],
        compiler_params=pltpu.CompilerParams(
            dimension_semantics=("arbitrary", "arbitrary"),
            vmem_limit_bytes=VMEM_LIMIT),
        name="prompt_mixer",
    )(sinks, x, mod_p, g1, win, wpool, pscale, wa, wb, wout, bias)


def _ffn_kernel(xp_ref, xs_ref, modp_ref, mods_ref, g2_ref, wup_ref, wdn_ref, fg_ref,
                op_ref, os_ref, *, final, n_prompt, tiles_per_row, reps):
    i = pl.program_id(0)

    def ffn(x, sh2, sc2, gt2):
        h2 = _rms_mod(x, g2_ref[...], sc2, sh2)
        ff = _bdot(h2, wup_ref[...])
        ff = jnp.square(jnp.maximum(ff, 0.0))
        y = x + gt2 * _bdot(ff, wdn_ref[...])
        if final:
            ms = jnp.mean(y * y, axis=-1, keepdims=True)
            y = y * lax.rsqrt(ms + RMS_EPS) * fg_ref[...]
        return y

    @pl.when(i < n_prompt)
    def _prompt():
        mod = modp_ref[pl.ds(i // tiles_per_row, 1), :]
        op_ref[...] = ffn(xp_ref[...], mod[:, 0:D_MODEL], mod[:, D_MODEL:2 * D_MODEL],
                          mod[:, 2 * D_MODEL:3 * D_MODEL])

    @pl.when(i == n_prompt)
    def _sample():
        mod = jnp.concatenate([mods_ref[...]] * reps, axis=0)
        os_ref[...] = ffn(xs_ref[...], mod[:, 0:D_MODEL], mod[:, D_MODEL:2 * D_MODEL],
                          mod[:, 2 * D_MODEL:3 * D_MODEL])


def _ffn_call(xp, xs, mod_p, mod_s, g2, wup, wdn, fg, l, final):
    b, t, _ = xp.shape
    ms = xs.shape[0]
    nb = mod_s.shape[1]
    tm = FFN_ROWS
    tpr = t // tm
    n_prompt = b * tpr
    last = n_prompt - 1

    def p_idx(i):
        ii = jnp.minimum(i, last)
        return (ii // tpr, ii % tpr, 0)

    return pl.pallas_call(
        functools.partial(_ffn_kernel, final=final, n_prompt=n_prompt, tiles_per_row=tpr,
                          reps=ms // nb),
        grid=(n_prompt + 1,),
        in_specs=[
            pl.BlockSpec((None, tm, D_MODEL), p_idx),
            _const_spec((ms, D_MODEL)),
            pl.BlockSpec((None, b, MOD_SLAB), lambda i: (l, 0, 1), pipeline_mode=pl.Buffered(1)),
            pl.BlockSpec((None, nb, MOD_SLAB), lambda i: (l, 0, 1), pipeline_mode=pl.Buffered(1)),
            _layer_spec((1, D_MODEL), l),
            _layer_spec((D_MODEL, D_FF), l),
            _layer_spec((D_FF, D_MODEL), l),
            _const_spec((1, D_MODEL)),
        ],
        out_specs=[
            pl.BlockSpec((None, tm, D_MODEL), p_idx),
            _whole_out_spec((ms, D_MODEL)),
        ],
        out_shape=[jax.ShapeDtypeStruct((b, t, D_MODEL), F32),
                   jax.ShapeDtypeStruct((ms, D_MODEL), F32)],
        compiler_params=pltpu.CompilerParams(
            dimension_semantics=("arbitrary",),
            vmem_limit_bytes=VMEM_LIMIT),
        name="ffn_final" if final else "ffn",
    )(xp, xs, mod_p, mod_s, g2, wup, wdn, fg)


def _sproj_kernel(x_ref, mod_ref, g1_ref, win_ref, wpool_ref, pscale_ref, wa_ref, pre_ref,
                  q_ref, k_ref, v_ref, ma_ref, sgb_ref, npool_ref, *, nb, tn):
    x = x_ref[...]
    mod = jnp.concatenate([mod_ref[:, 0:2 * D_MODEL]] * tn, axis=0)
    sh1 = mod[:, 0:D_MODEL]
    sc1 = mod[:, D_MODEL:2 * D_MODEL]
    h = _rms_mod(x, g1_ref[...], sc1, sh1).astype(BF16)

    u = jnp.dot(h, win_ref[:, OFF_U:OFF_Q], preferred_element_type=F32)
    def up(r):
        if r < POOL_PAD:
            return pre_ref[r]
        r -= POOL_PAD
        return u[r * nb:(r + 1) * nb, :]

    d_rows = []
    for tt in range(tn):
        parts = []
        for g, w in enumerate(POOL_WINDOWS):
            lo = g * POOL_GROUP
            acc = up(POOL_PAD + tt)[:, lo:lo + POOL_GROUP]
            for s in range(1, w):
                acc = acc + up(POOL_PAD + tt - s)[:, lo:lo + POOL_GROUP]
            cnt = float(min(w, PAST_LEN + tt + 1))
            parts.append(acc / cnt - up(POOL_PAD + tt)[:, lo:lo + POOL_GROUP])
        d_rows.append(parts)
    y = jnp.concatenate(
        [_bdot(jnp.concatenate([d_rows[tt][g] for tt in range(tn)], axis=0), wpool_ref[g])
         for g in range(len(POOL_WINDOWS))], axis=1)
    y = y * pscale_ref[...]
    br_a = _bdot(y, wa_ref[...])
    for r in range(POOL_PAD):
        npool_ref[r] = up(r + tn)

    q = jnp.dot(h, win_ref[:, OFF_Q:OFF_K], preferred_element_type=F32)
    q_ref[...] = q * (HEAD_DIM ** -0.5)
    kv = jnp.dot(h, win_ref[:, OFF_K:OFF_GA], preferred_element_type=F32)
    k_ref[...] = kv[:, 0:KV_WIDTH]
    v_ref[...] = kv[:, KV_WIDTH:2 * KV_WIDTH]
    ga = jnp.dot(h, win_ref[:, OFF_GA:OFF_GB], preferred_element_type=F32)
    ma_ref[...] = jax.nn.sigmoid(ga) * br_a
    gb = jnp.dot(h, win_ref[:, OFF_GB:IN_WIDTH], preferred_element_type=F32)
    sgb_ref[...] = jax.nn.sigmoid(gb)


def _sproj_call(x, mod_s, g1, win, wpool, pscale, wa, prefix_t, l, nb, tn):
    m = x.shape[0]
    shapes = [
        jax.ShapeDtypeStruct((m, ATTN_WIDTH), F32),
        jax.ShapeDtypeStruct((m, KV_WIDTH), F32),
        jax.ShapeDtypeStruct((m, KV_WIDTH), F32),
        jax.ShapeDtypeStruct((m, D_MODEL), F32),
        jax.ShapeDtypeStruct((m, D_MODEL), F32),
        jax.ShapeDtypeStruct((POOL_PAD, nb, POOL_WIDTH), F32),
    ]
    return pl.pallas_call(
        functools.partial(_sproj_kernel, nb=nb, tn=tn),
        grid=(1,),
        in_specs=[
            _const_spec(x.shape),
            pl.BlockSpec((None, nb, MOD_SLAB), lambda i: (l, 0, 0), pipeline_mode=pl.Buffered(1)),
            _layer_spec((1, D_MODEL), l),
            _layer_spec((D_MODEL, IN_WIDTH), l),
            _layer_spec((len(POOL_WINDOWS), POOL_GROUP, POOL_GROUP), l),
            _layer_spec((1, POOL_WIDTH), l),
            _layer_spec((POOL_WIDTH, D_MODEL), l),
            _layer_spec((POOL_PAD, nb, POOL_WIDTH), l),
        ],
        out_specs=[_whole_out_spec(s.shape) for s in shapes],
        out_shape=shapes,
        compiler_params=pltpu.CompilerParams(
            dimension_semantics=("arbitrary",),
            vmem_limit_bytes=VMEM_LIMIT),
        name="sample_proj",
    )(x, mod_s, g1, win, wpool, pscale, wa, prefix_t)


def _sattn_kernel(sinks_ref, slopes_ref, q_ref, kn_ref, vn_ref, kc_ref, vc_ref, *rest,
                  layer, nb, tn, ch, nch):
    if layer:
        pk_ref, pv_ref = rest[:2]
        rest = rest[2:]
    o_ref, nk_ref, nv_ref, qt, knt, vnt, tbuf, sbuf, ot, carry = rest
    i = pl.program_id(0)
    srows = WINDOW + 8
    combos = [(tt, head) for tt in range(tn) for head in range(N_HEADS)]

    def rolled_chunk(out_ref, prev_ref, nxt):
        if layer:
            out_ref[0:layer] = prev_ref[...]
        out_ref[layer, :, 0:ch - tn, :] = carry[:, tn:ch, :]
        for tt in range(tn):
            out_ref[layer, :, ch - tn + tt, :] = nxt(tt)

    def roll_k(nxt):
        rolled_chunk(nk_ref, pk_ref if layer else None, nxt)

    def roll_v(nxt):
        rolled_chunk(nv_ref, pv_ref if layer else None, nxt)

    @pl.when(i == 0)
    def _start():
        for tt in range(tn):
            qt[tt] = q_ref[tt * nb:(tt + 1) * nb, :].T
            knt[tt] = kn_ref[tt * nb:(tt + 1) * nb, :].T
            vnt[tt] = vn_ref[tt * nb:(tt + 1) * nb, :].T

    def load_tiles(c_ref):
        keys = jnp.swapaxes(c_ref[...], 0, 1)
        for jj in range(ch):
            tbuf[jj] = keys[jj].T

    def head_rows(head):
        hk = head // GQA_GROUP
        return slice(hk * HEAD_DIM, (hk + 1) * HEAD_DIM)

    @pl.when((i >= 1) & (i < nch))
    def _roll_k():
        roll_k(lambda tt: kc_ref[:, tt, :])

    @pl.when(i == nch)
    def _roll_k_last():
        roll_k(lambda tt: kn_ref[tt * nb:(tt + 1) * nb, :])

    @pl.when((i > nch) & (i < 2 * nch))
    def _roll_v():
        roll_v(lambda tt: vc_ref[:, tt, :])

    @pl.when(i == 2 * nch)
    def _roll_v_last():
        roll_v(lambda tt: vn_ref[tt * nb:(tt + 1) * nb, :])

    @pl.when(i < nch)
    def _scores():
        carry[...] = kc_ref[...]
        load_tiles(kc_ref)
        row0 = pl.multiple_of(i * ch, ch)
        for ci, (tt, head) in enumerate(combos):
            qh = qt[tt, head * HEAD_DIM:(head + 1) * HEAD_DIM, :]
            rows = [jnp.sum(qh * tbuf[jj, head_rows(head), :], axis=0, keepdims=True)
                    for jj in range(ch)]
            sbuf[ci, pl.ds(row0, ch), :] = jnp.concatenate(rows, axis=0)

    @pl.when(i == nch - 1)
    def _softmax():
        for ci, (tt, head) in enumerate(combos):
            qh = qt[tt, head * HEAD_DIM:(head + 1) * HEAD_DIM, :]
            rows = [jnp.sum(qh * knt[t2, head_rows(head), :], axis=0, keepdims=True)
                    for t2 in range(tn)]
            rows.append(jnp.zeros((8 - tn, nb), F32))
            sbuf[ci, WINDOW:srows, :] = jnp.concatenate(rows, axis=0)
        jpos = lax.broadcasted_iota(jnp.int32, (srows, nb), 0)
        for ci, (tt, head) in enumerate(combos):
            dist = tt + WINDOW - jpos
            valid = (dist >= 0) & (dist < WINDOW)
            s = sbuf[ci] - slopes_ref[head] * dist.astype(F32)
            s = jnp.where(valid, s, NEG_INF)
            sink = sinks_ref[layer, head]
            m = jnp.maximum(jnp.max(s, axis=0, keepdims=True), sink)
            p = jnp.exp(s - m)
            den = jnp.sum(p, axis=0, keepdims=True) + jnp.exp(sink - m)
            p = p / den
            sbuf[ci] = p
            acc = jnp.zeros((HEAD_DIM, nb), F32)
            for t2 in range(tt + 1):
                acc = acc + p[WINDOW + t2:WINDOW + t2 + 1, :] * vnt[t2, head_rows(head), :]
            ot[tt, head * HEAD_DIM:(head + 1) * HEAD_DIM, :] = acc

    @pl.when((i >= nch) & (i < 2 * nch))
    def _values():
        carry[...] = vc_ref[...]
        load_tiles(vc_ref)
        row0 = pl.multiple_of((i - nch) * ch, ch)
        for ci, (tt, head) in enumerate(combos):
            acc = ot[tt, head * HEAD_DIM:(head + 1) * HEAD_DIM, :]
            for jj in range(ch):
                acc = acc + sbuf[ci, pl.ds(row0 + jj, 1), :] * tbuf[jj, head_rows(head), :]
            ot[tt, head * HEAD_DIM:(head + 1) * HEAD_DIM, :] = acc

    @pl.when(i == 2 * nch)
    def _finish():
        for tt in range(tn):
            o_ref[tt * nb:(tt + 1) * nb, :] = ot[tt].T


def _sattn_call(q, kn, vn, ck, cv, sinks, slopes, rolled, l, nb, tn):
    m = q.shape[0]
    ch = SATTN_KEYS
    nch = WINDOW // ch
    last = nch - 1
    chunk = (None, nb, ch, KV_WIDTH)
    cache_shape = jax.ShapeDtypeStruct((l + 1, nb, WINDOW, KV_WIDTH), F32)
    k_lag = lambda i, a, b: (0, 0, jnp.clip(i - 1, 0, last), 0)
    v_lag = lambda i, a, b: (0, 0, jnp.clip(i - nch - 1, 0, last), 0)
    in_specs = [
        _const_spec((m, ATTN_WIDTH)),
        _const_spec((m, KV_WIDTH)),
        _const_spec((m, KV_WIDTH)),
        pl.BlockSpec(chunk, lambda i, a, b: (l, 0, jnp.minimum(i, last), 0)),
        pl.BlockSpec(chunk, lambda i, a, b: (l, 0, jnp.clip(i - nch, 0, last), 0)),
    ]
    args = [sinks, slopes, q, kn, vn, ck, cv]
    if l:
        in_specs += [pl.BlockSpec((l, nb, ch, KV_WIDTH), k_lag),
                     pl.BlockSpec((l, nb, ch, KV_WIDTH), v_lag)]
        args += list(rolled)
    grid_spec = pltpu.PrefetchScalarGridSpec(
        num_scalar_prefetch=2,
        grid=(2 * nch + 1,),
        in_specs=in_specs,
        out_specs=[
            _whole_out_spec((m, ATTN_WIDTH)),
            pl.BlockSpec((l + 1, nb, ch, KV_WIDTH), k_lag),
            pl.BlockSpec((l + 1, nb, ch, KV_WIDTH), v_lag),
        ],
        scratch_shapes=[
            pltpu.VMEM((tn, ATTN_WIDTH, nb), F32),
            pltpu.VMEM((tn, KV_WIDTH, nb), F32),
            pltpu.VMEM((tn, KV_WIDTH, nb), F32),
            pltpu.VMEM((ch, KV_WIDTH, nb), F32),
            pltpu.VMEM((tn * N_HEADS, WINDOW + 8, nb), F32),
            pltpu.VMEM((tn, ATTN_WIDTH, nb), F32),
            pltpu.VMEM((nb, ch, KV_WIDTH), F32),
        ],
    )
    return pl.pallas_call(
        functools.partial(_sattn_kernel, layer=l, nb=nb, tn=tn, ch=ch, nch=nch),
        grid_spec=grid_spec,
        out_shape=[jax.ShapeDtypeStruct((m, ATTN_WIDTH), F32), cache_shape, cache_shape],
        compiler_params=pltpu.CompilerParams(
            dimension_semantics=("arbitrary",),
            vmem_limit_bytes=VMEM_LIMIT),
        name="sample_attn",
    )(*args)


def _spost_kernel(x_ref, o_ref, ma_ref, sgb_ref, mod_ref, wb_ref, wout_ref, xo_ref, *, tn):
    gt1 = jnp.concatenate([mod_ref[...]] * tn, axis=0)
    br_b = _bdot(o_ref[...], wb_ref[...])
    merged = ma_ref[...] + sgb_ref[...] * br_b
    xo_ref[...] = x_ref[...] + gt1 * _bdot(merged, wout_ref[...])


def _spost_call(x, o, ma, sgb, mod_s, wb, wout, l, nb, tn):
    return pl.pallas_call(
        functools.partial(_spost_kernel, tn=tn),
        grid=(1,),
        in_specs=[
            _const_spec(x.shape), _const_spec(o.shape), _const_spec(ma.shape),
            _const_spec(sgb.shape),
            pl.BlockSpec((None, nb, D_MODEL), lambda i: (l, 0, 2), pipeline_mode=pl.Buffered(1)),
            _layer_spec((ATTN_WIDTH, D_MODEL), l),
            _layer_spec((D_MODEL, D_MODEL), l),
        ],
        out_specs=_whole_out_spec(x.shape),
        out_shape=jax.ShapeDtypeStruct(x.shape, F32),
        compiler_params=pltpu.CompilerParams(
            dimension_semantics=("arbitrary",),
            vmem_limit_bytes=VMEM_LIMIT),
        name="sample_post",
    )(x, o, ma, sgb, mod_s, wb, wout)


def kernel(x_prompt, x_sample, cache_k, cache_v, state_pool, c_prompt, c_sample,
           w_ada, b_ada, norm1_g, w_in, w_pool, pool_scale, attn_sinks, w_a, w_b,
           w_out, norm2_g, w_up, w_down, final_g):
    bp, tp, _ = x_prompt.shape
    nb, tn, _ = x_sample.shape
    assert tp % MIX_ROWS == 0 and tp % FFN_ROWS == 0 and tn <= 8 and nb == LANES
    assert bp % 8 == 0

    mod_p, mod_s = _mod_call(c_prompt, c_sample, w_ada, b_ada)

    bf = lambda w: w.astype(BF16)
    win, wpool, wa, wb, wout, wup, wdn = map(bf, (w_in, w_pool, w_a, w_b, w_out, w_up, w_down))
    bias = jnp.asarray(_prompt_bias())
    slopes = jnp.asarray(_alibi_slopes(), F32)
    fg = final_g.reshape(1, D_MODEL)
    g1 = norm1_g.reshape(DEPTH, 1, D_MODEL)
    g2 = norm2_g.reshape(DEPTH, 1, D_MODEL)
    psc = pool_scale.reshape(DEPTH, 1, POOL_WIDTH)

    xs = x_sample.transpose(1, 0, 2).reshape(tn * nb, D_MODEL)
    ck = cache_k.reshape(DEPTH, nb, WINDOW, KV_WIDTH)
    cv = cache_v.reshape(DEPTH, nb, WINDOW, KV_WIDTH)
    prefix_t = state_pool.transpose(0, 2, 1, 3)

    xp = x_prompt
    kp, vp, pp, ps, rolled = [], [], [], [], None
    for l in range(DEPTH):
        last = l == DEPTH - 1
        xp, nk, nv, npool = _pmix_call(xp, mod_p, g1, win, wpool, psc, wa, wb, wout,
                                       attn_sinks, bias, l)
        kp.append(nk); vp.append(nv); pp.append(npool)

        q, kn, vn, ma, sgb, npool_s = _sproj_call(xs, mod_s, g1, win, wpool, psc, wa,
                                                  prefix_t, l, nb, tn)
        o, *rolled = _sattn_call(q, kn, vn, ck, cv, attn_sinks, slopes, rolled, l, nb, tn)
        xs = _spost_call(xs, o, ma, sgb, mod_s, wb, wout, l, nb, tn)
        ps.append(npool_s)

        xp, xs = _ffn_call(xp, xs, mod_p, mod_s, g2, wup, wdn, fg, l, last)

    to_bt = lambda a: a.reshape(tn, nb, a.shape[-1]).transpose(1, 0, 2)

    kv_shape_p = (DEPTH, bp, WINDOW, N_KV_HEADS, HEAD_DIM)
    kv_shape_s = (DEPTH, nb, WINDOW, N_KV_HEADS, HEAD_DIM)
    return (xp,
            to_bt(xs),
            jnp.stack(kp).reshape(kv_shape_p),
            jnp.stack(vp).reshape(kv_shape_p),
            jnp.stack(pp),
            rolled[0].reshape(kv_shape_s),
            rolled[1].reshape(kv_shape_s),
            jnp.stack(ps).transpose(0, 2, 1, 3))
```

```python
import functools

import numpy as np
import jax
import jax.numpy as jnp
from jax import lax
from jax.experimental import pallas as pl
from jax.experimental.pallas import tpu as pltpu

D_MODEL = 1024
DEPTH = 2
PAST_LEN = 16384
POOL_WIDTH = D_MODEL // 2
POOL_WINDOWS = (2, 4, 8, 16)
POOL_GROUP = POOL_WIDTH // len(POOL_WINDOWS)
POOL_PAD = max(POOL_WINDOWS) - 1
N_HEADS = 8
N_KV_HEADS = 2
HEAD_DIM = 64
GQA_GROUP = N_HEADS // N_KV_HEADS
ATTN_WIDTH = N_HEADS * HEAD_DIM
KV_WIDTH = N_KV_HEADS * HEAD_DIM
WINDOW = 128
ATTN_BLOCK = 128
D_FF = 4 * D_MODEL
RMS_EPS = 1e-6
NEG_INF = -1e30

OFF_U = 0
OFF_Q = OFF_U + POOL_WIDTH
OFF_K = OFF_Q + ATTN_WIDTH
OFF_V = OFF_K + KV_WIDTH
OFF_GA = OFF_V + KV_WIDTH
OFF_GB = OFF_GA + D_MODEL
IN_WIDTH = OFF_GB + D_MODEL
MOD_SLAB = 3 * D_MODEL

LANES = 128
HIST = 16
assert all(w & (w - 1) == 0 for w in POOL_WINDOWS) and list(POOL_WINDOWS) == sorted(POOL_WINDOWS)
assert HIST >= POOL_PAD
VMEM_LIMIT = 56 * 1024 * 1024
SUB_ROWS = 256
MIX_ROWS = 2 * SUB_ROWS
FFN_ROWS = 512
SATTN_KEYS = 16

F32 = jnp.float32
BF16 = jnp.bfloat16


def _bdot(a, b):
    return jnp.dot(a.astype(BF16), b.astype(BF16), preferred_element_type=F32)


def _rms_mod(x, g, sc, sh):
    ms = jnp.mean(x * x, axis=-1, keepdims=True)
    return (x * lax.rsqrt(ms + RMS_EPS) * g) * (1.0 + sc) + sh


def _alibi_slopes():
    return 2.0 ** (-8.0 * (np.arange(N_HEADS) + 1) / N_HEADS)


def _const_spec(shape):
    nd = len(shape)
    return pl.BlockSpec(shape, lambda *_: (0,) * nd, pipeline_mode=pl.Buffered(1))


def _layer_spec(shape, l):
    nd = len(shape)
    return pl.BlockSpec((None,) + tuple(shape), lambda *_: (l,) + (0,) * nd,
                        pipeline_mode=pl.Buffered(1))


def _whole_out_spec(shape):
    nd = len(shape)
    return pl.BlockSpec(shape, lambda *_: (0,) * nd)


def _weight_spec(w, shape, l):
    return _const_spec(shape) if w.shape == tuple(shape) else _layer_spec(shape, l)


def _convert_specs(jobs, n_steps, step_of):
    in_specs, out_specs, out_shapes = [], [], []
    for w, l in jobs:
        _, r, c = w.shape
        rows = r // n_steps
        assert rows * n_steps == r and rows % 16 == 0, (w.shape, n_steps)
        in_specs.append(pl.BlockSpec((None, rows, c), lambda *a, l=l: (l, step_of(*a), 0)))
        out_specs.append(pl.BlockSpec((rows, c), lambda *a: (step_of(*a), 0)))
        out_shapes.append(jax.ShapeDtypeStruct((r, c), BF16))
    return in_specs, out_specs, out_shapes


def _convert_slabs(src_refs, dst_refs):
    for src, dst in zip(src_refs, dst_refs):
        dst[...] = src[...].astype(BF16)


def _mod_kernel(cp_ref, cs_ref, w_ref, b_ref, op_ref, os_ref):
    w = w_ref[...].astype(BF16)
    b = b_ref[...]
    for c_ref, o_ref in ((cp_ref, op_ref), (cs_ref, os_ref)):
        c = c_ref[...]
        s = (c * jax.nn.sigmoid(c)).astype(BF16)
        o_ref[...] = jnp.dot(s, w, preferred_element_type=F32) + b


def _mod_call(c_p, c_s, w_ada, b_ada):
    tn = 2048
    n = 6 * D_MODEL
    mp, ms = c_p.shape[0], c_s.shape[0]
    return pl.pallas_call(
        _mod_kernel,
        grid=(DEPTH, n // tn),
        in_specs=[
            _const_spec((mp, D_MODEL)),
            _const_spec((ms, D_MODEL)),
            pl.BlockSpec((None, D_MODEL, tn), lambda l, j: (l, 0, j)),
            pl.BlockSpec((None, 1, tn), lambda l, j: (l, 0, j)),
        ],
        out_specs=[
            pl.BlockSpec((None, mp, tn), lambda l, j: (l, 0, j)),
            pl.BlockSpec((None, ms, tn), lambda l, j: (l, 0, j)),
        ],
        out_shape=[jax.ShapeDtypeStruct((DEPTH, mp, n), F32),
                   jax.ShapeDtypeStruct((DEPTH, ms, n), F32)],
        compiler_params=pltpu.CompilerParams(
            dimension_semantics=("arbitrary", "arbitrary"),
            vmem_limit_bytes=VMEM_LIMIT),
        name="adaln_mod",
    )(c_p, c_s, w_ada, b_ada.reshape(DEPTH, 1, n))


def _prompt_bias():
    i = np.arange(ATTN_BLOCK)[:, None]
    j = np.arange(2 * ATTN_BLOCK)[None, :]
    dist = i + ATTN_BLOCK - j
    valid = (dist >= 0) & (dist < WINDOW)
    valid_first = valid & (j >= ATTN_BLOCK)
    sl = _alibi_slopes()[:, None, None]
    b = np.where(valid[None], -sl * dist[None], NEG_INF)
    b0 = np.where(valid_first[None], -sl * dist[None], NEG_INF)
    return np.stack([b, b0]).astype(np.float32)


def _pmix_kernel(sinks_ref, x_ref, mod_ref, g1_ref, win_ref, wpool_ref, pscale_ref,
                 wa_ref, wb_ref, wout_ref, bias_ref, *rest, layer, n_cvt):
    cvt_in, rest = rest[:n_cvt], rest[n_cvt:]
    (xo_ref, nk_ref, nv_ref, npool_ref), rest = rest[:4], rest[4:]
    cvt_out, (ubuf, ka, kb, va, vb, obuf) = rest[:n_cvt], rest[n_cvt:]
    _convert_slabs(cvt_in, cvt_out)
    tm, sub = MIX_ROWS, SUB_ROWS
    bi = pl.program_id(0)
    t = pl.program_id(1)
    nt = pl.num_programs(1)

    @pl.when(t == 0)
    def _init():
        ubuf[0:HIST, :] = jnp.zeros((HIST, POOL_WIDTH), F32)
        zero = jnp.zeros((N_KV_HEADS, ATTN_BLOCK, LANES), BF16)
        ka[:, 0:ATTN_BLOCK, :] = zero
        kb[:, 0:ATTN_BLOCK, :] = zero
        va[:, 0:ATTN_BLOCK, :] = zero
        vb[:, 0:ATTN_BLOCK, :] = zero

    mod = mod_ref[pl.ds(bi, 1), :]
    sh1 = mod[:, 0:D_MODEL]
    sc1 = mod[:, D_MODEL:2 * D_MODEL]
    gt1 = mod[:, 2 * D_MODEL:3 * D_MODEL]
    lane = lax.broadcasted_iota(jnp.int32, (sub, LANES), 1)
    low = lane < HEAD_DIM
    low_q = lax.broadcasted_iota(jnp.int32, (ATTN_BLOCK, LANES), 1) < HEAD_DIM
    nt_dims = (((1,), (1,)), ((), ()))
    st = [dict(ro=i * sub) for i in range(tm // sub)]

    def norm(c):
        c["x"] = x_ref[c["ro"]:c["ro"] + sub, :]
        c["h"] = _rms_mod(c["x"], g1_ref[...], sc1, sh1).astype(BF16)

    def proj(c):
        h = c["h"]
        c["u"] = jnp.dot(h, win_ref[:, OFF_U:OFF_Q], preferred_element_type=F32)
        q = jnp.dot(h, win_ref[:, OFF_Q:OFF_K], preferred_element_type=F32)
        c["qb"] = (q * (HEAD_DIM ** -0.5)).astype(BF16)
        kv = jnp.dot(h, win_ref[:, OFF_K:OFF_GA], preferred_element_type=F32)
        c["k"] = kv[:, 0:KV_WIDTH]
        c["v"] = kv[:, KV_WIDTH:2 * KV_WIDTH]

    def pool_sums(c):
        ro, u = c["ro"], c["u"]
        ubuf[HIST + ro:HIST + ro + sub, :] = u
        pos = t * tm + ro + lax.broadcasted_iota(jnp.int32, (sub, 1), 0)
        cur = ubuf[ro:ro + HIST + sub, :]
        d, w = [], 1
        for g, wg in enumerate(POOL_WINDOWS):
            while w < wg:
                cur = cur + pltpu.roll(cur, w, axis=0)
                w *= 2
            ug = u[:, g * POOL_GROUP:(g + 1) * POOL_GROUP]
            cnt = jnp.minimum(wg, pos + 1).astype(F32)
            d.append((cur[HIST:, 0:POOL_GROUP] / cnt - ug).astype(BF16))
            if g + 1 < len(POOL_WINDOWS):
                cur = cur[:, POOL_GROUP:]
        c["d"] = d

    def kv_store(c):
        r = ATTN_BLOCK + c["ro"]
        k, v = c["k"], c["v"]
        kr = pltpu.roll(k, HEAD_DIM, axis=1)
        vr = pltpu.roll(v, HEAD_DIM, axis=1)
        ka[0, r:r + sub, :] = jnp.where(low, k, 0.0).astype(BF16)
        kb[0, r:r + sub, :] = jnp.where(low, 0.0, kr).astype(BF16)
        ka[1, r:r + sub, :] = jnp.where(low, kr, 0.0).astype(BF16)
        kb[1, r:r + sub, :] = jnp.where(low, 0.0, k).astype(BF16)
        va[0, r:r + sub, :] = jnp.where(low, v, 0.0).astype(BF16)
        vb[0, r:r + sub, :] = jnp.where(low, 0.0, vr).astype(BF16)
        va[1, r:r + sub, :] = jnp.where(low, vr, 0.0).astype(BF16)
        vb[1, r:r + sub, :] = jnp.where(low, 0.0, v).astype(BF16)

    def pool_proj(c):
        y = jnp.concatenate(
            [jnp.dot(c["d"][g], wpool_ref[g], preferred_element_type=F32)
             for g in range(len(POOL_WINDOWS))], axis=1)
        c["br_a"] = _bdot(y * pscale_ref[...], wa_ref[...])

    def units(c):
        j0 = c["ro"] // ATTN_BLOCK
        return [(j, hk, pr) for j in range(j0, j0 + sub // ATTN_BLOCK)
                for hk in range(N_KV_HEADS) for pr in range(GQA_GROUP // 2)]

    def scores(c):
        sc = {}
        for (j, hk, pr) in units(c):
            r0 = j * ATTN_BLOCK
            c0 = (hk * (GQA_GROUP // 2) + pr) * LANES
            q2 = c["qb"][r0 - c["ro"]:r0 - c["ro"] + ATTN_BLOCK, c0:c0 + LANES]
            ks = (ka[hk, r0:r0 + 2 * ATTN_BLOCK, :], kb[hk, r0:r0 + 2 * ATTN_BLOCK, :])
            for e in range(2):
                sc[(j, hk, pr, e)] = lax.dot_general(q2, ks[e], nt_dims,
                                                     preferred_element_type=F32)
        c["s"] = sc

    def gate_a(c):
        c["ga"] = jnp.dot(c["h"], win_ref[:, OFF_GA:OFF_GB], preferred_element_type=F32)

    def gate_b(c):
        c["gb"] = jnp.dot(c["h"], win_ref[:, OFF_GB:IN_WIDTH], preferred_element_type=F32)

    def softmax(c):
        probs, dens = {}, {}
        for (j, hk, pr) in units(c):
            for e in range(2):
                head = hk * GQA_GROUP + pr * 2 + e
                if j == 0:
                    bias = jnp.where(t == 0, bias_ref[1, head], bias_ref[0, head])
                else:
                    bias = bias_ref[0, head]
                s = c["s"][(j, hk, pr, e)] + bias
                sink = sinks_ref[layer, head]
                m = jnp.maximum(jnp.max(s, axis=-1, keepdims=True), sink)
                p = jnp.exp(s - m)
                dens[(j, hk, pr, e)] = jnp.sum(p, axis=-1, keepdims=True) + jnp.exp(sink - m)
                probs[(j, hk, pr, e)] = p.astype(BF16)
        c["p"], c["den"] = probs, dens

    def values(c):
        for (j, hk, pr) in units(c):
            r0 = j * ATTN_BLOCK
            c0 = (hk * (GQA_GROUP // 2) + pr) * LANES
            vs = (va[hk, r0:r0 + 2 * ATTN_BLOCK, :], vb[hk, r0:r0 + 2 * ATTN_BLOCK, :])
            o2 = (jnp.dot(c["p"][(j, hk, pr, 0)], vs[0], preferred_element_type=F32)
                  + jnp.dot(c["p"][(j, hk, pr, 1)], vs[1], preferred_element_type=F32))
            den = jnp.where(low_q, c["den"][(j, hk, pr, 0)], c["den"][(j, hk, pr, 1)])
            obuf[r0:r0 + ATTN_BLOCK, c0:c0 + LANES] = o2 / den

    def tail(c):
        ro = c["ro"]
        br_b = _bdot(obuf[ro:ro + sub, :], wb_ref[...])
        merged = jax.nn.sigmoid(c["ga"]) * c["br_a"] + jax.nn.sigmoid(c["gb"]) * br_b
        xo_ref[ro:ro + sub, :] = c["x"] + gt1 * _bdot(merged, wout_ref[...])

    stages = [
        (norm,),
        (proj,),
        (pool_sums, kv_store),
        (pool_proj, scores, gate_a),
        (softmax,),
        (values, gate_b),
        (tail,),
    ]
    order = sorted((2 * s + 3 * i, i, s) for i in range(len(st)) for s in range(len(stages)))
    for _, i, s in order:
        for fn in stages[s]:
            fn(st[i])

    @pl.when(t == nt - 1)
    def _state():
        nk_ref[...] = st[-1]["k"][sub - WINDOW:, :]
        nv_ref[...] = st[-1]["v"][sub - WINDOW:, :]
        npool_ref[...] = ubuf[HIST + tm - POOL_PAD:HIST + tm, :]

    ubuf[0:HIST, :] = ubuf[tm:tm + HIST, :]
    for buf in (ka, kb, va, vb):
        buf[:, 0:ATTN_BLOCK, :] = buf[:, tm:tm + ATTN_BLOCK, :]


def _pmix_call(x, mod_p, g1, win, wpool, pscale, wa, wb, wout, sinks, bias, l, cvt_jobs):
    b, t, _ = x.shape
    tm = MIX_ROWS
    nt = t // tm
    cvt_in, cvt_out, cvt_shapes = _convert_specs(cvt_jobs, b * nt, lambda i, j, s: i * nt + j)
    grid_spec = pltpu.PrefetchScalarGridSpec(
        num_scalar_prefetch=1,
        grid=(b, nt),
        in_specs=[
            pl.BlockSpec((None, tm, D_MODEL), lambda i, j, s: (i, j, 0)),
            pl.BlockSpec((None, b, MOD_SLAB), lambda i, j, s: (l, 0, 0),
                         pipeline_mode=pl.Buffered(1)),
            _layer_spec((1, D_MODEL), l),
            _weight_spec(win, (D_MODEL, IN_WIDTH), l),
            _weight_spec(wpool, (len(POOL_WINDOWS), POOL_GROUP, POOL_GROUP), l),
            _layer_spec((1, POOL_WIDTH), l),
            _weight_spec(wa, (POOL_WIDTH, D_MODEL), l),
            _weight_spec(wb, (ATTN_WIDTH, D_MODEL), l),
            _weight_spec(wout, (D_MODEL, D_MODEL), l),
            _const_spec((2, N_HEADS, ATTN_BLOCK, 2 * ATTN_BLOCK)),
        ] + cvt_in,
        out_specs=[
            pl.BlockSpec((None, tm, D_MODEL), lambda i, j, s: (i, j, 0)),
            pl.BlockSpec((None, WINDOW, KV_WIDTH), lambda i, j, s: (i, 0, 0)),
            pl.BlockSpec((None, WINDOW, KV_WIDTH), lambda i, j, s: (i, 0, 0)),
            pl.BlockSpec((None, POOL_PAD, POOL_WIDTH), lambda i, j, s: (i, 0, 0)),
        ] + cvt_out,
        scratch_shapes=[
            pltpu.VMEM((HIST + tm, POOL_WIDTH), F32),
            pltpu.VMEM((N_KV_HEADS, ATTN_BLOCK + tm, LANES), BF16),
            pltpu.VMEM((N_KV_HEADS, ATTN_BLOCK + tm, LANES), BF16),
            pltpu.VMEM((N_KV_HEADS, ATTN_BLOCK + tm, LANES), BF16),
            pltpu.VMEM((N_KV_HEADS, ATTN_BLOCK + tm, LANES), BF16),
            pltpu.VMEM((tm, ATTN_WIDTH), F32),
        ],
    )
    return pl.pallas_call(
        functools.partial(_pmix_kernel, layer=l, n_cvt=len(cvt_jobs)),
        grid_spec=grid_spec,
        out_shape=[
            jax.ShapeDtypeStruct((b, t, D_MODEL), F32),
            jax.ShapeDtypeStruct((b, WINDOW, KV_WIDTH), F32),
            jax.ShapeDtypeStruct((b, WINDOW, KV_WIDTH), F32),
            jax.ShapeDtypeStruct((b, POOL_PAD, POOL_WIDTH), F32),
        ] + cvt_shapes,
        compiler_params=pltpu.CompilerParams(
            dimension_semantics=("arbitrary", "arbitrary"),
            vmem_limit_bytes=VMEM_LIMIT),
        name="prompt_mixer",
    )(sinks, x, mod_p, g1, win, wpool, pscale, wa, wb, wout, bias, *[w for w, _ in cvt_jobs])


def _ffn_kernel(xp_ref, xs_ref, modp_ref, mods_ref, g2_ref, wup_ref, wdn_ref, fg_ref,
                *rest, final, n_prompt, tiles_per_row, reps, n_cvt):
    cvt_in, (op_ref, os_ref), cvt_out = rest[:n_cvt], rest[n_cvt:n_cvt + 2], rest[n_cvt + 2:]
    _convert_slabs(cvt_in, cvt_out)
    i = pl.program_id(0)

    def ffn(x, sh2, sc2, gt2):
        h2 = _rms_mod(x, g2_ref[...], sc2, sh2)
        ff = _bdot(h2, wup_ref[...])
        ff = jnp.square(jnp.maximum(ff, 0.0))
        y = x + gt2 * _bdot(ff, wdn_ref[...])
        if final:
            ms = jnp.mean(y * y, axis=-1, keepdims=True)
            y = y * lax.rsqrt(ms + RMS_EPS) * fg_ref[...]
        return y

    @pl.when(i < n_prompt)
    def _prompt():
        mod = modp_ref[pl.ds(i // tiles_per_row, 1), :]
        op_ref[...] = ffn(xp_ref[...], mod[:, 0:D_MODEL], mod[:, D_MODEL:2 * D_MODEL],
                          mod[:, 2 * D_MODEL:3 * D_MODEL])

    @pl.when(i == n_prompt)
    def _sample():
        mod = jnp.concatenate([mods_ref[...]] * reps, axis=0)
        os_ref[...] = ffn(xs_ref[...], mod[:, 0:D_MODEL], mod[:, D_MODEL:2 * D_MODEL],
                          mod[:, 2 * D_MODEL:3 * D_MODEL])


def _ffn_call(xp, xs, mod_p, mod_s, g2, wup, wdn, fg, l, final, cvt_jobs):
    b, t, _ = xp.shape
    ms = xs.shape[0]
    nb = mod_s.shape[1]
    tm = FFN_ROWS
    tpr = t // tm
    n_prompt = b * tpr
    last = n_prompt - 1

    def p_idx(i):
        ii = jnp.minimum(i, last)
        return (ii // tpr, ii % tpr, 0)

    cvt_in, cvt_out, cvt_shapes = _convert_specs(cvt_jobs, n_prompt, lambda i: jnp.minimum(i, last))
    return pl.pallas_call(
        functools.partial(_ffn_kernel, final=final, n_prompt=n_prompt, tiles_per_row=tpr,
                          reps=ms // nb, n_cvt=len(cvt_jobs)),
        grid=(n_prompt + 1,),
        in_specs=[
            pl.BlockSpec((None, tm, D_MODEL), p_idx),
            _const_spec((ms, D_MODEL)),
            pl.BlockSpec((None, b, MOD_SLAB), lambda i: (l, 0, 1), pipeline_mode=pl.Buffered(1)),
            pl.BlockSpec((None, nb, MOD_SLAB), lambda i: (l, 0, 1), pipeline_mode=pl.Buffered(1)),
            _layer_spec((1, D_MODEL), l),
            _weight_spec(wup, (D_MODEL, D_FF), l),
            _weight_spec(wdn, (D_FF, D_MODEL), l),
            _const_spec((1, D_MODEL)),
        ] + cvt_in,
        out_specs=[
            pl.BlockSpec((None, tm, D_MODEL), p_idx),
            _whole_out_spec((ms, D_MODEL)),
        ] + cvt_out,
        out_shape=[jax.ShapeDtypeStruct((b, t, D_MODEL), F32),
                   jax.ShapeDtypeStruct((ms, D_MODEL), F32)] + cvt_shapes,
        compiler_params=pltpu.CompilerParams(
            dimension_semantics=("arbitrary",),
            vmem_limit_bytes=VMEM_LIMIT),
        name="ffn_final" if final else "ffn",
    )(xp, xs, mod_p, mod_s, g2, wup, wdn, fg, *[w for w, _ in cvt_jobs])


def _sproj_kernel(x_ref, mod_ref, g1_ref, win_ref, wpool_ref, pscale_ref, wa_ref, pre_ref,
                  q_ref, k_ref, v_ref, ma_ref, sgb_ref, npool_ref, *, nb, tn):
    x = x_ref[...]
    mod = jnp.concatenate([mod_ref[:, 0:2 * D_MODEL]] * tn, axis=0)
    sh1 = mod[:, 0:D_MODEL]
    sc1 = mod[:, D_MODEL:2 * D_MODEL]
    h = _rms_mod(x, g1_ref[...], sc1, sh1).astype(BF16)

    u = jnp.dot(h, win_ref[:, OFF_U:OFF_Q], preferred_element_type=F32)
    def up(r):
        if r < POOL_PAD:
            return pre_ref[r]
        r -= POOL_PAD
        return u[r * nb:(r + 1) * nb, :]

    d_rows = []
    for tt in range(tn):
        parts = []
        for g, w in enumerate(POOL_WINDOWS):
            lo = g * POOL_GROUP
            acc = up(POOL_PAD + tt)[:, lo:lo + POOL_GROUP]
            for s in range(1, w):
                acc = acc + up(POOL_PAD + tt - s)[:, lo:lo + POOL_GROUP]
            cnt = float(min(w, PAST_LEN + tt + 1))
            parts.append(acc / cnt - up(POOL_PAD + tt)[:, lo:lo + POOL_GROUP])
        d_rows.append(parts)
    y = jnp.concatenate(
        [_bdot(jnp.concatenate([d_rows[tt][g] for tt in range(tn)], axis=0), wpool_ref[g])
         for g in range(len(POOL_WINDOWS))], axis=1)
    y = y * pscale_ref[...]
    br_a = _bdot(y, wa_ref[...])
    for r in range(POOL_PAD):
        npool_ref[r] = up(r + tn)

    q = jnp.dot(h, win_ref[:, OFF_Q:OFF_K], preferred_element_type=F32)
    q_ref[...] = q * (HEAD_DIM ** -0.5)
    kv = jnp.dot(h, win_ref[:, OFF_K:OFF_GA], preferred_element_type=F32)
    k_ref[...] = kv[:, 0:KV_WIDTH]
    v_ref[...] = kv[:, KV_WIDTH:2 * KV_WIDTH]
    ga = jnp.dot(h, win_ref[:, OFF_GA:OFF_GB], preferred_element_type=F32)
    ma_ref[...] = jax.nn.sigmoid(ga) * br_a
    gb = jnp.dot(h, win_ref[:, OFF_GB:IN_WIDTH], preferred_element_type=F32)
    sgb_ref[...] = jax.nn.sigmoid(gb)


def _sproj_call(x, mod_s, g1, win, wpool, pscale, wa, prefix_t, l, nb, tn):
    m = x.shape[0]
    shapes = [
        jax.ShapeDtypeStruct((m, ATTN_WIDTH), F32),
        jax.ShapeDtypeStruct((m, KV_WIDTH), F32),
        jax.ShapeDtypeStruct((m, KV_WIDTH), F32),
        jax.ShapeDtypeStruct((m, D_MODEL), F32),
        jax.ShapeDtypeStruct((m, D_MODEL), F32),
        jax.ShapeDtypeStruct((POOL_PAD, nb, POOL_WIDTH), F32),
    ]
    return pl.pallas_call(
        functools.partial(_sproj_kernel, nb=nb, tn=tn),
        grid=(1,),
        in_specs=[
            _const_spec(x.shape),
            pl.BlockSpec((None, nb, MOD_SLAB), lambda i: (l, 0, 0), pipeline_mode=pl.Buffered(1)),
            _layer_spec((1, D_MODEL), l),
            _weight_spec(win, (D_MODEL, IN_WIDTH), l),
            _weight_spec(wpool, (len(POOL_WINDOWS), POOL_GROUP, POOL_GROUP), l),
            _layer_spec((1, POOL_WIDTH), l),
            _weight_spec(wa, (POOL_WIDTH, D_MODEL), l),
            _layer_spec((POOL_PAD, nb, POOL_WIDTH), l),
        ],
        out_specs=[_whole_out_spec(s.shape) for s in shapes],
        out_shape=shapes,
        compiler_params=pltpu.CompilerParams(
            dimension_semantics=("arbitrary",),
            vmem_limit_bytes=VMEM_LIMIT),
        name="sample_proj",
    )(x, mod_s, g1, win, wpool, pscale, wa, prefix_t)


def _sattn_kernel(sinks_ref, slopes_ref, q_ref, kn_ref, vn_ref, kc_ref, vc_ref, *rest,
                  layer, nb, tn, ch, nch):
    if layer:
        pk_ref, pv_ref = rest[:2]
        rest = rest[2:]
    o_ref, nk_ref, nv_ref, qt, knt, vnt, tbuf, sbuf, ot, carry = rest
    i = pl.program_id(0)
    srows = WINDOW + 8
    combos = [(tt, head) for tt in range(tn) for head in range(N_HEADS)]

    def rolled_chunk(out_ref, prev_ref, nxt):
        if layer:
            out_ref[0:layer] = prev_ref[...]
        out_ref[layer, :, 0:ch - tn, :] = carry[:, tn:ch, :]
        for tt in range(tn):
            out_ref[layer, :, ch - tn + tt, :] = nxt(tt)

    def roll_k(nxt):
        rolled_chunk(nk_ref, pk_ref if layer else None, nxt)

    def roll_v(nxt):
        rolled_chunk(nv_ref, pv_ref if layer else None, nxt)

    @pl.when(i == 0)
    def _start():
        for tt in range(tn):
            qt[tt] = q_ref[tt * nb:(tt + 1) * nb, :].T
            knt[tt] = kn_ref[tt * nb:(tt + 1) * nb, :].T
            vnt[tt] = vn_ref[tt * nb:(tt + 1) * nb, :].T

    def load_tiles(c_ref):
        keys = jnp.swapaxes(c_ref[...], 0, 1)
        for jj in range(ch):
            tbuf[jj] = keys[jj].T

    def head_rows(head):
        hk = head // GQA_GROUP
        return slice(hk * HEAD_DIM, (hk + 1) * HEAD_DIM)

    @pl.when((i >= 1) & (i < nch))
    def _roll_k():
        roll_k(lambda tt: kc_ref[:, tt, :])

    @pl.when(i == nch)
    def _roll_k_last():
        roll_k(lambda tt: kn_ref[tt * nb:(tt + 1) * nb, :])

    @pl.when((i > nch) & (i < 2 * nch))
    def _roll_v():
        roll_v(lambda tt: vc_ref[:, tt, :])

    @pl.when(i == 2 * nch)
    def _roll_v_last():
        roll_v(lambda tt: vn_ref[tt * nb:(tt + 1) * nb, :])

    @pl.when(i < nch)
    def _scores():
        carry[...] = kc_ref[...]
        load_tiles(kc_ref)
        row0 = pl.multiple_of(i * ch, ch)
        for ci, (tt, head) in enumerate(combos):
            qh = qt[tt, head * HEAD_DIM:(head + 1) * HEAD_DIM, :]
            rows = [jnp.sum(qh * tbuf[jj, head_rows(head), :], axis=0, keepdims=True)
                    for jj in range(ch)]
            sbuf[ci, pl.ds(row0, ch), :] = jnp.concatenate(rows, axis=0)

    @pl.when(i == nch - 1)
    def _softmax():
        for ci, (tt, head) in enumerate(combos):
            qh = qt[tt, head * HEAD_DIM:(head + 1) * HEAD_DIM, :]
            rows = [jnp.sum(qh * knt[t2, head_rows(head), :], axis=0, keepdims=True)
                    for t2 in range(tn)]
            rows.append(jnp.zeros((8 - tn, nb), F32))
            sbuf[ci, WINDOW:srows, :] = jnp.concatenate(rows, axis=0)
        jpos = lax.broadcasted_iota(jnp.int32, (srows, nb), 0)
        for ci, (tt, head) in enumerate(combos):
            dist = tt + WINDOW - jpos
            valid = (dist >= 0) & (dist < WINDOW)
            s = sbuf[ci] - slopes_ref[head] * dist.astype(F32)
            s = jnp.where(valid, s, NEG_INF)
            sink = sinks_ref[layer, head]
            m = jnp.maximum(jnp.max(s, axis=0, keepdims=True), sink)
            p = jnp.exp(s - m)
            den = jnp.sum(p, axis=0, keepdims=True) + jnp.exp(sink - m)
            p = p / den
            sbuf[ci] = p
            acc = jnp.zeros((HEAD_DIM, nb), F32)
            for t2 in range(tt + 1):
                acc = acc + p[WINDOW + t2:WINDOW + t2 + 1, :] * vnt[t2, head_rows(head), :]
            ot[tt, head * HEAD_DIM:(head + 1) * HEAD_DIM, :] = acc

    @pl.when((i >= nch) & (i < 2 * nch))
    def _values():
        carry[...] = vc_ref[...]
        load_tiles(vc_ref)
        row0 = pl.multiple_of((i - nch) * ch, ch)
        for ci, (tt, head) in enumerate(combos):
            acc = ot[tt, head * HEAD_DIM:(head + 1) * HEAD_DIM, :]
            for jj in range(ch):
                acc = acc + sbuf[ci, pl.ds(row0 + jj, 1), :] * tbuf[jj, head_rows(head), :]
            ot[tt, head * HEAD_DIM:(head + 1) * HEAD_DIM, :] = acc

    @pl.when(i == 2 * nch)
    def _finish():
        for tt in range(tn):
            o_ref[tt * nb:(tt + 1) * nb, :] = ot[tt].T


def _sattn_call(q, kn, vn, ck, cv, sinks, slopes, rolled, l, nb, tn):
    m = q.shape[0]
    ch = SATTN_KEYS
    nch = WINDOW // ch
    last = nch - 1
    chunk = (None, nb, ch, KV_WIDTH)
    cache_shape = jax.ShapeDtypeStruct((l + 1, nb, WINDOW, KV_WIDTH), F32)
    k_lag = lambda i, a, b: (0, 0, jnp.clip(i - 1, 0, last), 0)
    v_lag = lambda i, a, b: (0, 0, jnp.clip(i - nch - 1, 0, last), 0)
    in_specs = [
        _const_spec((m, ATTN_WIDTH)),
        _const_spec((m, KV_WIDTH)),
        _const_spec((m, KV_WIDTH)),
        pl.BlockSpec(chunk, lambda i, a, b: (l, 0, jnp.minimum(i, last), 0)),
        pl.BlockSpec(chunk, lambda i, a, b: (l, 0, jnp.clip(i - nch, 0, last), 0)),
    ]
    args = [sinks, slopes, q, kn, vn, ck, cv]
    if l:
        in_specs += [pl.BlockSpec((l, nb, ch, KV_WIDTH), k_lag),
                     pl.BlockSpec((l, nb, ch, KV_WIDTH), v_lag)]
        args += list(rolled)
    grid_spec = pltpu.PrefetchScalarGridSpec(
        num_scalar_prefetch=2,
        grid=(2 * nch + 1,),
        in_specs=in_specs,
        out_specs=[
            _whole_out_spec((m, ATTN_WIDTH)),
            pl.BlockSpec((l + 1, nb, ch, KV_WIDTH), k_lag),
            pl.BlockSpec((l + 1, nb, ch, KV_WIDTH), v_lag),
        ],
        scratch_shapes=[
            pltpu.VMEM((tn, ATTN_WIDTH, nb), F32),
            pltpu.VMEM((tn, KV_WIDTH, nb), F32),
            pltpu.VMEM((tn, KV_WIDTH, nb), F32),
            pltpu.VMEM((ch, KV_WIDTH, nb), F32),
            pltpu.VMEM((tn * N_HEADS, WINDOW + 8, nb), F32),
            pltpu.VMEM((tn, ATTN_WIDTH, nb), F32),
            pltpu.VMEM((nb, ch, KV_WIDTH), F32),
        ],
    )
    return pl.pallas_call(
        functools.partial(_sattn_kernel, layer=l, nb=nb, tn=tn, ch=ch, nch=nch),
        grid_spec=grid_spec,
        out_shape=[jax.ShapeDtypeStruct((m, ATTN_WIDTH), F32), cache_shape, cache_shape],
        compiler_params=pltpu.CompilerParams(
            dimension_semantics=("arbitrary",),
            vmem_limit_bytes=VMEM_LIMIT),
        name="sample_attn",
    )(*args)


def _spost_kernel(x_ref, o_ref, ma_ref, sgb_ref, mod_ref, wb_ref, wout_ref, xo_ref, *, tn):
    gt1 = jnp.concatenate([mod_ref[...]] * tn, axis=0)
    br_b = _bdot(o_ref[...], wb_ref[...])
    merged = ma_ref[...] + sgb_ref[...] * br_b
    xo_ref[...] = x_ref[...] + gt1 * _bdot(merged, wout_ref[...])


def _spost_call(x, o, ma, sgb, mod_s, wb, wout, l, nb, tn):
    return pl.pallas_call(
        functools.partial(_spost_kernel, tn=tn),
        grid=(1,),
        in_specs=[
            _const_spec(x.shape), _const_spec(o.shape), _const_spec(ma.shape),
            _const_spec(sgb.shape),
            pl.BlockSpec((None, nb, D_MODEL), lambda i: (l, 0, 2), pipeline_mode=pl.Buffered(1)),
            _weight_spec(wb, (ATTN_WIDTH, D_MODEL), l),
            _weight_spec(wout, (D_MODEL, D_MODEL), l),
        ],
        out_specs=_whole_out_spec(x.shape),
        out_shape=jax.ShapeDtypeStruct(x.shape, F32),
        compiler_params=pltpu.CompilerParams(
            dimension_semantics=("arbitrary",),
            vmem_limit_bytes=VMEM_LIMIT),
        name="sample_post",
    )(x, o, ma, sgb, mod_s, wb, wout)


def kernel(x_prompt, x_sample, cache_k, cache_v, state_pool, c_prompt, c_sample,
           w_ada, b_ada, norm1_g, w_in, w_pool, pool_scale, attn_sinks, w_a, w_b,
           w_out, norm2_g, w_up, w_down, final_g):
    bp, tp, _ = x_prompt.shape
    nb, tn, _ = x_sample.shape
    assert tp % MIX_ROWS == 0 and tp % FFN_ROWS == 0 and tn <= 8 and nb == LANES
    assert bp % 8 == 0

    mod_p, mod_s = _mod_call(c_prompt, c_sample, w_ada, b_ada)

    w_pool2 = w_pool.reshape(DEPTH, POOL_WIDTH, POOL_GROUP)
    mix_w = [w[0].astype(BF16) for w in (w_in, w_pool2, w_a, w_b, w_out)]
    bias = jnp.asarray(_prompt_bias())
    slopes = jnp.asarray(_alibi_slopes(), F32)
    fg = final_g.reshape(1, D_MODEL)
    g1 = norm1_g.reshape(DEPTH, 1, D_MODEL)
    g2 = norm2_g.reshape(DEPTH, 1, D_MODEL)
    psc = pool_scale.reshape(DEPTH, 1, POOL_WIDTH)

    xs = x_sample.transpose(1, 0, 2).reshape(tn * nb, D_MODEL)
    ck = cache_k.reshape(DEPTH, nb, WINDOW, KV_WIDTH)
    cv = cache_v.reshape(DEPTH, nb, WINDOW, KV_WIDTH)
    prefix_t = state_pool.transpose(0, 2, 1, 3)

    xp = x_prompt
    kp, vp, pp, ps, rolled = [], [], [], [], None
    for l in range(DEPTH):
        last = l == DEPTH - 1
        win, wpool, wa, wb, wout = mix_w
        wpool = wpool.reshape(len(POOL_WINDOWS), POOL_GROUP, POOL_GROUP)
        xp, nk, nv, npool, wup, wdn = _pmix_call(xp, mod_p, g1, win, wpool, psc, wa, wb, wout,
                                                 attn_sinks, bias, l, [(w_up, l), (w_down, l)])
        kp.append(nk); vp.append(nv); pp.append(npool)

        q, kn, vn, ma, sgb, npool_s = _sproj_call(xs, mod_s, g1, win, wpool, psc, wa,
                                                  prefix_t, l, nb, tn)
        o, *rolled = _sattn_call(q, kn, vn, ck, cv, attn_sinks, slopes, rolled, l, nb, tn)
        xs = _spost_call(xs, o, ma, sgb, mod_s, wb, wout, l, nb, tn)
        ps.append(npool_s)

        nxt = [] if last else [(w, l + 1) for w in (w_in, w_pool2, w_a, w_b, w_out)]
        xp, xs, *mix_w = _ffn_call(xp, xs, mod_p, mod_s, g2, wup, wdn, fg, l, last, nxt)

    to_bt = lambda a: a.reshape(tn, nb, a.shape[-1]).transpose(1, 0, 2)

    kv_shape_p = (DEPTH, bp, WINDOW, N_KV_HEADS, HEAD_DIM)
    kv_shape_s = (DEPTH, nb, WINDOW, N_KV_HEADS, HEAD_DIM)
    return (xp,
            to_bt(xs),
            jnp.stack(kp).reshape(kv_shape_p),
            jnp.stack(vp).reshape(kv_shape_p),
            jnp.stack(pp),
            rolled[0].reshape(kv_shape_s),
            rolled[1].reshape(kv_shape_s),
            jnp.stack(ps).transpose(0, 2, 1, 3))
```

```python
import functools

import numpy as np
import jax
import jax.numpy as jnp
from jax import lax
from jax.experimental import pallas as pl
from jax.experimental.pallas import tpu as pltpu

D_MODEL = 1024
DEPTH = 2
PAST_LEN = 16384
POOL_WIDTH = D_MODEL // 2
POOL_WINDOWS = (2, 4, 8, 16)
POOL_GROUP = POOL_WIDTH // len(POOL_WINDOWS)
POOL_PAD = max(POOL_WINDOWS) - 1
N_HEADS = 8
N_KV_HEADS = 2
HEAD_DIM = 64
GQA_GROUP = N_HEADS // N_KV_HEADS
ATTN_WIDTH = N_HEADS * HEAD_DIM
KV_WIDTH = N_KV_HEADS * HEAD_DIM
WINDOW = 128
ATTN_BLOCK = 128
D_FF = 4 * D_MODEL
RMS_EPS = 1e-6
NEG_INF = -1e30

OFF_U = 0
OFF_Q = OFF_U + POOL_WIDTH
OFF_K = OFF_Q + ATTN_WIDTH
OFF_V = OFF_K + KV_WIDTH
OFF_GA = OFF_V + KV_WIDTH
OFF_GB = OFF_GA + D_MODEL
IN_WIDTH = OFF_GB + D_MODEL
MOD_SLAB = 3 * D_MODEL

LANES = 128
HIST = 16
assert all(w & (w - 1) == 0 for w in POOL_WINDOWS) and list(POOL_WINDOWS) == sorted(POOL_WINDOWS)
assert HIST >= POOL_PAD
VMEM_LIMIT = 56 * 1024 * 1024
SUB_ROWS = 256
MIX_ROWS = 4 * SUB_ROWS
FFN_ROWS = 512
SATTN_KEYS = 16

F32 = jnp.float32
BF16 = jnp.bfloat16


def _bdot(a, b):
    return jnp.dot(a.astype(BF16), b.astype(BF16), preferred_element_type=F32)


def _rms_mod(x, g, sc, sh):
    ms = jnp.mean(x * x, axis=-1, keepdims=True)
    return (x * lax.rsqrt(ms + RMS_EPS) * g) * (1.0 + sc) + sh


def _alibi_slopes():
    return 2.0 ** (-8.0 * (np.arange(N_HEADS) + 1) / N_HEADS)


def _const_spec(shape):
    nd = len(shape)
    return pl.BlockSpec(shape, lambda *_: (0,) * nd, pipeline_mode=pl.Buffered(1))


def _layer_spec(shape, l):
    nd = len(shape)
    return pl.BlockSpec((None,) + tuple(shape), lambda *_: (l,) + (0,) * nd,
                        pipeline_mode=pl.Buffered(1))


def _whole_out_spec(shape):
    nd = len(shape)
    return pl.BlockSpec(shape, lambda *_: (0,) * nd)


def _weight_spec(w, shape, l):
    return _const_spec(shape) if w.shape == tuple(shape) else _layer_spec(shape, l)


def _convert_specs(jobs, n_steps, step_of):
    in_specs, out_specs, out_shapes = [], [], []
    for w, l in jobs:
        _, r, c = w.shape
        rows = r // n_steps
        assert rows * n_steps == r and rows % 16 == 0, (w.shape, n_steps)
        in_specs.append(pl.BlockSpec((None, rows, c), lambda *a, l=l: (l, step_of(*a), 0)))
        out_specs.append(pl.BlockSpec((rows, c), lambda *a: (step_of(*a), 0)))
        out_shapes.append(jax.ShapeDtypeStruct((r, c), BF16))
    return in_specs, out_specs, out_shapes


def _convert_slabs(src_refs, dst_refs):
    for src, dst in zip(src_refs, dst_refs):
        dst[...] = src[...].astype(BF16)


def _mod_kernel(cp_ref, cs_ref, w_ref, b_ref, op_ref, os_ref):
    w = w_ref[...].astype(BF16)
    b = b_ref[...]
    for c_ref, o_ref in ((cp_ref, op_ref), (cs_ref, os_ref)):
        c = c_ref[...]
        s = (c * jax.nn.sigmoid(c)).astype(BF16)
        o_ref[...] = jnp.dot(s, w, preferred_element_type=F32) + b


def _mod_call(c_p, c_s, w_ada, b_ada):
    tn = 2048
    n = 6 * D_MODEL
    mp, ms = c_p.shape[0], c_s.shape[0]
    return pl.pallas_call(
        _mod_kernel,
        grid=(DEPTH, n // tn),
        in_specs=[
            _const_spec((mp, D_MODEL)),
            _const_spec((ms, D_MODEL)),
            pl.BlockSpec((None, D_MODEL, tn), lambda l, j: (l, 0, j)),
            pl.BlockSpec((None, 1, tn), lambda l, j: (l, 0, j)),
        ],
        out_specs=[
            pl.BlockSpec((None, mp, tn), lambda l, j: (l, 0, j)),
            pl.BlockSpec((None, ms, tn), lambda l, j: (l, 0, j)),
        ],
        out_shape=[jax.ShapeDtypeStruct((DEPTH, mp, n), F32),
                   jax.ShapeDtypeStruct((DEPTH, ms, n), F32)],
        compiler_params=pltpu.CompilerParams(
            dimension_semantics=("arbitrary", "arbitrary"),
            vmem_limit_bytes=VMEM_LIMIT),
        name="adaln_mod",
    )(c_p, c_s, w_ada, b_ada.reshape(DEPTH, 1, n))


def _prompt_bias():
    i = np.arange(ATTN_BLOCK)[:, None]
    j = np.arange(2 * ATTN_BLOCK)[None, :]
    dist = i + ATTN_BLOCK - j
    valid = (dist >= 0) & (dist < WINDOW)
    valid_first = valid & (j >= ATTN_BLOCK)
    sl = _alibi_slopes()[:, None, None]
    b = np.where(valid[None], -sl * dist[None], NEG_INF)
    b0 = np.where(valid_first[None], -sl * dist[None], NEG_INF)
    return np.stack([b, b0]).astype(np.float32)


def _pmix_kernel(sinks_ref, x_ref, mod_ref, g1_ref, win_ref, wpool_ref, pscale_ref,
                 wa_ref, wb_ref, wout_ref, bias_ref, *rest, layer, n_cvt):
    cvt_in, rest = rest[:n_cvt], rest[n_cvt:]
    (xo_ref, nk_ref, nv_ref, npool_ref), rest = rest[:4], rest[4:]
    cvt_out, (ubuf, ka, kb, va, vb, obuf) = rest[:n_cvt], rest[n_cvt:]
    _convert_slabs(cvt_in, cvt_out)
    tm, sub = MIX_ROWS, SUB_ROWS
    bi = pl.program_id(0)
    t = pl.program_id(1)
    nt = pl.num_programs(1)

    @pl.when(t == 0)
    def _init():
        ubuf[0:HIST, :] = jnp.zeros((HIST, POOL_WIDTH), F32)
        zero = jnp.zeros((N_KV_HEADS, ATTN_BLOCK, LANES), BF16)
        ka[:, 0:ATTN_BLOCK, :] = zero
        kb[:, 0:ATTN_BLOCK, :] = zero
        va[:, 0:ATTN_BLOCK, :] = zero
        vb[:, 0:ATTN_BLOCK, :] = zero

    mod = mod_ref[pl.ds(bi, 1), :]
    sh1 = mod[:, 0:D_MODEL]
    sc1 = mod[:, D_MODEL:2 * D_MODEL]
    gt1 = mod[:, 2 * D_MODEL:3 * D_MODEL]
    lane = lax.broadcasted_iota(jnp.int32, (sub, LANES), 1)
    low = lane < HEAD_DIM
    low_q = lax.broadcasted_iota(jnp.int32, (ATTN_BLOCK, LANES), 1) < HEAD_DIM
    nt_dims = (((1,), (1,)), ((), ()))
    st = [dict(ro=i * sub) for i in range(tm // sub)]

    def norm(c):
        c["x"] = x_ref[c["ro"]:c["ro"] + sub, :]
        c["h"] = _rms_mod(c["x"], g1_ref[...], sc1, sh1).astype(BF16)

    def proj(c):
        h = c["h"]
        c["u"] = jnp.dot(h, win_ref[:, OFF_U:OFF_Q], preferred_element_type=F32)
        q = jnp.dot(h, win_ref[:, OFF_Q:OFF_K], preferred_element_type=F32)
        c["qb"] = (q * (HEAD_DIM ** -0.5)).astype(BF16)
        kv = jnp.dot(h, win_ref[:, OFF_K:OFF_GA], preferred_element_type=F32)
        c["k"] = kv[:, 0:KV_WIDTH]
        c["v"] = kv[:, KV_WIDTH:2 * KV_WIDTH]

    def pool_sums(c):
        ro, u = c["ro"], c["u"]
        ubuf[HIST + ro:HIST + ro + sub, :] = u
        pos = t * tm + ro + lax.broadcasted_iota(jnp.int32, (sub, 1), 0)
        cur = ubuf[ro:ro + HIST + sub, :]
        d, w = [], 1
        for g, wg in enumerate(POOL_WINDOWS):
            while w < wg:
                cur = cur + pltpu.roll(cur, w, axis=0)
                w *= 2
            ug = u[:, g * POOL_GROUP:(g + 1) * POOL_GROUP]
            cnt = jnp.minimum(wg, pos + 1).astype(F32)
            d.append((cur[HIST:, 0:POOL_GROUP] / cnt - ug).astype(BF16))
            if g + 1 < len(POOL_WINDOWS):
                cur = cur[:, POOL_GROUP:]
        c["d"] = d

    def kv_store(c):
        r = ATTN_BLOCK + c["ro"]
        k, v = c["k"], c["v"]
        kr = pltpu.roll(k, HEAD_DIM, axis=1)
        vr = pltpu.roll(v, HEAD_DIM, axis=1)
        ka[0, r:r + sub, :] = jnp.where(low, k, 0.0).astype(BF16)
        kb[0, r:r + sub, :] = jnp.where(low, 0.0, kr).astype(BF16)
        ka[1, r:r + sub, :] = jnp.where(low, kr, 0.0).astype(BF16)
        kb[1, r:r + sub, :] = jnp.where(low, 0.0, k).astype(BF16)
        va[0, r:r + sub, :] = jnp.where(low, v, 0.0).astype(BF16)
        vb[0, r:r + sub, :] = jnp.where(low, 0.0, vr).astype(BF16)
        va[1, r:r + sub, :] = jnp.where(low, vr, 0.0).astype(BF16)
        vb[1, r:r + sub, :] = jnp.where(low, 0.0, v).astype(BF16)

    def pool_proj(c):
        y = jnp.concatenate(
            [jnp.dot(c["d"][g], wpool_ref[g], preferred_element_type=F32)
             for g in range(len(POOL_WINDOWS))], axis=1)
        c["br_a"] = _bdot(y * pscale_ref[...], wa_ref[...])

    def units(c):
        j0 = c["ro"] // ATTN_BLOCK
        return [(j, hk, pr) for j in range(j0, j0 + sub // ATTN_BLOCK)
                for hk in range(N_KV_HEADS) for pr in range(GQA_GROUP // 2)]

    def scores(c):
        sc = {}
        for (j, hk, pr) in units(c):
            r0 = j * ATTN_BLOCK
            c0 = (hk * (GQA_GROUP // 2) + pr) * LANES
            q2 = c["qb"][r0 - c["ro"]:r0 - c["ro"] + ATTN_BLOCK, c0:c0 + LANES]
            ks = (ka[hk, r0:r0 + 2 * ATTN_BLOCK, :], kb[hk, r0:r0 + 2 * ATTN_BLOCK, :])
            for e in range(2):
                sc[(j, hk, pr, e)] = lax.dot_general(q2, ks[e], nt_dims,
                                                     preferred_element_type=F32)
        c["s"] = sc

    def gate_a(c):
        c["ga"] = jnp.dot(c["h"], win_ref[:, OFF_GA:OFF_GB], preferred_element_type=F32)

    def gate_b(c):
        c["gb"] = jnp.dot(c["h"], win_ref[:, OFF_GB:IN_WIDTH], preferred_element_type=F32)

    def softmax(c):
        probs, dens = {}, {}
        for (j, hk, pr) in units(c):
            for e in range(2):
                head = hk * GQA_GROUP + pr * 2 + e
                if j == 0:
                    bias = jnp.where(t == 0, bias_ref[1, head], bias_ref[0, head])
                else:
                    bias = bias_ref[0, head]
                s = c["s"][(j, hk, pr, e)] + bias
                sink = sinks_ref[layer, head]
                m = jnp.maximum(jnp.max(s, axis=-1, keepdims=True), sink)
                p = jnp.exp(s - m)
                dens[(j, hk, pr, e)] = jnp.sum(p, axis=-1, keepdims=True) + jnp.exp(sink - m)
                probs[(j, hk, pr, e)] = p.astype(BF16)
        c["p"], c["den"] = probs, dens

    def values(c):
        for (j, hk, pr) in units(c):
            r0 = j * ATTN_BLOCK
            c0 = (hk * (GQA_GROUP // 2) + pr) * LANES
            vs = (va[hk, r0:r0 + 2 * ATTN_BLOCK, :], vb[hk, r0:r0 + 2 * ATTN_BLOCK, :])
            o2 = (jnp.dot(c["p"][(j, hk, pr, 0)], vs[0], preferred_element_type=F32)
                  + jnp.dot(c["p"][(j, hk, pr, 1)], vs[1], preferred_element_type=F32))
            den = jnp.where(low_q, c["den"][(j, hk, pr, 0)], c["den"][(j, hk, pr, 1)])
            obuf[r0:r0 + ATTN_BLOCK, c0:c0 + LANES] = o2 / den

    def tail(c):
        ro = c["ro"]
        br_b = _bdot(obuf[ro:ro + sub, :], wb_ref[...])
        merged = jax.nn.sigmoid(c["ga"]) * c["br_a"] + jax.nn.sigmoid(c["gb"]) * br_b
        xo_ref[ro:ro + sub, :] = c["x"] + gt1 * _bdot(merged, wout_ref[...])

    stages = [
        (norm,),
        (proj,),
        (pool_sums, kv_store),
        (pool_proj, scores, gate_a),
        (softmax,),
        (values, gate_b),
        (tail,),
    ]
    order = sorted((2 * s + 3 * i, i, s) for i in range(len(st)) for s in range(len(stages)))
    for _, i, s in order:
        for fn in stages[s]:
            fn(st[i])

    @pl.when(t == nt - 1)
    def _state():
        nk_ref[...] = st[-1]["k"][sub - WINDOW:, :]
        nv_ref[...] = st[-1]["v"][sub - WINDOW:, :]
        npool_ref[...] = ubuf[HIST + tm - POOL_PAD:HIST + tm, :]

    ubuf[0:HIST, :] = ubuf[tm:tm + HIST, :]
    for buf in (ka, kb, va, vb):
        buf[:, 0:ATTN_BLOCK, :] = buf[:, tm:tm + ATTN_BLOCK, :]


def _pmix_call(x, mod_p, g1, win, wpool, pscale, wa, wb, wout, sinks, bias, l, cvt_jobs):
    b, t, _ = x.shape
    tm = MIX_ROWS
    nt = t // tm
    cvt_in, cvt_out, cvt_shapes = _convert_specs(cvt_jobs, b * nt, lambda i, j, s: i * nt + j)
    grid_spec = pltpu.PrefetchScalarGridSpec(
        num_scalar_prefetch=1,
        grid=(b, nt),
        in_specs=[
            pl.BlockSpec((None, tm, D_MODEL), lambda i, j, s: (i, j, 0)),
            pl.BlockSpec((None, b, MOD_SLAB), lambda i, j, s: (l, 0, 0),
                         pipeline_mode=pl.Buffered(1)),
            _layer_spec((1, D_MODEL), l),
            _weight_spec(win, (D_MODEL, IN_WIDTH), l),
            _weight_spec(wpool, (len(POOL_WINDOWS), POOL_GROUP, POOL_GROUP), l),
            _layer_spec((1, POOL_WIDTH), l),
            _weight_spec(wa, (POOL_WIDTH, D_MODEL), l),
            _weight_spec(wb, (ATTN_WIDTH, D_MODEL), l),
            _weight_spec(wout, (D_MODEL, D_MODEL), l),
            _const_spec((2, N_HEADS, ATTN_BLOCK, 2 * ATTN_BLOCK)),
        ] + cvt_in,
        out_specs=[
            pl.BlockSpec((None, tm, D_MODEL), lambda i, j, s: (i, j, 0)),
            pl.BlockSpec((None, WINDOW, KV_WIDTH), lambda i, j, s: (i, 0, 0)),
            pl.BlockSpec((None, WINDOW, KV_WIDTH), lambda i, j, s: (i, 0, 0)),
            pl.BlockSpec((None, POOL_PAD, POOL_WIDTH), lambda i, j, s: (i, 0, 0)),
        ] + cvt_out,
        scratch_shapes=[
            pltpu.VMEM((HIST + tm, POOL_WIDTH), F32),
            pltpu.VMEM((N_KV_HEADS, ATTN_BLOCK + tm, LANES), BF16),
            pltpu.VMEM((N_KV_HEADS, ATTN_BLOCK + tm, LANES), BF16),
            pltpu.VMEM((N_KV_HEADS, ATTN_BLOCK + tm, LANES), BF16),
            pltpu.VMEM((N_KV_HEADS, ATTN_BLOCK + tm, LANES), BF16),
            pltpu.VMEM((tm, ATTN_WIDTH), F32),
        ],
    )
    return pl.pallas_call(
        functools.partial(_pmix_kernel, layer=l, n_cvt=len(cvt_jobs)),
        grid_spec=grid_spec,
        out_shape=[
            jax.ShapeDtypeStruct((b, t, D_MODEL), F32),
            jax.ShapeDtypeStruct((b, WINDOW, KV_WIDTH), F32),
            jax.ShapeDtypeStruct((b, WINDOW, KV_WIDTH), F32),
            jax.ShapeDtypeStruct((b, POOL_PAD, POOL_WIDTH), F32),
        ] + cvt_shapes,
        compiler_params=pltpu.CompilerParams(
            dimension_semantics=("arbitrary", "arbitrary"),
            vmem_limit_bytes=VMEM_LIMIT),
        name="prompt_mixer",
    )(sinks, x, mod_p, g1, win, wpool, pscale, wa, wb, wout, bias, *[w for w, _ in cvt_jobs])


def _ffn_kernel(xp_ref, xs_ref, modp_ref, mods_ref, g2_ref, wup_ref, wdn_ref, fg_ref,
                *rest, final, n_prompt, tiles_per_row, reps, n_cvt):
    cvt_in, (op_ref, os_ref), cvt_out = rest[:n_cvt], rest[n_cvt:n_cvt + 2], rest[n_cvt + 2:]
    _convert_slabs(cvt_in, cvt_out)
    i = pl.program_id(0)

    def ffn(x, sh2, sc2, gt2):
        h2 = _rms_mod(x, g2_ref[...], sc2, sh2)
        ff = _bdot(h2, wup_ref[...])
        ff = jnp.square(jnp.maximum(ff, 0.0))
        y = x + gt2 * _bdot(ff, wdn_ref[...])
        if final:
            ms = jnp.mean(y * y, axis=-1, keepdims=True)
            y = y * lax.rsqrt(ms + RMS_EPS) * fg_ref[...]
        return y

    @pl.when(i < n_prompt)
    def _prompt():
        mod = modp_ref[pl.ds(i // tiles_per_row, 1), :]
        op_ref[...] = ffn(xp_ref[...], mod[:, 0:D_MODEL], mod[:, D_MODEL:2 * D_MODEL],
                          mod[:, 2 * D_MODEL:3 * D_MODEL])

    @pl.when(i == n_prompt)
    def _sample():
        mod = jnp.concatenate([mods_ref[...]] * reps, axis=0)
        os_ref[...] = ffn(xs_ref[...], mod[:, 0:D_MODEL], mod[:, D_MODEL:2 * D_MODEL],
                          mod[:, 2 * D_MODEL:3 * D_MODEL])


def _ffn_call(xp, xs, mod_p, mod_s, g2, wup, wdn, fg, l, final, cvt_jobs):
    b, t, _ = xp.shape
    ms = xs.shape[0]
    nb = mod_s.shape[1]
    tm = FFN_ROWS
    tpr = t // tm
    n_prompt = b * tpr
    last = n_prompt - 1

    def p_idx(i):
        ii = jnp.minimum(i, last)
        return (ii // tpr, ii % tpr, 0)

    cvt_in, cvt_out, cvt_shapes = _convert_specs(cvt_jobs, n_prompt, lambda i: jnp.minimum(i, last))
    return pl.pallas_call(
        functools.partial(_ffn_kernel, final=final, n_prompt=n_prompt, tiles_per_row=tpr,
                          reps=ms // nb, n_cvt=len(cvt_jobs)),
        grid=(n_prompt + 1,),
        in_specs=[
            pl.BlockSpec((None, tm, D_MODEL), p_idx),
            _const_spec((ms, D_MODEL)),
            pl.BlockSpec((None, b, MOD_SLAB), lambda i: (l, 0, 1), pipeline_mode=pl.Buffered(1)),
            pl.BlockSpec((None, nb, MOD_SLAB), lambda i: (l, 0, 1), pipeline_mode=pl.Buffered(1)),
            _layer_spec((1, D_MODEL), l),
            _weight_spec(wup, (D_MODEL, D_FF), l),
            _weight_spec(wdn, (D_FF, D_MODEL), l),
            _const_spec((1, D_MODEL)),
        ] + cvt_in,
        out_specs=[
            pl.BlockSpec((None, tm, D_MODEL), p_idx),
            _whole_out_spec((ms, D_MODEL)),
        ] + cvt_out,
        out_shape=[jax.ShapeDtypeStruct((b, t, D_MODEL), F32),
                   jax.ShapeDtypeStruct((ms, D_MODEL), F32)] + cvt_shapes,
        compiler_params=pltpu.CompilerParams(
            dimension_semantics=("arbitrary",),
            vmem_limit_bytes=VMEM_LIMIT),
        name="ffn_final" if final else "ffn",
    )(xp, xs, mod_p, mod_s, g2, wup, wdn, fg, *[w for w, _ in cvt_jobs])


def _sproj_kernel(x_ref, mod_ref, g1_ref, win_ref, wpool_ref, pscale_ref, wa_ref, pre_ref,
                  q_ref, k_ref, v_ref, ma_ref, sgb_ref, npool_ref, *, nb, tn):
    x = x_ref[...]
    mod = jnp.concatenate([mod_ref[:, 0:2 * D_MODEL]] * tn, axis=0)
    sh1 = mod[:, 0:D_MODEL]
    sc1 = mod[:, D_MODEL:2 * D_MODEL]
    h = _rms_mod(x, g1_ref[...], sc1, sh1).astype(BF16)

    u = jnp.dot(h, win_ref[:, OFF_U:OFF_Q], preferred_element_type=F32)
    def up(r):
        if r < POOL_PAD:
            return pre_ref[r]
        r -= POOL_PAD
        return u[r * nb:(r + 1) * nb, :]

    d_rows = []
    for tt in range(tn):
        parts = []
        for g, w in enumerate(POOL_WINDOWS):
            lo = g * POOL_GROUP
            acc = up(POOL_PAD + tt)[:, lo:lo + POOL_GROUP]
            for s in range(1, w):
                acc = acc + up(POOL_PAD + tt - s)[:, lo:lo + POOL_GROUP]
            cnt = float(min(w, PAST_LEN + tt + 1))
            parts.append(acc / cnt - up(POOL_PAD + tt)[:, lo:lo + POOL_GROUP])
        d_rows.append(parts)
    y = jnp.concatenate(
        [_bdot(jnp.concatenate([d_rows[tt][g] for tt in range(tn)], axis=0), wpool_ref[g])
         for g in range(len(POOL_WINDOWS))], axis=1)
    y = y * pscale_ref[...]
    br_a = _bdot(y, wa_ref[...])
    for r in range(POOL_PAD):
        npool_ref[r] = up(r + tn)

    q = jnp.dot(h, win_ref[:, OFF_Q:OFF_K], preferred_element_type=F32)
    q_ref[...] = q * (HEAD_DIM ** -0.5)
    kv = jnp.dot(h, win_ref[:, OFF_K:OFF_GA], preferred_element_type=F32)
    k_ref[...] = kv[:, 0:KV_WIDTH]
    v_ref[...] = kv[:, KV_WIDTH:2 * KV_WIDTH]
    ga = jnp.dot(h, win_ref[:, OFF_GA:OFF_GB], preferred_element_type=F32)
    ma_ref[...] = jax.nn.sigmoid(ga) * br_a
    gb = jnp.dot(h, win_ref[:, OFF_GB:IN_WIDTH], preferred_element_type=F32)
    sgb_ref[...] = jax.nn.sigmoid(gb)


def _sproj_call(x, mod_s, g1, win, wpool, pscale, wa, prefix_t, l, nb, tn):
    m = x.shape[0]
    shapes = [
        jax.ShapeDtypeStruct((m, ATTN_WIDTH), F32),
        jax.ShapeDtypeStruct((m, KV_WIDTH), F32),
        jax.ShapeDtypeStruct((m, KV_WIDTH), F32),
        jax.ShapeDtypeStruct((m, D_MODEL), F32),
        jax.ShapeDtypeStruct((m, D_MODEL), F32),
        jax.ShapeDtypeStruct((POOL_PAD, nb, POOL_WIDTH), F32),
    ]
    return pl.pallas_call(
        functools.partial(_sproj_kernel, nb=nb, tn=tn),
        grid=(1,),
        in_specs=[
            _const_spec(x.shape),
            pl.BlockSpec((None, nb, MOD_SLAB), lambda i: (l, 0, 0), pipeline_mode=pl.Buffered(1)),
            _layer_spec((1, D_MODEL), l),
            _weight_spec(win, (D_MODEL, IN_WIDTH), l),
            _weight_spec(wpool, (len(POOL_WINDOWS), POOL_GROUP, POOL_GROUP), l),
            _layer_spec((1, POOL_WIDTH), l),
            _weight_spec(wa, (POOL_WIDTH, D_MODEL), l),
            _layer_spec((POOL_PAD, nb, POOL_WIDTH), l),
        ],
        out_specs=[_whole_out_spec(s.shape) for s in shapes],
        out_shape=shapes,
        compiler_params=pltpu.CompilerParams(
            dimension_semantics=("arbitrary",),
            vmem_limit_bytes=VMEM_LIMIT),
        name="sample_proj",
    )(x, mod_s, g1, win, wpool, pscale, wa, prefix_t)


def _sattn_kernel(sinks_ref, slopes_ref, q_ref, kn_ref, vn_ref, kc_ref, vc_ref, *rest,
                  layer, nb, tn, ch, nch):
    if layer:
        pk_ref, pv_ref = rest[:2]
        rest = rest[2:]
    o_ref, nk_ref, nv_ref, qt, knt, vnt, tbuf, sbuf, ot, carry = rest
    i = pl.program_id(0)
    srows = WINDOW + 8
    combos = [(tt, head) for tt in range(tn) for head in range(N_HEADS)]

    def rolled_chunk(out_ref, prev_ref, nxt):
        if layer:
            out_ref[0:layer] = prev_ref[...]
        out_ref[layer, :, 0:ch - tn, :] = carry[:, tn:ch, :]
        for tt in range(tn):
            out_ref[layer, :, ch - tn + tt, :] = nxt(tt)

    def roll_k(nxt):
        rolled_chunk(nk_ref, pk_ref if layer else None, nxt)

    def roll_v(nxt):
        rolled_chunk(nv_ref, pv_ref if layer else None, nxt)

    @pl.when(i == 0)
    def _start():
        for tt in range(tn):
            qt[tt] = q_ref[tt * nb:(tt + 1) * nb, :].T
            knt[tt] = kn_ref[tt * nb:(tt + 1) * nb, :].T
            vnt[tt] = vn_ref[tt * nb:(tt + 1) * nb, :].T

    def load_tiles(c_ref):
        keys = jnp.swapaxes(c_ref[...], 0, 1)
        for jj in range(ch):
            tbuf[jj] = keys[jj].T

    def head_rows(head):
        hk = head // GQA_GROUP
        return slice(hk * HEAD_DIM, (hk + 1) * HEAD_DIM)

    @pl.when((i >= 1) & (i < nch))
    def _roll_k():
        roll_k(lambda tt: kc_ref[:, tt, :])

    @pl.when(i == nch)
    def _roll_k_last():
        roll_k(lambda tt: kn_ref[tt * nb:(tt + 1) * nb, :])

    @pl.when((i > nch) & (i < 2 * nch))
    def _roll_v():
        roll_v(lambda tt: vc_ref[:, tt, :])

    @pl.when(i == 2 * nch)
    def _roll_v_last():
        roll_v(lambda tt: vn_ref[tt * nb:(tt + 1) * nb, :])

    @pl.when(i < nch)
    def _scores():
        carry[...] = kc_ref[...]
        load_tiles(kc_ref)
        row0 = pl.multiple_of(i * ch, ch)
        for ci, (tt, head) in enumerate(combos):
            qh = qt[tt, head * HEAD_DIM:(head + 1) * HEAD_DIM, :]
            rows = [jnp.sum(qh * tbuf[jj, head_rows(head), :], axis=0, keepdims=True)
                    for jj in range(ch)]
            sbuf[ci, pl.ds(row0, ch), :] = jnp.concatenate(rows, axis=0)

    @pl.when(i == nch - 1)
    def _softmax():
        for ci, (tt, head) in enumerate(combos):
            qh = qt[tt, head * HEAD_DIM:(head + 1) * HEAD_DIM, :]
            rows = [jnp.sum(qh * knt[t2, head_rows(head), :], axis=0, keepdims=True)
                    for t2 in range(tn)]
            rows.append(jnp.zeros((8 - tn, nb), F32))
            sbuf[ci, WINDOW:srows, :] = jnp.concatenate(rows, axis=0)
        jpos = lax.broadcasted_iota(jnp.int32, (srows, nb), 0)
        for ci, (tt, head) in enumerate(combos):
            dist = tt + WINDOW - jpos
            valid = (dist >= 0) & (dist < WINDOW)
            s = sbuf[ci] - slopes_ref[head] * dist.astype(F32)
            s = jnp.where(valid, s, NEG_INF)
            sink = sinks_ref[layer, head]
            m = jnp.maximum(jnp.max(s, axis=0, keepdims=True), sink)
            p = jnp.exp(s - m)
            den = jnp.sum(p, axis=0, keepdims=True) + jnp.exp(sink - m)
            p = p / den
            sbuf[ci] = p
            acc = jnp.zeros((HEAD_DIM, nb), F32)
            for t2 in range(tt + 1):
                acc = acc + p[WINDOW + t2:WINDOW + t2 + 1, :] * vnt[t2, head_rows(head), :]
            ot[tt, head * HEAD_DIM:(head + 1) * HEAD_DIM, :] = acc

    @pl.when((i >= nch) & (i < 2 * nch))
    def _values():
        carry[...] = vc_ref[...]
        load_tiles(vc_ref)
        row0 = pl.multiple_of((i - nch) * ch, ch)
        for ci, (tt, head) in enumerate(combos):
            acc = ot[tt, head * HEAD_DIM:(head + 1) * HEAD_DIM, :]
            for jj in range(ch):
                acc = acc + sbuf[ci, pl.ds(row0 + jj, 1), :] * tbuf[jj, head_rows(head), :]
            ot[tt, head * HEAD_DIM:(head + 1) * HEAD_DIM, :] = acc

    @pl.when(i == 2 * nch)
    def _finish():
        for tt in range(tn):
            o_ref[tt * nb:(tt + 1) * nb, :] = ot[tt].T


def _sattn_call(q, kn, vn, ck, cv, sinks, slopes, rolled, l, nb, tn):
    m = q.shape[0]
    ch = SATTN_KEYS
    nch = WINDOW // ch
    last = nch - 1
    chunk = (None, nb, ch, KV_WIDTH)
    cache_shape = jax.ShapeDtypeStruct((l + 1, nb, WINDOW, KV_WIDTH), F32)
    k_lag = lambda i, a, b: (0, 0, jnp.clip(i - 1, 0, last), 0)
    v_lag = lambda i, a, b: (0, 0, jnp.clip(i - nch - 1, 0, last), 0)
    in_specs = [
        _const_spec((m, ATTN_WIDTH)),
        _const_spec((m, KV_WIDTH)),
        _const_spec((m, KV_WIDTH)),
        pl.BlockSpec(chunk, lambda i, a, b: (l, 0, jnp.minimum(i, last), 0)),
        pl.BlockSpec(chunk, lambda i, a, b: (l, 0, jnp.clip(i - nch, 0, last), 0)),
    ]
    args = [sinks, slopes, q, kn, vn, ck, cv]
    if l:
        in_specs += [pl.BlockSpec((l, nb, ch, KV_WIDTH), k_lag),
                     pl.BlockSpec((l, nb, ch, KV_WIDTH), v_lag)]
        args += list(rolled)
    grid_spec = pltpu.PrefetchScalarGridSpec(
        num_scalar_prefetch=2,
        grid=(2 * nch + 1,),
        in_specs=in_specs,
        out_specs=[
            _whole_out_spec((m, ATTN_WIDTH)),
            pl.BlockSpec((l + 1, nb, ch, KV_WIDTH), k_lag),
            pl.BlockSpec((l + 1, nb, ch, KV_WIDTH), v_lag),
        ],
        scratch_shapes=[
            pltpu.VMEM((tn, ATTN_WIDTH, nb), F32),
            pltpu.VMEM((tn, KV_WIDTH, nb), F32),
            pltpu.VMEM((tn, KV_WIDTH, nb), F32),
            pltpu.VMEM((ch, KV_WIDTH, nb), F32),
            pltpu.VMEM((tn * N_HEADS, WINDOW + 8, nb), F32),
            pltpu.VMEM((tn, ATTN_WIDTH, nb), F32),
            pltpu.VMEM((nb, ch, KV_WIDTH), F32),
        ],
    )
    return pl.pallas_call(
        functools.partial(_sattn_kernel, layer=l, nb=nb, tn=tn, ch=ch, nch=nch),
        grid_spec=grid_spec,
        out_shape=[jax.ShapeDtypeStruct((m, ATTN_WIDTH), F32), cache_shape, cache_shape],
        compiler_params=pltpu.CompilerParams(
            dimension_semantics=("arbitrary",),
            vmem_limit_bytes=VMEM_LIMIT),
        name="sample_attn",
    )(*args)


def _spost_kernel(x_ref, o_ref, ma_ref, sgb_ref, mod_ref, wb_ref, wout_ref, xo_ref, *, tn):
    gt1 = jnp.concatenate([mod_ref[...]] * tn, axis=0)
    br_b = _bdot(o_ref[...], wb_ref[...])
    merged = ma_ref[...] + sgb_ref[...] * br_b
    xo_ref[...] = x_ref[...] + gt1 * _bdot(merged, wout_ref[...])


def _spost_call(x, o, ma, sgb, mod_s, wb, wout, l, nb, tn):
    return pl.pallas_call(
        functools.partial(_spost_kernel, tn=tn),
        grid=(1,),
        in_specs=[
            _const_spec(x.shape), _const_spec(o.shape), _const_spec(ma.shape),
            _const_spec(sgb.shape),
            pl.BlockSpec((None, nb, D_MODEL), lambda i: (l, 0, 2), pipeline_mode=pl.Buffered(1)),
            _weight_spec(wb, (ATTN_WIDTH, D_MODEL), l),
            _weight_spec(wout, (D_MODEL, D_MODEL), l),
        ],
        out_specs=_whole_out_spec(x.shape),
        out_shape=jax.ShapeDtypeStruct(x.shape, F32),
        compiler_params=pltpu.CompilerParams(
            dimension_semantics=("arbitrary",),
            vmem_limit_bytes=VMEM_LIMIT),
        name="sample_post",
    )(x, o, ma, sgb, mod_s, wb, wout)


def kernel(x_prompt, x_sample, cache_k, cache_v, state_pool, c_prompt, c_sample,
           w_ada, b_ada, norm1_g, w_in, w_pool, pool_scale, attn_sinks, w_a, w_b,
           w_out, norm2_g, w_up, w_down, final_g):
    bp, tp, _ = x_prompt.shape
    nb, tn, _ = x_sample.shape
    assert tp % MIX_ROWS == 0 and tp % FFN_ROWS == 0 and tn <= 8 and nb == LANES
    assert bp % 8 == 0

    mod_p, mod_s = _mod_call(c_prompt, c_sample, w_ada, b_ada)

    w_pool2 = w_pool.reshape(DEPTH, POOL_WIDTH, POOL_GROUP)
    mix_w = [w[0].astype(BF16) for w in (w_in, w_pool2, w_a, w_b, w_out)]
    bias = jnp.asarray(_prompt_bias())
    slopes = jnp.asarray(_alibi_slopes(), F32)
    fg = final_g.reshape(1, D_MODEL)
    g1 = norm1_g.reshape(DEPTH, 1, D_MODEL)
    g2 = norm2_g.reshape(DEPTH, 1, D_MODEL)
    psc = pool_scale.reshape(DEPTH, 1, POOL_WIDTH)

    xs = x_sample.transpose(1, 0, 2).reshape(tn * nb, D_MODEL)
    ck = cache_k.reshape(DEPTH, nb, WINDOW, KV_WIDTH)
    cv = cache_v.reshape(DEPTH, nb, WINDOW, KV_WIDTH)
    prefix_t = state_pool.transpose(0, 2, 1, 3)

    xp = x_prompt
    kp, vp, pp, ps, rolled = [], [], [], [], None
    for l in range(DEPTH):
        last = l == DEPTH - 1
        win, wpool, wa, wb, wout = mix_w
        wpool = wpool.reshape(len(POOL_WINDOWS), POOL_GROUP, POOL_GROUP)
        xp, nk, nv, npool, wup, wdn = _pmix_call(xp, mod_p, g1, win, wpool, psc, wa, wb, wout,
                                                 attn_sinks, bias, l, [(w_up, l), (w_down, l)])
        kp.append(nk); vp.append(nv); pp.append(npool)

        q, kn, vn, ma, sgb, npool_s = _sproj_call(xs, mod_s, g1, win, wpool, psc, wa,
                                                  prefix_t, l, nb, tn)
        o, *rolled = _sattn_call(q, kn, vn, ck, cv, attn_sinks, slopes, rolled, l, nb, tn)
        xs = _spost_call(xs, o, ma, sgb, mod_s, wb, wout, l, nb, tn)
        ps.append(npool_s)

        nxt = [] if last else [(w, l + 1) for w in (w_in, w_pool2, w_a, w_b, w_out)]
        xp, xs, *mix_w = _ffn_call(xp, xs, mod_p, mod_s, g2, wup, wdn, fg, l, last, nxt)

    to_bt = lambda a: a.reshape(tn, nb, a.shape[-1]).transpose(1, 0, 2)

    kv_shape_p = (DEPTH, bp, WINDOW, N_KV_HEADS, HEAD_DIM)
    kv_shape_s = (DEPTH, nb, WINDOW, N_KV_HEADS, HEAD_DIM)
    return (xp,
            to_bt(xs),
            jnp.stack(kp).reshape(kv_shape_p),
            jnp.stack(vp).reshape(kv_shape_p),
            jnp.stack(pp),
            rolled[0].reshape(kv_shape_s),
            rolled[1].reshape(kv_shape_s),
            jnp.stack(ps).transpose(0, 2, 1, 3))
```

```python
import functools

import numpy as np
import jax
import jax.numpy as jnp
from jax import lax
from jax.experimental import pallas as pl
from jax.experimental.pallas import tpu as pltpu

D_MODEL = 1024
DEPTH = 2
PAST_LEN = 16384
POOL_WIDTH = D_MODEL // 2
POOL_WINDOWS = (2, 4, 8, 16)
POOL_GROUP = POOL_WIDTH // len(POOL_WINDOWS)
POOL_PAD = max(POOL_WINDOWS) - 1
N_HEADS = 8
N_KV_HEADS = 2
HEAD_DIM = 64
GQA_GROUP = N_HEADS // N_KV_HEADS
ATTN_WIDTH = N_HEADS * HEAD_DIM
KV_WIDTH = N_KV_HEADS * HEAD_DIM
WINDOW = 128
ATTN_BLOCK = 128
D_FF = 4 * D_MODEL
RMS_EPS = 1e-6
NEG_INF = -1e30

OFF_U = 0
OFF_Q = OFF_U + POOL_WIDTH
OFF_K = OFF_Q + ATTN_WIDTH
OFF_V = OFF_K + KV_WIDTH
OFF_GA = OFF_V + KV_WIDTH
OFF_GB = OFF_GA + D_MODEL
IN_WIDTH = OFF_GB + D_MODEL
MOD_SLAB = 3 * D_MODEL

LANES = 128
HIST = 16
assert all(w & (w - 1) == 0 for w in POOL_WINDOWS) and list(POOL_WINDOWS) == sorted(POOL_WINDOWS)
assert HIST >= POOL_PAD
VMEM_LIMIT = 56 * 1024 * 1024
SUB_ROWS = 256
MIX_ROWS = 2 * SUB_ROWS
FFN_ROWS = 512
SATTN_BATCH = 8

F32 = jnp.float32
BF16 = jnp.bfloat16


def _bdot(a, b):
    return jnp.dot(a.astype(BF16), b.astype(BF16), preferred_element_type=F32)


def _rms_mod(x, g, sc, sh):
    ms = jnp.mean(x * x, axis=-1, keepdims=True)
    return (x * lax.rsqrt(ms + RMS_EPS) * g) * (1.0 + sc) + sh


def _alibi_slopes():
    return 2.0 ** (-8.0 * (np.arange(N_HEADS) + 1) / N_HEADS)


def _const_spec(shape):
    nd = len(shape)
    return pl.BlockSpec(shape, lambda *_: (0,) * nd, pipeline_mode=pl.Buffered(1))


def _layer_spec(shape, l):
    nd = len(shape)
    return pl.BlockSpec((None,) + tuple(shape), lambda *_: (l,) + (0,) * nd,
                        pipeline_mode=pl.Buffered(1))


def _whole_out_spec(shape):
    nd = len(shape)
    return pl.BlockSpec(shape, lambda *_: (0,) * nd)


def _weight_spec(w, shape, l):
    return _const_spec(shape) if w.shape == tuple(shape) else _layer_spec(shape, l)


def _convert_specs(jobs, n_steps, step_of):
    in_specs, out_specs, out_shapes = [], [], []
    for w, l in jobs:
        _, r, c = w.shape
        rows = r // n_steps
        assert rows * n_steps == r and rows % 16 == 0, (w.shape, n_steps)
        in_specs.append(pl.BlockSpec((None, rows, c), lambda *a, l=l: (l, step_of(*a), 0)))
        out_specs.append(pl.BlockSpec((rows, c), lambda *a: (step_of(*a), 0)))
        out_shapes.append(jax.ShapeDtypeStruct((r, c), BF16))
    return in_specs, out_specs, out_shapes


def _convert_slabs(src_refs, dst_refs):
    for src, dst in zip(src_refs, dst_refs):
        dst[...] = src[...].astype(BF16)


def _mod_kernel(cp_ref, cs_ref, w_ref, b_ref, op_ref, os_ref):
    w = w_ref[...].astype(BF16)
    b = b_ref[...]
    for c_ref, o_ref in ((cp_ref, op_ref), (cs_ref, os_ref)):
        c = c_ref[...]
        s = (c * jax.nn.sigmoid(c)).astype(BF16)
        o_ref[...] = jnp.dot(s, w, preferred_element_type=F32) + b


def _mod_call(c_p, c_s, w_ada, b_ada):
    tn = 2048
    n = 6 * D_MODEL
    mp, ms = c_p.shape[0], c_s.shape[0]
    return pl.pallas_call(
        _mod_kernel,
        grid=(DEPTH, n // tn),
        in_specs=[
            _const_spec((mp, D_MODEL)),
            _const_spec((ms, D_MODEL)),
            pl.BlockSpec((None, D_MODEL, tn), lambda l, j: (l, 0, j)),
            pl.BlockSpec((None, 1, tn), lambda l, j: (l, 0, j)),
        ],
        out_specs=[
            pl.BlockSpec((None, mp, tn), lambda l, j: (l, 0, j)),
            pl.BlockSpec((None, ms, tn), lambda l, j: (l, 0, j)),
        ],
        out_shape=[jax.ShapeDtypeStruct((DEPTH, mp, n), F32),
                   jax.ShapeDtypeStruct((DEPTH, ms, n), F32)],
        compiler_params=pltpu.CompilerParams(
            dimension_semantics=("arbitrary", "arbitrary"),
            vmem_limit_bytes=VMEM_LIMIT),
        name="adaln_mod",
    )(c_p, c_s, w_ada, b_ada.reshape(DEPTH, 1, n))


def _prompt_bias():
    i = np.arange(ATTN_BLOCK)[:, None]
    j = np.arange(2 * ATTN_BLOCK)[None, :]
    dist = i + ATTN_BLOCK - j
    valid = (dist >= 0) & (dist < WINDOW)
    valid_first = valid & (j >= ATTN_BLOCK)
    sl = _alibi_slopes()[:, None, None]
    b = np.where(valid[None], -sl * dist[None], NEG_INF)
    b0 = np.where(valid_first[None], -sl * dist[None], NEG_INF)
    return np.stack([b, b0]).astype(np.float32)


def _pmix_kernel(sinks_ref, x_ref, mod_ref, g1_ref, win_ref, wpool_ref, pscale_ref,
                 wa_ref, wb_ref, wout_ref, bias_ref, *rest, layer, n_cvt):
    cvt_in, rest = rest[:n_cvt], rest[n_cvt:]
    (xo_ref, nk_ref, nv_ref, npool_ref), rest = rest[:4], rest[4:]
    cvt_out, (ubuf, ka, kb, va, vb, obuf) = rest[:n_cvt], rest[n_cvt:]
    _convert_slabs(cvt_in, cvt_out)
    tm, sub = MIX_ROWS, SUB_ROWS
    bi = pl.program_id(0)
    t = pl.program_id(1)
    nt = pl.num_programs(1)

    @pl.when(t == 0)
    def _init():
        ubuf[0:HIST, :] = jnp.zeros((HIST, POOL_WIDTH), F32)
        zero = jnp.zeros((N_KV_HEADS, ATTN_BLOCK, LANES), BF16)
        ka[:, 0:ATTN_BLOCK, :] = zero
        kb[:, 0:ATTN_BLOCK, :] = zero
        va[:, 0:ATTN_BLOCK, :] = zero
        vb[:, 0:ATTN_BLOCK, :] = zero

    mod = mod_ref[pl.ds(bi, 1), :]
    sh1 = mod[:, 0:D_MODEL]
    sc1 = mod[:, D_MODEL:2 * D_MODEL]
    gt1 = mod[:, 2 * D_MODEL:3 * D_MODEL]
    lane = lax.broadcasted_iota(jnp.int32, (sub, LANES), 1)
    low = lane < HEAD_DIM
    low_q = lax.broadcasted_iota(jnp.int32, (ATTN_BLOCK, LANES), 1) < HEAD_DIM
    nt_dims = (((1,), (1,)), ((), ()))
    st = [dict(ro=i * sub) for i in range(tm // sub)]

    def norm(c):
        c["x"] = x_ref[c["ro"]:c["ro"] + sub, :]
        c["h"] = _rms_mod(c["x"], g1_ref[...], sc1, sh1).astype(BF16)

    def proj(c):
        h = c["h"]
        c["u"] = jnp.dot(h, win_ref[:, OFF_U:OFF_Q], preferred_element_type=F32)
        q = jnp.dot(h, win_ref[:, OFF_Q:OFF_K], preferred_element_type=F32)
        c["qb"] = (q * (HEAD_DIM ** -0.5)).astype(BF16)
        kv = jnp.dot(h, win_ref[:, OFF_K:OFF_GA], preferred_element_type=F32)
        c["k"] = kv[:, 0:KV_WIDTH]
        c["v"] = kv[:, KV_WIDTH:2 * KV_WIDTH]

    def pool_sums(c):
        ro, u = c["ro"], c["u"]
        ubuf[HIST + ro:HIST + ro + sub, :] = u
        pos = t * tm + ro + lax.broadcasted_iota(jnp.int32, (sub, 1), 0)
        cur = ubuf[ro:ro + HIST + sub, :]
        d, w = [], 1
        for g, wg in enumerate(POOL_WINDOWS):
            while w < wg:
                cur = cur + pltpu.roll(cur, w, axis=0)
                w *= 2
            ug = u[:, g * POOL_GROUP:(g + 1) * POOL_GROUP]
            cnt = jnp.minimum(wg, pos + 1).astype(F32)
            d.append((cur[HIST:, 0:POOL_GROUP] / cnt - ug).astype(BF16))
            if g + 1 < len(POOL_WINDOWS):
                cur = cur[:, POOL_GROUP:]
        c["d"] = d

    def kv_store(c):
        r = ATTN_BLOCK + c["ro"]
        k, v = c["k"], c["v"]
        kr = pltpu.roll(k, HEAD_DIM, axis=1)
        vr = pltpu.roll(v, HEAD_DIM, axis=1)
        ka[0, r:r + sub, :] = jnp.where(low, k, 0.0).astype(BF16)
        kb[0, r:r + sub, :] = jnp.where(low, 0.0, kr).astype(BF16)
        ka[1, r:r + sub, :] = jnp.where(low, kr, 0.0).astype(BF16)
        kb[1, r:r + sub, :] = jnp.where(low, 0.0, k).astype(BF16)
        va[0, r:r + sub, :] = jnp.where(low, v, 0.0).astype(BF16)
        vb[0, r:r + sub, :] = jnp.where(low, 0.0, vr).astype(BF16)
        va[1, r:r + sub, :] = jnp.where(low, vr, 0.0).astype(BF16)
        vb[1, r:r + sub, :] = jnp.where(low, 0.0, v).astype(BF16)

    def pool_proj(c):
        y = jnp.concatenate(
            [jnp.dot(c["d"][g], wpool_ref[g], preferred_element_type=F32)
             for g in range(len(POOL_WINDOWS))], axis=1)
        c["br_a"] = _bdot(y * pscale_ref[...], wa_ref[...])

    def units(c):
        j0 = c["ro"] // ATTN_BLOCK
        return [(j, hk, pr) for j in range(j0, j0 + sub // ATTN_BLOCK)
                for hk in range(N_KV_HEADS) for pr in range(GQA_GROUP // 2)]

    def scores(c):
        sc = {}
        for (j, hk, pr) in units(c):
            r0 = j * ATTN_BLOCK
            c0 = (hk * (GQA_GROUP // 2) + pr) * LANES
            q2 = c["qb"][r0 - c["ro"]:r0 - c["ro"] + ATTN_BLOCK, c0:c0 + LANES]
            ks = (ka[hk, r0:r0 + 2 * ATTN_BLOCK, :], kb[hk, r0:r0 + 2 * ATTN_BLOCK, :])
            for e in range(2):
                sc[(j, hk, pr, e)] = lax.dot_general(q2, ks[e], nt_dims,
                                                     preferred_element_type=F32)
        c["s"] = sc

    def gate_a(c):
        c["ga"] = jnp.dot(c["h"], win_ref[:, OFF_GA:OFF_GB], preferred_element_type=F32)

    def gate_b(c):
        c["gb"] = jnp.dot(c["h"], win_ref[:, OFF_GB:IN_WIDTH], preferred_element_type=F32)

    def softmax(c):
        probs, dens = {}, {}
        for (j, hk, pr) in units(c):
            for e in range(2):
                head = hk * GQA_GROUP + pr * 2 + e
                if j == 0:
                    bias = jnp.where(t == 0, bias_ref[1, head], bias_ref[0, head])
                else:
                    bias = bias_ref[0, head]
                s = c["s"][(j, hk, pr, e)] + bias
                sink = sinks_ref[layer, head]
                m = jnp.maximum(jnp.max(s, axis=-1, keepdims=True), sink)
                p = jnp.exp(s - m)
                dens[(j, hk, pr, e)] = jnp.sum(p, axis=-1, keepdims=True) + jnp.exp(sink - m)
                probs[(j, hk, pr, e)] = p.astype(BF16)
        c["p"], c["den"] = probs, dens

    def values(c):
        for (j, hk, pr) in units(c):
            r0 = j * ATTN_BLOCK
            c0 = (hk * (GQA_GROUP // 2) + pr) * LANES
            vs = (va[hk, r0:r0 + 2 * ATTN_BLOCK, :], vb[hk, r0:r0 + 2 * ATTN_BLOCK, :])
            o2 = (jnp.dot(c["p"][(j, hk, pr, 0)], vs[0], preferred_element_type=F32)
                  + jnp.dot(c["p"][(j, hk, pr, 1)], vs[1], preferred_element_type=F32))
            den = jnp.where(low_q, c["den"][(j, hk, pr, 0)], c["den"][(j, hk, pr, 1)])
            obuf[r0:r0 + ATTN_BLOCK, c0:c0 + LANES] = o2 / den

    def tail(c):
        ro = c["ro"]
        br_b = _bdot(obuf[ro:ro + sub, :], wb_ref[...])
        merged = jax.nn.sigmoid(c["ga"]) * c["br_a"] + jax.nn.sigmoid(c["gb"]) * br_b
        xo_ref[ro:ro + sub, :] = c["x"] + gt1 * _bdot(merged, wout_ref[...])

    stages = [
        (norm,),
        (proj,),
        (pool_sums, kv_store),
        (pool_proj, scores, gate_a),
        (softmax,),
        (values, gate_b),
        (tail,),
    ]
    order = sorted((2 * s + 3 * i, i, s) for i in range(len(st)) for s in range(len(stages)))
    for _, i, s in order:
        for fn in stages[s]:
            fn(st[i])

    @pl.when(t == nt - 1)
    def _state():
        nk_ref[...] = st[-1]["k"][sub - WINDOW:, :]
        nv_ref[...] = st[-1]["v"][sub - WINDOW:, :]
        npool_ref[...] = ubuf[HIST + tm - POOL_PAD:HIST + tm, :]

    ubuf[0:HIST, :] = ubuf[tm:tm + HIST, :]
    for buf in (ka, kb, va, vb):
        buf[:, 0:ATTN_BLOCK, :] = buf[:, tm:tm + ATTN_BLOCK, :]


def _pmix_call(x, mod_p, g1, win, wpool, pscale, wa, wb, wout, sinks, bias, l, cvt_jobs):
    b, t, _ = x.shape
    tm = MIX_ROWS
    nt = t // tm
    cvt_in, cvt_out, cvt_shapes = _convert_specs(cvt_jobs, b * nt, lambda i, j, s: i * nt + j)
    grid_spec = pltpu.PrefetchScalarGridSpec(
        num_scalar_prefetch=1,
        grid=(b, nt),
        in_specs=[
            pl.BlockSpec((None, tm, D_MODEL), lambda i, j, s: (i, j, 0)),
            pl.BlockSpec((None, b, MOD_SLAB), lambda i, j, s: (l, 0, 0),
                         pipeline_mode=pl.Buffered(1)),
            _layer_spec((1, D_MODEL), l),
            _weight_spec(win, (D_MODEL, IN_WIDTH), l),
            _weight_spec(wpool, (len(POOL_WINDOWS), POOL_GROUP, POOL_GROUP), l),
            _layer_spec((1, POOL_WIDTH), l),
            _weight_spec(wa, (POOL_WIDTH, D_MODEL), l),
            _weight_spec(wb, (ATTN_WIDTH, D_MODEL), l),
            _weight_spec(wout, (D_MODEL, D_MODEL), l),
            _const_spec((2, N_HEADS, ATTN_BLOCK, 2 * ATTN_BLOCK)),
        ] + cvt_in,
        out_specs=[
            pl.BlockSpec((None, tm, D_MODEL), lambda i, j, s: (i, j, 0)),
            pl.BlockSpec((None, WINDOW, KV_WIDTH), lambda i, j, s: (i, 0, 0)),
            pl.BlockSpec((None, WINDOW, KV_WIDTH), lambda i, j, s: (i, 0, 0)),
            pl.BlockSpec((None, POOL_PAD, POOL_WIDTH), lambda i, j, s: (i, 0, 0)),
        ] + cvt_out,
        scratch_shapes=[
            pltpu.VMEM((HIST + tm, POOL_WIDTH), F32),
            pltpu.VMEM((N_KV_HEADS, ATTN_BLOCK + tm, LANES), BF16),
            pltpu.VMEM((N_KV_HEADS, ATTN_BLOCK + tm, LANES), BF16),
            pltpu.VMEM((N_KV_HEADS, ATTN_BLOCK + tm, LANES), BF16),
            pltpu.VMEM((N_KV_HEADS, ATTN_BLOCK + tm, LANES), BF16),
            pltpu.VMEM((tm, ATTN_WIDTH), F32),
        ],
    )
    return pl.pallas_call(
        functools.partial(_pmix_kernel, layer=l, n_cvt=len(cvt_jobs)),
        grid_spec=grid_spec,
        out_shape=[
            jax.ShapeDtypeStruct((b, t, D_MODEL), F32),
            jax.ShapeDtypeStruct((b, WINDOW, KV_WIDTH), F32),
            jax.ShapeDtypeStruct((b, WINDOW, KV_WIDTH), F32),
            jax.ShapeDtypeStruct((b, POOL_PAD, POOL_WIDTH), F32),
        ] + cvt_shapes,
        compiler_params=pltpu.CompilerParams(
            dimension_semantics=("arbitrary", "arbitrary"),
            vmem_limit_bytes=VMEM_LIMIT),
        name="prompt_mixer",
    )(sinks, x, mod_p, g1, win, wpool, pscale, wa, wb, wout, bias, *[w for w, _ in cvt_jobs])


def _ffn_kernel(xp_ref, xs_ref, modp_ref, mods_ref, g2_ref, wup_ref, wdn_ref, fg_ref,
                *rest, final, n_prompt, tiles_per_row, reps, n_cvt):
    cvt_in, (op_ref, os_ref), cvt_out = rest[:n_cvt], rest[n_cvt:n_cvt + 2], rest[n_cvt + 2:]
    _convert_slabs(cvt_in, cvt_out)
    i = pl.program_id(0)

    def ffn(x, sh2, sc2, gt2):
        h2 = _rms_mod(x, g2_ref[...], sc2, sh2)
        ff = _bdot(h2, wup_ref[...])
        ff = jnp.square(jnp.maximum(ff, 0.0))
        y = x + gt2 * _bdot(ff, wdn_ref[...])
        if final:
            ms = jnp.mean(y * y, axis=-1, keepdims=True)
            y = y * lax.rsqrt(ms + RMS_EPS) * fg_ref[...]
        return y

    @pl.when(i < n_prompt)
    def _prompt():
        mod = modp_ref[pl.ds(i // tiles_per_row, 1), :]
        op_ref[...] = ffn(xp_ref[...], mod[:, 0:D_MODEL], mod[:, D_MODEL:2 * D_MODEL],
                          mod[:, 2 * D_MODEL:3 * D_MODEL])

    @pl.when(i == n_prompt)
    def _sample():
        mod = jnp.concatenate([mods_ref[...]] * reps, axis=0)
        os_ref[...] = ffn(xs_ref[...], mod[:, 0:D_MODEL], mod[:, D_MODEL:2 * D_MODEL],
                          mod[:, 2 * D_MODEL:3 * D_MODEL])


def _ffn_call(xp, xs, mod_p, mod_s, g2, wup, wdn, fg, l, final, cvt_jobs):
    b, t, _ = xp.shape
    ms = xs.shape[0]
    nb = mod_s.shape[1]
    tm = FFN_ROWS
    tpr = t // tm
    n_prompt = b * tpr
    last = n_prompt - 1

    def p_idx(i):
        ii = jnp.minimum(i, last)
        return (ii // tpr, ii % tpr, 0)

    cvt_in, cvt_out, cvt_shapes = _convert_specs(cvt_jobs, n_prompt, lambda i: jnp.minimum(i, last))
    return pl.pallas_call(
        functools.partial(_ffn_kernel, final=final, n_prompt=n_prompt, tiles_per_row=tpr,
                          reps=ms // nb, n_cvt=len(cvt_jobs)),
        grid=(n_prompt + 1,),
        in_specs=[
            pl.BlockSpec((None, tm, D_MODEL), p_idx),
            _const_spec((ms, D_MODEL)),
            pl.BlockSpec((None, b, MOD_SLAB), lambda i: (l, 0, 1), pipeline_mode=pl.Buffered(1)),
            pl.BlockSpec((None, nb, MOD_SLAB), lambda i: (l, 0, 1), pipeline_mode=pl.Buffered(1)),
            _layer_spec((1, D_MODEL), l),
            _weight_spec(wup, (D_MODEL, D_FF), l),
            _weight_spec(wdn, (D_FF, D_MODEL), l),
            _const_spec((1, D_MODEL)),
        ] + cvt_in,
        out_specs=[
            pl.BlockSpec((None, tm, D_MODEL), p_idx),
            _whole_out_spec((ms, D_MODEL)),
        ] + cvt_out,
        out_shape=[jax.ShapeDtypeStruct((b, t, D_MODEL), F32),
                   jax.ShapeDtypeStruct((ms, D_MODEL), F32)] + cvt_shapes,
        compiler_params=pltpu.CompilerParams(
            dimension_semantics=("arbitrary",),
            vmem_limit_bytes=VMEM_LIMIT),
        name="ffn_final" if final else "ffn",
    )(xp, xs, mod_p, mod_s, g2, wup, wdn, fg, *[w for w, _ in cvt_jobs])


def _sproj_kernel(x_ref, mod_ref, g1_ref, win_ref, wpool_ref, pscale_ref, wa_ref, pre_ref,
                  q_ref, k_ref, v_ref, ma_ref, sgb_ref, npool_ref, *, nb, tn):
    x = x_ref[...]
    mod = jnp.concatenate([mod_ref[:, 0:2 * D_MODEL]] * tn, axis=0)
    sh1 = mod[:, 0:D_MODEL]
    sc1 = mod[:, D_MODEL:2 * D_MODEL]
    h = _rms_mod(x, g1_ref[...], sc1, sh1).astype(BF16)

    u = jnp.dot(h, win_ref[:, OFF_U:OFF_Q], preferred_element_type=F32)
    def up(r):
        if r < POOL_PAD:
            return pre_ref[r]
        r -= POOL_PAD
        return u[r * nb:(r + 1) * nb, :]

    d_rows = []
    for tt in range(tn):
        parts = []
        for g, w in enumerate(POOL_WINDOWS):
            lo = g * POOL_GROUP
            acc = up(POOL_PAD + tt)[:, lo:lo + POOL_GROUP]
            for s in range(1, w):
                acc = acc + up(POOL_PAD + tt - s)[:, lo:lo + POOL_GROUP]
            cnt = float(min(w, PAST_LEN + tt + 1))
            parts.append(acc / cnt - up(POOL_PAD + tt)[:, lo:lo + POOL_GROUP])
        d_rows.append(parts)
    y = jnp.concatenate(
        [_bdot(jnp.concatenate([d_rows[tt][g] for tt in range(tn)], axis=0), wpool_ref[g])
         for g in range(len(POOL_WINDOWS))], axis=1)
    y = y * pscale_ref[...]
    br_a = _bdot(y, wa_ref[...])
    for r in range(POOL_PAD):
        npool_ref[r] = up(r + tn)

    q = jnp.dot(h, win_ref[:, OFF_Q:OFF_K], preferred_element_type=F32)
    q_ref[...] = q * (HEAD_DIM ** -0.5)
    kv = jnp.dot(h, win_ref[:, OFF_K:OFF_GA], preferred_element_type=F32)
    k_ref[...] = kv[:, 0:KV_WIDTH]
    v_ref[...] = kv[:, KV_WIDTH:2 * KV_WIDTH]
    ga = jnp.dot(h, win_ref[:, OFF_GA:OFF_GB], preferred_element_type=F32)
    ma_ref[...] = jax.nn.sigmoid(ga) * br_a
    gb = jnp.dot(h, win_ref[:, OFF_GB:IN_WIDTH], preferred_element_type=F32)
    sgb_ref[...] = jax.nn.sigmoid(gb)


def _sproj_call(x, mod_s, g1, win, wpool, pscale, wa, prefix_t, l, nb, tn):
    m = x.shape[0]
    shapes = [
        jax.ShapeDtypeStruct((m, ATTN_WIDTH), F32),
        jax.ShapeDtypeStruct((m, KV_WIDTH), F32),
        jax.ShapeDtypeStruct((m, KV_WIDTH), F32),
        jax.ShapeDtypeStruct((m, D_MODEL), F32),
        jax.ShapeDtypeStruct((m, D_MODEL), F32),
        jax.ShapeDtypeStruct((POOL_PAD, nb, POOL_WIDTH), F32),
    ]
    return pl.pallas_call(
        functools.partial(_sproj_kernel, nb=nb, tn=tn),
        grid=(1,),
        in_specs=[
            _const_spec(x.shape),
            pl.BlockSpec((None, nb, MOD_SLAB), lambda i: (l, 0, 0), pipeline_mode=pl.Buffered(1)),
            _layer_spec((1, D_MODEL), l),
            _weight_spec(win, (D_MODEL, IN_WIDTH), l),
            _weight_spec(wpool, (len(POOL_WINDOWS), POOL_GROUP, POOL_GROUP), l),
            _layer_spec((1, POOL_WIDTH), l),
            _weight_spec(wa, (POOL_WIDTH, D_MODEL), l),
            _layer_spec((POOL_PAD, nb, POOL_WIDTH), l),
        ],
        out_specs=[_whole_out_spec(s.shape) for s in shapes],
        out_shape=shapes,
        compiler_params=pltpu.CompilerParams(
            dimension_semantics=("arbitrary",),
            vmem_limit_bytes=VMEM_LIMIT),
        name="sample_proj",
    )(x, mod_s, g1, win, wpool, pscale, wa, prefix_t)


def _sample_bias(tn):
    t = np.arange(tn)[:, None]
    lane = np.arange(2 * WINDOW)[None, :]
    cached = lane < WINDOW
    new_t = lane - (2 * WINDOW - tn)
    dist = np.where(cached, t + WINDOW - lane, t - new_t)
    valid = np.where(cached, (dist >= 0) & (dist < WINDOW), (new_t >= 0) & (dist >= 0))
    sl = _alibi_slopes()[:, None, None]
    return np.where(valid[None], -sl * dist[None], NEG_INF).astype(np.float32)


def _sattn_kernel(sinks_ref, q_ref, kc_ref, vc_ref, kn_ref, vn_ref, bias_ref, *rest,
                   layer, nb, tn, bb):
    if layer:
        pk_ref, pv_ref = rest[:2]
        rest = rest[2:]
    o_ref, nk_ref, nv_ref, wk, wv = rest
    i = pl.program_id(0)
    rows = GQA_GROUP * tn
    keep = WINDOW - tn

    @pl.when(i == 0)
    def _new_rows():
        for c0 in range(0, nb * tn, LANES):
            wk[:, c0:c0 + LANES] = kn_ref[c0:c0 + LANES, :].T
            wv[:, c0:c0 + LANES] = vn_ref[c0:c0 + LANES, :].T

    if layer:
        nk_ref[0:layer] = pk_ref[...]
        nv_ref[0:layer] = pv_ref[...]

    lane0 = i * (bb * tn)
    tile0 = pl.multiple_of((lane0 // LANES) * LANES, LANES)
    off0 = lane0 % LANES
    wkt = wk[:, pl.ds(tile0, LANES)]
    wvt = wv[:, pl.ds(tile0, LANES)]
    lane = lax.broadcasted_iota(jnp.int32, (HEAD_DIM, WINDOW), 1)
    tail = lane >= keep
    grow = lax.broadcasted_iota(jnp.int32, (rows, 1), 0) // tn
    units = [(bl, hk) for bl in range(bb) for hk in range(N_KV_HEADS)]
    biases, sinks = [], []
    for hk in range(N_KV_HEADS):
        biases.append(jnp.concatenate(
            [bias_ref[hk * GQA_GROUP + g] for g in range(GQA_GROUP)], axis=0))
        sink = jnp.zeros((rows, 1), F32)
        for g in range(GQA_GROUP):
            sink = jnp.where(grow == g, sinks_ref[layer, hk * GQA_GROUP + g], sink)
        sinks.append(sink)

    scores, values = {}, {}
    for bl in range(bb):
        shift = (keep - off0 - bl * tn) % LANES
        nkb = pltpu.roll(wkt, shift, axis=1)
        nvb = pltpu.roll(wvt, shift, axis=1)
        for hk in range(N_KV_HEADS):
            kt = kc_ref[bl, hk]
            vt = vc_ref[bl, hk]
            nkt = jnp.where(tail, nkb[hk * HEAD_DIM:(hk + 1) * HEAD_DIM, :], 0.0)
            nvt = jnp.where(tail, nvb[hk * HEAD_DIM:(hk + 1) * HEAD_DIM, :], 0.0)
            nk_ref[layer, bl, hk] = jnp.where(tail, nkt, pltpu.roll(kt, keep, axis=1))
            nv_ref[layer, bl, hk] = jnp.where(tail, nvt, pltpu.roll(vt, keep, axis=1))
            keys = jnp.concatenate([kt, nkt], axis=1).astype(BF16)
            values[(bl, hk)] = jnp.concatenate([vt, nvt], axis=1).astype(BF16)
            scores[(bl, hk)] = jnp.dot(q_ref[bl, hk], keys, preferred_element_type=F32)
    probs, dens = {}, {}
    for (bl, hk) in units:
        s = scores[(bl, hk)] + biases[hk]
        m = jnp.maximum(jnp.max(s, axis=-1, keepdims=True), sinks[hk])
        p = jnp.exp(s - m)
        dens[(bl, hk)] = jnp.sum(p, axis=-1, keepdims=True) + jnp.exp(sinks[hk] - m)
        probs[(bl, hk)] = p.astype(BF16)
    for (bl, hk) in units:
        o = lax.dot_general(probs[(bl, hk)], values[(bl, hk)], (((1,), (1,)), ((), ())),
                            preferred_element_type=F32)
        o_ref[bl, hk] = o / dens[(bl, hk)]


def _sattn_call(q4, kn_bt, vn_bt, ck, cv, sinks, rolled, l, nb, tn):
    bb = SATTN_BATCH
    rows = GQA_GROUP * tn
    unit = (bb, N_KV_HEADS, HEAD_DIM, WINDOW)
    cache_shape = jax.ShapeDtypeStruct((l + 1, nb) + unit[1:], F32)
    in_specs = [
        pl.BlockSpec((bb, N_KV_HEADS, rows, HEAD_DIM), lambda i, s: (i, 0, 0, 0)),
        pl.BlockSpec((None,) + unit, lambda i, s: (l, i, 0, 0, 0)),
        pl.BlockSpec((None,) + unit, lambda i, s: (l, i, 0, 0, 0)),
        _const_spec((nb * tn, KV_WIDTH)),
        _const_spec((nb * tn, KV_WIDTH)),
        _const_spec((N_HEADS, tn, 2 * WINDOW)),
    ]
    args = [sinks, q4, ck, cv, kn_bt, vn_bt, jnp.asarray(_sample_bias(tn))]
    if l:
        in_specs += [pl.BlockSpec((l,) + unit, lambda i, s: (0, i, 0, 0, 0))] * 2
        args += list(rolled)
    grid_spec = pltpu.PrefetchScalarGridSpec(
        num_scalar_prefetch=1,
        grid=(nb // bb,),
        in_specs=in_specs,
        out_specs=[
            pl.BlockSpec((bb, N_KV_HEADS, rows, HEAD_DIM), lambda i, s: (i, 0, 0, 0)),
            pl.BlockSpec((l + 1,) + unit, lambda i, s: (0, i, 0, 0, 0)),
            pl.BlockSpec((l + 1,) + unit, lambda i, s: (0, i, 0, 0, 0)),
        ],
        scratch_shapes=[
            pltpu.VMEM((KV_WIDTH, nb * tn), F32),
            pltpu.VMEM((KV_WIDTH, nb * tn), F32),
        ],
    )
    return pl.pallas_call(
        functools.partial(_sattn_kernel, layer=l, nb=nb, tn=tn, bb=bb),
        grid_spec=grid_spec,
        out_shape=[jax.ShapeDtypeStruct((nb, N_KV_HEADS, rows, HEAD_DIM), F32),
                   cache_shape, cache_shape],
        compiler_params=pltpu.CompilerParams(
            dimension_semantics=("arbitrary",),
            vmem_limit_bytes=VMEM_LIMIT),
        name="sample_attn",
    )(*args)


def _spost_kernel(x_ref, o_ref, ma_ref, sgb_ref, mod_ref, wb_ref, wout_ref, xo_ref, *, tn):
    gt1 = jnp.concatenate([mod_ref[...]] * tn, axis=0)
    br_b = _bdot(o_ref[...], wb_ref[...])
    merged = ma_ref[...] + sgb_ref[...] * br_b
    xo_ref[...] = x_ref[...] + gt1 * _bdot(merged, wout_ref[...])


def _spost_call(x, o, ma, sgb, mod_s, wb, wout, l, nb, tn):
    return pl.pallas_call(
        functools.partial(_spost_kernel, tn=tn),
        grid=(1,),
        in_specs=[
            _const_spec(x.shape), _const_spec(o.shape), _const_spec(ma.shape),
            _const_spec(sgb.shape),
            pl.BlockSpec((None, nb, D_MODEL), lambda i: (l, 0, 2), pipeline_mode=pl.Buffered(1)),
            _weight_spec(wb, (ATTN_WIDTH, D_MODEL), l),
            _weight_spec(wout, (D_MODEL, D_MODEL), l),
        ],
        out_specs=_whole_out_spec(x.shape),
        out_shape=jax.ShapeDtypeStruct(x.shape, F32),
        compiler_params=pltpu.CompilerParams(
            dimension_semantics=("arbitrary",),
            vmem_limit_bytes=VMEM_LIMIT),
        name="sample_post",
    )(x, o, ma, sgb, mod_s, wb, wout)


def kernel(x_prompt, x_sample, cache_k, cache_v, state_pool, c_prompt, c_sample,
           w_ada, b_ada, norm1_g, w_in, w_pool, pool_scale, attn_sinks, w_a, w_b,
           w_out, norm2_g, w_up, w_down, final_g):
    bp, tp, _ = x_prompt.shape
    nb, tn, _ = x_sample.shape
    assert tp % MIX_ROWS == 0 and tp % FFN_ROWS == 0 and tn <= 8 and nb == LANES
    assert bp % 8 == 0

    mod_p, mod_s = _mod_call(c_prompt, c_sample, w_ada, b_ada)

    w_pool2 = w_pool.reshape(DEPTH, POOL_WIDTH, POOL_GROUP)
    mix_w = [w[0].astype(BF16) for w in (w_in, w_pool2, w_a, w_b, w_out)]
    bias = jnp.asarray(_prompt_bias())
    fg = final_g.reshape(1, D_MODEL)
    g1 = norm1_g.reshape(DEPTH, 1, D_MODEL)
    g2 = norm2_g.reshape(DEPTH, 1, D_MODEL)
    psc = pool_scale.reshape(DEPTH, 1, POOL_WIDTH)

    xs = x_sample.transpose(1, 0, 2).reshape(tn * nb, D_MODEL)
    ck = cache_k.transpose(0, 1, 3, 4, 2)
    cv = cache_v.transpose(0, 1, 3, 4, 2)
    prefix_t = state_pool.transpose(0, 2, 1, 3)

    xp = x_prompt
    kp, vp, pp, ps, rolled = [], [], [], [], None
    for l in range(DEPTH):
        last = l == DEPTH - 1
        win, wpool, wa, wb, wout = mix_w
        wpool = wpool.reshape(len(POOL_WINDOWS), POOL_GROUP, POOL_GROUP)
        xp, nk, nv, npool, wup, wdn = _pmix_call(xp, mod_p, g1, win, wpool, psc, wa, wb, wout,
                                                 attn_sinks, bias, l, [(w_up, l), (w_down, l)])
        kp.append(nk); vp.append(nv); pp.append(npool)

        q, kn, vn, ma, sgb, npool_s = _sproj_call(xs, mod_s, g1, win, wpool, psc, wa,
                                                  prefix_t, l, nb, tn)
        q4 = q.reshape(tn, nb, N_KV_HEADS, GQA_GROUP, HEAD_DIM).transpose(1, 2, 3, 0, 4)
        q4 = q4.reshape(nb, N_KV_HEADS, GQA_GROUP * tn, HEAD_DIM).astype(BF16)
        by_batch = lambda a: a.reshape(tn, nb, KV_WIDTH).transpose(1, 0, 2).reshape(nb * tn, KV_WIDTH)
        o4, *rolled = _sattn_call(q4, by_batch(kn), by_batch(vn), ck, cv, attn_sinks, rolled,
                                   l, nb, tn)
        o = o4.reshape(nb, N_KV_HEADS, GQA_GROUP, tn, HEAD_DIM).transpose(3, 0, 1, 2, 4)
        o = o.reshape(tn * nb, ATTN_WIDTH)
        xs = _spost_call(xs, o, ma, sgb, mod_s, wb, wout, l, nb, tn)
        ps.append(npool_s)

        nxt = [] if last else [(w, l + 1) for w in (w_in, w_pool2, w_a, w_b, w_out)]
        xp, xs, *mix_w = _ffn_call(xp, xs, mod_p, mod_s, g2, wup, wdn, fg, l, last, nxt)

    to_bt = lambda a: a.reshape(tn, nb, a.shape[-1]).transpose(1, 0, 2)

    kv_shape_p = (DEPTH, bp, WINDOW, N_KV_HEADS, HEAD_DIM)
    kv_shape_s = (DEPTH, nb, WINDOW, N_KV_HEADS, HEAD_DIM)
    return (xp,
            to_bt(xs),
            jnp.stack(kp).reshape(kv_shape_p),
            jnp.stack(vp).reshape(kv_shape_p),
            jnp.stack(pp),
            rolled[0].transpose(0, 1, 4, 2, 3),
            rolled[1].transpose(0, 1, 4, 2, 3),
            jnp.stack(ps).transpose(0, 2, 1, 3))
```

```python
import functools

import numpy as np
import jax
import jax.numpy as jnp
from jax import lax
from jax.experimental import pallas as pl
from jax.experimental.pallas import tpu as pltpu

D_MODEL = 1024
DEPTH = 2
PAST_LEN = 16384
POOL_WIDTH = D_MODEL // 2
POOL_WINDOWS = (2, 4, 8, 16)
POOL_GROUP = POOL_WIDTH // len(POOL_WINDOWS)
POOL_PAD = max(POOL_WINDOWS) - 1
N_HEADS = 8
N_KV_HEADS = 2
HEAD_DIM = 64
GQA_GROUP = N_HEADS // N_KV_HEADS
ATTN_WIDTH = N_HEADS * HEAD_DIM
KV_WIDTH = N_KV_HEADS * HEAD_DIM
WINDOW = 128
ATTN_BLOCK = 128
D_FF = 4 * D_MODEL
RMS_EPS = 1e-6
NEG_INF = -1e30

OFF_U = 0
OFF_Q = OFF_U + POOL_WIDTH
OFF_K = OFF_Q + ATTN_WIDTH
OFF_V = OFF_K + KV_WIDTH
OFF_GA = OFF_V + KV_WIDTH
OFF_GB = OFF_GA + D_MODEL
IN_WIDTH = OFF_GB + D_MODEL
MOD_SLAB = 3 * D_MODEL

LANES = 128
HIST = 16
assert all(w & (w - 1) == 0 for w in POOL_WINDOWS) and list(POOL_WINDOWS) == sorted(POOL_WINDOWS)
assert HIST >= POOL_PAD
VMEM_LIMIT = 56 * 1024 * 1024
SUB_ROWS = 512
MIX_ROWS = 2 * SUB_ROWS
FFN_ROWS = 512
SATTN_BATCH = 8

F32 = jnp.float32
BF16 = jnp.bfloat16


def _bdot(a, b):
    return jnp.dot(a.astype(BF16), b.astype(BF16), preferred_element_type=F32)


def _rms_mod(x, g, sc, sh):
    ms = jnp.mean(x * x, axis=-1, keepdims=True)
    return (x * lax.rsqrt(ms + RMS_EPS) * g) * (1.0 + sc) + sh


def _alibi_slopes():
    return 2.0 ** (-8.0 * (np.arange(N_HEADS) + 1) / N_HEADS)


def _const_spec(shape):
    nd = len(shape)
    return pl.BlockSpec(shape, lambda *_: (0,) * nd, pipeline_mode=pl.Buffered(1))


def _layer_spec(shape, l):
    nd = len(shape)
    return pl.BlockSpec((None,) + tuple(shape), lambda *_: (l,) + (0,) * nd,
                        pipeline_mode=pl.Buffered(1))


def _whole_out_spec(shape):
    nd = len(shape)
    return pl.BlockSpec(shape, lambda *_: (0,) * nd)


def _weight_spec(w, shape, l):
    return _const_spec(shape) if w.shape == tuple(shape) else _layer_spec(shape, l)


def _convert_specs(jobs, n_steps, step_of):
    in_specs, out_specs, out_shapes = [], [], []
    for w, l in jobs:
        _, r, c = w.shape
        rows = r // n_steps
        assert rows * n_steps == r and rows % 16 == 0, (w.shape, n_steps)
        in_specs.append(pl.BlockSpec((None, rows, c), lambda *a, l=l: (l, step_of(*a), 0)))
        out_specs.append(pl.BlockSpec((rows, c), lambda *a: (step_of(*a), 0)))
        out_shapes.append(jax.ShapeDtypeStruct((r, c), BF16))
    return in_specs, out_specs, out_shapes


def _convert_slabs(src_refs, dst_refs):
    for src, dst in zip(src_refs, dst_refs):
        dst[...] = src[...].astype(BF16)


def _mod_kernel(cp_ref, cs_ref, w_ref, b_ref, op_ref, os_ref):
    w = w_ref[...].astype(BF16)
    b = b_ref[...]
    for c_ref, o_ref in ((cp_ref, op_ref), (cs_ref, os_ref)):
        c = c_ref[...]
        s = (c * jax.nn.sigmoid(c)).astype(BF16)
        o_ref[...] = jnp.dot(s, w, preferred_element_type=F32) + b


def _mod_call(c_p, c_s, w_ada, b_ada):
    tn = 2048
    n = 6 * D_MODEL
    mp, ms = c_p.shape[0], c_s.shape[0]
    return pl.pallas_call(
        _mod_kernel,
        grid=(DEPTH, n // tn),
        in_specs=[
            _const_spec((mp, D_MODEL)),
            _const_spec((ms, D_MODEL)),
            pl.BlockSpec((None, D_MODEL, tn), lambda l, j: (l, 0, j)),
            pl.BlockSpec((None, 1, tn), lambda l, j: (l, 0, j)),
        ],
        out_specs=[
            pl.BlockSpec((None, mp, tn), lambda l, j: (l, 0, j)),
            pl.BlockSpec((None, ms, tn), lambda l, j: (l, 0, j)),
        ],
        out_shape=[jax.ShapeDtypeStruct((DEPTH, mp, n), F32),
                   jax.ShapeDtypeStruct((DEPTH, ms, n), F32)],
        compiler_params=pltpu.CompilerParams(
            dimension_semantics=("arbitrary", "arbitrary"),
            vmem_limit_bytes=VMEM_LIMIT),
        name="adaln_mod",
    )(c_p, c_s, w_ada, b_ada.reshape(DEPTH, 1, n))


def _prompt_bias():
    i = np.arange(ATTN_BLOCK)[:, None]
    j = np.arange(2 * ATTN_BLOCK)[None, :]
    dist = i + ATTN_BLOCK - j
    valid = (dist >= 0) & (dist < WINDOW)
    valid_first = valid & (j >= ATTN_BLOCK)
    sl = _alibi_slopes()[:, None, None]
    b = np.where(valid[None], -sl * dist[None], NEG_INF)
    b0 = np.where(valid_first[None], -sl * dist[None], NEG_INF)
    return np.stack([b, b0]).astype(np.float32)


def _pmix_kernel(sinks_ref, x_ref, mod_ref, g1_ref, win_ref, wpool_ref, pscale_ref,
                 wa_ref, wb_ref, wout_ref, bias_ref, *rest, layer, n_cvt):
    cvt_in, rest = rest[:n_cvt], rest[n_cvt:]
    (xo_ref, nk_ref, nv_ref, npool_ref), rest = rest[:4], rest[4:]
    cvt_out, (ubuf, ka, kb, va, vb, obuf) = rest[:n_cvt], rest[n_cvt:]
    _convert_slabs(cvt_in, cvt_out)
    tm, sub = MIX_ROWS, SUB_ROWS
    bi = pl.program_id(0)
    t = pl.program_id(1)
    nt = pl.num_programs(1)

    @pl.when(t == 0)
    def _init():
        ubuf[0:HIST, :] = jnp.zeros((HIST, POOL_WIDTH), F32)
        zero = jnp.zeros((N_KV_HEADS, ATTN_BLOCK, LANES), BF16)
        ka[:, 0:ATTN_BLOCK, :] = zero
        kb[:, 0:ATTN_BLOCK, :] = zero
        va[:, 0:ATTN_BLOCK, :] = zero
        vb[:, 0:ATTN_BLOCK, :] = zero

    mod = mod_ref[pl.ds(bi, 1), :]
    sh1 = mod[:, 0:D_MODEL]
    sc1 = mod[:, D_MODEL:2 * D_MODEL]
    gt1 = mod[:, 2 * D_MODEL:3 * D_MODEL]
    lane = lax.broadcasted_iota(jnp.int32, (sub, LANES), 1)
    low = lane < HEAD_DIM
    low_q = lax.broadcasted_iota(jnp.int32, (ATTN_BLOCK, LANES), 1) < HEAD_DIM
    nt_dims = (((1,), (1,)), ((), ()))
    st = [dict(ro=i * sub) for i in range(tm // sub)]

    def norm(c):
        c["x"] = x_ref[c["ro"]:c["ro"] + sub, :]
        c["h"] = _rms_mod(c["x"], g1_ref[...], sc1, sh1).astype(BF16)

    def proj(c):
        h = c["h"]
        c["u"] = jnp.dot(h, win_ref[:, OFF_U:OFF_Q], preferred_element_type=F32)
        q = jnp.dot(h, win_ref[:, OFF_Q:OFF_K], preferred_element_type=F32)
        c["qb"] = (q * (HEAD_DIM ** -0.5)).astype(BF16)
        kv = jnp.dot(h, win_ref[:, OFF_K:OFF_GA], preferred_element_type=F32)
        c["k"] = kv[:, 0:KV_WIDTH]
        c["v"] = kv[:, KV_WIDTH:2 * KV_WIDTH]

    def pool_sums(c):
        ro, u = c["ro"], c["u"]
        ubuf[HIST + ro:HIST + ro + sub, :] = u
        pos = t * tm + ro + lax.broadcasted_iota(jnp.int32, (sub, 1), 0)
        cur = ubuf[ro:ro + HIST + sub, :]
        d, w = [], 1
        for g, wg in enumerate(POOL_WINDOWS):
            while w < wg:
                cur = cur + pltpu.roll(cur, w, axis=0)
                w *= 2
            ug = u[:, g * POOL_GROUP:(g + 1) * POOL_GROUP]
            cnt = jnp.minimum(wg, pos + 1).astype(F32)
            d.append((cur[HIST:, 0:POOL_GROUP] / cnt - ug).astype(BF16))
            if g + 1 < len(POOL_WINDOWS):
                cur = cur[:, POOL_GROUP:]
        c["d"] = d

    def kv_store(c):
        r = ATTN_BLOCK + c["ro"]
        k, v = c["k"], c["v"]
        kr = pltpu.roll(k, HEAD_DIM, axis=1)
        vr = pltpu.roll(v, HEAD_DIM, axis=1)
        ka[0, r:r + sub, :] = jnp.where(low, k, 0.0).astype(BF16)
        kb[0, r:r + sub, :] = jnp.where(low, 0.0, kr).astype(BF16)
        ka[1, r:r + sub, :] = jnp.where(low, kr, 0.0).astype(BF16)
        kb[1, r:r + sub, :] = jnp.where(low, 0.0, k).astype(BF16)
        va[0, r:r + sub, :] = jnp.where(low, v, 0.0).astype(BF16)
        vb[0, r:r + sub, :] = jnp.where(low, 0.0, vr).astype(BF16)
        va[1, r:r + sub, :] = jnp.where(low, vr, 0.0).astype(BF16)
        vb[1, r:r + sub, :] = jnp.where(low, 0.0, v).astype(BF16)

    def pool_proj(c):
        y = jnp.concatenate(
            [jnp.dot(c["d"][g], wpool_ref[g], preferred_element_type=F32)
             for g in range(len(POOL_WINDOWS))], axis=1)
        c["br_a"] = _bdot(y * pscale_ref[...], wa_ref[...])

    def units(c):
        j0 = c["ro"] // ATTN_BLOCK
        return [(j, hk, pr) for j in range(j0, j0 + sub // ATTN_BLOCK)
                for hk in range(N_KV_HEADS) for pr in range(GQA_GROUP // 2)]

    def scores(c):
        sc = {}
        for (j, hk, pr) in units(c):
            r0 = j * ATTN_BLOCK
            c0 = (hk * (GQA_GROUP // 2) + pr) * LANES
            q2 = c["qb"][r0 - c["ro"]:r0 - c["ro"] + ATTN_BLOCK, c0:c0 + LANES]
            ks = (ka[hk, r0:r0 + 2 * ATTN_BLOCK, :], kb[hk, r0:r0 + 2 * ATTN_BLOCK, :])
            for e in range(2):
                sc[(j, hk, pr, e)] = lax.dot_general(q2, ks[e], nt_dims,
                                                     preferred_element_type=F32)
        c["s"] = sc

    def gate_a(c):
        c["ga"] = jnp.dot(c["h"], win_ref[:, OFF_GA:OFF_GB], preferred_element_type=F32)

    def gate_b(c):
        c["gb"] = jnp.dot(c["h"], win_ref[:, OFF_GB:IN_WIDTH], preferred_element_type=F32)

    def softmax(c):
        probs, dens = {}, {}
        for (j, hk, pr) in units(c):
            for e in range(2):
                head = hk * GQA_GROUP + pr * 2 + e
                if j == 0:
                    bias = jnp.where(t == 0, bias_ref[1, head], bias_ref[0, head])
                else:
                    bias = bias_ref[0, head]
                s = c["s"][(j, hk, pr, e)] + bias
                sink = sinks_ref[layer, head]
                m = jnp.maximum(jnp.max(s, axis=-1, keepdims=True), sink)
                p = jnp.exp(s - m)
                dens[(j, hk, pr, e)] = jnp.sum(p, axis=-1, keepdims=True) + jnp.exp(sink - m)
                probs[(j, hk, pr, e)] = p.astype(BF16)
        c["p"], c["den"] = probs, dens

    def values(c):
        for (j, hk, pr) in units(c):
            r0 = j * ATTN_BLOCK
            c0 = (hk * (GQA_GROUP // 2) + pr) * LANES
            vs = (va[hk, r0:r0 + 2 * ATTN_BLOCK, :], vb[hk, r0:r0 + 2 * ATTN_BLOCK, :])
            o2 = (jnp.dot(c["p"][(j, hk, pr, 0)], vs[0], preferred_element_type=F32)
                  + jnp.dot(c["p"][(j, hk, pr, 1)], vs[1], preferred_element_type=F32))
            den = jnp.where(low_q, c["den"][(j, hk, pr, 0)], c["den"][(j, hk, pr, 1)])
            obuf[r0:r0 + ATTN_BLOCK, c0:c0 + LANES] = o2 / den

    def tail(c):
        ro = c["ro"]
        br_b = _bdot(obuf[ro:ro + sub, :], wb_ref[...])
        merged = jax.nn.sigmoid(c["ga"]) * c["br_a"] + jax.nn.sigmoid(c["gb"]) * br_b
        xo_ref[ro:ro + sub, :] = c["x"] + gt1 * _bdot(merged, wout_ref[...])

    stages = [
        (norm,),
        (proj,),
        (pool_sums, kv_store),
        (pool_proj, scores, gate_a),
        (softmax,),
        (values, gate_b),
        (tail,),
    ]
    order = sorted((2 * s + 3 * i, i, s) for i in range(len(st)) for s in range(len(stages)))
    for _, i, s in order:
        for fn in stages[s]:
            fn(st[i])

    @pl.when(t == nt - 1)
    def _state():
        nk_ref[...] = st[-1]["k"][sub - WINDOW:, :]
        nv_ref[...] = st[-1]["v"][sub - WINDOW:, :]
        npool_ref[...] = ubuf[HIST + tm - POOL_PAD:HIST + tm, :]

    ubuf[0:HIST, :] = ubuf[tm:tm + HIST, :]
    for buf in (ka, kb, va, vb):
        buf[:, 0:ATTN_BLOCK, :] = buf[:, tm:tm + ATTN_BLOCK, :]


def _pmix_call(x, mod_p, g1, win, wpool, pscale, wa, wb, wout, sinks, bias, l, cvt_jobs):
    b, t, _ = x.shape
    tm = MIX_ROWS
    nt = t // tm
    cvt_in, cvt_out, cvt_shapes = _convert_specs(cvt_jobs, b * nt, lambda i, j, s: i * nt + j)
    grid_spec = pltpu.PrefetchScalarGridSpec(
        num_scalar_prefetch=1,
        grid=(b, nt),
        in_specs=[
            pl.BlockSpec((None, tm, D_MODEL), lambda i, j, s: (i, j, 0)),
            pl.BlockSpec((None, b, MOD_SLAB), lambda i, j, s: (l, 0, 0),
                         pipeline_mode=pl.Buffered(1)),
            _layer_spec((1, D_MODEL), l),
            _weight_spec(win, (D_MODEL, IN_WIDTH), l),
            _weight_spec(wpool, (len(POOL_WINDOWS), POOL_GROUP, POOL_GROUP), l),
            _layer_spec((1, POOL_WIDTH), l),
            _weight_spec(wa, (POOL_WIDTH, D_MODEL), l),
            _weight_spec(wb, (ATTN_WIDTH, D_MODEL), l),
            _weight_spec(wout, (D_MODEL, D_MODEL), l),
            _const_spec((2, N_HEADS, ATTN_BLOCK, 2 * ATTN_BLOCK)),
        ] + cvt_in,
        out_specs=[
            pl.BlockSpec((None, tm, D_MODEL), lambda i, j, s: (i, j, 0)),
            pl.BlockSpec((None, WINDOW, KV_WIDTH), lambda i, j, s: (i, 0, 0)),
            pl.BlockSpec((None, WINDOW, KV_WIDTH), lambda i, j, s: (i, 0, 0)),
            pl.BlockSpec((None, POOL_PAD, POOL_WIDTH), lambda i, j, s: (i, 0, 0)),
        ] + cvt_out,
        scratch_shapes=[
            pltpu.VMEM((HIST + tm, POOL_WIDTH), F32),
            pltpu.VMEM((N_KV_HEADS, ATTN_BLOCK + tm, LANES), BF16),
            pltpu.VMEM((N_KV_HEADS, ATTN_BLOCK + tm, LANES), BF16),
            pltpu.VMEM((N_KV_HEADS, ATTN_BLOCK + tm, LANES), BF16),
            pltpu.VMEM((N_KV_HEADS, ATTN_BLOCK + tm, LANES), BF16),
            pltpu.VMEM((tm, ATTN_WIDTH), F32),
        ],
    )
    return pl.pallas_call(
        functools.partial(_pmix_kernel, layer=l, n_cvt=len(cvt_jobs)),
        grid_spec=grid_spec,
        out_shape=[
            jax.ShapeDtypeStruct((b, t, D_MODEL), F32),
            jax.ShapeDtypeStruct((b, WINDOW, KV_WIDTH), F32),
            jax.ShapeDtypeStruct((b, WINDOW, KV_WIDTH), F32),
            jax.ShapeDtypeStruct((b, POOL_PAD, POOL_WIDTH), F32),
        ] + cvt_shapes,
        compiler_params=pltpu.CompilerParams(
            dimension_semantics=("arbitrary", "arbitrary"),
            vmem_limit_bytes=VMEM_LIMIT),
        name="prompt_mixer",
    )(sinks, x, mod_p, g1, win, wpool, pscale, wa, wb, wout, bias, *[w for w, _ in cvt_jobs])


def _ffn_kernel(xp_ref, xs_ref, modp_ref, mods_ref, g2_ref, wup_ref, wdn_ref, fg_ref,
                *rest, final, n_prompt, tiles_per_row, reps, n_cvt):
    cvt_in, (op_ref, os_ref), cvt_out = rest[:n_cvt], rest[n_cvt:n_cvt + 2], rest[n_cvt + 2:]
    _convert_slabs(cvt_in, cvt_out)
    i = pl.program_id(0)

    def ffn(x, sh2, sc2, gt2):
        h2 = _rms_mod(x, g2_ref[...], sc2, sh2)
        ff = _bdot(h2, wup_ref[...])
        ff = jnp.square(jnp.maximum(ff, 0.0))
        y = x + gt2 * _bdot(ff, wdn_ref[...])
        if final:
            ms = jnp.mean(y * y, axis=-1, keepdims=True)
            y = y * lax.rsqrt(ms + RMS_EPS) * fg_ref[...]
        return y

    @pl.when(i < n_prompt)
    def _prompt():
        mod = modp_ref[pl.ds(i // tiles_per_row, 1), :]
        op_ref[...] = ffn(xp_ref[...], mod[:, 0:D_MODEL], mod[:, D_MODEL:2 * D_MODEL],
                          mod[:, 2 * D_MODEL:3 * D_MODEL])

    @pl.when(i == n_prompt)
    def _sample():
        mod = jnp.concatenate([mods_ref[...]] * reps, axis=0)
        os_ref[...] = ffn(xs_ref[...], mod[:, 0:D_MODEL], mod[:, D_MODEL:2 * D_MODEL],
                          mod[:, 2 * D_MODEL:3 * D_MODEL])


def _ffn_call(xp, xs, mod_p, mod_s, g2, wup, wdn, fg, l, final, cvt_jobs):
    b, t, _ = xp.shape
    ms = xs.shape[0]
    nb = mod_s.shape[1]
    tm = FFN_ROWS
    tpr = t // tm
    n_prompt = b * tpr
    last = n_prompt - 1

    def p_idx(i):
        ii = jnp.minimum(i, last)
        return (ii // tpr, ii % tpr, 0)

    cvt_in, cvt_out, cvt_shapes = _convert_specs(cvt_jobs, n_prompt, lambda i: jnp.minimum(i, last))
    return pl.pallas_call(
        functools.partial(_ffn_kernel, final=final, n_prompt=n_prompt, tiles_per_row=tpr,
                          reps=ms // nb, n_cvt=len(cvt_jobs)),
        grid=(n_prompt + 1,),
        in_specs=[
            pl.BlockSpec((None, tm, D_MODEL), p_idx),
            _const_spec((ms, D_MODEL)),
            pl.BlockSpec((None, b, MOD_SLAB), lambda i: (l, 0, 1), pipeline_mode=pl.Buffered(1)),
            pl.BlockSpec((None, nb, MOD_SLAB), lambda i: (l, 0, 1), pipeline_mode=pl.Buffered(1)),
            _layer_spec((1, D_MODEL), l),
            _weight_spec(wup, (D_MODEL, D_FF), l),
            _weight_spec(wdn, (D_FF, D_MODEL), l),
            _const_spec((1, D_MODEL)),
        ] + cvt_in,
        out_specs=[
            pl.BlockSpec((None, tm, D_MODEL), p_idx),
            _whole_out_spec((ms, D_MODEL)),
        ] + cvt_out,
        out_shape=[jax.ShapeDtypeStruct((b, t, D_MODEL), F32),
                   jax.ShapeDtypeStruct((ms, D_MODEL), F32)] + cvt_shapes,
        compiler_params=pltpu.CompilerParams(
            dimension_semantics=("arbitrary",),
            vmem_limit_bytes=VMEM_LIMIT),
        name="ffn_final" if final else "ffn",
    )(xp, xs, mod_p, mod_s, g2, wup, wdn, fg, *[w for w, _ in cvt_jobs])


def _sproj_kernel(x_ref, mod_ref, g1_ref, win_ref, wpool_ref, pscale_ref, wa_ref, pre_ref,
                  q_ref, k_ref, v_ref, ma_ref, sgb_ref, npool_ref, *, nb, tn):
    x = x_ref[...]
    mod = jnp.concatenate([mod_ref[:, 0:2 * D_MODEL]] * tn, axis=0)
    sh1 = mod[:, 0:D_MODEL]
    sc1 = mod[:, D_MODEL:2 * D_MODEL]
    h = _rms_mod(x, g1_ref[...], sc1, sh1).astype(BF16)

    u = jnp.dot(h, win_ref[:, OFF_U:OFF_Q], preferred_element_type=F32)
    def up(r):
        if r < POOL_PAD:
            return pre_ref[r]
        r -= POOL_PAD
        return u[r * nb:(r + 1) * nb, :]

    d_rows = []
    for tt in range(tn):
        parts = []
        for g, w in enumerate(POOL_WINDOWS):
            lo = g * POOL_GROUP
            acc = up(POOL_PAD + tt)[:, lo:lo + POOL_GROUP]
            for s in range(1, w):
                acc = acc + up(POOL_PAD + tt - s)[:, lo:lo + POOL_GROUP]
            cnt = float(min(w, PAST_LEN + tt + 1))
            parts.append(acc / cnt - up(POOL_PAD + tt)[:, lo:lo + POOL_GROUP])
        d_rows.append(parts)
    y = jnp.concatenate(
        [_bdot(jnp.concatenate([d_rows[tt][g] for tt in range(tn)], axis=0), wpool_ref[g])
         for g in range(len(POOL_WINDOWS))], axis=1)
    y = y * pscale_ref[...]
    br_a = _bdot(y, wa_ref[...])
    for r in range(POOL_PAD):
        npool_ref[r] = up(r + tn)

    q = jnp.dot(h, win_ref[:, OFF_Q:OFF_K], preferred_element_type=F32)
    q_ref[...] = q * (HEAD_DIM ** -0.5)
    kv = jnp.dot(h, win_ref[:, OFF_K:OFF_GA], preferred_element_type=F32)
    k_ref[...] = kv[:, 0:KV_WIDTH]
    v_ref[...] = kv[:, KV_WIDTH:2 * KV_WIDTH]
    ga = jnp.dot(h, win_ref[:, OFF_GA:OFF_GB], preferred_element_type=F32)
    ma_ref[...] = jax.nn.sigmoid(ga) * br_a
    gb = jnp.dot(h, win_ref[:, OFF_GB:IN_WIDTH], preferred_element_type=F32)
    sgb_ref[...] = jax.nn.sigmoid(gb)


def _sproj_call(x, mod_s, g1, win, wpool, pscale, wa, prefix_t, l, nb, tn):
    m = x.shape[0]
    shapes = [
        jax.ShapeDtypeStruct((m, ATTN_WIDTH), F32),
        jax.ShapeDtypeStruct((m, KV_WIDTH), F32),
        jax.ShapeDtypeStruct((m, KV_WIDTH), F32),
        jax.ShapeDtypeStruct((m, D_MODEL), F32),
        jax.ShapeDtypeStruct((m, D_MODEL), F32),
        jax.ShapeDtypeStruct((POOL_PAD, nb, POOL_WIDTH), F32),
    ]
    return pl.pallas_call(
        functools.partial(_sproj_kernel, nb=nb, tn=tn),
        grid=(1,),
        in_specs=[
            _const_spec(x.shape),
            pl.BlockSpec((None, nb, MOD_SLAB), lambda i: (l, 0, 0), pipeline_mode=pl.Buffered(1)),
            _layer_spec((1, D_MODEL), l),
            _weight_spec(win, (D_MODEL, IN_WIDTH), l),
            _weight_spec(wpool, (len(POOL_WINDOWS), POOL_GROUP, POOL_GROUP), l),
            _layer_spec((1, POOL_WIDTH), l),
            _weight_spec(wa, (POOL_WIDTH, D_MODEL), l),
            _layer_spec((POOL_PAD, nb, POOL_WIDTH), l),
        ],
        out_specs=[_whole_out_spec(s.shape) for s in shapes],
        out_shape=shapes,
        compiler_params=pltpu.CompilerParams(
            dimension_semantics=("arbitrary",),
            vmem_limit_bytes=VMEM_LIMIT),
        name="sample_proj",
    )(x, mod_s, g1, win, wpool, pscale, wa, prefix_t)


def _sample_bias(tn):
    t = np.arange(tn)[:, None]
    lane = np.arange(2 * WINDOW)[None, :]
    cached = lane < WINDOW
    new_t = lane - (2 * WINDOW - tn)
    dist = np.where(cached, t + WINDOW - lane, t - new_t)
    valid = np.where(cached, (dist >= 0) & (dist < WINDOW), (new_t >= 0) & (dist >= 0))
    sl = _alibi_slopes()[:, None, None]
    return np.where(valid[None], -sl * dist[None], NEG_INF).astype(np.float32)


def _sattn_kernel(sinks_ref, q_ref, kc_ref, vc_ref, kn_ref, vn_ref, bias_ref, *rest,
                   layer, nb, tn, bb):
    if layer:
        pk_ref, pv_ref = rest[:2]
        rest = rest[2:]
    o_ref, nk_ref, nv_ref, wk, wv = rest
    i = pl.program_id(0)
    rows = GQA_GROUP * tn
    keep = WINDOW - tn

    @pl.when(i == 0)
    def _new_rows():
        for c0 in range(0, nb * tn, LANES):
            wk[:, c0:c0 + LANES] = kn_ref[c0:c0 + LANES, :].T
            wv[:, c0:c0 + LANES] = vn_ref[c0:c0 + LANES, :].T

    if layer:
        nk_ref[0:layer] = pk_ref[...]
        nv_ref[0:layer] = pv_ref[...]

    lane0 = i * (bb * tn)
    tile0 = pl.multiple_of((lane0 // LANES) * LANES, LANES)
    off0 = lane0 % LANES
    wkt = wk[:, pl.ds(tile0, LANES)]
    wvt = wv[:, pl.ds(tile0, LANES)]
    lane = lax.broadcasted_iota(jnp.int32, (HEAD_DIM, WINDOW), 1)
    tail = lane >= keep
    grow = lax.broadcasted_iota(jnp.int32, (rows, 1), 0) // tn
    units = [(bl, hk) for bl in range(bb) for hk in range(N_KV_HEADS)]
    biases, sinks = [], []
    for hk in range(N_KV_HEADS):
        biases.append(jnp.concatenate(
            [bias_ref[hk * GQA_GROUP + g] for g in range(GQA_GROUP)], axis=0))
        sink = jnp.zeros((rows, 1), F32)
        for g in range(GQA_GROUP):
            sink = jnp.where(grow == g, sinks_ref[layer, hk * GQA_GROUP + g], sink)
        sinks.append(sink)

    scores, values = {}, {}
    for bl in range(bb):
        shift = (keep - off0 - bl * tn) % LANES
        nkb = pltpu.roll(wkt, shift, axis=1)
        nvb = pltpu.roll(wvt, shift, axis=1)
        for hk in range(N_KV_HEADS):
            kt = kc_ref[bl, hk]
            vt = vc_ref[bl, hk]
            nkt = jnp.where(tail, nkb[hk * HEAD_DIM:(hk + 1) * HEAD_DIM, :], 0.0)
            nvt = jnp.where(tail, nvb[hk * HEAD_DIM:(hk + 1) * HEAD_DIM, :], 0.0)
            nk_ref[layer, bl, hk] = jnp.where(tail, nkt, pltpu.roll(kt, keep, axis=1))
            nv_ref[layer, bl, hk] = jnp.where(tail, nvt, pltpu.roll(vt, keep, axis=1))
            keys = jnp.concatenate([kt, nkt], axis=1).astype(BF16)
            values[(bl, hk)] = jnp.concatenate([vt, nvt], axis=1).astype(BF16)
            scores[(bl, hk)] = jnp.dot(q_ref[bl, hk], keys, preferred_element_type=F32)
    probs, dens = {}, {}
    for (bl, hk) in units:
        s = scores[(bl, hk)] + biases[hk]
        m = jnp.maximum(jnp.max(s, axis=-1, keepdims=True), sinks[hk])
        p = jnp.exp(s - m)
        dens[(bl, hk)] = jnp.sum(p, axis=-1, keepdims=True) + jnp.exp(sinks[hk] - m)
        probs[(bl, hk)] = p.astype(BF16)
    for (bl, hk) in units:
        o = lax.dot_general(probs[(bl, hk)], values[(bl, hk)], (((1,), (1,)), ((), ())),
                            preferred_element_type=F32)
        o_ref[bl, hk] = o / dens[(bl, hk)]


def _sattn_call(q4, kn_bt, vn_bt, ck, cv, sinks, rolled, l, nb, tn):
    bb = SATTN_BATCH
    rows = GQA_GROUP * tn
    unit = (bb, N_KV_HEADS, HEAD_DIM, WINDOW)
    cache_shape = jax.ShapeDtypeStruct((l + 1, nb) + unit[1:], F32)
    in_specs = [
        pl.BlockSpec((bb, N_KV_HEADS, rows, HEAD_DIM), lambda i, s: (i, 0, 0, 0)),
        pl.BlockSpec((None,) + unit, lambda i, s: (l, i, 0, 0, 0)),
        pl.BlockSpec((None,) + unit, lambda i, s: (l, i, 0, 0, 0)),
        _const_spec((nb * tn, KV_WIDTH)),
        _const_spec((nb * tn, KV_WIDTH)),
        _const_spec((N_HEADS, tn, 2 * WINDOW)),
    ]
    args = [sinks, q4, ck, cv, kn_bt, vn_bt, jnp.asarray(_sample_bias(tn))]
    if l:
        in_specs += [pl.BlockSpec((l,) + unit, lambda i, s: (0, i, 0, 0, 0))] * 2
        args += list(rolled)
    grid_spec = pltpu.PrefetchScalarGridSpec(
        num_scalar_prefetch=1,
        grid=(nb // bb,),
        in_specs=in_specs,
        out_specs=[
            pl.BlockSpec((bb, N_KV_HEADS, rows, HEAD_DIM), lambda i, s: (i, 0, 0, 0)),
            pl.BlockSpec((l + 1,) + unit, lambda i, s: (0, i, 0, 0, 0)),
            pl.BlockSpec((l + 1,) + unit, lambda i, s: (0, i, 0, 0, 0)),
        ],
        scratch_shapes=[
            pltpu.VMEM((KV_WIDTH, nb * tn), F32),
            pltpu.VMEM((KV_WIDTH, nb * tn), F32),
        ],
    )
    return pl.pallas_call(
        functools.partial(_sattn_kernel, layer=l, nb=nb, tn=tn, bb=bb),
        grid_spec=grid_spec,
        out_shape=[jax.ShapeDtypeStruct((nb, N_KV_HEADS, rows, HEAD_DIM), F32),
                   cache_shape, cache_shape],
        compiler_params=pltpu.CompilerParams(
            dimension_semantics=("arbitrary",),
            vmem_limit_bytes=VMEM_LIMIT),
        name="sample_attn",
    )(*args)


def _spost_kernel(x_ref, o_ref, ma_ref, sgb_ref, mod_ref, wb_ref, wout_ref, xo_ref, *, tn):
    gt1 = jnp.concatenate([mod_ref[...]] * tn, axis=0)
    br_b = _bdot(o_ref[...], wb_ref[...])
    merged = ma_ref[...] + sgb_ref[...] * br_b
    xo_ref[...] = x_ref[...] + gt1 * _bdot(merged, wout_ref[...])


def _spost_call(x, o, ma, sgb, mod_s, wb, wout, l, nb, tn):
    return pl.pallas_call(
        functools.partial(_spost_kernel, tn=tn),
        grid=(1,),
        in_specs=[
            _const_spec(x.shape), _const_spec(o.shape), _const_spec(ma.shape),
            _const_spec(sgb.shape),
            pl.BlockSpec((None, nb, D_MODEL), lambda i: (l, 0, 2), pipeline_mode=pl.Buffered(1)),
            _weight_spec(wb, (ATTN_WIDTH, D_MODEL), l),
            _weight_spec(wout, (D_MODEL, D_MODEL), l),
        ],
        out_specs=_whole_out_spec(x.shape),
        out_shape=jax.ShapeDtypeStruct(x.shape, F32),
        compiler_params=pltpu.CompilerParams(
            dimension_semantics=("arbitrary",),
            vmem_limit_bytes=VMEM_LIMIT),
        name="sample_post",
    )(x, o, ma, sgb, mod_s, wb, wout)


def kernel(x_prompt, x_sample, cache_k, cache_v, state_pool, c_prompt, c_sample,
           w_ada, b_ada, norm1_g, w_in, w_pool, pool_scale, attn_sinks, w_a, w_b,
           w_out, norm2_g, w_up, w_down, final_g):
    bp, tp, _ = x_prompt.shape
    nb, tn, _ = x_sample.shape
    assert tp % MIX_ROWS == 0 and tp % FFN_ROWS == 0 and tn <= 8 and nb == LANES
    assert bp % 8 == 0

    mod_p, mod_s = _mod_call(c_prompt, c_sample, w_ada, b_ada)

    w_pool2 = w_pool.reshape(DEPTH, POOL_WIDTH, POOL_GROUP)
    mix_w = [w[0].astype(BF16) for w in (w_in, w_pool2, w_a, w_b, w_out)]
    bias = jnp.asarray(_prompt_bias())
    fg = final_g.reshape(1, D_MODEL)
    g1 = norm1_g.reshape(DEPTH, 1, D_MODEL)
    g2 = norm2_g.reshape(DEPTH, 1, D_MODEL)
    psc = pool_scale.reshape(DEPTH, 1, POOL_WIDTH)

    xs = x_sample.transpose(1, 0, 2).reshape(tn * nb, D_MODEL)
    ck = cache_k.transpose(0, 1, 3, 4, 2)
    cv = cache_v.transpose(0, 1, 3, 4, 2)
    prefix_t = state_pool.transpose(0, 2, 1, 3)

    xp = x_prompt
    kp, vp, pp, ps, rolled = [], [], [], [], None
    for l in range(DEPTH):
        last = l == DEPTH - 1
        win, wpool, wa, wb, wout = mix_w
        wpool = wpool.reshape(len(POOL_WINDOWS), POOL_GROUP, POOL_GROUP)
        xp, nk, nv, npool, wup, wdn = _pmix_call(xp, mod_p, g1, win, wpool, psc, wa, wb, wout,
                                                 attn_sinks, bias, l, [(w_up, l), (w_down, l)])
        kp.append(nk); vp.append(nv); pp.append(npool)

        q, kn, vn, ma, sgb, npool_s = _sproj_call(xs, mod_s, g1, win, wpool, psc, wa,
                                                  prefix_t, l, nb, tn)
        q4 = q.reshape(tn, nb, N_KV_HEADS, GQA_GROUP, HEAD_DIM).transpose(1, 2, 3, 0, 4)
        q4 = q4.reshape(nb, N_KV_HEADS, GQA_GROUP * tn, HEAD_DIM).astype(BF16)
        by_batch = lambda a: a.reshape(tn, nb, KV_WIDTH).transpose(1, 0, 2).reshape(nb * tn, KV_WIDTH)
        o4, *rolled = _sattn_call(q4, by_batch(kn), by_batch(vn), ck, cv, attn_sinks, rolled,
                                   l, nb, tn)
        o = o4.reshape(nb, N_KV_HEADS, GQA_GROUP, tn, HEAD_DIM).transpose(3, 0, 1, 2, 4)
        o = o.reshape(tn * nb, ATTN_WIDTH)
        xs = _spost_call(xs, o, ma, sgb, mod_s, wb, wout, l, nb, tn)
        ps.append(npool_s)

        nxt = [] if last else [(w, l + 1) for w in (w_in, w_pool2, w_a, w_b, w_out)]
        xp, xs, *mix_w = _ffn_call(xp, xs, mod_p, mod_s, g2, wup, wdn, fg, l, last, nxt)

    to_bt = lambda a: a.reshape(tn, nb, a.shape[-1]).transpose(1, 0, 2)

    kv_shape_p = (DEPTH, bp, WINDOW, N_KV_HEADS, HEAD_DIM)
    kv_shape_s = (DEPTH, nb, WINDOW, N_KV_HEADS, HEAD_DIM)
    return (xp,
            to_bt(xs),
            jnp.stack(kp).reshape(kv_shape_p),
            jnp.stack(vp).reshape(kv_shape_p),
            jnp.stack(pp),
            rolled[0].transpose(0, 1, 4, 2, 3),
            rolled[1].transpose(0, 1, 4, 2, 3),
            jnp.stack(ps).transpose(0, 2, 1, 3))
```

```python
import functools

import numpy as np
import jax
import jax.numpy as jnp
from jax import lax
from jax.experimental import pallas as pl
from jax.experimental.pallas import tpu as pltpu

D_MODEL = 1024
DEPTH = 2
PAST_LEN = 16384
POOL_WIDTH = D_MODEL // 2
POOL_WINDOWS = (2, 4, 8, 16)
POOL_GROUP = POOL_WIDTH // len(POOL_WINDOWS)
POOL_PAD = max(POOL_WINDOWS) - 1
N_HEADS = 8
N_KV_HEADS = 2
HEAD_DIM = 64
GQA_GROUP = N_HEADS // N_KV_HEADS
ATTN_WIDTH = N_HEADS * HEAD_DIM
KV_WIDTH = N_KV_HEADS * HEAD_DIM
WINDOW = 128
ATTN_BLOCK = 128
D_FF = 4 * D_MODEL
RMS_EPS = 1e-6
NEG_INF = -1e30

OFF_U = 0
OFF_Q = OFF_U + POOL_WIDTH
OFF_K = OFF_Q + ATTN_WIDTH
OFF_V = OFF_K + KV_WIDTH
OFF_GA = OFF_V + KV_WIDTH
OFF_GB = OFF_GA + D_MODEL
IN_WIDTH = OFF_GB + D_MODEL
MOD_SLAB = 3 * D_MODEL

LANES = 128
HIST = 16
assert all(w & (w - 1) == 0 for w in POOL_WINDOWS) and list(POOL_WINDOWS) == sorted(POOL_WINDOWS)
assert HIST >= POOL_PAD
VMEM_LIMIT = 56 * 1024 * 1024
SUB_ROWS = 512
MIX_ROWS = 2 * SUB_ROWS
FFN_ROWS = 512
SATTN_BATCH = 8

F32 = jnp.float32
BF16 = jnp.bfloat16


def _bdot(a, b):
    return jnp.dot(a.astype(BF16), b.astype(BF16), preferred_element_type=F32)


def _rms_mod(x, g, sc, sh):
    ms = jnp.mean(x * x, axis=-1, keepdims=True)
    return (x * lax.rsqrt(ms + RMS_EPS) * g) * (1.0 + sc) + sh


def _alibi_slopes():
    return 2.0 ** (-8.0 * (np.arange(N_HEADS) + 1) / N_HEADS)


def _const_spec(shape):
    nd = len(shape)
    return pl.BlockSpec(shape, lambda *_: (0,) * nd, pipeline_mode=pl.Buffered(1))


def _layer_spec(shape, l):
    nd = len(shape)
    return pl.BlockSpec((None,) + tuple(shape), lambda *_: (l,) + (0,) * nd,
                        pipeline_mode=pl.Buffered(1))


def _whole_out_spec(shape):
    nd = len(shape)
    return pl.BlockSpec(shape, lambda *_: (0,) * nd)


def _weight_spec(w, shape, l):
    return _const_spec(shape) if w.shape == tuple(shape) else _layer_spec(shape, l)


def _convert_specs(jobs, n_steps, step_of):
    in_specs, out_specs, out_shapes = [], [], []
    for w, l in jobs:
        _, r, c = w.shape
        rows = r // n_steps
        assert rows * n_steps == r and rows % 16 == 0, (w.shape, n_steps)
        in_specs.append(pl.BlockSpec((None, rows, c), lambda *a, l=l: (l, step_of(*a), 0)))
        out_specs.append(pl.BlockSpec((rows, c), lambda *a: (step_of(*a), 0)))
        out_shapes.append(jax.ShapeDtypeStruct((r, c), BF16))
    return in_specs, out_specs, out_shapes


def _convert_slabs(src_refs, dst_refs):
    for src, dst in zip(src_refs, dst_refs):
        dst[...] = src[...].astype(BF16)


def _mod_kernel(cp_ref, cs_ref, w_ref, b_ref, op_ref, os_ref):
    w = w_ref[...].astype(BF16)
    b = b_ref[...]
    for c_ref, o_ref in ((cp_ref, op_ref), (cs_ref, os_ref)):
        c = c_ref[...]
        s = (c * jax.nn.sigmoid(c)).astype(BF16)
        o_ref[...] = jnp.dot(s, w, preferred_element_type=F32) + b


def _mod_call(c_p, c_s, w_ada, b_ada):
    tn = 2048
    n = 6 * D_MODEL
    mp, ms = c_p.shape[0], c_s.shape[0]
    return pl.pallas_call(
        _mod_kernel,
        grid=(DEPTH, n // tn),
        in_specs=[
            _const_spec((mp, D_MODEL)),
            _const_spec((ms, D_MODEL)),
            pl.BlockSpec((None, D_MODEL, tn), lambda l, j: (l, 0, j)),
            pl.BlockSpec((None, 1, tn), lambda l, j: (l, 0, j)),
        ],
        out_specs=[
            pl.BlockSpec((None, mp, tn), lambda l, j: (l, 0, j)),
            pl.BlockSpec((None, ms, tn), lambda l, j: (l, 0, j)),
        ],
        out_shape=[jax.ShapeDtypeStruct((DEPTH, mp, n), F32),
                   jax.ShapeDtypeStruct((DEPTH, ms, n), F32)],
        compiler_params=pltpu.CompilerParams(
            dimension_semantics=("arbitrary", "arbitrary"),
            vmem_limit_bytes=VMEM_LIMIT),
        name="adaln_mod",
    )(c_p, c_s, w_ada, b_ada.reshape(DEPTH, 1, n))


def _poolw_kernel(wp_ref, ps_ref, wa_ref, o_ref):
    o_ref[...] = jnp.dot(wp_ref[...] * ps_ref[...], wa_ref[...],
                         preferred_element_type=F32,
                         precision=lax.Precision.HIGHEST).astype(BF16)


def _poolw_call(w_pool, pool_scale, w_a):
    ng = len(POOL_WINDOWS)
    return pl.pallas_call(
        _poolw_kernel,
        grid=(DEPTH, ng),
        in_specs=[
            pl.BlockSpec((None, None, POOL_GROUP, POOL_GROUP), lambda l, g: (l, g, 0, 0)),
            pl.BlockSpec((None, 1, POOL_GROUP), lambda l, g: (l, 0, g)),
            pl.BlockSpec((None, POOL_GROUP, D_MODEL), lambda l, g: (l, g, 0)),
        ],
        out_specs=pl.BlockSpec((None, POOL_GROUP, D_MODEL), lambda l, g: (l, g, 0)),
        out_shape=jax.ShapeDtypeStruct((DEPTH, POOL_WIDTH, D_MODEL), BF16),
        compiler_params=pltpu.CompilerParams(
            dimension_semantics=("arbitrary", "arbitrary"),
            vmem_limit_bytes=VMEM_LIMIT),
        name="pool_weights",
    )(w_pool, pool_scale, w_a)


def _prompt_bias():
    i = np.arange(ATTN_BLOCK)[:, None]
    j = np.arange(2 * ATTN_BLOCK)[None, :]
    dist = i + ATTN_BLOCK - j
    valid = (dist >= 0) & (dist < WINDOW)
    valid_first = valid & (j >= ATTN_BLOCK)
    sl = _alibi_slopes()[:, None, None]
    b = np.where(valid[None], -sl * dist[None], NEG_INF)
    b0 = np.where(valid_first[None], -sl * dist[None], NEG_INF)
    return np.stack([b, b0]).astype(np.float32)


def _pmix_kernel(sinks_ref, x_ref, mod_ref, g1_ref, win_ref, wpa_ref,
                 wb_ref, wout_ref, bias_ref, *rest, layer, n_cvt):
    cvt_in, rest = rest[:n_cvt], rest[n_cvt:]
    (xo_ref, nk_ref, nv_ref, npool_ref), rest = rest[:4], rest[4:]
    cvt_out, (ubuf, ka, kb, va, vb, obuf) = rest[:n_cvt], rest[n_cvt:]
    _convert_slabs(cvt_in, cvt_out)
    tm, sub = MIX_ROWS, SUB_ROWS
    bi = pl.program_id(0)
    t = pl.program_id(1)
    nt = pl.num_programs(1)

    @pl.when(t == 0)
    def _init():
        ubuf[0:HIST, :] = jnp.zeros((HIST, POOL_WIDTH), F32)
        zero = jnp.zeros((N_KV_HEADS, ATTN_BLOCK, LANES), BF16)
        ka[:, 0:ATTN_BLOCK, :] = zero
        kb[:, 0:ATTN_BLOCK, :] = zero
        va[:, 0:ATTN_BLOCK, :] = zero
        vb[:, 0:ATTN_BLOCK, :] = zero

    mod = mod_ref[pl.ds(bi, 1), :]
    sh1 = mod[:, 0:D_MODEL]
    sc1 = mod[:, D_MODEL:2 * D_MODEL]
    gt1 = mod[:, 2 * D_MODEL:3 * D_MODEL]
    lane = lax.broadcasted_iota(jnp.int32, (sub, LANES), 1)
    low = lane < HEAD_DIM
    low_q = lax.broadcasted_iota(jnp.int32, (ATTN_BLOCK, LANES), 1) < HEAD_DIM
    nt_dims = (((1,), (1,)), ((), ()))
    st = [dict(ro=i * sub) for i in range(tm // sub)]

    def norm(c):
        c["x"] = x_ref[c["ro"]:c["ro"] + sub, :]
        c["h"] = _rms_mod(c["x"], g1_ref[...], sc1, sh1).astype(BF16)

    def proj(c):
        h = c["h"]
        c["u"] = jnp.dot(h, win_ref[:, OFF_U:OFF_Q], preferred_element_type=F32)
        q = jnp.dot(h, win_ref[:, OFF_Q:OFF_K], preferred_element_type=F32)
        c["qb"] = (q * (HEAD_DIM ** -0.5)).astype(BF16)
        kv = jnp.dot(h, win_ref[:, OFF_K:OFF_GA], preferred_element_type=F32)
        c["k"] = kv[:, 0:KV_WIDTH]
        c["v"] = kv[:, KV_WIDTH:2 * KV_WIDTH]

    def pool_sums(c):
        ro, u = c["ro"], c["u"]
        ubuf[HIST + ro:HIST + ro + sub, :] = u
        pos = t * tm + ro + lax.broadcasted_iota(jnp.int32, (sub, 1), 0)
        cur = ubuf[ro:ro + HIST + sub, :]
        d, w = [], 1
        for g, wg in enumerate(POOL_WINDOWS):
            while w < wg:
                cur = cur + pltpu.roll(cur, w, axis=0)
                w *= 2
            ug = u[:, g * POOL_GROUP:(g + 1) * POOL_GROUP]
            cnt = jnp.minimum(wg, pos + 1).astype(F32)
            d.append((cur[HIST:, 0:POOL_GROUP] / cnt - ug).astype(BF16))
            if g + 1 < len(POOL_WINDOWS):
                cur = cur[:, POOL_GROUP:]
        c["d"] = d

    def kv_store(c):
        r = ATTN_BLOCK + c["ro"]
        zero = jnp.zeros((sub, LANES), BF16)
        for x, xa, xb in ((c["k"], ka, kb), (c["v"], va, vb)):
            x16 = x.astype(BF16)
            xr16 = pltpu.roll(x, HEAD_DIM, axis=1).astype(BF16)
            xa[0, r:r + sub, :] = jnp.where(low, x16, zero)
            xb[0, r:r + sub, :] = jnp.where(low, zero, xr16)
            xa[1, r:r + sub, :] = jnp.where(low, xr16, zero)
            xb[1, r:r + sub, :] = jnp.where(low, zero, x16)

    def pool_proj(c):
        c["br_a"] = jnp.dot(jnp.concatenate(c["d"], axis=1), wpa_ref[...],
                            preferred_element_type=F32)

    def units(c):
        j0 = c["ro"] // ATTN_BLOCK
        return [(j, hk, pr) for j in range(j0, j0 + sub // ATTN_BLOCK)
                for hk in range(N_KV_HEADS) for pr in range(GQA_GROUP // 2)]

    def scores(c):
        sc = {}
        for (j, hk, pr) in units(c):
            r0 = j * ATTN_BLOCK
            c0 = (hk * (GQA_GROUP // 2) + pr) * LANES
            q2 = c["qb"][r0 - c["ro"]:r0 - c["ro"] + ATTN_BLOCK, c0:c0 + LANES]
            ks = (ka[hk, r0:r0 + 2 * ATTN_BLOCK, :], kb[hk, r0:r0 + 2 * ATTN_BLOCK, :])
            for e in range(2):
                sc[(j, hk, pr, e)] = lax.dot_general(q2, ks[e], nt_dims,
                                                     preferred_element_type=F32)
        c["s"] = sc

    def gate_a(c):
        c["ga"] = jnp.dot(c["h"], win_ref[:, OFF_GA:OFF_GB], preferred_element_type=F32)

    def gate_b(c):
        c["gb"] = jnp.dot(c["h"], win_ref[:, OFF_GB:IN_WIDTH], preferred_element_type=F32)

    def softmax(c):
        probs, dens = {}, {}
        for (j, hk, pr) in units(c):
            for e in range(2):
                head = hk * GQA_GROUP + pr * 2 + e
                if j == 0:
                    bias = jnp.where(t == 0, bias_ref[1, head], bias_ref[0, head])
                else:
                    bias = bias_ref[0, head]
                s = c["s"][(j, hk, pr, e)] + bias
                sink = sinks_ref[layer, head]
                m = jnp.maximum(jnp.max(s, axis=-1, keepdims=True), sink)
                p = jnp.exp(s - m)
                dens[(j, hk, pr, e)] = jnp.sum(p, axis=-1, keepdims=True) + jnp.exp(sink - m)
                probs[(j, hk, pr, e)] = p.astype(BF16)
        c["p"], c["den"] = probs, dens

    def values(c):
        for (j, hk, pr) in units(c):
            r0 = j * ATTN_BLOCK
            c0 = (hk * (GQA_GROUP // 2) + pr) * LANES
            vs = (va[hk, r0:r0 + 2 * ATTN_BLOCK, :], vb[hk, r0:r0 + 2 * ATTN_BLOCK, :])
            o2 = (jnp.dot(c["p"][(j, hk, pr, 0)], vs[0], preferred_element_type=F32)
                  + jnp.dot(c["p"][(j, hk, pr, 1)], vs[1], preferred_element_type=F32))
            den = jnp.where(low_q, c["den"][(j, hk, pr, 0)], c["den"][(j, hk, pr, 1)])
            obuf[r0:r0 + ATTN_BLOCK, c0:c0 + LANES] = o2 / den

    def tail(c):
        ro = c["ro"]
        br_b = _bdot(obuf[ro:ro + sub, :], wb_ref[...])
        merged = jax.nn.sigmoid(c["ga"]) * c["br_a"] + jax.nn.sigmoid(c["gb"]) * br_b
        xo_ref[ro:ro + sub, :] = c["x"] + gt1 * _bdot(merged, wout_ref[...])

    stages = [
        (norm,),
        (proj, gate_a),
        (pool_sums, kv_store),
        (pool_proj, scores, gate_b),
        (softmax,),
        (values,),
        (tail,),
    ]
    order = sorted((2 * s + 3 * i, i, s) for i in range(len(st)) for s in range(len(stages)))
    for _, i, s in order:
        for fn in stages[s]:
            fn(st[i])

    @pl.when(t == nt - 1)
    def _state():
        nk_ref[...] = st[-1]["k"][sub - WINDOW:, :]
        nv_ref[...] = st[-1]["v"][sub - WINDOW:, :]
        npool_ref[...] = ubuf[HIST + tm - POOL_PAD:HIST + tm, :]

    ubuf[0:HIST, :] = ubuf[tm:tm + HIST, :]
    for buf in (ka, kb, va, vb):
        buf[:, 0:ATTN_BLOCK, :] = buf[:, tm:tm + ATTN_BLOCK, :]


def _pmix_call(x, mod_p, g1, win, wpa, wb, wout, sinks, bias, l, cvt_jobs):
    b, t, _ = x.shape
    tm = MIX_ROWS
    nt = t // tm
    cvt_in, cvt_out, cvt_shapes = _convert_specs(cvt_jobs, b * nt, lambda i, j, s: i * nt + j)
    grid_spec = pltpu.PrefetchScalarGridSpec(
        num_scalar_prefetch=1,
        grid=(b, nt),
        in_specs=[
            pl.BlockSpec((None, tm, D_MODEL), lambda i, j, s: (i, j, 0)),
            pl.BlockSpec((None, b, MOD_SLAB), lambda i, j, s: (l, 0, 0),
                         pipeline_mode=pl.Buffered(1)),
            _layer_spec((1, D_MODEL), l),
            _weight_spec(win, (D_MODEL, IN_WIDTH), l),
            _weight_spec(wpa, (POOL_WIDTH, D_MODEL), l),
            _weight_spec(wb, (ATTN_WIDTH, D_MODEL), l),
            _weight_spec(wout, (D_MODEL, D_MODEL), l),
            _const_spec((2, N_HEADS, ATTN_BLOCK, 2 * ATTN_BLOCK)),
        ] + cvt_in,
        out_specs=[
            pl.BlockSpec((None, tm, D_MODEL), lambda i, j, s: (i, j, 0)),
            pl.BlockSpec((None, WINDOW, KV_WIDTH), lambda i, j, s: (i, 0, 0)),
            pl.BlockSpec((None, WINDOW, KV_WIDTH), lambda i, j, s: (i, 0, 0)),
            pl.BlockSpec((None, POOL_PAD, POOL_WIDTH), lambda i, j, s: (i, 0, 0)),
        ] + cvt_out,
        scratch_shapes=[
            pltpu.VMEM((HIST + tm, POOL_WIDTH), F32),
            pltpu.VMEM((N_KV_HEADS, ATTN_BLOCK + tm, LANES), BF16),
            pltpu.VMEM((N_KV_HEADS, ATTN_BLOCK + tm, LANES), BF16),
            pltpu.VMEM((N_KV_HEADS, ATTN_BLOCK + tm, LANES), BF16),
            pltpu.VMEM((N_KV_HEADS, ATTN_BLOCK + tm, LANES), BF16),
            pltpu.VMEM((tm, ATTN_WIDTH), F32),
        ],
    )
    return pl.pallas_call(
        functools.partial(_pmix_kernel, layer=l, n_cvt=len(cvt_jobs)),
        grid_spec=grid_spec,
        out_shape=[
            jax.ShapeDtypeStruct((b, t, D_MODEL), F32),
            jax.ShapeDtypeStruct((b, WINDOW, KV_WIDTH), F32),
            jax.ShapeDtypeStruct((b, WINDOW, KV_WIDTH), F32),
            jax.ShapeDtypeStruct((b, POOL_PAD, POOL_WIDTH), F32),
        ] + cvt_shapes,
        compiler_params=pltpu.CompilerParams(
            dimension_semantics=("arbitrary", "arbitrary"),
            vmem_limit_bytes=VMEM_LIMIT),
        name="prompt_mixer",
    )(sinks, x, mod_p, g1, win, wpa, wb, wout, bias, *[w for w, _ in cvt_jobs])


def _ffn_kernel(xp_ref, xs_ref, modp_ref, mods_ref, g2_ref, wup_ref, wdn_ref, fg_ref,
                *rest, final, n_prompt, tiles_per_row, reps, n_cvt):
    cvt_in, (op_ref, os_ref), cvt_out = rest[:n_cvt], rest[n_cvt:n_cvt + 2], rest[n_cvt + 2:]
    _convert_slabs(cvt_in, cvt_out)
    i = pl.program_id(0)

    def ffn(x, sh2, sc2, gt2):
        h2 = _rms_mod(x, g2_ref[...], sc2, sh2)
        ff = _bdot(h2, wup_ref[...])
        ff = jnp.square(jnp.maximum(ff, 0.0))
        y = x + gt2 * _bdot(ff, wdn_ref[...])
        if final:
            ms = jnp.mean(y * y, axis=-1, keepdims=True)
            y = y * lax.rsqrt(ms + RMS_EPS) * fg_ref[...]
        return y

    @pl.when(i < n_prompt)
    def _prompt():
        mod = modp_ref[pl.ds(i // tiles_per_row, 1), :]
        op_ref[...] = ffn(xp_ref[...], mod[:, 0:D_MODEL], mod[:, D_MODEL:2 * D_MODEL],
                          mod[:, 2 * D_MODEL:3 * D_MODEL])

    @pl.when(i == n_prompt)
    def _sample():
        mod = jnp.concatenate([mods_ref[...]] * reps, axis=0)
        os_ref[...] = ffn(xs_ref[...], mod[:, 0:D_MODEL], mod[:, D_MODEL:2 * D_MODEL],
                          mod[:, 2 * D_MODEL:3 * D_MODEL])


def _ffn_call(xp, xs, mod_p, mod_s, g2, wup, wdn, fg, l, final, cvt_jobs):
    b, t, _ = xp.shape
    ms = xs.shape[0]
    nb = mod_s.shape[1]
    tm = FFN_ROWS
    tpr = t // tm
    n_prompt = b * tpr
    last = n_prompt - 1

    def p_idx(i):
        ii = jnp.minimum(i, last)
        return (ii // tpr, ii % tpr, 0)

    cvt_in, cvt_out, cvt_shapes = _convert_specs(cvt_jobs, n_prompt, lambda i: jnp.minimum(i, last))
    return pl.pallas_call(
        functools.partial(_ffn_kernel, final=final, n_prompt=n_prompt, tiles_per_row=tpr,
                          reps=ms // nb, n_cvt=len(cvt_jobs)),
        grid=(n_prompt + 1,),
        in_specs=[
            pl.BlockSpec((None, tm, D_MODEL), p_idx),
            _const_spec((ms, D_MODEL)),
            pl.BlockSpec((None, b, MOD_SLAB), lambda i: (l, 0, 1), pipeline_mode=pl.Buffered(1)),
            pl.BlockSpec((None, nb, MOD_SLAB), lambda i: (l, 0, 1), pipeline_mode=pl.Buffered(1)),
            _layer_spec((1, D_MODEL), l),
            _weight_spec(wup, (D_MODEL, D_FF), l),
            _weight_spec(wdn, (D_FF, D_MODEL), l),
            _const_spec((1, D_MODEL)),
        ] + cvt_in,
        out_specs=[
            pl.BlockSpec((None, tm, D_MODEL), p_idx),
            _whole_out_spec((ms, D_MODEL)),
        ] + cvt_out,
        out_shape=[jax.ShapeDtypeStruct((b, t, D_MODEL), F32),
                   jax.ShapeDtypeStruct((ms, D_MODEL), F32)] + cvt_shapes,
        compiler_params=pltpu.CompilerParams(
            dimension_semantics=("arbitrary",),
            vmem_limit_bytes=VMEM_LIMIT),
        name="ffn_final" if final else "ffn",
    )(xp, xs, mod_p, mod_s, g2, wup, wdn, fg, *[w for w, _ in cvt_jobs])


def _sproj_kernel(x_ref, mod_ref, g1_ref, win_ref, wpa_ref, pre_ref,
                  q_ref, k_ref, v_ref, ma_ref, sgb_ref, npool_ref, *, nb, tn):
    x = x_ref[...]
    mod = jnp.concatenate([mod_ref[:, 0:2 * D_MODEL]] * tn, axis=0)
    sh1 = mod[:, 0:D_MODEL]
    sc1 = mod[:, D_MODEL:2 * D_MODEL]
    h = _rms_mod(x, g1_ref[...], sc1, sh1).astype(BF16)

    u = jnp.dot(h, win_ref[:, OFF_U:OFF_Q], preferred_element_type=F32)
    def up(r):
        if r < POOL_PAD:
            return pre_ref[r]
        r -= POOL_PAD
        return u[r * nb:(r + 1) * nb, :]

    d_rows = []
    for tt in range(tn):
        parts = []
        for g, w in enumerate(POOL_WINDOWS):
            lo = g * POOL_GROUP
            acc = up(POOL_PAD + tt)[:, lo:lo + POOL_GROUP]
            for s in range(1, w):
                acc = acc + up(POOL_PAD + tt - s)[:, lo:lo + POOL_GROUP]
            cnt = float(min(w, PAST_LEN + tt + 1))
            parts.append(acc / cnt - up(POOL_PAD + tt)[:, lo:lo + POOL_GROUP])
        d_rows.append(parts)
    d = jnp.concatenate([jnp.concatenate(parts, axis=1) for parts in d_rows], axis=0)
    br_a = _bdot(d, wpa_ref[...])
    for r in range(POOL_PAD):
        npool_ref[r] = up(r + tn)

    q = jnp.dot(h, win_ref[:, OFF_Q:OFF_K], preferred_element_type=F32)
    q_ref[...] = q * (HEAD_DIM ** -0.5)
    kv = jnp.dot(h, win_ref[:, OFF_K:OFF_GA], preferred_element_type=F32)
    k_ref[...] = kv[:, 0:KV_WIDTH]
    v_ref[...] = kv[:, KV_WIDTH:2 * KV_WIDTH]
    ga = jnp.dot(h, win_ref[:, OFF_GA:OFF_GB], preferred_element_type=F32)
    ma_ref[...] = jax.nn.sigmoid(ga) * br_a
    gb = jnp.dot(h, win_ref[:, OFF_GB:IN_WIDTH], preferred_element_type=F32)
    sgb_ref[...] = jax.nn.sigmoid(gb)


def _sproj_call(x, mod_s, g1, win, wpa, prefix_t, l, nb, tn):
    m = x.shape[0]
    shapes = [
        jax.ShapeDtypeStruct((m, ATTN_WIDTH), F32),
        jax.ShapeDtypeStruct((m, KV_WIDTH), F32),
        jax.ShapeDtypeStruct((m, KV_WIDTH), F32),
        jax.ShapeDtypeStruct((m, D_MODEL), F32),
        jax.ShapeDtypeStruct((m, D_MODEL), F32),
        jax.ShapeDtypeStruct((POOL_PAD, nb, POOL_WIDTH), F32),
    ]
    return pl.pallas_call(
        functools.partial(_sproj_kernel, nb=nb, tn=tn),
        grid=(1,),
        in_specs=[
            _const_spec(x.shape),
            pl.BlockSpec((None, nb, MOD_SLAB), lambda i: (l, 0, 0), pipeline_mode=pl.Buffered(1)),
            _layer_spec((1, D_MODEL), l),
            _weight_spec(win, (D_MODEL, IN_WIDTH), l),
            _weight_spec(wpa, (POOL_WIDTH, D_MODEL), l),
            _layer_spec((POOL_PAD, nb, POOL_WIDTH), l),
        ],
        out_specs=[_whole_out_spec(s.shape) for s in shapes],
        out_shape=shapes,
        compiler_params=pltpu.CompilerParams(
            dimension_semantics=("arbitrary",),
            vmem_limit_bytes=VMEM_LIMIT),
        name="sample_proj",
    )(x, mod_s, g1, win, wpa, prefix_t)


def _sample_bias(tn):
    t = np.arange(tn)[:, None]
    lane = np.arange(2 * WINDOW)[None, :]
    cached = lane < WINDOW
    new_t = lane - (2 * WINDOW - tn)
    dist = np.where(cached, t + WINDOW - lane, t - new_t)
    valid = np.where(cached, (dist >= 0) & (dist < WINDOW), (new_t >= 0) & (dist >= 0))
    sl = _alibi_slopes()[:, None, None]
    return np.where(valid[None], -sl * dist[None], NEG_INF).astype(np.float32)


def _sattn_kernel(sinks_ref, q_ref, kc_ref, vc_ref, kn_ref, vn_ref, bias_ref, *rest,
                   layer, nb, tn, bb):
    if layer:
        pk_ref, pv_ref = rest[:2]
        rest = rest[2:]
    o_ref, nk_ref, nv_ref, wk, wv = rest
    i = pl.program_id(0)
    rows = GQA_GROUP * tn
    keep = WINDOW - tn

    @pl.when(i == 0)
    def _new_rows():
        for c0 in range(0, nb * tn, LANES):
            wk[:, c0:c0 + LANES] = kn_ref[c0:c0 + LANES, :].T
            wv[:, c0:c0 + LANES] = vn_ref[c0:c0 + LANES, :].T

    if layer:
        nk_ref[0:layer] = pk_ref[...]
        nv_ref[0:layer] = pv_ref[...]

    lane0 = i * (bb * tn)
    tile0 = pl.multiple_of((lane0 // LANES) * LANES, LANES)
    off0 = lane0 % LANES
    wkt = wk[:, pl.ds(tile0, LANES)]
    wvt = wv[:, pl.ds(tile0, LANES)]
    lane = lax.broadcasted_iota(jnp.int32, (HEAD_DIM, WINDOW), 1)
    tail = lane >= keep
    grow = lax.broadcasted_iota(jnp.int32, (rows, 1), 0) // tn
    units = [(bl, hk) for bl in range(bb) for hk in range(N_KV_HEADS)]
    biases, sinks = [], []
    for hk in range(N_KV_HEADS):
        biases.append(jnp.concatenate(
            [bias_ref[hk * GQA_GROUP + g] for g in range(GQA_GROUP)], axis=0))
        sink = jnp.zeros((rows, 1), F32)
        for g in range(GQA_GROUP):
            sink = jnp.where(grow == g, sinks_ref[layer, hk * GQA_GROUP + g], sink)
        sinks.append(sink)

    scores, values = {}, {}
    for bl in range(bb):
        shift = (keep - off0 - bl * tn) % LANES
        nkb = pltpu.roll(wkt, shift, axis=1)
        nvb = pltpu.roll(wvt, shift, axis=1)
        for hk in range(N_KV_HEADS):
            kt = kc_ref[bl, hk]
            vt = vc_ref[bl, hk]
            nkt = jnp.where(tail, nkb[hk * HEAD_DIM:(hk + 1) * HEAD_DIM, :], 0.0)
            nvt = jnp.where(tail, nvb[hk * HEAD_DIM:(hk + 1) * HEAD_DIM, :], 0.0)
            nk_ref[layer, bl, hk] = jnp.where(tail, nkt, pltpu.roll(kt, keep, axis=1))
            nv_ref[layer, bl, hk] = jnp.where(tail, nvt, pltpu.roll(vt, keep, axis=1))
            keys = jnp.concatenate([kt, nkt], axis=1).astype(BF16)
            values[(bl, hk)] = jnp.concatenate([vt, nvt], axis=1).astype(BF16)
            scores[(bl, hk)] = jnp.dot(q_ref[bl, hk], keys, preferred_element_type=F32)
    probs, dens = {}, {}
    for (bl, hk) in units:
        s = scores[(bl, hk)] + biases[hk]
        m = jnp.maximum(jnp.max(s, axis=-1, keepdims=True), sinks[hk])
        p = jnp.exp(s - m)
        dens[(bl, hk)] = jnp.sum(p, axis=-1, keepdims=True) + jnp.exp(sinks[hk] - m)
        probs[(bl, hk)] = p.astype(BF16)
    for (bl, hk) in units:
        o = lax.dot_general(probs[(bl, hk)], values[(bl, hk)], (((1,), (1,)), ((), ())),
                            preferred_element_type=F32)
        o_ref[bl, hk] = o / dens[(bl, hk)]


def _sattn_call(q4, kn_bt, vn_bt, ck, cv, sinks, rolled, l, nb, tn):
    bb = SATTN_BATCH
    rows = GQA_GROUP * tn
    unit = (bb, N_KV_HEADS, HEAD_DIM, WINDOW)
    cache_shape = jax.ShapeDtypeStruct((l + 1, nb) + unit[1:], F32)
    in_specs = [
        pl.BlockSpec((bb, N_KV_HEADS, rows, HEAD_DIM), lambda i, s: (i, 0, 0, 0)),
        pl.BlockSpec((None,) + unit, lambda i, s: (l, i, 0, 0, 0)),
        pl.BlockSpec((None,) + unit, lambda i, s: (l, i, 0, 0, 0)),
        _const_spec((nb * tn, KV_WIDTH)),
        _const_spec((nb * tn, KV_WIDTH)),
        _const_spec((N_HEADS, tn, 2 * WINDOW)),
    ]
    args = [sinks, q4, ck, cv, kn_bt, vn_bt, jnp.asarray(_sample_bias(tn))]
    if l:
        in_specs += [pl.BlockSpec((l,) + unit, lambda i, s: (0, i, 0, 0, 0))] * 2
        args += list(rolled)
    grid_spec = pltpu.PrefetchScalarGridSpec(
        num_scalar_prefetch=1,
        grid=(nb // bb,),
        in_specs=in_specs,
        out_specs=[
            pl.BlockSpec((bb, N_KV_HEADS, rows, HEAD_DIM), lambda i, s: (i, 0, 0, 0)),
            pl.BlockSpec((l + 1,) + unit, lambda i, s: (0, i, 0, 0, 0)),
            pl.BlockSpec((l + 1,) + unit, lambda i, s: (0, i, 0, 0, 0)),
        ],
        scratch_shapes=[
            pltpu.VMEM((KV_WIDTH, nb * tn), F32),
            pltpu.VMEM((KV_WIDTH, nb * tn), F32),
        ],
    )
    return pl.pallas_call(
        functools.partial(_sattn_kernel, layer=l, nb=nb, tn=tn, bb=bb),
        grid_spec=grid_spec,
        out_shape=[jax.ShapeDtypeStruct((nb, N_KV_HEADS, rows, HEAD_DIM), F32),
                   cache_shape, cache_shape],
        compiler_params=pltpu.CompilerParams(
            dimension_semantics=("arbitrary",),
            vmem_limit_bytes=VMEM_LIMIT),
        name="sample_attn",
    )(*args)


def _spost_kernel(x_ref, o_ref, ma_ref, sgb_ref, mod_ref, wb_ref, wout_ref, xo_ref, *, tn):
    gt1 = jnp.concatenate([mod_ref[...]] * tn, axis=0)
    br_b = _bdot(o_ref[...], wb_ref[...])
    merged = ma_ref[...] + sgb_ref[...] * br_b
    xo_ref[...] = x_ref[...] + gt1 * _bdot(merged, wout_ref[...])


def _spost_call(x, o, ma, sgb, mod_s, wb, wout, l, nb, tn):
    return pl.pallas_call(
        functools.partial(_spost_kernel, tn=tn),
        grid=(1,),
        in_specs=[
            _const_spec(x.shape), _const_spec(o.shape), _const_spec(ma.shape),
            _const_spec(sgb.shape),
            pl.BlockSpec((None, nb, D_MODEL), lambda i: (l, 0, 2), pipeline_mode=pl.Buffered(1)),
            _weight_spec(wb, (ATTN_WIDTH, D_MODEL), l),
            _weight_spec(wout, (D_MODEL, D_MODEL), l),
        ],
        out_specs=_whole_out_spec(x.shape),
        out_shape=jax.ShapeDtypeStruct(x.shape, F32),
        compiler_params=pltpu.CompilerParams(
            dimension_semantics=("arbitrary",),
            vmem_limit_bytes=VMEM_LIMIT),
        name="sample_post",
    )(x, o, ma, sgb, mod_s, wb, wout)


def kernel(x_prompt, x_sample, cache_k, cache_v, state_pool, c_prompt, c_sample,
           w_ada, b_ada, norm1_g, w_in, w_pool, pool_scale, attn_sinks, w_a, w_b,
           w_out, norm2_g, w_up, w_down, final_g):
    bp, tp, _ = x_prompt.shape
    nb, tn, _ = x_sample.shape
    assert tp % MIX_ROWS == 0 and tp % FFN_ROWS == 0 and tn <= 8 and nb == LANES
    assert bp % 8 == 0

    mod_p, mod_s = _mod_call(c_prompt, c_sample, w_ada, b_ada)

    mix_w = [w[0].astype(BF16) for w in (w_in, w_b, w_out)]
    wpa = _poolw_call(w_pool, pool_scale.reshape(DEPTH, 1, POOL_WIDTH), w_a)
    bias = jnp.asarray(_prompt_bias())
    fg = final_g.reshape(1, D_MODEL)
    g1 = norm1_g.reshape(DEPTH, 1, D_MODEL)
    g2 = norm2_g.reshape(DEPTH, 1, D_MODEL)

    xs = x_sample.transpose(1, 0, 2).reshape(tn * nb, D_MODEL)
    ck = cache_k.transpose(0, 1, 3, 4, 2)
    cv = cache_v.transpose(0, 1, 3, 4, 2)
    prefix_t = state_pool.transpose(0, 2, 1, 3)

    xp = x_prompt
    kp, vp, pp, ps, rolled = [], [], [], [], None
    for l in range(DEPTH):
        last = l == DEPTH - 1
        win, wb, wout = mix_w
        xp, nk, nv, npool, wup, wdn = _pmix_call(xp, mod_p, g1, win, wpa, wb, wout,
                                                 attn_sinks, bias, l, [(w_up, l), (w_down, l)])
        kp.append(nk); vp.append(nv); pp.append(npool)

        q, kn, vn, ma, sgb, npool_s = _sproj_call(xs, mod_s, g1, win, wpa, prefix_t, l, nb, tn)
        q4 = q.reshape(tn, nb, N_KV_HEADS, GQA_GROUP, HEAD_DIM).transpose(1, 2, 3, 0, 4)
        q4 = q4.reshape(nb, N_KV_HEADS, GQA_GROUP * tn, HEAD_DIM).astype(BF16)
        by_batch = lambda a: a.reshape(tn, nb, KV_WIDTH).transpose(1, 0, 2).reshape(nb * tn, KV_WIDTH)
        o4, *rolled = _sattn_call(q4, by_batch(kn), by_batch(vn), ck, cv, attn_sinks, rolled,
                                   l, nb, tn)
        o = o4.reshape(nb, N_KV_HEADS, GQA_GROUP, tn, HEAD_DIM).transpose(3, 0, 1, 2, 4)
        o = o.reshape(tn * nb, ATTN_WIDTH)
        xs = _spost_call(xs, o, ma, sgb, mod_s, wb, wout, l, nb, tn)
        ps.append(npool_s)

        nxt = [] if last else [(w, l + 1) for w in (w_in, w_b, w_out)]
        xp, xs, *mix_w = _ffn_call(xp, xs, mod_p, mod_s, g2, wup, wdn, fg, l, last, nxt)

    to_bt = lambda a: a.reshape(tn, nb, a.shape[-1]).transpose(1, 0, 2)

    kv_shape_p = (DEPTH, bp, WINDOW, N_KV_HEADS, HEAD_DIM)
    kv_shape_s = (DEPTH, nb, WINDOW, N_KV_HEADS, HEAD_DIM)
    return (xp,
            to_bt(xs),
            jnp.stack(kp).reshape(kv_shape_p),
            jnp.stack(vp).reshape(kv_shape_p),
            jnp.stack(pp),
            rolled[0].transpose(0, 1, 4, 2, 3),
            rolled[1].transpose(0, 1, 4, 2, 3),
            jnp.stack(ps).transpose(0, 2, 1, 3))
```

```python
import functools

import numpy as np
import jax
import jax.numpy as jnp
from jax import lax
from jax.experimental import pallas as pl
from jax.experimental.pallas import tpu as pltpu

D_MODEL = 1024
DEPTH = 2
PAST_LEN = 16384
POOL_WIDTH = D_MODEL // 2
POOL_WINDOWS = (2, 4, 8, 16)
POOL_GROUP = POOL_WIDTH // len(POOL_WINDOWS)
POOL_PAD = max(POOL_WINDOWS) - 1
N_HEADS = 8
N_KV_HEADS = 2
HEAD_DIM = 64
GQA_GROUP = N_HEADS // N_KV_HEADS
ATTN_WIDTH = N_HEADS * HEAD_DIM
KV_WIDTH = N_KV_HEADS * HEAD_DIM
WINDOW = 128
ATTN_BLOCK = 128
D_FF = 4 * D_MODEL
RMS_EPS = 1e-6
NEG_INF = -1e30

OFF_U = 0
OFF_Q = OFF_U + POOL_WIDTH
OFF_K = OFF_Q + ATTN_WIDTH
OFF_V = OFF_K + KV_WIDTH
OFF_GA = OFF_V + KV_WIDTH
OFF_GB = OFF_GA + D_MODEL
IN_WIDTH = OFF_GB + D_MODEL
MOD_SLAB = 3 * D_MODEL

LANES = 128
HIST = 16
assert all(w & (w - 1) == 0 for w in POOL_WINDOWS) and list(POOL_WINDOWS) == sorted(POOL_WINDOWS)
assert HIST >= POOL_PAD
VMEM_LIMIT = 56 * 1024 * 1024
SUB_ROWS = 512
MIX_ROWS = 2 * SUB_ROWS
FFN_ROWS = 512
SATTN_BATCH = 16

F32 = jnp.float32
BF16 = jnp.bfloat16


def _bdot(a, b):
    return jnp.dot(a.astype(BF16), b.astype(BF16), preferred_element_type=F32)


def _rms_mod(x, g, sc, sh):
    ms = jnp.mean(x * x, axis=-1, keepdims=True)
    return (x * lax.rsqrt(ms + RMS_EPS) * g) * (1.0 + sc) + sh


def _alibi_slopes():
    return 2.0 ** (-8.0 * (np.arange(N_HEADS) + 1) / N_HEADS)


def _const_spec(shape):
    nd = len(shape)
    return pl.BlockSpec(shape, lambda *_: (0,) * nd, pipeline_mode=pl.Buffered(1))


def _layer_spec(shape, l):
    nd = len(shape)
    return pl.BlockSpec((None,) + tuple(shape), lambda *_: (l,) + (0,) * nd,
                        pipeline_mode=pl.Buffered(1))


def _whole_out_spec(shape):
    nd = len(shape)
    return pl.BlockSpec(shape, lambda *_: (0,) * nd)


def _weight_spec(w, shape, l):
    return _const_spec(shape) if w.shape == tuple(shape) else _layer_spec(shape, l)


def _convert_specs(jobs, n_steps, step_of):
    in_specs, out_specs, out_shapes = [], [], []
    for w, l in jobs:
        _, r, c = w.shape
        rows = r // n_steps
        assert rows * n_steps == r and rows % 16 == 0, (w.shape, n_steps)
        in_specs.append(pl.BlockSpec((None, rows, c), lambda *a, l=l: (l, step_of(*a), 0)))
        out_specs.append(pl.BlockSpec((rows, c), lambda *a: (step_of(*a), 0)))
        out_shapes.append(jax.ShapeDtypeStruct((r, c), BF16))
    return in_specs, out_specs, out_shapes


def _convert_slabs(src_refs, dst_refs):
    for src, dst in zip(src_refs, dst_refs):
        dst[...] = src[...].astype(BF16)


def _mod_kernel(cp_ref, cs_ref, w_ref, b_ref, *rest, n_cvt):
    cvt_in, (op_ref, os_ref), cvt_out = rest[:n_cvt], rest[n_cvt:n_cvt + 2], rest[n_cvt + 2:]
    _convert_slabs(cvt_in, cvt_out)
    w = w_ref[...].astype(BF16)
    b = b_ref[...]
    for c_ref, o_ref in ((cp_ref, op_ref), (cs_ref, os_ref)):
        c = c_ref[...]
        s = (c * jax.nn.sigmoid(c)).astype(BF16)
        o_ref[...] = jnp.dot(s, w, preferred_element_type=F32) + b


def _mod_call(c_p, c_s, w_ada, b_ada, cvt_jobs):
    tn = 1536
    n = 6 * D_MODEL
    nj = n // tn
    mp, ms = c_p.shape[0], c_s.shape[0]
    cvt_in, cvt_out, cvt_shapes = _convert_specs(cvt_jobs, DEPTH * nj, lambda l, j: l * nj + j)
    return pl.pallas_call(
        functools.partial(_mod_kernel, n_cvt=len(cvt_jobs)),
        grid=(DEPTH, nj),
        in_specs=[
            _const_spec((mp, D_MODEL)),
            _const_spec((ms, D_MODEL)),
            pl.BlockSpec((None, D_MODEL, tn), lambda l, j: (l, 0, j)),
            pl.BlockSpec((None, 1, tn), lambda l, j: (l, 0, j)),
        ] + cvt_in,
        out_specs=[
            pl.BlockSpec((None, mp, tn), lambda l, j: (l, 0, j)),
            pl.BlockSpec((None, ms, tn), lambda l, j: (l, 0, j)),
        ] + cvt_out,
        out_shape=[jax.ShapeDtypeStruct((DEPTH, mp, n), F32),
                   jax.ShapeDtypeStruct((DEPTH, ms, n), F32)] + cvt_shapes,
        compiler_params=pltpu.CompilerParams(
            dimension_semantics=("arbitrary", "arbitrary"),
            vmem_limit_bytes=VMEM_LIMIT),
        name="adaln_mod",
    )(c_p, c_s, w_ada, b_ada.reshape(DEPTH, 1, n), *[w for w, _ in cvt_jobs])


def _poolw_kernel(wp_ref, ps_ref, wa_ref, o_ref):
    for g in range(len(POOL_WINDOWS)):
        rows = slice(g * POOL_GROUP, (g + 1) * POOL_GROUP)
        o_ref[rows, :] = jnp.dot(wp_ref[g] * ps_ref[:, rows], wa_ref[rows, :],
                                 preferred_element_type=F32,
                                 precision=lax.Precision.HIGHEST).astype(BF16)


def _poolw_call(w_pool, pool_scale, w_a):
    return pl.pallas_call(
        _poolw_kernel,
        grid=(DEPTH,),
        in_specs=[
            pl.BlockSpec((None,) + w_pool.shape[1:], lambda l: (l, 0, 0, 0)),
            pl.BlockSpec((None, 1, POOL_WIDTH), lambda l: (l, 0, 0)),
            pl.BlockSpec((None, POOL_WIDTH, D_MODEL), lambda l: (l, 0, 0)),
        ],
        out_specs=pl.BlockSpec((None, POOL_WIDTH, D_MODEL), lambda l: (l, 0, 0)),
        out_shape=jax.ShapeDtypeStruct((DEPTH, POOL_WIDTH, D_MODEL), BF16),
        compiler_params=pltpu.CompilerParams(
            dimension_semantics=("arbitrary",),
            vmem_limit_bytes=VMEM_LIMIT),
        name="pool_weights",
    )(w_pool, pool_scale, w_a)


def _prompt_bias():
    i = np.arange(ATTN_BLOCK)[:, None]
    j = np.arange(2 * ATTN_BLOCK)[None, :]
    dist = i + ATTN_BLOCK - j
    valid = (dist >= 0) & (dist < WINDOW)
    valid_first = valid & (j >= ATTN_BLOCK)
    sl = _alibi_slopes()[:, None, None]
    b = np.where(valid[None], -sl * dist[None], NEG_INF)
    b0 = np.where(valid_first[None], -sl * dist[None], NEG_INF)
    return np.stack([b, b0]).astype(np.float32)


def _pmix_kernel(sinks_ref, x_ref, mod_ref, g1_ref, win_ref, wpa_ref,
                 wb_ref, wout_ref, bias_ref, *rest, layer, n_cvt):
    cvt_in, rest = rest[:n_cvt], rest[n_cvt:]
    (xo_ref, nk_ref, nv_ref, npool_ref), rest = rest[:4], rest[4:]
    cvt_out, (ubuf, ka, kb, va, vb, obuf) = rest[:n_cvt], rest[n_cvt:]
    _convert_slabs(cvt_in, cvt_out)
    tm, sub = MIX_ROWS, SUB_ROWS
    bi = pl.program_id(0)
    t = pl.program_id(1)
    nt = pl.num_programs(1)

    @pl.when(t == 0)
    def _init():
        ubuf[0:HIST, :] = jnp.zeros((HIST, POOL_WIDTH), F32)
        zero = jnp.zeros((N_KV_HEADS, ATTN_BLOCK, LANES), BF16)
        ka[:, 0:ATTN_BLOCK, :] = zero
        kb[:, 0:ATTN_BLOCK, :] = zero
        va[:, 0:ATTN_BLOCK, :] = zero
        vb[:, 0:ATTN_BLOCK, :] = zero

    mod = mod_ref[pl.ds(bi, 1), :]
    sh1 = mod[:, 0:D_MODEL]
    sc1 = mod[:, D_MODEL:2 * D_MODEL]
    gt1 = mod[:, 2 * D_MODEL:3 * D_MODEL]
    lane = lax.broadcasted_iota(jnp.int32, (sub, LANES), 1)
    low = lane < HEAD_DIM
    low_q = lax.broadcasted_iota(jnp.int32, (ATTN_BLOCK, LANES), 1) < HEAD_DIM
    nt_dims = (((1,), (1,)), ((), ()))
    st = [dict(ro=i * sub) for i in range(tm // sub)]

    def norm(c):
        c["x"] = x_ref[c["ro"]:c["ro"] + sub, :]
        c["h"] = _rms_mod(c["x"], g1_ref[...], sc1, sh1).astype(BF16)

    def proj(c):
        h = c["h"]
        c["u"] = jnp.dot(h, win_ref[:, OFF_U:OFF_Q], preferred_element_type=F32)
        q = jnp.dot(h, win_ref[:, OFF_Q:OFF_K], preferred_element_type=F32)
        c["qb"] = (q * (HEAD_DIM ** -0.5)).astype(BF16)
        kv = jnp.dot(h, win_ref[:, OFF_K:OFF_GA], preferred_element_type=F32)
        c["k"] = kv[:, 0:KV_WIDTH]
        c["v"] = kv[:, KV_WIDTH:2 * KV_WIDTH]

    def pool_sums(c):
        ro, u = c["ro"], c["u"]
        ubuf[HIST + ro:HIST + ro + sub, :] = u
        pos = t * tm + ro + lax.broadcasted_iota(jnp.int32, (sub, 1), 0)
        cur = ubuf[ro:ro + HIST + sub, :]
        d, w = [], 1
        for g, wg in enumerate(POOL_WINDOWS):
            while w < wg:
                cur = cur + pltpu.roll(cur, w, axis=0)
                w *= 2
            ug = u[:, g * POOL_GROUP:(g + 1) * POOL_GROUP]
            cnt = jnp.minimum(wg, pos + 1).astype(F32)
            d.append((cur[HIST:, 0:POOL_GROUP] / cnt - ug).astype(BF16))
            if g + 1 < len(POOL_WINDOWS):
                cur = cur[:, POOL_GROUP:]
        c["d"] = d

    def kv_store(c):
        r = ATTN_BLOCK + c["ro"]
        zero = jnp.zeros((sub, LANES), BF16)
        for x, xa, xb in ((c["k"], ka, kb), (c["v"], va, vb)):
            x16 = x.astype(BF16)
            xr16 = pltpu.roll(x, HEAD_DIM, axis=1).astype(BF16)
            xa[0, r:r + sub, :] = jnp.where(low, x16, zero)
            xb[0, r:r + sub, :] = jnp.where(low, zero, xr16)
            xa[1, r:r + sub, :] = jnp.where(low, xr16, zero)
            xb[1, r:r + sub, :] = jnp.where(low, zero, x16)

    def pool_proj(c):
        c["br_a"] = jnp.dot(jnp.concatenate(c["d"], axis=1), wpa_ref[...],
                            preferred_element_type=F32)

    def units(c):
        j0 = c["ro"] // ATTN_BLOCK
        return [(j, hk, pr) for j in range(j0, j0 + sub // ATTN_BLOCK)
                for hk in range(N_KV_HEADS) for pr in range(GQA_GROUP // 2)]

    def scores(c):
        sc = {}
        for (j, hk, pr) in units(c):
            r0 = j * ATTN_BLOCK
            c0 = (hk * (GQA_GROUP // 2) + pr) * LANES
            q2 = c["qb"][r0 - c["ro"]:r0 - c["ro"] + ATTN_BLOCK, c0:c0 + LANES]
            ks = (ka[hk, r0:r0 + 2 * ATTN_BLOCK, :], kb[hk, r0:r0 + 2 * ATTN_BLOCK, :])
            for e in range(2):
                sc[(j, hk, pr, e)] = lax.dot_general(q2, ks[e], nt_dims,
                                                     preferred_element_type=F32)
        c["s"] = sc

    def gate_a(c):
        c["ga"] = jnp.dot(c["h"], win_ref[:, OFF_GA:OFF_GB], preferred_element_type=F32)

    def gate_b(c):
        c["gb"] = jnp.dot(c["h"], win_ref[:, OFF_GB:IN_WIDTH], preferred_element_type=F32)

    def softmax(c):
        probs, dens = {}, {}
        for (j, hk, pr) in units(c):
            for e in range(2):
                head = hk * GQA_GROUP + pr * 2 + e
                if j == 0:
                    bias = jnp.where(t == 0, bias_ref[1, head], bias_ref[0, head])
                else:
                    bias = bias_ref[0, head]
                s = c["s"][(j, hk, pr, e)] + bias
                sink = sinks_ref[layer, head]
                m = jnp.maximum(jnp.max(s, axis=-1, keepdims=True), sink)
                p = jnp.exp(s - m)
                dens[(j, hk, pr, e)] = jnp.sum(p, axis=-1, keepdims=True) + jnp.exp(sink - m)
                probs[(j, hk, pr, e)] = p.astype(BF16)
        c["p"], c["den"] = probs, dens

    def values(c):
        for (j, hk, pr) in units(c):
            r0 = j * ATTN_BLOCK
            c0 = (hk * (GQA_GROUP // 2) + pr) * LANES
            vs = (va[hk, r0:r0 + 2 * ATTN_BLOCK, :], vb[hk, r0:r0 + 2 * ATTN_BLOCK, :])
            o2 = (jnp.dot(c["p"][(j, hk, pr, 0)], vs[0], preferred_element_type=F32)
                  + jnp.dot(c["p"][(j, hk, pr, 1)], vs[1], preferred_element_type=F32))
            den = jnp.where(low_q, c["den"][(j, hk, pr, 0)], c["den"][(j, hk, pr, 1)])
            obuf[r0:r0 + ATTN_BLOCK, c0:c0 + LANES] = o2 / den

    def tail(c):
        ro = c["ro"]
        br_b = _bdot(obuf[ro:ro + sub, :], wb_ref[...])
        merged = jax.nn.sigmoid(c["ga"]) * c["br_a"] + jax.nn.sigmoid(c["gb"]) * br_b
        xo_ref[ro:ro + sub, :] = c["x"] + gt1 * _bdot(merged, wout_ref[...])

    stages = [
        (norm,),
        (proj, gate_a),
        (pool_sums, kv_store),
        (pool_proj, scores, gate_b),
        (softmax,),
        (values,),
        (tail,),
    ]
    order = sorted((2 * s + 3 * i, i, s) for i in range(len(st)) for s in range(len(stages)))
    for _, i, s in order:
        for fn in stages[s]:
            fn(st[i])

    @pl.when(t == nt - 1)
    def _state():
        nk_ref[...] = st[-1]["k"][sub - WINDOW:, :].T
        nv_ref[...] = st[-1]["v"][sub - WINDOW:, :].T
        npool_ref[...] = ubuf[HIST + tm - POOL_PAD:HIST + tm, :]

    ubuf[0:HIST, :] = ubuf[tm:tm + HIST, :]
    for buf in (ka, kb, va, vb):
        buf[:, 0:ATTN_BLOCK, :] = buf[:, tm:tm + ATTN_BLOCK, :]


def _pmix_call(x, mod_p, g1, win, wpa, wb, wout, sinks, bias, l, cvt_jobs):
    b, t, _ = x.shape
    tm = MIX_ROWS
    nt = t // tm
    cvt_in, cvt_out, cvt_shapes = _convert_specs(cvt_jobs, b * nt, lambda i, j, s: i * nt + j)
    grid_spec = pltpu.PrefetchScalarGridSpec(
        num_scalar_prefetch=1,
        grid=(b, nt),
        in_specs=[
            pl.BlockSpec((None, tm, D_MODEL), lambda i, j, s: (i, j, 0)),
            pl.BlockSpec((None, b, MOD_SLAB), lambda i, j, s: (l, 0, 0),
                         pipeline_mode=pl.Buffered(1)),
            _layer_spec((1, D_MODEL), l),
            _weight_spec(win, (D_MODEL, IN_WIDTH), l),
            _weight_spec(wpa, (POOL_WIDTH, D_MODEL), l),
            _weight_spec(wb, (ATTN_WIDTH, D_MODEL), l),
            _weight_spec(wout, (D_MODEL, D_MODEL), l),
            _const_spec((2, N_HEADS, ATTN_BLOCK, 2 * ATTN_BLOCK)),
        ] + cvt_in,
        out_specs=[
            pl.BlockSpec((None, tm, D_MODEL), lambda i, j, s: (i, j, 0)),
            pl.BlockSpec((None, KV_WIDTH, WINDOW), lambda i, j, s: (i, 0, 0)),
            pl.BlockSpec((None, KV_WIDTH, WINDOW), lambda i, j, s: (i, 0, 0)),
            pl.BlockSpec((None, POOL_PAD, POOL_WIDTH), lambda i, j, s: (i, 0, 0)),
        ] + cvt_out,
        scratch_shapes=[
            pltpu.VMEM((HIST + tm, POOL_WIDTH), F32),
            pltpu.VMEM((N_KV_HEADS, ATTN_BLOCK + tm, LANES), BF16),
            pltpu.VMEM((N_KV_HEADS, ATTN_BLOCK + tm, LANES), BF16),
            pltpu.VMEM((N_KV_HEADS, ATTN_BLOCK + tm, LANES), BF16),
            pltpu.VMEM((N_KV_HEADS, ATTN_BLOCK + tm, LANES), BF16),
            pltpu.VMEM((tm, ATTN_WIDTH), F32),
        ],
    )
    return pl.pallas_call(
        functools.partial(_pmix_kernel, layer=l, n_cvt=len(cvt_jobs)),
        grid_spec=grid_spec,
        out_shape=[
            jax.ShapeDtypeStruct((b, t, D_MODEL), F32),
            jax.ShapeDtypeStruct((b, KV_WIDTH, WINDOW), F32),
            jax.ShapeDtypeStruct((b, KV_WIDTH, WINDOW), F32),
            jax.ShapeDtypeStruct((b, POOL_PAD, POOL_WIDTH), F32),
        ] + cvt_shapes,
        compiler_params=pltpu.CompilerParams(
            dimension_semantics=("arbitrary", "arbitrary"),
            vmem_limit_bytes=VMEM_LIMIT),
        name="prompt_mixer",
    )(sinks, x, mod_p, g1, win, wpa, wb, wout, bias, *[w for w, _ in cvt_jobs])


def _ffn_kernel(xp_ref, xs_ref, modp_ref, mods_ref, g2_ref, wup_ref, wdn_ref, fg_ref,
                *rest, final, n_prompt, tiles_per_row, reps, n_cvt):
    cvt_in, (op_ref, os_ref), cvt_out = rest[:n_cvt], rest[n_cvt:n_cvt + 2], rest[n_cvt + 2:]
    _convert_slabs(cvt_in, cvt_out)
    i = pl.program_id(0)

    def ffn(x, sh2, sc2, gt2):
        h2 = _rms_mod(x, g2_ref[...], sc2, sh2)
        ff = _bdot(h2, wup_ref[...])
        ff = jnp.square(jnp.maximum(ff, 0.0))
        y = x + gt2 * _bdot(ff, wdn_ref[...])
        if final:
            ms = jnp.mean(y * y, axis=-1, keepdims=True)
            y = y * lax.rsqrt(ms + RMS_EPS) * fg_ref[...]
        return y

    @pl.when(i < n_prompt)
    def _prompt():
        mod = modp_ref[pl.ds(i // tiles_per_row, 1), :]
        op_ref[...] = ffn(xp_ref[...], mod[:, 0:D_MODEL], mod[:, D_MODEL:2 * D_MODEL],
                          mod[:, 2 * D_MODEL:3 * D_MODEL])

    @pl.when(i == n_prompt)
    def _sample():
        mod = jnp.concatenate([mods_ref[...]] * reps, axis=0)
        os_ref[...] = ffn(xs_ref[...], mod[:, 0:D_MODEL], mod[:, D_MODEL:2 * D_MODEL],
                          mod[:, 2 * D_MODEL:3 * D_MODEL])


def _ffn_call(xp, xs, mod_p, mod_s, g2, wup, wdn, fg, l, final, cvt_jobs):
    b, t, _ = xp.shape
    ms = xs.shape[0]
    nb = mod_s.shape[1]
    tm = FFN_ROWS
    tpr = t // tm
    n_prompt = b * tpr
    last = n_prompt - 1

    def p_idx(i):
        ii = jnp.minimum(i, last)
        return (ii // tpr, ii % tpr, 0)

    cvt_in, cvt_out, cvt_shapes = _convert_specs(cvt_jobs, n_prompt, lambda i: jnp.minimum(i, last))
    return pl.pallas_call(
        functools.partial(_ffn_kernel, final=final, n_prompt=n_prompt, tiles_per_row=tpr,
                          reps=ms // nb, n_cvt=len(cvt_jobs)),
        grid=(n_prompt + 1,),
        in_specs=[
            pl.BlockSpec((None, tm, D_MODEL), p_idx),
            _const_spec((ms, D_MODEL)),
            pl.BlockSpec((None, b, MOD_SLAB), lambda i: (l, 0, 1), pipeline_mode=pl.Buffered(1)),
            pl.BlockSpec((None, nb, MOD_SLAB), lambda i: (l, 0, 1), pipeline_mode=pl.Buffered(1)),
            _layer_spec((1, D_MODEL), l),
            _weight_spec(wup, (D_MODEL, D_FF), l),
            _weight_spec(wdn, (D_FF, D_MODEL), l),
            _const_spec((1, D_MODEL)),
        ] + cvt_in,
        out_specs=[
            pl.BlockSpec((None, tm, D_MODEL), p_idx),
            _whole_out_spec((ms, D_MODEL)),
        ] + cvt_out,
        out_shape=[jax.ShapeDtypeStruct((b, t, D_MODEL), F32),
                   jax.ShapeDtypeStruct((ms, D_MODEL), F32)] + cvt_shapes,
        compiler_params=pltpu.CompilerParams(
            dimension_semantics=("arbitrary",),
            vmem_limit_bytes=VMEM_LIMIT),
        name="ffn_final" if final else "ffn",
    )(xp, xs, mod_p, mod_s, g2, wup, wdn, fg, *[w for w, _ in cvt_jobs])


def _sproj_kernel(x_ref, mod_ref, g1_ref, win_ref, wpa_ref, pre_ref,
                  q_ref, k_ref, v_ref, ma_ref, sgb_ref, npool_ref, *, nb, tn):
    x = x_ref[...]
    mod = jnp.concatenate([mod_ref[:, 0:2 * D_MODEL]] * tn, axis=0)
    sh1 = mod[:, 0:D_MODEL]
    sc1 = mod[:, D_MODEL:2 * D_MODEL]
    h = _rms_mod(x, g1_ref[...], sc1, sh1).astype(BF16)

    u = jnp.dot(h, win_ref[:, OFF_U:OFF_Q], preferred_element_type=F32)
    def up(r):
        if r < POOL_PAD:
            return pre_ref[r]
        r -= POOL_PAD
        return u[r * nb:(r + 1) * nb, :]

    d_rows = []
    for tt in range(tn):
        parts = []
        for g, w in enumerate(POOL_WINDOWS):
            lo = g * POOL_GROUP
            acc = up(POOL_PAD + tt)[:, lo:lo + POOL_GROUP]
            for s in range(1, w):
                acc = acc + up(POOL_PAD + tt - s)[:, lo:lo + POOL_GROUP]
            cnt = float(min(w, PAST_LEN + tt + 1))
            parts.append(acc / cnt - up(POOL_PAD + tt)[:, lo:lo + POOL_GROUP])
        d_rows.append(parts)
    d = jnp.concatenate([jnp.concatenate(parts, axis=1) for parts in d_rows], axis=0)
    br_a = _bdot(d, wpa_ref[...])
    for r in range(POOL_PAD):
        npool_ref[r] = up(r + tn)

    q = jnp.dot(h, win_ref[:, OFF_Q:OFF_K], preferred_element_type=F32)
    q_ref[...] = q * (HEAD_DIM ** -0.5)
    kv = jnp.dot(h, win_ref[:, OFF_K:OFF_GA], preferred_element_type=F32)
    k_ref[...] = kv[:, 0:KV_WIDTH]
    v_ref[...] = kv[:, KV_WIDTH:2 * KV_WIDTH]
    ga = jnp.dot(h, win_ref[:, OFF_GA:OFF_GB], preferred_element_type=F32)
    ma_ref[...] = jax.nn.sigmoid(ga) * br_a
    gb = jnp.dot(h, win_ref[:, OFF_GB:IN_WIDTH], preferred_element_type=F32)
    sgb_ref[...] = jax.nn.sigmoid(gb)


def _sproj_call(x, mod_s, g1, win, wpa, prefix_t, l, nb, tn):
    m = x.shape[0]
    shapes = [
        jax.ShapeDtypeStruct((m, ATTN_WIDTH), F32),
        jax.ShapeDtypeStruct((m, KV_WIDTH), F32),
        jax.ShapeDtypeStruct((m, KV_WIDTH), F32),
        jax.ShapeDtypeStruct((m, D_MODEL), F32),
        jax.ShapeDtypeStruct((m, D_MODEL), F32),
        jax.ShapeDtypeStruct((POOL_PAD, nb, POOL_WIDTH), F32),
    ]
    return pl.pallas_call(
        functools.partial(_sproj_kernel, nb=nb, tn=tn),
        grid=(1,),
        in_specs=[
            _const_spec(x.shape),
            pl.BlockSpec((None, nb, MOD_SLAB), lambda i: (l, 0, 0), pipeline_mode=pl.Buffered(1)),
            _layer_spec((1, D_MODEL), l),
            _weight_spec(win, (D_MODEL, IN_WIDTH), l),
            _weight_spec(wpa, (POOL_WIDTH, D_MODEL), l),
            _layer_spec((POOL_PAD, nb, POOL_WIDTH), l),
        ],
        out_specs=[_whole_out_spec(s.shape) for s in shapes],
        out_shape=shapes,
        compiler_params=pltpu.CompilerParams(
            dimension_semantics=("arbitrary",),
            vmem_limit_bytes=VMEM_LIMIT),
        name="sample_proj",
    )(x, mod_s, g1, win, wpa, prefix_t)


def _sample_bias(tn):
    t = np.arange(tn)[:, None]
    lane = np.arange(2 * WINDOW)[None, :]
    cached = lane < WINDOW
    new_t = lane - (2 * WINDOW - tn)
    dist = np.where(cached, t + WINDOW - lane, t - new_t)
    valid = np.where(cached, (dist >= 0) & (dist < WINDOW), (new_t >= 0) & (dist >= 0))
    sl = _alibi_slopes()[:, None, None]
    return np.where(valid[None], -sl * dist[None], NEG_INF).astype(np.float32)


def _sattn_kernel(sinks_ref, q_ref, kc_ref, vc_ref, kn_ref, vn_ref, bias_ref, *rest,
                   layer, nb, tn, bb):
    if layer:
        pk_ref, pv_ref = rest[:2]
        rest = rest[2:]
    o_ref, nk_ref, nv_ref, wk, wv = rest
    i = pl.program_id(0)
    rows = GQA_GROUP * tn
    keep = WINDOW - tn

    @pl.when(i == 0)
    def _new_rows():
        for c0 in range(0, nb * tn, LANES):
            wk[:, c0:c0 + LANES] = kn_ref[c0:c0 + LANES, :].T
            wv[:, c0:c0 + LANES] = vn_ref[c0:c0 + LANES, :].T

    if layer:
        nk_ref[0:layer] = pk_ref[...]
        nv_ref[0:layer] = pv_ref[...]

    lane0 = i * (bb * tn)
    tile0 = pl.multiple_of((lane0 // LANES) * LANES, LANES)
    off0 = lane0 % LANES
    wkt = wk[:, pl.ds(tile0, LANES)]
    wvt = wv[:, pl.ds(tile0, LANES)]
    lane = lax.broadcasted_iota(jnp.int32, (HEAD_DIM, WINDOW), 1)
    tail = lane >= keep
    grow = lax.broadcasted_iota(jnp.int32, (rows, 1), 0) // tn
    units = [(bl, hk) for bl in range(bb) for hk in range(N_KV_HEADS)]
    biases, sinks = [], []
    for hk in range(N_KV_HEADS):
        biases.append(jnp.concatenate(
            [bias_ref[hk * GQA_GROUP + g] for g in range(GQA_GROUP)], axis=0))
        sink = jnp.zeros((rows, 1), F32)
        for g in range(GQA_GROUP):
            sink = jnp.where(grow == g, sinks_ref[layer, hk * GQA_GROUP + g], sink)
        sinks.append(sink)

    scores, values = {}, {}
    for bl in range(bb):
        shift = (keep - off0 - bl * tn) % LANES
        nkb = pltpu.roll(wkt, shift, axis=1)
        nvb = pltpu.roll(wvt, shift, axis=1)
        for hk in range(N_KV_HEADS):
            kt = kc_ref[bl, hk]
            vt = vc_ref[bl, hk]
            nkt = jnp.where(tail, nkb[hk * HEAD_DIM:(hk + 1) * HEAD_DIM, :], 0.0)
            nvt = jnp.where(tail, nvb[hk * HEAD_DIM:(hk + 1) * HEAD_DIM, :], 0.0)
            nk_ref[layer, bl, hk] = jnp.where(tail, nkt, pltpu.roll(kt, keep, axis=1))
            nv_ref[layer, bl, hk] = jnp.where(tail, nvt, pltpu.roll(vt, keep, axis=1))
            keys = jnp.concatenate([kt, nkt], axis=1).astype(BF16)
            values[(bl, hk)] = jnp.concatenate([vt, nvt], axis=1).astype(BF16)
            scores[(bl, hk)] = jnp.dot(q_ref[bl, hk], keys, preferred_element_type=F32)
    probs, dens = {}, {}
    for (bl, hk) in units:
        s = scores[(bl, hk)] + biases[hk]
        m = jnp.maximum(jnp.max(s, axis=-1, keepdims=True), sinks[hk])
        p = jnp.exp(s - m)
        dens[(bl, hk)] = jnp.sum(p, axis=-1, keepdims=True) + jnp.exp(sinks[hk] - m)
        probs[(bl, hk)] = p.astype(BF16)
    for (bl, hk) in units:
        o = lax.dot_general(probs[(bl, hk)], values[(bl, hk)], (((1,), (1,)), ((), ())),
                            preferred_element_type=F32)
        o_ref[bl, hk] = o / dens[(bl, hk)]


def _sattn_call(q4, kn_bt, vn_bt, ck, cv, sinks, rolled, l, nb, tn):
    bb = SATTN_BATCH
    rows = GQA_GROUP * tn
    unit = (bb, N_KV_HEADS, HEAD_DIM, WINDOW)
    cache_shape = jax.ShapeDtypeStruct((l + 1, nb) + unit[1:], F32)
    in_specs = [
        pl.BlockSpec((bb, N_KV_HEADS, rows, HEAD_DIM), lambda i, s: (i, 0, 0, 0)),
        pl.BlockSpec((None,) + unit, lambda i, s: (l, i, 0, 0, 0)),
        pl.BlockSpec((None,) + unit, lambda i, s: (l, i, 0, 0, 0)),
        _const_spec((nb * tn, KV_WIDTH)),
        _const_spec((nb * tn, KV_WIDTH)),
        _const_spec((N_HEADS, tn, 2 * WINDOW)),
    ]
    args = [sinks, q4, ck, cv, kn_bt, vn_bt, jnp.asarray(_sample_bias(tn))]
    if l:
        in_specs += [pl.BlockSpec((l,) + unit, lambda i, s: (0, i, 0, 0, 0))] * 2
        args += list(rolled)
    grid_spec = pltpu.PrefetchScalarGridSpec(
        num_scalar_prefetch=1,
        grid=(nb // bb,),
        in_specs=in_specs,
        out_specs=[
            pl.BlockSpec((bb, N_KV_HEADS, rows, HEAD_DIM), lambda i, s: (i, 0, 0, 0)),
            pl.BlockSpec((l + 1,) + unit, lambda i, s: (0, i, 0, 0, 0)),
            pl.BlockSpec((l + 1,) + unit, lambda i, s: (0, i, 0, 0, 0)),
        ],
        scratch_shapes=[
            pltpu.VMEM((KV_WIDTH, nb * tn), F32),
            pltpu.VMEM((KV_WIDTH, nb * tn), F32),
        ],
    )
    return pl.pallas_call(
        functools.partial(_sattn_kernel, layer=l, nb=nb, tn=tn, bb=bb),
        grid_spec=grid_spec,
        out_shape=[jax.ShapeDtypeStruct((nb, N_KV_HEADS, rows, HEAD_DIM), F32),
                   cache_shape, cache_shape],
        compiler_params=pltpu.CompilerParams(
            dimension_semantics=("arbitrary",),
            vmem_limit_bytes=VMEM_LIMIT),
        name="sample_attn",
    )(*args)


def _spost_kernel(x_ref, o_ref, ma_ref, sgb_ref, mod_ref, wb_ref, wout_ref, xo_ref, *, tn):
    gt1 = jnp.concatenate([mod_ref[...]] * tn, axis=0)
    br_b = _bdot(o_ref[...], wb_ref[...])
    merged = ma_ref[...] + sgb_ref[...] * br_b
    xo_ref[...] = x_ref[...] + gt1 * _bdot(merged, wout_ref[...])


def _spost_call(x, o, ma, sgb, mod_s, wb, wout, l, nb, tn):
    return pl.pallas_call(
        functools.partial(_spost_kernel, tn=tn),
        grid=(1,),
        in_specs=[
            _const_spec(x.shape), _const_spec(o.shape), _const_spec(ma.shape),
            _const_spec(sgb.shape),
            pl.BlockSpec((None, nb, D_MODEL), lambda i: (l, 0, 2), pipeline_mode=pl.Buffered(1)),
            _weight_spec(wb, (ATTN_WIDTH, D_MODEL), l),
            _weight_spec(wout, (D_MODEL, D_MODEL), l),
        ],
        out_specs=_whole_out_spec(x.shape),
        out_shape=jax.ShapeDtypeStruct(x.shape, F32),
        compiler_params=pltpu.CompilerParams(
            dimension_semantics=("arbitrary",),
            vmem_limit_bytes=VMEM_LIMIT),
        name="sample_post",
    )(x, o, ma, sgb, mod_s, wb, wout)


def kernel(x_prompt, x_sample, cache_k, cache_v, state_pool, c_prompt, c_sample,
           w_ada, b_ada, norm1_g, w_in, w_pool, pool_scale, attn_sinks, w_a, w_b,
           w_out, norm2_g, w_up, w_down, final_g):
    bp, tp, _ = x_prompt.shape
    nb, tn, _ = x_sample.shape
    assert tp % MIX_ROWS == 0 and tp % FFN_ROWS == 0 and tn <= 8 and nb == LANES
    assert bp % 8 == 0

    mod_p, mod_s, *mix_w = _mod_call(c_prompt, c_sample, w_ada, b_ada,
                                     [(w, 0) for w in (w_in, w_b, w_out)])
    wpa = _poolw_call(w_pool, pool_scale.reshape(DEPTH, 1, POOL_WIDTH), w_a)
    bias = jnp.asarray(_prompt_bias())
    fg = final_g.reshape(1, D_MODEL)
    g1 = norm1_g.reshape(DEPTH, 1, D_MODEL)
    g2 = norm2_g.reshape(DEPTH, 1, D_MODEL)

    xs = x_sample.transpose(1, 0, 2).reshape(tn * nb, D_MODEL)
    ck = cache_k.transpose(0, 1, 3, 4, 2)
    cv = cache_v.transpose(0, 1, 3, 4, 2)
    prefix_t = state_pool.transpose(0, 2, 1, 3)

    xp = x_prompt
    kp, vp, pp, ps, rolled = [], [], [], [], None
    for l in range(DEPTH):
        last = l == DEPTH - 1
        win, wb, wout = mix_w
        xp, nk, nv, npool, wup, wdn = _pmix_call(xp, mod_p, g1, win, wpa, wb, wout,
                                                 attn_sinks, bias, l, [(w_up, l), (w_down, l)])
        kp.append(nk); vp.append(nv); pp.append(npool)

        q, kn, vn, ma, sgb, npool_s = _sproj_call(xs, mod_s, g1, win, wpa, prefix_t, l, nb, tn)
        q4 = q.reshape(tn, nb, N_KV_HEADS, GQA_GROUP, HEAD_DIM).transpose(1, 2, 3, 0, 4)
        q4 = q4.reshape(nb, N_KV_HEADS, GQA_GROUP * tn, HEAD_DIM).astype(BF16)
        by_batch = lambda a: a.reshape(tn, nb, KV_WIDTH).transpose(1, 0, 2).reshape(nb * tn, KV_WIDTH)
        o4, *rolled = _sattn_call(q4, by_batch(kn), by_batch(vn), ck, cv, attn_sinks, rolled,
                                   l, nb, tn)
        o = o4.reshape(nb, N_KV_HEADS, GQA_GROUP, tn, HEAD_DIM).transpose(3, 0, 1, 2, 4)
        o = o.reshape(tn * nb, ATTN_WIDTH)
        xs = _spost_call(xs, o, ma, sgb, mod_s, wb, wout, l, nb, tn)
        ps.append(npool_s)

        nxt = [] if last else [(w, l + 1) for w in (w_in, w_b, w_out)]
        xp, xs, *mix_w = _ffn_call(xp, xs, mod_p, mod_s, g2, wup, wdn, fg, l, last, nxt)

    to_bt = lambda a: a.reshape(tn, nb, a.shape[-1]).transpose(1, 0, 2)

    kv_shape_p = (DEPTH, bp, N_KV_HEADS, HEAD_DIM, WINDOW)
    return (xp,
            to_bt(xs),
            jnp.stack(kp).reshape(kv_shape_p).transpose(0, 1, 4, 2, 3),
            jnp.stack(vp).reshape(kv_shape_p).transpose(0, 1, 4, 2, 3),
            jnp.stack(pp),
            rolled[0].transpose(0, 1, 4, 2, 3),
            rolled[1].transpose(0, 1, 4, 2, 3),
            jnp.stack(ps).transpose(0, 2, 1, 3))
```

```python
import functools

import numpy as np
import jax
import jax.numpy as jnp
from jax import lax
from jax.experimental import pallas as pl
from jax.experimental.pallas import tpu as pltpu

D_MODEL = 1024
DEPTH = 2
PAST_LEN = 16384
POOL_WIDTH = D_MODEL // 2
POOL_WINDOWS = (2, 4, 8, 16)
POOL_GROUP = POOL_WIDTH // len(POOL_WINDOWS)
POOL_PAD = max(POOL_WINDOWS) - 1
N_HEADS = 8
N_KV_HEADS = 2
HEAD_DIM = 64
GQA_GROUP = N_HEADS // N_KV_HEADS
ATTN_WIDTH = N_HEADS * HEAD_DIM
KV_WIDTH = N_KV_HEADS * HEAD_DIM
WINDOW = 128
ATTN_BLOCK = 128
D_FF = 4 * D_MODEL
RMS_EPS = 1e-6
NEG_INF = -1e30

OFF_U = 0
OFF_Q = OFF_U + POOL_WIDTH
OFF_K = OFF_Q + ATTN_WIDTH
OFF_V = OFF_K + KV_WIDTH
OFF_GA = OFF_V + KV_WIDTH
OFF_GB = OFF_GA + D_MODEL
IN_WIDTH = OFF_GB + D_MODEL
MOD_SLAB = 3 * D_MODEL

LANES = 128
HIST = 16
assert all(w & (w - 1) == 0 for w in POOL_WINDOWS) and list(POOL_WINDOWS) == sorted(POOL_WINDOWS)
assert HIST >= POOL_PAD
VMEM_LIMIT = 56 * 1024 * 1024
SUB_ROWS = 512
MIX_ROWS = 2 * SUB_ROWS
FFN_SUB_ROWS = 512
FFN_ROWS = 2 * FFN_SUB_ROWS
SATTN_BATCH = 16

F32 = jnp.float32
BF16 = jnp.bfloat16


def _bdot(a, b):
    return jnp.dot(a.astype(BF16), b.astype(BF16), preferred_element_type=F32)


def _rms_mod(x, g, sc, sh):
    ms = jnp.mean(x * x, axis=-1, keepdims=True)
    return (x * lax.rsqrt(ms + RMS_EPS) * g) * (1.0 + sc) + sh


def _alibi_slopes():
    return 2.0 ** (-8.0 * (np.arange(N_HEADS) + 1) / N_HEADS)


def _const_spec(shape):
    nd = len(shape)
    return pl.BlockSpec(shape, lambda *_: (0,) * nd, pipeline_mode=pl.Buffered(1))


def _layer_spec(shape, l):
    nd = len(shape)
    return pl.BlockSpec((None,) + tuple(shape), lambda *_: (l,) + (0,) * nd,
                        pipeline_mode=pl.Buffered(1))


def _whole_out_spec(shape):
    nd = len(shape)
    return pl.BlockSpec(shape, lambda *_: (0,) * nd)


def _weight_spec(w, shape, l):
    return _const_spec(shape) if w.shape == tuple(shape) else _layer_spec(shape, l)


def _convert_specs(jobs, n_steps, step_of):
    in_specs, out_specs, out_shapes = [], [], []
    for w, l in jobs:
        _, r, c = w.shape
        rows = r // n_steps
        assert rows * n_steps == r and rows % 16 == 0, (w.shape, n_steps)
        in_specs.append(pl.BlockSpec((None, rows, c), lambda *a, l=l: (l, step_of(*a), 0)))
        out_specs.append(pl.BlockSpec((rows, c), lambda *a: (step_of(*a), 0)))
        out_shapes.append(jax.ShapeDtypeStruct((r, c), BF16))
    return in_specs, out_specs, out_shapes


def _convert_slabs(src_refs, dst_refs):
    for src, dst in zip(src_refs, dst_refs):
        dst[...] = src[...].astype(BF16)


def _mod_kernel(cp_ref, cs_ref, w_ref, b_ref, *rest, n_cvt):
    cvt_in, (op_ref, os_ref), cvt_out = rest[:n_cvt], rest[n_cvt:n_cvt + 2], rest[n_cvt + 2:]
    _convert_slabs(cvt_in, cvt_out)
    w = w_ref[...].astype(BF16)
    b = b_ref[...]
    for c_ref, o_ref in ((cp_ref, op_ref), (cs_ref, os_ref)):
        c = c_ref[...]
        s = (c * jax.nn.sigmoid(c)).astype(BF16)
        o_ref[...] = jnp.dot(s, w, preferred_element_type=F32) + b


def _mod_call(c_p, c_s, w_ada, b_ada, cvt_jobs):
    tn = 1536
    n = 6 * D_MODEL
    nj = n // tn
    mp, ms = c_p.shape[0], c_s.shape[0]
    cvt_in, cvt_out, cvt_shapes = _convert_specs(cvt_jobs, DEPTH * nj, lambda l, j: l * nj + j)
    return pl.pallas_call(
        functools.partial(_mod_kernel, n_cvt=len(cvt_jobs)),
        grid=(DEPTH, nj),
        in_specs=[
            _const_spec((mp, D_MODEL)),
            _const_spec((ms, D_MODEL)),
            pl.BlockSpec((None, D_MODEL, tn), lambda l, j: (l, 0, j)),
            pl.BlockSpec((None, 1, tn), lambda l, j: (l, 0, j)),
        ] + cvt_in,
        out_specs=[
            pl.BlockSpec((None, mp, tn), lambda l, j: (l, 0, j)),
            pl.BlockSpec((None, ms, tn), lambda l, j: (l, 0, j)),
        ] + cvt_out,
        out_shape=[jax.ShapeDtypeStruct((DEPTH, mp, n), F32),
                   jax.ShapeDtypeStruct((DEPTH, ms, n), F32)] + cvt_shapes,
        compiler_params=pltpu.CompilerParams(
            dimension_semantics=("arbitrary", "arbitrary"),
            vmem_limit_bytes=VMEM_LIMIT),
        name="adaln_mod",
    )(c_p, c_s, w_ada, b_ada.reshape(DEPTH, 1, n), *[w for w, _ in cvt_jobs])


def _poolw_kernel(wp_ref, ps_ref, wa_ref, o_ref):
    for g in range(len(POOL_WINDOWS)):
        rows = slice(g * POOL_GROUP, (g + 1) * POOL_GROUP)
        o_ref[rows, :] = jnp.dot(wp_ref[g] * ps_ref[:, rows], wa_ref[rows, :],
                                 preferred_element_type=F32,
                                 precision=lax.Precision.HIGHEST).astype(BF16)


def _poolw_call(w_pool, pool_scale, w_a):
    return pl.pallas_call(
        _poolw_kernel,
        grid=(DEPTH,),
        in_specs=[
            pl.BlockSpec((None,) + w_pool.shape[1:], lambda l: (l, 0, 0, 0)),
            pl.BlockSpec((None, 1, POOL_WIDTH), lambda l: (l, 0, 0)),
            pl.BlockSpec((None, POOL_WIDTH, D_MODEL), lambda l: (l, 0, 0)),
        ],
        out_specs=pl.BlockSpec((None, POOL_WIDTH, D_MODEL), lambda l: (l, 0, 0)),
        out_shape=jax.ShapeDtypeStruct((DEPTH, POOL_WIDTH, D_MODEL), BF16),
        compiler_params=pltpu.CompilerParams(
            dimension_semantics=("arbitrary",),
            vmem_limit_bytes=VMEM_LIMIT),
        name="pool_weights",
    )(w_pool, pool_scale, w_a)


def _prompt_bias():
    i = np.arange(ATTN_BLOCK)[:, None]
    j = np.arange(2 * ATTN_BLOCK)[None, :]
    dist = i + ATTN_BLOCK - j
    valid = (dist >= 0) & (dist < WINDOW)
    valid_first = valid & (j >= ATTN_BLOCK)
    sl = _alibi_slopes()[:, None, None]
    b = np.where(valid[None], -sl * dist[None], NEG_INF)
    b0 = np.where(valid_first[None], -sl * dist[None], NEG_INF)
    return np.stack([b, b0]).astype(np.float32)


def _pmix_kernel(sinks_ref, x_ref, mod_ref, g1_ref, win_ref, wpa_ref,
                 wb_ref, wout_ref, bias_ref, *rest, layer, n_cvt):
    cvt_in, rest = rest[:n_cvt], rest[n_cvt:]
    (xo_ref, nk_ref, nv_ref, npool_ref), rest = rest[:4], rest[4:]
    cvt_out, (ubuf, ka, kb, va, vb, obuf) = rest[:n_cvt], rest[n_cvt:]
    _convert_slabs(cvt_in, cvt_out)
    tm, sub = MIX_ROWS, SUB_ROWS
    bi = pl.program_id(0)
    t = pl.program_id(1)
    nt = pl.num_programs(1)

    @pl.when(t == 0)
    def _init():
        ubuf[0:HIST, :] = jnp.zeros((HIST, POOL_WIDTH), F32)
        zero = jnp.zeros((N_KV_HEADS, ATTN_BLOCK, LANES), BF16)
        ka[:, 0:ATTN_BLOCK, :] = zero
        kb[:, 0:ATTN_BLOCK, :] = zero
        va[:, 0:ATTN_BLOCK, :] = zero
        vb[:, 0:ATTN_BLOCK, :] = zero

    mod = mod_ref[pl.ds(bi, 1), :]
    sh1 = mod[:, 0:D_MODEL]
    sc1 = mod[:, D_MODEL:2 * D_MODEL]
    gt1 = mod[:, 2 * D_MODEL:3 * D_MODEL]
    lane = lax.broadcasted_iota(jnp.int32, (sub, LANES), 1)
    low = lane < HEAD_DIM
    low_q = lax.broadcasted_iota(jnp.int32, (ATTN_BLOCK, LANES), 1) < HEAD_DIM
    nt_dims = (((1,), (1,)), ((), ()))
    st = [dict(ro=i * sub) for i in range(tm // sub)]

    def norm(c):
        c["x"] = x_ref[c["ro"]:c["ro"] + sub, :]
        c["h"] = _rms_mod(c["x"], g1_ref[...], sc1, sh1).astype(BF16)

    def proj(c):
        h = c["h"]
        c["u"] = jnp.dot(h, win_ref[:, OFF_U:OFF_Q], preferred_element_type=F32)
        q = jnp.dot(h, win_ref[:, OFF_Q:OFF_K], preferred_element_type=F32)
        c["qb"] = (q * (HEAD_DIM ** -0.5)).astype(BF16)
        kv = jnp.dot(h, win_ref[:, OFF_K:OFF_GA], preferred_element_type=F32)
        c["k"] = kv[:, 0:KV_WIDTH]
        c["v"] = kv[:, KV_WIDTH:2 * KV_WIDTH]

    def pool_sums(c):
        ro, u = c["ro"], c["u"]
        ubuf[HIST + ro:HIST + ro + sub, :] = u
        pos = t * tm + ro + lax.broadcasted_iota(jnp.int32, (sub, 1), 0)
        cur = ubuf[ro:ro + HIST + sub, :]
        d, w = [], 1
        for g, wg in enumerate(POOL_WINDOWS):
            while w < wg:
                cur = cur + pltpu.roll(cur, w, axis=0)
                w *= 2
            ug = u[:, g * POOL_GROUP:(g + 1) * POOL_GROUP]
            cnt = jnp.minimum(wg, pos + 1).astype(F32)
            d.append((cur[HIST:, 0:POOL_GROUP] / cnt - ug).astype(BF16))
            if g + 1 < len(POOL_WINDOWS):
                cur = cur[:, POOL_GROUP:]
        c["d"] = d

    def kv_store(c):
        r = ATTN_BLOCK + c["ro"]
        zero = jnp.zeros((sub, LANES), BF16)
        for x, xa, xb in ((c["k"], ka, kb), (c["v"], va, vb)):
            x16 = x.astype(BF16)
            xr16 = pltpu.roll(x, HEAD_DIM, axis=1).astype(BF16)
            xa[0, r:r + sub, :] = jnp.where(low, x16, zero)
            xb[0, r:r + sub, :] = jnp.where(low, zero, xr16)
            xa[1, r:r + sub, :] = jnp.where(low, xr16, zero)
            xb[1, r:r + sub, :] = jnp.where(low, zero, x16)

    def pool_proj(c):
        c["br_a"] = jnp.dot(jnp.concatenate(c["d"], axis=1), wpa_ref[...],
                            preferred_element_type=F32)

    def units(c):
        j0 = c["ro"] // ATTN_BLOCK
        return [(j, hk, pr) for j in range(j0, j0 + sub // ATTN_BLOCK)
                for hk in range(N_KV_HEADS) for pr in range(GQA_GROUP // 2)]

    def scores(c):
        sc = {}
        for (j, hk, pr) in units(c):
            r0 = j * ATTN_BLOCK
            c0 = (hk * (GQA_GROUP // 2) + pr) * LANES
            q2 = c["qb"][r0 - c["ro"]:r0 - c["ro"] + ATTN_BLOCK, c0:c0 + LANES]
            ks = (ka[hk, r0:r0 + 2 * ATTN_BLOCK, :], kb[hk, r0:r0 + 2 * ATTN_BLOCK, :])
            for e in range(2):
                sc[(j, hk, pr, e)] = lax.dot_general(q2, ks[e], nt_dims,
                                                     preferred_element_type=F32)
        c["s"] = sc

    def gate_a(c):
        c["ga"] = jnp.dot(c["h"], win_ref[:, OFF_GA:OFF_GB], preferred_element_type=F32)

    def gate_b(c):
        c["gb"] = jnp.dot(c["h"], win_ref[:, OFF_GB:IN_WIDTH], preferred_element_type=F32)

    def softmax(c):
        probs, dens = {}, {}
        for (j, hk, pr) in units(c):
            for e in range(2):
                head = hk * GQA_GROUP + pr * 2 + e
                if j == 0:
                    bias = jnp.where(t == 0, bias_ref[1, head], bias_ref[0, head])
                else:
                    bias = bias_ref[0, head]
                s = c["s"][(j, hk, pr, e)] + bias
                sink = sinks_ref[layer, head]
                m = jnp.maximum(jnp.max(s, axis=-1, keepdims=True), sink)
                p = jnp.exp(s - m)
                dens[(j, hk, pr, e)] = jnp.sum(p, axis=-1, keepdims=True) + jnp.exp(sink - m)
                probs[(j, hk, pr, e)] = p.astype(BF16)
        c["p"], c["den"] = probs, dens

    def values(c):
        for (j, hk, pr) in units(c):
            r0 = j * ATTN_BLOCK
            c0 = (hk * (GQA_GROUP // 2) + pr) * LANES
            vs = (va[hk, r0:r0 + 2 * ATTN_BLOCK, :], vb[hk, r0:r0 + 2 * ATTN_BLOCK, :])
            o2 = (jnp.dot(c["p"][(j, hk, pr, 0)], vs[0], preferred_element_type=F32)
                  + jnp.dot(c["p"][(j, hk, pr, 1)], vs[1], preferred_element_type=F32))
            den = jnp.where(low_q, c["den"][(j, hk, pr, 0)], c["den"][(j, hk, pr, 1)])
            obuf[r0:r0 + ATTN_BLOCK, c0:c0 + LANES] = o2 / den

    def tail(c):
        ro = c["ro"]
        br_b = _bdot(obuf[ro:ro + sub, :], wb_ref[...])
        merged = jax.nn.sigmoid(c["ga"]) * c["br_a"] + jax.nn.sigmoid(c["gb"]) * br_b
        xo_ref[ro:ro + sub, :] = c["x"] + gt1 * _bdot(merged, wout_ref[...])

    stages = [
        (norm,),
        (proj, gate_a),
        (pool_sums, kv_store),
        (pool_proj, scores, gate_b),
        (softmax,),
        (values,),
        (tail,),
    ]
    order = sorted((2 * s + 3 * i, i, s) for i in range(len(st)) for s in range(len(stages)))
    for _, i, s in order:
        for fn in stages[s]:
            fn(st[i])

    @pl.when(t == nt - 1)
    def _state():
        nk_ref[...] = st[-1]["k"][sub - WINDOW:, :].T
        nv_ref[...] = st[-1]["v"][sub - WINDOW:, :].T
        npool_ref[...] = ubuf[HIST + tm - POOL_PAD:HIST + tm, :]

    ubuf[0:HIST, :] = ubuf[tm:tm + HIST, :]
    for buf in (ka, kb, va, vb):
        buf[:, 0:ATTN_BLOCK, :] = buf[:, tm:tm + ATTN_BLOCK, :]


def _pmix_call(x, mod_p, g1, win, wpa, wb, wout, sinks, bias, l, cvt_jobs):
    b, t, _ = x.shape
    tm = MIX_ROWS
    nt = t // tm
    cvt_in, cvt_out, cvt_shapes = _convert_specs(cvt_jobs, b * nt, lambda i, j, s: i * nt + j)
    grid_spec = pltpu.PrefetchScalarGridSpec(
        num_scalar_prefetch=1,
        grid=(b, nt),
        in_specs=[
            pl.BlockSpec((None, tm, D_MODEL), lambda i, j, s: (i, j, 0)),
            pl.BlockSpec((None, b, MOD_SLAB), lambda i, j, s: (l, 0, 0),
                         pipeline_mode=pl.Buffered(1)),
            _layer_spec((1, D_MODEL), l),
            _weight_spec(win, (D_MODEL, IN_WIDTH), l),
            _weight_spec(wpa, (POOL_WIDTH, D_MODEL), l),
            _weight_spec(wb, (ATTN_WIDTH, D_MODEL), l),
            _weight_spec(wout, (D_MODEL, D_MODEL), l),
            _const_spec((2, N_HEADS, ATTN_BLOCK, 2 * ATTN_BLOCK)),
        ] + cvt_in,
        out_specs=[
            pl.BlockSpec((None, tm, D_MODEL), lambda i, j, s: (i, j, 0)),
            pl.BlockSpec((None, KV_WIDTH, WINDOW), lambda i, j, s: (i, 0, 0)),
            pl.BlockSpec((None, KV_WIDTH, WINDOW), lambda i, j, s: (i, 0, 0)),
            pl.BlockSpec((None, POOL_PAD, POOL_WIDTH), lambda i, j, s: (i, 0, 0)),
        ] + cvt_out,
        scratch_shapes=[
            pltpu.VMEM((HIST + tm, POOL_WIDTH), F32),
            pltpu.VMEM((N_KV_HEADS, ATTN_BLOCK + tm, LANES), BF16),
            pltpu.VMEM((N_KV_HEADS, ATTN_BLOCK + tm, LANES), BF16),
            pltpu.VMEM((N_KV_HEADS, ATTN_BLOCK + tm, LANES), BF16),
            pltpu.VMEM((N_KV_HEADS, ATTN_BLOCK + tm, LANES), BF16),
            pltpu.VMEM((tm, ATTN_WIDTH), F32),
        ],
    )
    return pl.pallas_call(
        functools.partial(_pmix_kernel, layer=l, n_cvt=len(cvt_jobs)),
        grid_spec=grid_spec,
        out_shape=[
            jax.ShapeDtypeStruct((b, t, D_MODEL), F32),
            jax.ShapeDtypeStruct((b, KV_WIDTH, WINDOW), F32),
            jax.ShapeDtypeStruct((b, KV_WIDTH, WINDOW), F32),
            jax.ShapeDtypeStruct((b, POOL_PAD, POOL_WIDTH), F32),
        ] + cvt_shapes,
        compiler_params=pltpu.CompilerParams(
            dimension_semantics=("arbitrary", "arbitrary"),
            vmem_limit_bytes=VMEM_LIMIT),
        name="prompt_mixer",
    )(sinks, x, mod_p, g1, win, wpa, wb, wout, bias, *[w for w, _ in cvt_jobs])


def _ffn_kernel(xp_ref, xs_ref, modp_ref, mods_ref, g2_ref, wup_ref, wdn_ref, fg_ref,
                *rest, final, n_prompt, tiles_per_row, reps, n_cvt):
    cvt_in, (op_ref, os_ref), cvt_out = rest[:n_cvt], rest[n_cvt:n_cvt + 2], rest[n_cvt + 2:]
    _convert_slabs(cvt_in, cvt_out)
    i = pl.program_id(0)

    def ffn(x_ref, o_ref, mod):
        sh2, sc2, gt2 = (mod[:, k * D_MODEL:(k + 1) * D_MODEL] for k in range(3))
        subs = range(0, x_ref.shape[0], FFN_SUB_ROWS)
        rows = lambda a, r: a if a.shape[0] == 1 else a[r:r + FFN_SUB_ROWS]
        norm = lambda r: _rms_mod(x_ref[r:r + FFN_SUB_ROWS, :], g2_ref[...],
                                  rows(sc2, r), rows(sh2, r)).astype(BF16)
        h2 = norm(0)
        for r in subs:
            ff = jnp.dot(h2, wup_ref[...], preferred_element_type=F32)
            if r + FFN_SUB_ROWS in subs:
                h2 = norm(r + FFN_SUB_ROWS)
            ff = jnp.square(jnp.maximum(ff, 0.0))
            y = x_ref[r:r + FFN_SUB_ROWS, :] + rows(gt2, r) * _bdot(ff, wdn_ref[...])
            if final:
                ms = jnp.mean(y * y, axis=-1, keepdims=True)
                y = y * lax.rsqrt(ms + RMS_EPS) * fg_ref[...]
            o_ref[r:r + FFN_SUB_ROWS, :] = y

    @pl.when(i < n_prompt)
    def _prompt():
        ffn(xp_ref, op_ref, modp_ref[pl.ds(i // tiles_per_row, 1), :])

    @pl.when(i == n_prompt)
    def _sample():
        ffn(xs_ref, os_ref, jnp.concatenate([mods_ref[...]] * reps, axis=0))


def _ffn_call(xp, xs, mod_p, mod_s, g2, wup, wdn, fg, l, final, cvt_jobs):
    b, t, _ = xp.shape
    ms = xs.shape[0]
    nb = mod_s.shape[1]
    tm = FFN_ROWS
    tpr = t // tm
    n_prompt = b * tpr
    last = n_prompt - 1

    def p_idx(i):
        ii = jnp.minimum(i, last)
        return (ii // tpr, ii % tpr, 0)

    cvt_in, cvt_out, cvt_shapes = _convert_specs(cvt_jobs, n_prompt, lambda i: jnp.minimum(i, last))
    return pl.pallas_call(
        functools.partial(_ffn_kernel, final=final, n_prompt=n_prompt, tiles_per_row=tpr,
                          reps=ms // nb, n_cvt=len(cvt_jobs)),
        grid=(n_prompt + 1,),
        in_specs=[
            pl.BlockSpec((None, tm, D_MODEL), p_idx),
            _const_spec((ms, D_MODEL)),
            pl.BlockSpec((None, b, MOD_SLAB), lambda i: (l, 0, 1), pipeline_mode=pl.Buffered(1)),
            pl.BlockSpec((None, nb, MOD_SLAB), lambda i: (l, 0, 1), pipeline_mode=pl.Buffered(1)),
            _layer_spec((1, D_MODEL), l),
            _weight_spec(wup, (D_MODEL, D_FF), l),
            _weight_spec(wdn, (D_FF, D_MODEL), l),
            _const_spec((1, D_MODEL)),
        ] + cvt_in,
        out_specs=[
            pl.BlockSpec((None, tm, D_MODEL), p_idx),
            _whole_out_spec((ms, D_MODEL)),
        ] + cvt_out,
        out_shape=[jax.ShapeDtypeStruct((b, t, D_MODEL), F32),
                   jax.ShapeDtypeStruct((ms, D_MODEL), F32)] + cvt_shapes,
        compiler_params=pltpu.CompilerParams(
            dimension_semantics=("arbitrary",),
            vmem_limit_bytes=VMEM_LIMIT),
        name="ffn_final" if final else "ffn",
    )(xp, xs, mod_p, mod_s, g2, wup, wdn, fg, *[w for w, _ in cvt_jobs])


def _sproj_kernel(x_ref, mod_ref, g1_ref, win_ref, wpa_ref, pre_ref,
                  q_ref, k_ref, v_ref, ma_ref, sgb_ref, npool_ref, *, nb, tn):
    x = x_ref[...]
    mod = jnp.concatenate([mod_ref[:, 0:2 * D_MODEL]] * tn, axis=0)
    sh1 = mod[:, 0:D_MODEL]
    sc1 = mod[:, D_MODEL:2 * D_MODEL]
    h = _rms_mod(x, g1_ref[...], sc1, sh1).astype(BF16)

    u = jnp.dot(h, win_ref[:, OFF_U:OFF_Q], preferred_element_type=F32)
    def up(r):
        if r < POOL_PAD:
            return pre_ref[r]
        r -= POOL_PAD
        return u[r * nb:(r + 1) * nb, :]

    d_rows = []
    for tt in range(tn):
        parts = []
        for g, w in enumerate(POOL_WINDOWS):
            lo = g * POOL_GROUP
            acc = up(POOL_PAD + tt)[:, lo:lo + POOL_GROUP]
            for s in range(1, w):
                acc = acc + up(POOL_PAD + tt - s)[:, lo:lo + POOL_GROUP]
            cnt = float(min(w, PAST_LEN + tt + 1))
            parts.append(acc / cnt - up(POOL_PAD + tt)[:, lo:lo + POOL_GROUP])
        d_rows.append(parts)
    d = jnp.concatenate([jnp.concatenate(parts, axis=1) for parts in d_rows], axis=0)
    br_a = _bdot(d, wpa_ref[...])
    for r in range(POOL_PAD):
        npool_ref[r] = up(r + tn)

    q = jnp.dot(h, win_ref[:, OFF_Q:OFF_K], preferred_element_type=F32)
    q_ref[...] = q * (HEAD_DIM ** -0.5)
    kv = jnp.dot(h, win_ref[:, OFF_K:OFF_GA], preferred_element_type=F32)
    k_ref[...] = kv[:, 0:KV_WIDTH]
    v_ref[...] = kv[:, KV_WIDTH:2 * KV_WIDTH]
    ga = jnp.dot(h, win_ref[:, OFF_GA:OFF_GB], preferred_element_type=F32)
    ma_ref[...] = jax.nn.sigmoid(ga) * br_a
    gb = jnp.dot(h, win_ref[:, OFF_GB:IN_WIDTH], preferred_element_type=F32)
    sgb_ref[...] = jax.nn.sigmoid(gb)


def _sproj_call(x, mod_s, g1, win, wpa, prefix_t, l, nb, tn):
    m = x.shape[0]
    shapes = [
        jax.ShapeDtypeStruct((m, ATTN_WIDTH), F32),
        jax.ShapeDtypeStruct((m, KV_WIDTH), F32),
        jax.ShapeDtypeStruct((m, KV_WIDTH), F32),
        jax.ShapeDtypeStruct((m, D_MODEL), F32),
        jax.ShapeDtypeStruct((m, D_MODEL), F32),
        jax.ShapeDtypeStruct((POOL_PAD, nb, POOL_WIDTH), F32),
    ]
    return pl.pallas_call(
        functools.partial(_sproj_kernel, nb=nb, tn=tn),
        grid=(1,),
        in_specs=[
            _const_spec(x.shape),
            pl.BlockSpec((None, nb, MOD_SLAB), lambda i: (l, 0, 0), pipeline_mode=pl.Buffered(1)),
            _layer_spec((1, D_MODEL), l),
            _weight_spec(win, (D_MODEL, IN_WIDTH), l),
            _weight_spec(wpa, (POOL_WIDTH, D_MODEL), l),
            _layer_spec((POOL_PAD, nb, POOL_WIDTH), l),
        ],
        out_specs=[_whole_out_spec(s.shape) for s in shapes],
        out_shape=shapes,
        compiler_params=pltpu.CompilerParams(
            dimension_semantics=("arbitrary",),
            vmem_limit_bytes=VMEM_LIMIT),
        name="sample_proj",
    )(x, mod_s, g1, win, wpa, prefix_t)


def _sample_bias(tn):
    t = np.arange(tn)[:, None]
    lane = np.arange(2 * WINDOW)[None, :]
    cached = lane < WINDOW
    new_t = lane - (2 * WINDOW - tn)
    dist = np.where(cached, t + WINDOW - lane, t - new_t)
    valid = np.where(cached, (dist >= 0) & (dist < WINDOW), (new_t >= 0) & (dist >= 0))
    sl = _alibi_slopes()[:, None, None]
    return np.where(valid[None], -sl * dist[None], NEG_INF).astype(np.float32)


def _sattn_kernel(sinks_ref, q_ref, kc_ref, vc_ref, kn_ref, vn_ref, bias_ref, *rest,
                   layer, nb, tn, bb):
    if layer:
        pk_ref, pv_ref = rest[:2]
        rest = rest[2:]
    o_ref, nk_ref, nv_ref, wk, wv = rest
    i = pl.program_id(0)
    rows = GQA_GROUP * tn
    keep = WINDOW - tn

    @pl.when(i == 0)
    def _new_rows():
        for c0 in range(0, nb * tn, LANES):
            wk[:, c0:c0 + LANES] = kn_ref[c0:c0 + LANES, :].T
            wv[:, c0:c0 + LANES] = vn_ref[c0:c0 + LANES, :].T

    if layer:
        nk_ref[0:layer] = pk_ref[...]
        nv_ref[0:layer] = pv_ref[...]

    lane0 = i * (bb * tn)
    tile0 = pl.multiple_of((lane0 // LANES) * LANES, LANES)
    off0 = lane0 % LANES
    wkt = wk[:, pl.ds(tile0, LANES)]
    wvt = wv[:, pl.ds(tile0, LANES)]
    lane = lax.broadcasted_iota(jnp.int32, (HEAD_DIM, WINDOW), 1)
    tail = lane >= keep
    grow = lax.broadcasted_iota(jnp.int32, (rows, 1), 0) // tn
    units = [(bl, hk) for bl in range(bb) for hk in range(N_KV_HEADS)]
    biases, sinks = [], []
    for hk in range(N_KV_HEADS):
        biases.append(jnp.concatenate(
            [bias_ref[hk * GQA_GROUP + g] for g in range(GQA_GROUP)], axis=0))
        sink = jnp.zeros((rows, 1), F32)
        for g in range(GQA_GROUP):
            sink = jnp.where(grow == g, sinks_ref[layer, hk * GQA_GROUP + g], sink)
        sinks.append(sink)

    scores, values = {}, {}
    for bl in range(bb):
        shift = (keep - off0 - bl * tn) % LANES
        nkb = pltpu.roll(wkt, shift, axis=1)
        nvb = pltpu.roll(wvt, shift, axis=1)
        for hk in range(N_KV_HEADS):
            kt = kc_ref[bl, hk]
            vt = vc_ref[bl, hk]
            nkt = jnp.where(tail, nkb[hk * HEAD_DIM:(hk + 1) * HEAD_DIM, :], 0.0)
            nvt = jnp.where(tail, nvb[hk * HEAD_DIM:(hk + 1) * HEAD_DIM, :], 0.0)
            nk_ref[layer, bl, hk] = jnp.where(tail, nkt, pltpu.roll(kt, keep, axis=1))
            nv_ref[layer, bl, hk] = jnp.where(tail, nvt, pltpu.roll(vt, keep, axis=1))
            keys = jnp.concatenate([kt, nkt], axis=1).astype(BF16)
            values[(bl, hk)] = jnp.concatenate([vt, nvt], axis=1).astype(BF16)
            scores[(bl, hk)] = jnp.dot(q_ref[bl, hk], keys, preferred_element_type=F32)
    probs, dens = {}, {}
    for (bl, hk) in units:
        s = scores[(bl, hk)] + biases[hk]
        m = jnp.maximum(jnp.max(s, axis=-1, keepdims=True), sinks[hk])
        p = jnp.exp(s - m)
        dens[(bl, hk)] = jnp.sum(p, axis=-1, keepdims=True) + jnp.exp(sinks[hk] - m)
        probs[(bl, hk)] = p.astype(BF16)
    for (bl, hk) in units:
        o = lax.dot_general(probs[(bl, hk)], values[(bl, hk)], (((1,), (1,)), ((), ())),
                            preferred_element_type=F32)
        o_ref[bl, hk] = o / dens[(bl, hk)]


def _sattn_call(q4, kn_bt, vn_bt, ck, cv, sinks, rolled, l, nb, tn):
    bb = SATTN_BATCH
    rows = GQA_GROUP * tn
    unit = (bb, N_KV_HEADS, HEAD_DIM, WINDOW)
    cache_shape = jax.ShapeDtypeStruct((l + 1, nb) + unit[1:], F32)
    in_specs = [
        pl.BlockSpec((bb, N_KV_HEADS, rows, HEAD_DIM), lambda i, s: (i, 0, 0, 0)),
        pl.BlockSpec((None,) + unit, lambda i, s: (l, i, 0, 0, 0)),
        pl.BlockSpec((None,) + unit, lambda i, s: (l, i, 0, 0, 0)),
        _const_spec((nb * tn, KV_WIDTH)),
        _const_spec((nb * tn, KV_WIDTH)),
        _const_spec((N_HEADS, tn, 2 * WINDOW)),
    ]
    args = [sinks, q4, ck, cv, kn_bt, vn_bt, jnp.asarray(_sample_bias(tn))]
    if l:
        in_specs += [pl.BlockSpec((l,) + unit, lambda i, s: (0, i, 0, 0, 0))] * 2
        args += list(rolled)
    grid_spec = pltpu.PrefetchScalarGridSpec(
        num_scalar_prefetch=1,
        grid=(nb // bb,),
        in_specs=in_specs,
        out_specs=[
            pl.BlockSpec((bb, N_KV_HEADS, rows, HEAD_DIM), lambda i, s: (i, 0, 0, 0)),
            pl.BlockSpec((l + 1,) + unit, lambda i, s: (0, i, 0, 0, 0)),
            pl.BlockSpec((l + 1,) + unit, lambda i, s: (0, i, 0, 0, 0)),
        ],
        scratch_shapes=[
            pltpu.VMEM((KV_WIDTH, nb * tn), F32),
            pltpu.VMEM((KV_WIDTH, nb * tn), F32),
        ],
    )
    return pl.pallas_call(
        functools.partial(_sattn_kernel, layer=l, nb=nb, tn=tn, bb=bb),
        grid_spec=grid_spec,
        out_shape=[jax.ShapeDtypeStruct((nb, N_KV_HEADS, rows, HEAD_DIM), F32),
                   cache_shape, cache_shape],
        compiler_params=pltpu.CompilerParams(
            dimension_semantics=("arbitrary",),
            vmem_limit_bytes=VMEM_LIMIT),
        name="sample_attn",
    )(*args)


def _spost_kernel(x_ref, o_ref, ma_ref, sgb_ref, mod_ref, wb_ref, wout_ref, xo_ref, *, tn):
    gt1 = jnp.concatenate([mod_ref[...]] * tn, axis=0)
    br_b = _bdot(o_ref[...], wb_ref[...])
    merged = ma_ref[...] + sgb_ref[...] * br_b
    xo_ref[...] = x_ref[...] + gt1 * _bdot(merged, wout_ref[...])


def _spost_call(x, o, ma, sgb, mod_s, wb, wout, l, nb, tn):
    return pl.pallas_call(
        functools.partial(_spost_kernel, tn=tn),
        grid=(1,),
        in_specs=[
            _const_spec(x.shape), _const_spec(o.shape), _const_spec(ma.shape),
            _const_spec(sgb.shape),
            pl.BlockSpec((None, nb, D_MODEL), lambda i: (l, 0, 2), pipeline_mode=pl.Buffered(1)),
            _weight_spec(wb, (ATTN_WIDTH, D_MODEL), l),
            _weight_spec(wout, (D_MODEL, D_MODEL), l),
        ],
        out_specs=_whole_out_spec(x.shape),
        out_shape=jax.ShapeDtypeStruct(x.shape, F32),
        compiler_params=pltpu.CompilerParams(
            dimension_semantics=("arbitrary",),
            vmem_limit_bytes=VMEM_LIMIT),
        name="sample_post",
    )(x, o, ma, sgb, mod_s, wb, wout)


def kernel(x_prompt, x_sample, cache_k, cache_v, state_pool, c_prompt, c_sample,
           w_ada, b_ada, norm1_g, w_in, w_pool, pool_scale, attn_sinks, w_a, w_b,
           w_out, norm2_g, w_up, w_down, final_g):
    bp, tp, _ = x_prompt.shape
    nb, tn, _ = x_sample.shape
    assert tp % MIX_ROWS == 0 and tp % FFN_ROWS == 0 and tn <= 8 and nb == LANES
    assert bp % 8 == 0

    mod_p, mod_s, *mix_w = _mod_call(c_prompt, c_sample, w_ada, b_ada,
                                     [(w, 0) for w in (w_in, w_b, w_out)])
    wpa = _poolw_call(w_pool, pool_scale.reshape(DEPTH, 1, POOL_WIDTH), w_a)
    bias = jnp.asarray(_prompt_bias())
    fg = final_g.reshape(1, D_MODEL)
    g1 = norm1_g.reshape(DEPTH, 1, D_MODEL)
    g2 = norm2_g.reshape(DEPTH, 1, D_MODEL)

    xs = x_sample.transpose(1, 0, 2).reshape(tn * nb, D_MODEL)
    ck = cache_k.transpose(0, 1, 3, 4, 2)
    cv = cache_v.transpose(0, 1, 3, 4, 2)
    prefix_t = state_pool.transpose(0, 2, 1, 3)

    xp = x_prompt
    kp, vp, pp, ps, rolled = [], [], [], [], None
    for l in range(DEPTH):
        last = l == DEPTH - 1
        win, wb, wout = mix_w
        xp, nk, nv, npool, wup, wdn = _pmix_call(xp, mod_p, g1, win, wpa, wb, wout,
                                                 attn_sinks, bias, l, [(w_up, l), (w_down, l)])
        kp.append(nk); vp.append(nv); pp.append(npool)

        q, kn, vn, ma, sgb, npool_s = _sproj_call(xs, mod_s, g1, win, wpa, prefix_t, l, nb, tn)
        q4 = q.reshape(tn, nb, N_KV_HEADS, GQA_GROUP, HEAD_DIM).transpose(1, 2, 3, 0, 4)
        q4 = q4.reshape(nb, N_KV_HEADS, GQA_GROUP * tn, HEAD_DIM).astype(BF16)
        by_batch = lambda a: a.reshape(tn, nb, KV_WIDTH).transpose(1, 0, 2).reshape(nb * tn, KV_WIDTH)
        o4, *rolled = _sattn_call(q4, by_batch(kn), by_batch(vn), ck, cv, attn_sinks, rolled,
                                   l, nb, tn)
        o = o4.reshape(nb, N_KV_HEADS, GQA_GROUP, tn, HEAD_DIM).transpose(3, 0, 1, 2, 4)
        o = o.reshape(tn * nb, ATTN_WIDTH)
        xs = _spost_call(xs, o, ma, sgb, mod_s, wb, wout, l, nb, tn)
        ps.append(npool_s)

        nxt = [] if last else [(w, l + 1) for w in (w_in, w_b, w_out)]
        xp, xs, *mix_w = _ffn_call(xp, xs, mod_p, mod_s, g2, wup, wdn, fg, l, last, nxt)

    to_bt = lambda a: a.reshape(tn, nb, a.shape[-1]).transpose(1, 0, 2)

    kv_shape_p = (DEPTH, bp, N_KV_HEADS, HEAD_DIM, WINDOW)
    return (xp,
            to_bt(xs),
            jnp.stack(kp).reshape(kv_shape_p).transpose(0, 1, 4, 2, 3),
            jnp.stack(vp).reshape(kv_shape_p).transpose(0, 1, 4, 2, 3),
            jnp.stack(pp),
            rolled[0].transpose(0, 1, 4, 2, 3),
            rolled[1].transpose(0, 1, 4, 2, 3),
            jnp.stack(ps).transpose(0, 2, 1, 3))
```

```python
import functools

import numpy as np
import jax
import jax.numpy as jnp
from jax import lax
from jax.experimental import pallas as pl
from jax.experimental.pallas import tpu as pltpu

D_MODEL = 1024
DEPTH = 2
PAST_LEN = 16384
POOL_WIDTH = D_MODEL // 2
POOL_WINDOWS = (2, 4, 8, 16)
POOL_GROUP = POOL_WIDTH // len(POOL_WINDOWS)
POOL_PAD = max(POOL_WINDOWS) - 1
N_HEADS = 8
N_KV_HEADS = 2
HEAD_DIM = 64
GQA_GROUP = N_HEADS // N_KV_HEADS
ATTN_WIDTH = N_HEADS * HEAD_DIM
KV_WIDTH = N_KV_HEADS * HEAD_DIM
WINDOW = 128
ATTN_BLOCK = 128
D_FF = 4 * D_MODEL
RMS_EPS = 1e-6
NEG_INF = -1e30

OFF_U = 0
OFF_Q = OFF_U + POOL_WIDTH
OFF_K = OFF_Q + ATTN_WIDTH
OFF_V = OFF_K + KV_WIDTH
OFF_GA = OFF_V + KV_WIDTH
OFF_GB = OFF_GA + D_MODEL
IN_WIDTH = OFF_GB + D_MODEL
MOD_SLAB = 3 * D_MODEL

LANES = 128
HIST = 16
assert all(w & (w - 1) == 0 for w in POOL_WINDOWS) and list(POOL_WINDOWS) == sorted(POOL_WINDOWS)
assert HIST >= POOL_PAD
VMEM_LIMIT = 56 * 1024 * 1024
SUB_ROWS = 512
MIX_ROWS = 2 * SUB_ROWS
FFN_SUB_ROWS = 512
FFN_ROWS = 2 * FFN_SUB_ROWS
SATTN_BATCH = 16

F32 = jnp.float32
BF16 = jnp.bfloat16


def _bdot(a, b):
    return jnp.dot(a.astype(BF16), b.astype(BF16), preferred_element_type=F32)


def _rms_mod(x, g, sc, sh):
    ms = jnp.mean(x * x, axis=-1, keepdims=True)
    return (x * lax.rsqrt(ms + RMS_EPS) * g) * (1.0 + sc) + sh


def _alibi_slopes():
    return 2.0 ** (-8.0 * (np.arange(N_HEADS) + 1) / N_HEADS)


def _const_spec(shape):
    nd = len(shape)
    return pl.BlockSpec(shape, lambda *_: (0,) * nd, pipeline_mode=pl.Buffered(1))


def _layer_spec(shape, l):
    nd = len(shape)
    return pl.BlockSpec((None,) + tuple(shape), lambda *_: (l,) + (0,) * nd,
                        pipeline_mode=pl.Buffered(1))


def _whole_out_spec(shape):
    nd = len(shape)
    return pl.BlockSpec(shape, lambda *_: (0,) * nd)


def _weight_spec(w, shape, l):
    return _const_spec(shape) if w.shape == tuple(shape) else _layer_spec(shape, l)


def _convert_specs(jobs, n_steps, step_of):
    in_specs, out_specs, out_shapes = [], [], []
    for w, l in jobs:
        _, r, c = w.shape
        rows = r // n_steps
        assert rows * n_steps == r and rows % 16 == 0, (w.shape, n_steps)
        in_specs.append(pl.BlockSpec((None, rows, c), lambda *a, l=l: (l, step_of(*a), 0)))
        out_specs.append(pl.BlockSpec((rows, c), lambda *a: (step_of(*a), 0)))
        out_shapes.append(jax.ShapeDtypeStruct((r, c), BF16))
    return in_specs, out_specs, out_shapes


def _convert_slabs(src_refs, dst_refs):
    for src, dst in zip(src_refs, dst_refs):
        dst[...] = src[...].astype(BF16)


def _mod_kernel(cp_ref, cs_ref, w_ref, b_ref, *rest, n_cvt):
    cvt_in, (op_ref, os_ref), cvt_out = rest[:n_cvt], rest[n_cvt:n_cvt + 2], rest[n_cvt + 2:]
    _convert_slabs(cvt_in, cvt_out)
    w = w_ref[...].astype(BF16)
    b = b_ref[...]
    for c_ref, o_ref in ((cp_ref, op_ref), (cs_ref, os_ref)):
        c = c_ref[...]
        s = (c * jax.nn.sigmoid(c)).astype(BF16)
        o_ref[...] = jnp.dot(s, w, preferred_element_type=F32) + b


def _mod_call(c_p, c_s, w_ada, b_ada, cvt_jobs):
    tn = 1536
    n = 6 * D_MODEL
    nj = n // tn
    mp, ms = c_p.shape[0], c_s.shape[0]
    cvt_in, cvt_out, cvt_shapes = _convert_specs(cvt_jobs, DEPTH * nj, lambda l, j: l * nj + j)
    return pl.pallas_call(
        functools.partial(_mod_kernel, n_cvt=len(cvt_jobs)),
        grid=(DEPTH, nj),
        in_specs=[
            _const_spec((mp, D_MODEL)),
            _const_spec((ms, D_MODEL)),
            pl.BlockSpec((None, D_MODEL, tn), lambda l, j: (l, 0, j)),
            pl.BlockSpec((None, 1, tn), lambda l, j: (l, 0, j)),
        ] + cvt_in,
        out_specs=[
            pl.BlockSpec((None, mp, tn), lambda l, j: (l, 0, j)),
            pl.BlockSpec((None, ms, tn), lambda l, j: (l, 0, j)),
        ] + cvt_out,
        out_shape=[jax.ShapeDtypeStruct((DEPTH, mp, n), F32),
                   jax.ShapeDtypeStruct((DEPTH, ms, n), F32)] + cvt_shapes,
        compiler_params=pltpu.CompilerParams(
            dimension_semantics=("arbitrary", "arbitrary"),
            vmem_limit_bytes=VMEM_LIMIT),
        name="adaln_mod",
    )(c_p, c_s, w_ada, b_ada.reshape(DEPTH, 1, n), *[w for w, _ in cvt_jobs])


def _poolw_kernel(wp_ref, ps_ref, wa_ref, o_ref):
    for g in range(len(POOL_WINDOWS)):
        rows = slice(g * POOL_GROUP, (g + 1) * POOL_GROUP)
        o_ref[rows, :] = jnp.dot(wp_ref[g] * ps_ref[:, rows], wa_ref[rows, :],
                                 preferred_element_type=F32,
                                 precision=lax.Precision.HIGHEST).astype(BF16)


def _poolw_call(w_pool, pool_scale, w_a):
    return pl.pallas_call(
        _poolw_kernel,
        grid=(DEPTH,),
        in_specs=[
            pl.BlockSpec((None,) + w_pool.shape[1:], lambda l: (l, 0, 0, 0)),
            pl.BlockSpec((None, 1, POOL_WIDTH), lambda l: (l, 0, 0)),
            pl.BlockSpec((None, POOL_WIDTH, D_MODEL), lambda l: (l, 0, 0)),
        ],
        out_specs=pl.BlockSpec((None, POOL_WIDTH, D_MODEL), lambda l: (l, 0, 0)),
        out_shape=jax.ShapeDtypeStruct((DEPTH, POOL_WIDTH, D_MODEL), BF16),
        compiler_params=pltpu.CompilerParams(
            dimension_semantics=("arbitrary",),
            vmem_limit_bytes=VMEM_LIMIT),
        name="pool_weights",
    )(w_pool, pool_scale, w_a)


def _prompt_bias():
    i = np.arange(ATTN_BLOCK)[:, None]
    j = np.arange(2 * ATTN_BLOCK)[None, :]
    dist = i + ATTN_BLOCK - j
    valid = (dist >= 0) & (dist < WINDOW)
    valid_first = valid & (j >= ATTN_BLOCK)
    sl = _alibi_slopes()[:, None, None]
    b = np.where(valid[None], -sl * dist[None], NEG_INF)
    b0 = np.where(valid_first[None], -sl * dist[None], NEG_INF)
    return np.stack([b, b0]).astype(np.float32)


def _pmix_kernel(sinks_ref, x_ref, mod_ref, g1_ref, win_ref, wpa_ref,
                 wb_ref, wout_ref, bias_ref, *rest, layer, n_cvt):
    cvt_in, rest = rest[:n_cvt], rest[n_cvt:]
    (xo_ref, nk_ref, nv_ref, npool_ref), rest = rest[:4], rest[4:]
    cvt_out, (ubuf, ka, kb, va, vb, obuf) = rest[:n_cvt], rest[n_cvt:]
    _convert_slabs(cvt_in, cvt_out)
    tm, sub = MIX_ROWS, SUB_ROWS
    bi = pl.program_id(0)
    t = pl.program_id(1)
    nt = pl.num_programs(1)

    @pl.when(t == 0)
    def _init():
        ubuf[0:HIST, :] = jnp.zeros((HIST, POOL_WIDTH), F32)
        zero = jnp.zeros((N_KV_HEADS, ATTN_BLOCK, LANES), BF16)
        ka[:, 0:ATTN_BLOCK, :] = zero
        kb[:, 0:ATTN_BLOCK, :] = zero
        va[:, 0:ATTN_BLOCK, :] = zero
        vb[:, 0:ATTN_BLOCK, :] = zero

    mod = mod_ref[pl.ds(bi, 1), :]
    sh1 = mod[:, 0:D_MODEL]
    sc1 = mod[:, D_MODEL:2 * D_MODEL]
    gt1 = mod[:, 2 * D_MODEL:3 * D_MODEL]
    lane = lax.broadcasted_iota(jnp.int32, (sub, LANES), 1)
    low = lane < HEAD_DIM
    low_q = lax.broadcasted_iota(jnp.int32, (ATTN_BLOCK, LANES), 1) < HEAD_DIM
    nt_dims = (((1,), (1,)), ((), ()))
    st = [dict(ro=i * sub) for i in range(tm // sub)]

    def norm(c):
        c["x"] = x_ref[c["ro"]:c["ro"] + sub, :]
        c["h"] = _rms_mod(c["x"], g1_ref[...], sc1, sh1).astype(BF16)

    def proj(c):
        h = c["h"]
        c["u"] = jnp.dot(h, win_ref[:, OFF_U:OFF_Q], preferred_element_type=F32)
        q = jnp.dot(h, win_ref[:, OFF_Q:OFF_K], preferred_element_type=F32)
        c["qb"] = (q * (HEAD_DIM ** -0.5)).astype(BF16)
        kv = jnp.dot(h, win_ref[:, OFF_K:OFF_GA], preferred_element_type=F32)
        c["k"] = kv[:, 0:KV_WIDTH]
        c["v"] = kv[:, KV_WIDTH:2 * KV_WIDTH]

    def pool_sums(c):
        ro, u = c["ro"], c["u"]
        ubuf[HIST + ro:HIST + ro + sub, :] = u
        pos = t * tm + ro + lax.broadcasted_iota(jnp.int32, (sub, 1), 0)
        cur = ubuf[ro:ro + HIST + sub, :]
        d, w = [], 1
        for g, wg in enumerate(POOL_WINDOWS):
            while w < wg:
                cur = cur + pltpu.roll(cur, w, axis=0)
                w *= 2
            ug = u[:, g * POOL_GROUP:(g + 1) * POOL_GROUP]
            cnt = jnp.minimum(wg, pos + 1).astype(F32)
            d.append((cur[HIST:, 0:POOL_GROUP] / cnt - ug).astype(BF16))
            if g + 1 < len(POOL_WINDOWS):
                cur = cur[:, POOL_GROUP:]
        c["d"] = d

    def kv_store(c):
        r = ATTN_BLOCK + c["ro"]
        zero = jnp.zeros((sub, LANES), BF16)
        for x, xa, xb in ((c["k"], ka, kb), (c["v"], va, vb)):
            x16 = x.astype(BF16)
            xr16 = pltpu.roll(x, HEAD_DIM, axis=1).astype(BF16)
            xa[0, r:r + sub, :] = jnp.where(low, x16, zero)
            xb[0, r:r + sub, :] = jnp.where(low, zero, xr16)
            xa[1, r:r + sub, :] = jnp.where(low, xr16, zero)
            xb[1, r:r + sub, :] = jnp.where(low, zero, x16)

    def pool_proj(c):
        c["br_a"] = jnp.dot(jnp.concatenate(c["d"], axis=1), wpa_ref[...],
                            preferred_element_type=F32)

    def units(c):
        j0 = c["ro"] // ATTN_BLOCK
        return [(j, hk, pr) for j in range(j0, j0 + sub // ATTN_BLOCK)
                for hk in range(N_KV_HEADS) for pr in range(GQA_GROUP // 2)]

    def scores(c):
        sc = {}
        for (j, hk, pr) in units(c):
            r0 = j * ATTN_BLOCK
            c0 = (hk * (GQA_GROUP // 2) + pr) * LANES
            q2 = c["qb"][r0 - c["ro"]:r0 - c["ro"] + ATTN_BLOCK, c0:c0 + LANES]
            ks = (ka[hk, r0:r0 + 2 * ATTN_BLOCK, :], kb[hk, r0:r0 + 2 * ATTN_BLOCK, :])
            for e in range(2):
                sc[(j, hk, pr, e)] = lax.dot_general(q2, ks[e], nt_dims,
                                                     preferred_element_type=F32)
        c["s"] = sc

    def gate_a(c):
        c["ga"] = jnp.dot(c["h"], win_ref[:, OFF_GA:OFF_GB], preferred_element_type=F32)

    def gate_b(c):
        c["gb"] = jnp.dot(c["h"], win_ref[:, OFF_GB:IN_WIDTH], preferred_element_type=F32)

    def softmax(c):
        probs, dens = {}, {}
        for (j, hk, pr) in units(c):
            for e in range(2):
                head = hk * GQA_GROUP + pr * 2 + e
                if j == 0:
                    bias = jnp.where(t == 0, bias_ref[1, head], bias_ref[0, head])
                else:
                    bias = bias_ref[0, head]
                s = c["s"][(j, hk, pr, e)] + bias
                sink = sinks_ref[layer, head]
                m = jnp.maximum(jnp.max(s, axis=-1, keepdims=True), sink)
                p = jnp.exp(s - m)
                dens[(j, hk, pr, e)] = jnp.sum(p, axis=-1, keepdims=True) + jnp.exp(sink - m)
                probs[(j, hk, pr, e)] = p.astype(BF16)
        c["p"], c["den"] = probs, dens

    def values(c):
        for (j, hk, pr) in units(c):
            r0 = j * ATTN_BLOCK
            c0 = (hk * (GQA_GROUP // 2) + pr) * LANES
            vs = (va[hk, r0:r0 + 2 * ATTN_BLOCK, :], vb[hk, r0:r0 + 2 * ATTN_BLOCK, :])
            o2 = (jnp.dot(c["p"][(j, hk, pr, 0)], vs[0], preferred_element_type=F32)
                  + jnp.dot(c["p"][(j, hk, pr, 1)], vs[1], preferred_element_type=F32))
            den = jnp.where(low_q, c["den"][(j, hk, pr, 0)], c["den"][(j, hk, pr, 1)])
            obuf[r0:r0 + ATTN_BLOCK, c0:c0 + LANES] = o2 / den

    def tail(c):
        ro = c["ro"]
        br_b = _bdot(obuf[ro:ro + sub, :], wb_ref[...])
        merged = jax.nn.sigmoid(c["ga"]) * c["br_a"] + jax.nn.sigmoid(c["gb"]) * br_b
        xo_ref[ro:ro + sub, :] = c["x"] + gt1 * _bdot(merged, wout_ref[...])

    stages = [
        (norm,),
        (proj, gate_a),
        (pool_sums, kv_store),
        (pool_proj, scores, gate_b),
        (softmax,),
        (values,),
        (tail,),
    ]
    order = sorted((2 * s + 3 * i, i, s) for i in range(len(st)) for s in range(len(stages)))
    for _, i, s in order:
        for fn in stages[s]:
            fn(st[i])

    @pl.when(t == nt - 1)
    def _state():
        nk_ref[...] = st[-1]["k"][sub - WINDOW:, :].T
        nv_ref[...] = st[-1]["v"][sub - WINDOW:, :].T
        npool_ref[...] = ubuf[HIST + tm - POOL_PAD:HIST + tm, :]

    ubuf[0:HIST, :] = ubuf[tm:tm + HIST, :]
    for buf in (ka, kb, va, vb):
        buf[:, 0:ATTN_BLOCK, :] = buf[:, tm:tm + ATTN_BLOCK, :]


def _pmix_call(x, mod_p, g1, win, wpa, wb, wout, sinks, bias, l, cvt_jobs):
    b, t, _ = x.shape
    tm = MIX_ROWS
    nt = t // tm
    cvt_in, cvt_out, cvt_shapes = _convert_specs(cvt_jobs, b * nt, lambda i, j, s: i * nt + j)
    grid_spec = pltpu.PrefetchScalarGridSpec(
        num_scalar_prefetch=1,
        grid=(b, nt),
        in_specs=[
            pl.BlockSpec((None, tm, D_MODEL), lambda i, j, s: (i, j, 0)),
            pl.BlockSpec((None, b, MOD_SLAB), lambda i, j, s: (l, 0, 0),
                         pipeline_mode=pl.Buffered(1)),
            _layer_spec((1, D_MODEL), l),
            _weight_spec(win, (D_MODEL, IN_WIDTH), l),
            _weight_spec(wpa, (POOL_WIDTH, D_MODEL), l),
            _weight_spec(wb, (ATTN_WIDTH, D_MODEL), l),
            _weight_spec(wout, (D_MODEL, D_MODEL), l),
            _const_spec((2, N_HEADS, ATTN_BLOCK, 2 * ATTN_BLOCK)),
        ] + cvt_in,
        out_specs=[
            pl.BlockSpec((None, tm, D_MODEL), lambda i, j, s: (i, j, 0)),
            pl.BlockSpec((None, KV_WIDTH, WINDOW), lambda i, j, s: (i, 0, 0)),
            pl.BlockSpec((None, KV_WIDTH, WINDOW), lambda i, j, s: (i, 0, 0)),
            pl.BlockSpec((None, POOL_PAD, POOL_WIDTH), lambda i, j, s: (i, 0, 0)),
        ] + cvt_out,
        scratch_shapes=[
            pltpu.VMEM((HIST + tm, POOL_WIDTH), F32),
            pltpu.VMEM((N_KV_HEADS, ATTN_BLOCK + tm, LANES), BF16),
            pltpu.VMEM((N_KV_HEADS, ATTN_BLOCK + tm, LANES), BF16),
            pltpu.VMEM((N_KV_HEADS, ATTN_BLOCK + tm, LANES), BF16),
            pltpu.VMEM((N_KV_HEADS, ATTN_BLOCK + tm, LANES), BF16),
            pltpu.VMEM((tm, ATTN_WIDTH), F32),
        ],
    )
    return pl.pallas_call(
        functools.partial(_pmix_kernel, layer=l, n_cvt=len(cvt_jobs)),
        grid_spec=grid_spec,
        out_shape=[
            jax.ShapeDtypeStruct((b, t, D_MODEL), F32),
            jax.ShapeDtypeStruct((b, KV_WIDTH, WINDOW), F32),
            jax.ShapeDtypeStruct((b, KV_WIDTH, WINDOW), F32),
            jax.ShapeDtypeStruct((b, POOL_PAD, POOL_WIDTH), F32),
        ] + cvt_shapes,
        compiler_params=pltpu.CompilerParams(
            dimension_semantics=("arbitrary", "arbitrary"),
            vmem_limit_bytes=VMEM_LIMIT),
        name="prompt_mixer",
    )(sinks, x, mod_p, g1, win, wpa, wb, wout, bias, *[w for w, _ in cvt_jobs])


def _ffn_kernel(xp_ref, xs_ref, modp_ref, mods_ref, g2_ref, wup_ref, wdn_ref, fg_ref,
                *rest, final, n_prompt, tiles_per_row, reps, n_cvt):
    cvt_in, (op_ref, os_ref), cvt_out = rest[:n_cvt], rest[n_cvt:n_cvt + 2], rest[n_cvt + 2:]
    _convert_slabs(cvt_in, cvt_out)
    i = pl.program_id(0)

    def ffn(x_ref, o_ref, mod):
        sh2, sc2, gt2 = (mod[:, k * D_MODEL:(k + 1) * D_MODEL] for k in range(3))
        subs = range(0, x_ref.shape[0], FFN_SUB_ROWS)
        rows = lambda a, r: a if a.shape[0] == 1 else a[r:r + FFN_SUB_ROWS]
        norm = lambda r: _rms_mod(x_ref[r:r + FFN_SUB_ROWS, :], g2_ref[...],
                                  rows(sc2, r), rows(sh2, r)).astype(BF16)
        h2 = norm(0)
        for r in subs:
            ff = jnp.dot(h2, wup_ref[...], preferred_element_type=F32)
            if r + FFN_SUB_ROWS in subs:
                h2 = norm(r + FFN_SUB_ROWS)
            ff = jnp.square(jnp.maximum(ff, 0.0))
            y = x_ref[r:r + FFN_SUB_ROWS, :] + rows(gt2, r) * _bdot(ff, wdn_ref[...])
            if final:
                ms = jnp.mean(y * y, axis=-1, keepdims=True)
                y = y * lax.rsqrt(ms + RMS_EPS) * fg_ref[...]
            if len(o_ref.shape) == 3:
                o_ref[...] = jnp.swapaxes(y.reshape(reps, y.shape[0] // reps, D_MODEL), 0, 1)
            else:
                o_ref[r:r + FFN_SUB_ROWS, :] = y

    @pl.when(i < n_prompt)
    def _prompt():
        ffn(xp_ref, op_ref, modp_ref[pl.ds(i // tiles_per_row, 1), :])

    @pl.when(i == n_prompt)
    def _sample():
        ffn(xs_ref, os_ref, jnp.concatenate([mods_ref[...]] * reps, axis=0))


def _ffn_call(xp, xs, mod_p, mod_s, g2, wup, wdn, fg, l, final, cvt_jobs):
    b, t, _ = xp.shape
    ms = xs.shape[0]
    nb = mod_s.shape[1]
    tm = FFN_ROWS
    tpr = t // tm
    n_prompt = b * tpr
    last = n_prompt - 1

    def p_idx(i):
        ii = jnp.minimum(i, last)
        return (ii // tpr, ii % tpr, 0)

    assert ms <= FFN_SUB_ROWS
    s_shape = (nb, ms // nb, D_MODEL) if final else (ms, D_MODEL)
    cvt_in, cvt_out, cvt_shapes = _convert_specs(cvt_jobs, n_prompt, lambda i: jnp.minimum(i, last))
    return pl.pallas_call(
        functools.partial(_ffn_kernel, final=final, n_prompt=n_prompt, tiles_per_row=tpr,
                          reps=ms // nb, n_cvt=len(cvt_jobs)),
        grid=(n_prompt + 1,),
        in_specs=[
            pl.BlockSpec((None, tm, D_MODEL), p_idx),
            _const_spec((ms, D_MODEL)),
            pl.BlockSpec((None, b, MOD_SLAB), lambda i: (l, 0, 1), pipeline_mode=pl.Buffered(1)),
            pl.BlockSpec((None, nb, MOD_SLAB), lambda i: (l, 0, 1), pipeline_mode=pl.Buffered(1)),
            _layer_spec((1, D_MODEL), l),
            _weight_spec(wup, (D_MODEL, D_FF), l),
            _weight_spec(wdn, (D_FF, D_MODEL), l),
            _const_spec((1, D_MODEL)),
        ] + cvt_in,
        out_specs=[
            pl.BlockSpec((None, tm, D_MODEL), p_idx),
            _whole_out_spec(s_shape),
        ] + cvt_out,
        out_shape=[jax.ShapeDtypeStruct((b, t, D_MODEL), F32),
                   jax.ShapeDtypeStruct(s_shape, F32)] + cvt_shapes,
        compiler_params=pltpu.CompilerParams(
            dimension_semantics=("arbitrary",),
            vmem_limit_bytes=VMEM_LIMIT),
        name="ffn_final" if final else "ffn",
    )(xp, xs, mod_p, mod_s, g2, wup, wdn, fg, *[w for w, _ in cvt_jobs])


def _sproj_kernel(x_ref, mod_ref, g1_ref, win_ref, wpa_ref, pre_ref,
                  q_ref, k_ref, v_ref, ma_ref, sgb_ref, npool_ref, *, nb, tn):
    x = x_ref[...]
    if x.ndim == 3:
        x = jnp.swapaxes(x, 0, 1).reshape(tn * nb, D_MODEL)
    mod = jnp.concatenate([mod_ref[:, 0:2 * D_MODEL]] * tn, axis=0)
    sh1 = mod[:, 0:D_MODEL]
    sc1 = mod[:, D_MODEL:2 * D_MODEL]
    h = _rms_mod(x, g1_ref[...], sc1, sh1).astype(BF16)

    u = jnp.dot(h, win_ref[:, OFF_U:OFF_Q], preferred_element_type=F32)
    pre = jnp.swapaxes(pre_ref[...], 0, 1)

    def up(r):
        if r < POOL_PAD:
            return pre[r]
        r -= POOL_PAD
        return u[r * nb:(r + 1) * nb, :]

    d_rows = []
    for tt in range(tn):
        parts = []
        for g, w in enumerate(POOL_WINDOWS):
            lo = g * POOL_GROUP
            acc = up(POOL_PAD + tt)[:, lo:lo + POOL_GROUP]
            for s in range(1, w):
                acc = acc + up(POOL_PAD + tt - s)[:, lo:lo + POOL_GROUP]
            cnt = float(min(w, PAST_LEN + tt + 1))
            parts.append(acc / cnt - up(POOL_PAD + tt)[:, lo:lo + POOL_GROUP])
        d_rows.append(parts)
    d = jnp.concatenate([jnp.concatenate(parts, axis=1) for parts in d_rows], axis=0)
    br_a = _bdot(d, wpa_ref[...])
    new_pool = jnp.stack([up(r + tn) for r in range(POOL_PAD)], axis=0)
    npool_ref[...] = jnp.swapaxes(new_pool, 0, 1)

    q = jnp.dot(h, win_ref[:, OFF_Q:OFF_K], preferred_element_type=F32)
    q_ref[...] = q * (HEAD_DIM ** -0.5)
    kv = jnp.dot(h, win_ref[:, OFF_K:OFF_GA], preferred_element_type=F32)
    k_ref[...] = kv[:, 0:KV_WIDTH]
    v_ref[...] = kv[:, KV_WIDTH:2 * KV_WIDTH]
    ga = jnp.dot(h, win_ref[:, OFF_GA:OFF_GB], preferred_element_type=F32)
    ma_ref[...] = jax.nn.sigmoid(ga) * br_a
    gb = jnp.dot(h, win_ref[:, OFF_GB:IN_WIDTH], preferred_element_type=F32)
    sgb_ref[...] = jax.nn.sigmoid(gb)


def _sproj_call(x, mod_s, g1, win, wpa, prefix_t, l, nb, tn):
    m = tn * nb
    shapes = [
        jax.ShapeDtypeStruct((m, ATTN_WIDTH), F32),
        jax.ShapeDtypeStruct((m, KV_WIDTH), F32),
        jax.ShapeDtypeStruct((m, KV_WIDTH), F32),
        jax.ShapeDtypeStruct((m, D_MODEL), F32),
        jax.ShapeDtypeStruct((m, D_MODEL), F32),
        jax.ShapeDtypeStruct((nb, POOL_PAD, POOL_WIDTH), F32),
    ]
    return pl.pallas_call(
        functools.partial(_sproj_kernel, nb=nb, tn=tn),
        grid=(1,),
        in_specs=[
            _const_spec(x.shape),
            pl.BlockSpec((None, nb, MOD_SLAB), lambda i: (l, 0, 0), pipeline_mode=pl.Buffered(1)),
            _layer_spec((1, D_MODEL), l),
            _weight_spec(win, (D_MODEL, IN_WIDTH), l),
            _weight_spec(wpa, (POOL_WIDTH, D_MODEL), l),
            _layer_spec((nb, POOL_PAD, POOL_WIDTH), l),
        ],
        out_specs=[_whole_out_spec(s.shape) for s in shapes],
        out_shape=shapes,
        compiler_params=pltpu.CompilerParams(
            dimension_semantics=("arbitrary",),
            vmem_limit_bytes=VMEM_LIMIT),
        name="sample_proj",
    )(x, mod_s, g1, win, wpa, prefix_t)


def _sample_bias(tn):
    t = np.arange(tn)[:, None]
    lane = np.arange(2 * WINDOW)[None, :]
    cached = lane < WINDOW
    new_t = lane - (2 * WINDOW - tn)
    dist = np.where(cached, t + WINDOW - lane, t - new_t)
    valid = np.where(cached, (dist >= 0) & (dist < WINDOW), (new_t >= 0) & (dist >= 0))
    sl = _alibi_slopes()[:, None, None]
    return np.where(valid[None], -sl * dist[None], NEG_INF).astype(np.float32)


def _sattn_kernel(sinks_ref, q_ref, kc_ref, vc_ref, kn_ref, vn_ref, bias_ref, *rest,
                   layer, nb, tn, bb):
    if layer:
        pk_ref, pv_ref = rest[:2]
        rest = rest[2:]
    o_ref, nk_ref, nv_ref, wk, wv = rest
    i = pl.program_id(0)
    rows = GQA_GROUP * tn
    keep = WINDOW - tn

    @pl.when(i == 0)
    def _new_rows():
        for c0 in range(0, nb * tn, LANES):
            wk[:, c0:c0 + LANES] = kn_ref[c0:c0 + LANES, :].T
            wv[:, c0:c0 + LANES] = vn_ref[c0:c0 + LANES, :].T

    if layer:
        nk_ref[0:layer] = pk_ref[...]
        nv_ref[0:layer] = pv_ref[...]

    lane0 = i * (bb * tn)
    tile0 = pl.multiple_of((lane0 // LANES) * LANES, LANES)
    off0 = lane0 % LANES
    wkt = wk[:, pl.ds(tile0, LANES)]
    wvt = wv[:, pl.ds(tile0, LANES)]
    lane = lax.broadcasted_iota(jnp.int32, (HEAD_DIM, WINDOW), 1)
    tail = lane >= keep
    grow = lax.broadcasted_iota(jnp.int32, (rows, 1), 0) // tn
    units = [(bl, hk) for bl in range(bb) for hk in range(N_KV_HEADS)]
    biases, sinks = [], []
    for hk in range(N_KV_HEADS):
        biases.append(jnp.concatenate(
            [bias_ref[hk * GQA_GROUP + g] for g in range(GQA_GROUP)], axis=0))
        sink = jnp.zeros((rows, 1), F32)
        for g in range(GQA_GROUP):
            sink = jnp.where(grow == g, sinks_ref[layer, hk * GQA_GROUP + g], sink)
        sinks.append(sink)

    scores, values = {}, {}
    for bl in range(bb):
        shift = (keep - off0 - bl * tn) % LANES
        nkb = pltpu.roll(wkt, shift, axis=1)
        nvb = pltpu.roll(wvt, shift, axis=1)
        for hk in range(N_KV_HEADS):
            kt = kc_ref[bl, hk]
            vt = vc_ref[bl, hk]
            nkt = jnp.where(tail, nkb[hk * HEAD_DIM:(hk + 1) * HEAD_DIM, :], 0.0)
            nvt = jnp.where(tail, nvb[hk * HEAD_DIM:(hk + 1) * HEAD_DIM, :], 0.0)
            nk_ref[layer, bl, hk] = jnp.where(tail, nkt, pltpu.roll(kt, keep, axis=1))
            nv_ref[layer, bl, hk] = jnp.where(tail, nvt, pltpu.roll(vt, keep, axis=1))
            keys = jnp.concatenate([kt, nkt], axis=1).astype(BF16)
            values[(bl, hk)] = jnp.concatenate([vt, nvt], axis=1).astype(BF16)
            scores[(bl, hk)] = jnp.dot(q_ref[bl, hk], keys, preferred_element_type=F32)
    probs, dens = {}, {}
    for (bl, hk) in units:
        s = scores[(bl, hk)] + biases[hk]
        m = jnp.maximum(jnp.max(s, axis=-1, keepdims=True), sinks[hk])
        p = jnp.exp(s - m)
        dens[(bl, hk)] = jnp.sum(p, axis=-1, keepdims=True) + jnp.exp(sinks[hk] - m)
        probs[(bl, hk)] = p.astype(BF16)
    for (bl, hk) in units:
        o = lax.dot_general(probs[(bl, hk)], values[(bl, hk)], (((1,), (1,)), ((), ())),
                            preferred_element_type=F32)
        o_ref[bl, hk] = o / dens[(bl, hk)]


def _sattn_call(q4, kn_bt, vn_bt, ck, cv, sinks, rolled, l, nb, tn):
    bb = SATTN_BATCH
    rows = GQA_GROUP * tn
    unit = (bb, N_KV_HEADS, HEAD_DIM, WINDOW)
    cache_shape = jax.ShapeDtypeStruct((l + 1, nb) + unit[1:], F32)
    in_specs = [
        pl.BlockSpec((bb, N_KV_HEADS, rows, HEAD_DIM), lambda i, s: (i, 0, 0, 0)),
        pl.BlockSpec((None,) + unit, lambda i, s: (l, i, 0, 0, 0)),
        pl.BlockSpec((None,) + unit, lambda i, s: (l, i, 0, 0, 0)),
        _const_spec((nb * tn, KV_WIDTH)),
        _const_spec((nb * tn, KV_WIDTH)),
        _const_spec((N_HEADS, tn, 2 * WINDOW)),
    ]
    args = [sinks, q4, ck, cv, kn_bt, vn_bt, jnp.asarray(_sample_bias(tn))]
    if l:
        in_specs += [pl.BlockSpec((l,) + unit, lambda i, s: (0, i, 0, 0, 0))] * 2
        args += list(rolled)
    grid_spec = pltpu.PrefetchScalarGridSpec(
        num_scalar_prefetch=1,
        grid=(nb // bb,),
        in_specs=in_specs,
        out_specs=[
            pl.BlockSpec((bb, N_KV_HEADS, rows, HEAD_DIM), lambda i, s: (i, 0, 0, 0)),
            pl.BlockSpec((l + 1,) + unit, lambda i, s: (0, i, 0, 0, 0)),
            pl.BlockSpec((l + 1,) + unit, lambda i, s: (0, i, 0, 0, 0)),
        ],
        scratch_shapes=[
            pltpu.VMEM((KV_WIDTH, nb * tn), F32),
            pltpu.VMEM((KV_WIDTH, nb * tn), F32),
        ],
    )
    return pl.pallas_call(
        functools.partial(_sattn_kernel, layer=l, nb=nb, tn=tn, bb=bb),
        grid_spec=grid_spec,
        out_shape=[jax.ShapeDtypeStruct((nb, N_KV_HEADS, rows, HEAD_DIM), F32),
                   cache_shape, cache_shape],
        compiler_params=pltpu.CompilerParams(
            dimension_semantics=("arbitrary",),
            vmem_limit_bytes=VMEM_LIMIT),
        name="sample_attn",
    )(*args)


def _spost_kernel(x_ref, o_ref, ma_ref, sgb_ref, mod_ref, wb_ref, wout_ref, xo_ref, *, tn):
    x = x_ref[...]
    if x.ndim == 3:
        x = jnp.swapaxes(x, 0, 1).reshape(xo_ref.shape)
    gt1 = jnp.concatenate([mod_ref[...]] * tn, axis=0)
    br_b = _bdot(o_ref[...], wb_ref[...])
    merged = ma_ref[...] + sgb_ref[...] * br_b
    xo_ref[...] = x + gt1 * _bdot(merged, wout_ref[...])


def _spost_call(x, o, ma, sgb, mod_s, wb, wout, l, nb, tn):
    out_shape = (tn * nb, D_MODEL)
    return pl.pallas_call(
        functools.partial(_spost_kernel, tn=tn),
        grid=(1,),
        in_specs=[
            _const_spec(x.shape), _const_spec(o.shape), _const_spec(ma.shape),
            _const_spec(sgb.shape),
            pl.BlockSpec((None, nb, D_MODEL), lambda i: (l, 0, 2), pipeline_mode=pl.Buffered(1)),
            _weight_spec(wb, (ATTN_WIDTH, D_MODEL), l),
            _weight_spec(wout, (D_MODEL, D_MODEL), l),
        ],
        out_specs=_whole_out_spec(out_shape),
        out_shape=jax.ShapeDtypeStruct(out_shape, F32),
        compiler_params=pltpu.CompilerParams(
            dimension_semantics=("arbitrary",),
            vmem_limit_bytes=VMEM_LIMIT),
        name="sample_post",
    )(x, o, ma, sgb, mod_s, wb, wout)


def kernel(x_prompt, x_sample, cache_k, cache_v, state_pool, c_prompt, c_sample,
           w_ada, b_ada, norm1_g, w_in, w_pool, pool_scale, attn_sinks, w_a, w_b,
           w_out, norm2_g, w_up, w_down, final_g):
    bp, tp, _ = x_prompt.shape
    nb, tn, _ = x_sample.shape
    assert tp % MIX_ROWS == 0 and tp % FFN_ROWS == 0 and tn <= 8 and nb == LANES
    assert bp % 8 == 0

    mod_p, mod_s, *mix_w = _mod_call(c_prompt, c_sample, w_ada, b_ada,
                                     [(w, 0) for w in (w_in, w_b, w_out)])
    wpa = _poolw_call(w_pool, pool_scale.reshape(DEPTH, 1, POOL_WIDTH), w_a)
    bias = jnp.asarray(_prompt_bias())
    fg = final_g.reshape(1, D_MODEL)
    g1 = norm1_g.reshape(DEPTH, 1, D_MODEL)
    g2 = norm2_g.reshape(DEPTH, 1, D_MODEL)

    xs = x_sample
    ck = cache_k.transpose(0, 1, 3, 4, 2)
    cv = cache_v.transpose(0, 1, 3, 4, 2)

    xp = x_prompt
    kp, vp, pp, ps, rolled = [], [], [], [], None
    for l in range(DEPTH):
        last = l == DEPTH - 1
        win, wb, wout = mix_w
        xp, nk, nv, npool, wup, wdn = _pmix_call(xp, mod_p, g1, win, wpa, wb, wout,
                                                 attn_sinks, bias, l, [(w_up, l), (w_down, l)])
        kp.append(nk); vp.append(nv); pp.append(npool)

        q, kn, vn, ma, sgb, npool_s = _sproj_call(xs, mod_s, g1, win, wpa, state_pool, l, nb, tn)
        q4 = q.reshape(tn, nb, N_KV_HEADS, GQA_GROUP, HEAD_DIM).transpose(1, 2, 3, 0, 4)
        q4 = q4.reshape(nb, N_KV_HEADS, GQA_GROUP * tn, HEAD_DIM).astype(BF16)
        by_batch = lambda a: a.reshape(tn, nb, KV_WIDTH).transpose(1, 0, 2).reshape(nb * tn, KV_WIDTH)
        o4, *rolled = _sattn_call(q4, by_batch(kn), by_batch(vn), ck, cv, attn_sinks, rolled,
                                   l, nb, tn)
        o = o4.reshape(nb, N_KV_HEADS, GQA_GROUP, tn, HEAD_DIM).transpose(3, 0, 1, 2, 4)
        o = o.reshape(tn * nb, ATTN_WIDTH)
        xs = _spost_call(xs, o, ma, sgb, mod_s, wb, wout, l, nb, tn)
        ps.append(npool_s)

        nxt = [] if last else [(w, l + 1) for w in (w_in, w_b, w_out)]
        xp, xs, *mix_w = _ffn_call(xp, xs, mod_p, mod_s, g2, wup, wdn, fg, l, last, nxt)

    kv_shape_p = (DEPTH, bp, N_KV_HEADS, HEAD_DIM, WINDOW)
    return (xp,
            xs,
            jnp.stack(kp).reshape(kv_shape_p).transpose(0, 1, 4, 2, 3),
            jnp.stack(vp).reshape(kv_shape_p).transpose(0, 1, 4, 2, 3),
            jnp.stack(pp),
            rolled[0].transpose(0, 1, 4, 2, 3),
            rolled[1].transpose(0, 1, 4, 2, 3),
            jnp.stack(ps))
```

```python
import functools

import numpy as np
import jax
import jax.numpy as jnp
from jax import lax
from jax.experimental import pallas as pl
from jax.experimental.pallas import tpu as pltpu

D_MODEL = 1024
DEPTH = 2
PAST_LEN = 16384
POOL_WIDTH = D_MODEL // 2
POOL_WINDOWS = (2, 4, 8, 16)
POOL_GROUP = POOL_WIDTH // len(POOL_WINDOWS)
POOL_PAD = max(POOL_WINDOWS) - 1
N_HEADS = 8
N_KV_HEADS = 2
HEAD_DIM = 64
GQA_GROUP = N_HEADS // N_KV_HEADS
ATTN_WIDTH = N_HEADS * HEAD_DIM
KV_WIDTH = N_KV_HEADS * HEAD_DIM
WINDOW = 128
ATTN_BLOCK = 128
D_FF = 4 * D_MODEL
RMS_EPS = 1e-6
NEG_INF = -1e30

OFF_U = 0
OFF_Q = OFF_U + POOL_WIDTH
OFF_K = OFF_Q + ATTN_WIDTH
OFF_V = OFF_K + KV_WIDTH
OFF_GA = OFF_V + KV_WIDTH
OFF_GB = OFF_GA + D_MODEL
IN_WIDTH = OFF_GB + D_MODEL
MOD_SLAB = 3 * D_MODEL

LANES = 128
HIST = 16
assert all(w & (w - 1) == 0 for w in POOL_WINDOWS) and list(POOL_WINDOWS) == sorted(POOL_WINDOWS)
assert HIST >= POOL_PAD
VMEM_LIMIT = 56 * 1024 * 1024
SUB_ROWS = 512
MIX_ROWS = 2 * SUB_ROWS
FFN_SUB_ROWS = 512
FFN_ROWS = 2 * FFN_SUB_ROWS
SATTN_BATCH = 16

F32 = jnp.float32
BF16 = jnp.bfloat16


def _bdot(a, b):
    return jnp.dot(a.astype(BF16), b.astype(BF16), preferred_element_type=F32)


def _rms_mod(x, g, sc, sh):
    ms = jnp.mean(x * x, axis=-1, keepdims=True)
    return (x * lax.rsqrt(ms + RMS_EPS) * g) * (1.0 + sc) + sh


def _alibi_slopes():
    return 2.0 ** (-8.0 * (np.arange(N_HEADS) + 1) / N_HEADS)


def _const_spec(shape):
    nd = len(shape)
    return pl.BlockSpec(shape, lambda *_: (0,) * nd, pipeline_mode=pl.Buffered(1))


def _layer_spec(shape, l):
    nd = len(shape)
    return pl.BlockSpec((None,) + tuple(shape), lambda *_: (l,) + (0,) * nd,
                        pipeline_mode=pl.Buffered(1))


def _whole_out_spec(shape):
    nd = len(shape)
    return pl.BlockSpec(shape, lambda *_: (0,) * nd)


def _weight_spec(w, shape, l):
    return _const_spec(shape) if w.shape == tuple(shape) else _layer_spec(shape, l)


def _convert_specs(jobs, n_steps, step_of):
    in_specs, out_specs, out_shapes = [], [], []
    for w, l in jobs:
        _, r, c = w.shape
        rows = r // n_steps
        assert rows * n_steps == r and rows % 16 == 0, (w.shape, n_steps)
        in_specs.append(pl.BlockSpec((None, rows, c), lambda *a, l=l: (l, step_of(*a), 0)))
        out_specs.append(pl.BlockSpec((rows, c), lambda *a: (step_of(*a), 0)))
        out_shapes.append(jax.ShapeDtypeStruct((r, c), BF16))
    return in_specs, out_specs, out_shapes


def _convert_slabs(src_refs, dst_refs):
    for src, dst in zip(src_refs, dst_refs):
        dst[...] = src[...].astype(BF16)


def _mod_kernel(cp_ref, cs_ref, w_ref, b_ref, *rest, n_cvt):
    cvt_in, (op_ref, os_ref), cvt_out = rest[:n_cvt], rest[n_cvt:n_cvt + 2], rest[n_cvt + 2:]
    _convert_slabs(cvt_in, cvt_out)
    w = w_ref[...].astype(BF16)
    b = b_ref[...]
    for c_ref, o_ref in ((cp_ref, op_ref), (cs_ref, os_ref)):
        c = c_ref[...]
        s = (c * jax.nn.sigmoid(c)).astype(BF16)
        o_ref[...] = jnp.dot(s, w, preferred_element_type=F32) + b


def _mod_call(c_p, c_s, w_ada, b_ada, cvt_jobs):
    tn = 1536
    n = 6 * D_MODEL
    nj = n // tn
    mp, ms = c_p.shape[0], c_s.shape[0]
    cvt_in, cvt_out, cvt_shapes = _convert_specs(cvt_jobs, DEPTH * nj, lambda l, j: l * nj + j)
    return pl.pallas_call(
        functools.partial(_mod_kernel, n_cvt=len(cvt_jobs)),
        grid=(DEPTH, nj),
        in_specs=[
            _const_spec((mp, D_MODEL)),
            _const_spec((ms, D_MODEL)),
            pl.BlockSpec((None, D_MODEL, tn), lambda l, j: (l, 0, j)),
            pl.BlockSpec((None, 1, tn), lambda l, j: (l, 0, j)),
        ] + cvt_in,
        out_specs=[
            pl.BlockSpec((None, mp, tn), lambda l, j: (l, 0, j)),
            pl.BlockSpec((None, ms, tn), lambda l, j: (l, 0, j)),
        ] + cvt_out,
        out_shape=[jax.ShapeDtypeStruct((DEPTH, mp, n), F32),
                   jax.ShapeDtypeStruct((DEPTH, ms, n), F32)] + cvt_shapes,
        compiler_params=pltpu.CompilerParams(
            dimension_semantics=("arbitrary", "arbitrary"),
            vmem_limit_bytes=VMEM_LIMIT),
        name="adaln_mod",
    )(c_p, c_s, w_ada, b_ada.reshape(DEPTH, 1, n), *[w for w, _ in cvt_jobs])


def _poolw_kernel(wp_ref, ps_ref, wa_ref, o_ref):
    for g in range(len(POOL_WINDOWS)):
        rows = slice(g * POOL_GROUP, (g + 1) * POOL_GROUP)
        o_ref[rows, :] = jnp.dot(wp_ref[g] * ps_ref[:, rows], wa_ref[rows, :],
                                 preferred_element_type=F32,
                                 precision=lax.Precision.HIGHEST).astype(BF16)


def _poolw_call(w_pool, pool_scale, w_a):
    return pl.pallas_call(
        _poolw_kernel,
        grid=(DEPTH,),
        in_specs=[
            pl.BlockSpec((None,) + w_pool.shape[1:], lambda l: (l, 0, 0, 0)),
            pl.BlockSpec((None, 1, POOL_WIDTH), lambda l: (l, 0, 0)),
            pl.BlockSpec((None, POOL_WIDTH, D_MODEL), lambda l: (l, 0, 0)),
        ],
        out_specs=pl.BlockSpec((None, POOL_WIDTH, D_MODEL), lambda l: (l, 0, 0)),
        out_shape=jax.ShapeDtypeStruct((DEPTH, POOL_WIDTH, D_MODEL), BF16),
        compiler_params=pltpu.CompilerParams(
            dimension_semantics=("arbitrary",),
            vmem_limit_bytes=VMEM_LIMIT),
        name="pool_weights",
    )(w_pool, pool_scale, w_a)


def _prompt_bias():
    i = np.arange(ATTN_BLOCK)[:, None]
    j = np.arange(2 * ATTN_BLOCK)[None, :]
    dist = i + ATTN_BLOCK - j
    valid = (dist >= 0) & (dist < WINDOW)
    valid_first = valid & (j >= ATTN_BLOCK)
    sl = _alibi_slopes()[:, None, None]
    b = np.where(valid[None], -sl * dist[None], NEG_INF)
    b0 = np.where(valid_first[None], -sl * dist[None], NEG_INF)
    return np.stack([b, b0]).astype(np.float32)


def _pmix_kernel(sinks_ref, x_ref, mod_ref, g1_ref, win_ref, wpa_ref,
                 wb_ref, wout_ref, bias_ref, *rest, layer, n_cvt):
    cvt_in, rest = rest[:n_cvt], rest[n_cvt:]
    (xo_ref, nk_ref, nv_ref, npool_ref), rest = rest[:4], rest[4:]
    cvt_out, (ubuf, ka, kb, va, vb, obuf) = rest[:n_cvt], rest[n_cvt:]
    _convert_slabs(cvt_in, cvt_out)
    tm, sub = MIX_ROWS, SUB_ROWS
    bi = pl.program_id(0)
    t = pl.program_id(1)
    nt = pl.num_programs(1)

    @pl.when(t == 0)
    def _init():
        ubuf[0:HIST, :] = jnp.zeros((HIST, POOL_WIDTH), F32)
        zero = jnp.zeros((N_KV_HEADS, ATTN_BLOCK, LANES), BF16)
        ka[:, 0:ATTN_BLOCK, :] = zero
        kb[:, 0:ATTN_BLOCK, :] = zero
        va[:, 0:ATTN_BLOCK, :] = zero
        vb[:, 0:ATTN_BLOCK, :] = zero

    mod = mod_ref[pl.ds(bi, 1), :]
    sh1 = mod[:, 0:D_MODEL]
    sc1 = mod[:, D_MODEL:2 * D_MODEL]
    gt1 = mod[:, 2 * D_MODEL:3 * D_MODEL]
    lane = lax.broadcasted_iota(jnp.int32, (sub, LANES), 1)
    low = lane < HEAD_DIM
    low_q = lax.broadcasted_iota(jnp.int32, (ATTN_BLOCK, LANES), 1) < HEAD_DIM
    nt_dims = (((1,), (1,)), ((), ()))
    st = [dict(ro=i * sub) for i in range(tm // sub)]

    def norm(c):
        c["x"] = x_ref[c["ro"]:c["ro"] + sub, :]
        c["h"] = _rms_mod(c["x"], g1_ref[...], sc1, sh1).astype(BF16)

    def proj(c):
        h = c["h"]
        c["u"] = jnp.dot(h, win_ref[:, OFF_U:OFF_Q], preferred_element_type=F32)
        q = jnp.dot(h, win_ref[:, OFF_Q:OFF_K], preferred_element_type=F32)
        c["qb"] = (q * (HEAD_DIM ** -0.5)).astype(BF16)
        kv = jnp.dot(h, win_ref[:, OFF_K:OFF_GA], preferred_element_type=F32)
        c["k"] = kv[:, 0:KV_WIDTH]
        c["v"] = kv[:, KV_WIDTH:2 * KV_WIDTH]

    def pool_sums(c):
        ro, u = c["ro"], c["u"]
        ubuf[HIST + ro:HIST + ro + sub, :] = u
        pos = t * tm + ro + lax.broadcasted_iota(jnp.int32, (sub, 1), 0)
        cur = ubuf[ro:ro + HIST + sub, :]
        d, w = [], 1
        for g, wg in enumerate(POOL_WINDOWS):
            while w < wg:
                cur = cur + pltpu.roll(cur, w, axis=0)
                w *= 2
            ug = u[:, g * POOL_GROUP:(g + 1) * POOL_GROUP]
            cnt = jnp.minimum(wg, pos + 1).astype(F32)
            d.append((cur[HIST:, 0:POOL_GROUP] / cnt - ug).astype(BF16))
            if g + 1 < len(POOL_WINDOWS):
                cur = cur[:, POOL_GROUP:]
        c["d"] = d

    def kv_store(c):
        r = ATTN_BLOCK + c["ro"]
        zero = jnp.zeros((sub, LANES), BF16)
        for x, xa, xb in ((c["k"], ka, kb), (c["v"], va, vb)):
            x16 = x.astype(BF16)
            xr16 = pltpu.roll(x, HEAD_DIM, axis=1).astype(BF16)
            xa[0, r:r + sub, :] = jnp.where(low, x16, zero)
            xb[0, r:r + sub, :] = jnp.where(low, zero, xr16)
            xa[1, r:r + sub, :] = jnp.where(low, xr16, zero)
            xb[1, r:r + sub, :] = jnp.where(low, zero, x16)

    def pool_proj(c):
        c["br_a"] = jnp.dot(jnp.concatenate(c["d"], axis=1), wpa_ref[...],
                            preferred_element_type=F32)

    def units(c):
        j0 = c["ro"] // ATTN_BLOCK
        return [(j, hk, pr) for j in range(j0, j0 + sub // ATTN_BLOCK)
                for hk in range(N_KV_HEADS) for pr in range(GQA_GROUP // 2)]

    def scores(c):
        sc = {}
        for (j, hk, pr) in units(c):
            r0 = j * ATTN_BLOCK
            c0 = (hk * (GQA_GROUP // 2) + pr) * LANES
            q2 = c["qb"][r0 - c["ro"]:r0 - c["ro"] + ATTN_BLOCK, c0:c0 + LANES]
            ks = (ka[hk, r0:r0 + 2 * ATTN_BLOCK, :], kb[hk, r0:r0 + 2 * ATTN_BLOCK, :])
            for e in range(2):
                sc[(j, hk, pr, e)] = lax.dot_general(q2, ks[e], nt_dims,
                                                     preferred_element_type=F32)
        c["s"] = sc

    def gate_a(c):
        c["ga"] = jnp.dot(c["h"], win_ref[:, OFF_GA:OFF_GB], preferred_element_type=F32)

    def gate_b(c):
        c["gb"] = jnp.dot(c["h"], win_ref[:, OFF_GB:IN_WIDTH], preferred_element_type=F32)

    def softmax(c):
        probs, dens = {}, {}
        for (j, hk, pr) in units(c):
            for e in range(2):
                head = hk * GQA_GROUP + pr * 2 + e
                if j == 0:
                    bias = jnp.where(t == 0, bias_ref[1, head], bias_ref[0, head])
                else:
                    bias = bias_ref[0, head]
                s = c["s"][(j, hk, pr, e)] + bias
                sink = sinks_ref[layer, head]
                m = jnp.maximum(jnp.max(s, axis=-1, keepdims=True), sink)
                p = jnp.exp(s - m)
                dens[(j, hk, pr, e)] = jnp.sum(p, axis=-1, keepdims=True) + jnp.exp(sink - m)
                probs[(j, hk, pr, e)] = p.astype(BF16)
        c["p"], c["den"] = probs, dens

    def values(c):
        for (j, hk, pr) in units(c):
            r0 = j * ATTN_BLOCK
            c0 = (hk * (GQA_GROUP // 2) + pr) * LANES
            vs = (va[hk, r0:r0 + 2 * ATTN_BLOCK, :], vb[hk, r0:r0 + 2 * ATTN_BLOCK, :])
            o2 = (jnp.dot(c["p"][(j, hk, pr, 0)], vs[0], preferred_element_type=F32)
                  + jnp.dot(c["p"][(j, hk, pr, 1)], vs[1], preferred_element_type=F32))
            den = jnp.where(low_q, c["den"][(j, hk, pr, 0)], c["den"][(j, hk, pr, 1)])
            obuf[r0:r0 + ATTN_BLOCK, c0:c0 + LANES] = o2 / den

    def tail(c):
        ro = c["ro"]
        br_b = _bdot(obuf[ro:ro + sub, :], wb_ref[...])
        merged = jax.nn.sigmoid(c["ga"]) * c["br_a"] + jax.nn.sigmoid(c["gb"]) * br_b
        xo_ref[ro:ro + sub, :] = c["x"] + gt1 * _bdot(merged, wout_ref[...])

    stages = [
        (norm,),
        (proj, gate_a),
        (pool_sums, kv_store),
        (pool_proj, scores, gate_b),
        (softmax,),
        (values,),
        (tail,),
    ]
    order = sorted((2 * s + 3 * i, i, s) for i in range(len(st)) for s in range(len(stages)))
    for _, i, s in order:
        for fn in stages[s]:
            fn(st[i])

    @pl.when(t == nt - 1)
    def _state():
        nk_ref[...] = st[-1]["k"][sub - WINDOW:, :].T
        nv_ref[...] = st[-1]["v"][sub - WINDOW:, :].T
        npool_ref[...] = ubuf[HIST + tm - POOL_PAD:HIST + tm, :]

    ubuf[0:HIST, :] = ubuf[tm:tm + HIST, :]
    for buf in (ka, kb, va, vb):
        buf[:, 0:ATTN_BLOCK, :] = buf[:, tm:tm + ATTN_BLOCK, :]


def _pmix_call(x, mod_p, g1, win, wpa, wb, wout, sinks, bias, l, cvt_jobs):
    b, t, _ = x.shape
    tm = MIX_ROWS
    nt = t // tm
    cvt_in, cvt_out, cvt_shapes = _convert_specs(cvt_jobs, b * nt, lambda i, j, s: i * nt + j)
    grid_spec = pltpu.PrefetchScalarGridSpec(
        num_scalar_prefetch=1,
        grid=(b, nt),
        in_specs=[
            pl.BlockSpec((None, tm, D_MODEL), lambda i, j, s: (i, j, 0)),
            pl.BlockSpec((None, b, MOD_SLAB), lambda i, j, s: (l, 0, 0),
                         pipeline_mode=pl.Buffered(1)),
            _layer_spec((1, D_MODEL), l),
            _weight_spec(win, (D_MODEL, IN_WIDTH), l),
            _weight_spec(wpa, (POOL_WIDTH, D_MODEL), l),
            _weight_spec(wb, (ATTN_WIDTH, D_MODEL), l),
            _weight_spec(wout, (D_MODEL, D_MODEL), l),
            _const_spec((2, N_HEADS, ATTN_BLOCK, 2 * ATTN_BLOCK)),
        ] + cvt_in,
        out_specs=[
            pl.BlockSpec((None, tm, D_MODEL), lambda i, j, s: (i, j, 0)),
            pl.BlockSpec((None, KV_WIDTH, WINDOW), lambda i, j, s: (i, 0, 0)),
            pl.BlockSpec((None, KV_WIDTH, WINDOW), lambda i, j, s: (i, 0, 0)),
            pl.BlockSpec((None, POOL_PAD, POOL_WIDTH), lambda i, j, s: (i, 0, 0)),
        ] + cvt_out,
        scratch_shapes=[
            pltpu.VMEM((HIST + tm, POOL_WIDTH), F32),
            pltpu.VMEM((N_KV_HEADS, ATTN_BLOCK + tm, LANES), BF16),
            pltpu.VMEM((N_KV_HEADS, ATTN_BLOCK + tm, LANES), BF16),
            pltpu.VMEM((N_KV_HEADS, ATTN_BLOCK + tm, LANES), BF16),
            pltpu.VMEM((N_KV_HEADS, ATTN_BLOCK + tm, LANES), BF16),
            pltpu.VMEM((tm, ATTN_WIDTH), F32),
        ],
    )
    return pl.pallas_call(
        functools.partial(_pmix_kernel, layer=l, n_cvt=len(cvt_jobs)),
        grid_spec=grid_spec,
        out_shape=[
            jax.ShapeDtypeStruct((b, t, D_MODEL), F32),
            jax.ShapeDtypeStruct((b, KV_WIDTH, WINDOW), F32),
            jax.ShapeDtypeStruct((b, KV_WIDTH, WINDOW), F32),
            jax.ShapeDtypeStruct((b, POOL_PAD, POOL_WIDTH), F32),
        ] + cvt_shapes,
        compiler_params=pltpu.CompilerParams(
            dimension_semantics=("arbitrary", "arbitrary"),
            vmem_limit_bytes=VMEM_LIMIT),
        name="prompt_mixer",
    )(sinks, x, mod_p, g1, win, wpa, wb, wout, bias, *[w for w, _ in cvt_jobs])


def _ffn_kernel(xp_ref, xs_ref, modp_ref, mods_ref, g2_ref, wup_ref, wdn_ref, fg_ref,
                *rest, final, n_prompt, tiles_per_row, reps, n_cvt):
    cvt_in, (op_ref, os_ref), cvt_out = rest[:n_cvt], rest[n_cvt:n_cvt + 2], rest[n_cvt + 2:]
    _convert_slabs(cvt_in, cvt_out)
    i = pl.program_id(0)

    def ffn(x_ref, o_ref, mod):
        sh2, sc2, gt2 = (mod[:, k * D_MODEL:(k + 1) * D_MODEL] for k in range(3))
        subs = range(0, x_ref.shape[0], FFN_SUB_ROWS)
        rows = lambda a, r: a if a.shape[0] == 1 else a[r:r + FFN_SUB_ROWS]
        norm = lambda r: _rms_mod(x_ref[r:r + FFN_SUB_ROWS, :], g2_ref[...],
                                  rows(sc2, r), rows(sh2, r)).astype(BF16)
        h2 = norm(0)
        for r in subs:
            ff = jnp.dot(h2, wup_ref[...], preferred_element_type=F32)
            if r + FFN_SUB_ROWS in subs:
                h2 = norm(r + FFN_SUB_ROWS)
            ff = jnp.square(jnp.maximum(ff, 0.0))
            y = x_ref[r:r + FFN_SUB_ROWS, :] + rows(gt2, r) * _bdot(ff, wdn_ref[...])
            if final:
                ms = jnp.mean(y * y, axis=-1, keepdims=True)
                y = y * lax.rsqrt(ms + RMS_EPS) * fg_ref[...]
            if len(o_ref.shape) == 3:
                o_ref[...] = jnp.swapaxes(y.reshape(reps, y.shape[0] // reps, D_MODEL), 0, 1)
            else:
                o_ref[r:r + FFN_SUB_ROWS, :] = y

    @pl.when(i < n_prompt)
    def _prompt():
        ffn(xp_ref, op_ref, modp_ref[pl.ds(i // tiles_per_row, 1), :])

    @pl.when(i == n_prompt)
    def _sample():
        ffn(xs_ref, os_ref, jnp.concatenate([mods_ref[...]] * reps, axis=0))


def _ffn_call(xp, xs, mod_p, mod_s, g2, wup, wdn, fg, l, final, cvt_jobs):
    b, t, _ = xp.shape
    ms = xs.shape[0]
    nb = mod_s.shape[1]
    tm = FFN_ROWS
    tpr = t // tm
    n_prompt = b * tpr
    last = n_prompt - 1

    def p_idx(i):
        ii = jnp.minimum(i, last)
        return (ii // tpr, ii % tpr, 0)

    assert ms <= FFN_SUB_ROWS
    s_shape = (nb, ms // nb, D_MODEL) if final else (ms, D_MODEL)
    cvt_in, cvt_out, cvt_shapes = _convert_specs(cvt_jobs, n_prompt, lambda i: jnp.minimum(i, last))
    return pl.pallas_call(
        functools.partial(_ffn_kernel, final=final, n_prompt=n_prompt, tiles_per_row=tpr,
                          reps=ms // nb, n_cvt=len(cvt_jobs)),
        grid=(n_prompt + 1,),
        in_specs=[
            pl.BlockSpec((None, tm, D_MODEL), p_idx),
            _const_spec((ms, D_MODEL)),
            pl.BlockSpec((None, b, MOD_SLAB), lambda i: (l, 0, 1), pipeline_mode=pl.Buffered(1)),
            pl.BlockSpec((None, nb, MOD_SLAB), lambda i: (l, 0, 1), pipeline_mode=pl.Buffered(1)),
            _layer_spec((1, D_MODEL), l),
            _weight_spec(wup, (D_MODEL, D_FF), l),
            _weight_spec(wdn, (D_FF, D_MODEL), l),
            _const_spec((1, D_MODEL)),
        ] + cvt_in,
        out_specs=[
            pl.BlockSpec((None, tm, D_MODEL), p_idx),
            _whole_out_spec(s_shape),
        ] + cvt_out,
        out_shape=[jax.ShapeDtypeStruct((b, t, D_MODEL), F32),
                   jax.ShapeDtypeStruct(s_shape, F32)] + cvt_shapes,
        compiler_params=pltpu.CompilerParams(
            dimension_semantics=("arbitrary",),
            vmem_limit_bytes=VMEM_LIMIT),
        name="ffn_final" if final else "ffn",
    )(xp, xs, mod_p, mod_s, g2, wup, wdn, fg, *[w for w, _ in cvt_jobs])


def _sproj_kernel(x_ref, mod_ref, g1_ref, win_ref, wpa_ref, pre_ref,
                  q_ref, k_ref, v_ref, ma_ref, sgb_ref, npool_ref, *, nb, tn):
    x = x_ref[...]
    if x.ndim == 3:
        x = jnp.swapaxes(x, 0, 1).reshape(tn * nb, D_MODEL)
    mod = jnp.concatenate([mod_ref[:, 0:2 * D_MODEL]] * tn, axis=0)
    sh1 = mod[:, 0:D_MODEL]
    sc1 = mod[:, D_MODEL:2 * D_MODEL]
    h = _rms_mod(x, g1_ref[...], sc1, sh1).astype(BF16)

    u = jnp.dot(h, win_ref[:, OFF_U:OFF_Q], preferred_element_type=F32)
    def up(r):
        if r < POOL_PAD:
            return pre_ref[r]
        r -= POOL_PAD
        return u[r * nb:(r + 1) * nb, :]

    d_rows = []
    for tt in range(tn):
        parts = []
        for g, w in enumerate(POOL_WINDOWS):
            lo = g * POOL_GROUP
            acc = up(POOL_PAD + tt)[:, lo:lo + POOL_GROUP]
            for s in range(1, w):
                acc = acc + up(POOL_PAD + tt - s)[:, lo:lo + POOL_GROUP]
            cnt = float(min(w, PAST_LEN + tt + 1))
            parts.append(acc / cnt - up(POOL_PAD + tt)[:, lo:lo + POOL_GROUP])
        d_rows.append(parts)
    d = jnp.concatenate([jnp.concatenate(parts, axis=1) for parts in d_rows], axis=0)
    br_a = _bdot(d, wpa_ref[...])
    for r in range(POOL_PAD):
        npool_ref[r] = up(r + tn)

    q = jnp.dot(h, win_ref[:, OFF_Q:OFF_K], preferred_element_type=F32)
    q_ref[...] = q * (HEAD_DIM ** -0.5)
    kv = jnp.dot(h, win_ref[:, OFF_K:OFF_GA], preferred_element_type=F32)
    k_ref[...] = kv[:, 0:KV_WIDTH]
    v_ref[...] = kv[:, KV_WIDTH:2 * KV_WIDTH]
    ga = jnp.dot(h, win_ref[:, OFF_GA:OFF_GB], preferred_element_type=F32)
    ma_ref[...] = jax.nn.sigmoid(ga) * br_a
    gb = jnp.dot(h, win_ref[:, OFF_GB:IN_WIDTH], preferred_element_type=F32)
    sgb_ref[...] = jax.nn.sigmoid(gb)


def _sproj_call(x, mod_s, g1, win, wpa, prefix_t, l, nb, tn):
    m = tn * nb
    shapes = [
        jax.ShapeDtypeStruct((m, ATTN_WIDTH), F32),
        jax.ShapeDtypeStruct((m, KV_WIDTH), F32),
        jax.ShapeDtypeStruct((m, KV_WIDTH), F32),
        jax.ShapeDtypeStruct((m, D_MODEL), F32),
        jax.ShapeDtypeStruct((m, D_MODEL), F32),
        jax.ShapeDtypeStruct((POOL_PAD, nb, POOL_WIDTH), F32),
    ]
    return pl.pallas_call(
        functools.partial(_sproj_kernel, nb=nb, tn=tn),
        grid=(1,),
        in_specs=[
            _const_spec(x.shape),
            pl.BlockSpec((None, nb, MOD_SLAB), lambda i: (l, 0, 0), pipeline_mode=pl.Buffered(1)),
            _layer_spec((1, D_MODEL), l),
            _weight_spec(win, (D_MODEL, IN_WIDTH), l),
            _weight_spec(wpa, (POOL_WIDTH, D_MODEL), l),
            _layer_spec((POOL_PAD, nb, POOL_WIDTH), l),
        ],
        out_specs=[_whole_out_spec(s.shape) for s in shapes],
        out_shape=shapes,
        compiler_params=pltpu.CompilerParams(
            dimension_semantics=("arbitrary",),
            vmem_limit_bytes=VMEM_LIMIT),
        name="sample_proj",
    )(x, mod_s, g1, win, wpa, prefix_t)


def _sample_bias(tn):
    t = np.arange(tn)[:, None]
    lane = np.arange(2 * WINDOW)[None, :]
    cached = lane < WINDOW
    new_t = lane - (2 * WINDOW - tn)
    dist = np.where(cached, t + WINDOW - lane, t - new_t)
    valid = np.where(cached, (dist >= 0) & (dist < WINDOW), (new_t >= 0) & (dist >= 0))
    sl = _alibi_slopes()[:, None, None]
    return np.where(valid[None], -sl * dist[None], NEG_INF).astype(np.float32)


def _sattn_kernel(sinks_ref, q_ref, kc_ref, vc_ref, kn_ref, vn_ref, bias_ref, *rest,
                   layer, nb, tn, bb):
    if layer:
        pk_ref, pv_ref = rest[:2]
        rest = rest[2:]
    o_ref, nk_ref, nv_ref, wk, wv = rest
    i = pl.program_id(0)
    rows = GQA_GROUP * tn
    keep = WINDOW - tn

    @pl.when(i == 0)
    def _new_rows():
        for c0 in range(0, nb * tn, LANES):
            wk[:, c0:c0 + LANES] = kn_ref[c0:c0 + LANES, :].T
            wv[:, c0:c0 + LANES] = vn_ref[c0:c0 + LANES, :].T

    if layer:
        nk_ref[0:layer] = pk_ref[...]
        nv_ref[0:layer] = pv_ref[...]

    lane0 = i * (bb * tn)
    tile0 = pl.multiple_of((lane0 // LANES) * LANES, LANES)
    off0 = lane0 % LANES
    wkt = wk[:, pl.ds(tile0, LANES)]
    wvt = wv[:, pl.ds(tile0, LANES)]
    lane = lax.broadcasted_iota(jnp.int32, (HEAD_DIM, WINDOW), 1)
    tail = lane >= keep
    grow = lax.broadcasted_iota(jnp.int32, (rows, 1), 0) // tn
    units = [(bl, hk) for bl in range(bb) for hk in range(N_KV_HEADS)]
    biases, sinks = [], []
    for hk in range(N_KV_HEADS):
        biases.append(jnp.concatenate(
            [bias_ref[hk * GQA_GROUP + g] for g in range(GQA_GROUP)], axis=0))
        sink = jnp.zeros((rows, 1), F32)
        for g in range(GQA_GROUP):
            sink = jnp.where(grow == g, sinks_ref[layer, hk * GQA_GROUP + g], sink)
        sinks.append(sink)

    scores, values = {}, {}
    for bl in range(bb):
        shift = (keep - off0 - bl * tn) % LANES
        nkb = pltpu.roll(wkt, shift, axis=1)
        nvb = pltpu.roll(wvt, shift, axis=1)
        for hk in range(N_KV_HEADS):
            kt = kc_ref[bl, hk]
            vt = vc_ref[bl, hk]
            nkt = jnp.where(tail, nkb[hk * HEAD_DIM:(hk + 1) * HEAD_DIM, :], 0.0)
            nvt = jnp.where(tail, nvb[hk * HEAD_DIM:(hk + 1) * HEAD_DIM, :], 0.0)
            nk_ref[layer, bl, hk] = jnp.where(tail, nkt, pltpu.roll(kt, keep, axis=1))
            nv_ref[layer, bl, hk] = jnp.where(tail, nvt, pltpu.roll(vt, keep, axis=1))
            keys = jnp.concatenate([kt, nkt], axis=1).astype(BF16)
            values[(bl, hk)] = jnp.concatenate([vt, nvt], axis=1).astype(BF16)
            scores[(bl, hk)] = jnp.dot(q_ref[bl, hk], keys, preferred_element_type=F32)
    probs, dens = {}, {}
    for (bl, hk) in units:
        s = scores[(bl, hk)] + biases[hk]
        m = jnp.maximum(jnp.max(s, axis=-1, keepdims=True), sinks[hk])
        p = jnp.exp(s - m)
        dens[(bl, hk)] = jnp.sum(p, axis=-1, keepdims=True) + jnp.exp(sinks[hk] - m)
        probs[(bl, hk)] = p.astype(BF16)
    for (bl, hk) in units:
        o = lax.dot_general(probs[(bl, hk)], values[(bl, hk)], (((1,), (1,)), ((), ())),
                            preferred_element_type=F32)
        o_ref[bl, hk] = o / dens[(bl, hk)]


def _sattn_call(q4, kn_bt, vn_bt, ck, cv, sinks, rolled, l, nb, tn):
    bb = SATTN_BATCH
    rows = GQA_GROUP * tn
    unit = (bb, N_KV_HEADS, HEAD_DIM, WINDOW)
    cache_shape = jax.ShapeDtypeStruct((l + 1, nb) + unit[1:], F32)
    in_specs = [
        pl.BlockSpec((bb, N_KV_HEADS, rows, HEAD_DIM), lambda i, s: (i, 0, 0, 0)),
        pl.BlockSpec((None,) + unit, lambda i, s: (l, i, 0, 0, 0)),
        pl.BlockSpec((None,) + unit, lambda i, s: (l, i, 0, 0, 0)),
        _const_spec((nb * tn, KV_WIDTH)),
        _const_spec((nb * tn, KV_WIDTH)),
        _const_spec((N_HEADS, tn, 2 * WINDOW)),
    ]
    args = [sinks, q4, ck, cv, kn_bt, vn_bt, jnp.asarray(_sample_bias(tn))]
    if l:
        in_specs += [pl.BlockSpec((l,) + unit, lambda i, s: (0, i, 0, 0, 0))] * 2
        args += list(rolled)
    grid_spec = pltpu.PrefetchScalarGridSpec(
        num_scalar_prefetch=1,
        grid=(nb // bb,),
        in_specs=in_specs,
        out_specs=[
            pl.BlockSpec((bb, N_KV_HEADS, rows, HEAD_DIM), lambda i, s: (i, 0, 0, 0)),
            pl.BlockSpec((l + 1,) + unit, lambda i, s: (0, i, 0, 0, 0)),
            pl.BlockSpec((l + 1,) + unit, lambda i, s: (0, i, 0, 0, 0)),
        ],
        scratch_shapes=[
            pltpu.VMEM((KV_WIDTH, nb * tn), F32),
            pltpu.VMEM((KV_WIDTH, nb * tn), F32),
        ],
    )
    return pl.pallas_call(
        functools.partial(_sattn_kernel, layer=l, nb=nb, tn=tn, bb=bb),
        grid_spec=grid_spec,
        out_shape=[jax.ShapeDtypeStruct((nb, N_KV_HEADS, rows, HEAD_DIM), F32),
                   cache_shape, cache_shape],
        compiler_params=pltpu.CompilerParams(
            dimension_semantics=("arbitrary",),
            vmem_limit_bytes=VMEM_LIMIT),
        name="sample_attn",
    )(*args)


def _spost_kernel(x_ref, o_ref, ma_ref, sgb_ref, mod_ref, wb_ref, wout_ref, xo_ref, *, tn):
    x = x_ref[...]
    if x.ndim == 3:
        x = jnp.swapaxes(x, 0, 1).reshape(xo_ref.shape)
    gt1 = jnp.concatenate([mod_ref[...]] * tn, axis=0)
    br_b = _bdot(o_ref[...], wb_ref[...])
    merged = ma_ref[...] + sgb_ref[...] * br_b
    xo_ref[...] = x + gt1 * _bdot(merged, wout_ref[...])


def _spost_call(x, o, ma, sgb, mod_s, wb, wout, l, nb, tn):
    out_shape = (tn * nb, D_MODEL)
    return pl.pallas_call(
        functools.partial(_spost_kernel, tn=tn),
        grid=(1,),
        in_specs=[
            _const_spec(x.shape), _const_spec(o.shape), _const_spec(ma.shape),
            _const_spec(sgb.shape),
            pl.BlockSpec((None, nb, D_MODEL), lambda i: (l, 0, 2), pipeline_mode=pl.Buffered(1)),
            _weight_spec(wb, (ATTN_WIDTH, D_MODEL), l),
            _weight_spec(wout, (D_MODEL, D_MODEL), l),
        ],
        out_specs=_whole_out_spec(out_shape),
        out_shape=jax.ShapeDtypeStruct(out_shape, F32),
        compiler_params=pltpu.CompilerParams(
            dimension_semantics=("arbitrary",),
            vmem_limit_bytes=VMEM_LIMIT),
        name="sample_post",
    )(x, o, ma, sgb, mod_s, wb, wout)


def kernel(x_prompt, x_sample, cache_k, cache_v, state_pool, c_prompt, c_sample,
           w_ada, b_ada, norm1_g, w_in, w_pool, pool_scale, attn_sinks, w_a, w_b,
           w_out, norm2_g, w_up, w_down, final_g):
    bp, tp, _ = x_prompt.shape
    nb, tn, _ = x_sample.shape
    assert tp % MIX_ROWS == 0 and tp % FFN_ROWS == 0 and tn <= 8 and nb == LANES
    assert bp % 8 == 0

    mod_p, mod_s, *mix_w = _mod_call(c_prompt, c_sample, w_ada, b_ada,
                                     [(w, 0) for w in (w_in, w_b, w_out)])
    wpa = _poolw_call(w_pool, pool_scale.reshape(DEPTH, 1, POOL_WIDTH), w_a)
    bias = jnp.asarray(_prompt_bias())
    fg = final_g.reshape(1, D_MODEL)
    g1 = norm1_g.reshape(DEPTH, 1, D_MODEL)
    g2 = norm2_g.reshape(DEPTH, 1, D_MODEL)

    xs = x_sample
    prefix_t = state_pool.transpose(0, 2, 1, 3)
    ck = cache_k.transpose(0, 1, 3, 4, 2)
    cv = cache_v.transpose(0, 1, 3, 4, 2)

    xp = x_prompt
    kp, vp, pp, ps, rolled = [], [], [], [], None
    for l in range(DEPTH):
        last = l == DEPTH - 1
        win, wb, wout = mix_w
        xp, nk, nv, npool, wup, wdn = _pmix_call(xp, mod_p, g1, win, wpa, wb, wout,
                                                 attn_sinks, bias, l, [(w_up, l), (w_down, l)])
        kp.append(nk); vp.append(nv); pp.append(npool)

        q, kn, vn, ma, sgb, npool_s = _sproj_call(xs, mod_s, g1, win, wpa, prefix_t, l, nb, tn)
        q4 = q.reshape(tn, nb, N_KV_HEADS, GQA_GROUP, HEAD_DIM).transpose(1, 2, 3, 0, 4)
        q4 = q4.reshape(nb, N_KV_HEADS, GQA_GROUP * tn, HEAD_DIM).astype(BF16)
        by_batch = lambda a: a.reshape(tn, nb, KV_WIDTH).transpose(1, 0, 2).reshape(nb * tn, KV_WIDTH)
        o4, *rolled = _sattn_call(q4, by_batch(kn), by_batch(vn), ck, cv, attn_sinks, rolled,
                                   l, nb, tn)
        o = o4.reshape(nb, N_KV_HEADS, GQA_GROUP, tn, HEAD_DIM).transpose(3, 0, 1, 2, 4)
        o = o.reshape(tn * nb, ATTN_WIDTH)
        xs = _spost_call(xs, o, ma, sgb, mod_s, wb, wout, l, nb, tn)
        ps.append(npool_s)

        nxt = [] if last else [(w, l + 1) for w in (w_in, w_b, w_out)]
        xp, xs, *mix_w = _ffn_call(xp, xs, mod_p, mod_s, g2, wup, wdn, fg, l, last, nxt)

    kv_shape_p = (DEPTH, bp, N_KV_HEADS, HEAD_DIM, WINDOW)
    return (xp,
            xs,
            jnp.stack(kp).reshape(kv_shape_p).transpose(0, 1, 4, 2, 3),
            jnp.stack(vp).reshape(kv_shape_p).transpose(0, 1, 4, 2, 3),
            jnp.stack(pp),
            rolled[0].transpose(0, 1, 4, 2, 3),
            rolled[1].transpose(0, 1, 4, 2, 3),
            jnp.stack(ps).transpose(0, 2, 1, 3))
```

```python
import functools

import numpy as np
import jax
import jax.numpy as jnp
from jax import lax
from jax.experimental import pallas as pl
from jax.experimental.pallas import tpu as pltpu

D_MODEL = 1024
DEPTH = 2
PAST_LEN = 16384
POOL_WIDTH = D_MODEL // 2
POOL_WINDOWS = (2, 4, 8, 16)
POOL_GROUP = POOL_WIDTH // len(POOL_WINDOWS)
POOL_PAD = max(POOL_WINDOWS) - 1
N_HEADS = 8
N_KV_HEADS = 2
HEAD_DIM = 64
GQA_GROUP = N_HEADS // N_KV_HEADS
ATTN_WIDTH = N_HEADS * HEAD_DIM
KV_WIDTH = N_KV_HEADS * HEAD_DIM
WINDOW = 128
ATTN_BLOCK = 128
D_FF = 4 * D_MODEL
RMS_EPS = 1e-6
NEG_INF = -1e30

OFF_U = 0
OFF_Q = OFF_U + POOL_WIDTH
OFF_K = OFF_Q + ATTN_WIDTH
OFF_V = OFF_K + KV_WIDTH
OFF_GA = OFF_V + KV_WIDTH
OFF_GB = OFF_GA + D_MODEL
IN_WIDTH = OFF_GB + D_MODEL
MOD_SLAB = 3 * D_MODEL

LANES = 128
HIST = 16
assert all(w & (w - 1) == 0 for w in POOL_WINDOWS) and list(POOL_WINDOWS) == sorted(POOL_WINDOWS)
assert HIST >= POOL_PAD
VMEM_LIMIT = 56 * 1024 * 1024
SUB_ROWS = 512
MIX_ROWS = 2 * SUB_ROWS
FFN_SUB_ROWS = 512
FFN_ROWS = 2 * FFN_SUB_ROWS
SATTN_BATCH = 16

F32 = jnp.float32
BF16 = jnp.bfloat16


def _bdot(a, b):
    return jnp.dot(a.astype(BF16), b.astype(BF16), preferred_element_type=F32)


def _rms_mod(x, g, sc, sh):
    ms = jnp.mean(x * x, axis=-1, keepdims=True)
    return (x * lax.rsqrt(ms + RMS_EPS) * g) * (1.0 + sc) + sh


def _alibi_slopes():
    return 2.0 ** (-8.0 * (np.arange(N_HEADS) + 1) / N_HEADS)


def _const_spec(shape):
    nd = len(shape)
    return pl.BlockSpec(shape, lambda *_: (0,) * nd, pipeline_mode=pl.Buffered(1))


def _layer_spec(shape, l):
    nd = len(shape)
    return pl.BlockSpec((None,) + tuple(shape), lambda *_: (l,) + (0,) * nd,
                        pipeline_mode=pl.Buffered(1))


def _whole_out_spec(shape):
    nd = len(shape)
    return pl.BlockSpec(shape, lambda *_: (0,) * nd)


def _weight_spec(w, shape, l):
    return _const_spec(shape) if w.shape == tuple(shape) else _layer_spec(shape, l)


def _convert_specs(jobs, n_steps, step_of):
    in_specs, out_specs, out_shapes = [], [], []
    for w, l in jobs:
        _, r, c = w.shape
        rows = r // n_steps
        assert rows * n_steps == r and rows % 16 == 0, (w.shape, n_steps)
        in_specs.append(pl.BlockSpec((None, rows, c), lambda *a, l=l: (l, step_of(*a), 0)))
        out_specs.append(pl.BlockSpec((rows, c), lambda *a: (step_of(*a), 0)))
        out_shapes.append(jax.ShapeDtypeStruct((r, c), BF16))
    return in_specs, out_specs, out_shapes


def _convert_slabs(src_refs, dst_refs):
    for src, dst in zip(src_refs, dst_refs):
        dst[...] = src[...].astype(BF16)


def _mod_kernel(cp_ref, cs_ref, w_ref, b_ref, *rest, n_cvt):
    cvt_in, (op_ref, os_ref), cvt_out = rest[:n_cvt], rest[n_cvt:n_cvt + 2], rest[n_cvt + 2:]
    _convert_slabs(cvt_in, cvt_out)
    w = w_ref[...].astype(BF16)
    b = b_ref[...]
    for c_ref, o_ref in ((cp_ref, op_ref), (cs_ref, os_ref)):
        c = c_ref[...]
        s = (c * jax.nn.sigmoid(c)).astype(BF16)
        o_ref[...] = jnp.dot(s, w, preferred_element_type=F32) + b


def _mod_call(c_p, c_s, w_ada, b_ada, cvt_jobs):
    tn = 1536
    n = 6 * D_MODEL
    nj = n // tn
    mp, ms = c_p.shape[0], c_s.shape[0]
    cvt_in, cvt_out, cvt_shapes = _convert_specs(cvt_jobs, DEPTH * nj, lambda l, j: l * nj + j)
    return pl.pallas_call(
        functools.partial(_mod_kernel, n_cvt=len(cvt_jobs)),
        grid=(DEPTH, nj),
        in_specs=[
            _const_spec((mp, D_MODEL)),
            _const_spec((ms, D_MODEL)),
            pl.BlockSpec((None, D_MODEL, tn), lambda l, j: (l, 0, j)),
            pl.BlockSpec((None, 1, tn), lambda l, j: (l, 0, j)),
        ] + cvt_in,
        out_specs=[
            pl.BlockSpec((None, mp, tn), lambda l, j: (l, 0, j)),
            pl.BlockSpec((None, ms, tn), lambda l, j: (l, 0, j)),
        ] + cvt_out,
        out_shape=[jax.ShapeDtypeStruct((DEPTH, mp, n), F32),
                   jax.ShapeDtypeStruct((DEPTH, ms, n), F32)] + cvt_shapes,
        compiler_params=pltpu.CompilerParams(
            dimension_semantics=("arbitrary", "arbitrary"),
            vmem_limit_bytes=VMEM_LIMIT),
        name="adaln_mod",
    )(c_p, c_s, w_ada, b_ada.reshape(DEPTH, 1, n), *[w for w, _ in cvt_jobs])


def _poolw_kernel(wp_ref, ps_ref, wa_ref, o_ref):
    for g in range(len(POOL_WINDOWS)):
        rows = slice(g * POOL_GROUP, (g + 1) * POOL_GROUP)
        o_ref[rows, :] = jnp.dot(wp_ref[g] * ps_ref[:, rows], wa_ref[rows, :],
                                 preferred_element_type=F32,
                                 precision=lax.Precision.HIGHEST).astype(BF16)


def _poolw_call(w_pool, pool_scale, w_a):
    return pl.pallas_call(
        _poolw_kernel,
        grid=(DEPTH,),
        in_specs=[
            pl.BlockSpec((None,) + w_pool.shape[1:], lambda l: (l, 0, 0, 0)),
            pl.BlockSpec((None, 1, POOL_WIDTH), lambda l: (l, 0, 0)),
            pl.BlockSpec((None, POOL_WIDTH, D_MODEL), lambda l: (l, 0, 0)),
        ],
        out_specs=pl.BlockSpec((None, POOL_WIDTH, D_MODEL), lambda l: (l, 0, 0)),
        out_shape=jax.ShapeDtypeStruct((DEPTH, POOL_WIDTH, D_MODEL), BF16),
        compiler_params=pltpu.CompilerParams(
            dimension_semantics=("arbitrary",),
            vmem_limit_bytes=VMEM_LIMIT),
        name="pool_weights",
    )(w_pool, pool_scale, w_a)


def _prompt_bias():
    i = np.arange(ATTN_BLOCK)[:, None]
    j = np.arange(2 * ATTN_BLOCK)[None, :]
    dist = i + ATTN_BLOCK - j
    valid = (dist >= 0) & (dist < WINDOW)
    valid_first = valid & (j >= ATTN_BLOCK)
    sl = _alibi_slopes()[:, None, None]
    b = np.where(valid[None], -sl * dist[None], NEG_INF)
    b0 = np.where(valid_first[None], -sl * dist[None], NEG_INF)
    return np.stack([b, b0]).astype(np.float32)


def _pmix_kernel(sinks_ref, x_ref, mod_ref, g1_ref, win_ref, wpa_ref,
                 wb_ref, wout_ref, bias_ref, *rest, layer, n_cvt):
    cvt_in, rest = rest[:n_cvt], rest[n_cvt:]
    (xo_ref, nk_ref, nv_ref, npool_ref), rest = rest[:4], rest[4:]
    cvt_out, (ubuf, ka, kb, va, vb, obuf) = rest[:n_cvt], rest[n_cvt:]
    _convert_slabs(cvt_in, cvt_out)
    tm, sub = MIX_ROWS, SUB_ROWS
    bi = pl.program_id(0)
    t = pl.program_id(1)
    nt = pl.num_programs(1)

    @pl.when(t == 0)
    def _init():
        ubuf[0:HIST, :] = jnp.zeros((HIST, POOL_WIDTH), F32)
        zero = jnp.zeros((N_KV_HEADS, ATTN_BLOCK, LANES), BF16)
        ka[:, 0:ATTN_BLOCK, :] = zero
        kb[:, 0:ATTN_BLOCK, :] = zero
        va[:, 0:ATTN_BLOCK, :] = zero
        vb[:, 0:ATTN_BLOCK, :] = zero

    mod = mod_ref[pl.ds(bi, 1), :]
    sh1 = mod[:, 0:D_MODEL]
    sc1 = mod[:, D_MODEL:2 * D_MODEL]
    gt1 = mod[:, 2 * D_MODEL:3 * D_MODEL]
    lane = lax.broadcasted_iota(jnp.int32, (sub, LANES), 1)
    low = lane < HEAD_DIM
    low_q = lax.broadcasted_iota(jnp.int32, (ATTN_BLOCK, LANES), 1) < HEAD_DIM
    nt_dims = (((1,), (1,)), ((), ()))
    st = [dict(ro=i * sub) for i in range(tm // sub)]

    def norm(c):
        c["x"] = x_ref[c["ro"]:c["ro"] + sub, :]
        c["h"] = _rms_mod(c["x"], g1_ref[...], sc1, sh1).astype(BF16)

    def proj(c):
        h = c["h"]
        c["u"] = jnp.dot(h, win_ref[:, OFF_U:OFF_Q], preferred_element_type=F32)
        q = jnp.dot(h, win_ref[:, OFF_Q:OFF_K], preferred_element_type=F32)
        c["qb"] = (q * (HEAD_DIM ** -0.5)).astype(BF16)
        kv = jnp.dot(h, win_ref[:, OFF_K:OFF_GA], preferred_element_type=F32)
        c["k"] = kv[:, 0:KV_WIDTH]
        c["v"] = kv[:, KV_WIDTH:2 * KV_WIDTH]

    def pool_sums(c):
        ro, u = c["ro"], c["u"]
        ubuf[HIST + ro:HIST + ro + sub, :] = u
        pos = t * tm + ro + lax.broadcasted_iota(jnp.int32, (sub, 1), 0)
        cur = ubuf[ro:ro + HIST + sub, :]
        d, w = [], 1
        for g, wg in enumerate(POOL_WINDOWS):
            while w < wg:
                cur = cur + pltpu.roll(cur, w, axis=0)
                w *= 2
            ug = u[:, g * POOL_GROUP:(g + 1) * POOL_GROUP]
            cnt = jnp.minimum(wg, pos + 1).astype(F32)
            d.append((cur[HIST:, 0:POOL_GROUP] / cnt - ug).astype(BF16))
            if g + 1 < len(POOL_WINDOWS):
                cur = cur[:, POOL_GROUP:]
        c["d"] = d

    def kv_store(c):
        r = ATTN_BLOCK + c["ro"]
        zero = jnp.zeros((sub, LANES), BF16)
        for x, xa, xb in ((c["k"], ka, kb), (c["v"], va, vb)):
            x16 = x.astype(BF16)
            xr16 = pltpu.roll(x, HEAD_DIM, axis=1).astype(BF16)
            xa[0, r:r + sub, :] = jnp.where(low, x16, zero)
            xb[0, r:r + sub, :] = jnp.where(low, zero, xr16)
            xa[1, r:r + sub, :] = jnp.where(low, xr16, zero)
            xb[1, r:r + sub, :] = jnp.where(low, zero, x16)

    def pool_proj(c):
        c["br_a"] = jnp.dot(jnp.concatenate(c["d"], axis=1), wpa_ref[...],
                            preferred_element_type=F32)

    def units(c):
        j0 = c["ro"] // ATTN_BLOCK
        return [(j, hk, pr) for j in range(j0, j0 + sub // ATTN_BLOCK)
                for hk in range(N_KV_HEADS) for pr in range(GQA_GROUP // 2)]

    def scores(c):
        sc = {}
        for (j, hk, pr) in units(c):
            r0 = j * ATTN_BLOCK
            c0 = (hk * (GQA_GROUP // 2) + pr) * LANES
            q2 = c["qb"][r0 - c["ro"]:r0 - c["ro"] + ATTN_BLOCK, c0:c0 + LANES]
            ks = (ka[hk, r0:r0 + 2 * ATTN_BLOCK, :], kb[hk, r0:r0 + 2 * ATTN_BLOCK, :])
            for e in range(2):
                sc[(j, hk, pr, e)] = lax.dot_general(q2, ks[e], nt_dims,
                                                     preferred_element_type=F32)
        c["s"] = sc

    def gate_a(c):
        c["ga"] = jnp.dot(c["h"], win_ref[:, OFF_GA:OFF_GB], preferred_element_type=F32)

    def gate_b(c):
        c["gb"] = jnp.dot(c["h"], win_ref[:, OFF_GB:IN_WIDTH], preferred_element_type=F32)

    def softmax(c):
        probs, dens = {}, {}
        for (j, hk, pr) in units(c):
            for e in range(2):
                head = hk * GQA_GROUP + pr * 2 + e
                if j == 0:
                    bias = jnp.where(t == 0, bias_ref[1, head], bias_ref[0, head])
                else:
                    bias = bias_ref[0, head]
                s = c["s"][(j, hk, pr, e)] + bias
                sink = sinks_ref[layer, head]
                m = jnp.maximum(jnp.max(s, axis=-1, keepdims=True), sink)
                p = jnp.exp(s - m)
                dens[(j, hk, pr, e)] = jnp.sum(p, axis=-1, keepdims=True) + jnp.exp(sink - m)
                probs[(j, hk, pr, e)] = p.astype(BF16)
        c["p"], c["den"] = probs, dens

    def values(c):
        for (j, hk, pr) in units(c):
            r0 = j * ATTN_BLOCK
            c0 = (hk * (GQA_GROUP // 2) + pr) * LANES
            vs = (va[hk, r0:r0 + 2 * ATTN_BLOCK, :], vb[hk, r0:r0 + 2 * ATTN_BLOCK, :])
            o2 = (jnp.dot(c["p"][(j, hk, pr, 0)], vs[0], preferred_element_type=F32)
                  + jnp.dot(c["p"][(j, hk, pr, 1)], vs[1], preferred_element_type=F32))
            den = jnp.where(low_q, c["den"][(j, hk, pr, 0)], c["den"][(j, hk, pr, 1)])
            obuf[r0:r0 + ATTN_BLOCK, c0:c0 + LANES] = o2 / den

    def tail(c):
        ro = c["ro"]
        br_b = _bdot(obuf[ro:ro + sub, :], wb_ref[...])
        merged = jax.nn.sigmoid(c["ga"]) * c["br_a"] + jax.nn.sigmoid(c["gb"]) * br_b
        xo_ref[ro:ro + sub, :] = c["x"] + gt1 * _bdot(merged, wout_ref[...])

    stages = [
        (norm,),
        (proj, gate_a),
        (pool_sums, kv_store),
        (pool_proj, scores, gate_b),
        (softmax,),
        (values,),
        (tail,),
    ]
    order = sorted((2 * s + 3 * i, i, s) for i in range(len(st)) for s in range(len(stages)))
    for _, i, s in order:
        for fn in stages[s]:
            fn(st[i])

    @pl.when(t == nt - 1)
    def _state():
        nk_ref[...] = st[-1]["k"][sub - WINDOW:, :].T
        nv_ref[...] = st[-1]["v"][sub - WINDOW:, :].T
        npool_ref[...] = ubuf[HIST + tm - POOL_PAD:HIST + tm, :]

    ubuf[0:HIST, :] = ubuf[tm:tm + HIST, :]
    for buf in (ka, kb, va, vb):
        buf[:, 0:ATTN_BLOCK, :] = buf[:, tm:tm + ATTN_BLOCK, :]


def _pmix_call(x, mod_p, g1, win, wpa, wb, wout, sinks, bias, l, cvt_jobs):
    b, t, _ = x.shape
    tm = MIX_ROWS
    nt = t // tm
    cvt_in, cvt_out, cvt_shapes = _convert_specs(cvt_jobs, b * nt, lambda i, j, s: i * nt + j)
    grid_spec = pltpu.PrefetchScalarGridSpec(
        num_scalar_prefetch=1,
        grid=(b, nt),
        in_specs=[
            pl.BlockSpec((None, tm, D_MODEL), lambda i, j, s: (i, j, 0)),
            pl.BlockSpec((None, b, MOD_SLAB), lambda i, j, s: (l, 0, 0),
                         pipeline_mode=pl.Buffered(1)),
            _layer_spec((1, D_MODEL), l),
            _weight_spec(win, (D_MODEL, IN_WIDTH), l),
            _weight_spec(wpa, (POOL_WIDTH, D_MODEL), l),
            _weight_spec(wb, (ATTN_WIDTH, D_MODEL), l),
            _weight_spec(wout, (D_MODEL, D_MODEL), l),
            _const_spec((2, N_HEADS, ATTN_BLOCK, 2 * ATTN_BLOCK)),
        ] + cvt_in,
        out_specs=[
            pl.BlockSpec((None, tm, D_MODEL), lambda i, j, s: (i, j, 0)),
            pl.BlockSpec((None, KV_WIDTH, WINDOW), lambda i, j, s: (i, 0, 0)),
            pl.BlockSpec((None, KV_WIDTH, WINDOW), lambda i, j, s: (i, 0, 0)),
            pl.BlockSpec((None, POOL_PAD, POOL_WIDTH), lambda i, j, s: (i, 0, 0)),
        ] + cvt_out,
        scratch_shapes=[
            pltpu.VMEM((HIST + tm, POOL_WIDTH), F32),
            pltpu.VMEM((N_KV_HEADS, ATTN_BLOCK + tm, LANES), BF16),
            pltpu.VMEM((N_KV_HEADS, ATTN_BLOCK + tm, LANES), BF16),
            pltpu.VMEM((N_KV_HEADS, ATTN_BLOCK + tm, LANES), BF16),
            pltpu.VMEM((N_KV_HEADS, ATTN_BLOCK + tm, LANES), BF16),
            pltpu.VMEM((tm, ATTN_WIDTH), F32),
        ],
    )
    return pl.pallas_call(
        functools.partial(_pmix_kernel, layer=l, n_cvt=len(cvt_jobs)),
        grid_spec=grid_spec,
        out_shape=[
            jax.ShapeDtypeStruct((b, t, D_MODEL), F32),
            jax.ShapeDtypeStruct((b, KV_WIDTH, WINDOW), F32),
            jax.ShapeDtypeStruct((b, KV_WIDTH, WINDOW), F32),
            jax.ShapeDtypeStruct((b, POOL_PAD, POOL_WIDTH), F32),
        ] + cvt_shapes,
        compiler_params=pltpu.CompilerParams(
            dimension_semantics=("arbitrary", "arbitrary"),
            vmem_limit_bytes=VMEM_LIMIT),
        name="prompt_mixer",
    )(sinks, x, mod_p, g1, win, wpa, wb, wout, bias, *[w for w, _ in cvt_jobs])


def _ffn_kernel(xp_ref, xs_ref, modp_ref, mods_ref, g2_ref, wup_ref, wdn_ref, fg_ref,
                so_ref, ma_ref, sgb_ref, gt1_ref, wb_ref, wout_ref,
                *rest, final, n_prompt, tiles_per_row, reps, n_cvt):
    cvt_in, (op_ref, os_ref), cvt_out = rest[:n_cvt], rest[n_cvt:n_cvt + 2], rest[n_cvt + 2:]
    _convert_slabs(cvt_in, cvt_out)
    i = pl.program_id(0)

    def ffn(load, n_rows, o_ref, mod):
        sh2, sc2, gt2 = (mod[:, k * D_MODEL:(k + 1) * D_MODEL] for k in range(3))
        subs = range(0, n_rows, FFN_SUB_ROWS)
        rows = lambda a, r: a if a.shape[0] == 1 else a[r:r + FFN_SUB_ROWS]
        norm = lambda r: _rms_mod(load(r), g2_ref[...], rows(sc2, r), rows(sh2, r)).astype(BF16)
        h2 = norm(0)
        for r in subs:
            ff = jnp.dot(h2, wup_ref[...], preferred_element_type=F32)
            if r + FFN_SUB_ROWS in subs:
                h2 = norm(r + FFN_SUB_ROWS)
            ff = jnp.square(jnp.maximum(ff, 0.0))
            y = load(r) + rows(gt2, r) * _bdot(ff, wdn_ref[...])
            if final:
                ms = jnp.mean(y * y, axis=-1, keepdims=True)
                y = y * lax.rsqrt(ms + RMS_EPS) * fg_ref[...]
            if len(o_ref.shape) == 3:
                o_ref[...] = jnp.swapaxes(y.reshape(reps, y.shape[0] // reps, D_MODEL), 0, 1)
            else:
                o_ref[r:r + FFN_SUB_ROWS, :] = y

    @pl.when(i < n_prompt)
    def _prompt():
        ffn(lambda r: xp_ref[r:r + FFN_SUB_ROWS, :], xp_ref.shape[0], op_ref,
            modp_ref[pl.ds(i // tiles_per_row, 1), :])

    @pl.when(i == n_prompt)
    def _sample():
        x = xs_ref[...]
        if x.ndim == 3:
            x = jnp.swapaxes(x, 0, 1).reshape(ma_ref.shape)
        gt1 = jnp.concatenate([gt1_ref[...]] * reps, axis=0)
        br_b = _bdot(so_ref[...], wb_ref[...])
        merged = ma_ref[...] + sgb_ref[...] * br_b
        x = x + gt1 * _bdot(merged, wout_ref[...])
        ffn(lambda r: x[r:r + FFN_SUB_ROWS, :], x.shape[0], os_ref,
            jnp.concatenate([mods_ref[...]] * reps, axis=0))


def _ffn_call(xp, xs, so, ma, sgb, mod_p, mod_s, g2, wup, wdn, wb, wout, fg, l, final, cvt_jobs):
    b, t, _ = xp.shape
    ms = so.shape[0]
    nb = mod_s.shape[1]
    tm = FFN_ROWS
    tpr = t // tm
    n_prompt = b * tpr
    last = n_prompt - 1

    def p_idx(i):
        ii = jnp.minimum(i, last)
        return (ii // tpr, ii % tpr, 0)

    assert ms <= FFN_SUB_ROWS
    s_shape = (nb, ms // nb, D_MODEL) if final else (ms, D_MODEL)
    cvt_in, cvt_out, cvt_shapes = _convert_specs(cvt_jobs, n_prompt, lambda i: jnp.minimum(i, last))
    return pl.pallas_call(
        functools.partial(_ffn_kernel, final=final, n_prompt=n_prompt, tiles_per_row=tpr,
                          reps=ms // nb, n_cvt=len(cvt_jobs)),
        grid=(n_prompt + 1,),
        in_specs=[
            pl.BlockSpec((None, tm, D_MODEL), p_idx),
            _const_spec(xs.shape),
            pl.BlockSpec((None, b, MOD_SLAB), lambda i: (l, 0, 1), pipeline_mode=pl.Buffered(1)),
            pl.BlockSpec((None, nb, MOD_SLAB), lambda i: (l, 0, 1), pipeline_mode=pl.Buffered(1)),
            _layer_spec((1, D_MODEL), l),
            _weight_spec(wup, (D_MODEL, D_FF), l),
            _weight_spec(wdn, (D_FF, D_MODEL), l),
            _const_spec((1, D_MODEL)),
            _const_spec(so.shape), _const_spec(ma.shape), _const_spec(sgb.shape),
            pl.BlockSpec((None, nb, D_MODEL), lambda i: (l, 0, 2), pipeline_mode=pl.Buffered(1)),
            _weight_spec(wb, (ATTN_WIDTH, D_MODEL), l),
            _weight_spec(wout, (D_MODEL, D_MODEL), l),
        ] + cvt_in,
        out_specs=[
            pl.BlockSpec((None, tm, D_MODEL), p_idx),
            _whole_out_spec(s_shape),
        ] + cvt_out,
        out_shape=[jax.ShapeDtypeStruct((b, t, D_MODEL), F32),
                   jax.ShapeDtypeStruct(s_shape, F32)] + cvt_shapes,
        compiler_params=pltpu.CompilerParams(
            dimension_semantics=("arbitrary",),
            vmem_limit_bytes=VMEM_LIMIT),
        name="ffn_final" if final else "ffn",
    )(xp, xs, mod_p, mod_s, g2, wup, wdn, fg, so, ma, sgb, mod_s, wb, wout,
      *[w for w, _ in cvt_jobs])


def _sproj_kernel(x_ref, mod_ref, g1_ref, win_ref, wpa_ref, pre_ref,
                  q_ref, k_ref, v_ref, ma_ref, sgb_ref, npool_ref, *, nb, tn):
    x = x_ref[...]
    if x.ndim == 3:
        x = jnp.swapaxes(x, 0, 1).reshape(tn * nb, D_MODEL)
    mod = jnp.concatenate([mod_ref[:, 0:2 * D_MODEL]] * tn, axis=0)
    sh1 = mod[:, 0:D_MODEL]
    sc1 = mod[:, D_MODEL:2 * D_MODEL]
    h = _rms_mod(x, g1_ref[...], sc1, sh1).astype(BF16)

    u = jnp.dot(h, win_ref[:, OFF_U:OFF_Q], preferred_element_type=F32)
    def up(r):
        if r < POOL_PAD:
            return pre_ref[r]
        r -= POOL_PAD
        return u[r * nb:(r + 1) * nb, :]

    d_rows = []
    for tt in range(tn):
        parts = []
        for g, w in enumerate(POOL_WINDOWS):
            lo = g * POOL_GROUP
            acc = up(POOL_PAD + tt)[:, lo:lo + POOL_GROUP]
            for s in range(1, w):
                acc = acc + up(POOL_PAD + tt - s)[:, lo:lo + POOL_GROUP]
            cnt = float(min(w, PAST_LEN + tt + 1))
            parts.append(acc / cnt - up(POOL_PAD + tt)[:, lo:lo + POOL_GROUP])
        d_rows.append(parts)
    d = jnp.concatenate([jnp.concatenate(parts, axis=1) for parts in d_rows], axis=0)
    br_a = _bdot(d, wpa_ref[...])
    for r in range(POOL_PAD):
        npool_ref[r] = up(r + tn)

    q = jnp.dot(h, win_ref[:, OFF_Q:OFF_K], preferred_element_type=F32)
    q_ref[...] = q * (HEAD_DIM ** -0.5)
    kv = jnp.dot(h, win_ref[:, OFF_K:OFF_GA], preferred_element_type=F32)
    k_ref[...] = kv[:, 0:KV_WIDTH]
    v_ref[...] = kv[:, KV_WIDTH:2 * KV_WIDTH]
    ga = jnp.dot(h, win_ref[:, OFF_GA:OFF_GB], preferred_element_type=F32)
    ma_ref[...] = jax.nn.sigmoid(ga) * br_a
    gb = jnp.dot(h, win_ref[:, OFF_GB:IN_WIDTH], preferred_element_type=F32)
    sgb_ref[...] = jax.nn.sigmoid(gb)


def _sproj_call(x, mod_s, g1, win, wpa, prefix_t, l, nb, tn):
    m = tn * nb
    shapes = [
        jax.ShapeDtypeStruct((m, ATTN_WIDTH), F32),
        jax.ShapeDtypeStruct((m, KV_WIDTH), F32),
        jax.ShapeDtypeStruct((m, KV_WIDTH), F32),
        jax.ShapeDtypeStruct((m, D_MODEL), F32),
        jax.ShapeDtypeStruct((m, D_MODEL), F32),
        jax.ShapeDtypeStruct((POOL_PAD, nb, POOL_WIDTH), F32),
    ]
    return pl.pallas_call(
        functools.partial(_sproj_kernel, nb=nb, tn=tn),
        grid=(1,),
        in_specs=[
            _const_spec(x.shape),
            pl.BlockSpec((None, nb, MOD_SLAB), lambda i: (l, 0, 0), pipeline_mode=pl.Buffered(1)),
            _layer_spec((1, D_MODEL), l),
            _weight_spec(win, (D_MODEL, IN_WIDTH), l),
            _weight_spec(wpa, (POOL_WIDTH, D_MODEL), l),
            _layer_spec((POOL_PAD, nb, POOL_WIDTH), l),
        ],
        out_specs=[_whole_out_spec(s.shape) for s in shapes],
        out_shape=shapes,
        compiler_params=pltpu.CompilerParams(
            dimension_semantics=("arbitrary",),
            vmem_limit_bytes=VMEM_LIMIT),
        name="sample_proj",
    )(x, mod_s, g1, win, wpa, prefix_t)


def _sample_bias(tn):
    t = np.arange(tn)[:, None]
    lane = np.arange(2 * WINDOW)[None, :]
    cached = lane < WINDOW
    new_t = lane - (2 * WINDOW - tn)
    dist = np.where(cached, t + WINDOW - lane, t - new_t)
    valid = np.where(cached, (dist >= 0) & (dist < WINDOW), (new_t >= 0) & (dist >= 0))
    sl = _alibi_slopes()[:, None, None]
    return np.where(valid[None], -sl * dist[None], NEG_INF).astype(np.float32)


def _sattn_kernel(sinks_ref, q_ref, kc_ref, vc_ref, kn_ref, vn_ref, bias_ref, *rest,
                   layer, nb, tn, bb):
    if layer:
        pk_ref, pv_ref = rest[:2]
        rest = rest[2:]
    o_ref, nk_ref, nv_ref, wk, wv = rest
    i = pl.program_id(0)
    rows = GQA_GROUP * tn
    keep = WINDOW - tn

    @pl.when(i == 0)
    def _new_rows():
        for c0 in range(0, nb * tn, LANES):
            wk[:, c0:c0 + LANES] = kn_ref[c0:c0 + LANES, :].T
            wv[:, c0:c0 + LANES] = vn_ref[c0:c0 + LANES, :].T

    if layer:
        nk_ref[0:layer] = pk_ref[...]
        nv_ref[0:layer] = pv_ref[...]

    lane0 = i * (bb * tn)
    tile0 = pl.multiple_of((lane0 // LANES) * LANES, LANES)
    off0 = lane0 % LANES
    wkt = wk[:, pl.ds(tile0, LANES)]
    wvt = wv[:, pl.ds(tile0, LANES)]
    lane = lax.broadcasted_iota(jnp.int32, (HEAD_DIM, WINDOW), 1)
    tail = lane >= keep
    grow = lax.broadcasted_iota(jnp.int32, (rows, 1), 0) // tn
    units = [(bl, hk) for bl in range(bb) for hk in range(N_KV_HEADS)]
    biases, sinks = [], []
    for hk in range(N_KV_HEADS):
        biases.append(jnp.concatenate(
            [bias_ref[hk * GQA_GROUP + g] for g in range(GQA_GROUP)], axis=0))
        sink = jnp.zeros((rows, 1), F32)
        for g in range(GQA_GROUP):
            sink = jnp.where(grow == g, sinks_ref[layer, hk * GQA_GROUP + g], sink)
        sinks.append(sink)

    scores, values = {}, {}
    for bl in range(bb):
        shift = (keep - off0 - bl * tn) % LANES
        nkb = pltpu.roll(wkt, shift, axis=1)
        nvb = pltpu.roll(wvt, shift, axis=1)
        for hk in range(N_KV_HEADS):
            kt = kc_ref[bl, hk]
            vt = vc_ref[bl, hk]
            nkt = jnp.where(tail, nkb[hk * HEAD_DIM:(hk + 1) * HEAD_DIM, :], 0.0)
            nvt = jnp.where(tail, nvb[hk * HEAD_DIM:(hk + 1) * HEAD_DIM, :], 0.0)
            nk_ref[layer, bl, hk] = jnp.where(tail, nkt, pltpu.roll(kt, keep, axis=1))
            nv_ref[layer, bl, hk] = jnp.where(tail, nvt, pltpu.roll(vt, keep, axis=1))
            keys = jnp.concatenate([kt, nkt], axis=1).astype(BF16)
            values[(bl, hk)] = jnp.concatenate([vt, nvt], axis=1).astype(BF16)
            scores[(bl, hk)] = jnp.dot(q_ref[bl, hk], keys, preferred_element_type=F32)
    probs, dens = {}, {}
    for (bl, hk) in units:
        s = scores[(bl, hk)] + biases[hk]
        m = jnp.maximum(jnp.max(s, axis=-1, keepdims=True), sinks[hk])
        p = jnp.exp(s - m)
        dens[(bl, hk)] = jnp.sum(p, axis=-1, keepdims=True) + jnp.exp(sinks[hk] - m)
        probs[(bl, hk)] = p.astype(BF16)
    for (bl, hk) in units:
        o = lax.dot_general(probs[(bl, hk)], values[(bl, hk)], (((1,), (1,)), ((), ())),
                            preferred_element_type=F32)
        o_ref[bl, hk] = o / dens[(bl, hk)]


def _sattn_call(q4, kn_bt, vn_bt, ck, cv, sinks, rolled, l, nb, tn):
    bb = SATTN_BATCH
    rows = GQA_GROUP * tn
    unit = (bb, N_KV_HEADS, HEAD_DIM, WINDOW)
    cache_shape = jax.ShapeDtypeStruct((l + 1, nb) + unit[1:], F32)
    in_specs = [
        pl.BlockSpec((bb, N_KV_HEADS, rows, HEAD_DIM), lambda i, s: (i, 0, 0, 0)),
        pl.BlockSpec((None,) + unit, lambda i, s: (l, i, 0, 0, 0)),
        pl.BlockSpec((None,) + unit, lambda i, s: (l, i, 0, 0, 0)),
        _const_spec((nb * tn, KV_WIDTH)),
        _const_spec((nb * tn, KV_WIDTH)),
        _const_spec((N_HEADS, tn, 2 * WINDOW)),
    ]
    args = [sinks, q4, ck, cv, kn_bt, vn_bt, jnp.asarray(_sample_bias(tn))]
    if l:
        in_specs += [pl.BlockSpec((l,) + unit, lambda i, s: (0, i, 0, 0, 0))] * 2
        args += list(rolled)
    grid_spec = pltpu.PrefetchScalarGridSpec(
        num_scalar_prefetch=1,
        grid=(nb // bb,),
        in_specs=in_specs,
        out_specs=[
            pl.BlockSpec((bb, N_KV_HEADS, rows, HEAD_DIM), lambda i, s: (i, 0, 0, 0)),
            pl.BlockSpec((l + 1,) + unit, lambda i, s: (0, i, 0, 0, 0)),
            pl.BlockSpec((l + 1,) + unit, lambda i, s: (0, i, 0, 0, 0)),
        ],
        scratch_shapes=[
            pltpu.VMEM((KV_WIDTH, nb * tn), F32),
            pltpu.VMEM((KV_WIDTH, nb * tn), F32),
        ],
    )
    return pl.pallas_call(
        functools.partial(_sattn_kernel, layer=l, nb=nb, tn=tn, bb=bb),
        grid_spec=grid_spec,
        out_shape=[jax.ShapeDtypeStruct((nb, N_KV_HEADS, rows, HEAD_DIM), F32),
                   cache_shape, cache_shape],
        compiler_params=pltpu.CompilerParams(
            dimension_semantics=("arbitrary",),
            vmem_limit_bytes=VMEM_LIMIT),
        name="sample_attn",
    )(*args)


def kernel(x_prompt, x_sample, cache_k, cache_v, state_pool, c_prompt, c_sample,
           w_ada, b_ada, norm1_g, w_in, w_pool, pool_scale, attn_sinks, w_a, w_b,
           w_out, norm2_g, w_up, w_down, final_g):
    bp, tp, _ = x_prompt.shape
    nb, tn, _ = x_sample.shape
    assert tp % MIX_ROWS == 0 and tp % FFN_ROWS == 0 and tn <= 8 and nb == LANES
    assert bp % 8 == 0

    mod_p, mod_s, *mix_w = _mod_call(c_prompt, c_sample, w_ada, b_ada,
                                     [(w, 0) for w in (w_in, w_b, w_out)])
    wpa = _poolw_call(w_pool, pool_scale.reshape(DEPTH, 1, POOL_WIDTH), w_a)
    bias = jnp.asarray(_prompt_bias())
    fg = final_g.reshape(1, D_MODEL)
    g1 = norm1_g.reshape(DEPTH, 1, D_MODEL)
    g2 = norm2_g.reshape(DEPTH, 1, D_MODEL)

    xs = x_sample
    prefix_t = state_pool.transpose(0, 2, 1, 3)
    ck = cache_k.transpose(0, 1, 3, 4, 2)
    cv = cache_v.transpose(0, 1, 3, 4, 2)

    xp = x_prompt
    kp, vp, pp, ps, rolled = [], [], [], [], None
    for l in range(DEPTH):
        last = l == DEPTH - 1
        win, wb, wout = mix_w
        xp, nk, nv, npool, wup, wdn = _pmix_call(xp, mod_p, g1, win, wpa, wb, wout,
                                                 attn_sinks, bias, l, [(w_up, l), (w_down, l)])
        kp.append(nk); vp.append(nv); pp.append(npool)

        q, kn, vn, ma, sgb, npool_s = _sproj_call(xs, mod_s, g1, win, wpa, prefix_t, l, nb, tn)
        q4 = q.reshape(tn, nb, N_KV_HEADS, GQA_GROUP, HEAD_DIM).transpose(1, 2, 3, 0, 4)
        q4 = q4.reshape(nb, N_KV_HEADS, GQA_GROUP * tn, HEAD_DIM).astype(BF16)
        by_batch = lambda a: a.reshape(tn, nb, KV_WIDTH).transpose(1, 0, 2).reshape(nb * tn, KV_WIDTH)
        o4, *rolled = _sattn_call(q4, by_batch(kn), by_batch(vn), ck, cv, attn_sinks, rolled,
                                   l, nb, tn)
        o = o4.reshape(nb, N_KV_HEADS, GQA_GROUP, tn, HEAD_DIM).transpose(3, 0, 1, 2, 4)
        o = o.reshape(tn * nb, ATTN_WIDTH)
        ps.append(npool_s)

        nxt = [] if last else [(w, l + 1) for w in (w_in, w_b, w_out)]
        xp, xs, *mix_w = _ffn_call(xp, xs, o, ma, sgb, mod_p, mod_s, g2, wup, wdn, wb, wout, fg,
                                   l, last, nxt)

    kv_shape_p = (DEPTH, bp, N_KV_HEADS, HEAD_DIM, WINDOW)
    return (xp,
            xs,
            jnp.stack(kp).reshape(kv_shape_p).transpose(0, 1, 4, 2, 3),
            jnp.stack(vp).reshape(kv_shape_p).transpose(0, 1, 4, 2, 3),
            jnp.stack(pp),
            rolled[0].transpose(0, 1, 4, 2, 3),
            rolled[1].transpose(0, 1, 4, 2, 3),
            jnp.stack(ps).transpose(0, 2, 1, 3))
```

```python
import functools

import numpy as np
import jax
import jax.numpy as jnp
from jax import lax
from jax.experimental import pallas as pl
from jax.experimental.pallas import tpu as pltpu

D_MODEL = 1024
DEPTH = 2
PAST_LEN = 16384
POOL_WIDTH = D_MODEL // 2
POOL_WINDOWS = (2, 4, 8, 16)
POOL_GROUP = POOL_WIDTH // len(POOL_WINDOWS)
POOL_PAD = max(POOL_WINDOWS) - 1
N_HEADS = 8
N_KV_HEADS = 2
HEAD_DIM = 64
GQA_GROUP = N_HEADS // N_KV_HEADS
ATTN_WIDTH = N_HEADS * HEAD_DIM
KV_WIDTH = N_KV_HEADS * HEAD_DIM
WINDOW = 128
ATTN_BLOCK = 128
D_FF = 4 * D_MODEL
RMS_EPS = 1e-6
NEG_INF = -1e30

OFF_U = 0
OFF_Q = OFF_U + POOL_WIDTH
OFF_K = OFF_Q + ATTN_WIDTH
OFF_V = OFF_K + KV_WIDTH
OFF_GA = OFF_V + KV_WIDTH
OFF_GB = OFF_GA + D_MODEL
IN_WIDTH = OFF_GB + D_MODEL
MOD_SLAB = 3 * D_MODEL

LANES = 128
HIST = 16
assert all(w & (w - 1) == 0 for w in POOL_WINDOWS) and list(POOL_WINDOWS) == sorted(POOL_WINDOWS)
assert HIST >= POOL_PAD
VMEM_LIMIT = 56 * 1024 * 1024
SUB_ROWS = 512
MIX_ROWS = 2 * SUB_ROWS
FFN_SUB_ROWS = 512
FFN_ROWS = 2 * FFN_SUB_ROWS
SATTN_BATCH = 16

F32 = jnp.float32
BF16 = jnp.bfloat16


def _bdot(a, b):
    return jnp.dot(a.astype(BF16), b.astype(BF16), preferred_element_type=F32)


def _rms_mod(x, g, sc, sh):
    ms = jnp.mean(x * x, axis=-1, keepdims=True)
    return (x * lax.rsqrt(ms + RMS_EPS) * g) * (1.0 + sc) + sh


def _alibi_slopes():
    return 2.0 ** (-8.0 * (np.arange(N_HEADS) + 1) / N_HEADS)


def _const_spec(shape):
    nd = len(shape)
    return pl.BlockSpec(shape, lambda *_: (0,) * nd, pipeline_mode=pl.Buffered(1))


def _layer_spec(shape, l):
    nd = len(shape)
    return pl.BlockSpec((None,) + tuple(shape), lambda *_: (l,) + (0,) * nd,
                        pipeline_mode=pl.Buffered(1))


def _whole_out_spec(shape):
    nd = len(shape)
    return pl.BlockSpec(shape, lambda *_: (0,) * nd)


def _weight_spec(w, shape, l):
    return _const_spec(shape) if w.shape == tuple(shape) else _layer_spec(shape, l)


def _convert_specs(jobs, n_steps, step_of):
    in_specs, out_specs, out_shapes = [], [], []
    for w, l in jobs:
        _, r, c = w.shape
        rows = r // n_steps
        assert rows * n_steps == r and rows % 16 == 0, (w.shape, n_steps)
        in_specs.append(pl.BlockSpec((None, rows, c), lambda *a, l=l: (l, step_of(*a), 0)))
        out_specs.append(pl.BlockSpec((rows, c), lambda *a: (step_of(*a), 0)))
        out_shapes.append(jax.ShapeDtypeStruct((r, c), BF16))
    return in_specs, out_specs, out_shapes


def _convert_slabs(src_refs, dst_refs):
    for src, dst in zip(src_refs, dst_refs):
        dst[...] = src[...].astype(BF16)


def _mod_kernel(cp_ref, cs_ref, w_ref, b_ref, *rest, n_cvt):
    cvt_in, (op_ref, os_ref), cvt_out = rest[:n_cvt], rest[n_cvt:n_cvt + 2], rest[n_cvt + 2:]
    _convert_slabs(cvt_in, cvt_out)
    w = w_ref[...].astype(BF16)
    b = b_ref[...]
    for c_ref, o_ref in ((cp_ref, op_ref), (cs_ref, os_ref)):
        c = c_ref[...]
        s = (c * jax.nn.sigmoid(c)).astype(BF16)
        o_ref[...] = jnp.dot(s, w, preferred_element_type=F32) + b


def _mod_call(c_p, c_s, w_ada, b_ada, cvt_jobs):
    tn = 1536
    n = 6 * D_MODEL
    nj = n // tn
    mp, ms = c_p.shape[0], c_s.shape[0]
    cvt_in, cvt_out, cvt_shapes = _convert_specs(cvt_jobs, DEPTH * nj, lambda l, j: l * nj + j)
    return pl.pallas_call(
        functools.partial(_mod_kernel, n_cvt=len(cvt_jobs)),
        grid=(DEPTH, nj),
        in_specs=[
            _const_spec((mp, D_MODEL)),
            _const_spec((ms, D_MODEL)),
            pl.BlockSpec((None, D_MODEL, tn), lambda l, j: (l, 0, j)),
            pl.BlockSpec((None, 1, tn), lambda l, j: (l, 0, j)),
        ] + cvt_in,
        out_specs=[
            pl.BlockSpec((None, mp, tn), lambda l, j: (l, 0, j)),
            pl.BlockSpec((None, ms, tn), lambda l, j: (l, 0, j)),
        ] + cvt_out,
        out_shape=[jax.ShapeDtypeStruct((DEPTH, mp, n), F32),
                   jax.ShapeDtypeStruct((DEPTH, ms, n), F32)] + cvt_shapes,
        compiler_params=pltpu.CompilerParams(
            dimension_semantics=("arbitrary", "arbitrary"),
            vmem_limit_bytes=VMEM_LIMIT),
        name="adaln_mod",
    )(c_p, c_s, w_ada, b_ada.reshape(DEPTH, 1, n), *[w for w, _ in cvt_jobs])


def _poolw_kernel(wp_ref, ps_ref, wa_ref, o_ref):
    for g in range(len(POOL_WINDOWS)):
        rows = slice(g * POOL_GROUP, (g + 1) * POOL_GROUP)
        o_ref[rows, :] = jnp.dot(wp_ref[g] * ps_ref[:, rows], wa_ref[rows, :],
                                 preferred_element_type=F32,
                                 precision=lax.Precision.HIGHEST).astype(BF16)


def _poolw_call(w_pool, pool_scale, w_a):
    return pl.pallas_call(
        _poolw_kernel,
        grid=(DEPTH,),
        in_specs=[
            pl.BlockSpec((None,) + w_pool.shape[1:], lambda l: (l, 0, 0, 0)),
            pl.BlockSpec((None, 1, POOL_WIDTH), lambda l: (l, 0, 0)),
            pl.BlockSpec((None, POOL_WIDTH, D_MODEL), lambda l: (l, 0, 0)),
        ],
        out_specs=pl.BlockSpec((None, POOL_WIDTH, D_MODEL), lambda l: (l, 0, 0)),
        out_shape=jax.ShapeDtypeStruct((DEPTH, POOL_WIDTH, D_MODEL), BF16),
        compiler_params=pltpu.CompilerParams(
            dimension_semantics=("arbitrary",),
            vmem_limit_bytes=VMEM_LIMIT),
        name="pool_weights",
    )(w_pool, pool_scale, w_a)


def _prompt_bias():
    i = np.arange(ATTN_BLOCK)[:, None]
    j = np.arange(2 * ATTN_BLOCK)[None, :]
    dist = i + ATTN_BLOCK - j
    valid = (dist >= 0) & (dist < WINDOW)
    valid_first = valid & (j >= ATTN_BLOCK)
    sl = _alibi_slopes()[:, None, None]
    b = np.where(valid[None], -sl * dist[None], NEG_INF)
    b0 = np.where(valid_first[None], -sl * dist[None], NEG_INF)
    return np.stack([b, b0]).astype(np.float32)


def _pmix_kernel(sinks_ref, x_ref, mod_ref, g1_ref, win_ref, wpa_ref,
                 wb_ref, wout_ref, bias_ref, *rest, layer, n_cvt):
    cvt_in, rest = rest[:n_cvt], rest[n_cvt:]
    (xo_ref, nk_ref, nv_ref, npool_ref), rest = rest[:4], rest[4:]
    cvt_out, (ubuf, ka, kb, va, vb, obuf) = rest[:n_cvt], rest[n_cvt:]
    _convert_slabs(cvt_in, cvt_out)
    tm, sub = MIX_ROWS, SUB_ROWS
    bi = pl.program_id(0)
    t = pl.program_id(1)
    nt = pl.num_programs(1)

    @pl.when(t == 0)
    def _init():
        ubuf[0:HIST, :] = jnp.zeros((HIST, POOL_WIDTH), F32)
        zero = jnp.zeros((N_KV_HEADS, ATTN_BLOCK, LANES), BF16)
        ka[:, 0:ATTN_BLOCK, :] = zero
        kb[:, 0:ATTN_BLOCK, :] = zero
        va[:, 0:ATTN_BLOCK, :] = zero
        vb[:, 0:ATTN_BLOCK, :] = zero

    mod = mod_ref[pl.ds(bi, 1), :]
    sh1 = mod[:, 0:D_MODEL]
    sc1 = mod[:, D_MODEL:2 * D_MODEL]
    gt1 = mod[:, 2 * D_MODEL:3 * D_MODEL]
    lane = lax.broadcasted_iota(jnp.int32, (sub, LANES), 1)
    low = lane < HEAD_DIM
    low_q = lax.broadcasted_iota(jnp.int32, (ATTN_BLOCK, LANES), 1) < HEAD_DIM
    nt_dims = (((1,), (1,)), ((), ()))
    st = [dict(ro=i * sub) for i in range(tm // sub)]

    def norm(c):
        c["x"] = x_ref[c["ro"]:c["ro"] + sub, :]
        c["h"] = _rms_mod(c["x"], g1_ref[...], sc1, sh1).astype(BF16)

    def proj(c):
        h = c["h"]
        c["u"] = jnp.dot(h, win_ref[:, OFF_U:OFF_Q], preferred_element_type=F32)
        q = jnp.dot(h, win_ref[:, OFF_Q:OFF_K], preferred_element_type=F32)
        c["qb"] = (q * (HEAD_DIM ** -0.5)).astype(BF16)
        kv = jnp.dot(h, win_ref[:, OFF_K:OFF_GA], preferred_element_type=F32)
        c["k"] = kv[:, 0:KV_WIDTH]
        c["v"] = kv[:, KV_WIDTH:2 * KV_WIDTH]

    def pool_sums(c):
        ro, u = c["ro"], c["u"]
        ubuf[HIST + ro:HIST + ro + sub, :] = u
        pos = t * tm + ro + lax.broadcasted_iota(jnp.int32, (sub, 1), 0)
        cur = ubuf[ro:ro + HIST + sub, :]
        d, w = [], 1
        for g, wg in enumerate(POOL_WINDOWS):
            while w < wg:
                cur = cur + pltpu.roll(cur, w, axis=0)
                w *= 2
            ug = u[:, g * POOL_GROUP:(g + 1) * POOL_GROUP]
            cnt = jnp.minimum(wg, pos + 1).astype(F32)
            d.append((cur[HIST:, 0:POOL_GROUP] / cnt - ug).astype(BF16))
            if g + 1 < len(POOL_WINDOWS):
                cur = cur[:, POOL_GROUP:]
        c["d"] = d

    def kv_store(c):
        r = ATTN_BLOCK + c["ro"]
        zero = jnp.zeros((sub, LANES), BF16)
        for x, xa, xb in ((c["k"], ka, kb), (c["v"], va, vb)):
            x16 = x.astype(BF16)
            xr16 = pltpu.roll(x, HEAD_DIM, axis=1).astype(BF16)
            xa[0, r:r + sub, :] = jnp.where(low, x16, zero)
            xb[0, r:r + sub, :] = jnp.where(low, zero, xr16)
            xa[1, r:r + sub, :] = jnp.where(low, xr16, zero)
            xb[1, r:r + sub, :] = jnp.where(low, zero, x16)

    def pool_proj(c):
        c["br_a"] = jnp.dot(jnp.concatenate(c["d"], axis=1), wpa_ref[...],
                            preferred_element_type=F32)

    def units(c):
        j0 = c["ro"] // ATTN_BLOCK
        return [(j, hk, pr) for j in range(j0, j0 + sub // ATTN_BLOCK)
                for hk in range(N_KV_HEADS) for pr in range(GQA_GROUP // 2)]

    def scores(c):
        sc = {}
        for (j, hk, pr) in units(c):
            r0 = j * ATTN_BLOCK
            c0 = (hk * (GQA_GROUP // 2) + pr) * LANES
            q2 = c["qb"][r0 - c["ro"]:r0 - c["ro"] + ATTN_BLOCK, c0:c0 + LANES]
            ks = (ka[hk, r0:r0 + 2 * ATTN_BLOCK, :], kb[hk, r0:r0 + 2 * ATTN_BLOCK, :])
            for e in range(2):
                sc[(j, hk, pr, e)] = lax.dot_general(q2, ks[e], nt_dims,
                                                     preferred_element_type=F32)
        c["s"] = sc

    def gate_a(c):
        c["ga"] = jnp.dot(c["h"], win_ref[:, OFF_GA:OFF_GB], preferred_element_type=F32)

    def gate_b(c):
        c["gb"] = jnp.dot(c["h"], win_ref[:, OFF_GB:IN_WIDTH], preferred_element_type=F32)

    def softmax(c):
        probs, dens = {}, {}
        for (j, hk, pr) in units(c):
            for e in range(2):
                head = hk * GQA_GROUP + pr * 2 + e
                if j == 0:
                    bias = jnp.where(t == 0, bias_ref[1, head], bias_ref[0, head])
                else:
                    bias = bias_ref[0, head]
                s = c["s"][(j, hk, pr, e)] + bias
                sink = sinks_ref[layer, head]
                m = jnp.maximum(jnp.max(s, axis=-1, keepdims=True), sink)
                p = jnp.exp(s - m)
                dens[(j, hk, pr, e)] = jnp.sum(p, axis=-1, keepdims=True) + jnp.exp(sink - m)
                probs[(j, hk, pr, e)] = p.astype(BF16)
        c["p"], c["den"] = probs, dens

    def values(c):
        for (j, hk, pr) in units(c):
            r0 = j * ATTN_BLOCK
            c0 = (hk * (GQA_GROUP // 2) + pr) * LANES
            vs = (va[hk, r0:r0 + 2 * ATTN_BLOCK, :], vb[hk, r0:r0 + 2 * ATTN_BLOCK, :])
            o2 = (jnp.dot(c["p"][(j, hk, pr, 0)], vs[0], preferred_element_type=F32)
                  + jnp.dot(c["p"][(j, hk, pr, 1)], vs[1], preferred_element_type=F32))
            den = jnp.where(low_q, c["den"][(j, hk, pr, 0)], c["den"][(j, hk, pr, 1)])
            obuf[r0:r0 + ATTN_BLOCK, c0:c0 + LANES] = o2 / den

    def tail(c):
        ro = c["ro"]
        br_b = _bdot(obuf[ro:ro + sub, :], wb_ref[...])
        merged = jax.nn.sigmoid(c["ga"]) * c["br_a"] + jax.nn.sigmoid(c["gb"]) * br_b
        xo_ref[ro:ro + sub, :] = c["x"] + gt1 * _bdot(merged, wout_ref[...])

    stages = [
        (norm,),
        (proj, gate_a),
        (pool_sums, kv_store),
        (pool_proj, scores, gate_b),
        (softmax,),
        (values,),
        (tail,),
    ]
    order = sorted((2 * s + 3 * i, i, s) for i in range(len(st)) for s in range(len(stages)))
    for _, i, s in order:
        for fn in stages[s]:
            fn(st[i])

    @pl.when(t == nt - 1)
    def _state():
        nk_ref[...] = st[-1]["k"][sub - WINDOW:, :].T
        nv_ref[...] = st[-1]["v"][sub - WINDOW:, :].T
        npool_ref[...] = ubuf[HIST + tm - POOL_PAD:HIST + tm, :]

    ubuf[0:HIST, :] = ubuf[tm:tm + HIST, :]
    for buf in (ka, kb, va, vb):
        buf[:, 0:ATTN_BLOCK, :] = buf[:, tm:tm + ATTN_BLOCK, :]


def _pmix_call(x, mod_p, g1, win, wpa, wb, wout, sinks, bias, l, cvt_jobs):
    b, t, _ = x.shape
    tm = MIX_ROWS
    nt = t // tm
    cvt_in, cvt_out, cvt_shapes = _convert_specs(cvt_jobs, b * nt, lambda i, j, s: i * nt + j)
    grid_spec = pltpu.PrefetchScalarGridSpec(
        num_scalar_prefetch=1,
        grid=(b, nt),
        in_specs=[
            pl.BlockSpec((None, tm, D_MODEL), lambda i, j, s: (i, j, 0)),
            pl.BlockSpec((None, b, MOD_SLAB), lambda i, j, s: (l, 0, 0),
                         pipeline_mode=pl.Buffered(1)),
            _layer_spec((1, D_MODEL), l),
            _weight_spec(win, (D_MODEL, IN_WIDTH), l),
            _weight_spec(wpa, (POOL_WIDTH, D_MODEL), l),
            _weight_spec(wb, (ATTN_WIDTH, D_MODEL), l),
            _weight_spec(wout, (D_MODEL, D_MODEL), l),
            _const_spec((2, N_HEADS, ATTN_BLOCK, 2 * ATTN_BLOCK)),
        ] + cvt_in,
        out_specs=[
            pl.BlockSpec((None, tm, D_MODEL), lambda i, j, s: (i, j, 0)),
            pl.BlockSpec((None, KV_WIDTH, WINDOW), lambda i, j, s: (i, 0, 0)),
            pl.BlockSpec((None, KV_WIDTH, WINDOW), lambda i, j, s: (i, 0, 0)),
            pl.BlockSpec((None, POOL_PAD, POOL_WIDTH), lambda i, j, s: (i, 0, 0)),
        ] + cvt_out,
        scratch_shapes=[
            pltpu.VMEM((HIST + tm, POOL_WIDTH), F32),
            pltpu.VMEM((N_KV_HEADS, ATTN_BLOCK + tm, LANES), BF16),
            pltpu.VMEM((N_KV_HEADS, ATTN_BLOCK + tm, LANES), BF16),
            pltpu.VMEM((N_KV_HEADS, ATTN_BLOCK + tm, LANES), BF16),
            pltpu.VMEM((N_KV_HEADS, ATTN_BLOCK + tm, LANES), BF16),
            pltpu.VMEM((tm, ATTN_WIDTH), F32),
        ],
    )
    return pl.pallas_call(
        functools.partial(_pmix_kernel, layer=l, n_cvt=len(cvt_jobs)),
        grid_spec=grid_spec,
        out_shape=[
            jax.ShapeDtypeStruct((b, t, D_MODEL), F32),
            jax.ShapeDtypeStruct((b, KV_WIDTH, WINDOW), F32),
            jax.ShapeDtypeStruct((b, KV_WIDTH, WINDOW), F32),
            jax.ShapeDtypeStruct((b, POOL_PAD, POOL_WIDTH), F32),
        ] + cvt_shapes,
        compiler_params=pltpu.CompilerParams(
            dimension_semantics=("arbitrary", "arbitrary"),
            vmem_limit_bytes=VMEM_LIMIT),
        name="prompt_mixer",
    )(sinks, x, mod_p, g1, win, wpa, wb, wout, bias, *[w for w, _ in cvt_jobs])


def _ffn_kernel(xp_ref, xs_ref, modp_ref, mods_ref, g2_ref, wup_ref, wdn_ref, fg_ref,
                so_ref, ma_ref, sgb_ref, gt1_ref, wb_ref, wout_ref,
                *rest, final, n_prompt, tiles_per_row, reps, n_cvt):
    cvt_in, (op_ref, os_ref), cvt_out = rest[:n_cvt], rest[n_cvt:n_cvt + 2], rest[n_cvt + 2:]
    _convert_slabs(cvt_in, cvt_out)
    i = pl.program_id(0)

    def ffn(load, n_rows, o_ref, mod):
        sh2, sc2, gt2 = (mod[:, k * D_MODEL:(k + 1) * D_MODEL] for k in range(3))
        subs = range(0, n_rows, FFN_SUB_ROWS)
        rows = lambda a, r: a if a.shape[0] == 1 else a[r:r + FFN_SUB_ROWS]
        norm = lambda r: _rms_mod(load(r), g2_ref[...], rows(sc2, r), rows(sh2, r)).astype(BF16)
        h2 = norm(0)
        for r in subs:
            ff = jnp.dot(h2, wup_ref[...], preferred_element_type=F32)
            if r + FFN_SUB_ROWS in subs:
                h2 = norm(r + FFN_SUB_ROWS)
            ff = jnp.square(jnp.maximum(ff, 0.0))
            y = load(r) + rows(gt2, r) * _bdot(ff, wdn_ref[...])
            if final:
                ms = jnp.mean(y * y, axis=-1, keepdims=True)
                y = y * lax.rsqrt(ms + RMS_EPS) * fg_ref[...]
            if len(o_ref.shape) == 3:
                o_ref[...] = jnp.swapaxes(y.reshape(reps, y.shape[0] // reps, D_MODEL), 0, 1)
            else:
                o_ref[r:r + FFN_SUB_ROWS, :] = y

    @pl.when(i < n_prompt)
    def _prompt():
        ffn(lambda r: xp_ref[r:r + FFN_SUB_ROWS, :], xp_ref.shape[0], op_ref,
            modp_ref[pl.ds(i // tiles_per_row, 1), :])

    @pl.when(i == n_prompt)
    def _sample():
        x = xs_ref[...]
        if x.ndim == 3:
            x = jnp.swapaxes(x, 0, 1).reshape(ma_ref.shape)
        gt1 = jnp.concatenate([gt1_ref[...]] * reps, axis=0)
        so = jnp.swapaxes(so_ref[...], 0, 1).reshape(ma_ref.shape[0], ATTN_WIDTH)
        br_b = _bdot(so, wb_ref[...])
        merged = ma_ref[...] + sgb_ref[...] * br_b
        x = x + gt1 * _bdot(merged, wout_ref[...])
        ffn(lambda r: x[r:r + FFN_SUB_ROWS, :], x.shape[0], os_ref,
            jnp.concatenate([mods_ref[...]] * reps, axis=0))


def _ffn_call(xp, xs, so, ma, sgb, mod_p, mod_s, g2, wup, wdn, wb, wout, fg, l, final, cvt_jobs):
    b, t, _ = xp.shape
    ms = ma.shape[0]
    nb = mod_s.shape[1]
    tm = FFN_ROWS
    tpr = t // tm
    n_prompt = b * tpr
    last = n_prompt - 1

    def p_idx(i):
        ii = jnp.minimum(i, last)
        return (ii // tpr, ii % tpr, 0)

    assert ms <= FFN_SUB_ROWS
    s_shape = (nb, ms // nb, D_MODEL) if final else (ms, D_MODEL)
    cvt_in, cvt_out, cvt_shapes = _convert_specs(cvt_jobs, n_prompt, lambda i: jnp.minimum(i, last))
    return pl.pallas_call(
        functools.partial(_ffn_kernel, final=final, n_prompt=n_prompt, tiles_per_row=tpr,
                          reps=ms // nb, n_cvt=len(cvt_jobs)),
        grid=(n_prompt + 1,),
        in_specs=[
            pl.BlockSpec((None, tm, D_MODEL), p_idx),
            _const_spec(xs.shape),
            pl.BlockSpec((None, b, MOD_SLAB), lambda i: (l, 0, 1), pipeline_mode=pl.Buffered(1)),
            pl.BlockSpec((None, nb, MOD_SLAB), lambda i: (l, 0, 1), pipeline_mode=pl.Buffered(1)),
            _layer_spec((1, D_MODEL), l),
            _weight_spec(wup, (D_MODEL, D_FF), l),
            _weight_spec(wdn, (D_FF, D_MODEL), l),
            _const_spec((1, D_MODEL)),
            _const_spec(so.shape), _const_spec(ma.shape), _const_spec(sgb.shape),
            pl.BlockSpec((None, nb, D_MODEL), lambda i: (l, 0, 2), pipeline_mode=pl.Buffered(1)),
            _weight_spec(wb, (ATTN_WIDTH, D_MODEL), l),
            _weight_spec(wout, (D_MODEL, D_MODEL), l),
        ] + cvt_in,
        out_specs=[
            pl.BlockSpec((None, tm, D_MODEL), p_idx),
            _whole_out_spec(s_shape),
        ] + cvt_out,
        out_shape=[jax.ShapeDtypeStruct((b, t, D_MODEL), F32),
                   jax.ShapeDtypeStruct(s_shape, F32)] + cvt_shapes,
        compiler_params=pltpu.CompilerParams(
            dimension_semantics=("arbitrary",),
            vmem_limit_bytes=VMEM_LIMIT),
        name="ffn_final" if final else "ffn",
    )(xp, xs, mod_p, mod_s, g2, wup, wdn, fg, so, ma, sgb, mod_s, wb, wout,
      *[w for w, _ in cvt_jobs])


def _sproj_kernel(x_ref, mod_ref, g1_ref, win_ref, wpa_ref, pre_ref,
                  q_ref, k_ref, v_ref, ma_ref, sgb_ref, npool_ref, *, nb, tn):
    x = x_ref[...]
    if x.ndim == 3:
        x = jnp.swapaxes(x, 0, 1).reshape(tn * nb, D_MODEL)
    mod = jnp.concatenate([mod_ref[:, 0:2 * D_MODEL]] * tn, axis=0)
    sh1 = mod[:, 0:D_MODEL]
    sc1 = mod[:, D_MODEL:2 * D_MODEL]
    h = _rms_mod(x, g1_ref[...], sc1, sh1).astype(BF16)

    u = jnp.dot(h, win_ref[:, OFF_U:OFF_Q], preferred_element_type=F32)
    def up(r):
        if r < POOL_PAD:
            return pre_ref[r]
        r -= POOL_PAD
        return u[r * nb:(r + 1) * nb, :]

    d_rows = []
    for tt in range(tn):
        parts = []
        for g, w in enumerate(POOL_WINDOWS):
            lo = g * POOL_GROUP
            acc = up(POOL_PAD + tt)[:, lo:lo + POOL_GROUP]
            for s in range(1, w):
                acc = acc + up(POOL_PAD + tt - s)[:, lo:lo + POOL_GROUP]
            cnt = float(min(w, PAST_LEN + tt + 1))
            parts.append(acc / cnt - up(POOL_PAD + tt)[:, lo:lo + POOL_GROUP])
        d_rows.append(parts)
    d = jnp.concatenate([jnp.concatenate(parts, axis=1) for parts in d_rows], axis=0)
    br_a = _bdot(d, wpa_ref[...])
    for r in range(POOL_PAD):
        npool_ref[r] = up(r + tn)

    q = jnp.dot(h, win_ref[:, OFF_Q:OFF_K], preferred_element_type=F32)
    q_ref[...] = jnp.swapaxes((q * (HEAD_DIM ** -0.5)).reshape(tn, nb, ATTN_WIDTH), 0, 1)
    kv = jnp.dot(h, win_ref[:, OFF_K:OFF_GA], preferred_element_type=F32)
    k_ref[...] = kv[:, 0:KV_WIDTH]
    v_ref[...] = kv[:, KV_WIDTH:2 * KV_WIDTH]
    ga = jnp.dot(h, win_ref[:, OFF_GA:OFF_GB], preferred_element_type=F32)
    ma_ref[...] = jax.nn.sigmoid(ga) * br_a
    gb = jnp.dot(h, win_ref[:, OFF_GB:IN_WIDTH], preferred_element_type=F32)
    sgb_ref[...] = jax.nn.sigmoid(gb)


def _sproj_call(x, mod_s, g1, win, wpa, prefix_t, l, nb, tn):
    m = tn * nb
    shapes = [
        jax.ShapeDtypeStruct((nb, tn, ATTN_WIDTH), F32),
        jax.ShapeDtypeStruct((m, KV_WIDTH), F32),
        jax.ShapeDtypeStruct((m, KV_WIDTH), F32),
        jax.ShapeDtypeStruct((m, D_MODEL), F32),
        jax.ShapeDtypeStruct((m, D_MODEL), F32),
        jax.ShapeDtypeStruct((POOL_PAD, nb, POOL_WIDTH), F32),
    ]
    return pl.pallas_call(
        functools.partial(_sproj_kernel, nb=nb, tn=tn),
        grid=(1,),
        in_specs=[
            _const_spec(x.shape),
            pl.BlockSpec((None, nb, MOD_SLAB), lambda i: (l, 0, 0), pipeline_mode=pl.Buffered(1)),
            _layer_spec((1, D_MODEL), l),
            _weight_spec(win, (D_MODEL, IN_WIDTH), l),
            _weight_spec(wpa, (POOL_WIDTH, D_MODEL), l),
            _layer_spec((POOL_PAD, nb, POOL_WIDTH), l),
        ],
        out_specs=[_whole_out_spec(s.shape) for s in shapes],
        out_shape=shapes,
        compiler_params=pltpu.CompilerParams(
            dimension_semantics=("arbitrary",),
            vmem_limit_bytes=VMEM_LIMIT),
        name="sample_proj",
    )(x, mod_s, g1, win, wpa, prefix_t)


def _sample_bias(tn):
    t = np.arange(tn)[:, None]
    lane = np.arange(2 * WINDOW)[None, :]
    cached = lane < WINDOW
    new_t = lane - (2 * WINDOW - tn)
    dist = np.where(cached, t + WINDOW - lane, t - new_t)
    valid = np.where(cached, (dist >= 0) & (dist < WINDOW), (new_t >= 0) & (dist >= 0))
    sl = _alibi_slopes()[:, None, None]
    return np.where(valid[None], -sl * dist[None], NEG_INF).astype(np.float32)


def _sattn_kernel(sinks_ref, q_ref, kc_ref, vc_ref, kn_ref, vn_ref, bias_ref, *rest,
                   layer, nb, tn, bb):
    if layer:
        pk_ref, pv_ref = rest[:2]
        rest = rest[2:]
    o_ref, nk_ref, nv_ref, wk, wv = rest
    i = pl.program_id(0)
    rows = GQA_GROUP * tn
    keep = WINDOW - tn

    @pl.when(i == 0)
    def _new_rows():
        for c0 in range(0, nb * tn, LANES):
            wk[:, c0:c0 + LANES] = kn_ref[c0:c0 + LANES, :].T
            wv[:, c0:c0 + LANES] = vn_ref[c0:c0 + LANES, :].T

    if layer:
        nk_ref[0:layer] = pk_ref[...]
        nv_ref[0:layer] = pv_ref[...]

    lane0 = i * (bb * tn)
    tile0 = pl.multiple_of((lane0 // LANES) * LANES, LANES)
    off0 = lane0 % LANES
    wkt = wk[:, pl.ds(tile0, LANES)]
    wvt = wv[:, pl.ds(tile0, LANES)]
    lane = lax.broadcasted_iota(jnp.int32, (HEAD_DIM, WINDOW), 1)
    tail = lane >= keep
    grow = lax.broadcasted_iota(jnp.int32, (rows, 1), 0) // tn
    units = [(bl, hk) for bl in range(bb) for hk in range(N_KV_HEADS)]
    biases, sinks = [], []
    for hk in range(N_KV_HEADS):
        biases.append(jnp.concatenate(
            [bias_ref[hk * GQA_GROUP + g] for g in range(GQA_GROUP)], axis=0))
        sink = jnp.zeros((rows, 1), F32)
        for g in range(GQA_GROUP):
            sink = jnp.where(grow == g, sinks_ref[layer, hk * GQA_GROUP + g], sink)
        sinks.append(sink)

    scores, values = {}, {}
    for bl in range(bb):
        shift = (keep - off0 - bl * tn) % LANES
        nkb = pltpu.roll(wkt, shift, axis=1)
        nvb = pltpu.roll(wvt, shift, axis=1)
        for hk in range(N_KV_HEADS):
            kt = kc_ref[bl, hk]
            vt = vc_ref[bl, hk]
            nkt = jnp.where(tail, nkb[hk * HEAD_DIM:(hk + 1) * HEAD_DIM, :], 0.0)
            nvt = jnp.where(tail, nvb[hk * HEAD_DIM:(hk + 1) * HEAD_DIM, :], 0.0)
            nk_ref[layer, bl, hk] = jnp.where(tail, nkt, pltpu.roll(kt, keep, axis=1))
            nv_ref[layer, bl, hk] = jnp.where(tail, nvt, pltpu.roll(vt, keep, axis=1))
            keys = jnp.concatenate([kt, nkt], axis=1).astype(BF16)
            values[(bl, hk)] = jnp.concatenate([vt, nvt], axis=1).astype(BF16)
            qb = q_ref[bl]
            qu = jnp.concatenate(
                [qb[:, (hk * GQA_GROUP + g) * HEAD_DIM:(hk * GQA_GROUP + g + 1) * HEAD_DIM]
                 for g in range(GQA_GROUP)], axis=0).astype(BF16)
            scores[(bl, hk)] = jnp.dot(qu, keys, preferred_element_type=F32)
    probs, dens = {}, {}
    for (bl, hk) in units:
        s = scores[(bl, hk)] + biases[hk]
        m = jnp.maximum(jnp.max(s, axis=-1, keepdims=True), sinks[hk])
        p = jnp.exp(s - m)
        dens[(bl, hk)] = jnp.sum(p, axis=-1, keepdims=True) + jnp.exp(sinks[hk] - m)
        probs[(bl, hk)] = p.astype(BF16)
    for (bl, hk) in units:
        o = lax.dot_general(probs[(bl, hk)], values[(bl, hk)], (((1,), (1,)), ((), ())),
                            preferred_element_type=F32)
        o = o / dens[(bl, hk)]
        gw = GQA_GROUP * HEAD_DIM
        o_ref[bl, :, hk * gw:(hk + 1) * gw] = jnp.concatenate(
            [o[g * tn:(g + 1) * tn, :] for g in range(GQA_GROUP)], axis=1)


def _sattn_call(q4, kn_bt, vn_bt, ck, cv, sinks, rolled, l, nb, tn):
    bb = SATTN_BATCH
    unit = (bb, N_KV_HEADS, HEAD_DIM, WINDOW)
    cache_shape = jax.ShapeDtypeStruct((l + 1, nb) + unit[1:], F32)
    in_specs = [
        pl.BlockSpec((bb, tn, ATTN_WIDTH), lambda i, s: (i, 0, 0)),
        pl.BlockSpec((None,) + unit, lambda i, s: (l, i, 0, 0, 0)),
        pl.BlockSpec((None,) + unit, lambda i, s: (l, i, 0, 0, 0)),
        _const_spec((nb * tn, KV_WIDTH)),
        _const_spec((nb * tn, KV_WIDTH)),
        _const_spec((N_HEADS, tn, 2 * WINDOW)),
    ]
    args = [sinks, q4, ck, cv, kn_bt, vn_bt, jnp.asarray(_sample_bias(tn))]
    if l:
        in_specs += [pl.BlockSpec((l,) + unit, lambda i, s: (0, i, 0, 0, 0))] * 2
        args += list(rolled)
    grid_spec = pltpu.PrefetchScalarGridSpec(
        num_scalar_prefetch=1,
        grid=(nb // bb,),
        in_specs=in_specs,
        out_specs=[
            pl.BlockSpec((bb, tn, ATTN_WIDTH), lambda i, s: (i, 0, 0)),
            pl.BlockSpec((l + 1,) + unit, lambda i, s: (0, i, 0, 0, 0)),
            pl.BlockSpec((l + 1,) + unit, lambda i, s: (0, i, 0, 0, 0)),
        ],
        scratch_shapes=[
            pltpu.VMEM((KV_WIDTH, nb * tn), F32),
            pltpu.VMEM((KV_WIDTH, nb * tn), F32),
        ],
    )
    return pl.pallas_call(
        functools.partial(_sattn_kernel, layer=l, nb=nb, tn=tn, bb=bb),
        grid_spec=grid_spec,
        out_shape=[jax.ShapeDtypeStruct((nb, tn, ATTN_WIDTH), F32), cache_shape, cache_shape],
        compiler_params=pltpu.CompilerParams(
            dimension_semantics=("arbitrary",),
            vmem_limit_bytes=VMEM_LIMIT),
        name="sample_attn",
    )(*args)


def kernel(x_prompt, x_sample, cache_k, cache_v, state_pool, c_prompt, c_sample,
           w_ada, b_ada, norm1_g, w_in, w_pool, pool_scale, attn_sinks, w_a, w_b,
           w_out, norm2_g, w_up, w_down, final_g):
    bp, tp, _ = x_prompt.shape
    nb, tn, _ = x_sample.shape
    assert tp % MIX_ROWS == 0 and tp % FFN_ROWS == 0 and tn <= 8 and nb == LANES
    assert bp % 8 == 0

    mod_p, mod_s, *mix_w = _mod_call(c_prompt, c_sample, w_ada, b_ada,
                                     [(w, 0) for w in (w_in, w_b, w_out)])
    wpa = _poolw_call(w_pool, pool_scale.reshape(DEPTH, 1, POOL_WIDTH), w_a)
    bias = jnp.asarray(_prompt_bias())
    fg = final_g.reshape(1, D_MODEL)
    g1 = norm1_g.reshape(DEPTH, 1, D_MODEL)
    g2 = norm2_g.reshape(DEPTH, 1, D_MODEL)

    xs = x_sample
    prefix_t = state_pool.transpose(0, 2, 1, 3)
    ck = cache_k.transpose(0, 1, 3, 4, 2)
    cv = cache_v.transpose(0, 1, 3, 4, 2)

    xp = x_prompt
    kp, vp, pp, ps, rolled = [], [], [], [], None
    for l in range(DEPTH):
        last = l == DEPTH - 1
        win, wb, wout = mix_w
        xp, nk, nv, npool, wup, wdn = _pmix_call(xp, mod_p, g1, win, wpa, wb, wout,
                                                 attn_sinks, bias, l, [(w_up, l), (w_down, l)])
        kp.append(nk); vp.append(nv); pp.append(npool)

        q, kn, vn, ma, sgb, npool_s = _sproj_call(xs, mod_s, g1, win, wpa, prefix_t, l, nb, tn)
        by_batch = lambda a: a.reshape(tn, nb, KV_WIDTH).transpose(1, 0, 2).reshape(nb * tn, KV_WIDTH)
        o, *rolled = _sattn_call(q, by_batch(kn), by_batch(vn), ck, cv, attn_sinks, rolled,
                                 l, nb, tn)
        ps.append(npool_s)

        nxt = [] if last else [(w, l + 1) for w in (w_in, w_b, w_out)]
        xp, xs, *mix_w = _ffn_call(xp, xs, o, ma, sgb, mod_p, mod_s, g2, wup, wdn, wb, wout, fg,
                                   l, last, nxt)

    kv_shape_p = (DEPTH, bp, N_KV_HEADS, HEAD_DIM, WINDOW)
    return (xp,
            xs,
            jnp.stack(kp).reshape(kv_shape_p).transpose(0, 1, 4, 2, 3),
            jnp.stack(vp).reshape(kv_shape_p).transpose(0, 1, 4, 2, 3),
            jnp.stack(pp),
            rolled[0].transpose(0, 1, 4, 2, 3),
            rolled[1].transpose(0, 1, 4, 2, 3),
            jnp.stack(ps).transpose(0, 2, 1, 3))
```

```python
import functools

import numpy as np
import jax
import jax.numpy as jnp
from jax import lax
from jax.experimental import pallas as pl
from jax.experimental.pallas import tpu as pltpu

D_MODEL = 1024
DEPTH = 2
PAST_LEN = 16384
POOL_WIDTH = D_MODEL // 2
POOL_WINDOWS = (2, 4, 8, 16)
POOL_GROUP = POOL_WIDTH // len(POOL_WINDOWS)
POOL_PAD = max(POOL_WINDOWS) - 1
N_HEADS = 8
N_KV_HEADS = 2
HEAD_DIM = 64
GQA_GROUP = N_HEADS // N_KV_HEADS
ATTN_WIDTH = N_HEADS * HEAD_DIM
KV_WIDTH = N_KV_HEADS * HEAD_DIM
WINDOW = 128
ATTN_BLOCK = 128
D_FF = 4 * D_MODEL
RMS_EPS = 1e-6
NEG_INF = -1e30

OFF_U = 0
OFF_Q = OFF_U + POOL_WIDTH
OFF_K = OFF_Q + ATTN_WIDTH
OFF_V = OFF_K + KV_WIDTH
OFF_GA = OFF_V + KV_WIDTH
OFF_GB = OFF_GA + D_MODEL
IN_WIDTH = OFF_GB + D_MODEL
MOD_SLAB = 3 * D_MODEL

LANES = 128
HIST = 16
assert all(w & (w - 1) == 0 for w in POOL_WINDOWS) and list(POOL_WINDOWS) == sorted(POOL_WINDOWS)
assert HIST >= POOL_PAD
VMEM_LIMIT = 56 * 1024 * 1024
SUB_ROWS = 512
MIX_ROWS = 2 * SUB_ROWS
FFN_SUB_ROWS = 512
FFN_ROWS = 2 * FFN_SUB_ROWS
SATTN_BATCH = 16

F32 = jnp.float32
BF16 = jnp.bfloat16


def _bdot(a, b):
    return jnp.dot(a.astype(BF16), b.astype(BF16), preferred_element_type=F32)


def _rms_mod(x, g, sc, sh):
    ms = jnp.mean(x * x, axis=-1, keepdims=True)
    return (x * lax.rsqrt(ms + RMS_EPS) * g) * (1.0 + sc) + sh


def _alibi_slopes():
    return 2.0 ** (-8.0 * (np.arange(N_HEADS) + 1) / N_HEADS)


def _const_spec(shape):
    nd = len(shape)
    return pl.BlockSpec(shape, lambda *_: (0,) * nd, pipeline_mode=pl.Buffered(1))


def _layer_spec(shape, l):
    nd = len(shape)
    return pl.BlockSpec((None,) + tuple(shape), lambda *_: (l,) + (0,) * nd,
                        pipeline_mode=pl.Buffered(1))


def _whole_out_spec(shape):
    nd = len(shape)
    return pl.BlockSpec(shape, lambda *_: (0,) * nd)


def _weight_spec(w, shape, l):
    return _const_spec(shape) if w.shape == tuple(shape) else _layer_spec(shape, l)


def _convert_specs(jobs, n_steps, step_of):
    in_specs, out_specs, out_shapes = [], [], []
    for w, l in jobs:
        _, r, c = w.shape
        rows = r // n_steps
        assert rows * n_steps == r and rows % 16 == 0, (w.shape, n_steps)
        in_specs.append(pl.BlockSpec((None, rows, c), lambda *a, l=l: (l, step_of(*a), 0)))
        out_specs.append(pl.BlockSpec((rows, c), lambda *a: (step_of(*a), 0)))
        out_shapes.append(jax.ShapeDtypeStruct((r, c), BF16))
    return in_specs, out_specs, out_shapes


def _convert_slabs(src_refs, dst_refs):
    for src, dst in zip(src_refs, dst_refs):
        dst[...] = src[...].astype(BF16)


def _mod_kernel(cp_ref, cs_ref, w_ref, b_ref, *rest, n_cvt):
    cvt_in, (op_ref, os_ref), cvt_out = rest[:n_cvt], rest[n_cvt:n_cvt + 2], rest[n_cvt + 2:]
    _convert_slabs(cvt_in, cvt_out)
    w = w_ref[...].astype(BF16)
    b = b_ref[...]
    for c_ref, o_ref in ((cp_ref, op_ref), (cs_ref, os_ref)):
        c = c_ref[...]
        s = (c * jax.nn.sigmoid(c)).astype(BF16)
        o_ref[...] = jnp.dot(s, w, preferred_element_type=F32) + b


def _mod_call(c_p, c_s, w_ada, b_ada, cvt_jobs):
    tn = 1536
    n = 6 * D_MODEL
    nj = n // tn
    mp, ms = c_p.shape[0], c_s.shape[0]
    cvt_in, cvt_out, cvt_shapes = _convert_specs(cvt_jobs, DEPTH * nj, lambda l, j: l * nj + j)
    return pl.pallas_call(
        functools.partial(_mod_kernel, n_cvt=len(cvt_jobs)),
        grid=(DEPTH, nj),
        in_specs=[
            _const_spec((mp, D_MODEL)),
            _const_spec((ms, D_MODEL)),
            pl.BlockSpec((None, D_MODEL, tn), lambda l, j: (l, 0, j)),
            pl.BlockSpec((None, 1, tn), lambda l, j: (l, 0, j)),
        ] + cvt_in,
        out_specs=[
            pl.BlockSpec((None, mp, tn), lambda l, j: (l, 0, j)),
            pl.BlockSpec((None, ms, tn), lambda l, j: (l, 0, j)),
        ] + cvt_out,
        out_shape=[jax.ShapeDtypeStruct((DEPTH, mp, n), F32),
                   jax.ShapeDtypeStruct((DEPTH, ms, n), F32)] + cvt_shapes,
        compiler_params=pltpu.CompilerParams(
            dimension_semantics=("arbitrary", "arbitrary"),
            vmem_limit_bytes=VMEM_LIMIT),
        name="adaln_mod",
    )(c_p, c_s, w_ada, b_ada.reshape(DEPTH, 1, n), *[w for w, _ in cvt_jobs])


def _poolw_kernel(wp_ref, ps_ref, wa_ref, o_ref):
    for g in range(len(POOL_WINDOWS)):
        rows = slice(g * POOL_GROUP, (g + 1) * POOL_GROUP)
        o_ref[rows, :] = jnp.dot(wp_ref[g] * ps_ref[:, rows], wa_ref[rows, :],
                                 preferred_element_type=F32,
                                 precision=lax.Precision.HIGHEST).astype(BF16)


def _poolw_call(w_pool, pool_scale, w_a):
    return pl.pallas_call(
        _poolw_kernel,
        grid=(DEPTH,),
        in_specs=[
            pl.BlockSpec((None,) + w_pool.shape[1:], lambda l: (l, 0, 0, 0)),
            pl.BlockSpec((None, 1, POOL_WIDTH), lambda l: (l, 0, 0)),
            pl.BlockSpec((None, POOL_WIDTH, D_MODEL), lambda l: (l, 0, 0)),
        ],
        out_specs=pl.BlockSpec((None, POOL_WIDTH, D_MODEL), lambda l: (l, 0, 0)),
        out_shape=jax.ShapeDtypeStruct((DEPTH, POOL_WIDTH, D_MODEL), BF16),
        compiler_params=pltpu.CompilerParams(
            dimension_semantics=("arbitrary",),
            vmem_limit_bytes=VMEM_LIMIT),
        name="pool_weights",
    )(w_pool, pool_scale, w_a)


def _prompt_bias():
    i = np.arange(ATTN_BLOCK)[:, None]
    j = np.arange(2 * ATTN_BLOCK)[None, :]
    dist = i + ATTN_BLOCK - j
    valid = (dist >= 0) & (dist < WINDOW)
    valid_first = valid & (j >= ATTN_BLOCK)
    sl = _alibi_slopes()[:, None, None]
    b = np.where(valid[None], -sl * dist[None], NEG_INF)
    b0 = np.where(valid_first[None], -sl * dist[None], NEG_INF)
    return np.stack([b, b0]).astype(np.float32)


def _pmix_kernel(sinks_ref, x_ref, mod_ref, g1_ref, win_ref, wpa_ref,
                 wb_ref, wout_ref, bias_ref, *rest, layer, n_cvt):
    cvt_in, rest = rest[:n_cvt], rest[n_cvt:]
    (xo_ref, nk_ref, nv_ref, npool_ref), rest = rest[:4], rest[4:]
    cvt_out, (ubuf, ka, kb, va, vb, obuf) = rest[:n_cvt], rest[n_cvt:]
    _convert_slabs(cvt_in, cvt_out)
    tm, sub = MIX_ROWS, SUB_ROWS
    bi = pl.program_id(0)
    t = pl.program_id(1)
    nt = pl.num_programs(1)

    @pl.when(t == 0)
    def _init():
        ubuf[0:HIST, :] = jnp.zeros((HIST, POOL_WIDTH), F32)
        zero = jnp.zeros((N_KV_HEADS, ATTN_BLOCK, LANES), BF16)
        ka[:, 0:ATTN_BLOCK, :] = zero
        kb[:, 0:ATTN_BLOCK, :] = zero
        va[:, 0:ATTN_BLOCK, :] = zero
        vb[:, 0:ATTN_BLOCK, :] = zero

    mod = mod_ref[pl.ds(bi, 1), :]
    sh1 = mod[:, 0:D_MODEL]
    sc1 = mod[:, D_MODEL:2 * D_MODEL]
    gt1 = mod[:, 2 * D_MODEL:3 * D_MODEL]
    lane = lax.broadcasted_iota(jnp.int32, (sub, LANES), 1)
    low = lane < HEAD_DIM
    low_q = lax.broadcasted_iota(jnp.int32, (ATTN_BLOCK, LANES), 1) < HEAD_DIM
    nt_dims = (((1,), (1,)), ((), ()))
    st = [dict(ro=i * sub) for i in range(tm // sub)]

    def norm(c):
        c["x"] = x_ref[c["ro"]:c["ro"] + sub, :]
        c["h"] = _rms_mod(c["x"], g1_ref[...], sc1, sh1).astype(BF16)

    def proj(c):
        h = c["h"]
        c["u"] = jnp.dot(h, win_ref[:, OFF_U:OFF_Q], preferred_element_type=F32)
        q = jnp.dot(h, win_ref[:, OFF_Q:OFF_K], preferred_element_type=F32)
        c["qb"] = (q * (HEAD_DIM ** -0.5)).astype(BF16)
        kv = jnp.dot(h, win_ref[:, OFF_K:OFF_GA], preferred_element_type=F32)
        c["k"] = kv[:, 0:KV_WIDTH]
        c["v"] = kv[:, KV_WIDTH:2 * KV_WIDTH]

    def pool_sums(c):
        ro, u = c["ro"], c["u"]
        ubuf[HIST + ro:HIST + ro + sub, :] = u
        pos = t * tm + ro + lax.broadcasted_iota(jnp.int32, (sub, 1), 0)
        cur = ubuf[ro:ro + HIST + sub, :]
        d, w = [], 1
        for g, wg in enumerate(POOL_WINDOWS):
            while w < wg:
                cur = cur + pltpu.roll(cur, w, axis=0)
                w *= 2
            ug = u[:, g * POOL_GROUP:(g + 1) * POOL_GROUP]
            cnt = jnp.minimum(wg, pos + 1).astype(F32)
            d.append((cur[HIST:, 0:POOL_GROUP] / cnt - ug).astype(BF16))
            if g + 1 < len(POOL_WINDOWS):
                cur = cur[:, POOL_GROUP:]
        c["d"] = d

    def kv_store(c):
        r = ATTN_BLOCK + c["ro"]
        zero = jnp.zeros((sub, LANES), BF16)
        for x, xa, xb in ((c["k"], ka, kb), (c["v"], va, vb)):
            x16 = x.astype(BF16)
            xr16 = pltpu.roll(x, HEAD_DIM, axis=1).astype(BF16)
            xa[0, r:r + sub, :] = jnp.where(low, x16, zero)
            xb[0, r:r + sub, :] = jnp.where(low, zero, xr16)
            xa[1, r:r + sub, :] = jnp.where(low, xr16, zero)
            xb[1, r:r + sub, :] = jnp.where(low, zero, x16)

    def pool_proj(c):
        c["br_a"] = jnp.dot(jnp.concatenate(c["d"], axis=1), wpa_ref[...],
                            preferred_element_type=F32)

    def units(c):
        j0 = c["ro"] // ATTN_BLOCK
        return [(j, hk, pr) for j in range(j0, j0 + sub // ATTN_BLOCK)
                for hk in range(N_KV_HEADS) for pr in range(GQA_GROUP // 2)]

    def scores(c):
        sc = {}
        for (j, hk, pr) in units(c):
            r0 = j * ATTN_BLOCK
            c0 = (hk * (GQA_GROUP // 2) + pr) * LANES
            q2 = c["qb"][r0 - c["ro"]:r0 - c["ro"] + ATTN_BLOCK, c0:c0 + LANES]
            ks = (ka[hk, r0:r0 + 2 * ATTN_BLOCK, :], kb[hk, r0:r0 + 2 * ATTN_BLOCK, :])
            for e in range(2):
                sc[(j, hk, pr, e)] = lax.dot_general(q2, ks[e], nt_dims,
                                                     preferred_element_type=F32)
        c["s"] = sc

    def gate_a(c):
        c["ga"] = jnp.dot(c["h"], win_ref[:, OFF_GA:OFF_GB], preferred_element_type=F32)

    def gate_b(c):
        c["gb"] = jnp.dot(c["h"], win_ref[:, OFF_GB:IN_WIDTH], preferred_element_type=F32)

    def softmax(c):
        probs, dens = {}, {}
        for (j, hk, pr) in units(c):
            for e in range(2):
                head = hk * GQA_GROUP + pr * 2 + e
                if j == 0:
                    bias = jnp.where(t == 0, bias_ref[1, head], bias_ref[0, head])
                else:
                    bias = bias_ref[0, head]
                s = c["s"][(j, hk, pr, e)] + bias
                sink = sinks_ref[layer, head]
                m = jnp.maximum(jnp.max(s, axis=-1, keepdims=True), sink)
                p = jnp.exp(s - m)
                dens[(j, hk, pr, e)] = jnp.sum(p, axis=-1, keepdims=True) + jnp.exp(sink - m)
                probs[(j, hk, pr, e)] = p.astype(BF16)
        c["p"], c["den"] = probs, dens

    def values(c):
        for (j, hk, pr) in units(c):
            r0 = j * ATTN_BLOCK
            c0 = (hk * (GQA_GROUP // 2) + pr) * LANES
            vs = (va[hk, r0:r0 + 2 * ATTN_BLOCK, :], vb[hk, r0:r0 + 2 * ATTN_BLOCK, :])
            o2 = (jnp.dot(c["p"][(j, hk, pr, 0)], vs[0], preferred_element_type=F32)
                  + jnp.dot(c["p"][(j, hk, pr, 1)], vs[1], preferred_element_type=F32))
            den = jnp.where(low_q, c["den"][(j, hk, pr, 0)], c["den"][(j, hk, pr, 1)])
            obuf[r0:r0 + ATTN_BLOCK, c0:c0 + LANES] = o2 / den

    def tail(c):
        ro = c["ro"]
        br_b = _bdot(obuf[ro:ro + sub, :], wb_ref[...])
        merged = jax.nn.sigmoid(c["ga"]) * c["br_a"] + jax.nn.sigmoid(c["gb"]) * br_b
        xo_ref[ro:ro + sub, :] = c["x"] + gt1 * _bdot(merged, wout_ref[...])

    stages = [
        (norm,),
        (proj, gate_a),
        (pool_sums, kv_store),
        (pool_proj, scores, gate_b),
        (softmax,),
        (values,),
        (tail,),
    ]
    order = sorted((2 * s + 3 * i, i, s) for i in range(len(st)) for s in range(len(stages)))
    for _, i, s in order:
        for fn in stages[s]:
            fn(st[i])

    @pl.when(t == nt - 1)
    def _state():
        nk_ref[...] = st[-1]["k"][sub - WINDOW:, :].T
        nv_ref[...] = st[-1]["v"][sub - WINDOW:, :].T
        npool_ref[...] = ubuf[HIST + tm - POOL_PAD:HIST + tm, :]

    ubuf[0:HIST, :] = ubuf[tm:tm + HIST, :]
    for buf in (ka, kb, va, vb):
        buf[:, 0:ATTN_BLOCK, :] = buf[:, tm:tm + ATTN_BLOCK, :]


def _pmix_call(x, mod_p, g1, win, wpa, wb, wout, sinks, bias, l, cvt_jobs):
    b, t, _ = x.shape
    tm = MIX_ROWS
    nt = t // tm
    cvt_in, cvt_out, cvt_shapes = _convert_specs(cvt_jobs, b * nt, lambda i, j, s: i * nt + j)
    grid_spec = pltpu.PrefetchScalarGridSpec(
        num_scalar_prefetch=1,
        grid=(b, nt),
        in_specs=[
            pl.BlockSpec((None, tm, D_MODEL), lambda i, j, s: (i, j, 0)),
            pl.BlockSpec((None, b, MOD_SLAB), lambda i, j, s: (l, 0, 0),
                         pipeline_mode=pl.Buffered(1)),
            _layer_spec((1, D_MODEL), l),
            _weight_spec(win, (D_MODEL, IN_WIDTH), l),
            _weight_spec(wpa, (POOL_WIDTH, D_MODEL), l),
            _weight_spec(wb, (ATTN_WIDTH, D_MODEL), l),
            _weight_spec(wout, (D_MODEL, D_MODEL), l),
            _const_spec((2, N_HEADS, ATTN_BLOCK, 2 * ATTN_BLOCK)),
        ] + cvt_in,
        out_specs=[
            pl.BlockSpec((None, tm, D_MODEL), lambda i, j, s: (i, j, 0)),
            pl.BlockSpec((None, KV_WIDTH, WINDOW), lambda i, j, s: (i, 0, 0)),
            pl.BlockSpec((None, KV_WIDTH, WINDOW), lambda i, j, s: (i, 0, 0)),
            pl.BlockSpec((None, POOL_PAD, POOL_WIDTH), lambda i, j, s: (i, 0, 0)),
        ] + cvt_out,
        scratch_shapes=[
            pltpu.VMEM((HIST + tm, POOL_WIDTH), F32),
            pltpu.VMEM((N_KV_HEADS, ATTN_BLOCK + tm, LANES), BF16),
            pltpu.VMEM((N_KV_HEADS, ATTN_BLOCK + tm, LANES), BF16),
            pltpu.VMEM((N_KV_HEADS, ATTN_BLOCK + tm, LANES), BF16),
            pltpu.VMEM((N_KV_HEADS, ATTN_BLOCK + tm, LANES), BF16),
            pltpu.VMEM((tm, ATTN_WIDTH), F32),
        ],
    )
    return pl.pallas_call(
        functools.partial(_pmix_kernel, layer=l, n_cvt=len(cvt_jobs)),
        grid_spec=grid_spec,
        out_shape=[
            jax.ShapeDtypeStruct((b, t, D_MODEL), F32),
            jax.ShapeDtypeStruct((b, KV_WIDTH, WINDOW), F32),
            jax.ShapeDtypeStruct((b, KV_WIDTH, WINDOW), F32),
            jax.ShapeDtypeStruct((b, POOL_PAD, POOL_WIDTH), F32),
        ] + cvt_shapes,
        compiler_params=pltpu.CompilerParams(
            dimension_semantics=("arbitrary", "arbitrary"),
            vmem_limit_bytes=VMEM_LIMIT),
        name="prompt_mixer",
    )(sinks, x, mod_p, g1, win, wpa, wb, wout, bias, *[w for w, _ in cvt_jobs])


def _ffn_kernel(xp_ref, xs_ref, modp_ref, mods_ref, g2_ref, wup_ref, wdn_ref, fg_ref,
                so_ref, ma_ref, sgb_ref, gt1_ref, wb_ref, wout_ref,
                *rest, final, n_prompt, tiles_per_row, reps, n_cvt):
    cvt_in, (op_ref, os_ref), cvt_out = rest[:n_cvt], rest[n_cvt:n_cvt + 2], rest[n_cvt + 2:]
    _convert_slabs(cvt_in, cvt_out)
    i = pl.program_id(0)

    def ffn(load, n_rows, o_ref, mod):
        sh2, sc2, gt2 = (mod[:, k * D_MODEL:(k + 1) * D_MODEL] for k in range(3))
        subs = range(0, n_rows, FFN_SUB_ROWS)
        rows = lambda a, r: a if a.shape[0] == 1 else a[r:r + FFN_SUB_ROWS]
        norm = lambda r: _rms_mod(load(r), g2_ref[...], rows(sc2, r), rows(sh2, r)).astype(BF16)
        h2 = norm(0)
        for r in subs:
            ff = jnp.dot(h2, wup_ref[...], preferred_element_type=F32)
            if r + FFN_SUB_ROWS in subs:
                h2 = norm(r + FFN_SUB_ROWS)
            ff = jnp.square(jnp.maximum(ff, 0.0))
            y = load(r) + rows(gt2, r) * _bdot(ff, wdn_ref[...])
            if final:
                ms = jnp.mean(y * y, axis=-1, keepdims=True)
                y = y * lax.rsqrt(ms + RMS_EPS) * fg_ref[...]
            if len(o_ref.shape) == 3:
                o_ref[...] = jnp.swapaxes(y.reshape(reps, y.shape[0] // reps, D_MODEL), 0, 1)
            else:
                o_ref[r:r + FFN_SUB_ROWS, :] = y

    @pl.when(i < n_prompt)
    def _prompt():
        ffn(lambda r: xp_ref[r:r + FFN_SUB_ROWS, :], xp_ref.shape[0], op_ref,
            modp_ref[pl.ds(i // tiles_per_row, 1), :])

    @pl.when(i == n_prompt)
    def _sample():
        x = xs_ref[...]
        if x.ndim == 3:
            x = jnp.swapaxes(x, 0, 1).reshape(ma_ref.shape)
        gt1 = jnp.concatenate([gt1_ref[...]] * reps, axis=0)
        so = jnp.swapaxes(so_ref[...], 0, 1).reshape(ma_ref.shape[0], ATTN_WIDTH)
        br_b = _bdot(so, wb_ref[...])
        merged = ma_ref[...] + sgb_ref[...] * br_b
        x = x + gt1 * _bdot(merged, wout_ref[...])
        ffn(lambda r: x[r:r + FFN_SUB_ROWS, :], x.shape[0], os_ref,
            jnp.concatenate([mods_ref[...]] * reps, axis=0))


def _ffn_call(xp, xs, so, ma, sgb, mod_p, mod_s, g2, wup, wdn, wb, wout, fg, l, final, cvt_jobs):
    b, t, _ = xp.shape
    ms = ma.shape[0]
    nb = mod_s.shape[1]
    tm = FFN_ROWS
    tpr = t // tm
    n_prompt = b * tpr
    last = n_prompt - 1

    def p_idx(i):
        ii = jnp.minimum(i, last)
        return (ii // tpr, ii % tpr, 0)

    assert ms <= FFN_SUB_ROWS
    s_shape = (nb, ms // nb, D_MODEL) if final else (ms, D_MODEL)
    cvt_in, cvt_out, cvt_shapes = _convert_specs(cvt_jobs, n_prompt, lambda i: jnp.minimum(i, last))
    return pl.pallas_call(
        functools.partial(_ffn_kernel, final=final, n_prompt=n_prompt, tiles_per_row=tpr,
                          reps=ms // nb, n_cvt=len(cvt_jobs)),
        grid=(n_prompt + 1,),
        in_specs=[
            pl.BlockSpec((None, tm, D_MODEL), p_idx),
            _const_spec(xs.shape),
            pl.BlockSpec((None, b, MOD_SLAB), lambda i: (l, 0, 1), pipeline_mode=pl.Buffered(1)),
            pl.BlockSpec((None, nb, MOD_SLAB), lambda i: (l, 0, 1), pipeline_mode=pl.Buffered(1)),
            _layer_spec((1, D_MODEL), l),
            _weight_spec(wup, (D_MODEL, D_FF), l),
            _weight_spec(wdn, (D_FF, D_MODEL), l),
            _const_spec((1, D_MODEL)),
            _const_spec(so.shape), _const_spec(ma.shape), _const_spec(sgb.shape),
            pl.BlockSpec((None, nb, D_MODEL), lambda i: (l, 0, 2), pipeline_mode=pl.Buffered(1)),
            _weight_spec(wb, (ATTN_WIDTH, D_MODEL), l),
            _weight_spec(wout, (D_MODEL, D_MODEL), l),
        ] + cvt_in,
        out_specs=[
            pl.BlockSpec((None, tm, D_MODEL), p_idx),
            _whole_out_spec(s_shape),
        ] + cvt_out,
        out_shape=[jax.ShapeDtypeStruct((b, t, D_MODEL), F32),
                   jax.ShapeDtypeStruct(s_shape, F32)] + cvt_shapes,
        compiler_params=pltpu.CompilerParams(
            dimension_semantics=("arbitrary",),
            vmem_limit_bytes=VMEM_LIMIT),
        name="ffn_final" if final else "ffn",
    )(xp, xs, mod_p, mod_s, g2, wup, wdn, fg, so, ma, sgb, mod_s, wb, wout,
      *[w for w, _ in cvt_jobs])


def _sproj_kernel(x_ref, mod_ref, g1_ref, win_ref, wpa_ref, pre_ref,
                  q_ref, k_ref, v_ref, ma_ref, sgb_ref, npool_ref, *, nb, tn):
    x = x_ref[...]
    if x.ndim == 3:
        x = jnp.swapaxes(x, 0, 1).reshape(tn * nb, D_MODEL)
    mod = jnp.concatenate([mod_ref[:, 0:2 * D_MODEL]] * tn, axis=0)
    sh1 = mod[:, 0:D_MODEL]
    sc1 = mod[:, D_MODEL:2 * D_MODEL]
    h = _rms_mod(x, g1_ref[...], sc1, sh1).astype(BF16)

    u = jnp.dot(h, win_ref[:, OFF_U:OFF_Q], preferred_element_type=F32)
    def up(r):
        if r < POOL_PAD:
            return pre_ref[r]
        r -= POOL_PAD
        return u[r * nb:(r + 1) * nb, :]

    d_rows = []
    for tt in range(tn):
        parts = []
        for g, w in enumerate(POOL_WINDOWS):
            lo = g * POOL_GROUP
            acc = up(POOL_PAD + tt)[:, lo:lo + POOL_GROUP]
            for s in range(1, w):
                acc = acc + up(POOL_PAD + tt - s)[:, lo:lo + POOL_GROUP]
            cnt = float(min(w, PAST_LEN + tt + 1))
            parts.append(acc / cnt - up(POOL_PAD + tt)[:, lo:lo + POOL_GROUP])
        d_rows.append(parts)
    d = jnp.concatenate([jnp.concatenate(parts, axis=1) for parts in d_rows], axis=0)
    br_a = _bdot(d, wpa_ref[...])
    for r in range(POOL_PAD):
        npool_ref[r] = up(r + tn)

    q = jnp.dot(h, win_ref[:, OFF_Q:OFF_K], preferred_element_type=F32)
    q_ref[...] = jnp.swapaxes((q * (HEAD_DIM ** -0.5)).reshape(tn, nb, ATTN_WIDTH), 0, 1)
    kv = jnp.dot(h, win_ref[:, OFF_K:OFF_GA], preferred_element_type=F32)
    kv = jnp.swapaxes(kv.reshape(tn, nb, 2 * KV_WIDTH), 0, 1).reshape(nb * tn, 2 * KV_WIDTH)
    k_ref[...] = kv[:, 0:KV_WIDTH]
    v_ref[...] = kv[:, KV_WIDTH:2 * KV_WIDTH]
    ga = jnp.dot(h, win_ref[:, OFF_GA:OFF_GB], preferred_element_type=F32)
    ma_ref[...] = jax.nn.sigmoid(ga) * br_a
    gb = jnp.dot(h, win_ref[:, OFF_GB:IN_WIDTH], preferred_element_type=F32)
    sgb_ref[...] = jax.nn.sigmoid(gb)


def _sproj_call(x, mod_s, g1, win, wpa, prefix_t, l, nb, tn):
    m = tn * nb
    shapes = [
        jax.ShapeDtypeStruct((nb, tn, ATTN_WIDTH), F32),
        jax.ShapeDtypeStruct((m, KV_WIDTH), F32),
        jax.ShapeDtypeStruct((m, KV_WIDTH), F32),
        jax.ShapeDtypeStruct((m, D_MODEL), F32),
        jax.ShapeDtypeStruct((m, D_MODEL), F32),
        jax.ShapeDtypeStruct((POOL_PAD, nb, POOL_WIDTH), F32),
    ]
    return pl.pallas_call(
        functools.partial(_sproj_kernel, nb=nb, tn=tn),
        grid=(1,),
        in_specs=[
            _const_spec(x.shape),
            pl.BlockSpec((None, nb, MOD_SLAB), lambda i: (l, 0, 0), pipeline_mode=pl.Buffered(1)),
            _layer_spec((1, D_MODEL), l),
            _weight_spec(win, (D_MODEL, IN_WIDTH), l),
            _weight_spec(wpa, (POOL_WIDTH, D_MODEL), l),
            _layer_spec((POOL_PAD, nb, POOL_WIDTH), l),
        ],
        out_specs=[_whole_out_spec(s.shape) for s in shapes],
        out_shape=shapes,
        compiler_params=pltpu.CompilerParams(
            dimension_semantics=("arbitrary",),
            vmem_limit_bytes=VMEM_LIMIT),
        name="sample_proj",
    )(x, mod_s, g1, win, wpa, prefix_t)


def _sample_bias(tn):
    t = np.arange(tn)[:, None]
    lane = np.arange(2 * WINDOW)[None, :]
    cached = lane < WINDOW
    new_t = lane - (2 * WINDOW - tn)
    dist = np.where(cached, t + WINDOW - lane, t - new_t)
    valid = np.where(cached, (dist >= 0) & (dist < WINDOW), (new_t >= 0) & (dist >= 0))
    sl = _alibi_slopes()[:, None, None]
    return np.where(valid[None], -sl * dist[None], NEG_INF).astype(np.float32)


def _sattn_kernel(sinks_ref, q_ref, kc_ref, vc_ref, kn_ref, vn_ref, bias_ref, *rest,
                   layer, nb, tn, bb):
    if layer:
        pk_ref, pv_ref = rest[:2]
        rest = rest[2:]
    o_ref, nk_ref, nv_ref, wk, wv = rest
    i = pl.program_id(0)
    rows = GQA_GROUP * tn
    keep = WINDOW - tn

    @pl.when(i == 0)
    def _new_rows():
        for c0 in range(0, nb * tn, LANES):
            wk[:, c0:c0 + LANES] = kn_ref[c0:c0 + LANES, :].T
            wv[:, c0:c0 + LANES] = vn_ref[c0:c0 + LANES, :].T

    if layer:
        nk_ref[0:layer] = pk_ref[...]
        nv_ref[0:layer] = pv_ref[...]

    lane0 = i * (bb * tn)
    tile0 = pl.multiple_of((lane0 // LANES) * LANES, LANES)
    off0 = lane0 % LANES
    wkt = wk[:, pl.ds(tile0, LANES)]
    wvt = wv[:, pl.ds(tile0, LANES)]
    lane = lax.broadcasted_iota(jnp.int32, (HEAD_DIM, WINDOW), 1)
    tail = lane >= keep
    grow = lax.broadcasted_iota(jnp.int32, (rows, 1), 0) // tn
    units = [(bl, hk) for bl in range(bb) for hk in range(N_KV_HEADS)]
    biases, sinks = [], []
    for hk in range(N_KV_HEADS):
        biases.append(jnp.concatenate(
            [bias_ref[hk * GQA_GROUP + g] for g in range(GQA_GROUP)], axis=0))
        sink = jnp.zeros((rows, 1), F32)
        for g in range(GQA_GROUP):
            sink = jnp.where(grow == g, sinks_ref[layer, hk * GQA_GROUP + g], sink)
        sinks.append(sink)

    scores, values = {}, {}
    for bl in range(bb):
        shift = (keep - off0 - bl * tn) % LANES
        nkb = pltpu.roll(wkt, shift, axis=1)
        nvb = pltpu.roll(wvt, shift, axis=1)
        for hk in range(N_KV_HEADS):
            kt = kc_ref[bl, hk]
            vt = vc_ref[bl, hk]
            nkt = jnp.where(tail, nkb[hk * HEAD_DIM:(hk + 1) * HEAD_DIM, :], 0.0)
            nvt = jnp.where(tail, nvb[hk * HEAD_DIM:(hk + 1) * HEAD_DIM, :], 0.0)
            nk_ref[layer, bl, hk] = jnp.where(tail, nkt, pltpu.roll(kt, keep, axis=1))
            nv_ref[layer, bl, hk] = jnp.where(tail, nvt, pltpu.roll(vt, keep, axis=1))
            keys = jnp.concatenate([kt, nkt], axis=1).astype(BF16)
            values[(bl, hk)] = jnp.concatenate([vt, nvt], axis=1).astype(BF16)
            qb = q_ref[bl]
            qu = jnp.concatenate(
                [qb[:, (hk * GQA_GROUP + g) * HEAD_DIM:(hk * GQA_GROUP + g + 1) * HEAD_DIM]
                 for g in range(GQA_GROUP)], axis=0).astype(BF16)
            scores[(bl, hk)] = jnp.dot(qu, keys, preferred_element_type=F32)
    probs, dens = {}, {}
    for (bl, hk) in units:
        s = scores[(bl, hk)] + biases[hk]
        m = jnp.maximum(jnp.max(s, axis=-1, keepdims=True), sinks[hk])
        p = jnp.exp(s - m)
        dens[(bl, hk)] = jnp.sum(p, axis=-1, keepdims=True) + jnp.exp(sinks[hk] - m)
        probs[(bl, hk)] = p.astype(BF16)
    for (bl, hk) in units:
        o = lax.dot_general(probs[(bl, hk)], values[(bl, hk)], (((1,), (1,)), ((), ())),
                            preferred_element_type=F32)
        o = o / dens[(bl, hk)]
        gw = GQA_GROUP * HEAD_DIM
        o_ref[bl, :, hk * gw:(hk + 1) * gw] = jnp.concatenate(
            [o[g * tn:(g + 1) * tn, :] for g in range(GQA_GROUP)], axis=1)


def _sattn_call(q4, kn_bt, vn_bt, ck, cv, sinks, rolled, l, nb, tn):
    bb = SATTN_BATCH
    unit = (bb, N_KV_HEADS, HEAD_DIM, WINDOW)
    cache_shape = jax.ShapeDtypeStruct((l + 1, nb) + unit[1:], F32)
    in_specs = [
        pl.BlockSpec((bb, tn, ATTN_WIDTH), lambda i, s: (i, 0, 0)),
        pl.BlockSpec((None,) + unit, lambda i, s: (l, i, 0, 0, 0)),
        pl.BlockSpec((None,) + unit, lambda i, s: (l, i, 0, 0, 0)),
        _const_spec((nb * tn, KV_WIDTH)),
        _const_spec((nb * tn, KV_WIDTH)),
        _const_spec((N_HEADS, tn, 2 * WINDOW)),
    ]
    args = [sinks, q4, ck, cv, kn_bt, vn_bt, jnp.asarray(_sample_bias(tn))]
    if l:
        in_specs += [pl.BlockSpec((l,) + unit, lambda i, s: (0, i, 0, 0, 0))] * 2
        args += list(rolled)
    grid_spec = pltpu.PrefetchScalarGridSpec(
        num_scalar_prefetch=1,
        grid=(nb // bb,),
        in_specs=in_specs,
        out_specs=[
            pl.BlockSpec((bb, tn, ATTN_WIDTH), lambda i, s: (i, 0, 0)),
            pl.BlockSpec((l + 1,) + unit, lambda i, s: (0, i, 0, 0, 0)),
            pl.BlockSpec((l + 1,) + unit, lambda i, s: (0, i, 0, 0, 0)),
        ],
        scratch_shapes=[
            pltpu.VMEM((KV_WIDTH, nb * tn), F32),
            pltpu.VMEM((KV_WIDTH, nb * tn), F32),
        ],
    )
    return pl.pallas_call(
        functools.partial(_sattn_kernel, layer=l, nb=nb, tn=tn, bb=bb),
        grid_spec=grid_spec,
        out_shape=[jax.ShapeDtypeStruct((nb, tn, ATTN_WIDTH), F32), cache_shape, cache_shape],
        compiler_params=pltpu.CompilerParams(
            dimension_semantics=("arbitrary",),
            vmem_limit_bytes=VMEM_LIMIT),
        name="sample_attn",
    )(*args)


def kernel(x_prompt, x_sample, cache_k, cache_v, state_pool, c_prompt, c_sample,
           w_ada, b_ada, norm1_g, w_in, w_pool, pool_scale, attn_sinks, w_a, w_b,
           w_out, norm2_g, w_up, w_down, final_g):
    bp, tp, _ = x_prompt.shape
    nb, tn, _ = x_sample.shape
    assert tp % MIX_ROWS == 0 and tp % FFN_ROWS == 0 and tn <= 8 and nb == LANES
    assert bp % 8 == 0

    mod_p, mod_s, *mix_w = _mod_call(c_prompt, c_sample, w_ada, b_ada,
                                     [(w, 0) for w in (w_in, w_b, w_out)])
    wpa = _poolw_call(w_pool, pool_scale.reshape(DEPTH, 1, POOL_WIDTH), w_a)
    bias = jnp.asarray(_prompt_bias())
    fg = final_g.reshape(1, D_MODEL)
    g1 = norm1_g.reshape(DEPTH, 1, D_MODEL)
    g2 = norm2_g.reshape(DEPTH, 1, D_MODEL)

    xs = x_sample
    prefix_t = state_pool.transpose(0, 2, 1, 3)
    ck = cache_k.transpose(0, 1, 3, 4, 2)
    cv = cache_v.transpose(0, 1, 3, 4, 2)

    xp = x_prompt
    kp, vp, pp, ps, rolled = [], [], [], [], None
    for l in range(DEPTH):
        last = l == DEPTH - 1
        win, wb, wout = mix_w
        xp, nk, nv, npool, wup, wdn = _pmix_call(xp, mod_p, g1, win, wpa, wb, wout,
                                                 attn_sinks, bias, l, [(w_up, l), (w_down, l)])
        kp.append(nk); vp.append(nv); pp.append(npool)

        q, kn, vn, ma, sgb, npool_s = _sproj_call(xs, mod_s, g1, win, wpa, prefix_t, l, nb, tn)
        o, *rolled = _sattn_call(q, kn, vn, ck, cv, attn_sinks, rolled, l, nb, tn)
        ps.append(npool_s)

        nxt = [] if last else [(w, l + 1) for w in (w_in, w_b, w_out)]
        xp, xs, *mix_w = _ffn_call(xp, xs, o, ma, sgb, mod_p, mod_s, g2, wup, wdn, wb, wout, fg,
                                   l, last, nxt)

    kv_shape_p = (DEPTH, bp, N_KV_HEADS, HEAD_DIM, WINDOW)
    return (xp,
            xs,
            jnp.stack(kp).reshape(kv_shape_p).transpose(0, 1, 4, 2, 3),
            jnp.stack(vp).reshape(kv_shape_p).transpose(0, 1, 4, 2, 3),
            jnp.stack(pp),
            rolled[0].transpose(0, 1, 4, 2, 3),
            rolled[1].transpose(0, 1, 4, 2, 3),
            jnp.stack(ps).transpose(0, 2, 1, 3))
```

```python
import functools

import numpy as np
import jax
import jax.numpy as jnp
from jax import lax
from jax.experimental import pallas as pl
from jax.experimental.pallas import tpu as pltpu

D_MODEL = 1024
DEPTH = 2
PAST_LEN = 16384
POOL_WIDTH = D_MODEL // 2
POOL_WINDOWS = (2, 4, 8, 16)
POOL_GROUP = POOL_WIDTH // len(POOL_WINDOWS)
POOL_PAD = max(POOL_WINDOWS) - 1
N_HEADS = 8
N_KV_HEADS = 2
HEAD_DIM = 64
GQA_GROUP = N_HEADS // N_KV_HEADS
ATTN_WIDTH = N_HEADS * HEAD_DIM
KV_WIDTH = N_KV_HEADS * HEAD_DIM
WINDOW = 128
ATTN_BLOCK = 128
D_FF = 4 * D_MODEL
RMS_EPS = 1e-6
NEG_INF = -1e30

OFF_U = 0
OFF_Q = OFF_U + POOL_WIDTH
OFF_K = OFF_Q + ATTN_WIDTH
OFF_V = OFF_K + KV_WIDTH
OFF_GA = OFF_V + KV_WIDTH
OFF_GB = OFF_GA + D_MODEL
IN_WIDTH = OFF_GB + D_MODEL
MOD_SLAB = 3 * D_MODEL

LANES = 128
HIST = 16
assert all(w & (w - 1) == 0 for w in POOL_WINDOWS) and list(POOL_WINDOWS) == sorted(POOL_WINDOWS)
assert HIST >= POOL_PAD
VMEM_LIMIT = 56 * 1024 * 1024
SUB_ROWS = 512
MIX_ROWS = 2 * SUB_ROWS
FFN_SUB_ROWS = 512
FFN_ROWS = 2 * FFN_SUB_ROWS
SATTN_BATCH = 16

F32 = jnp.float32
BF16 = jnp.bfloat16


def _bdot(a, b):
    return jnp.dot(a.astype(BF16), b.astype(BF16), preferred_element_type=F32)


def _rms_mod(x, g, sc, sh):
    ms = jnp.mean(x * x, axis=-1, keepdims=True)
    return (x * lax.rsqrt(ms + RMS_EPS) * g) * (1.0 + sc) + sh


def _alibi_slopes():
    return 2.0 ** (-8.0 * (np.arange(N_HEADS) + 1) / N_HEADS)


def _const_spec(shape):
    nd = len(shape)
    return pl.BlockSpec(shape, lambda *_: (0,) * nd, pipeline_mode=pl.Buffered(1))


def _layer_spec(shape, l):
    nd = len(shape)
    return pl.BlockSpec((None,) + tuple(shape), lambda *_: (l,) + (0,) * nd,
                        pipeline_mode=pl.Buffered(1))


def _whole_out_spec(shape):
    nd = len(shape)
    return pl.BlockSpec(shape, lambda *_: (0,) * nd)


def _weight_spec(w, shape, l):
    return _const_spec(shape) if w.shape == tuple(shape) else _layer_spec(shape, l)


def _convert_specs(jobs, n_steps, step_of):
    in_specs, out_specs, out_shapes = [], [], []
    for w, l in jobs:
        _, r, c = w.shape
        rows = r // n_steps
        assert rows * n_steps == r and rows % 16 == 0, (w.shape, n_steps)
        in_specs.append(pl.BlockSpec((None, rows, c), lambda *a, l=l: (l, step_of(*a), 0)))
        out_specs.append(pl.BlockSpec((rows, c), lambda *a: (step_of(*a), 0)))
        out_shapes.append(jax.ShapeDtypeStruct((r, c), BF16))
    return in_specs, out_specs, out_shapes


def _convert_slabs(src_refs, dst_refs):
    for src, dst in zip(src_refs, dst_refs):
        dst[...] = src[...].astype(BF16)


def _mod_kernel(cp_ref, cs_ref, w_ref, b_ref, *rest, n_cvt):
    cvt_in, (op_ref, os_ref), cvt_out = rest[:n_cvt], rest[n_cvt:n_cvt + 2], rest[n_cvt + 2:]
    _convert_slabs(cvt_in, cvt_out)
    w = w_ref[...].astype(BF16)
    b = b_ref[pl.ds(pl.program_id(0), 1), :]
    for c_ref, o_ref in ((cp_ref, op_ref), (cs_ref, os_ref)):
        c = c_ref[...]
        s = (c * jax.nn.sigmoid(c)).astype(BF16)
        o_ref[...] = jnp.dot(s, w, preferred_element_type=F32) + b


def _mod_call(c_p, c_s, w_ada, b_ada, cvt_jobs):
    tn = 1536
    n = 6 * D_MODEL
    nj = n // tn
    mp, ms = c_p.shape[0], c_s.shape[0]
    cvt_in, cvt_out, cvt_shapes = _convert_specs(cvt_jobs, DEPTH * nj, lambda l, j: l * nj + j)
    return pl.pallas_call(
        functools.partial(_mod_kernel, n_cvt=len(cvt_jobs)),
        grid=(DEPTH, nj),
        in_specs=[
            _const_spec((mp, D_MODEL)),
            _const_spec((ms, D_MODEL)),
            pl.BlockSpec((None, D_MODEL, tn), lambda l, j: (l, 0, j)),
            pl.BlockSpec((DEPTH, tn), lambda l, j: (0, j)),
        ] + cvt_in,
        out_specs=[
            pl.BlockSpec((None, mp, tn), lambda l, j: (l, 0, j)),
            pl.BlockSpec((None, ms, tn), lambda l, j: (l, 0, j)),
        ] + cvt_out,
        out_shape=[jax.ShapeDtypeStruct((DEPTH, mp, n), F32),
                   jax.ShapeDtypeStruct((DEPTH, ms, n), F32)] + cvt_shapes,
        compiler_params=pltpu.CompilerParams(
            dimension_semantics=("arbitrary", "arbitrary"),
            vmem_limit_bytes=VMEM_LIMIT),
        name="adaln_mod",
    )(c_p, c_s, w_ada, b_ada, *[w for w, _ in cvt_jobs])


def _poolw_kernel(wp_ref, ps_ref, wa_ref, o_ref):
    scale = ps_ref[pl.ds(pl.program_id(0), 1), :]
    for g in range(len(POOL_WINDOWS)):
        rows = slice(g * POOL_GROUP, (g + 1) * POOL_GROUP)
        o_ref[rows, :] = jnp.dot(wp_ref[g] * scale[:, rows], wa_ref[rows, :],
                                 preferred_element_type=F32,
                                 precision=lax.Precision.HIGHEST).astype(BF16)


def _poolw_call(w_pool, pool_scale, w_a):
    return pl.pallas_call(
        _poolw_kernel,
        grid=(DEPTH,),
        in_specs=[
            pl.BlockSpec((None,) + w_pool.shape[1:], lambda l: (l, 0, 0, 0)),
            pl.BlockSpec((DEPTH, POOL_WIDTH), lambda l: (0, 0)),
            pl.BlockSpec((None, POOL_WIDTH, D_MODEL), lambda l: (l, 0, 0)),
        ],
        out_specs=pl.BlockSpec((None, POOL_WIDTH, D_MODEL), lambda l: (l, 0, 0)),
        out_shape=jax.ShapeDtypeStruct((DEPTH, POOL_WIDTH, D_MODEL), BF16),
        compiler_params=pltpu.CompilerParams(
            dimension_semantics=("arbitrary",),
            vmem_limit_bytes=VMEM_LIMIT),
        name="pool_weights",
    )(w_pool, pool_scale, w_a)


def _prompt_bias():
    i = np.arange(ATTN_BLOCK)[:, None]
    j = np.arange(2 * ATTN_BLOCK)[None, :]
    dist = i + ATTN_BLOCK - j
    valid = (dist >= 0) & (dist < WINDOW)
    valid_first = valid & (j >= ATTN_BLOCK)
    sl = _alibi_slopes()[:, None, None]
    b = np.where(valid[None], -sl * dist[None], NEG_INF)
    b0 = np.where(valid_first[None], -sl * dist[None], NEG_INF)
    return np.stack([b, b0]).astype(np.float32)


def _pmix_kernel(sinks_ref, x_ref, mod_ref, g1_ref, win_ref, wpa_ref,
                 wb_ref, wout_ref, bias_ref, *rest, layer, n_cvt):
    cvt_in, rest = rest[:n_cvt], rest[n_cvt:]
    (xo_ref, nk_ref, nv_ref, npool_ref), rest = rest[:4], rest[4:]
    cvt_out, (ubuf, ka, kb, va, vb, obuf) = rest[:n_cvt], rest[n_cvt:]
    _convert_slabs(cvt_in, cvt_out)
    tm, sub = MIX_ROWS, SUB_ROWS
    bi = pl.program_id(0)
    t = pl.program_id(1)
    nt = pl.num_programs(1)

    @pl.when(t == 0)
    def _init():
        ubuf[0:HIST, :] = jnp.zeros((HIST, POOL_WIDTH), F32)
        zero = jnp.zeros((N_KV_HEADS, ATTN_BLOCK, LANES), BF16)
        ka[:, 0:ATTN_BLOCK, :] = zero
        kb[:, 0:ATTN_BLOCK, :] = zero
        va[:, 0:ATTN_BLOCK, :] = zero
        vb[:, 0:ATTN_BLOCK, :] = zero

    mod = mod_ref[pl.ds(bi, 1), :]
    sh1 = mod[:, 0:D_MODEL]
    sc1 = mod[:, D_MODEL:2 * D_MODEL]
    gt1 = mod[:, 2 * D_MODEL:3 * D_MODEL]
    lane = lax.broadcasted_iota(jnp.int32, (sub, LANES), 1)
    low = lane < HEAD_DIM
    low_q = lax.broadcasted_iota(jnp.int32, (ATTN_BLOCK, LANES), 1) < HEAD_DIM
    nt_dims = (((1,), (1,)), ((), ()))
    st = [dict(ro=i * sub) for i in range(tm // sub)]

    def norm(c):
        c["x"] = x_ref[c["ro"]:c["ro"] + sub, :]
        c["h"] = _rms_mod(c["x"], g1_ref[layer:layer + 1, :], sc1, sh1).astype(BF16)

    def proj(c):
        h = c["h"]
        c["u"] = jnp.dot(h, win_ref[:, OFF_U:OFF_Q], preferred_element_type=F32)
        q = jnp.dot(h, win_ref[:, OFF_Q:OFF_K], preferred_element_type=F32)
        c["qb"] = (q * (HEAD_DIM ** -0.5)).astype(BF16)
        kv = jnp.dot(h, win_ref[:, OFF_K:OFF_GA], preferred_element_type=F32)
        c["k"] = kv[:, 0:KV_WIDTH]
        c["v"] = kv[:, KV_WIDTH:2 * KV_WIDTH]

    def pool_sums(c):
        ro, u = c["ro"], c["u"]
        ubuf[HIST + ro:HIST + ro + sub, :] = u
        pos = t * tm + ro + lax.broadcasted_iota(jnp.int32, (sub, 1), 0)
        cur = ubuf[ro:ro + HIST + sub, :]
        d, w = [], 1
        for g, wg in enumerate(POOL_WINDOWS):
            while w < wg:
                cur = cur + pltpu.roll(cur, w, axis=0)
                w *= 2
            ug = u[:, g * POOL_GROUP:(g + 1) * POOL_GROUP]
            cnt = jnp.minimum(wg, pos + 1).astype(F32)
            d.append((cur[HIST:, 0:POOL_GROUP] / cnt - ug).astype(BF16))
            if g + 1 < len(POOL_WINDOWS):
                cur = cur[:, POOL_GROUP:]
        c["d"] = d

    def kv_store(c):
        r = ATTN_BLOCK + c["ro"]
        zero = jnp.zeros((sub, LANES), BF16)
        for x, xa, xb in ((c["k"], ka, kb), (c["v"], va, vb)):
            x16 = x.astype(BF16)
            xr16 = pltpu.roll(x, HEAD_DIM, axis=1).astype(BF16)
            xa[0, r:r + sub, :] = jnp.where(low, x16, zero)
            xb[0, r:r + sub, :] = jnp.where(low, zero, xr16)
            xa[1, r:r + sub, :] = jnp.where(low, xr16, zero)
            xb[1, r:r + sub, :] = jnp.where(low, zero, x16)

    def pool_proj(c):
        c["br_a"] = jnp.dot(jnp.concatenate(c["d"], axis=1), wpa_ref[...],
                            preferred_element_type=F32)

    def units(c):
        j0 = c["ro"] // ATTN_BLOCK
        return [(j, hk, pr) for j in range(j0, j0 + sub // ATTN_BLOCK)
                for hk in range(N_KV_HEADS) for pr in range(GQA_GROUP // 2)]

    def scores(c):
        sc = {}
        for (j, hk, pr) in units(c):
            r0 = j * ATTN_BLOCK
            c0 = (hk * (GQA_GROUP // 2) + pr) * LANES
            q2 = c["qb"][r0 - c["ro"]:r0 - c["ro"] + ATTN_BLOCK, c0:c0 + LANES]
            ks = (ka[hk, r0:r0 + 2 * ATTN_BLOCK, :], kb[hk, r0:r0 + 2 * ATTN_BLOCK, :])
            for e in range(2):
                sc[(j, hk, pr, e)] = lax.dot_general(q2, ks[e], nt_dims,
                                                     preferred_element_type=F32)
        c["s"] = sc

    def gate_a(c):
        c["ga"] = jnp.dot(c["h"], win_ref[:, OFF_GA:OFF_GB], preferred_element_type=F32)

    def gate_b(c):
        c["gb"] = jnp.dot(c["h"], win_ref[:, OFF_GB:IN_WIDTH], preferred_element_type=F32)

    def softmax(c):
        probs, dens = {}, {}
        for (j, hk, pr) in units(c):
            for e in range(2):
                head = hk * GQA_GROUP + pr * 2 + e
                if j == 0:
                    bias = jnp.where(t == 0, bias_ref[1, head], bias_ref[0, head])
                else:
                    bias = bias_ref[0, head]
                s = c["s"][(j, hk, pr, e)] + bias
                sink = sinks_ref[layer, head]
                m = jnp.maximum(jnp.max(s, axis=-1, keepdims=True), sink)
                p = jnp.exp(s - m)
                dens[(j, hk, pr, e)] = jnp.sum(p, axis=-1, keepdims=True) + jnp.exp(sink - m)
                probs[(j, hk, pr, e)] = p.astype(BF16)
        c["p"], c["den"] = probs, dens

    def values(c):
        for (j, hk, pr) in units(c):
            r0 = j * ATTN_BLOCK
            c0 = (hk * (GQA_GROUP // 2) + pr) * LANES
            vs = (va[hk, r0:r0 + 2 * ATTN_BLOCK, :], vb[hk, r0:r0 + 2 * ATTN_BLOCK, :])
            o2 = (jnp.dot(c["p"][(j, hk, pr, 0)], vs[0], preferred_element_type=F32)
                  + jnp.dot(c["p"][(j, hk, pr, 1)], vs[1], preferred_element_type=F32))
            den = jnp.where(low_q, c["den"][(j, hk, pr, 0)], c["den"][(j, hk, pr, 1)])
            obuf[r0:r0 + ATTN_BLOCK, c0:c0 + LANES] = o2 / den

    def tail(c):
        ro = c["ro"]
        br_b = _bdot(obuf[ro:ro + sub, :], wb_ref[...])
        merged = jax.nn.sigmoid(c["ga"]) * c["br_a"] + jax.nn.sigmoid(c["gb"]) * br_b
        xo_ref[ro:ro + sub, :] = c["x"] + gt1 * _bdot(merged, wout_ref[...])

    stages = [
        (norm,),
        (proj, gate_a),
        (pool_sums, kv_store),
        (pool_proj, scores, gate_b),
        (softmax,),
        (values,),
        (tail,),
    ]
    order = sorted((2 * s + 3 * i, i, s) for i in range(len(st)) for s in range(len(stages)))
    for _, i, s in order:
        for fn in stages[s]:
            fn(st[i])

    @pl.when(t == nt - 1)
    def _state():
        nk_ref[...] = st[-1]["k"][sub - WINDOW:, :].T
        nv_ref[...] = st[-1]["v"][sub - WINDOW:, :].T
        npool_ref[...] = ubuf[HIST + tm - POOL_PAD:HIST + tm, :]

    ubuf[0:HIST, :] = ubuf[tm:tm + HIST, :]
    for buf in (ka, kb, va, vb):
        buf[:, 0:ATTN_BLOCK, :] = buf[:, tm:tm + ATTN_BLOCK, :]


def _pmix_call(x, mod_p, g1, win, wpa, wb, wout, sinks, bias, l, cvt_jobs):
    b, t, _ = x.shape
    tm = MIX_ROWS
    nt = t // tm
    cvt_in, cvt_out, cvt_shapes = _convert_specs(cvt_jobs, b * nt, lambda i, j, s: i * nt + j)
    grid_spec = pltpu.PrefetchScalarGridSpec(
        num_scalar_prefetch=1,
        grid=(b, nt),
        in_specs=[
            pl.BlockSpec((None, tm, D_MODEL), lambda i, j, s: (i, j, 0)),
            pl.BlockSpec((None, b, MOD_SLAB), lambda i, j, s: (l, 0, 0),
                         pipeline_mode=pl.Buffered(1)),
            _const_spec((DEPTH, D_MODEL)),
            _weight_spec(win, (D_MODEL, IN_WIDTH), l),
            _weight_spec(wpa, (POOL_WIDTH, D_MODEL), l),
            _weight_spec(wb, (ATTN_WIDTH, D_MODEL), l),
            _weight_spec(wout, (D_MODEL, D_MODEL), l),
            _const_spec((2, N_HEADS, ATTN_BLOCK, 2 * ATTN_BLOCK)),
        ] + cvt_in,
        out_specs=[
            pl.BlockSpec((None, tm, D_MODEL), lambda i, j, s: (i, j, 0)),
            pl.BlockSpec((None, KV_WIDTH, WINDOW), lambda i, j, s: (i, 0, 0)),
            pl.BlockSpec((None, KV_WIDTH, WINDOW), lambda i, j, s: (i, 0, 0)),
            pl.BlockSpec((None, POOL_PAD, POOL_WIDTH), lambda i, j, s: (i, 0, 0)),
        ] + cvt_out,
        scratch_shapes=[
            pltpu.VMEM((HIST + tm, POOL_WIDTH), F32),
            pltpu.VMEM((N_KV_HEADS, ATTN_BLOCK + tm, LANES), BF16),
            pltpu.VMEM((N_KV_HEADS, ATTN_BLOCK + tm, LANES), BF16),
            pltpu.VMEM((N_KV_HEADS, ATTN_BLOCK + tm, LANES), BF16),
            pltpu.VMEM((N_KV_HEADS, ATTN_BLOCK + tm, LANES), BF16),
            pltpu.VMEM((tm, ATTN_WIDTH), F32),
        ],
    )
    return pl.pallas_call(
        functools.partial(_pmix_kernel, layer=l, n_cvt=len(cvt_jobs)),
        grid_spec=grid_spec,
        out_shape=[
            jax.ShapeDtypeStruct((b, t, D_MODEL), F32),
            jax.ShapeDtypeStruct((b, KV_WIDTH, WINDOW), F32),
            jax.ShapeDtypeStruct((b, KV_WIDTH, WINDOW), F32),
            jax.ShapeDtypeStruct((b, POOL_PAD, POOL_WIDTH), F32),
        ] + cvt_shapes,
        compiler_params=pltpu.CompilerParams(
            dimension_semantics=("arbitrary", "arbitrary"),
            vmem_limit_bytes=VMEM_LIMIT),
        name="prompt_mixer",
    )(sinks, x, mod_p, g1, win, wpa, wb, wout, bias, *[w for w, _ in cvt_jobs])


def _ffn_kernel(xp_ref, xs_ref, modp_ref, mods_ref, g2_ref, wup_ref, wdn_ref, fg_ref,
                so_ref, ma_ref, sgb_ref, gt1_ref, wb_ref, wout_ref,
                *rest, layer, final, n_prompt, tiles_per_row, reps, n_cvt):
    cvt_in, (op_ref, os_ref), cvt_out = rest[:n_cvt], rest[n_cvt:n_cvt + 2], rest[n_cvt + 2:]
    _convert_slabs(cvt_in, cvt_out)
    i = pl.program_id(0)

    def ffn(load, n_rows, o_ref, mod):
        sh2, sc2, gt2 = (mod[:, k * D_MODEL:(k + 1) * D_MODEL] for k in range(3))
        subs = range(0, n_rows, FFN_SUB_ROWS)
        rows = lambda a, r: a if a.shape[0] == 1 else a[r:r + FFN_SUB_ROWS]
        norm = lambda r: _rms_mod(load(r), g2_ref[layer:layer + 1, :],
                                  rows(sc2, r), rows(sh2, r)).astype(BF16)
        h2 = norm(0)
        for r in subs:
            ff = jnp.dot(h2, wup_ref[...], preferred_element_type=F32)
            if r + FFN_SUB_ROWS in subs:
                h2 = norm(r + FFN_SUB_ROWS)
            ff = jnp.square(jnp.maximum(ff, 0.0))
            y = load(r) + rows(gt2, r) * _bdot(ff, wdn_ref[...])
            if final:
                ms = jnp.mean(y * y, axis=-1, keepdims=True)
                y = y * lax.rsqrt(ms + RMS_EPS) * fg_ref[...]
            if len(o_ref.shape) == 3:
                o_ref[...] = jnp.swapaxes(y.reshape(reps, y.shape[0] // reps, D_MODEL), 0, 1)
            else:
                o_ref[r:r + FFN_SUB_ROWS, :] = y

    @pl.when(i < n_prompt)
    def _prompt():
        ffn(lambda r: xp_ref[r:r + FFN_SUB_ROWS, :], xp_ref.shape[0], op_ref,
            modp_ref[pl.ds(i // tiles_per_row, 1), :])

    @pl.when(i == n_prompt)
    def _sample():
        x = xs_ref[...]
        if x.ndim == 3:
            x = jnp.swapaxes(x, 0, 1).reshape(ma_ref.shape)
        gt1 = jnp.concatenate([gt1_ref[...]] * reps, axis=0)
        so = jnp.swapaxes(so_ref[...], 0, 1).reshape(ma_ref.shape[0], ATTN_WIDTH)
        br_b = _bdot(so, wb_ref[...])
        merged = ma_ref[...] + sgb_ref[...] * br_b
        x = x + gt1 * _bdot(merged, wout_ref[...])
        ffn(lambda r: x[r:r + FFN_SUB_ROWS, :], x.shape[0], os_ref,
            jnp.concatenate([mods_ref[...]] * reps, axis=0))


def _ffn_call(xp, xs, so, ma, sgb, mod_p, mod_s, g2, wup, wdn, wb, wout, fg, l, final, cvt_jobs):
    b, t, _ = xp.shape
    ms = ma.shape[0]
    nb = mod_s.shape[1]
    tm = FFN_ROWS
    tpr = t // tm
    n_prompt = b * tpr
    last = n_prompt - 1

    def p_idx(i):
        ii = jnp.minimum(i, last)
        return (ii // tpr, ii % tpr, 0)

    assert ms <= FFN_SUB_ROWS
    s_shape = (nb, ms // nb, D_MODEL) if final else (ms, D_MODEL)
    cvt_in, cvt_out, cvt_shapes = _convert_specs(cvt_jobs, n_prompt, lambda i: jnp.minimum(i, last))
    return pl.pallas_call(
        functools.partial(_ffn_kernel, layer=l, final=final, n_prompt=n_prompt, tiles_per_row=tpr,
                          reps=ms // nb, n_cvt=len(cvt_jobs)),
        grid=(n_prompt + 1,),
        in_specs=[
            pl.BlockSpec((None, tm, D_MODEL), p_idx),
            _const_spec(xs.shape),
            pl.BlockSpec((None, b, MOD_SLAB), lambda i: (l, 0, 1), pipeline_mode=pl.Buffered(1)),
            pl.BlockSpec((None, nb, MOD_SLAB), lambda i: (l, 0, 1), pipeline_mode=pl.Buffered(1)),
            _const_spec((DEPTH, D_MODEL)),
            _weight_spec(wup, (D_MODEL, D_FF), l),
            _weight_spec(wdn, (D_FF, D_MODEL), l),
            _const_spec((1, D_MODEL)),
            _const_spec(so.shape), _const_spec(ma.shape), _const_spec(sgb.shape),
            pl.BlockSpec((None, nb, D_MODEL), lambda i: (l, 0, 2), pipeline_mode=pl.Buffered(1)),
            _weight_spec(wb, (ATTN_WIDTH, D_MODEL), l),
            _weight_spec(wout, (D_MODEL, D_MODEL), l),
        ] + cvt_in,
        out_specs=[
            pl.BlockSpec((None, tm, D_MODEL), p_idx),
            _whole_out_spec(s_shape),
        ] + cvt_out,
        out_shape=[jax.ShapeDtypeStruct((b, t, D_MODEL), F32),
                   jax.ShapeDtypeStruct(s_shape, F32)] + cvt_shapes,
        compiler_params=pltpu.CompilerParams(
            dimension_semantics=("arbitrary",),
            vmem_limit_bytes=VMEM_LIMIT),
        name="ffn_final" if final else "ffn",
    )(xp, xs, mod_p, mod_s, g2, wup, wdn, fg, so, ma, sgb, mod_s, wb, wout,
      *[w for w, _ in cvt_jobs])


def _sproj_kernel(x_ref, mod_ref, g1_ref, win_ref, wpa_ref, pre_ref,
                  q_ref, k_ref, v_ref, ma_ref, sgb_ref, npool_ref, *, layer, nb, tn):
    x = x_ref[...]
    if x.ndim == 3:
        x = jnp.swapaxes(x, 0, 1).reshape(tn * nb, D_MODEL)
    mod = jnp.concatenate([mod_ref[:, 0:2 * D_MODEL]] * tn, axis=0)
    sh1 = mod[:, 0:D_MODEL]
    sc1 = mod[:, D_MODEL:2 * D_MODEL]
    h = _rms_mod(x, g1_ref[layer:layer + 1, :], sc1, sh1).astype(BF16)

    u = jnp.dot(h, win_ref[:, OFF_U:OFF_Q], preferred_element_type=F32)
    def up(r):
        if r < POOL_PAD:
            return pre_ref[r]
        r -= POOL_PAD
        return u[r * nb:(r + 1) * nb, :]

    d_rows = []
    for tt in range(tn):
        parts = []
        for g, w in enumerate(POOL_WINDOWS):
            lo = g * POOL_GROUP
            acc = up(POOL_PAD + tt)[:, lo:lo + POOL_GROUP]
            for s in range(1, w):
                acc = acc + up(POOL_PAD + tt - s)[:, lo:lo + POOL_GROUP]
            cnt = float(min(w, PAST_LEN + tt + 1))
            parts.append(acc / cnt - up(POOL_PAD + tt)[:, lo:lo + POOL_GROUP])
        d_rows.append(parts)
    d = jnp.concatenate([jnp.concatenate(parts, axis=1) for parts in d_rows], axis=0)
    br_a = _bdot(d, wpa_ref[...])
    for r in range(POOL_PAD):
        npool_ref[r] = up(r + tn)

    q = jnp.dot(h, win_ref[:, OFF_Q:OFF_K], preferred_element_type=F32)
    q_ref[...] = jnp.swapaxes((q * (HEAD_DIM ** -0.5)).reshape(tn, nb, ATTN_WIDTH), 0, 1)
    kv = jnp.dot(h, win_ref[:, OFF_K:OFF_GA], preferred_element_type=F32)
    kv = jnp.swapaxes(kv.reshape(tn, nb, 2 * KV_WIDTH), 0, 1).reshape(nb * tn, 2 * KV_WIDTH)
    k_ref[...] = kv[:, 0:KV_WIDTH]
    v_ref[...] = kv[:, KV_WIDTH:2 * KV_WIDTH]
    ga = jnp.dot(h, win_ref[:, OFF_GA:OFF_GB], preferred_element_type=F32)
    ma_ref[...] = jax.nn.sigmoid(ga) * br_a
    gb = jnp.dot(h, win_ref[:, OFF_GB:IN_WIDTH], preferred_element_type=F32)
    sgb_ref[...] = jax.nn.sigmoid(gb)


def _sproj_call(x, mod_s, g1, win, wpa, prefix_t, l, nb, tn):
    m = tn * nb
    shapes = [
        jax.ShapeDtypeStruct((nb, tn, ATTN_WIDTH), F32),
        jax.ShapeDtypeStruct((m, KV_WIDTH), F32),
        jax.ShapeDtypeStruct((m, KV_WIDTH), F32),
        jax.ShapeDtypeStruct((m, D_MODEL), F32),
        jax.ShapeDtypeStruct((m, D_MODEL), F32),
        jax.ShapeDtypeStruct((POOL_PAD, nb, POOL_WIDTH), F32),
    ]
    return pl.pallas_call(
        functools.partial(_sproj_kernel, layer=l, nb=nb, tn=tn),
        grid=(1,),
        in_specs=[
            _const_spec(x.shape),
            pl.BlockSpec((None, nb, MOD_SLAB), lambda i: (l, 0, 0), pipeline_mode=pl.Buffered(1)),
            _const_spec((DEPTH, D_MODEL)),
            _weight_spec(win, (D_MODEL, IN_WIDTH), l),
            _weight_spec(wpa, (POOL_WIDTH, D_MODEL), l),
            _layer_spec((POOL_PAD, nb, POOL_WIDTH), l),
        ],
        out_specs=[_whole_out_spec(s.shape) for s in shapes],
        out_shape=shapes,
        compiler_params=pltpu.CompilerParams(
            dimension_semantics=("arbitrary",),
            vmem_limit_bytes=VMEM_LIMIT),
        name="sample_proj",
    )(x, mod_s, g1, win, wpa, prefix_t)


def _sample_bias(tn):
    t = np.arange(tn)[:, None]
    lane = np.arange(2 * WINDOW)[None, :]
    cached = lane < WINDOW
    new_t = lane - (2 * WINDOW - tn)
    dist = np.where(cached, t + WINDOW - lane, t - new_t)
    valid = np.where(cached, (dist >= 0) & (dist < WINDOW), (new_t >= 0) & (dist >= 0))
    sl = _alibi_slopes()[:, None, None]
    return np.where(valid[None], -sl * dist[None], NEG_INF).astype(np.float32)


def _sattn_kernel(sinks_ref, q_ref, kc_ref, vc_ref, kn_ref, vn_ref, bias_ref, *rest,
                   layer, nb, tn, bb):
    if layer:
        pk_ref, pv_ref = rest[:2]
        rest = rest[2:]
    o_ref, nk_ref, nv_ref, wk, wv = rest
    i = pl.program_id(0)
    rows = GQA_GROUP * tn
    keep = WINDOW - tn

    @pl.when(i == 0)
    def _new_rows():
        for c0 in range(0, nb * tn, LANES):
            wk[:, c0:c0 + LANES] = kn_ref[c0:c0 + LANES, :].T
            wv[:, c0:c0 + LANES] = vn_ref[c0:c0 + LANES, :].T

    if layer:
        nk_ref[0:layer] = pk_ref[...]
        nv_ref[0:layer] = pv_ref[...]

    lane0 = i * (bb * tn)
    tile0 = pl.multiple_of((lane0 // LANES) * LANES, LANES)
    off0 = lane0 % LANES
    wkt = wk[:, pl.ds(tile0, LANES)]
    wvt = wv[:, pl.ds(tile0, LANES)]
    lane = lax.broadcasted_iota(jnp.int32, (HEAD_DIM, WINDOW), 1)
    tail = lane >= keep
    grow = lax.broadcasted_iota(jnp.int32, (rows, 1), 0) // tn
    units = [(bl, hk) for bl in range(bb) for hk in range(N_KV_HEADS)]
    biases, sinks = [], []
    for hk in range(N_KV_HEADS):
        biases.append(jnp.concatenate(
            [bias_ref[hk * GQA_GROUP + g] for g in range(GQA_GROUP)], axis=0))
        sink = jnp.zeros((rows, 1), F32)
        for g in range(GQA_GROUP):
            sink = jnp.where(grow == g, sinks_ref[layer, hk * GQA_GROUP + g], sink)
        sinks.append(sink)

    scores, values = {}, {}
    for bl in range(bb):
        shift = (keep - off0 - bl * tn) % LANES
        nkb = pltpu.roll(wkt, shift, axis=1)
        nvb = pltpu.roll(wvt, shift, axis=1)
        for hk in range(N_KV_HEADS):
            kt = kc_ref[bl, hk]
            vt = vc_ref[bl, hk]
            nkt = jnp.where(tail, nkb[hk * HEAD_DIM:(hk + 1) * HEAD_DIM, :], 0.0)
            nvt = jnp.where(tail, nvb[hk * HEAD_DIM:(hk + 1) * HEAD_DIM, :], 0.0)
            nk_ref[layer, bl, hk] = jnp.where(tail, nkt, pltpu.roll(kt, keep, axis=1))
            nv_ref[layer, bl, hk] = jnp.where(tail, nvt, pltpu.roll(vt, keep, axis=1))
            keys = jnp.concatenate([kt, nkt], axis=1).astype(BF16)
            values[(bl, hk)] = jnp.concatenate([vt, nvt], axis=1).astype(BF16)
            qb = q_ref[bl]
            qu = jnp.concatenate(
                [qb[:, (hk * GQA_GROUP + g) * HEAD_DIM:(hk * GQA_GROUP + g + 1) * HEAD_DIM]
                 for g in range(GQA_GROUP)], axis=0).astype(BF16)
            scores[(bl, hk)] = jnp.dot(qu, keys, preferred_element_type=F32)
    probs, dens = {}, {}
    for (bl, hk) in units:
        s = scores[(bl, hk)] + biases[hk]
        m = jnp.maximum(jnp.max(s, axis=-1, keepdims=True), sinks[hk])
        p = jnp.exp(s - m)
        dens[(bl, hk)] = jnp.sum(p, axis=-1, keepdims=True) + jnp.exp(sinks[hk] - m)
        probs[(bl, hk)] = p.astype(BF16)
    for (bl, hk) in units:
        o = lax.dot_general(probs[(bl, hk)], values[(bl, hk)], (((1,), (1,)), ((), ())),
                            preferred_element_type=F32)
        o = o / dens[(bl, hk)]
        gw = GQA_GROUP * HEAD_DIM
        o_ref[bl, :, hk * gw:(hk + 1) * gw] = jnp.concatenate(
            [o[g * tn:(g + 1) * tn, :] for g in range(GQA_GROUP)], axis=1)


def _sattn_call(q4, kn_bt, vn_bt, ck, cv, sinks, rolled, l, nb, tn):
    bb = SATTN_BATCH
    unit = (bb, N_KV_HEADS, HEAD_DIM, WINDOW)
    cache_shape = jax.ShapeDtypeStruct((l + 1, nb) + unit[1:], F32)
    in_specs = [
        pl.BlockSpec((bb, tn, ATTN_WIDTH), lambda i, s: (i, 0, 0)),
        pl.BlockSpec((None,) + unit, lambda i, s: (l, i, 0, 0, 0)),
        pl.BlockSpec((None,) + unit, lambda i, s: (l, i, 0, 0, 0)),
        _const_spec((nb * tn, KV_WIDTH)),
        _const_spec((nb * tn, KV_WIDTH)),
        _const_spec((N_HEADS, tn, 2 * WINDOW)),
    ]
    args = [sinks, q4, ck, cv, kn_bt, vn_bt, jnp.asarray(_sample_bias(tn))]
    if l:
        in_specs += [pl.BlockSpec((l,) + unit, lambda i, s: (0, i, 0, 0, 0))] * 2
        args += list(rolled)
    grid_spec = pltpu.PrefetchScalarGridSpec(
        num_scalar_prefetch=1,
        grid=(nb // bb,),
        in_specs=in_specs,
        out_specs=[
            pl.BlockSpec((bb, tn, ATTN_WIDTH), lambda i, s: (i, 0, 0)),
            pl.BlockSpec((l + 1,) + unit, lambda i, s: (0, i, 0, 0, 0)),
            pl.BlockSpec((l + 1,) + unit, lambda i, s: (0, i, 0, 0, 0)),
        ],
        scratch_shapes=[
            pltpu.VMEM((KV_WIDTH, nb * tn), F32),
            pltpu.VMEM((KV_WIDTH, nb * tn), F32),
        ],
    )
    return pl.pallas_call(
        functools.partial(_sattn_kernel, layer=l, nb=nb, tn=tn, bb=bb),
        grid_spec=grid_spec,
        out_shape=[jax.ShapeDtypeStruct((nb, tn, ATTN_WIDTH), F32), cache_shape, cache_shape],
        compiler_params=pltpu.CompilerParams(
            dimension_semantics=("arbitrary",),
            vmem_limit_bytes=VMEM_LIMIT),
        name="sample_attn",
    )(*args)


def kernel(x_prompt, x_sample, cache_k, cache_v, state_pool, c_prompt, c_sample,
           w_ada, b_ada, norm1_g, w_in, w_pool, pool_scale, attn_sinks, w_a, w_b,
           w_out, norm2_g, w_up, w_down, final_g):
    bp, tp, _ = x_prompt.shape
    nb, tn, _ = x_sample.shape
    assert tp % MIX_ROWS == 0 and tp % FFN_ROWS == 0 and tn <= 8 and nb == LANES
    assert bp % 8 == 0

    mod_p, mod_s, *mix_w = _mod_call(c_prompt, c_sample, w_ada, b_ada,
                                     [(w, 0) for w in (w_in, w_b, w_out)])
    wpa = _poolw_call(w_pool, pool_scale, w_a)
    bias = jnp.asarray(_prompt_bias())
    fg = final_g.reshape(1, D_MODEL)
    g1, g2 = norm1_g, norm2_g

    xs = x_sample
    prefix_t = state_pool.transpose(0, 2, 1, 3)
    ck = cache_k.transpose(0, 1, 3, 4, 2)
    cv = cache_v.transpose(0, 1, 3, 4, 2)

    xp = x_prompt
    kp, vp, pp, ps, rolled = [], [], [], [], None
    for l in range(DEPTH):
        last = l == DEPTH - 1
        win, wb, wout = mix_w
        xp, nk, nv, npool, wup, wdn = _pmix_call(xp, mod_p, g1, win, wpa, wb, wout,
                                                 attn_sinks, bias, l, [(w_up, l), (w_down, l)])
        kp.append(nk); vp.append(nv); pp.append(npool)

        q, kn, vn, ma, sgb, npool_s = _sproj_call(xs, mod_s, g1, win, wpa, prefix_t, l, nb, tn)
        o, *rolled = _sattn_call(q, kn, vn, ck, cv, attn_sinks, rolled, l, nb, tn)
        ps.append(npool_s)

        nxt = [] if last else [(w, l + 1) for w in (w_in, w_b, w_out)]
        xp, xs, *mix_w = _ffn_call(xp, xs, o, ma, sgb, mod_p, mod_s, g2, wup, wdn, wb, wout, fg,
                                   l, last, nxt)

    kv_shape_p = (DEPTH, bp, N_KV_HEADS, HEAD_DIM, WINDOW)
    return (xp,
            xs,
            jnp.stack(kp).reshape(kv_shape_p).transpose(0, 1, 4, 2, 3),
            jnp.stack(vp).reshape(kv_shape_p).transpose(0, 1, 4, 2, 3),
            jnp.stack(pp),
            rolled[0].transpose(0, 1, 4, 2, 3),
            rolled[1].transpose(0, 1, 4, 2, 3),
            jnp.stack(ps).transpose(0, 2, 1, 3))
```

```python
import functools

import numpy as np
import jax
import jax.numpy as jnp
from jax import lax
from jax.experimental import pallas as pl
from jax.experimental.pallas import tpu as pltpu

D_MODEL = 1024
DEPTH = 2
PAST_LEN = 16384
POOL_WIDTH = D_MODEL // 2
POOL_WINDOWS = (2, 4, 8, 16)
POOL_GROUP = POOL_WIDTH // len(POOL_WINDOWS)
POOL_PAD = max(POOL_WINDOWS) - 1
N_HEADS = 8
N_KV_HEADS = 2
HEAD_DIM = 64
GQA_GROUP = N_HEADS // N_KV_HEADS
ATTN_WIDTH = N_HEADS * HEAD_DIM
KV_WIDTH = N_KV_HEADS * HEAD_DIM
WINDOW = 128
ATTN_BLOCK = 128
D_FF = 4 * D_MODEL
RMS_EPS = 1e-6
NEG_INF = -1e30

OFF_U = 0
OFF_Q = OFF_U + POOL_WIDTH
OFF_K = OFF_Q + ATTN_WIDTH
OFF_V = OFF_K + KV_WIDTH
OFF_GA = OFF_V + KV_WIDTH
OFF_GB = OFF_GA + D_MODEL
IN_WIDTH = OFF_GB + D_MODEL
MOD_SLAB = 3 * D_MODEL

LANES = 128
HIST = 16
assert all(w & (w - 1) == 0 for w in POOL_WINDOWS) and list(POOL_WINDOWS) == sorted(POOL_WINDOWS)
assert HIST >= POOL_PAD
VMEM_LIMIT = 56 * 1024 * 1024
SUB_ROWS = 512
MIX_ROWS = 2 * SUB_ROWS
FFN_SUB_ROWS = 512
FFN_ROWS = 2 * FFN_SUB_ROWS
SATTN_BATCH = 16

F32 = jnp.float32
BF16 = jnp.bfloat16


def _bdot(a, b):
    return jnp.dot(a.astype(BF16), b.astype(BF16), preferred_element_type=F32)


def _rms_mod(x, g, sc, sh):
    ms = jnp.mean(x * x, axis=-1, keepdims=True)
    return (x * lax.rsqrt(ms + RMS_EPS) * g) * (1.0 + sc) + sh


def _alibi_slopes():
    return 2.0 ** (-8.0 * (np.arange(N_HEADS) + 1) / N_HEADS)


def _const_spec(shape):
    nd = len(shape)
    return pl.BlockSpec(shape, lambda *_: (0,) * nd, pipeline_mode=pl.Buffered(1))


def _layer_spec(shape, l):
    nd = len(shape)
    return pl.BlockSpec((None,) + tuple(shape), lambda *_: (l,) + (0,) * nd,
                        pipeline_mode=pl.Buffered(1))


def _whole_out_spec(shape):
    nd = len(shape)
    return pl.BlockSpec(shape, lambda *_: (0,) * nd)


def _weight_spec(w, shape, l):
    return _const_spec(shape) if w.shape == tuple(shape) else _layer_spec(shape, l)


def _convert_specs(jobs, n_steps, step_of):
    in_specs, out_specs, out_shapes = [], [], []
    for w, l in jobs:
        _, r, c = w.shape
        rows = r // n_steps
        assert rows * n_steps == r and rows % 16 == 0, (w.shape, n_steps)
        in_specs.append(pl.BlockSpec((None, rows, c), lambda *a, l=l: (l, step_of(*a), 0)))
        out_specs.append(pl.BlockSpec((rows, c), lambda *a: (step_of(*a), 0)))
        out_shapes.append(jax.ShapeDtypeStruct((r, c), BF16))
    return in_specs, out_specs, out_shapes


def _convert_slabs(src_refs, dst_refs):
    for src, dst in zip(src_refs, dst_refs):
        dst[...] = src[...].astype(BF16)


def _mod_kernel(cp_ref, cs_ref, w_ref, b_ref, *rest, n_cvt):
    cvt_in, (op_ref, os_ref), cvt_out = rest[:n_cvt], rest[n_cvt:n_cvt + 2], rest[n_cvt + 2:]
    _convert_slabs(cvt_in, cvt_out)
    w = w_ref[...].astype(BF16)
    b = b_ref[pl.ds(pl.program_id(0), 1), :]
    for c_ref, o_ref in ((cp_ref, op_ref), (cs_ref, os_ref)):
        c = c_ref[...]
        s = (c * jax.nn.sigmoid(c)).astype(BF16)
        o_ref[...] = jnp.dot(s, w, preferred_element_type=F32) + b


def _mod_call(c_p, c_s, w_ada, b_ada, cvt_jobs):
    tn = 1536
    n = 6 * D_MODEL
    nj = n // tn
    mp, ms = c_p.shape[0], c_s.shape[0]
    cvt_in, cvt_out, cvt_shapes = _convert_specs(cvt_jobs, DEPTH * nj, lambda l, j: l * nj + j)
    return pl.pallas_call(
        functools.partial(_mod_kernel, n_cvt=len(cvt_jobs)),
        grid=(DEPTH, nj),
        in_specs=[
            _const_spec((mp, D_MODEL)),
            _const_spec((ms, D_MODEL)),
            pl.BlockSpec((None, D_MODEL, tn), lambda l, j: (l, 0, j)),
            pl.BlockSpec((DEPTH, tn), lambda l, j: (0, j)),
        ] + cvt_in,
        out_specs=[
            pl.BlockSpec((None, mp, tn), lambda l, j: (l, 0, j)),
            pl.BlockSpec((None, ms, tn), lambda l, j: (l, 0, j)),
        ] + cvt_out,
        out_shape=[jax.ShapeDtypeStruct((DEPTH, mp, n), F32),
                   jax.ShapeDtypeStruct((DEPTH, ms, n), F32)] + cvt_shapes,
        compiler_params=pltpu.CompilerParams(
            dimension_semantics=("arbitrary", "arbitrary"),
            vmem_limit_bytes=VMEM_LIMIT),
        name="adaln_mod",
    )(c_p, c_s, w_ada, b_ada, *[w for w, _ in cvt_jobs])


def _poolw_kernel(wp_ref, ps_ref, wa_ref, o_ref):
    scale = ps_ref[pl.ds(pl.program_id(0), 1), :]
    for g in range(len(POOL_WINDOWS)):
        rows = slice(g * POOL_GROUP, (g + 1) * POOL_GROUP)
        o_ref[rows, :] = jnp.dot(wp_ref[g] * scale[:, rows], wa_ref[rows, :],
                                 preferred_element_type=F32,
                                 precision=lax.Precision.HIGHEST).astype(BF16)


def _poolw_call(w_pool, pool_scale, w_a):
    return pl.pallas_call(
        _poolw_kernel,
        grid=(DEPTH,),
        in_specs=[
            pl.BlockSpec((None,) + w_pool.shape[1:], lambda l: (l, 0, 0, 0)),
            pl.BlockSpec((DEPTH, POOL_WIDTH), lambda l: (0, 0)),
            pl.BlockSpec((None, POOL_WIDTH, D_MODEL), lambda l: (l, 0, 0)),
        ],
        out_specs=pl.BlockSpec((None, POOL_WIDTH, D_MODEL), lambda l: (l, 0, 0)),
        out_shape=jax.ShapeDtypeStruct((DEPTH, POOL_WIDTH, D_MODEL), BF16),
        compiler_params=pltpu.CompilerParams(
            dimension_semantics=("arbitrary",),
            vmem_limit_bytes=VMEM_LIMIT),
        name="pool_weights",
    )(w_pool, pool_scale, w_a)


def _prompt_bias():
    i = np.arange(ATTN_BLOCK)[:, None]
    j = np.arange(2 * ATTN_BLOCK)[None, :]
    dist = i + ATTN_BLOCK - j
    valid = (dist >= 0) & (dist < WINDOW)
    valid_first = valid & (j >= ATTN_BLOCK)
    sl = _alibi_slopes()[:, None, None]
    b = np.where(valid[None], -sl * dist[None], NEG_INF)
    b0 = np.where(valid_first[None], -sl * dist[None], NEG_INF)
    return np.stack([b, b0]).astype(np.float32)


def _pmix_kernel(sinks_ref, x_ref, mod_ref, g1_ref, win_ref, wpa_ref,
                 wb_ref, wout_ref, bias_ref, *rest, layer, n_cvt):
    cvt_in, rest = rest[:n_cvt], rest[n_cvt:]
    (xo_ref, nk_ref, nv_ref, npool_ref), rest = rest[:4], rest[4:]
    cvt_out, (ubuf, ka, kb, va, vb, obuf) = rest[:n_cvt], rest[n_cvt:]
    _convert_slabs(cvt_in, cvt_out)
    tm, sub = MIX_ROWS, SUB_ROWS
    bi = pl.program_id(0)
    t = pl.program_id(1)
    nt = pl.num_programs(1)

    @pl.when(t == 0)
    def _init():
        ubuf[0:HIST, :] = jnp.zeros((HIST, POOL_WIDTH), F32)
        zero = jnp.zeros((N_KV_HEADS, ATTN_BLOCK, LANES), BF16)
        ka[:, 0:ATTN_BLOCK, :] = zero
        kb[:, 0:ATTN_BLOCK, :] = zero
        va[:, 0:ATTN_BLOCK, :] = zero
        vb[:, 0:ATTN_BLOCK, :] = zero

    mod = mod_ref[pl.ds(bi, 1), :]
    sh1 = mod[:, 0:D_MODEL]
    sc1 = mod[:, D_MODEL:2 * D_MODEL]
    gt1 = mod[:, 2 * D_MODEL:3 * D_MODEL]
    lane = lax.broadcasted_iota(jnp.int32, (sub, LANES), 1)
    low = lane < HEAD_DIM
    low_q = lax.broadcasted_iota(jnp.int32, (ATTN_BLOCK, LANES), 1) < HEAD_DIM
    nt_dims = (((1,), (1,)), ((), ()))
    st = [dict(ro=i * sub) for i in range(tm // sub)]

    def norm(c):
        c["x"] = x_ref[c["ro"]:c["ro"] + sub, :]
        c["h"] = _rms_mod(c["x"], g1_ref[layer:layer + 1, :], sc1, sh1).astype(BF16)

    def proj(c):
        h = c["h"]
        c["u"] = jnp.dot(h, win_ref[:, OFF_U:OFF_Q], preferred_element_type=F32)
        q = jnp.dot(h, win_ref[:, OFF_Q:OFF_K], preferred_element_type=F32)
        c["qb"] = (q * (HEAD_DIM ** -0.5)).astype(BF16)
        kv = jnp.dot(h, win_ref[:, OFF_K:OFF_GA], preferred_element_type=F32)
        c["k"] = kv[:, 0:KV_WIDTH]
        c["v"] = kv[:, KV_WIDTH:2 * KV_WIDTH]

    def pool_sums(c):
        ro, u = c["ro"], c["u"]
        ubuf[HIST + ro:HIST + ro + sub, :] = u
        pos = t * tm + ro + lax.broadcasted_iota(jnp.int32, (sub, 1), 0)
        cur = ubuf[ro:ro + HIST + sub, :]
        d, w = [], 1
        for g, wg in enumerate(POOL_WINDOWS):
            while w < wg:
                cur = cur + pltpu.roll(cur, w, axis=0)
                w *= 2
            ug = u[:, g * POOL_GROUP:(g + 1) * POOL_GROUP]
            cnt = jnp.minimum(wg, pos + 1).astype(F32)
            d.append((cur[HIST:, 0:POOL_GROUP] / cnt - ug).astype(BF16))
            if g + 1 < len(POOL_WINDOWS):
                cur = cur[:, POOL_GROUP:]
        c["d"] = d

    def kv_store(c):
        r = ATTN_BLOCK + c["ro"]
        zero = jnp.zeros((sub, LANES), BF16)
        for x, xa, xb in ((c["k"], ka, kb), (c["v"], va, vb)):
            x16 = x.astype(BF16)
            xr16 = pltpu.roll(x, HEAD_DIM, axis=1).astype(BF16)
            xa[0, r:r + sub, :] = jnp.where(low, x16, zero)
            xb[0, r:r + sub, :] = jnp.where(low, zero, xr16)
            xa[1, r:r + sub, :] = jnp.where(low, xr16, zero)
            xb[1, r:r + sub, :] = jnp.where(low, zero, x16)

    def pool_proj(c):
        c["br_a"] = jnp.dot(jnp.concatenate(c["d"], axis=1), wpa_ref[...],
                            preferred_element_type=F32)

    def units(c):
        j0 = c["ro"] // ATTN_BLOCK
        return [(j, hk, pr) for j in range(j0, j0 + sub // ATTN_BLOCK)
                for hk in range(N_KV_HEADS) for pr in range(GQA_GROUP // 2)]

    def scores(c):
        sc = {}
        for (j, hk, pr) in units(c):
            r0 = j * ATTN_BLOCK
            c0 = (hk * (GQA_GROUP // 2) + pr) * LANES
            q2 = c["qb"][r0 - c["ro"]:r0 - c["ro"] + ATTN_BLOCK, c0:c0 + LANES]
            ks = (ka[hk, r0:r0 + 2 * ATTN_BLOCK, :], kb[hk, r0:r0 + 2 * ATTN_BLOCK, :])
            for e in range(2):
                sc[(j, hk, pr, e)] = lax.dot_general(q2, ks[e], nt_dims,
                                                     preferred_element_type=F32)
        c["s"] = sc

    def gate_a(c):
        c["ga"] = jnp.dot(c["h"], win_ref[:, OFF_GA:OFF_GB], preferred_element_type=F32)

    def gate_b(c):
        c["gb"] = jnp.dot(c["h"], win_ref[:, OFF_GB:IN_WIDTH], preferred_element_type=F32)

    def softmax(c):
        probs, dens = {}, {}
        for (j, hk, pr) in units(c):
            for e in range(2):
                head = hk * GQA_GROUP + pr * 2 + e
                if j == 0:
                    bias = jnp.where(t == 0, bias_ref[1, head], bias_ref[0, head])
                else:
                    bias = bias_ref[0, head]
                s = c["s"][(j, hk, pr, e)] + bias
                sink = sinks_ref[layer, head]
                m = jnp.maximum(jnp.max(s, axis=-1, keepdims=True), sink)
                p = jnp.exp(s - m)
                dens[(j, hk, pr, e)] = jnp.sum(p, axis=-1, keepdims=True) + jnp.exp(sink - m)
                probs[(j, hk, pr, e)] = p.astype(BF16)
        c["p"], c["den"] = probs, dens

    def values(c):
        for (j, hk, pr) in units(c):
            r0 = j * ATTN_BLOCK
            c0 = (hk * (GQA_GROUP // 2) + pr) * LANES
            vs = (va[hk, r0:r0 + 2 * ATTN_BLOCK, :], vb[hk, r0:r0 + 2 * ATTN_BLOCK, :])
            o2 = (jnp.dot(c["p"][(j, hk, pr, 0)], vs[0], preferred_element_type=F32)
                  + jnp.dot(c["p"][(j, hk, pr, 1)], vs[1], preferred_element_type=F32))
            den = jnp.where(low_q, c["den"][(j, hk, pr, 0)], c["den"][(j, hk, pr, 1)])
            obuf[r0:r0 + ATTN_BLOCK, c0:c0 + LANES] = o2 / den

    def tail(c):
        ro = c["ro"]
        br_b = _bdot(obuf[ro:ro + sub, :], wb_ref[...])
        merged = jax.nn.sigmoid(c["ga"]) * c["br_a"] + jax.nn.sigmoid(c["gb"]) * br_b
        xo_ref[ro:ro + sub, :] = c["x"] + gt1 * _bdot(merged, wout_ref[...])

    stages = [
        (norm,),
        (proj, gate_a),
        (pool_sums, kv_store),
        (pool_proj, scores, gate_b),
        (softmax,),
        (values,),
        (tail,),
    ]
    order = sorted((2 * s + 3 * i, i, s) for i in range(len(st)) for s in range(len(stages)))
    for _, i, s in order:
        for fn in stages[s]:
            fn(st[i])

    @pl.when(t == nt - 1)
    def _state():
        nk_ref[...] = st[-1]["k"][sub - WINDOW:, :].T
        nv_ref[...] = st[-1]["v"][sub - WINDOW:, :].T
        npool_ref[...] = ubuf[HIST + tm - POOL_PAD:HIST + tm, :]

    ubuf[0:HIST, :] = ubuf[tm:tm + HIST, :]
    for buf in (ka, kb, va, vb):
        buf[:, 0:ATTN_BLOCK, :] = buf[:, tm:tm + ATTN_BLOCK, :]


def _pmix_call(x, mod_p, g1, win, wpa, wb, wout, sinks, bias, l, cvt_jobs):
    b, t, _ = x.shape
    tm = MIX_ROWS
    nt = t // tm
    cvt_in, cvt_out, cvt_shapes = _convert_specs(cvt_jobs, b * nt, lambda i, j, s: i * nt + j)
    grid_spec = pltpu.PrefetchScalarGridSpec(
        num_scalar_prefetch=1,
        grid=(b, nt),
        in_specs=[
            pl.BlockSpec((None, tm, D_MODEL), lambda i, j, s: (i, j, 0)),
            pl.BlockSpec((None, b, MOD_SLAB), lambda i, j, s: (l, 0, 0),
                         pipeline_mode=pl.Buffered(1)),
            _const_spec((DEPTH, D_MODEL)),
            _weight_spec(win, (D_MODEL, IN_WIDTH), l),
            _weight_spec(wpa, (POOL_WIDTH, D_MODEL), l),
            _weight_spec(wb, (ATTN_WIDTH, D_MODEL), l),
            _weight_spec(wout, (D_MODEL, D_MODEL), l),
            _const_spec((2, N_HEADS, ATTN_BLOCK, 2 * ATTN_BLOCK)),
        ] + cvt_in,
        out_specs=[
            pl.BlockSpec((None, tm, D_MODEL), lambda i, j, s: (i, j, 0)),
            pl.BlockSpec((None, KV_WIDTH, WINDOW), lambda i, j, s: (i, 0, 0)),
            pl.BlockSpec((None, KV_WIDTH, WINDOW), lambda i, j, s: (i, 0, 0)),
            pl.BlockSpec((None, POOL_PAD, POOL_WIDTH), lambda i, j, s: (i, 0, 0)),
        ] + cvt_out,
        scratch_shapes=[
            pltpu.VMEM((HIST + tm, POOL_WIDTH), F32),
            pltpu.VMEM((N_KV_HEADS, ATTN_BLOCK + tm, LANES), BF16),
            pltpu.VMEM((N_KV_HEADS, ATTN_BLOCK + tm, LANES), BF16),
            pltpu.VMEM((N_KV_HEADS, ATTN_BLOCK + tm, LANES), BF16),
            pltpu.VMEM((N_KV_HEADS, ATTN_BLOCK + tm, LANES), BF16),
            pltpu.VMEM((tm, ATTN_WIDTH), F32),
        ],
    )
    return pl.pallas_call(
        functools.partial(_pmix_kernel, layer=l, n_cvt=len(cvt_jobs)),
        grid_spec=grid_spec,
        out_shape=[
            jax.ShapeDtypeStruct((b, t, D_MODEL), F32),
            jax.ShapeDtypeStruct((b, KV_WIDTH, WINDOW), F32),
            jax.ShapeDtypeStruct((b, KV_WIDTH, WINDOW), F32),
            jax.ShapeDtypeStruct((b, POOL_PAD, POOL_WIDTH), F32),
        ] + cvt_shapes,
        compiler_params=pltpu.CompilerParams(
            dimension_semantics=("arbitrary", "arbitrary"),
            vmem_limit_bytes=VMEM_LIMIT),
        name="prompt_mixer",
    )(sinks, x, mod_p, g1, win, wpa, wb, wout, bias, *[w for w, _ in cvt_jobs])


def _ffn_kernel(xp_ref, xs_ref, modp_ref, mods_ref, g2_ref, wup_ref, wdn_ref, fg_ref,
                so_ref, ma_ref, sgb_ref, gt1_ref, wb_ref, wout_ref,
                *rest, layer, final, n_prompt, tiles_per_row, reps, n_cvt):
    cvt_in, (op_ref, os_ref), cvt_out = rest[:n_cvt], rest[n_cvt:n_cvt + 2], rest[n_cvt + 2:]
    _convert_slabs(cvt_in, cvt_out)
    i = pl.program_id(0)

    def ffn(load, n_rows, o_ref, mod):
        sh2, sc2, gt2 = (mod[:, k * D_MODEL:(k + 1) * D_MODEL] for k in range(3))
        subs = range(0, n_rows, FFN_SUB_ROWS)
        rows = lambda a, r: a if a.shape[0] == 1 else a[r:r + FFN_SUB_ROWS]
        norm = lambda r: _rms_mod(load(r), g2_ref[layer:layer + 1, :],
                                  rows(sc2, r), rows(sh2, r)).astype(BF16)
        h2 = norm(0)
        for r in subs:
            ff = jnp.dot(h2, wup_ref[...], preferred_element_type=F32)
            if r + FFN_SUB_ROWS in subs:
                h2 = norm(r + FFN_SUB_ROWS)
            ff = jnp.square(jnp.maximum(ff, 0.0))
            y = load(r) + rows(gt2, r) * _bdot(ff, wdn_ref[...])
            if final:
                ms = jnp.mean(y * y, axis=-1, keepdims=True)
                y = y * lax.rsqrt(ms + RMS_EPS) * fg_ref[...]
            if len(o_ref.shape) == 3:
                o_ref[...] = jnp.swapaxes(y.reshape(reps, y.shape[0] // reps, D_MODEL), 0, 1)
            else:
                o_ref[r:r + FFN_SUB_ROWS, :] = y

    @pl.when(i < n_prompt)
    def _prompt():
        ffn(lambda r: xp_ref[r:r + FFN_SUB_ROWS, :], xp_ref.shape[0], op_ref,
            modp_ref[pl.ds(i // tiles_per_row, 1), :])

    @pl.when(i == n_prompt)
    def _sample():
        x = xs_ref[...]
        if x.ndim == 3:
            x = jnp.swapaxes(x, 0, 1).reshape(ma_ref.shape)
        gt1 = jnp.concatenate([gt1_ref[...]] * reps, axis=0)
        so = jnp.swapaxes(so_ref[...], 0, 1).reshape(ma_ref.shape[0], ATTN_WIDTH)
        br_b = _bdot(so, wb_ref[...])
        merged = ma_ref[...] + sgb_ref[...] * br_b
        x = x + gt1 * _bdot(merged, wout_ref[...])
        ffn(lambda r: x[r:r + FFN_SUB_ROWS, :], x.shape[0], os_ref,
            jnp.concatenate([mods_ref[...]] * reps, axis=0))


def _ffn_call(xp, xs, so, ma, sgb, mod_p, mod_s, g2, wup, wdn, wb, wout, fg, l, final, cvt_jobs):
    b, t, _ = xp.shape
    ms = ma.shape[0]
    nb = mod_s.shape[1]
    tm = FFN_ROWS
    tpr = t // tm
    n_prompt = b * tpr
    last = n_prompt - 1

    def p_idx(i):
        ii = jnp.minimum(i, last)
        return (ii // tpr, ii % tpr, 0)

    assert ms <= FFN_SUB_ROWS
    s_shape = (nb, ms // nb, D_MODEL) if final else (ms, D_MODEL)
    cvt_in, cvt_out, cvt_shapes = _convert_specs(cvt_jobs, n_prompt, lambda i: jnp.minimum(i, last))
    return pl.pallas_call(
        functools.partial(_ffn_kernel, layer=l, final=final, n_prompt=n_prompt, tiles_per_row=tpr,
                          reps=ms // nb, n_cvt=len(cvt_jobs)),
        grid=(n_prompt + 1,),
        in_specs=[
            pl.BlockSpec((None, tm, D_MODEL), p_idx),
            _const_spec(xs.shape),
            pl.BlockSpec((None, b, MOD_SLAB), lambda i: (l, 0, 1), pipeline_mode=pl.Buffered(1)),
            pl.BlockSpec((None, nb, MOD_SLAB), lambda i: (l, 0, 1), pipeline_mode=pl.Buffered(1)),
            _const_spec((DEPTH, D_MODEL)),
            _weight_spec(wup, (D_MODEL, D_FF), l),
            _weight_spec(wdn, (D_FF, D_MODEL), l),
            _const_spec((1, D_MODEL)),
            _const_spec(so.shape), _const_spec(ma.shape), _const_spec(sgb.shape),
            pl.BlockSpec((None, nb, D_MODEL), lambda i: (l, 0, 2), pipeline_mode=pl.Buffered(1)),
            _weight_spec(wb, (ATTN_WIDTH, D_MODEL), l),
            _weight_spec(wout, (D_MODEL, D_MODEL), l),
        ] + cvt_in,
        out_specs=[
            pl.BlockSpec((None, tm, D_MODEL), p_idx),
            _whole_out_spec(s_shape),
        ] + cvt_out,
        out_shape=[jax.ShapeDtypeStruct((b, t, D_MODEL), F32),
                   jax.ShapeDtypeStruct(s_shape, F32)] + cvt_shapes,
        compiler_params=pltpu.CompilerParams(
            dimension_semantics=("arbitrary",),
            vmem_limit_bytes=VMEM_LIMIT),
        name="ffn_final" if final else "ffn",
    )(xp, xs, mod_p, mod_s, g2, wup, wdn, fg, so, ma, sgb, mod_s, wb, wout,
      *[w for w, _ in cvt_jobs])


def _sproj_kernel(x_ref, mod_ref, g1_ref, win_ref, wpa_ref, pre_ref, *rest, layer, nb, tn):
    if layer:
        npool_ref_prev, rest = rest[0], rest[1:]
    q_ref, k_ref, v_ref, ma_ref, sgb_ref, npool_ref = rest
    if layer:
        npool_ref[0:layer] = npool_ref_prev[...]
    x = x_ref[...]
    if x.ndim == 3:
        x = jnp.swapaxes(x, 0, 1).reshape(tn * nb, D_MODEL)
    mod = jnp.concatenate([mod_ref[:, 0:2 * D_MODEL]] * tn, axis=0)
    sh1 = mod[:, 0:D_MODEL]
    sc1 = mod[:, D_MODEL:2 * D_MODEL]
    h = _rms_mod(x, g1_ref[layer:layer + 1, :], sc1, sh1).astype(BF16)

    u = jnp.dot(h, win_ref[:, OFF_U:OFF_Q], preferred_element_type=F32)
    def up(r):
        if r < POOL_PAD:
            return pre_ref[r]
        r -= POOL_PAD
        return u[r * nb:(r + 1) * nb, :]

    d_rows = []
    for tt in range(tn):
        parts = []
        for g, w in enumerate(POOL_WINDOWS):
            lo = g * POOL_GROUP
            acc = up(POOL_PAD + tt)[:, lo:lo + POOL_GROUP]
            for s in range(1, w):
                acc = acc + up(POOL_PAD + tt - s)[:, lo:lo + POOL_GROUP]
            cnt = float(min(w, PAST_LEN + tt + 1))
            parts.append(acc / cnt - up(POOL_PAD + tt)[:, lo:lo + POOL_GROUP])
        d_rows.append(parts)
    d = jnp.concatenate([jnp.concatenate(parts, axis=1) for parts in d_rows], axis=0)
    br_a = _bdot(d, wpa_ref[...])
    for r in range(POOL_PAD):
        npool_ref[layer, r] = up(r + tn)

    q = jnp.dot(h, win_ref[:, OFF_Q:OFF_K], preferred_element_type=F32)
    q_ref[...] = jnp.swapaxes((q * (HEAD_DIM ** -0.5)).reshape(tn, nb, ATTN_WIDTH), 0, 1)
    kv = jnp.dot(h, win_ref[:, OFF_K:OFF_GA], preferred_element_type=F32)
    kv = jnp.swapaxes(kv.reshape(tn, nb, 2 * KV_WIDTH), 0, 1).reshape(nb * tn, 2 * KV_WIDTH)
    k_ref[...] = kv[:, 0:KV_WIDTH]
    v_ref[...] = kv[:, KV_WIDTH:2 * KV_WIDTH]
    ga = jnp.dot(h, win_ref[:, OFF_GA:OFF_GB], preferred_element_type=F32)
    ma_ref[...] = jax.nn.sigmoid(ga) * br_a
    gb = jnp.dot(h, win_ref[:, OFF_GB:IN_WIDTH], preferred_element_type=F32)
    sgb_ref[...] = jax.nn.sigmoid(gb)


def _sproj_call(x, mod_s, g1, win, wpa, prefix_t, prev_pool, l, nb, tn):
    m = tn * nb
    extra = [] if prev_pool is None else [prev_pool]
    shapes = [
        jax.ShapeDtypeStruct((nb, tn, ATTN_WIDTH), F32),
        jax.ShapeDtypeStruct((m, KV_WIDTH), F32),
        jax.ShapeDtypeStruct((m, KV_WIDTH), F32),
        jax.ShapeDtypeStruct((m, D_MODEL), F32),
        jax.ShapeDtypeStruct((m, D_MODEL), F32),
        jax.ShapeDtypeStruct((l + 1, POOL_PAD, nb, POOL_WIDTH), F32),
    ]
    return pl.pallas_call(
        functools.partial(_sproj_kernel, layer=l, nb=nb, tn=tn),
        grid=(1,),
        in_specs=[
            _const_spec(x.shape),
            pl.BlockSpec((None, nb, MOD_SLAB), lambda i: (l, 0, 0), pipeline_mode=pl.Buffered(1)),
            _const_spec((DEPTH, D_MODEL)),
            _weight_spec(win, (D_MODEL, IN_WIDTH), l),
            _weight_spec(wpa, (POOL_WIDTH, D_MODEL), l),
            _layer_spec((POOL_PAD, nb, POOL_WIDTH), l),
        ] + [_const_spec(a.shape) for a in extra],
        out_specs=[_whole_out_spec(s.shape) for s in shapes],
        out_shape=shapes,
        compiler_params=pltpu.CompilerParams(
            dimension_semantics=("arbitrary",),
            vmem_limit_bytes=VMEM_LIMIT),
        name="sample_proj",
    )(x, mod_s, g1, win, wpa, prefix_t, *extra)


def _sample_bias(tn):
    t = np.arange(tn)[:, None]
    lane = np.arange(2 * WINDOW)[None, :]
    cached = lane < WINDOW
    new_t = lane - (2 * WINDOW - tn)
    dist = np.where(cached, t + WINDOW - lane, t - new_t)
    valid = np.where(cached, (dist >= 0) & (dist < WINDOW), (new_t >= 0) & (dist >= 0))
    sl = _alibi_slopes()[:, None, None]
    return np.where(valid[None], -sl * dist[None], NEG_INF).astype(np.float32)


def _sattn_kernel(sinks_ref, q_ref, kc_ref, vc_ref, kn_ref, vn_ref, bias_ref, *rest,
                   layer, nb, tn, bb):
    if layer:
        pk_ref, pv_ref = rest[:2]
        rest = rest[2:]
    o_ref, nk_ref, nv_ref, wk, wv = rest
    i = pl.program_id(0)
    rows = GQA_GROUP * tn
    keep = WINDOW - tn

    @pl.when(i == 0)
    def _new_rows():
        for c0 in range(0, nb * tn, LANES):
            wk[:, c0:c0 + LANES] = kn_ref[c0:c0 + LANES, :].T
            wv[:, c0:c0 + LANES] = vn_ref[c0:c0 + LANES, :].T

    if layer:
        nk_ref[0:layer] = pk_ref[...]
        nv_ref[0:layer] = pv_ref[...]

    lane0 = i * (bb * tn)
    tile0 = pl.multiple_of((lane0 // LANES) * LANES, LANES)
    off0 = lane0 % LANES
    wkt = wk[:, pl.ds(tile0, LANES)]
    wvt = wv[:, pl.ds(tile0, LANES)]
    lane = lax.broadcasted_iota(jnp.int32, (HEAD_DIM, WINDOW), 1)
    tail = lane >= keep
    grow = lax.broadcasted_iota(jnp.int32, (rows, 1), 0) // tn
    units = [(bl, hk) for bl in range(bb) for hk in range(N_KV_HEADS)]
    biases, sinks = [], []
    for hk in range(N_KV_HEADS):
        biases.append(jnp.concatenate(
            [bias_ref[hk * GQA_GROUP + g] for g in range(GQA_GROUP)], axis=0))
        sink = jnp.zeros((rows, 1), F32)
        for g in range(GQA_GROUP):
            sink = jnp.where(grow == g, sinks_ref[layer, hk * GQA_GROUP + g], sink)
        sinks.append(sink)

    scores, values = {}, {}
    for bl in range(bb):
        shift = (keep - off0 - bl * tn) % LANES
        nkb = pltpu.roll(wkt, shift, axis=1)
        nvb = pltpu.roll(wvt, shift, axis=1)
        for hk in range(N_KV_HEADS):
            kt = kc_ref[bl, hk]
            vt = vc_ref[bl, hk]
            nkt = jnp.where(tail, nkb[hk * HEAD_DIM:(hk + 1) * HEAD_DIM, :], 0.0)
            nvt = jnp.where(tail, nvb[hk * HEAD_DIM:(hk + 1) * HEAD_DIM, :], 0.0)
            nk_ref[layer, bl, hk] = jnp.where(tail, nkt, pltpu.roll(kt, keep, axis=1))
            nv_ref[layer, bl, hk] = jnp.where(tail, nvt, pltpu.roll(vt, keep, axis=1))
            keys = jnp.concatenate([kt, nkt], axis=1).astype(BF16)
            values[(bl, hk)] = jnp.concatenate([vt, nvt], axis=1).astype(BF16)
            qb = q_ref[bl]
            qu = jnp.concatenate(
                [qb[:, (hk * GQA_GROUP + g) * HEAD_DIM:(hk * GQA_GROUP + g + 1) * HEAD_DIM]
                 for g in range(GQA_GROUP)], axis=0).astype(BF16)
            scores[(bl, hk)] = jnp.dot(qu, keys, preferred_element_type=F32)
    probs, dens = {}, {}
    for (bl, hk) in units:
        s = scores[(bl, hk)] + biases[hk]
        m = jnp.maximum(jnp.max(s, axis=-1, keepdims=True), sinks[hk])
        p = jnp.exp(s - m)
        dens[(bl, hk)] = jnp.sum(p, axis=-1, keepdims=True) + jnp.exp(sinks[hk] - m)
        probs[(bl, hk)] = p.astype(BF16)
    for (bl, hk) in units:
        o = lax.dot_general(probs[(bl, hk)], values[(bl, hk)], (((1,), (1,)), ((), ())),
                            preferred_element_type=F32)
        o = o / dens[(bl, hk)]
        gw = GQA_GROUP * HEAD_DIM
        o_ref[bl, :, hk * gw:(hk + 1) * gw] = jnp.concatenate(
            [o[g * tn:(g + 1) * tn, :] for g in range(GQA_GROUP)], axis=1)


def _sattn_call(q4, kn_bt, vn_bt, ck, cv, sinks, rolled, l, nb, tn):
    bb = SATTN_BATCH
    unit = (bb, N_KV_HEADS, HEAD_DIM, WINDOW)
    cache_shape = jax.ShapeDtypeStruct((l + 1, nb) + unit[1:], F32)
    in_specs = [
        pl.BlockSpec((bb, tn, ATTN_WIDTH), lambda i, s: (i, 0, 0)),
        pl.BlockSpec((None,) + unit, lambda i, s: (l, i, 0, 0, 0)),
        pl.BlockSpec((None,) + unit, lambda i, s: (l, i, 0, 0, 0)),
        _const_spec((nb * tn, KV_WIDTH)),
        _const_spec((nb * tn, KV_WIDTH)),
        _const_spec((N_HEADS, tn, 2 * WINDOW)),
    ]
    args = [sinks, q4, ck, cv, kn_bt, vn_bt, jnp.asarray(_sample_bias(tn))]
    if l:
        in_specs += [pl.BlockSpec((l,) + unit, lambda i, s: (0, i, 0, 0, 0))] * 2
        args += list(rolled)
    grid_spec = pltpu.PrefetchScalarGridSpec(
        num_scalar_prefetch=1,
        grid=(nb // bb,),
        in_specs=in_specs,
        out_specs=[
            pl.BlockSpec((bb, tn, ATTN_WIDTH), lambda i, s: (i, 0, 0)),
            pl.BlockSpec((l + 1,) + unit, lambda i, s: (0, i, 0, 0, 0)),
            pl.BlockSpec((l + 1,) + unit, lambda i, s: (0, i, 0, 0, 0)),
        ],
        scratch_shapes=[
            pltpu.VMEM((KV_WIDTH, nb * tn), F32),
            pltpu.VMEM((KV_WIDTH, nb * tn), F32),
        ],
    )
    return pl.pallas_call(
        functools.partial(_sattn_kernel, layer=l, nb=nb, tn=tn, bb=bb),
        grid_spec=grid_spec,
        out_shape=[jax.ShapeDtypeStruct((nb, tn, ATTN_WIDTH), F32), cache_shape, cache_shape],
        compiler_params=pltpu.CompilerParams(
            dimension_semantics=("arbitrary",),
            vmem_limit_bytes=VMEM_LIMIT),
        name="sample_attn",
    )(*args)


def kernel(x_prompt, x_sample, cache_k, cache_v, state_pool, c_prompt, c_sample,
           w_ada, b_ada, norm1_g, w_in, w_pool, pool_scale, attn_sinks, w_a, w_b,
           w_out, norm2_g, w_up, w_down, final_g):
    bp, tp, _ = x_prompt.shape
    nb, tn, _ = x_sample.shape
    assert tp % MIX_ROWS == 0 and tp % FFN_ROWS == 0 and tn <= 8 and nb == LANES
    assert bp % 8 == 0

    mod_p, mod_s, *mix_w = _mod_call(c_prompt, c_sample, w_ada, b_ada,
                                     [(w, 0) for w in (w_in, w_b, w_out)])
    wpa = _poolw_call(w_pool, pool_scale, w_a)
    bias = jnp.asarray(_prompt_bias())
    fg = final_g.reshape(1, D_MODEL)
    g1, g2 = norm1_g, norm2_g

    xs = x_sample
    prefix_t = state_pool.transpose(0, 2, 1, 3)
    ck = cache_k.transpose(0, 1, 3, 4, 2)
    cv = cache_v.transpose(0, 1, 3, 4, 2)

    xp = x_prompt
    kp, vp, pp, npool_s, rolled = [], [], [], None, None
    for l in range(DEPTH):
        last = l == DEPTH - 1
        win, wb, wout = mix_w
        xp, nk, nv, npool, wup, wdn = _pmix_call(xp, mod_p, g1, win, wpa, wb, wout,
                                                 attn_sinks, bias, l, [(w_up, l), (w_down, l)])
        kp.append(nk); vp.append(nv); pp.append(npool)

        q, kn, vn, ma, sgb, npool_s = _sproj_call(xs, mod_s, g1, win, wpa, prefix_t, npool_s,
                                                  l, nb, tn)
        o, *rolled = _sattn_call(q, kn, vn, ck, cv, attn_sinks, rolled, l, nb, tn)

        nxt = [] if last else [(w, l + 1) for w in (w_in, w_b, w_out)]
        xp, xs, *mix_w = _ffn_call(xp, xs, o, ma, sgb, mod_p, mod_s, g2, wup, wdn, wb, wout, fg,
                                   l, last, nxt)

    kv_shape_p = (DEPTH, bp, N_KV_HEADS, HEAD_DIM, WINDOW)
    return (xp,
            xs,
            jnp.stack(kp).reshape(kv_shape_p).transpose(0, 1, 4, 2, 3),
            jnp.stack(vp).reshape(kv_shape_p).transpose(0, 1, 4, 2, 3),
            jnp.stack(pp),
            rolled[0].transpose(0, 1, 4, 2, 3),
            rolled[1].transpose(0, 1, 4, 2, 3),
            npool_s.transpose(0, 2, 1, 3))
```

```python
import functools

import numpy as np
import jax
import jax.numpy as jnp
from jax import lax
from jax.experimental import pallas as pl
from jax.experimental.pallas import tpu as pltpu

D_MODEL = 1024
DEPTH = 2
PAST_LEN = 16384
POOL_WIDTH = D_MODEL // 2
POOL_WINDOWS = (2, 4, 8, 16)
POOL_GROUP = POOL_WIDTH // len(POOL_WINDOWS)
POOL_PAD = max(POOL_WINDOWS) - 1
N_HEADS = 8
N_KV_HEADS = 2
HEAD_DIM = 64
GQA_GROUP = N_HEADS // N_KV_HEADS
ATTN_WIDTH = N_HEADS * HEAD_DIM
KV_WIDTH = N_KV_HEADS * HEAD_DIM
WINDOW = 128
ATTN_BLOCK = 128
D_FF = 4 * D_MODEL
RMS_EPS = 1e-6
NEG_INF = -1e30

OFF_U = 0
OFF_Q = OFF_U + POOL_WIDTH
OFF_K = OFF_Q + ATTN_WIDTH
OFF_V = OFF_K + KV_WIDTH
OFF_GA = OFF_V + KV_WIDTH
OFF_GB = OFF_GA + D_MODEL
IN_WIDTH = OFF_GB + D_MODEL
MOD_SLAB = 3 * D_MODEL

LANES = 128
HIST = 16
assert all(w & (w - 1) == 0 for w in POOL_WINDOWS) and list(POOL_WINDOWS) == sorted(POOL_WINDOWS)
assert HIST >= POOL_PAD
VMEM_LIMIT = 56 * 1024 * 1024
SUB_ROWS = 512
MIX_ROWS = 2 * SUB_ROWS
FFN_SUB_ROWS = 512
FFN_ROWS = 2 * FFN_SUB_ROWS
SATTN_BATCH = 32

F32 = jnp.float32
BF16 = jnp.bfloat16


def _bdot(a, b):
    return jnp.dot(a.astype(BF16), b.astype(BF16), preferred_element_type=F32)


def _rms_mod(x, g, sc, sh):
    ms = jnp.mean(x * x, axis=-1, keepdims=True)
    return (x * lax.rsqrt(ms + RMS_EPS) * g) * (1.0 + sc) + sh


def _alibi_slopes():
    return 2.0 ** (-8.0 * (np.arange(N_HEADS) + 1) / N_HEADS)


def _const_spec(shape):
    nd = len(shape)
    return pl.BlockSpec(shape, lambda *_: (0,) * nd, pipeline_mode=pl.Buffered(1))


def _layer_spec(shape, l):
    nd = len(shape)
    return pl.BlockSpec((None,) + tuple(shape), lambda *_: (l,) + (0,) * nd,
                        pipeline_mode=pl.Buffered(1))


def _whole_out_spec(shape):
    nd = len(shape)
    return pl.BlockSpec(shape, lambda *_: (0,) * nd)


def _weight_spec(w, shape, l):
    return _const_spec(shape) if w.shape == tuple(shape) else _layer_spec(shape, l)


def _convert_specs(jobs, n_steps, step_of):
    in_specs, out_specs, out_shapes = [], [], []
    for w, l in jobs:
        _, r, c = w.shape
        rows = r // n_steps
        assert rows * n_steps == r and rows % 16 == 0, (w.shape, n_steps)
        in_specs.append(pl.BlockSpec((None, rows, c), lambda *a, l=l: (l, step_of(*a), 0)))
        out_specs.append(pl.BlockSpec((rows, c), lambda *a: (step_of(*a), 0)))
        out_shapes.append(jax.ShapeDtypeStruct((r, c), BF16))
    return in_specs, out_specs, out_shapes


def _convert_slabs(src_refs, dst_refs):
    for src, dst in zip(src_refs, dst_refs):
        dst[...] = src[...].astype(BF16)


def _mod_kernel(cp_ref, cs_ref, w_ref, b_ref, *rest, n_cvt):
    cvt_in, (op_ref, os_ref), cvt_out = rest[:n_cvt], rest[n_cvt:n_cvt + 2], rest[n_cvt + 2:]
    _convert_slabs(cvt_in, cvt_out)
    w = w_ref[...].astype(BF16)
    b = b_ref[pl.ds(pl.program_id(0), 1), :]
    for c_ref, o_ref in ((cp_ref, op_ref), (cs_ref, os_ref)):
        c = c_ref[...]
        s = (c * jax.nn.sigmoid(c)).astype(BF16)
        o_ref[...] = jnp.dot(s, w, preferred_element_type=F32) + b


def _mod_call(c_p, c_s, w_ada, b_ada, cvt_jobs):
    tn = 1536
    n = 6 * D_MODEL
    nj = n // tn
    mp, ms = c_p.shape[0], c_s.shape[0]
    cvt_in, cvt_out, cvt_shapes = _convert_specs(cvt_jobs, DEPTH * nj, lambda l, j: l * nj + j)
    return pl.pallas_call(
        functools.partial(_mod_kernel, n_cvt=len(cvt_jobs)),
        grid=(DEPTH, nj),
        in_specs=[
            _const_spec((mp, D_MODEL)),
            _const_spec((ms, D_MODEL)),
            pl.BlockSpec((None, D_MODEL, tn), lambda l, j: (l, 0, j)),
            pl.BlockSpec((DEPTH, tn), lambda l, j: (0, j)),
        ] + cvt_in,
        out_specs=[
            pl.BlockSpec((None, mp, tn), lambda l, j: (l, 0, j)),
            pl.BlockSpec((None, ms, tn), lambda l, j: (l, 0, j)),
        ] + cvt_out,
        out_shape=[jax.ShapeDtypeStruct((DEPTH, mp, n), F32),
                   jax.ShapeDtypeStruct((DEPTH, ms, n), F32)] + cvt_shapes,
        compiler_params=pltpu.CompilerParams(
            dimension_semantics=("arbitrary", "arbitrary"),
            vmem_limit_bytes=VMEM_LIMIT),
        name="adaln_mod",
    )(c_p, c_s, w_ada, b_ada, *[w for w, _ in cvt_jobs])


def _poolw_kernel(wp_ref, ps_ref, wa_ref, o_ref):
    scale = ps_ref[pl.ds(pl.program_id(0), 1), :]
    for g in range(len(POOL_WINDOWS)):
        rows = slice(g * POOL_GROUP, (g + 1) * POOL_GROUP)
        o_ref[rows, :] = jnp.dot(wp_ref[g] * scale[:, rows], wa_ref[rows, :],
                                 preferred_element_type=F32,
                                 precision=lax.Precision.HIGHEST).astype(BF16)


def _poolw_call(w_pool, pool_scale, w_a):
    return pl.pallas_call(
        _poolw_kernel,
        grid=(DEPTH,),
        in_specs=[
            pl.BlockSpec((None,) + w_pool.shape[1:], lambda l: (l, 0, 0, 0)),
            pl.BlockSpec((DEPTH, POOL_WIDTH), lambda l: (0, 0)),
            pl.BlockSpec((None, POOL_WIDTH, D_MODEL), lambda l: (l, 0, 0)),
        ],
        out_specs=pl.BlockSpec((None, POOL_WIDTH, D_MODEL), lambda l: (l, 0, 0)),
        out_shape=jax.ShapeDtypeStruct((DEPTH, POOL_WIDTH, D_MODEL), BF16),
        compiler_params=pltpu.CompilerParams(
            dimension_semantics=("arbitrary",),
            vmem_limit_bytes=VMEM_LIMIT),
        name="pool_weights",
    )(w_pool, pool_scale, w_a)


def _prompt_bias():
    i = np.arange(ATTN_BLOCK)[:, None]
    j = np.arange(2 * ATTN_BLOCK)[None, :]
    dist = i + ATTN_BLOCK - j
    valid = (dist >= 0) & (dist < WINDOW)
    valid_first = valid & (j >= ATTN_BLOCK)
    sl = _alibi_slopes()[:, None, None]
    b = np.where(valid[None], -sl * dist[None], NEG_INF)
    b0 = np.where(valid_first[None], -sl * dist[None], NEG_INF)
    return np.stack([b, b0]).astype(np.float32)


def _pmix_kernel(sinks_ref, x_ref, mod_ref, g1_ref, win_ref, wpa_ref,
                 wb_ref, wout_ref, bias_ref, *rest, layer, n_cvt):
    cvt_in, rest = rest[:n_cvt], rest[n_cvt:]
    (xo_ref, nk_ref, nv_ref, npool_ref), rest = rest[:4], rest[4:]
    cvt_out, (ubuf, ka, kb, va, vb, obuf) = rest[:n_cvt], rest[n_cvt:]
    _convert_slabs(cvt_in, cvt_out)
    tm, sub = MIX_ROWS, SUB_ROWS
    bi = pl.program_id(0)
    t = pl.program_id(1)
    nt = pl.num_programs(1)

    @pl.when(t == 0)
    def _init():
        ubuf[0:HIST, :] = jnp.zeros((HIST, POOL_WIDTH), F32)
        zero = jnp.zeros((N_KV_HEADS, ATTN_BLOCK, LANES), BF16)
        ka[:, 0:ATTN_BLOCK, :] = zero
        kb[:, 0:ATTN_BLOCK, :] = zero
        va[:, 0:ATTN_BLOCK, :] = zero
        vb[:, 0:ATTN_BLOCK, :] = zero

    mod = mod_ref[pl.ds(bi, 1), :]
    sh1 = mod[:, 0:D_MODEL]
    sc1 = mod[:, D_MODEL:2 * D_MODEL]
    gt1 = mod[:, 2 * D_MODEL:3 * D_MODEL]
    lane = lax.broadcasted_iota(jnp.int32, (sub, LANES), 1)
    low = lane < HEAD_DIM
    low_q = lax.broadcasted_iota(jnp.int32, (ATTN_BLOCK, LANES), 1) < HEAD_DIM
    nt_dims = (((1,), (1,)), ((), ()))
    st = [dict(ro=i * sub) for i in range(tm // sub)]

    def norm(c):
        c["x"] = x_ref[c["ro"]:c["ro"] + sub, :]
        c["h"] = _rms_mod(c["x"], g1_ref[layer:layer + 1, :], sc1, sh1).astype(BF16)

    def proj(c):
        h = c["h"]
        c["u"] = jnp.dot(h, win_ref[:, OFF_U:OFF_Q], preferred_element_type=F32)
        q = jnp.dot(h, win_ref[:, OFF_Q:OFF_K], preferred_element_type=F32)
        c["qb"] = (q * (HEAD_DIM ** -0.5)).astype(BF16)
        kv = jnp.dot(h, win_ref[:, OFF_K:OFF_GA], preferred_element_type=F32)
        c["k"] = kv[:, 0:KV_WIDTH]
        c["v"] = kv[:, KV_WIDTH:2 * KV_WIDTH]

    def pool_sums(c):
        ro, u = c["ro"], c["u"]
        ubuf[HIST + ro:HIST + ro + sub, :] = u
        pos = t * tm + ro + lax.broadcasted_iota(jnp.int32, (sub, 1), 0)
        cur = ubuf[ro:ro + HIST + sub, :]
        d, w = [], 1
        for g, wg in enumerate(POOL_WINDOWS):
            while w < wg:
                cur = cur + pltpu.roll(cur, w, axis=0)
                w *= 2
            ug = u[:, g * POOL_GROUP:(g + 1) * POOL_GROUP]
            cnt = jnp.minimum(wg, pos + 1).astype(F32)
            d.append((cur[HIST:, 0:POOL_GROUP] / cnt - ug).astype(BF16))
            if g + 1 < len(POOL_WINDOWS):
                cur = cur[:, POOL_GROUP:]
        c["d"] = d

    def kv_store(c):
        r = ATTN_BLOCK + c["ro"]
        zero = jnp.zeros((sub, LANES), BF16)
        for x, xa, xb in ((c["k"], ka, kb), (c["v"], va, vb)):
            x16 = x.astype(BF16)
            xr16 = pltpu.roll(x, HEAD_DIM, axis=1).astype(BF16)
            xa[0, r:r + sub, :] = jnp.where(low, x16, zero)
            xb[0, r:r + sub, :] = jnp.where(low, zero, xr16)
            xa[1, r:r + sub, :] = jnp.where(low, xr16, zero)
            xb[1, r:r + sub, :] = jnp.where(low, zero, x16)

    def pool_proj(c):
        c["br_a"] = jnp.dot(jnp.concatenate(c["d"], axis=1), wpa_ref[...],
                            preferred_element_type=F32)

    def units(c):
        j0 = c["ro"] // ATTN_BLOCK
        return [(j, hk, pr) for j in range(j0, j0 + sub // ATTN_BLOCK)
                for hk in range(N_KV_HEADS) for pr in range(GQA_GROUP // 2)]

    def scores(c):
        sc = {}
        for (j, hk, pr) in units(c):
            r0 = j * ATTN_BLOCK
            c0 = (hk * (GQA_GROUP // 2) + pr) * LANES
            q2 = c["qb"][r0 - c["ro"]:r0 - c["ro"] + ATTN_BLOCK, c0:c0 + LANES]
            ks = (ka[hk, r0:r0 + 2 * ATTN_BLOCK, :], kb[hk, r0:r0 + 2 * ATTN_BLOCK, :])
            for e in range(2):
                sc[(j, hk, pr, e)] = lax.dot_general(q2, ks[e], nt_dims,
                                                     preferred_element_type=F32)
        c["s"] = sc

    def gate_a(c):
        c["ga"] = jnp.dot(c["h"], win_ref[:, OFF_GA:OFF_GB], preferred_element_type=F32)

    def gate_b(c):
        c["gb"] = jnp.dot(c["h"], win_ref[:, OFF_GB:IN_WIDTH], preferred_element_type=F32)

    def softmax(c):
        probs, dens = {}, {}
        for (j, hk, pr) in units(c):
            for e in range(2):
                head = hk * GQA_GROUP + pr * 2 + e
                if j == 0:
                    bias = jnp.where(t == 0, bias_ref[1, head], bias_ref[0, head])
                else:
                    bias = bias_ref[0, head]
                s = c["s"][(j, hk, pr, e)] + bias
                sink = sinks_ref[layer, head]
                m = jnp.maximum(jnp.max(s, axis=-1, keepdims=True), sink)
                p = jnp.exp(s - m)
                dens[(j, hk, pr, e)] = jnp.sum(p, axis=-1, keepdims=True) + jnp.exp(sink - m)
                probs[(j, hk, pr, e)] = p.astype(BF16)
        c["p"], c["den"] = probs, dens

    def values(c):
        for (j, hk, pr) in units(c):
            r0 = j * ATTN_BLOCK
            c0 = (hk * (GQA_GROUP // 2) + pr) * LANES
            vs = (va[hk, r0:r0 + 2 * ATTN_BLOCK, :], vb[hk, r0:r0 + 2 * ATTN_BLOCK, :])
            o2 = (jnp.dot(c["p"][(j, hk, pr, 0)], vs[0], preferred_element_type=F32)
                  + jnp.dot(c["p"][(j, hk, pr, 1)], vs[1], preferred_element_type=F32))
            den = jnp.where(low_q, c["den"][(j, hk, pr, 0)], c["den"][(j, hk, pr, 1)])
            obuf[r0:r0 + ATTN_BLOCK, c0:c0 + LANES] = o2 / den

    def tail(c):
        ro = c["ro"]
        br_b = _bdot(obuf[ro:ro + sub, :], wb_ref[...])
        merged = jax.nn.sigmoid(c["ga"]) * c["br_a"] + jax.nn.sigmoid(c["gb"]) * br_b
        xo_ref[ro:ro + sub, :] = c["x"] + gt1 * _bdot(merged, wout_ref[...])

    stages = [
        (norm,),
        (proj, gate_a),
        (pool_sums, kv_store),
        (pool_proj, scores, gate_b),
        (softmax,),
        (values,),
        (tail,),
    ]
    order = sorted((2 * s + 3 * i, i, s) for i in range(len(st)) for s in range(len(stages)))
    for _, i, s in order:
        for fn in stages[s]:
            fn(st[i])

    @pl.when(t == nt - 1)
    def _state():
        nk_ref[...] = st[-1]["k"][sub - WINDOW:, :].T
        nv_ref[...] = st[-1]["v"][sub - WINDOW:, :].T
        npool_ref[...] = ubuf[HIST + tm - POOL_PAD:HIST + tm, :]

    ubuf[0:HIST, :] = ubuf[tm:tm + HIST, :]
    for buf in (ka, kb, va, vb):
        buf[:, 0:ATTN_BLOCK, :] = buf[:, tm:tm + ATTN_BLOCK, :]


def _pmix_call(x, mod_p, g1, win, wpa, wb, wout, sinks, bias, l, cvt_jobs):
    b, t, _ = x.shape
    tm = MIX_ROWS
    nt = t // tm
    cvt_in, cvt_out, cvt_shapes = _convert_specs(cvt_jobs, b * nt, lambda i, j, s: i * nt + j)
    grid_spec = pltpu.PrefetchScalarGridSpec(
        num_scalar_prefetch=1,
        grid=(b, nt),
        in_specs=[
            pl.BlockSpec((None, tm, D_MODEL), lambda i, j, s: (i, j, 0)),
            pl.BlockSpec((None, b, MOD_SLAB), lambda i, j, s: (l, 0, 0),
                         pipeline_mode=pl.Buffered(1)),
            _const_spec((DEPTH, D_MODEL)),
            _weight_spec(win, (D_MODEL, IN_WIDTH), l),
            _weight_spec(wpa, (POOL_WIDTH, D_MODEL), l),
            _weight_spec(wb, (ATTN_WIDTH, D_MODEL), l),
            _weight_spec(wout, (D_MODEL, D_MODEL), l),
            _const_spec((2, N_HEADS, ATTN_BLOCK, 2 * ATTN_BLOCK)),
        ] + cvt_in,
        out_specs=[
            pl.BlockSpec((None, tm, D_MODEL), lambda i, j, s: (i, j, 0)),
            pl.BlockSpec((None, KV_WIDTH, WINDOW), lambda i, j, s: (i, 0, 0)),
            pl.BlockSpec((None, KV_WIDTH, WINDOW), lambda i, j, s: (i, 0, 0)),
            pl.BlockSpec((None, POOL_PAD, POOL_WIDTH), lambda i, j, s: (i, 0, 0)),
        ] + cvt_out,
        scratch_shapes=[
            pltpu.VMEM((HIST + tm, POOL_WIDTH), F32),
            pltpu.VMEM((N_KV_HEADS, ATTN_BLOCK + tm, LANES), BF16),
            pltpu.VMEM((N_KV_HEADS, ATTN_BLOCK + tm, LANES), BF16),
            pltpu.VMEM((N_KV_HEADS, ATTN_BLOCK + tm, LANES), BF16),
            pltpu.VMEM((N_KV_HEADS, ATTN_BLOCK + tm, LANES), BF16),
            pltpu.VMEM((tm, ATTN_WIDTH), F32),
        ],
    )
    return pl.pallas_call(
        functools.partial(_pmix_kernel, layer=l, n_cvt=len(cvt_jobs)),
        grid_spec=grid_spec,
        out_shape=[
            jax.ShapeDtypeStruct((b, t, D_MODEL), F32),
            jax.ShapeDtypeStruct((b, KV_WIDTH, WINDOW), F32),
            jax.ShapeDtypeStruct((b, KV_WIDTH, WINDOW), F32),
            jax.ShapeDtypeStruct((b, POOL_PAD, POOL_WIDTH), F32),
        ] + cvt_shapes,
        compiler_params=pltpu.CompilerParams(
            dimension_semantics=("arbitrary", "arbitrary"),
            vmem_limit_bytes=VMEM_LIMIT),
        name="prompt_mixer",
    )(sinks, x, mod_p, g1, win, wpa, wb, wout, bias, *[w for w, _ in cvt_jobs])


def _ffn_kernel(xp_ref, xs_ref, modp_ref, mods_ref, g2_ref, wup_ref, wdn_ref, fg_ref,
                so_ref, ma_ref, sgb_ref, gt1_ref, wb_ref, wout_ref,
                *rest, layer, final, n_prompt, tiles_per_row, reps, n_cvt):
    cvt_in, (op_ref, os_ref), cvt_out = rest[:n_cvt], rest[n_cvt:n_cvt + 2], rest[n_cvt + 2:]
    _convert_slabs(cvt_in, cvt_out)
    i = pl.program_id(0)

    def ffn(load, n_rows, o_ref, mod):
        sh2, sc2, gt2 = (mod[:, k * D_MODEL:(k + 1) * D_MODEL] for k in range(3))
        subs = range(0, n_rows, FFN_SUB_ROWS)
        rows = lambda a, r: a if a.shape[0] == 1 else a[r:r + FFN_SUB_ROWS]
        norm = lambda r: _rms_mod(load(r), g2_ref[layer:layer + 1, :],
                                  rows(sc2, r), rows(sh2, r)).astype(BF16)
        h2 = norm(0)
        for r in subs:
            ff = jnp.dot(h2, wup_ref[...], preferred_element_type=F32)
            if r + FFN_SUB_ROWS in subs:
                h2 = norm(r + FFN_SUB_ROWS)
            ff = jnp.square(jnp.maximum(ff, 0.0))
            y = load(r) + rows(gt2, r) * _bdot(ff, wdn_ref[...])
            if final:
                ms = jnp.mean(y * y, axis=-1, keepdims=True)
                y = y * lax.rsqrt(ms + RMS_EPS) * fg_ref[...]
            if len(o_ref.shape) == 3:
                o_ref[...] = jnp.swapaxes(y.reshape(reps, y.shape[0] // reps, D_MODEL), 0, 1)
            else:
                o_ref[r:r + FFN_SUB_ROWS, :] = y

    @pl.when(i < n_prompt)
    def _prompt():
        ffn(lambda r: xp_ref[r:r + FFN_SUB_ROWS, :], xp_ref.shape[0], op_ref,
            modp_ref[pl.ds(i // tiles_per_row, 1), :])

    @pl.when(i == n_prompt)
    def _sample():
        x = xs_ref[...]
        if x.ndim == 3:
            x = jnp.swapaxes(x, 0, 1).reshape(ma_ref.shape)
        gt1 = jnp.concatenate([gt1_ref[...]] * reps, axis=0)
        so = jnp.swapaxes(so_ref[...], 0, 1).reshape(ma_ref.shape[0], ATTN_WIDTH)
        br_b = _bdot(so, wb_ref[...])
        merged = ma_ref[...] + sgb_ref[...] * br_b
        x = x + gt1 * _bdot(merged, wout_ref[...])
        ffn(lambda r: x[r:r + FFN_SUB_ROWS, :], x.shape[0], os_ref,
            jnp.concatenate([mods_ref[...]] * reps, axis=0))


def _ffn_call(xp, xs, so, ma, sgb, mod_p, mod_s, g2, wup, wdn, wb, wout, fg, l, final, cvt_jobs):
    b, t, _ = xp.shape
    ms = ma.shape[0]
    nb = mod_s.shape[1]
    tm = FFN_ROWS
    tpr = t // tm
    n_prompt = b * tpr
    last = n_prompt - 1

    def p_idx(i):
        ii = jnp.minimum(i, last)
        return (ii // tpr, ii % tpr, 0)

    assert ms <= FFN_SUB_ROWS
    s_shape = (nb, ms // nb, D_MODEL) if final else (ms, D_MODEL)
    cvt_in, cvt_out, cvt_shapes = _convert_specs(cvt_jobs, n_prompt, lambda i: jnp.minimum(i, last))
    return pl.pallas_call(
        functools.partial(_ffn_kernel, layer=l, final=final, n_prompt=n_prompt, tiles_per_row=tpr,
                          reps=ms // nb, n_cvt=len(cvt_jobs)),
        grid=(n_prompt + 1,),
        in_specs=[
            pl.BlockSpec((None, tm, D_MODEL), p_idx),
            _const_spec(xs.shape),
            pl.BlockSpec((None, b, MOD_SLAB), lambda i: (l, 0, 1), pipeline_mode=pl.Buffered(1)),
            pl.BlockSpec((None, nb, MOD_SLAB), lambda i: (l, 0, 1), pipeline_mode=pl.Buffered(1)),
            _const_spec((DEPTH, D_MODEL)),
            _weight_spec(wup, (D_MODEL, D_FF), l),
            _weight_spec(wdn, (D_FF, D_MODEL), l),
            _const_spec((1, D_MODEL)),
            _const_spec(so.shape), _const_spec(ma.shape), _const_spec(sgb.shape),
            pl.BlockSpec((None, nb, D_MODEL), lambda i: (l, 0, 2), pipeline_mode=pl.Buffered(1)),
            _weight_spec(wb, (ATTN_WIDTH, D_MODEL), l),
            _weight_spec(wout, (D_MODEL, D_MODEL), l),
        ] + cvt_in,
        out_specs=[
            pl.BlockSpec((None, tm, D_MODEL), p_idx),
            _whole_out_spec(s_shape),
        ] + cvt_out,
        out_shape=[jax.ShapeDtypeStruct((b, t, D_MODEL), F32),
                   jax.ShapeDtypeStruct(s_shape, F32)] + cvt_shapes,
        compiler_params=pltpu.CompilerParams(
            dimension_semantics=("arbitrary",),
            vmem_limit_bytes=VMEM_LIMIT),
        name="ffn_final" if final else "ffn",
    )(xp, xs, mod_p, mod_s, g2, wup, wdn, fg, so, ma, sgb, mod_s, wb, wout,
      *[w for w, _ in cvt_jobs])


def _sproj_kernel(x_ref, mod_ref, g1_ref, win_ref, wpa_ref, pre_ref, *rest, layer, nb, tn):
    if layer:
        npool_ref_prev, rest = rest[0], rest[1:]
    q_ref, k_ref, v_ref, ma_ref, sgb_ref, npool_ref = rest
    if layer:
        npool_ref[0:layer] = npool_ref_prev[...]
    x = x_ref[...]
    if x.ndim == 3:
        x = jnp.swapaxes(x, 0, 1).reshape(tn * nb, D_MODEL)
    mod = jnp.concatenate([mod_ref[:, 0:2 * D_MODEL]] * tn, axis=0)
    sh1 = mod[:, 0:D_MODEL]
    sc1 = mod[:, D_MODEL:2 * D_MODEL]
    h = _rms_mod(x, g1_ref[layer:layer + 1, :], sc1, sh1).astype(BF16)

    u = jnp.dot(h, win_ref[:, OFF_U:OFF_Q], preferred_element_type=F32)
    def up(r):
        if r < POOL_PAD:
            return pre_ref[r]
        r -= POOL_PAD
        return u[r * nb:(r + 1) * nb, :]

    d_rows = []
    for tt in range(tn):
        parts = []
        for g, w in enumerate(POOL_WINDOWS):
            lo = g * POOL_GROUP
            acc = up(POOL_PAD + tt)[:, lo:lo + POOL_GROUP]
            for s in range(1, w):
                acc = acc + up(POOL_PAD + tt - s)[:, lo:lo + POOL_GROUP]
            cnt = float(min(w, PAST_LEN + tt + 1))
            parts.append(acc / cnt - up(POOL_PAD + tt)[:, lo:lo + POOL_GROUP])
        d_rows.append(parts)
    d = jnp.concatenate([jnp.concatenate(parts, axis=1) for parts in d_rows], axis=0)
    br_a = _bdot(d, wpa_ref[...])
    for r in range(POOL_PAD):
        npool_ref[layer, r] = up(r + tn)

    q = jnp.dot(h, win_ref[:, OFF_Q:OFF_K], preferred_element_type=F32)
    q_ref[...] = jnp.swapaxes((q * (HEAD_DIM ** -0.5)).reshape(tn, nb, ATTN_WIDTH), 0, 1)
    kv = jnp.dot(h, win_ref[:, OFF_K:OFF_GA], preferred_element_type=F32)
    kv = jnp.swapaxes(kv.reshape(tn, nb, 2 * KV_WIDTH), 0, 1).reshape(nb * tn, 2 * KV_WIDTH)
    k_ref[...] = kv[:, 0:KV_WIDTH]
    v_ref[...] = kv[:, KV_WIDTH:2 * KV_WIDTH]
    ga = jnp.dot(h, win_ref[:, OFF_GA:OFF_GB], preferred_element_type=F32)
    ma_ref[...] = jax.nn.sigmoid(ga) * br_a
    gb = jnp.dot(h, win_ref[:, OFF_GB:IN_WIDTH], preferred_element_type=F32)
    sgb_ref[...] = jax.nn.sigmoid(gb)


def _sproj_call(x, mod_s, g1, win, wpa, prefix_t, prev_pool, l, nb, tn):
    m = tn * nb
    extra = [] if prev_pool is None else [prev_pool]
    shapes = [
        jax.ShapeDtypeStruct((nb, tn, ATTN_WIDTH), F32),
        jax.ShapeDtypeStruct((m, KV_WIDTH), F32),
        jax.ShapeDtypeStruct((m, KV_WIDTH), F32),
        jax.ShapeDtypeStruct((m, D_MODEL), F32),
        jax.ShapeDtypeStruct((m, D_MODEL), F32),
        jax.ShapeDtypeStruct((l + 1, POOL_PAD, nb, POOL_WIDTH), F32),
    ]
    return pl.pallas_call(
        functools.partial(_sproj_kernel, layer=l, nb=nb, tn=tn),
        grid=(1,),
        in_specs=[
            _const_spec(x.shape),
            pl.BlockSpec((None, nb, MOD_SLAB), lambda i: (l, 0, 0), pipeline_mode=pl.Buffered(1)),
            _const_spec((DEPTH, D_MODEL)),
            _weight_spec(win, (D_MODEL, IN_WIDTH), l),
            _weight_spec(wpa, (POOL_WIDTH, D_MODEL), l),
            _layer_spec((POOL_PAD, nb, POOL_WIDTH), l),
        ] + [_const_spec(a.shape) for a in extra],
        out_specs=[_whole_out_spec(s.shape) for s in shapes],
        out_shape=shapes,
        compiler_params=pltpu.CompilerParams(
            dimension_semantics=("arbitrary",),
            vmem_limit_bytes=VMEM_LIMIT),
        name="sample_proj",
    )(x, mod_s, g1, win, wpa, prefix_t, *extra)


def _sample_bias(tn):
    t = np.arange(tn)[:, None]
    lane = np.arange(2 * WINDOW)[None, :]
    cached = lane < WINDOW
    new_t = lane - (2 * WINDOW - tn)
    dist = np.where(cached, t + WINDOW - lane, t - new_t)
    valid = np.where(cached, (dist >= 0) & (dist < WINDOW), (new_t >= 0) & (dist >= 0))
    sl = _alibi_slopes()[:, None, None]
    return np.where(valid[None], -sl * dist[None], NEG_INF).astype(np.float32)


def _sattn_kernel(sinks_ref, q_ref, kc_ref, vc_ref, kn_ref, vn_ref, bias_ref, *rest,
                   layer, nb, tn, bb):
    if layer:
        pk_ref, pv_ref = rest[:2]
        rest = rest[2:]
    o_ref, nk_ref, nv_ref, wk, wv = rest
    i = pl.program_id(0)
    rows = GQA_GROUP * tn
    keep = WINDOW - tn

    @pl.when(i == 0)
    def _new_rows():
        for c0 in range(0, nb * tn, LANES):
            wk[:, c0:c0 + LANES] = kn_ref[c0:c0 + LANES, :].T
            wv[:, c0:c0 + LANES] = vn_ref[c0:c0 + LANES, :].T

    if layer:
        nk_ref[0:layer] = pk_ref[...]
        nv_ref[0:layer] = pv_ref[...]

    lane0 = i * (bb * tn)
    tile0 = pl.multiple_of((lane0 // LANES) * LANES, LANES)
    off0 = lane0 % LANES
    wkt = wk[:, pl.ds(tile0, LANES)]
    wvt = wv[:, pl.ds(tile0, LANES)]
    lane = lax.broadcasted_iota(jnp.int32, (HEAD_DIM, WINDOW), 1)
    tail = lane >= keep
    grow = lax.broadcasted_iota(jnp.int32, (rows, 1), 0) // tn
    units = [(bl, hk) for bl in range(bb) for hk in range(N_KV_HEADS)]
    biases, sinks = [], []
    for hk in range(N_KV_HEADS):
        biases.append(jnp.concatenate(
            [bias_ref[hk * GQA_GROUP + g] for g in range(GQA_GROUP)], axis=0))
        sink = jnp.zeros((rows, 1), F32)
        for g in range(GQA_GROUP):
            sink = jnp.where(grow == g, sinks_ref[layer, hk * GQA_GROUP + g], sink)
        sinks.append(sink)

    scores, values = {}, {}
    for bl in range(bb):
        shift = (keep - off0 - bl * tn) % LANES
        nkb = pltpu.roll(wkt, shift, axis=1)
        nvb = pltpu.roll(wvt, shift, axis=1)
        for hk in range(N_KV_HEADS):
            kt = kc_ref[bl, hk]
            vt = vc_ref[bl, hk]
            nkt = jnp.where(tail, nkb[hk * HEAD_DIM:(hk + 1) * HEAD_DIM, :], 0.0)
            nvt = jnp.where(tail, nvb[hk * HEAD_DIM:(hk + 1) * HEAD_DIM, :], 0.0)
            nk_ref[layer, bl, hk] = jnp.where(tail, nkt, pltpu.roll(kt, keep, axis=1))
            nv_ref[layer, bl, hk] = jnp.where(tail, nvt, pltpu.roll(vt, keep, axis=1))
            keys = jnp.concatenate([kt, nkt], axis=1).astype(BF16)
            values[(bl, hk)] = jnp.concatenate([vt, nvt], axis=1).astype(BF16)
            qb = q_ref[bl]
            qu = jnp.concatenate(
                [qb[:, (hk * GQA_GROUP + g) * HEAD_DIM:(hk * GQA_GROUP + g + 1) * HEAD_DIM]
                 for g in range(GQA_GROUP)], axis=0).astype(BF16)
            scores[(bl, hk)] = jnp.dot(qu, keys, preferred_element_type=F32)
    probs, dens = {}, {}
    for (bl, hk) in units:
        s = scores[(bl, hk)] + biases[hk]
        m = jnp.maximum(jnp.max(s, axis=-1, keepdims=True), sinks[hk])
        p = jnp.exp(s - m)
        dens[(bl, hk)] = jnp.sum(p, axis=-1, keepdims=True) + jnp.exp(sinks[hk] - m)
        probs[(bl, hk)] = p.astype(BF16)
    for (bl, hk) in units:
        o = lax.dot_general(probs[(bl, hk)], values[(bl, hk)], (((1,), (1,)), ((), ())),
                            preferred_element_type=F32)
        o = o / dens[(bl, hk)]
        gw = GQA_GROUP * HEAD_DIM
        o_ref[bl, :, hk * gw:(hk + 1) * gw] = jnp.concatenate(
            [o[g * tn:(g + 1) * tn, :] for g in range(GQA_GROUP)], axis=1)


def _sattn_call(q4, kn_bt, vn_bt, ck, cv, sinks, rolled, l, nb, tn):
    bb = SATTN_BATCH
    unit = (bb, N_KV_HEADS, HEAD_DIM, WINDOW)
    cache_shape = jax.ShapeDtypeStruct((l + 1, nb) + unit[1:], F32)
    in_specs = [
        pl.BlockSpec((bb, tn, ATTN_WIDTH), lambda i, s: (i, 0, 0)),
        pl.BlockSpec((None,) + unit, lambda i, s: (l, i, 0, 0, 0)),
        pl.BlockSpec((None,) + unit, lambda i, s: (l, i, 0, 0, 0)),
        _const_spec((nb * tn, KV_WIDTH)),
        _const_spec((nb * tn, KV_WIDTH)),
        _const_spec((N_HEADS, tn, 2 * WINDOW)),
    ]
    args = [sinks, q4, ck, cv, kn_bt, vn_bt, jnp.asarray(_sample_bias(tn))]
    if l:
        in_specs += [pl.BlockSpec((l,) + unit, lambda i, s: (0, i, 0, 0, 0))] * 2
        args += list(rolled)
    grid_spec = pltpu.PrefetchScalarGridSpec(
        num_scalar_prefetch=1,
        grid=(nb // bb,),
        in_specs=in_specs,
        out_specs=[
            pl.BlockSpec((bb, tn, ATTN_WIDTH), lambda i, s: (i, 0, 0)),
            pl.BlockSpec((l + 1,) + unit, lambda i, s: (0, i, 0, 0, 0)),
            pl.BlockSpec((l + 1,) + unit, lambda i, s: (0, i, 0, 0, 0)),
        ],
        scratch_shapes=[
            pltpu.VMEM((KV_WIDTH, nb * tn), F32),
            pltpu.VMEM((KV_WIDTH, nb * tn), F32),
        ],
    )
    return pl.pallas_call(
        functools.partial(_sattn_kernel, layer=l, nb=nb, tn=tn, bb=bb),
        grid_spec=grid_spec,
        out_shape=[jax.ShapeDtypeStruct((nb, tn, ATTN_WIDTH), F32), cache_shape, cache_shape],
        compiler_params=pltpu.CompilerParams(
            dimension_semantics=("arbitrary",),
            vmem_limit_bytes=VMEM_LIMIT),
        name="sample_attn",
    )(*args)


def kernel(x_prompt, x_sample, cache_k, cache_v, state_pool, c_prompt, c_sample,
           w_ada, b_ada, norm1_g, w_in, w_pool, pool_scale, attn_sinks, w_a, w_b,
           w_out, norm2_g, w_up, w_down, final_g):
    bp, tp, _ = x_prompt.shape
    nb, tn, _ = x_sample.shape
    assert tp % MIX_ROWS == 0 and tp % FFN_ROWS == 0 and tn <= 8 and nb == LANES
    assert bp % 8 == 0

    mod_p, mod_s, *mix_w = _mod_call(c_prompt, c_sample, w_ada, b_ada,
                                     [(w, 0) for w in (w_in, w_b, w_out)])
    wpa = _poolw_call(w_pool, pool_scale, w_a)
    bias = jnp.asarray(_prompt_bias())
    fg = final_g.reshape(1, D_MODEL)
    g1, g2 = norm1_g, norm2_g

    xs = x_sample
    prefix_t = state_pool.transpose(0, 2, 1, 3)
    ck = cache_k.transpose(0, 1, 3, 4, 2)
    cv = cache_v.transpose(0, 1, 3, 4, 2)

    xp = x_prompt
    kp, vp, pp, npool_s, rolled = [], [], [], None, None
    for l in range(DEPTH):
        last = l == DEPTH - 1
        win, wb, wout = mix_w
        xp, nk, nv, npool, wup, wdn = _pmix_call(xp, mod_p, g1, win, wpa, wb, wout,
                                                 attn_sinks, bias, l, [(w_up, l), (w_down, l)])
        kp.append(nk); vp.append(nv); pp.append(npool)

        q, kn, vn, ma, sgb, npool_s = _sproj_call(xs, mod_s, g1, win, wpa, prefix_t, npool_s,
                                                  l, nb, tn)
        o, *rolled = _sattn_call(q, kn, vn, ck, cv, attn_sinks, rolled, l, nb, tn)

        nxt = [] if last else [(w, l + 1) for w in (w_in, w_b, w_out)]
        xp, xs, *mix_w = _ffn_call(xp, xs, o, ma, sgb, mod_p, mod_s, g2, wup, wdn, wb, wout, fg,
                                   l, last, nxt)

    kv_shape_p = (DEPTH, bp, N_KV_HEADS, HEAD_DIM, WINDOW)
    return (xp,
            xs,
            jnp.stack(kp).reshape(kv_shape_p).transpose(0, 1, 4, 2, 3),
            jnp.stack(vp).reshape(kv_shape_p).transpose(0, 1, 4, 2, 3),
            jnp.stack(pp),
            rolled[0].transpose(0, 1, 4, 2, 3),
            rolled[1].transpose(0, 1, 4, 2, 3),
            npool_s.transpose(0, 2, 1, 3))
```

```python
import functools

import numpy as np
import jax
import jax.numpy as jnp
from jax import lax
from jax.experimental import pallas as pl
from jax.experimental.pallas import tpu as pltpu

D_MODEL = 1024
DEPTH = 2
PAST_LEN = 16384
POOL_WIDTH = D_MODEL // 2
POOL_WINDOWS = (2, 4, 8, 16)
POOL_GROUP = POOL_WIDTH // len(POOL_WINDOWS)
POOL_PAD = max(POOL_WINDOWS) - 1
N_HEADS = 8
N_KV_HEADS = 2
HEAD_DIM = 64
GQA_GROUP = N_HEADS // N_KV_HEADS
ATTN_WIDTH = N_HEADS * HEAD_DIM
KV_WIDTH = N_KV_HEADS * HEAD_DIM
WINDOW = 128
ATTN_BLOCK = 128
D_FF = 4 * D_MODEL
RMS_EPS = 1e-6
NEG_INF = -1e30

OFF_U = 0
OFF_Q = OFF_U + POOL_WIDTH
OFF_K = OFF_Q + ATTN_WIDTH
OFF_V = OFF_K + KV_WIDTH
OFF_GA = OFF_V + KV_WIDTH
OFF_GB = OFF_GA + D_MODEL
IN_WIDTH = OFF_GB + D_MODEL
MOD_SLAB = 3 * D_MODEL

LANES = 128
HIST = 16
assert all(w & (w - 1) == 0 for w in POOL_WINDOWS) and list(POOL_WINDOWS) == sorted(POOL_WINDOWS)
assert HIST >= POOL_PAD
VMEM_LIMIT = 56 * 1024 * 1024
SUB_ROWS = 512
MIX_ROWS = 2 * SUB_ROWS
FFN_SUB_ROWS = 512
FFN_ROWS = 2 * FFN_SUB_ROWS
SATTN_BATCH = 32

F32 = jnp.float32
BF16 = jnp.bfloat16


def _bdot(a, b):
    return jnp.dot(a.astype(BF16), b.astype(BF16), preferred_element_type=F32)


def _rms_mod(x, g, sc, sh):
    ms = jnp.mean(x * x, axis=-1, keepdims=True)
    return (x * lax.rsqrt(ms + RMS_EPS) * g) * (1.0 + sc) + sh


def _alibi_slopes():
    return 2.0 ** (-8.0 * (np.arange(N_HEADS) + 1) / N_HEADS)


def _const_spec(shape):
    nd = len(shape)
    return pl.BlockSpec(shape, lambda *_: (0,) * nd, pipeline_mode=pl.Buffered(1))


def _layer_spec(shape, l):
    nd = len(shape)
    return pl.BlockSpec((None,) + tuple(shape), lambda *_: (l,) + (0,) * nd,
                        pipeline_mode=pl.Buffered(1))


def _whole_out_spec(shape):
    nd = len(shape)
    return pl.BlockSpec(shape, lambda *_: (0,) * nd)


def _weight_spec(w, shape, l):
    return _const_spec(shape) if w.shape == tuple(shape) else _layer_spec(shape, l)


def _convert_specs(jobs, n_steps, step_of):
    in_specs, out_specs, out_shapes = [], [], []
    for w, l in jobs:
        _, r, c = w.shape
        rows = r // n_steps
        assert rows * n_steps == r and rows % 16 == 0, (w.shape, n_steps)
        in_specs.append(pl.BlockSpec((None, rows, c), lambda *a, l=l: (l, step_of(*a), 0)))
        out_specs.append(pl.BlockSpec((rows, c), lambda *a: (step_of(*a), 0)))
        out_shapes.append(jax.ShapeDtypeStruct((r, c), BF16))
    return in_specs, out_specs, out_shapes


def _convert_slabs(src_refs, dst_refs):
    for src, dst in zip(src_refs, dst_refs):
        dst[...] = src[...].astype(BF16)


def _mod_kernel(cp_ref, cs_ref, w_ref, b_ref, *rest, n_cvt):
    cvt_in, (op_ref, os_ref), cvt_out = rest[:n_cvt], rest[n_cvt:n_cvt + 2], rest[n_cvt + 2:]
    _convert_slabs(cvt_in, cvt_out)
    w = w_ref[...].astype(BF16)
    b = b_ref[pl.ds(pl.program_id(0), 1), :]
    for c_ref, o_ref in ((cp_ref, op_ref), (cs_ref, os_ref)):
        c = c_ref[...]
        s = (c * jax.nn.sigmoid(c)).astype(BF16)
        o_ref[...] = jnp.dot(s, w, preferred_element_type=F32) + b


def _mod_call(c_p, c_s, w_ada, b_ada, cvt_jobs):
    tn = 1536
    n = 6 * D_MODEL
    nj = n // tn
    mp, ms = c_p.shape[0], c_s.shape[0]
    cvt_in, cvt_out, cvt_shapes = _convert_specs(cvt_jobs, DEPTH * nj, lambda l, j: l * nj + j)
    return pl.pallas_call(
        functools.partial(_mod_kernel, n_cvt=len(cvt_jobs)),
        grid=(DEPTH, nj),
        in_specs=[
            _const_spec((mp, D_MODEL)),
            _const_spec((ms, D_MODEL)),
            pl.BlockSpec((None, D_MODEL, tn), lambda l, j: (l, 0, j)),
            pl.BlockSpec((DEPTH, tn), lambda l, j: (0, j)),
        ] + cvt_in,
        out_specs=[
            pl.BlockSpec((None, mp, tn), lambda l, j: (l, 0, j)),
            pl.BlockSpec((None, ms, tn), lambda l, j: (l, 0, j)),
        ] + cvt_out,
        out_shape=[jax.ShapeDtypeStruct((DEPTH, mp, n), F32),
                   jax.ShapeDtypeStruct((DEPTH, ms, n), F32)] + cvt_shapes,
        compiler_params=pltpu.CompilerParams(
            dimension_semantics=("arbitrary", "arbitrary"),
            vmem_limit_bytes=VMEM_LIMIT),
        name="adaln_mod",
    )(c_p, c_s, w_ada, b_ada, *[w for w, _ in cvt_jobs])


def _poolw_kernel(wp_ref, ps_ref, wa_ref, o_ref):
    scale = ps_ref[pl.ds(pl.program_id(0), 1), :]
    for g in range(len(POOL_WINDOWS)):
        rows = slice(g * POOL_GROUP, (g + 1) * POOL_GROUP)
        o_ref[rows, :] = jnp.dot(wp_ref[g] * scale[:, rows], wa_ref[rows, :],
                                 preferred_element_type=F32,
                                 precision=lax.Precision.HIGHEST).astype(BF16)


def _poolw_call(w_pool, pool_scale, w_a):
    return pl.pallas_call(
        _poolw_kernel,
        grid=(DEPTH,),
        in_specs=[
            pl.BlockSpec((None,) + w_pool.shape[1:], lambda l: (l, 0, 0, 0)),
            pl.BlockSpec((DEPTH, POOL_WIDTH), lambda l: (0, 0)),
            pl.BlockSpec((None, POOL_WIDTH, D_MODEL), lambda l: (l, 0, 0)),
        ],
        out_specs=pl.BlockSpec((None, POOL_WIDTH, D_MODEL), lambda l: (l, 0, 0)),
        out_shape=jax.ShapeDtypeStruct((DEPTH, POOL_WIDTH, D_MODEL), BF16),
        compiler_params=pltpu.CompilerParams(
            dimension_semantics=("arbitrary",),
            vmem_limit_bytes=VMEM_LIMIT),
        name="pool_weights",
    )(w_pool, pool_scale, w_a)


def _prompt_bias():
    i = np.arange(ATTN_BLOCK)[:, None]
    j = np.arange(2 * ATTN_BLOCK)[None, :]
    dist = i + ATTN_BLOCK - j
    valid = (dist >= 0) & (dist < WINDOW)
    valid_first = valid & (j >= ATTN_BLOCK)
    sl = _alibi_slopes()[:, None, None]
    b = np.where(valid[None], -sl * dist[None], NEG_INF)
    b0 = np.where(valid_first[None], -sl * dist[None], NEG_INF)
    return np.stack([b, b0]).astype(np.float32)


def _pmix_kernel(sinks_ref, x_ref, mod_ref, g1_ref, win_ref, wpa_ref,
                 wb_ref, wout_ref, bias_ref, *rest, layer, n_cvt):
    cvt_in, rest = rest[:n_cvt], rest[n_cvt:]
    if layer:
        prev_state, rest = rest[:3], rest[3:]
    (xo_ref, nk_ref, nv_ref, npool_ref), rest = rest[:4], rest[4:]
    cvt_out, (ubuf, ka, kb, va, vb, obuf) = rest[:n_cvt], rest[n_cvt:]
    _convert_slabs(cvt_in, cvt_out)
    tm, sub = MIX_ROWS, SUB_ROWS
    bi = pl.program_id(0)
    t = pl.program_id(1)
    nt = pl.num_programs(1)

    @pl.when(t == 0)
    def _init():
        ubuf[0:HIST, :] = jnp.zeros((HIST, POOL_WIDTH), F32)
        zero = jnp.zeros((N_KV_HEADS, ATTN_BLOCK, LANES), BF16)
        ka[:, 0:ATTN_BLOCK, :] = zero
        kb[:, 0:ATTN_BLOCK, :] = zero
        va[:, 0:ATTN_BLOCK, :] = zero
        vb[:, 0:ATTN_BLOCK, :] = zero

    mod = mod_ref[pl.ds(bi, 1), :]
    sh1 = mod[:, 0:D_MODEL]
    sc1 = mod[:, D_MODEL:2 * D_MODEL]
    gt1 = mod[:, 2 * D_MODEL:3 * D_MODEL]
    lane = lax.broadcasted_iota(jnp.int32, (sub, LANES), 1)
    low = lane < HEAD_DIM
    low_q = lax.broadcasted_iota(jnp.int32, (ATTN_BLOCK, LANES), 1) < HEAD_DIM
    nt_dims = (((1,), (1,)), ((), ()))
    st = [dict(ro=i * sub) for i in range(tm // sub)]

    def norm(c):
        c["x"] = x_ref[c["ro"]:c["ro"] + sub, :]
        c["h"] = _rms_mod(c["x"], g1_ref[layer:layer + 1, :], sc1, sh1).astype(BF16)

    def proj(c):
        h = c["h"]
        c["u"] = jnp.dot(h, win_ref[:, OFF_U:OFF_Q], preferred_element_type=F32)
        q = jnp.dot(h, win_ref[:, OFF_Q:OFF_K], preferred_element_type=F32)
        c["qb"] = (q * (HEAD_DIM ** -0.5)).astype(BF16)
        kv = jnp.dot(h, win_ref[:, OFF_K:OFF_GA], preferred_element_type=F32)
        c["k"] = kv[:, 0:KV_WIDTH]
        c["v"] = kv[:, KV_WIDTH:2 * KV_WIDTH]

    def pool_sums(c):
        ro, u = c["ro"], c["u"]
        ubuf[HIST + ro:HIST + ro + sub, :] = u
        pos = t * tm + ro + lax.broadcasted_iota(jnp.int32, (sub, 1), 0)
        cur = ubuf[ro:ro + HIST + sub, :]
        d, w = [], 1
        for g, wg in enumerate(POOL_WINDOWS):
            while w < wg:
                cur = cur + pltpu.roll(cur, w, axis=0)
                w *= 2
            ug = u[:, g * POOL_GROUP:(g + 1) * POOL_GROUP]
            cnt = jnp.minimum(wg, pos + 1).astype(F32)
            d.append((cur[HIST:, 0:POOL_GROUP] / cnt - ug).astype(BF16))
            if g + 1 < len(POOL_WINDOWS):
                cur = cur[:, POOL_GROUP:]
        c["d"] = d

    def kv_store(c):
        r = ATTN_BLOCK + c["ro"]
        zero = jnp.zeros((sub, LANES), BF16)
        for x, xa, xb in ((c["k"], ka, kb), (c["v"], va, vb)):
            x16 = x.astype(BF16)
            xr16 = pltpu.roll(x, HEAD_DIM, axis=1).astype(BF16)
            xa[0, r:r + sub, :] = jnp.where(low, x16, zero)
            xb[0, r:r + sub, :] = jnp.where(low, zero, xr16)
            xa[1, r:r + sub, :] = jnp.where(low, xr16, zero)
            xb[1, r:r + sub, :] = jnp.where(low, zero, x16)

    def pool_proj(c):
        c["br_a"] = jnp.dot(jnp.concatenate(c["d"], axis=1), wpa_ref[...],
                            preferred_element_type=F32)

    def units(c):
        j0 = c["ro"] // ATTN_BLOCK
        return [(j, hk, pr) for j in range(j0, j0 + sub // ATTN_BLOCK)
                for hk in range(N_KV_HEADS) for pr in range(GQA_GROUP // 2)]

    def scores(c):
        sc = {}
        for (j, hk, pr) in units(c):
            r0 = j * ATTN_BLOCK
            c0 = (hk * (GQA_GROUP // 2) + pr) * LANES
            q2 = c["qb"][r0 - c["ro"]:r0 - c["ro"] + ATTN_BLOCK, c0:c0 + LANES]
            ks = (ka[hk, r0:r0 + 2 * ATTN_BLOCK, :], kb[hk, r0:r0 + 2 * ATTN_BLOCK, :])
            for e in range(2):
                sc[(j, hk, pr, e)] = lax.dot_general(q2, ks[e], nt_dims,
                                                     preferred_element_type=F32)
        c["s"] = sc

    def gate_a(c):
        c["ga"] = jnp.dot(c["h"], win_ref[:, OFF_GA:OFF_GB], preferred_element_type=F32)

    def gate_b(c):
        c["gb"] = jnp.dot(c["h"], win_ref[:, OFF_GB:IN_WIDTH], preferred_element_type=F32)

    def softmax(c):
        probs, dens = {}, {}
        for (j, hk, pr) in units(c):
            for e in range(2):
                head = hk * GQA_GROUP + pr * 2 + e
                if j == 0:
                    bias = jnp.where(t == 0, bias_ref[1, head], bias_ref[0, head])
                else:
                    bias = bias_ref[0, head]
                s = c["s"][(j, hk, pr, e)] + bias
                sink = sinks_ref[layer, head]
                m = jnp.maximum(jnp.max(s, axis=-1, keepdims=True), sink)
                p = jnp.exp(s - m)
                dens[(j, hk, pr, e)] = jnp.sum(p, axis=-1, keepdims=True) + jnp.exp(sink - m)
                probs[(j, hk, pr, e)] = p.astype(BF16)
        c["p"], c["den"] = probs, dens

    def values(c):
        for (j, hk, pr) in units(c):
            r0 = j * ATTN_BLOCK
            c0 = (hk * (GQA_GROUP // 2) + pr) * LANES
            vs = (va[hk, r0:r0 + 2 * ATTN_BLOCK, :], vb[hk, r0:r0 + 2 * ATTN_BLOCK, :])
            o2 = (jnp.dot(c["p"][(j, hk, pr, 0)], vs[0], preferred_element_type=F32)
                  + jnp.dot(c["p"][(j, hk, pr, 1)], vs[1], preferred_element_type=F32))
            den = jnp.where(low_q, c["den"][(j, hk, pr, 0)], c["den"][(j, hk, pr, 1)])
            obuf[r0:r0 + ATTN_BLOCK, c0:c0 + LANES] = o2 / den

    def tail(c):
        ro = c["ro"]
        br_b = _bdot(obuf[ro:ro + sub, :], wb_ref[...])
        merged = jax.nn.sigmoid(c["ga"]) * c["br_a"] + jax.nn.sigmoid(c["gb"]) * br_b
        xo_ref[ro:ro + sub, :] = c["x"] + gt1 * _bdot(merged, wout_ref[...])

    stages = [
        (norm,),
        (proj, gate_a),
        (pool_sums, kv_store),
        (pool_proj, scores, gate_b),
        (softmax,),
        (values,),
        (tail,),
    ]
    order = sorted((2 * s + 3 * i, i, s) for i in range(len(st)) for s in range(len(stages)))
    for _, i, s in order:
        for fn in stages[s]:
            fn(st[i])

    @pl.when(t == nt - 1)
    def _state():
        nk_ref[layer] = st[-1]["k"][sub - WINDOW:, :].T
        nv_ref[layer] = st[-1]["v"][sub - WINDOW:, :].T
        npool_ref[layer] = ubuf[HIST + tm - POOL_PAD:HIST + tm, :]
        if layer:
            for dst, src in zip((nk_ref, nv_ref, npool_ref), prev_state):
                dst[0:layer] = src[...]

    ubuf[0:HIST, :] = ubuf[tm:tm + HIST, :]
    for buf in (ka, kb, va, vb):
        buf[:, 0:ATTN_BLOCK, :] = buf[:, tm:tm + ATTN_BLOCK, :]


def _pmix_call(x, mod_p, g1, win, wpa, wb, wout, sinks, bias, prev_state, l, cvt_jobs):
    b, t, _ = x.shape
    tm = MIX_ROWS
    nt = t // tm
    cvt_in, cvt_out, cvt_shapes = _convert_specs(cvt_jobs, b * nt, lambda i, j, s: i * nt + j)
    state_tiles = [(KV_WIDTH, WINDOW), (KV_WIDTH, WINDOW), (POOL_PAD, POOL_WIDTH)]
    by_row = lambda i, j, s: (0, i, 0, 0)
    prev_specs = [pl.BlockSpec((l, None) + tile, by_row) for tile in state_tiles] if l else []
    grid_spec = pltpu.PrefetchScalarGridSpec(
        num_scalar_prefetch=1,
        grid=(b, nt),
        in_specs=[
            pl.BlockSpec((None, tm, D_MODEL), lambda i, j, s: (i, j, 0)),
            pl.BlockSpec((None, b, MOD_SLAB), lambda i, j, s: (l, 0, 0),
                         pipeline_mode=pl.Buffered(1)),
            _const_spec((DEPTH, D_MODEL)),
            _weight_spec(win, (D_MODEL, IN_WIDTH), l),
            _weight_spec(wpa, (POOL_WIDTH, D_MODEL), l),
            _weight_spec(wb, (ATTN_WIDTH, D_MODEL), l),
            _weight_spec(wout, (D_MODEL, D_MODEL), l),
            _const_spec((2, N_HEADS, ATTN_BLOCK, 2 * ATTN_BLOCK)),
        ] + cvt_in + prev_specs,
        out_specs=[
            pl.BlockSpec((None, tm, D_MODEL), lambda i, j, s: (i, j, 0)),
        ] + [pl.BlockSpec((l + 1, None) + tile, by_row) for tile in state_tiles] + cvt_out,
        scratch_shapes=[
            pltpu.VMEM((HIST + tm, POOL_WIDTH), F32),
            pltpu.VMEM((N_KV_HEADS, ATTN_BLOCK + tm, LANES), BF16),
            pltpu.VMEM((N_KV_HEADS, ATTN_BLOCK + tm, LANES), BF16),
            pltpu.VMEM((N_KV_HEADS, ATTN_BLOCK + tm, LANES), BF16),
            pltpu.VMEM((N_KV_HEADS, ATTN_BLOCK + tm, LANES), BF16),
            pltpu.VMEM((tm, ATTN_WIDTH), F32),
        ],
    )
    return pl.pallas_call(
        functools.partial(_pmix_kernel, layer=l, n_cvt=len(cvt_jobs)),
        grid_spec=grid_spec,
        out_shape=[
            jax.ShapeDtypeStruct((b, t, D_MODEL), F32),
        ] + [jax.ShapeDtypeStruct((l + 1, b) + tile, F32) for tile in state_tiles] + cvt_shapes,
        compiler_params=pltpu.CompilerParams(
            dimension_semantics=("arbitrary", "arbitrary"),
            vmem_limit_bytes=VMEM_LIMIT),
        name="prompt_mixer",
    )(sinks, x, mod_p, g1, win, wpa, wb, wout, bias, *[w for w, _ in cvt_jobs], *prev_state)


def _ffn_kernel(xp_ref, xs_ref, modp_ref, mods_ref, g2_ref, wup_ref, wdn_ref, fg_ref,
                so_ref, ma_ref, sgb_ref, gt1_ref, wb_ref, wout_ref,
                *rest, layer, final, n_prompt, tiles_per_row, reps, n_cvt):
    cvt_in, (op_ref, os_ref), cvt_out = rest[:n_cvt], rest[n_cvt:n_cvt + 2], rest[n_cvt + 2:]
    _convert_slabs(cvt_in, cvt_out)
    i = pl.program_id(0)

    def ffn(load, n_rows, o_ref, mod):
        sh2, sc2, gt2 = (mod[:, k * D_MODEL:(k + 1) * D_MODEL] for k in range(3))
        subs = range(0, n_rows, FFN_SUB_ROWS)
        rows = lambda a, r: a if a.shape[0] == 1 else a[r:r + FFN_SUB_ROWS]
        norm = lambda r: _rms_mod(load(r), g2_ref[layer:layer + 1, :],
                                  rows(sc2, r), rows(sh2, r)).astype(BF16)
        h2 = norm(0)
        for r in subs:
            ff = jnp.dot(h2, wup_ref[...], preferred_element_type=F32)
            if r + FFN_SUB_ROWS in subs:
                h2 = norm(r + FFN_SUB_ROWS)
            ff = jnp.square(jnp.maximum(ff, 0.0))
            y = load(r) + rows(gt2, r) * _bdot(ff, wdn_ref[...])
            if final:
                ms = jnp.mean(y * y, axis=-1, keepdims=True)
                y = y * lax.rsqrt(ms + RMS_EPS) * fg_ref[...]
            if len(o_ref.shape) == 3:
                o_ref[...] = jnp.swapaxes(y.reshape(reps, y.shape[0] // reps, D_MODEL), 0, 1)
            else:
                o_ref[r:r + FFN_SUB_ROWS, :] = y

    @pl.when(i < n_prompt)
    def _prompt():
        ffn(lambda r: xp_ref[r:r + FFN_SUB_ROWS, :], xp_ref.shape[0], op_ref,
            modp_ref[pl.ds(i // tiles_per_row, 1), :])

    @pl.when(i == n_prompt)
    def _sample():
        x = xs_ref[...]
        if x.ndim == 3:
            x = jnp.swapaxes(x, 0, 1).reshape(ma_ref.shape)
        gt1 = jnp.concatenate([gt1_ref[...]] * reps, axis=0)
        so = jnp.swapaxes(so_ref[...], 0, 1).reshape(ma_ref.shape[0], ATTN_WIDTH)
        br_b = _bdot(so, wb_ref[...])
        merged = ma_ref[...] + sgb_ref[...] * br_b
        x = x + gt1 * _bdot(merged, wout_ref[...])
        ffn(lambda r: x[r:r + FFN_SUB_ROWS, :], x.shape[0], os_ref,
            jnp.concatenate([mods_ref[...]] * reps, axis=0))


def _ffn_call(xp, xs, so, ma, sgb, mod_p, mod_s, g2, wup, wdn, wb, wout, fg, l, final, cvt_jobs):
    b, t, _ = xp.shape
    ms = ma.shape[0]
    nb = mod_s.shape[1]
    tm = FFN_ROWS
    tpr = t // tm
    n_prompt = b * tpr
    last = n_prompt - 1

    def p_idx(i):
        ii = jnp.minimum(i, last)
        return (ii // tpr, ii % tpr, 0)

    assert ms <= FFN_SUB_ROWS
    s_shape = (nb, ms // nb, D_MODEL) if final else (ms, D_MODEL)
    cvt_in, cvt_out, cvt_shapes = _convert_specs(cvt_jobs, n_prompt, lambda i: jnp.minimum(i, last))
    return pl.pallas_call(
        functools.partial(_ffn_kernel, layer=l, final=final, n_prompt=n_prompt, tiles_per_row=tpr,
                          reps=ms // nb, n_cvt=len(cvt_jobs)),
        grid=(n_prompt + 1,),
        in_specs=[
            pl.BlockSpec((None, tm, D_MODEL), p_idx),
            _const_spec(xs.shape),
            pl.BlockSpec((None, b, MOD_SLAB), lambda i: (l, 0, 1), pipeline_mode=pl.Buffered(1)),
            pl.BlockSpec((None, nb, MOD_SLAB), lambda i: (l, 0, 1), pipeline_mode=pl.Buffered(1)),
            _const_spec((DEPTH, D_MODEL)),
            _weight_spec(wup, (D_MODEL, D_FF), l),
            _weight_spec(wdn, (D_FF, D_MODEL), l),
            _const_spec((1, D_MODEL)),
            _const_spec(so.shape), _const_spec(ma.shape), _const_spec(sgb.shape),
            pl.BlockSpec((None, nb, D_MODEL), lambda i: (l, 0, 2), pipeline_mode=pl.Buffered(1)),
            _weight_spec(wb, (ATTN_WIDTH, D_MODEL), l),
            _weight_spec(wout, (D_MODEL, D_MODEL), l),
        ] + cvt_in,
        out_specs=[
            pl.BlockSpec((None, tm, D_MODEL), p_idx),
            _whole_out_spec(s_shape),
        ] + cvt_out,
        out_shape=[jax.ShapeDtypeStruct((b, t, D_MODEL), F32),
                   jax.ShapeDtypeStruct(s_shape, F32)] + cvt_shapes,
        compiler_params=pltpu.CompilerParams(
            dimension_semantics=("arbitrary",),
            vmem_limit_bytes=VMEM_LIMIT),
        name="ffn_final" if final else "ffn",
    )(xp, xs, mod_p, mod_s, g2, wup, wdn, fg, so, ma, sgb, mod_s, wb, wout,
      *[w for w, _ in cvt_jobs])


def _sproj_kernel(x_ref, mod_ref, g1_ref, win_ref, wpa_ref, pre_ref, *rest, layer, nb, tn):
    if layer:
        npool_ref_prev, rest = rest[0], rest[1:]
    q_ref, k_ref, v_ref, ma_ref, sgb_ref, npool_ref = rest
    if layer:
        npool_ref[0:layer] = npool_ref_prev[...]
    x = x_ref[...]
    if x.ndim == 3:
        x = jnp.swapaxes(x, 0, 1).reshape(tn * nb, D_MODEL)
    mod = jnp.concatenate([mod_ref[:, 0:2 * D_MODEL]] * tn, axis=0)
    sh1 = mod[:, 0:D_MODEL]
    sc1 = mod[:, D_MODEL:2 * D_MODEL]
    h = _rms_mod(x, g1_ref[layer:layer + 1, :], sc1, sh1).astype(BF16)

    u = jnp.dot(h, win_ref[:, OFF_U:OFF_Q], preferred_element_type=F32)
    def up(r):
        if r < POOL_PAD:
            return pre_ref[r]
        r -= POOL_PAD
        return u[r * nb:(r + 1) * nb, :]

    d_rows = []
    for tt in range(tn):
        parts = []
        for g, w in enumerate(POOL_WINDOWS):
            lo = g * POOL_GROUP
            acc = up(POOL_PAD + tt)[:, lo:lo + POOL_GROUP]
            for s in range(1, w):
                acc = acc + up(POOL_PAD + tt - s)[:, lo:lo + POOL_GROUP]
            cnt = float(min(w, PAST_LEN + tt + 1))
            parts.append(acc / cnt - up(POOL_PAD + tt)[:, lo:lo + POOL_GROUP])
        d_rows.append(parts)
    d = jnp.concatenate([jnp.concatenate(parts, axis=1) for parts in d_rows], axis=0)
    br_a = _bdot(d, wpa_ref[...])
    for r in range(POOL_PAD):
        npool_ref[layer, r] = up(r + tn)

    q = jnp.dot(h, win_ref[:, OFF_Q:OFF_K], preferred_element_type=F32)
    q_ref[...] = jnp.swapaxes((q * (HEAD_DIM ** -0.5)).reshape(tn, nb, ATTN_WIDTH), 0, 1)
    kv = jnp.dot(h, win_ref[:, OFF_K:OFF_GA], preferred_element_type=F32)
    kv = jnp.swapaxes(kv.reshape(tn, nb, 2 * KV_WIDTH), 0, 1).reshape(nb * tn, 2 * KV_WIDTH)
    k_ref[...] = kv[:, 0:KV_WIDTH]
    v_ref[...] = kv[:, KV_WIDTH:2 * KV_WIDTH]
    ga = jnp.dot(h, win_ref[:, OFF_GA:OFF_GB], preferred_element_type=F32)
    ma_ref[...] = jax.nn.sigmoid(ga) * br_a
    gb = jnp.dot(h, win_ref[:, OFF_GB:IN_WIDTH], preferred_element_type=F32)
    sgb_ref[...] = jax.nn.sigmoid(gb)


def _sproj_call(x, mod_s, g1, win, wpa, prefix_t, prev_pool, l, nb, tn):
    m = tn * nb
    extra = [] if prev_pool is None else [prev_pool]
    shapes = [
        jax.ShapeDtypeStruct((nb, tn, ATTN_WIDTH), F32),
        jax.ShapeDtypeStruct((m, KV_WIDTH), F32),
        jax.ShapeDtypeStruct((m, KV_WIDTH), F32),
        jax.ShapeDtypeStruct((m, D_MODEL), F32),
        jax.ShapeDtypeStruct((m, D_MODEL), F32),
        jax.ShapeDtypeStruct((l + 1, POOL_PAD, nb, POOL_WIDTH), F32),
    ]
    return pl.pallas_call(
        functools.partial(_sproj_kernel, layer=l, nb=nb, tn=tn),
        grid=(1,),
        in_specs=[
            _const_spec(x.shape),
            pl.BlockSpec((None, nb, MOD_SLAB), lambda i: (l, 0, 0), pipeline_mode=pl.Buffered(1)),
            _const_spec((DEPTH, D_MODEL)),
            _weight_spec(win, (D_MODEL, IN_WIDTH), l),
            _weight_spec(wpa, (POOL_WIDTH, D_MODEL), l),
            _layer_spec((POOL_PAD, nb, POOL_WIDTH), l),
        ] + [_const_spec(a.shape) for a in extra],
        out_specs=[_whole_out_spec(s.shape) for s in shapes],
        out_shape=shapes,
        compiler_params=pltpu.CompilerParams(
            dimension_semantics=("arbitrary",),
            vmem_limit_bytes=VMEM_LIMIT),
        name="sample_proj",
    )(x, mod_s, g1, win, wpa, prefix_t, *extra)


def _sample_bias(tn):
    t = np.arange(tn)[:, None]
    lane = np.arange(2 * WINDOW)[None, :]
    cached = lane < WINDOW
    new_t = lane - (2 * WINDOW - tn)
    dist = np.where(cached, t + WINDOW - lane, t - new_t)
    valid = np.where(cached, (dist >= 0) & (dist < WINDOW), (new_t >= 0) & (dist >= 0))
    sl = _alibi_slopes()[:, None, None]
    return np.where(valid[None], -sl * dist[None], NEG_INF).astype(np.float32)


def _sattn_kernel(sinks_ref, q_ref, kc_ref, vc_ref, kn_ref, vn_ref, bias_ref, *rest,
                   layer, nb, tn, bb):
    if layer:
        pk_ref, pv_ref = rest[:2]
        rest = rest[2:]
    o_ref, nk_ref, nv_ref, wk, wv = rest
    i = pl.program_id(0)
    rows = GQA_GROUP * tn
    keep = WINDOW - tn

    @pl.when(i == 0)
    def _new_rows():
        for c0 in range(0, nb * tn, LANES):
            wk[:, c0:c0 + LANES] = kn_ref[c0:c0 + LANES, :].T
            wv[:, c0:c0 + LANES] = vn_ref[c0:c0 + LANES, :].T

    if layer:
        nk_ref[0:layer] = pk_ref[...]
        nv_ref[0:layer] = pv_ref[...]

    lane0 = i * (bb * tn)
    tile0 = pl.multiple_of((lane0 // LANES) * LANES, LANES)
    off0 = lane0 % LANES
    wkt = wk[:, pl.ds(tile0, LANES)]
    wvt = wv[:, pl.ds(tile0, LANES)]
    lane = lax.broadcasted_iota(jnp.int32, (HEAD_DIM, WINDOW), 1)
    tail = lane >= keep
    grow = lax.broadcasted_iota(jnp.int32, (rows, 1), 0) // tn
    units = [(bl, hk) for bl in range(bb) for hk in range(N_KV_HEADS)]
    biases, sinks = [], []
    for hk in range(N_KV_HEADS):
        biases.append(jnp.concatenate(
            [bias_ref[hk * GQA_GROUP + g] for g in range(GQA_GROUP)], axis=0))
        sink = jnp.zeros((rows, 1), F32)
        for g in range(GQA_GROUP):
            sink = jnp.where(grow == g, sinks_ref[layer, hk * GQA_GROUP + g], sink)
        sinks.append(sink)

    scores, values = {}, {}
    for bl in range(bb):
        shift = (keep - off0 - bl * tn) % LANES
        nkb = pltpu.roll(wkt, shift, axis=1)
        nvb = pltpu.roll(wvt, shift, axis=1)
        for hk in range(N_KV_HEADS):
            kt = kc_ref[bl, hk]
            vt = vc_ref[bl, hk]
            nkt = jnp.where(tail, nkb[hk * HEAD_DIM:(hk + 1) * HEAD_DIM, :], 0.0)
            nvt = jnp.where(tail, nvb[hk * HEAD_DIM:(hk + 1) * HEAD_DIM, :], 0.0)
            nk_ref[layer, bl, hk] = jnp.where(tail, nkt, pltpu.roll(kt, keep, axis=1))
            nv_ref[layer, bl, hk] = jnp.where(tail, nvt, pltpu.roll(vt, keep, axis=1))
            keys = jnp.concatenate([kt, nkt], axis=1).astype(BF16)
            values[(bl, hk)] = jnp.concatenate([vt, nvt], axis=1).astype(BF16)
            qb = q_ref[bl]
            qu = jnp.concatenate(
                [qb[:, (hk * GQA_GROUP + g) * HEAD_DIM:(hk * GQA_GROUP + g + 1) * HEAD_DIM]
                 for g in range(GQA_GROUP)], axis=0).astype(BF16)
            scores[(bl, hk)] = jnp.dot(qu, keys, preferred_element_type=F32)
    probs, dens = {}, {}
    for (bl, hk) in units:
        s = scores[(bl, hk)] + biases[hk]
        m = jnp.maximum(jnp.max(s, axis=-1, keepdims=True), sinks[hk])
        p = jnp.exp(s - m)
        dens[(bl, hk)] = jnp.sum(p, axis=-1, keepdims=True) + jnp.exp(sinks[hk] - m)
        probs[(bl, hk)] = p.astype(BF16)
    for (bl, hk) in units:
        o = lax.dot_general(probs[(bl, hk)], values[(bl, hk)], (((1,), (1,)), ((), ())),
                            preferred_element_type=F32)
        o = o / dens[(bl, hk)]
        gw = GQA_GROUP * HEAD_DIM
        o_ref[bl, :, hk * gw:(hk + 1) * gw] = jnp.concatenate(
            [o[g * tn:(g + 1) * tn, :] for g in range(GQA_GROUP)], axis=1)


def _sattn_call(q4, kn_bt, vn_bt, ck, cv, sinks, rolled, l, nb, tn):
    bb = SATTN_BATCH
    unit = (bb, N_KV_HEADS, HEAD_DIM, WINDOW)
    cache_shape = jax.ShapeDtypeStruct((l + 1, nb) + unit[1:], F32)
    in_specs = [
        pl.BlockSpec((bb, tn, ATTN_WIDTH), lambda i, s: (i, 0, 0)),
        pl.BlockSpec((None,) + unit, lambda i, s: (l, i, 0, 0, 0)),
        pl.BlockSpec((None,) + unit, lambda i, s: (l, i, 0, 0, 0)),
        _const_spec((nb * tn, KV_WIDTH)),
        _const_spec((nb * tn, KV_WIDTH)),
        _const_spec((N_HEADS, tn, 2 * WINDOW)),
    ]
    args = [sinks, q4, ck, cv, kn_bt, vn_bt, jnp.asarray(_sample_bias(tn))]
    if l:
        in_specs += [pl.BlockSpec((l,) + unit, lambda i, s: (0, i, 0, 0, 0))] * 2
        args += list(rolled)
    grid_spec = pltpu.PrefetchScalarGridSpec(
        num_scalar_prefetch=1,
        grid=(nb // bb,),
        in_specs=in_specs,
        out_specs=[
            pl.BlockSpec((bb, tn, ATTN_WIDTH), lambda i, s: (i, 0, 0)),
            pl.BlockSpec((l + 1,) + unit, lambda i, s: (0, i, 0, 0, 0)),
            pl.BlockSpec((l + 1,) + unit, lambda i, s: (0, i, 0, 0, 0)),
        ],
        scratch_shapes=[
            pltpu.VMEM((KV_WIDTH, nb * tn), F32),
            pltpu.VMEM((KV_WIDTH, nb * tn), F32),
        ],
    )
    return pl.pallas_call(
        functools.partial(_sattn_kernel, layer=l, nb=nb, tn=tn, bb=bb),
        grid_spec=grid_spec,
        out_shape=[jax.ShapeDtypeStruct((nb, tn, ATTN_WIDTH), F32), cache_shape, cache_shape],
        compiler_params=pltpu.CompilerParams(
            dimension_semantics=("arbitrary",),
            vmem_limit_bytes=VMEM_LIMIT),
        name="sample_attn",
    )(*args)


def kernel(x_prompt, x_sample, cache_k, cache_v, state_pool, c_prompt, c_sample,
           w_ada, b_ada, norm1_g, w_in, w_pool, pool_scale, attn_sinks, w_a, w_b,
           w_out, norm2_g, w_up, w_down, final_g):
    bp, tp, _ = x_prompt.shape
    nb, tn, _ = x_sample.shape
    assert tp % MIX_ROWS == 0 and tp % FFN_ROWS == 0 and tn <= 8 and nb == LANES
    assert bp % 8 == 0

    mod_p, mod_s, *mix_w = _mod_call(c_prompt, c_sample, w_ada, b_ada,
                                     [(w, 0) for w in (w_in, w_b, w_out)])
    wpa = _poolw_call(w_pool, pool_scale, w_a)
    bias = jnp.asarray(_prompt_bias())
    fg = final_g.reshape(1, D_MODEL)
    g1, g2 = norm1_g, norm2_g

    xs = x_sample
    prefix_t = state_pool.transpose(0, 2, 1, 3)
    ck = cache_k.transpose(0, 1, 3, 4, 2)
    cv = cache_v.transpose(0, 1, 3, 4, 2)

    xp = x_prompt
    state_p, npool_s, rolled = (), None, None
    for l in range(DEPTH):
        last = l == DEPTH - 1
        win, wb, wout = mix_w
        xp, *state_p, wup, wdn = _pmix_call(xp, mod_p, g1, win, wpa, wb, wout, attn_sinks, bias,
                                            state_p, l, [(w_up, l), (w_down, l)])

        q, kn, vn, ma, sgb, npool_s = _sproj_call(xs, mod_s, g1, win, wpa, prefix_t, npool_s,
                                                  l, nb, tn)
        o, *rolled = _sattn_call(q, kn, vn, ck, cv, attn_sinks, rolled, l, nb, tn)

        nxt = [] if last else [(w, l + 1) for w in (w_in, w_b, w_out)]
        xp, xs, *mix_w = _ffn_call(xp, xs, o, ma, sgb, mod_p, mod_s, g2, wup, wdn, wb, wout, fg,
                                   l, last, nxt)

    kv_shape_p = (DEPTH, bp, N_KV_HEADS, HEAD_DIM, WINDOW)
    return (xp,
            xs,
            state_p[0].reshape(kv_shape_p).transpose(0, 1, 4, 2, 3),
            state_p[1].reshape(kv_shape_p).transpose(0, 1, 4, 2, 3),
            state_p[2],
            rolled[0].transpose(0, 1, 4, 2, 3),
            rolled[1].transpose(0, 1, 4, 2, 3),
            npool_s.transpose(0, 2, 1, 3))
```

```python
import functools

import numpy as np
import jax
import jax.numpy as jnp
from jax import lax
from jax.experimental import pallas as pl
from jax.experimental.pallas import tpu as pltpu

D_MODEL = 1024
DEPTH = 2
PAST_LEN = 16384
POOL_WIDTH = D_MODEL // 2
POOL_WINDOWS = (2, 4, 8, 16)
POOL_GROUP = POOL_WIDTH // len(POOL_WINDOWS)
POOL_PAD = max(POOL_WINDOWS) - 1
N_HEADS = 8
N_KV_HEADS = 2
HEAD_DIM = 64
GQA_GROUP = N_HEADS // N_KV_HEADS
ATTN_WIDTH = N_HEADS * HEAD_DIM
KV_WIDTH = N_KV_HEADS * HEAD_DIM
WINDOW = 128
ATTN_BLOCK = 128
D_FF = 4 * D_MODEL
RMS_EPS = 1e-6
NEG_INF = -1e30

OFF_U = 0
OFF_Q = OFF_U + POOL_WIDTH
OFF_K = OFF_Q + ATTN_WIDTH
OFF_V = OFF_K + KV_WIDTH
OFF_GA = OFF_V + KV_WIDTH
OFF_GB = OFF_GA + D_MODEL
IN_WIDTH = OFF_GB + D_MODEL
MOD_SLAB = 3 * D_MODEL

LANES = 128
HIST = 16
assert all(w & (w - 1) == 0 for w in POOL_WINDOWS) and list(POOL_WINDOWS) == sorted(POOL_WINDOWS)
assert HIST >= POOL_PAD
VMEM_LIMIT = 56 * 1024 * 1024
SUB_ROWS = 512
MIX_ROWS = 2 * SUB_ROWS
FFN_SUB_ROWS = 512
FFN_ROWS = 2 * FFN_SUB_ROWS
SATTN_BATCH = 32

F32 = jnp.float32
BF16 = jnp.bfloat16


def _bdot(a, b):
    return jnp.dot(a.astype(BF16), b.astype(BF16), preferred_element_type=F32)


def _rms_mod(x, g, sc, sh):
    ms = jnp.mean(x * x, axis=-1, keepdims=True)
    return (x * lax.rsqrt(ms + RMS_EPS) * g) * (1.0 + sc) + sh


def _alibi_slopes():
    return 2.0 ** (-8.0 * (np.arange(N_HEADS) + 1) / N_HEADS)


def _const_spec(shape):
    nd = len(shape)
    return pl.BlockSpec(shape, lambda *_: (0,) * nd, pipeline_mode=pl.Buffered(1))


def _layer_spec(shape, l):
    nd = len(shape)
    return pl.BlockSpec((None,) + tuple(shape), lambda *_: (l,) + (0,) * nd,
                        pipeline_mode=pl.Buffered(1))


def _whole_out_spec(shape):
    nd = len(shape)
    return pl.BlockSpec(shape, lambda *_: (0,) * nd)


def _weight_spec(w, shape, l):
    return _const_spec(shape) if w.shape == tuple(shape) else _layer_spec(shape, l)


def _convert_specs(jobs, n_steps, step_of):
    in_specs, out_specs, out_shapes = [], [], []
    for w, l in jobs:
        _, r, c = w.shape
        rows = r // n_steps
        assert rows * n_steps == r and rows % 16 == 0, (w.shape, n_steps)
        in_specs.append(pl.BlockSpec((None, rows, c), lambda *a, l=l: (l, step_of(*a), 0)))
        out_specs.append(pl.BlockSpec((rows, c), lambda *a: (step_of(*a), 0)))
        out_shapes.append(jax.ShapeDtypeStruct((r, c), BF16))
    return in_specs, out_specs, out_shapes


def _convert_slabs(src_refs, dst_refs):
    for src, dst in zip(src_refs, dst_refs):
        dst[...] = src[...].astype(BF16)


def _mod_kernel(cp_ref, cs_ref, w_ref, b_ref, *rest, n_cvt):
    cvt_in, (op_ref, os_ref), cvt_out = rest[:n_cvt], rest[n_cvt:n_cvt + 2], rest[n_cvt + 2:]
    _convert_slabs(cvt_in, cvt_out)
    w = w_ref[...].astype(BF16)
    b = b_ref[pl.ds(pl.program_id(0), 1), :]
    for c_ref, o_ref in ((cp_ref, op_ref), (cs_ref, os_ref)):
        c = c_ref[...]
        s = (c * jax.nn.sigmoid(c)).astype(BF16)
        o_ref[...] = jnp.dot(s, w, preferred_element_type=F32) + b


def _mod_call(c_p, c_s, w_ada, b_ada, cvt_jobs):
    tn = 1536
    n = 6 * D_MODEL
    nj = n // tn
    mp, ms = c_p.shape[0], c_s.shape[0]
    cvt_in, cvt_out, cvt_shapes = _convert_specs(cvt_jobs, DEPTH * nj, lambda l, j: l * nj + j)
    return pl.pallas_call(
        functools.partial(_mod_kernel, n_cvt=len(cvt_jobs)),
        grid=(DEPTH, nj),
        in_specs=[
            _const_spec((mp, D_MODEL)),
            _const_spec((ms, D_MODEL)),
            pl.BlockSpec((None, D_MODEL, tn), lambda l, j: (l, 0, j)),
            pl.BlockSpec((DEPTH, tn), lambda l, j: (0, j)),
        ] + cvt_in,
        out_specs=[
            pl.BlockSpec((None, mp, tn), lambda l, j: (l, 0, j)),
            pl.BlockSpec((None, ms, tn), lambda l, j: (l, 0, j)),
        ] + cvt_out,
        out_shape=[jax.ShapeDtypeStruct((DEPTH, mp, n), F32),
                   jax.ShapeDtypeStruct((DEPTH, ms, n), F32)] + cvt_shapes,
        compiler_params=pltpu.CompilerParams(
            dimension_semantics=("arbitrary", "arbitrary"),
            vmem_limit_bytes=VMEM_LIMIT),
        name="adaln_mod",
    )(c_p, c_s, w_ada, b_ada, *[w for w, _ in cvt_jobs])


def _poolw_kernel(wp_ref, ps_ref, wa_ref, o_ref):
    scale = ps_ref[pl.ds(pl.program_id(0), 1), :]
    for g in range(len(POOL_WINDOWS)):
        rows = slice(g * POOL_GROUP, (g + 1) * POOL_GROUP)
        o_ref[rows, :] = jnp.dot(wp_ref[g] * scale[:, rows], wa_ref[rows, :],
                                 preferred_element_type=F32,
                                 precision=lax.Precision.HIGHEST).astype(BF16)


def _poolw_call(w_pool, pool_scale, w_a):
    return pl.pallas_call(
        _poolw_kernel,
        grid=(DEPTH,),
        in_specs=[
            pl.BlockSpec((None,) + w_pool.shape[1:], lambda l: (l, 0, 0, 0)),
            pl.BlockSpec((DEPTH, POOL_WIDTH), lambda l: (0, 0)),
            pl.BlockSpec((None, POOL_WIDTH, D_MODEL), lambda l: (l, 0, 0)),
        ],
        out_specs=pl.BlockSpec((None, POOL_WIDTH, D_MODEL), lambda l: (l, 0, 0)),
        out_shape=jax.ShapeDtypeStruct((DEPTH, POOL_WIDTH, D_MODEL), BF16),
        compiler_params=pltpu.CompilerParams(
            dimension_semantics=("arbitrary",),
            vmem_limit_bytes=VMEM_LIMIT),
        name="pool_weights",
    )(w_pool, pool_scale, w_a)


def _prompt_bias():
    i = np.arange(ATTN_BLOCK)[:, None]
    j = np.arange(2 * ATTN_BLOCK)[None, :]
    dist = i + ATTN_BLOCK - j
    valid = (dist >= 0) & (dist < WINDOW)
    valid_first = valid & (j >= ATTN_BLOCK)
    sl = _alibi_slopes()[:, None, None]
    b = np.where(valid[None], -sl * dist[None], NEG_INF)
    b0 = np.where(valid_first[None], -sl * dist[None], NEG_INF)
    return np.stack([b, b0]).astype(np.float32)


def _pmix_kernel(sinks_ref, x_ref, mod_ref, g1_ref, win_ref, wpa_ref,
                 wb_ref, wout_ref, bias_ref, *rest, layer, n_cvt):
    cvt_in, rest = rest[:n_cvt], rest[n_cvt:]
    (xo_ref, nk_ref, nv_ref, npool_ref), rest = rest[:4], rest[4:]
    cvt_out, (ubuf, ka, kb, va, vb, obuf) = rest[:n_cvt], rest[n_cvt:]
    _convert_slabs(cvt_in, cvt_out)
    tm, sub = MIX_ROWS, SUB_ROWS
    bi = pl.program_id(0)
    t = pl.program_id(1)
    nt = pl.num_programs(1)

    @pl.when(t == 0)
    def _init():
        ubuf[0:HIST, :] = jnp.zeros((HIST, POOL_WIDTH), F32)
        zero = jnp.zeros((N_KV_HEADS, ATTN_BLOCK, LANES), BF16)
        ka[:, 0:ATTN_BLOCK, :] = zero
        kb[:, 0:ATTN_BLOCK, :] = zero
        va[:, 0:ATTN_BLOCK, :] = zero
        vb[:, 0:ATTN_BLOCK, :] = zero

    mod = mod_ref[pl.ds(bi, 1), :]
    sh1 = mod[:, 0:D_MODEL]
    sc1 = mod[:, D_MODEL:2 * D_MODEL]
    gt1 = mod[:, 2 * D_MODEL:3 * D_MODEL]
    lane = lax.broadcasted_iota(jnp.int32, (sub, LANES), 1)
    low = lane < HEAD_DIM
    low_q = lax.broadcasted_iota(jnp.int32, (ATTN_BLOCK, LANES), 1) < HEAD_DIM
    nt_dims = (((1,), (1,)), ((), ()))
    st = [dict(ro=i * sub) for i in range(tm // sub)]

    def norm(c):
        c["x"] = x_ref[c["ro"]:c["ro"] + sub, :]
        c["h"] = _rms_mod(c["x"], g1_ref[layer:layer + 1, :], sc1, sh1).astype(BF16)

    def proj(c):
        h = c["h"]
        c["u"] = jnp.dot(h, win_ref[:, OFF_U:OFF_Q], preferred_element_type=F32)
        q = jnp.dot(h, win_ref[:, OFF_Q:OFF_K], preferred_element_type=F32)
        c["qb"] = (q * (HEAD_DIM ** -0.5)).astype(BF16)
        kv = jnp.dot(h, win_ref[:, OFF_K:OFF_GA], preferred_element_type=F32)
        c["k"] = kv[:, 0:KV_WIDTH]
        c["v"] = kv[:, KV_WIDTH:2 * KV_WIDTH]

    def pool_sums(c):
        ro, u = c["ro"], c["u"]
        ubuf[HIST + ro:HIST + ro + sub, :] = u
        pos = t * tm + ro + lax.broadcasted_iota(jnp.int32, (sub, 1), 0)
        cur = ubuf[ro:ro + HIST + sub, :]
        d, w = [], 1
        for g, wg in enumerate(POOL_WINDOWS):
            while w < wg:
                cur = cur + pltpu.roll(cur, w, axis=0)
                w *= 2
            ug = u[:, g * POOL_GROUP:(g + 1) * POOL_GROUP]
            cnt = jnp.minimum(wg, pos + 1).astype(F32)
            d.append((cur[HIST:, 0:POOL_GROUP] / cnt - ug).astype(BF16))
            if g + 1 < len(POOL_WINDOWS):
                cur = cur[:, POOL_GROUP:]
        c["d"] = d

    def kv_store(c):
        r = ATTN_BLOCK + c["ro"]
        zero = jnp.zeros((sub, LANES), BF16)
        for x, xa, xb in ((c["k"], ka, kb), (c["v"], va, vb)):
            x16 = x.astype(BF16)
            xr16 = pltpu.roll(x, HEAD_DIM, axis=1).astype(BF16)
            xa[0, r:r + sub, :] = jnp.where(low, x16, zero)
            xb[0, r:r + sub, :] = jnp.where(low, zero, xr16)
            xa[1, r:r + sub, :] = jnp.where(low, xr16, zero)
            xb[1, r:r + sub, :] = jnp.where(low, zero, x16)

    def pool_proj(c):
        c["br_a"] = jnp.dot(jnp.concatenate(c["d"], axis=1), wpa_ref[...],
                            preferred_element_type=F32)

    def units(c):
        j0 = c["ro"] // ATTN_BLOCK
        return [(j, hk, pr) for j in range(j0, j0 + sub // ATTN_BLOCK)
                for hk in range(N_KV_HEADS) for pr in range(GQA_GROUP // 2)]

    def scores(c):
        sc = {}
        for (j, hk, pr) in units(c):
            r0 = j * ATTN_BLOCK
            c0 = (hk * (GQA_GROUP // 2) + pr) * LANES
            q2 = c["qb"][r0 - c["ro"]:r0 - c["ro"] + ATTN_BLOCK, c0:c0 + LANES]
            ks = (ka[hk, r0:r0 + 2 * ATTN_BLOCK, :], kb[hk, r0:r0 + 2 * ATTN_BLOCK, :])
            for e in range(2):
                sc[(j, hk, pr, e)] = lax.dot_general(q2, ks[e], nt_dims,
                                                     preferred_element_type=F32)
        c["s"] = sc

    def gate_a(c):
        c["ga"] = jnp.dot(c["h"], win_ref[:, OFF_GA:OFF_GB], preferred_element_type=F32)

    def gate_b(c):
        c["gb"] = jnp.dot(c["h"], win_ref[:, OFF_GB:IN_WIDTH], preferred_element_type=F32)

    def softmax(c):
        probs, dens = {}, {}
        for (j, hk, pr) in units(c):
            for e in range(2):
                head = hk * GQA_GROUP + pr * 2 + e
                if j == 0:
                    bias = jnp.where(t == 0, bias_ref[1, head], bias_ref[0, head])
                else:
                    bias = bias_ref[0, head]
                s = c["s"][(j, hk, pr, e)] + bias
                sink = sinks_ref[layer, head]
                m = jnp.maximum(jnp.max(s, axis=-1, keepdims=True), sink)
                p = jnp.exp(s - m)
                dens[(j, hk, pr, e)] = jnp.sum(p, axis=-1, keepdims=True) + jnp.exp(sink - m)
                probs[(j, hk, pr, e)] = p.astype(BF16)
        c["p"], c["den"] = probs, dens

    def values(c):
        for (j, hk, pr) in units(c):
            r0 = j * ATTN_BLOCK
            c0 = (hk * (GQA_GROUP // 2) + pr) * LANES
            vs = (va[hk, r0:r0 + 2 * ATTN_BLOCK, :], vb[hk, r0:r0 + 2 * ATTN_BLOCK, :])
            o2 = (jnp.dot(c["p"][(j, hk, pr, 0)], vs[0], preferred_element_type=F32)
                  + jnp.dot(c["p"][(j, hk, pr, 1)], vs[1], preferred_element_type=F32))
            den = jnp.where(low_q, c["den"][(j, hk, pr, 0)], c["den"][(j, hk, pr, 1)])
            obuf[r0:r0 + ATTN_BLOCK, c0:c0 + LANES] = o2 / den

    def tail(c):
        ro = c["ro"]
        br_b = _bdot(obuf[ro:ro + sub, :], wb_ref[...])
        merged = jax.nn.sigmoid(c["ga"]) * c["br_a"] + jax.nn.sigmoid(c["gb"]) * br_b
        xo_ref[ro:ro + sub, :] = c["x"] + gt1 * _bdot(merged, wout_ref[...])

    stages = [
        (norm,),
        (proj, gate_a),
        (pool_sums, kv_store),
        (pool_proj, scores, gate_b),
        (softmax,),
        (values,),
        (tail,),
    ]
    order = sorted((2 * s + 3 * i, i, s) for i in range(len(st)) for s in range(len(stages)))
    for _, i, s in order:
        for fn in stages[s]:
            fn(st[i])

    @pl.when(t == nt - 1)
    def _state():
        nk_ref[...] = st[-1]["k"][sub - WINDOW:, :].T
        nv_ref[...] = st[-1]["v"][sub - WINDOW:, :].T
        npool_ref[...] = ubuf[HIST + tm - POOL_PAD:HIST + tm, :]

    ubuf[0:HIST, :] = ubuf[tm:tm + HIST, :]
    for buf in (ka, kb, va, vb):
        buf[:, 0:ATTN_BLOCK, :] = buf[:, tm:tm + ATTN_BLOCK, :]


def _pmix_call(x, mod_p, g1, win, wpa, wb, wout, sinks, bias, l, cvt_jobs):
    b, t, _ = x.shape
    tm = MIX_ROWS
    nt = t // tm
    cvt_in, cvt_out, cvt_shapes = _convert_specs(cvt_jobs, b * nt, lambda i, j, s: i * nt + j)
    grid_spec = pltpu.PrefetchScalarGridSpec(
        num_scalar_prefetch=1,
        grid=(b, nt),
        in_specs=[
            pl.BlockSpec((None, tm, D_MODEL), lambda i, j, s: (i, j, 0)),
            pl.BlockSpec((None, b, MOD_SLAB), lambda i, j, s: (l, 0, 0),
                         pipeline_mode=pl.Buffered(1)),
            _const_spec((DEPTH, D_MODEL)),
            _weight_spec(win, (D_MODEL, IN_WIDTH), l),
            _weight_spec(wpa, (POOL_WIDTH, D_MODEL), l),
            _weight_spec(wb, (ATTN_WIDTH, D_MODEL), l),
            _weight_spec(wout, (D_MODEL, D_MODEL), l),
            _const_spec((2, N_HEADS, ATTN_BLOCK, 2 * ATTN_BLOCK)),
        ] + cvt_in,
        out_specs=[
            pl.BlockSpec((None, tm, D_MODEL), lambda i, j, s: (i, j, 0)),
            pl.BlockSpec((None, KV_WIDTH, WINDOW), lambda i, j, s: (i, 0, 0)),
            pl.BlockSpec((None, KV_WIDTH, WINDOW), lambda i, j, s: (i, 0, 0)),
            pl.BlockSpec((None, POOL_PAD, POOL_WIDTH), lambda i, j, s: (i, 0, 0)),
        ] + cvt_out,
        scratch_shapes=[
            pltpu.VMEM((HIST + tm, POOL_WIDTH), F32),
            pltpu.VMEM((N_KV_HEADS, ATTN_BLOCK + tm, LANES), BF16),
            pltpu.VMEM((N_KV_HEADS, ATTN_BLOCK + tm, LANES), BF16),
            pltpu.VMEM((N_KV_HEADS, ATTN_BLOCK + tm, LANES), BF16),
            pltpu.VMEM((N_KV_HEADS, ATTN_BLOCK + tm, LANES), BF16),
            pltpu.VMEM((tm, ATTN_WIDTH), F32),
        ],
    )
    return pl.pallas_call(
        functools.partial(_pmix_kernel, layer=l, n_cvt=len(cvt_jobs)),
        grid_spec=grid_spec,
        out_shape=[
            jax.ShapeDtypeStruct((b, t, D_MODEL), F32),
            jax.ShapeDtypeStruct((b, KV_WIDTH, WINDOW), F32),
            jax.ShapeDtypeStruct((b, KV_WIDTH, WINDOW), F32),
            jax.ShapeDtypeStruct((b, POOL_PAD, POOL_WIDTH), F32),
        ] + cvt_shapes,
        compiler_params=pltpu.CompilerParams(
            dimension_semantics=("arbitrary", "arbitrary"),
            vmem_limit_bytes=VMEM_LIMIT),
        name="prompt_mixer",
    )(sinks, x, mod_p, g1, win, wpa, wb, wout, bias, *[w for w, _ in cvt_jobs])


def _ffn_kernel(xp_ref, xs_ref, modp_ref, mods_ref, g2_ref, wup_ref, wdn_ref, fg_ref,
                so_ref, ma_ref, sgb_ref, gt1_ref, wb_ref, wout_ref,
                *rest, layer, final, n_prompt, tiles_per_row, reps, n_cvt):
    cvt_in, (op_ref, os_ref), cvt_out = rest[:n_cvt], rest[n_cvt:n_cvt + 2], rest[n_cvt + 2:]
    _convert_slabs(cvt_in, cvt_out)
    i = pl.program_id(0)

    def ffn(load, n_rows, o_ref, mod):
        sh2, sc2, gt2 = (mod[:, k * D_MODEL:(k + 1) * D_MODEL] for k in range(3))
        subs = range(0, n_rows, FFN_SUB_ROWS)
        rows = lambda a, r: a if a.shape[0] == 1 else a[r:r + FFN_SUB_ROWS]
        norm = lambda r: _rms_mod(load(r), g2_ref[layer:layer + 1, :],
                                  rows(sc2, r), rows(sh2, r)).astype(BF16)
        h2 = norm(0)
        for r in subs:
            ff = jnp.dot(h2, wup_ref[...], preferred_element_type=F32)
            if r + FFN_SUB_ROWS in subs:
                h2 = norm(r + FFN_SUB_ROWS)
            ff = jnp.square(jnp.maximum(ff, 0.0))
            y = load(r) + rows(gt2, r) * _bdot(ff, wdn_ref[...])
            if final:
                ms = jnp.mean(y * y, axis=-1, keepdims=True)
                y = y * lax.rsqrt(ms + RMS_EPS) * fg_ref[...]
            if len(o_ref.shape) == 3:
                o_ref[...] = jnp.swapaxes(y.reshape(reps, y.shape[0] // reps, D_MODEL), 0, 1)
            else:
                o_ref[r:r + FFN_SUB_ROWS, :] = y

    @pl.when(i < n_prompt)
    def _prompt():
        ffn(lambda r: xp_ref[r:r + FFN_SUB_ROWS, :], xp_ref.shape[0], op_ref,
            modp_ref[pl.ds(i // tiles_per_row, 1), :])

    @pl.when(i == n_prompt)
    def _sample():
        x = xs_ref[...]
        if x.ndim == 3:
            x = jnp.swapaxes(x, 0, 1).reshape(ma_ref.shape)
        gt1 = jnp.concatenate([gt1_ref[...]] * reps, axis=0)
        so = jnp.swapaxes(so_ref[...], 0, 1).reshape(ma_ref.shape[0], ATTN_WIDTH)
        br_b = _bdot(so, wb_ref[...])
        merged = ma_ref[...] + sgb_ref[...] * br_b
        x = x + gt1 * _bdot(merged, wout_ref[...])
        ffn(lambda r: x[r:r + FFN_SUB_ROWS, :], x.shape[0], os_ref,
            jnp.concatenate([mods_ref[...]] * reps, axis=0))


def _ffn_call(xp, xs, so, ma, sgb, mod_p, mod_s, g2, wup, wdn, wb, wout, fg, l, final, cvt_jobs):
    b, t, _ = xp.shape
    ms = ma.shape[0]
    nb = mod_s.shape[1]
    tm = FFN_ROWS
    tpr = t // tm
    n_prompt = b * tpr
    last = n_prompt - 1

    def p_idx(i):
        ii = jnp.minimum(i, last)
        return (ii // tpr, ii % tpr, 0)

    assert ms <= FFN_SUB_ROWS
    s_shape = (nb, ms // nb, D_MODEL) if final else (ms, D_MODEL)
    cvt_in, cvt_out, cvt_shapes = _convert_specs(cvt_jobs, n_prompt, lambda i: jnp.minimum(i, last))
    return pl.pallas_call(
        functools.partial(_ffn_kernel, layer=l, final=final, n_prompt=n_prompt, tiles_per_row=tpr,
                          reps=ms // nb, n_cvt=len(cvt_jobs)),
        grid=(n_prompt + 1,),
        in_specs=[
            pl.BlockSpec((None, tm, D_MODEL), p_idx),
            _const_spec(xs.shape),
            pl.BlockSpec((None, b, MOD_SLAB), lambda i: (l, 0, 1), pipeline_mode=pl.Buffered(1)),
            pl.BlockSpec((None, nb, MOD_SLAB), lambda i: (l, 0, 1), pipeline_mode=pl.Buffered(1)),
            _const_spec((DEPTH, D_MODEL)),
            _weight_spec(wup, (D_MODEL, D_FF), l),
            _weight_spec(wdn, (D_FF, D_MODEL), l),
            _const_spec((1, D_MODEL)),
            _const_spec(so.shape), _const_spec(ma.shape), _const_spec(sgb.shape),
            pl.BlockSpec((None, nb, D_MODEL), lambda i: (l, 0, 2), pipeline_mode=pl.Buffered(1)),
            _weight_spec(wb, (ATTN_WIDTH, D_MODEL), l),
            _weight_spec(wout, (D_MODEL, D_MODEL), l),
        ] + cvt_in,
        out_specs=[
            pl.BlockSpec((None, tm, D_MODEL), p_idx),
            _whole_out_spec(s_shape),
        ] + cvt_out,
        out_shape=[jax.ShapeDtypeStruct((b, t, D_MODEL), F32),
                   jax.ShapeDtypeStruct(s_shape, F32)] + cvt_shapes,
        compiler_params=pltpu.CompilerParams(
            dimension_semantics=("arbitrary",),
            vmem_limit_bytes=VMEM_LIMIT),
        name="ffn_final" if final else "ffn",
    )(xp, xs, mod_p, mod_s, g2, wup, wdn, fg, so, ma, sgb, mod_s, wb, wout,
      *[w for w, _ in cvt_jobs])


def _sproj_kernel(x_ref, mod_ref, g1_ref, win_ref, wpa_ref, pre_ref, *rest, layer, nb, tn):
    if layer:
        npool_ref_prev, rest = rest[0], rest[1:]
    q_ref, k_ref, v_ref, ma_ref, sgb_ref, npool_ref = rest
    if layer:
        npool_ref[0:layer] = npool_ref_prev[...]
    x = x_ref[...]
    if x.ndim == 3:
        x = jnp.swapaxes(x, 0, 1).reshape(tn * nb, D_MODEL)
    mod = jnp.concatenate([mod_ref[:, 0:2 * D_MODEL]] * tn, axis=0)
    sh1 = mod[:, 0:D_MODEL]
    sc1 = mod[:, D_MODEL:2 * D_MODEL]
    h = _rms_mod(x, g1_ref[layer:layer + 1, :], sc1, sh1).astype(BF16)

    u = jnp.dot(h, win_ref[:, OFF_U:OFF_Q], preferred_element_type=F32)
    def up(r):
        if r < POOL_PAD:
            return pre_ref[r]
        r -= POOL_PAD
        return u[r * nb:(r + 1) * nb, :]

    d_rows = []
    for tt in range(tn):
        parts = []
        for g, w in enumerate(POOL_WINDOWS):
            lo = g * POOL_GROUP
            acc = up(POOL_PAD + tt)[:, lo:lo + POOL_GROUP]
            for s in range(1, w):
                acc = acc + up(POOL_PAD + tt - s)[:, lo:lo + POOL_GROUP]
            cnt = float(min(w, PAST_LEN + tt + 1))
            parts.append(acc / cnt - up(POOL_PAD + tt)[:, lo:lo + POOL_GROUP])
        d_rows.append(parts)
    d = jnp.concatenate([jnp.concatenate(parts, axis=1) for parts in d_rows], axis=0)
    br_a = _bdot(d, wpa_ref[...])
    for r in range(POOL_PAD):
        npool_ref[layer, r] = up(r + tn)

    q = jnp.dot(h, win_ref[:, OFF_Q:OFF_K], preferred_element_type=F32)
    q_ref[...] = jnp.swapaxes((q * (HEAD_DIM ** -0.5)).reshape(tn, nb, ATTN_WIDTH), 0, 1)
    kv = jnp.dot(h, win_ref[:, OFF_K:OFF_GA], preferred_element_type=F32)
    kv = jnp.swapaxes(kv.reshape(tn, nb, 2 * KV_WIDTH), 0, 1).reshape(nb * tn, 2 * KV_WIDTH)
    k_ref[...] = kv[:, 0:KV_WIDTH]
    v_ref[...] = kv[:, KV_WIDTH:2 * KV_WIDTH]
    ga = jnp.dot(h, win_ref[:, OFF_GA:OFF_GB], preferred_element_type=F32)
    ma_ref[...] = jax.nn.sigmoid(ga) * br_a
    gb = jnp.dot(h, win_ref[:, OFF_GB:IN_WIDTH], preferred_element_type=F32)
    sgb_ref[...] = jax.nn.sigmoid(gb)


def _sproj_call(x, mod_s, g1, win, wpa, prefix_t, prev_pool, l, nb, tn):
    m = tn * nb
    extra = [] if prev_pool is None else [prev_pool]
    shapes = [
        jax.ShapeDtypeStruct((nb, tn, ATTN_WIDTH), F32),
        jax.ShapeDtypeStruct((m, KV_WIDTH), F32),
        jax.ShapeDtypeStruct((m, KV_WIDTH), F32),
        jax.ShapeDtypeStruct((m, D_MODEL), F32),
        jax.ShapeDtypeStruct((m, D_MODEL), F32),
        jax.ShapeDtypeStruct((l + 1, POOL_PAD, nb, POOL_WIDTH), F32),
    ]
    return pl.pallas_call(
        functools.partial(_sproj_kernel, layer=l, nb=nb, tn=tn),
        grid=(1,),
        in_specs=[
            _const_spec(x.shape),
            pl.BlockSpec((None, nb, MOD_SLAB), lambda i: (l, 0, 0), pipeline_mode=pl.Buffered(1)),
            _const_spec((DEPTH, D_MODEL)),
            _weight_spec(win, (D_MODEL, IN_WIDTH), l),
            _weight_spec(wpa, (POOL_WIDTH, D_MODEL), l),
            _layer_spec((POOL_PAD, nb, POOL_WIDTH), l),
        ] + [_const_spec(a.shape) for a in extra],
        out_specs=[_whole_out_spec(s.shape) for s in shapes],
        out_shape=shapes,
        compiler_params=pltpu.CompilerParams(
            dimension_semantics=("arbitrary",),
            vmem_limit_bytes=VMEM_LIMIT),
        name="sample_proj",
    )(x, mod_s, g1, win, wpa, prefix_t, *extra)


def _sample_bias(tn):
    t = np.arange(tn)[:, None]
    lane = np.arange(2 * WINDOW)[None, :]
    cached = lane < WINDOW
    new_t = lane - (2 * WINDOW - tn)
    dist = np.where(cached, t + WINDOW - lane, t - new_t)
    valid = np.where(cached, (dist >= 0) & (dist < WINDOW), (new_t >= 0) & (dist >= 0))
    sl = _alibi_slopes()[:, None, None]
    return np.where(valid[None], -sl * dist[None], NEG_INF).astype(np.float32)


def _sattn_kernel(sinks_ref, q_ref, kc_ref, vc_ref, bias_ref, *rest,
                   layer, nb, tn, bb, n_c, roll):
    new_refs, rest = rest[:2 * n_c], rest[2 * n_c:]
    if roll:
        o_ref, nk_ref, nv_ref, wk, wv = rest
    else:
        o_ref, wk, wv = rest
    i = pl.program_id(0)
    rows = GQA_GROUP * tn
    keep = WINDOW - tn

    @pl.when(i == 0)
    def _new_rows():
        for r in range(n_c):
            for c0 in range(0, nb * tn, LANES):
                wk[r, :, c0:c0 + LANES] = new_refs[2 * r][c0:c0 + LANES, :].T
                wv[r, :, c0:c0 + LANES] = new_refs[2 * r + 1][c0:c0 + LANES, :].T

    lane0 = i * (bb * tn)
    tile0 = pl.multiple_of((lane0 // LANES) * LANES, LANES)
    off0 = lane0 % LANES
    wkt = [wk[r, :, pl.ds(tile0, LANES)] for r in range(n_c)]
    wvt = [wv[r, :, pl.ds(tile0, LANES)] for r in range(n_c)]
    lane = lax.broadcasted_iota(jnp.int32, (HEAD_DIM, WINDOW), 1)
    tail = lane >= keep
    grow = lax.broadcasted_iota(jnp.int32, (rows, 1), 0) // tn
    units = [(bl, hk) for bl in range(bb) for hk in range(N_KV_HEADS)]
    biases, sinks = [], []
    for hk in range(N_KV_HEADS):
        biases.append(jnp.concatenate(
            [bias_ref[hk * GQA_GROUP + g] for g in range(GQA_GROUP)], axis=0))
        sink = jnp.zeros((rows, 1), F32)
        for g in range(GQA_GROUP):
            sink = jnp.where(grow == g, sinks_ref[layer, hk * GQA_GROUP + g], sink)
        sinks.append(sink)

    scores, values = {}, {}
    for bl in range(bb):
        shift = (keep - off0 - bl * tn) % LANES
        nkb = [pltpu.roll(w, shift, axis=1) for w in wkt]
        nvb = [pltpu.roll(w, shift, axis=1) for w in wvt]
        for hk in range(N_KV_HEADS):
            for r in range(n_c):
                kt = kc_ref[r, bl, hk]
                vt = vc_ref[r, bl, hk]
                nkt = jnp.where(tail, nkb[r][hk * HEAD_DIM:(hk + 1) * HEAD_DIM, :], 0.0)
                nvt = jnp.where(tail, nvb[r][hk * HEAD_DIM:(hk + 1) * HEAD_DIM, :], 0.0)
                if roll:
                    nk_ref[r, bl, hk] = jnp.where(tail, nkt, pltpu.roll(kt, keep, axis=1))
                    nv_ref[r, bl, hk] = jnp.where(tail, nvt, pltpu.roll(vt, keep, axis=1))
            keys = jnp.concatenate([kt, nkt], axis=1).astype(BF16)
            values[(bl, hk)] = jnp.concatenate([vt, nvt], axis=1).astype(BF16)
            qb = q_ref[bl]
            qu = jnp.concatenate(
                [qb[:, (hk * GQA_GROUP + g) * HEAD_DIM:(hk * GQA_GROUP + g + 1) * HEAD_DIM]
                 for g in range(GQA_GROUP)], axis=0).astype(BF16)
            scores[(bl, hk)] = jnp.dot(qu, keys, preferred_element_type=F32)
    probs, dens = {}, {}
    for (bl, hk) in units:
        s = scores[(bl, hk)] + biases[hk]
        m = jnp.maximum(jnp.max(s, axis=-1, keepdims=True), sinks[hk])
        p = jnp.exp(s - m)
        dens[(bl, hk)] = jnp.sum(p, axis=-1, keepdims=True) + jnp.exp(sinks[hk] - m)
        probs[(bl, hk)] = p.astype(BF16)
    for (bl, hk) in units:
        o = lax.dot_general(probs[(bl, hk)], values[(bl, hk)], (((1,), (1,)), ((), ())),
                            preferred_element_type=F32)
        o = o / dens[(bl, hk)]
        gw = GQA_GROUP * HEAD_DIM
        o_ref[bl, :, hk * gw:(hk + 1) * gw] = jnp.concatenate(
            [o[g * tn:(g + 1) * tn, :] for g in range(GQA_GROUP)], axis=1)


def _sattn_call(q4, new_rows, ck, cv, sinks, l, roll, nb, tn):
    bb = SATTN_BATCH
    n_c = len(new_rows)
    assert n_c == (l + 1 if roll else 1)
    unit = (bb, N_KV_HEADS, HEAD_DIM, WINDOW)
    cache_block = pl.BlockSpec((n_c,) + unit, lambda i, s: (0 if roll else l, i, 0, 0, 0))
    in_specs = [
        pl.BlockSpec((bb, tn, ATTN_WIDTH), lambda i, s: (i, 0, 0)),
        cache_block,
        cache_block,
        _const_spec((N_HEADS, tn, 2 * WINDOW)),
    ] + [_const_spec((nb * tn, KV_WIDTH))] * (2 * n_c)
    args = [sinks, q4, ck, cv, jnp.asarray(_sample_bias(tn))] + [a for kv in new_rows for a in kv]
    out_specs = [pl.BlockSpec((bb, tn, ATTN_WIDTH), lambda i, s: (i, 0, 0))]
    out_shape = [jax.ShapeDtypeStruct((nb, tn, ATTN_WIDTH), F32)]
    if roll:
        out_specs += [pl.BlockSpec((n_c,) + unit, lambda i, s: (0, i, 0, 0, 0))] * 2
        out_shape += [jax.ShapeDtypeStruct((n_c, nb) + unit[1:], F32)] * 2
    grid_spec = pltpu.PrefetchScalarGridSpec(
        num_scalar_prefetch=1,
        grid=(nb // bb,),
        in_specs=in_specs,
        out_specs=out_specs,
        scratch_shapes=[
            pltpu.VMEM((n_c, KV_WIDTH, nb * tn), F32),
            pltpu.VMEM((n_c, KV_WIDTH, nb * tn), F32),
        ],
    )
    return pl.pallas_call(
        functools.partial(_sattn_kernel, layer=l, nb=nb, tn=tn, bb=bb, n_c=n_c, roll=roll),
        grid_spec=grid_spec,
        out_shape=out_shape,
        compiler_params=pltpu.CompilerParams(
            dimension_semantics=("arbitrary",),
            vmem_limit_bytes=VMEM_LIMIT),
        name="sample_attn",
    )(*args)


def kernel(x_prompt, x_sample, cache_k, cache_v, state_pool, c_prompt, c_sample,
           w_ada, b_ada, norm1_g, w_in, w_pool, pool_scale, attn_sinks, w_a, w_b,
           w_out, norm2_g, w_up, w_down, final_g):
    bp, tp, _ = x_prompt.shape
    nb, tn, _ = x_sample.shape
    assert tp % MIX_ROWS == 0 and tp % FFN_ROWS == 0 and tn <= 8 and nb == LANES
    assert bp % 8 == 0

    mod_p, mod_s, *mix_w = _mod_call(c_prompt, c_sample, w_ada, b_ada,
                                     [(w, 0) for w in (w_in, w_b, w_out)])
    wpa = _poolw_call(w_pool, pool_scale, w_a)
    bias = jnp.asarray(_prompt_bias())
    fg = final_g.reshape(1, D_MODEL)
    g1, g2 = norm1_g, norm2_g

    xs = x_sample
    prefix_t = state_pool.transpose(0, 2, 1, 3)
    ck = cache_k.transpose(0, 1, 3, 4, 2)
    cv = cache_v.transpose(0, 1, 3, 4, 2)

    xp = x_prompt
    kp, vp, pp, npool_s, new_rows = [], [], [], None, []
    for l in range(DEPTH):
        last = l == DEPTH - 1
        win, wb, wout = mix_w
        xp, nk, nv, npool, wup, wdn = _pmix_call(xp, mod_p, g1, win, wpa, wb, wout,
                                                 attn_sinks, bias, l, [(w_up, l), (w_down, l)])
        kp.append(nk); vp.append(nv); pp.append(npool)

        q, kn, vn, ma, sgb, npool_s = _sproj_call(xs, mod_s, g1, win, wpa, prefix_t, npool_s,
                                                  l, nb, tn)
        new_rows.append((kn, vn))
        o, *rolled = _sattn_call(q, new_rows if last else new_rows[-1:], ck, cv, attn_sinks,
                                 l, last, nb, tn)

        nxt = [] if last else [(w, l + 1) for w in (w_in, w_b, w_out)]
        xp, xs, *mix_w = _ffn_call(xp, xs, o, ma, sgb, mod_p, mod_s, g2, wup, wdn, wb, wout, fg,
                                   l, last, nxt)

    kv_shape_p = (DEPTH, bp, N_KV_HEADS, HEAD_DIM, WINDOW)
    return (xp,
            xs,
            jnp.stack(kp).reshape(kv_shape_p).transpose(0, 1, 4, 2, 3),
            jnp.stack(vp).reshape(kv_shape_p).transpose(0, 1, 4, 2, 3),
            jnp.stack(pp),
            rolled[0].transpose(0, 1, 4, 2, 3),
            rolled[1].transpose(0, 1, 4, 2, 3),
            npool_s.transpose(0, 2, 1, 3))
```
